```python
import jax, jax.numpy as jnp
from jax import lax
import numpy as np

D_MODEL = 1024
BATCH = 8
SEQ = 8192
DEPTH = 2

CHUNK = 64
N_MIXERS = 2
N_FOX = (DEPTH + 1) // 2
N_SGU = DEPTH // 2
FOX_HEADS = 16
FOX_HEAD_DIM = D_MODEL // FOX_HEADS
Q_BLOCK = 128
SGU_WIDTH = 2 * D_MODEL
SGU_GROUPS = 8
SGU_GROUP_DIM = SGU_WIDTH // SGU_GROUPS
SGU_BLOCK = 128
D_FF = 2816
CONV_WIDTH = 3
EPS = 1e-6

kernel_name = "fox_gmlp_convffn_adaln_hybrid"


def rmsnorm(x, g):
    xf = x.astype(jnp.float32)
    y = xf * lax.rsqrt(jnp.mean(xf * xf, axis=-1, keepdims=True) + EPS)
    return (y * g.astype(jnp.float32)).astype(x.dtype)


def layernorm(x, g, b):
    xf = x.astype(jnp.float32)
    mu = jnp.mean(xf, axis=-1, keepdims=True)
    var = jnp.mean(jnp.square(xf - mu), axis=-1, keepdims=True)
    y = (xf - mu) * lax.rsqrt(var + EPS)
    return (y * g.astype(jnp.float32) + b.astype(jnp.float32)).astype(x.dtype)


def modulate(h, shift, scale):
    return h * (1 + scale[:, None, :]) + shift[:, None, :]


def forgetting_attention(h, w_in, b_f, q_gain, k_gain, w_out):
    B, S, D = h.shape
    H, Dh = FOX_HEADS, FOX_HEAD_DIM
    proj = h @ w_in
    q, k, v, o, fl = jnp.split(proj, [D, 2 * D, 3 * D, 4 * D], axis=-1)
    q = rmsnorm(q.reshape(B, S, H, Dh), q_gain)
    k = rmsnorm(k.reshape(B, S, H, Dh), k_gain)
    v = v.reshape(B, S, H, Dh)
    logf = jax.nn.log_sigmoid((fl + b_f).astype(jnp.float32))
    F = jnp.cumsum(logf, axis=1).transpose(0, 2, 1)
    scale = Dh ** -0.5
    outs = []
    for qb in range(S // Q_BLOCK):
        q0, q1 = qb * Q_BLOCK, (qb + 1) * Q_BLOCK
        logits = jnp.einsum('bqhd,bkhd->bhqk', q[:, q0:q1], k[:, :q1]).astype(jnp.float32) * scale
        decay = F[:, :, q0:q1, None] - F[:, :, None, :q1]
        qpos = jnp.arange(q0, q1)[:, None]
        kpos = jnp.arange(q1)[None, :]
        logits = jnp.where(kpos <= qpos, logits + decay, -jnp.inf)
        p = jax.nn.softmax(logits, axis=-1).astype(v.dtype)
        outs.append(jnp.einsum('bhqk,bkhd->bqhd', p, v[:, :q1]))
    att = jnp.concatenate(outs, axis=1).reshape(B, S, D)
    return (att * jax.nn.sigmoid(o)) @ w_out


def spatial_gating_mlp(h, w_in, b_in, v_gain, v_bias, w_s, b_s, w_out):
    B, S, _ = h.shape
    z = jax.nn.gelu(h @ w_in + b_in)
    u, v = jnp.split(z, 2, axis=-1)
    v = layernorm(v, v_gain, v_bias)
    n = S // SGU_BLOCK
    v = v.reshape(B, n, SGU_BLOCK, SGU_GROUPS, SGU_GROUP_DIM)
    t = jnp.arange(SGU_BLOCK)
    mask = (t[None, :] // CHUNK) <= (t[:, None] // CHUNK)
    ws = jnp.where(mask[None], w_s, 0)
    mixed = jnp.einsum('gts,bnsgc->bntgc', ws, v) + b_s.T[None, None, :, :, None]
    y = u * mixed.reshape(B, S, SGU_WIDTH)
    return y @ w_out


def conv_gated_ffn(h, w_up, conv_w, conv_b, w_down):
    S = h.shape[1]
    a = h @ w_up
    ap = jnp.pad(a, ((0, 0), (CONV_WIDTH - 1, 0), (0, 0)))
    acc = ap[:, 0:S] * conv_w[0]
    for j in range(1, CONV_WIDTH):
        acc = acc + ap[:, j:j + S] * conv_w[j]
    a = acc + conv_b
    g, val = jnp.split(a, 2, axis=-1)
    return (jax.nn.silu(g) * val) @ w_down


def _fwd_setup_inputs(seed: int = 0) -> dict:
    key = jax.random.key(seed)
    ks = jax.random.split(key, 24)
    D = D_MODEL
    nrm = jax.random.normal
    f32 = jnp.float32
    return {
        "x": nrm(ks[0], (BATCH, SEQ, D), f32),
        "c": nrm(ks[1], (BATCH, D), f32),
        "fox_w_in": nrm(ks[2], (N_FOX, D, 4 * D + FOX_HEADS), f32) * D ** -0.5,
        "fox_b_f": 3.0 + 0.5 * nrm(ks[3], (N_FOX, FOX_HEADS), f32),
        "fox_q_gain": 1.0 + 0.02 * nrm(ks[4], (N_FOX, FOX_HEAD_DIM), f32),
        "fox_k_gain": 1.0 + 0.02 * nrm(ks[5], (N_FOX, FOX_HEAD_DIM), f32),
        "fox_w_out": nrm(ks[6], (N_FOX, D, D), f32) * D ** -0.5,
        "sgu_w_in": nrm(ks[7], (N_SGU, D, 2 * SGU_WIDTH), f32) * D ** -0.5,
        "sgu_b_in": 0.02 * nrm(ks[8], (N_SGU, 2 * SGU_WIDTH), f32),
        "sgu_v_gain": 1.0 + 0.02 * nrm(ks[9], (N_SGU, SGU_WIDTH), f32),
        "sgu_v_bias": 0.02 * nrm(ks[10], (N_SGU, SGU_WIDTH), f32),
        "sgu_w_s": nrm(ks[11], (N_SGU, SGU_GROUPS, SGU_BLOCK, SGU_BLOCK), f32) * SGU_BLOCK ** -0.5,
        "sgu_b_s": 1.0 + 0.1 * nrm(ks[12], (N_SGU, SGU_GROUPS, SGU_BLOCK), f32),
        "sgu_w_out": nrm(ks[13], (N_SGU, SGU_WIDTH, D), f32) * SGU_WIDTH ** -0.5,
        "ffn_w_up": nrm(ks[14], (DEPTH, D, 2 * D_FF), f32) * D ** -0.5,
        "ffn_conv_w": nrm(ks[15], (DEPTH, CONV_WIDTH, 2 * D_FF), f32) * CONV_WIDTH ** -0.5,
        "ffn_conv_b": 0.02 * nrm(ks[16], (DEPTH, 2 * D_FF), f32),
        "ffn_w_down": nrm(ks[17], (DEPTH, D_FF, D), f32) * D_FF ** -0.5,
        "ada_w": nrm(ks[18], (DEPTH, D, 6 * D), f32) * (0.5 * D ** -0.5),
        "ada_b": 0.02 * nrm(ks[19], (DEPTH, 6 * D), f32),
        "norm1_g": 1.0 + 0.02 * nrm(ks[20], (DEPTH, D), f32),
        "norm2_g": 1.0 + 0.02 * nrm(ks[21], (DEPTH, D), f32),
        "final_g": 1.0 + 0.02 * nrm(ks[22], (D,), f32),
    }


def _fwd_reference(x, c, fox_w_in, fox_b_f, fox_q_gain, fox_k_gain, fox_w_out,
              sgu_w_in, sgu_b_in, sgu_v_gain, sgu_v_bias, sgu_w_s, sgu_b_s, sgu_w_out,
              ffn_w_up, ffn_conv_w, ffn_conv_b, ffn_w_down,
              ada_w, ada_b, norm1_g, norm2_g, final_g):
    c_act = jax.nn.silu(c)
    for i in range(DEPTH):
        mod = c_act @ ada_w[i] + ada_b[i]
        sh1, sc1, g1, sh2, sc2, g2 = jnp.split(mod, 6, axis=-1)
        h = modulate(rmsnorm(x, norm1_g[i]), sh1, sc1)
        j = i // N_MIXERS
        if i % N_MIXERS == 0:
            y = forgetting_attention(h, fox_w_in[j], fox_b_f[j], fox_q_gain[j],
                                     fox_k_gain[j], fox_w_out[j])
        else:
            y = spatial_gating_mlp(h, sgu_w_in[j], sgu_b_in[j], sgu_v_gain[j],
                                   sgu_v_bias[j], sgu_w_s[j], sgu_b_s[j], sgu_w_out[j])
        x = x + g1[:, None, :] * y
        h = modulate(rmsnorm(x, norm2_g[i]), sh2, sc2)
        x = x + g2[:, None, :] * conv_gated_ffn(h, ffn_w_up[i], ffn_conv_w[i],
                                                ffn_conv_b[i], ffn_w_down[i])
    return rmsnorm(x, final_g)


import jax as _jax
import jax.numpy as _jnp

TWIN_FORMAT = 'train_step'
FWD_PARAMS = ['x', 'c', 'fox_w_in', 'fox_b_f', 'fox_q_gain', 'fox_k_gain', 'fox_w_out', 'sgu_w_in', 'sgu_b_in', 'sgu_v_gain', 'sgu_v_bias', 'sgu_w_s', 'sgu_b_s', 'sgu_w_out', 'ffn_w_up', 'ffn_conv_w', 'ffn_conv_b', 'ffn_w_down', 'ada_w', 'ada_b', 'norm1_g', 'norm2_g', 'final_g']
TWIN_WEIGHTS = ['fox_w_in', 'fox_b_f', 'fox_q_gain', 'fox_k_gain', 'fox_w_out', 'sgu_w_in', 'sgu_b_in', 'sgu_v_gain', 'sgu_v_bias', 'sgu_w_s', 'sgu_b_s', 'sgu_w_out', 'ffn_w_up', 'ffn_conv_w', 'ffn_conv_b', 'ffn_w_down', 'ada_w', 'ada_b', 'norm1_g', 'norm2_g', 'final_g']
TWIN_DIFF_INPUT = 'x'
TWIN_INPUTS = ['x', 'c', 'fox_w_in', 'fox_b_f', 'fox_q_gain', 'fox_k_gain', 'fox_w_out', 'sgu_w_in', 'sgu_b_in', 'sgu_v_gain', 'sgu_v_bias', 'sgu_w_s', 'sgu_b_s', 'sgu_w_out', 'ffn_w_up', 'ffn_conv_w', 'ffn_conv_b', 'ffn_w_down', 'ada_w', 'ada_b', 'norm1_g', 'norm2_g', 'final_g', 'loss_target', 'm_fox_w_in', 'm_fox_b_f', 'm_fox_q_gain', 'm_fox_k_gain', 'm_fox_w_out', 'm_sgu_w_in', 'm_sgu_b_in', 'm_sgu_v_gain', 'm_sgu_v_bias', 'm_sgu_w_s', 'm_sgu_b_s', 'm_sgu_w_out', 'm_ffn_w_up', 'm_ffn_conv_w', 'm_ffn_conv_b', 'm_ffn_w_down', 'm_ada_w', 'm_ada_b', 'm_norm1_g', 'm_norm2_g', 'm_final_g', 'v_fox_w_in', 'v_fox_b_f', 'v_fox_q_gain', 'v_fox_k_gain', 'v_fox_w_out', 'v_sgu_w_in', 'v_sgu_b_in', 'v_sgu_v_gain', 'v_sgu_v_bias', 'v_sgu_w_s', 'v_sgu_b_s', 'v_sgu_w_out', 'v_ffn_w_up', 'v_ffn_conv_w', 'v_ffn_conv_b', 'v_ffn_w_down', 'v_ada_w', 'v_ada_b', 'v_norm1_g', 'v_norm2_g', 'v_final_g']
TWIN_OUTPUTS = ['loss', 'grad_x', 'grad_fox_w_in', 'grad_fox_b_f', 'grad_fox_q_gain', 'grad_fox_k_gain', 'grad_fox_w_out', 'grad_sgu_w_in', 'grad_sgu_b_in', 'grad_sgu_v_gain', 'grad_sgu_v_bias', 'grad_sgu_w_s', 'grad_sgu_b_s', 'grad_sgu_w_out', 'grad_ffn_w_up', 'grad_ffn_conv_w', 'grad_ffn_conv_b', 'grad_ffn_w_down', 'grad_ada_w', 'grad_ada_b', 'grad_norm1_g', 'grad_norm2_g', 'grad_final_g', 'delta_fox_w_in', 'delta_fox_b_f', 'delta_fox_q_gain', 'delta_fox_k_gain', 'delta_fox_w_out', 'delta_sgu_w_in', 'delta_sgu_b_in', 'delta_sgu_v_gain', 'delta_sgu_v_bias', 'delta_sgu_w_s', 'delta_sgu_b_s', 'delta_sgu_w_out', 'delta_ffn_w_up', 'delta_ffn_conv_w', 'delta_ffn_conv_b', 'delta_ffn_w_down', 'delta_ada_w', 'delta_ada_b', 'delta_norm1_g', 'delta_norm2_g', 'delta_final_g', 'new_m_fox_w_in', 'new_m_fox_b_f', 'new_m_fox_q_gain', 'new_m_fox_k_gain', 'new_m_fox_w_out', 'new_m_sgu_w_in', 'new_m_sgu_b_in', 'new_m_sgu_v_gain', 'new_m_sgu_v_bias', 'new_m_sgu_w_s', 'new_m_sgu_b_s', 'new_m_sgu_w_out', 'new_m_ffn_w_up', 'new_m_ffn_conv_w', 'new_m_ffn_conv_b', 'new_m_ffn_w_down', 'new_m_ada_w', 'new_m_ada_b', 'new_m_norm1_g', 'new_m_norm2_g', 'new_m_final_g', 'new_v_fox_w_in', 'new_v_fox_b_f', 'new_v_fox_q_gain', 'new_v_fox_k_gain', 'new_v_fox_w_out', 'new_v_sgu_w_in', 'new_v_sgu_b_in', 'new_v_sgu_v_gain', 'new_v_sgu_v_bias', 'new_v_sgu_w_s', 'new_v_sgu_b_s', 'new_v_sgu_w_out', 'new_v_ffn_w_up', 'new_v_ffn_conv_w', 'new_v_ffn_conv_b', 'new_v_ffn_w_down', 'new_v_ada_w', 'new_v_ada_b', 'new_v_norm1_g', 'new_v_norm2_g', 'new_v_final_g']
TWIN_LEAF_KINDS = {'loss': 'loss', 'grad_x': 'grad_x', 'grad_fox_w_in': 'grad_w', 'grad_fox_b_f': 'grad_w', 'grad_fox_q_gain': 'grad_w', 'grad_fox_k_gain': 'grad_w', 'grad_fox_w_out': 'grad_w', 'grad_sgu_w_in': 'grad_w', 'grad_sgu_b_in': 'grad_w', 'grad_sgu_v_gain': 'grad_w', 'grad_sgu_v_bias': 'grad_w', 'grad_sgu_w_s': 'grad_w', 'grad_sgu_b_s': 'grad_w', 'grad_sgu_w_out': 'grad_w', 'grad_ffn_w_up': 'grad_w', 'grad_ffn_conv_w': 'grad_w', 'grad_ffn_conv_b': 'grad_w', 'grad_ffn_w_down': 'grad_w', 'grad_ada_w': 'grad_w', 'grad_ada_b': 'grad_w', 'grad_norm1_g': 'grad_w', 'grad_norm2_g': 'grad_w', 'grad_final_g': 'grad_w', 'delta_fox_w_in': 'delta_w', 'delta_fox_b_f': 'delta_w', 'delta_fox_q_gain': 'delta_w', 'delta_fox_k_gain': 'delta_w', 'delta_fox_w_out': 'delta_w', 'delta_sgu_w_in': 'delta_w', 'delta_sgu_b_in': 'delta_w', 'delta_sgu_v_gain': 'delta_w', 'delta_sgu_v_bias': 'delta_w', 'delta_sgu_w_s': 'delta_w', 'delta_sgu_b_s': 'delta_w', 'delta_sgu_w_out': 'delta_w', 'delta_ffn_w_up': 'delta_w', 'delta_ffn_conv_w': 'delta_w', 'delta_ffn_conv_b': 'delta_w', 'delta_ffn_w_down': 'delta_w', 'delta_ada_w': 'delta_w', 'delta_ada_b': 'delta_w', 'delta_norm1_g': 'delta_w', 'delta_norm2_g': 'delta_w', 'delta_final_g': 'delta_w', 'new_m_fox_w_in': 'new_m', 'new_m_fox_b_f': 'new_m', 'new_m_fox_q_gain': 'new_m', 'new_m_fox_k_gain': 'new_m', 'new_m_fox_w_out': 'new_m', 'new_m_sgu_w_in': 'new_m', 'new_m_sgu_b_in': 'new_m', 'new_m_sgu_v_gain': 'new_m', 'new_m_sgu_v_bias': 'new_m', 'new_m_sgu_w_s': 'new_m', 'new_m_sgu_b_s': 'new_m', 'new_m_sgu_w_out': 'new_m', 'new_m_ffn_w_up': 'new_m', 'new_m_ffn_conv_w': 'new_m', 'new_m_ffn_conv_b': 'new_m', 'new_m_ffn_w_down': 'new_m', 'new_m_ada_w': 'new_m', 'new_m_ada_b': 'new_m', 'new_m_norm1_g': 'new_m', 'new_m_norm2_g': 'new_m', 'new_m_final_g': 'new_m', 'new_v_fox_w_in': 'new_v', 'new_v_fox_b_f': 'new_v', 'new_v_fox_q_gain': 'new_v', 'new_v_fox_k_gain': 'new_v', 'new_v_fox_w_out': 'new_v', 'new_v_sgu_w_in': 'new_v', 'new_v_sgu_b_in': 'new_v', 'new_v_sgu_v_gain': 'new_v', 'new_v_sgu_v_bias': 'new_v', 'new_v_sgu_w_s': 'new_v', 'new_v_sgu_b_s': 'new_v', 'new_v_sgu_w_out': 'new_v', 'new_v_ffn_w_up': 'new_v', 'new_v_ffn_conv_w': 'new_v', 'new_v_ffn_conv_b': 'new_v', 'new_v_ffn_w_down': 'new_v', 'new_v_ada_w': 'new_v', 'new_v_ada_b': 'new_v', 'new_v_norm1_g': 'new_v', 'new_v_norm2_g': 'new_v', 'new_v_final_g': 'new_v'}


def _forward(args):
    return _fwd_reference(*[args[k] for k in FWD_PARAMS])


def _output_shape():
    def fwd():
        inp = _fwd_setup_inputs(0)
        return _fwd_reference(*[inp[k] for k in FWD_PARAMS])
    out = _jax.eval_shape(fwd)
    return out.shape, out.dtype

N_MICROBATCH = 1
ADAM_LR = 0.001
ADAM_B1 = 0.9
ADAM_B2 = 0.999
ADAM_EPS = 1e-08
ADAM_WD = 0.01
ADAM_STEP = 10
PER_EXAMPLE_BATCH_AXIS = {'x': 0, 'c': 0, 'loss_target': 0}
SHARED_INPUTS = []
_WEIGHT_DTYPES = {'fox_w_in': _jnp.float32, 'fox_b_f': _jnp.float32, 'fox_q_gain': _jnp.float32, 'fox_k_gain': _jnp.float32, 'fox_w_out': _jnp.float32, 'sgu_w_in': _jnp.float32, 'sgu_b_in': _jnp.float32, 'sgu_v_gain': _jnp.float32, 'sgu_v_bias': _jnp.float32, 'sgu_w_s': _jnp.float32, 'sgu_b_s': _jnp.float32, 'sgu_w_out': _jnp.float32, 'ffn_w_up': _jnp.float32, 'ffn_conv_w': _jnp.float32, 'ffn_conv_b': _jnp.float32, 'ffn_w_down': _jnp.float32, 'ada_w': _jnp.float32, 'ada_b': _jnp.float32, 'norm1_g': _jnp.float32, 'norm2_g': _jnp.float32, 'final_g': _jnp.float32}
MOMENT_SCALE = {'fox_w_in': 1.339896e-02, 'fox_b_f': 6.204274e-02, 'fox_q_gain': 6.892123e-02, 'fox_k_gain': 6.742846e-02, 'fox_w_out': 1.954181e-02, 'sgu_w_in': 4.386166e-02, 'sgu_b_in': 4.035239e-02, 'sgu_v_gain': 3.417907e-02, 'sgu_v_bias': 3.213187e-02, 'sgu_w_s': 4.469010e-02, 'sgu_b_s': 5.341143e-02, 'sgu_w_out': 6.914456e-02, 'ffn_w_up': 3.275166e-02, 'ffn_conv_w': 3.269476e-02, 'ffn_conv_b': 3.008176e-02, 'ffn_w_down': 5.334902e-02, 'ada_w': 7.416092e-02, 'ada_b': 1.310213e-01, 'norm1_g': 6.139871e-02, 'norm2_g': 7.390838e-02, 'final_g': 6.408013e+01}


def _to_microbatches(a, axis):
    t = _jnp.moveaxis(a, axis, 0)
    t = t.reshape((N_MICROBATCH, t.shape[0] // N_MICROBATCH) + t.shape[1:])
    return _jnp.moveaxis(t, 1, axis + 1)


def setup_inputs(seed: int = 0) -> dict:
    inp = _fwd_setup_inputs(seed)
    key = _jax.random.fold_in(_jax.random.key(seed), 7919)
    shape, _ = _output_shape()
    out = dict(inp)
    out["loss_target"] = _jax.random.normal(_jax.random.fold_in(key, 0), shape, _jnp.float32)
    for i, name in enumerate(TWIN_WEIGHTS):
        w = inp[name].astype(_jnp.float32)
        if MOMENT_SCALE is None:
            s = _jnp.sqrt(_jnp.mean(_jnp.square(w)) + 1e-30)
        else:
            s = MOMENT_SCALE[name]
        km, kv = _jax.random.split(_jax.random.fold_in(key, i + 1))
        out[name] = w
        out["m_" + name] = s * _jax.random.normal(km, w.shape, _jnp.float32)
        out["v_" + name] = (s * s) * _jax.random.uniform(kv, w.shape, _jnp.float32, 0.5, 1.5)
    if N_MICROBATCH > 1:
        for name, axis in PER_EXAMPLE_BATCH_AXIS.items():
            out[name] = _to_microbatches(out[name], axis)
    return {'x': out['x'], 'c': out['c'], 'fox_w_in': out['fox_w_in'], 'fox_b_f': out['fox_b_f'], 'fox_q_gain': out['fox_q_gain'], 'fox_k_gain': out['fox_k_gain'], 'fox_w_out': out['fox_w_out'], 'sgu_w_in': out['sgu_w_in'], 'sgu_b_in': out['sgu_b_in'], 'sgu_v_gain': out['sgu_v_gain'], 'sgu_v_bias': out['sgu_v_bias'], 'sgu_w_s': out['sgu_w_s'], 'sgu_b_s': out['sgu_b_s'], 'sgu_w_out': out['sgu_w_out'], 'ffn_w_up': out['ffn_w_up'], 'ffn_conv_w': out['ffn_conv_w'], 'ffn_conv_b': out['ffn_conv_b'], 'ffn_w_down': out['ffn_w_down'], 'ada_w': out['ada_w'], 'ada_b': out['ada_b'], 'norm1_g': out['norm1_g'], 'norm2_g': out['norm2_g'], 'final_g': out['final_g'], 'loss_target': out['loss_target'], 'm_fox_w_in': out['m_fox_w_in'], 'm_fox_b_f': out['m_fox_b_f'], 'm_fox_q_gain': out['m_fox_q_gain'], 'm_fox_k_gain': out['m_fox_k_gain'], 'm_fox_w_out': out['m_fox_w_out'], 'm_sgu_w_in': out['m_sgu_w_in'], 'm_sgu_b_in': out['m_sgu_b_in'], 'm_sgu_v_gain': out['m_sgu_v_gain'], 'm_sgu_v_bias': out['m_sgu_v_bias'], 'm_sgu_w_s': out['m_sgu_w_s'], 'm_sgu_b_s': out['m_sgu_b_s'], 'm_sgu_w_out': out['m_sgu_w_out'], 'm_ffn_w_up': out['m_ffn_w_up'], 'm_ffn_conv_w': out['m_ffn_conv_w'], 'm_ffn_conv_b': out['m_ffn_conv_b'], 'm_ffn_w_down': out['m_ffn_w_down'], 'm_ada_w': out['m_ada_w'], 'm_ada_b': out['m_ada_b'], 'm_norm1_g': out['m_norm1_g'], 'm_norm2_g': out['m_norm2_g'], 'm_final_g': out['m_final_g'], 'v_fox_w_in': out['v_fox_w_in'], 'v_fox_b_f': out['v_fox_b_f'], 'v_fox_q_gain': out['v_fox_q_gain'], 'v_fox_k_gain': out['v_fox_k_gain'], 'v_fox_w_out': out['v_fox_w_out'], 'v_sgu_w_in': out['v_sgu_w_in'], 'v_sgu_b_in': out['v_sgu_b_in'], 'v_sgu_v_gain': out['v_sgu_v_gain'], 'v_sgu_v_bias': out['v_sgu_v_bias'], 'v_sgu_w_s': out['v_sgu_w_s'], 'v_sgu_b_s': out['v_sgu_b_s'], 'v_sgu_w_out': out['v_sgu_w_out'], 'v_ffn_w_up': out['v_ffn_w_up'], 'v_ffn_conv_w': out['v_ffn_conv_w'], 'v_ffn_conv_b': out['v_ffn_conv_b'], 'v_ffn_w_down': out['v_ffn_w_down'], 'v_ada_w': out['v_ada_w'], 'v_ada_b': out['v_ada_b'], 'v_norm1_g': out['v_norm1_g'], 'v_norm2_g': out['v_norm2_g'], 'v_final_g': out['v_final_g']}


def _loss(weights, diff, rest, loss_target):
    with _jax.named_scope("forward"):
        args = {**rest, TWIN_DIFF_INPUT: diff, **{k: w.astype(_WEIGHT_DTYPES[k]) for k, w in weights.items()}}
        y = _forward(args)
    with _jax.named_scope("loss_head"):
        err = _jnp.square(y.astype(_jnp.float32) - loss_target)
        return 0.5 * _jnp.sum(_jnp.mean(err, axis=-1)) if err.ndim else 0.5 * err


def _adamw(w, g, m, v):
    m = ADAM_B1 * m + (1.0 - ADAM_B1) * g
    v = ADAM_B2 * v + (1.0 - ADAM_B2) * _jnp.square(g)
    m_hat = m / (1.0 - ADAM_B1 ** ADAM_STEP)
    v_hat = v / (1.0 - ADAM_B2 ** ADAM_STEP)
    delta = -ADAM_LR * (m_hat / (_jnp.sqrt(v_hat) + ADAM_EPS) + ADAM_WD * w)
    return delta, m, v


def reference(x, c, fox_w_in, fox_b_f, fox_q_gain, fox_k_gain, fox_w_out, sgu_w_in, sgu_b_in, sgu_v_gain, sgu_v_bias, sgu_w_s, sgu_b_s, sgu_w_out, ffn_w_up, ffn_conv_w, ffn_conv_b, ffn_w_down, ada_w, ada_b, norm1_g, norm2_g, final_g, loss_target, m_fox_w_in, m_fox_b_f, m_fox_q_gain, m_fox_k_gain, m_fox_w_out, m_sgu_w_in, m_sgu_b_in, m_sgu_v_gain, m_sgu_v_bias, m_sgu_w_s, m_sgu_b_s, m_sgu_w_out, m_ffn_w_up, m_ffn_conv_w, m_ffn_conv_b, m_ffn_w_down, m_ada_w, m_ada_b, m_norm1_g, m_norm2_g, m_final_g, v_fox_w_in, v_fox_b_f, v_fox_q_gain, v_fox_k_gain, v_fox_w_out, v_sgu_w_in, v_sgu_b_in, v_sgu_v_gain, v_sgu_v_bias, v_sgu_w_s, v_sgu_b_s, v_sgu_w_out, v_ffn_w_up, v_ffn_conv_w, v_ffn_conv_b, v_ffn_w_down, v_ada_w, v_ada_b, v_norm1_g, v_norm2_g, v_final_g):
    given = dict(x=x, c=c, fox_w_in=fox_w_in, fox_b_f=fox_b_f, fox_q_gain=fox_q_gain, fox_k_gain=fox_k_gain, fox_w_out=fox_w_out, sgu_w_in=sgu_w_in, sgu_b_in=sgu_b_in, sgu_v_gain=sgu_v_gain, sgu_v_bias=sgu_v_bias, sgu_w_s=sgu_w_s, sgu_b_s=sgu_b_s, sgu_w_out=sgu_w_out, ffn_w_up=ffn_w_up, ffn_conv_w=ffn_conv_w, ffn_conv_b=ffn_conv_b, ffn_w_down=ffn_w_down, ada_w=ada_w, ada_b=ada_b, norm1_g=norm1_g, norm2_g=norm2_g, final_g=final_g, loss_target=loss_target, m_fox_w_in=m_fox_w_in, m_fox_b_f=m_fox_b_f, m_fox_q_gain=m_fox_q_gain, m_fox_k_gain=m_fox_k_gain, m_fox_w_out=m_fox_w_out, m_sgu_w_in=m_sgu_w_in, m_sgu_b_in=m_sgu_b_in, m_sgu_v_gain=m_sgu_v_gain, m_sgu_v_bias=m_sgu_v_bias, m_sgu_w_s=m_sgu_w_s, m_sgu_b_s=m_sgu_b_s, m_sgu_w_out=m_sgu_w_out, m_ffn_w_up=m_ffn_w_up, m_ffn_conv_w=m_ffn_conv_w, m_ffn_conv_b=m_ffn_conv_b, m_ffn_w_down=m_ffn_w_down, m_ada_w=m_ada_w, m_ada_b=m_ada_b, m_norm1_g=m_norm1_g, m_norm2_g=m_norm2_g, m_final_g=m_final_g, v_fox_w_in=v_fox_w_in, v_fox_b_f=v_fox_b_f, v_fox_q_gain=v_fox_q_gain, v_fox_k_gain=v_fox_k_gain, v_fox_w_out=v_fox_w_out, v_sgu_w_in=v_sgu_w_in, v_sgu_b_in=v_sgu_b_in, v_sgu_v_gain=v_sgu_v_gain, v_sgu_v_bias=v_sgu_v_bias, v_sgu_w_s=v_sgu_w_s, v_sgu_b_s=v_sgu_b_s, v_sgu_w_out=v_sgu_w_out, v_ffn_w_up=v_ffn_w_up, v_ffn_conv_w=v_ffn_conv_w, v_ffn_conv_b=v_ffn_conv_b, v_ffn_w_down=v_ffn_w_down, v_ada_w=v_ada_w, v_ada_b=v_ada_b, v_norm1_g=v_norm1_g, v_norm2_g=v_norm2_g, v_final_g=v_final_g)
    weights = {n: given[n] for n in TWIN_WEIGHTS}
    shared = {n: given[n] for n in SHARED_INPUTS}
    per_example = {n: given[n] for n in ['x', 'c']}
    grad_fn = _jax.value_and_grad(_loss, argnums=(0, 1))

    def one_microbatch(ex, loss_target):
        ex = dict(ex)
        diff = ex.pop(TWIN_DIFF_INPUT)
        return grad_fn(weights, diff, {**shared, **ex}, loss_target)

    if N_MICROBATCH == 1:
        loss, (grad_w, grad_x) = one_microbatch(per_example, given["loss_target"])
    else:
        def body(carry, xs):
            loss_sum, grad_sum = carry
            l_k, (gw_k, gx_k) = one_microbatch(xs[0], xs[1])
            with _jax.named_scope("update"):
                return (loss_sum + l_k, _jax.tree.map(_jnp.add, grad_sum, gw_k)), gx_k

        init = (_jnp.zeros((), _jnp.float32), _jax.tree.map(_jnp.zeros_like, weights))
        (loss, grad_w), grad_x = _jax.lax.scan(body, init, (per_example, given["loss_target"]))
    with _jax.named_scope("update"):
        delta_w, new_m, new_v = {}, {}, {}
        for n in TWIN_WEIGHTS:
            delta_w[n], new_m[n], new_v[n] = _adamw(weights[n], grad_w[n], given["m_" + n], given["v_" + n])
    return (loss, grad_x, *[grad_w[n] for n in TWIN_WEIGHTS], *[delta_w[n] for n in TWIN_WEIGHTS],
            *[new_m[n] for n in TWIN_WEIGHTS], *[new_v[n] for n in TWIN_WEIGHTS])
```

```python
import functools
import math

import numpy as np
import jax
import jax.numpy as jnp
from jax import lax
from jax.experimental import pallas as pl
from jax.experimental.pallas import tpu as pltpu

F32 = jnp.float32
BF16 = jnp.bfloat16
MESH = pl.DeviceIdType.MESH

D = 1024
H = 16
DH = 64
NP = H // 2
LANES = 128
DFF = 2816
SGW = 2048
SGG = 8
SGC = 256
SGB = 128
CHUNK = 64
EPS = 1e-6
FOX_N = 4 * D + H
FOX_NP = 4224
GT = 256
NGT = DFF // GT
SCALE = DH ** -0.5

ADAM_LR = 0.001
ADAM_B1 = 0.9
ADAM_B2 = 0.999
ADAM_EPS = 1e-08
ADAM_WD = 0.01
ADAM_STEP = 10

V7X_VMEM_LIMIT = 56 * 1024 * 1024

L_F = 64
L_NF = 67
L_LSE = 70


def _cparams(sem=None):
    return pltpu.CompilerParams(dimension_semantics=sem, vmem_limit_bytes=V7X_VMEM_LIMIT)


def _split3(x):
    hi = x.astype(BF16)
    r = x - hi.astype(F32)
    mid = r.astype(BF16)
    lo = (r - mid.astype(F32)).astype(BF16)
    return hi, mid, lo


def _dot(a, b, dims=(((1,), (0,)), ((), ()))):
    return lax.dot_general(a, b, dims, preferred_element_type=F32)


def _dot_nt(a, b):
    return _dot(a, b, (((1,), (1,)), ((), ())))


def _dot_tn(a, b):
    return _dot(a, b, (((0,), (0,)), ((), ())))


def _exact_dot(m_bf16, x_f32):
    hi, mid, lo = _split3(x_f32)
    return _dot(m_bf16, hi) + _dot(m_bf16, mid) + _dot(m_bf16, lo)


def _exact_dot_r(x_f32, m_bf16):
    hi, mid, lo = _split3(x_f32)
    return _dot(hi, m_bf16) + _dot(mid, m_bf16) + _dot(lo, m_bf16)


def _head_block_ones():
    r = lax.broadcasted_iota(jnp.int32, (LANES, LANES), 0) // DH
    c = lax.broadcasted_iota(jnp.int32, (LANES, LANES), 1) // DH
    return (r == c).astype(BF16)


def _sigmoid(x):
    return 1.0 / (1.0 + jnp.exp(-x))


def _gelu(x):
    c = math.sqrt(2.0 / math.pi)
    return 0.5 * x * (1.0 + jnp.tanh(c * (x + 0.044715 * (x * x * x))))


def _gelu_grad(x):
    c = math.sqrt(2.0 / math.pi)
    t = jnp.tanh(c * (x + 0.044715 * (x * x * x)))
    return 0.5 * (1.0 + t) + 0.5 * x * (1.0 - t * t) * c * (1.0 + 3 * 0.044715 * (x * x))


def _rstd_rows(x):
    return lax.rsqrt(jnp.mean(x * x, axis=-1, keepdims=True) + EPS)


def _norm_mod_matmul(x, ng, sc, sh, w, bias, out_dtype, ts, tn, name):
    s, d = x.shape
    n = w.shape[1]

    def body(x_ref, ng_ref, sc_ref, sh_ref, w_ref, b_ref, o_ref, h_ref, hs):
        @pl.when(pl.program_id(1) == 0)
        def _():
            xv = x_ref[...]
            h = xv * _rstd_rows(xv) * ng_ref[...] * (1.0 + sc_ref[...]) + sh_ref[...]
            hs[...] = h.astype(BF16)
            h_ref[...] = h.astype(BF16)
        o_ref[...] = (_dot(hs[...], w_ref[...]) + b_ref[...]).astype(out_dtype)

    vec = pl.BlockSpec((1, d), lambda i, j: (0, 0))
    return pl.pallas_call(
        body, name=name, grid=(s // ts, n // tn),
        in_specs=[pl.BlockSpec((ts, d), lambda i, j: (i, 0)), vec, vec, vec,
                  pl.BlockSpec((d, tn), lambda i, j: (0, j)),
                  pl.BlockSpec((1, tn), lambda i, j: (0, j))],
        out_specs=[pl.BlockSpec((ts, tn), lambda i, j: (i, j)),
                   pl.BlockSpec((ts, d), lambda i, j: (i, 0))],
        out_shape=[jax.ShapeDtypeStruct((s, n), out_dtype), jax.ShapeDtypeStruct((s, d), BF16)],
        scratch_shapes=[pltpu.VMEM((ts, d), BF16)],
        compiler_params=_cparams(("arbitrary", "arbitrary")),
    )(x, ng, sc, sh, w, bias)


def _matmul(a, b, ta, tb, tm, tn, tk, out_dtype, name):
    m, k = (a.shape[1], a.shape[0]) if ta else a.shape
    n = b.shape[0] if tb else b.shape[1]
    nk = k // tk
    dims = (((0,) if ta else (1,), (1,) if tb else (0,)), ((), ()))

    def body(a_ref, b_ref, o_ref, acc):
        kk = pl.program_id(2)

        @pl.when(kk == 0)
        def _():
            acc[...] = jnp.zeros_like(acc)
        acc[...] += _dot(a_ref[...], b_ref[...], dims)

        @pl.when(kk == nk - 1)
        def _():
            o_ref[...] = acc[...].astype(out_dtype)

    a_spec = (pl.BlockSpec((tk, tm), lambda i, j, kk: (kk, i)) if ta
              else pl.BlockSpec((tm, tk), lambda i, j, kk: (i, kk)))
    b_spec = (pl.BlockSpec((tn, tk), lambda i, j, kk: (j, kk)) if tb
              else pl.BlockSpec((tk, tn), lambda i, j, kk: (kk, j)))
    return pl.pallas_call(
        body, name=name, grid=(m // tm, n // tn, nk),
        in_specs=[a_spec, b_spec],
        out_specs=pl.BlockSpec((tm, tn), lambda i, j, kk: (i, j)),
        out_shape=jax.ShapeDtypeStruct((m, n), out_dtype),
        scratch_shapes=[pltpu.VMEM((tm, tn), F32)],
        compiler_params=_cparams(("arbitrary", "arbitrary", "arbitrary")),
    )(a, b)


def _matmul_residual(a, w, xin, g, ts, name):
    s, k = a.shape
    d = w.shape[1]

    def body(a_ref, w_ref, x_ref, g_ref, o_ref, y_ref):
        y = _dot(a_ref[...], w_ref[...])
        o_ref[...] = x_ref[...] + g_ref[...] * y
        y_ref[...] = y.astype(BF16)

    return pl.pallas_call(
        body, name=name, grid=(s // ts,),
        in_specs=[pl.BlockSpec((ts, k), lambda i: (i, 0)),
                  pl.BlockSpec((k, d), lambda i: (0, 0)),
                  pl.BlockSpec((ts, d), lambda i: (i, 0)),
                  pl.BlockSpec((1, d), lambda i: (0, 0))],
        out_specs=[pl.BlockSpec((ts, d), lambda i: (i, 0)), pl.BlockSpec((ts, d), lambda i: (i, 0))],
        out_shape=[jax.ShapeDtypeStruct((s, d), F32), jax.ShapeDtypeStruct((s, d), BF16)],
        compiler_params=_cparams(("arbitrary",)),
    )(a, w, xin, g)


def _lane(shape):
    return lax.broadcasted_iota(jnp.int32, shape, 1)


def _pair_norm(x, gain2, bones):
    msq = _exact_dot_r(x * x, bones) * (1.0 / DH)
    r = lax.rsqrt(msq + EPS)
    xh = x * r
    return xh * gain2, xh, r


def _fox_post(proj, qg2, kg2, bf, ts, name):
    s = proj.shape[0]

    def body(p_ref, qg_ref, kg_ref, bf_ref, q_ref, k_ref, v_ref, carry):
        @pl.when(pl.program_id(0) == 0)
        def _():
            carry[...] = jnp.zeros_like(carry)
        lane = _lane((ts, LANES))
        bones = _head_block_ones()
        xf = p_ref[:, 4 * D:4 * D + LANES] + bf_ref[...]
        logf = jnp.minimum(xf, 0.0) - jnp.log(1.0 + jnp.exp(-jnp.abs(xf)))
        logf = jnp.where(lane < H, logf, 0.0)
        rr = lax.broadcasted_iota(jnp.int32, (ts, ts), 0)
        cc = lax.broadcasted_iota(jnp.int32, (ts, ts), 1)
        ltri = (cc <= rr).astype(BF16)
        fcum = _exact_dot(ltri, logf) + carry[0:1, :]
        carry[0:1, :] = fcum[ts - 1:ts, :]
        fhi, fmid, flo = _split3(fcum)
        fhi, fmid, flo = fhi.astype(F32), fmid.astype(F32), flo.astype(F32)
        one_q = ((lane >= L_NF) & (lane < L_NF + 3)).astype(F32)
        one_k = (((lane >= L_F) & (lane < L_F + 3)) | ((lane >= L_LSE) & (lane < L_LSE + 3))).astype(F32)
        one_v = ((lane >= L_F) & (lane < L_F + 3)).astype(F32)
        for p in range(NP):
            qn, _, _ = _pair_norm(p_ref[:, p * LANES:(p + 1) * LANES], qg_ref[...], bones)
            kn, _, _ = _pair_norm(p_ref[:, D + p * LANES:D + (p + 1) * LANES], kg_ref[...], bones)
            vv = p_ref[:, 2 * D + p * LANES:2 * D + (p + 1) * LANES]
            qn = qn * SCALE
            for e in range(2):
                h = 2 * p + e
                if e == 1:
                    qe, ke, ve = (pltpu.roll(t, DH, axis=1) for t in (qn, kn, vv))
                else:
                    qe, ke, ve = qn, kn, vv
                f0, f1, f2 = fhi[:, h:h + 1], fmid[:, h:h + 1], flo[:, h:h + 1]
                fq = jnp.where(lane == L_F, f0, jnp.where(lane == L_F + 1, f1, jnp.where(lane == L_F + 2, f2, one_q)))
                fk = jnp.where(lane == L_NF, -f0, jnp.where(lane == L_NF + 1, -f1, jnp.where(lane == L_NF + 2, -f2, one_k)))
                q_ref[h] = jnp.where(lane < DH, qe, fq).astype(BF16)
                k_ref[h] = jnp.where(lane < DH, ke, fk).astype(BF16)
                v_ref[h] = jnp.where(lane < DH, ve, one_v).astype(BF16)

    hs = pl.BlockSpec((H, ts, LANES), lambda i: (0, i, 0))
    vec = pl.BlockSpec((1, LANES), lambda i: (0, 0))
    shp = jax.ShapeDtypeStruct((H, s, LANES), BF16)
    return pl.pallas_call(
        body, name=name, grid=(s // ts,),
        in_specs=[pl.BlockSpec((ts, FOX_NP), lambda i: (i, 0)), vec, vec, vec],
        out_specs=[hs, hs, hs], out_shape=[shp, shp, shp],
        scratch_shapes=[pltpu.VMEM((8, LANES), F32)],
        compiler_params=_cparams(("arbitrary",)),
    )(proj, qg2, kg2, bf)


def _attn_fwd(qa, ka, va, tq, name):
    s = qa.shape[1]
    nq = s // tq

    def body(q_ref, k_ref, v_ref, o_ref, ql_ref):
        i = pl.program_id(1)
        lane = _lane((tq, LANES))
        outs = []
        for e in range(2):
            q = q_ref[e]

            def step(j, carry, masked):
                m, acc = carry
                off = pl.multiple_of(j * tq, tq)
                kb = k_ref[e, pl.ds(off, tq), :]
                vb = v_ref[e, pl.ds(off, tq), :]
                sc = _dot_nt(q, kb)
                if masked:
                    rr = lax.broadcasted_iota(jnp.int32, (tq, tq), 0)
                    cc = lax.broadcasted_iota(jnp.int32, (tq, tq), 1)
                    sc = jnp.where(cc <= rr, sc, -jnp.inf)
                m_new = jnp.maximum(m, jnp.max(sc, axis=-1, keepdims=True))
                pr = jnp.exp(sc - m_new)
                acc = acc * jnp.exp(m - m_new) + _dot(pr.astype(BF16), vb)
                return m_new, acc

            init = (jnp.full((tq, 1), -jnp.inf, F32), jnp.zeros((tq, LANES), F32))
            carry = lax.fori_loop(0, i, functools.partial(step, masked=False), init)
            m, acc = step(i, carry, True)
            l = acc[:, L_F:L_F + 1]
            outs.append(acc / l)
            lse = m + jnp.log(l)
            h0, h1, h2 = _split3(-lse)
            ql = jnp.where(lane == L_LSE, h0.astype(F32),
                           jnp.where(lane == L_LSE + 1, h1.astype(F32),
                                     jnp.where(lane == L_LSE + 2, h2.astype(F32), q.astype(F32))))
            ql_ref[e] = ql.astype(BF16)
        o_ref[...] = jnp.where(lane < DH, outs[0], pltpu.roll(outs[1], DH, axis=1))

    res = pl.BlockSpec((2, s, LANES), lambda p, i: (p, 0, 0))
    qs = pl.BlockSpec((2, tq, LANES), lambda p, i: (p, i, 0))
    return pl.pallas_call(
        body, name=name, grid=(NP, nq),
        in_specs=[qs, res, res],
        out_specs=[pl.BlockSpec((tq, LANES), lambda p, i: (i, p)), qs],
        out_shape=[jax.ShapeDtypeStruct((s, D), F32), jax.ShapeDtypeStruct((H, s, LANES), BF16)],
        compiler_params=_cparams(("arbitrary", "arbitrary")),
    )(qa, ka, va)


def _attn_bwd(ql, ka, va, doa, tq, name):
    s = ql.shape[1]
    nq = s // tq

    def body(q_ref, k_ref, v_ref, do_ref, dq_ref, dk_ref, dv_ref):
        dq_ref[...] = jnp.zeros_like(dq_ref)

        def kv_block(j, _):
            joff = pl.multiple_of(j * tq, tq)
            kb = k_ref[0, pl.ds(joff, tq), :]
            vb = v_ref[0, pl.ds(joff, tq), :]

            def step(i, carry, masked):
                dk, dv = carry
                ioff = pl.multiple_of(i * tq, tq)
                qb = q_ref[0, pl.ds(ioff, tq), :]
                dob = do_ref[0, pl.ds(ioff, tq), :]
                pr = jnp.exp(_dot_nt(qb, kb))
                if masked:
                    rr = lax.broadcasted_iota(jnp.int32, (tq, tq), 0)
                    cc = lax.broadcasted_iota(jnp.int32, (tq, tq), 1)
                    pr = jnp.where(cc <= rr, pr, 0.0)
                ds = (pr * _dot_nt(dob, vb)).astype(BF16)
                dv = dv + _dot_tn(pr.astype(BF16), dob)
                dk = dk + _dot_tn(ds, qb)
                dq_ref[0, pl.ds(ioff, tq), :] += _dot(ds, kb)
                return dk, dv

            zero = jnp.zeros((tq, LANES), F32)
            carry = step(j, (zero, zero), True)
            dk, dv = lax.fori_loop(j + 1, nq, functools.partial(step, masked=False), carry)
            dk_ref[0, pl.ds(joff, tq), :] = dk
            dv_ref[0, pl.ds(joff, tq), :] = dv
            return 0

        lax.fori_loop(0, nq, kv_block, 0)

    hs = pl.BlockSpec((1, s, LANES), lambda h: (h, 0, 0))
    shp = jax.ShapeDtypeStruct((H, s, LANES), F32)
    return pl.pallas_call(
        body, name=name, grid=(H,),
        in_specs=[hs, hs, hs, hs], out_specs=[hs, hs, hs], out_shape=[shp, shp, shp],
        compiler_params=_cparams(("arbitrary",)),
    )(ql, ka, va, doa)


def _gate(att, proj, ts, name):
    s = att.shape[0]

    def body(a_ref, o_ref, g_ref):
        g_ref[...] = (a_ref[...] * _sigmoid(o_ref[...])).astype(BF16)

    return pl.pallas_call(
        body, name=name, grid=(s // ts,),
        in_specs=[pl.BlockSpec((ts, D), lambda i: (i, 0)), pl.BlockSpec((ts, D), lambda i: (i, 3))],
        out_specs=pl.BlockSpec((ts, D), lambda i: (i, 0)),
        out_shape=jax.ShapeDtypeStruct((s, D), BF16),
        compiler_params=_cparams(("arbitrary",)),
    )(att, proj)


def _attn_bwd_prep(dgated, att, proj, ts, name):
    s = att.shape[0]

    def body(dg_ref, a_ref, o_ref, doa_ref, dop_ref):
        lane = _lane((ts, LANES))
        bones = _head_block_ones()
        for p in range(NP):
            sl = slice(p * LANES, (p + 1) * LANES)
            dg, a = dg_ref[:, sl], a_ref[:, sl]
            sig = _sigmoid(o_ref[:, sl])
            datt = dg * sig
            dop_ref[:, sl] = (dg * a * sig * (1.0 - sig)).astype(BF16)
            delta = _exact_dot_r(datt * a, bones)
            for e in range(2):
                de, dl = (datt, delta) if e == 0 else (pltpu.roll(datt, DH, axis=1), pltpu.roll(delta, DH, axis=1))
                h0, h1, h2 = _split3(-dl[:, 0:1])
                aug = jnp.where(lane == L_F, h0.astype(F32),
                                jnp.where(lane == L_F + 1, h1.astype(F32),
                                          jnp.where(lane == L_F + 2, h2.astype(F32), 0.0)))
                doa_ref[2 * p + e] = jnp.where(lane < DH, de, aug).astype(BF16)

    row = pl.BlockSpec((ts, D), lambda i: (i, 0))
    return pl.pallas_call(
        body, name=name, grid=(s // ts,),
        in_specs=[row, row, pl.BlockSpec((ts, D), lambda i: (i, 3))],
        out_specs=[pl.BlockSpec((H, ts, LANES), lambda i: (0, i, 0)), row],
        out_shape=[jax.ShapeDtypeStruct((H, s, LANES), BF16), jax.ShapeDtypeStruct((s, D), BF16)],
        compiler_params=_cparams(("arbitrary",)),
    )(dgated, att, proj)


def _fox_post_bwd(proj, dqa, dka, dva, dop, qg2, kg2, bf, ts, name):
    s = proj.shape[0]
    nt = s // ts

    def body(p_ref, dq_ref, dk_ref, dv_ref, dop_ref, qg_ref, kg_ref, bf_ref, o_ref, red_ref, carry):
        @pl.when(pl.program_id(0) == 0)
        def _():
            carry[...] = jnp.zeros_like(carry)
            red_ref[...] = jnp.zeros_like(red_ref)
        lane = _lane((ts, LANES))
        bones = _head_block_ones()
        d_f = jnp.zeros((ts, LANES), F32)
        dqg = jnp.zeros((1, LANES), F32)
        dkg = jnp.zeros((1, LANES), F32)
        for p in range(NP):
            pair = [jnp.where(lane < DH, ref[2 * p], pltpu.roll(ref[2 * p + 1], DH, axis=1))
                    for ref in (dq_ref, dk_ref, dv_ref)]
            for e in range(2):
                h = 2 * p + e
                col = dq_ref[h][:, L_F:L_F + 1] - dk_ref[h][:, L_NF:L_NF + 1]
                d_f = jnp.where(lane == h, col, d_f)
            for idx, (g_ref, base) in enumerate(((qg_ref, 0), (kg_ref, D))):
                x = p_ref[:, base + p * LANES:base + (p + 1) * LANES]
                _, xh, r = _pair_norm(x, g_ref[...], bones)
                dn = pair[idx] * (SCALE if idx == 0 else 1.0)
                t = dn * g_ref[...]
                mean_txh = _exact_dot_r(t * xh, bones) * (1.0 / DH)
                dx = r * (t - xh * mean_txh)
                o_ref[:, base + p * LANES:base + (p + 1) * LANES] = dx.astype(BF16)
                gsum = jnp.sum(dn * xh, axis=0, keepdims=True)
                if idx == 0:
                    dqg = dqg + gsum
                else:
                    dkg = dkg + gsum
            o_ref[:, 2 * D + p * LANES:2 * D + (p + 1) * LANES] = pair[2].astype(BF16)
        o_ref[:, 3 * D:4 * D] = dop_ref[...]
        rr = lax.broadcasted_iota(jnp.int32, (ts, ts), 0)
        cc = lax.broadcasted_iota(jnp.int32, (ts, ts), 1)
        utri = (cc >= rr).astype(BF16)
        dlogf = _exact_dot(utri, d_f) + carry[0:1, :]
        carry[0:1, :] = dlogf[0:1, :]
        xf = p_ref[:, 4 * D:4 * D + LANES] + bf_ref[...]
        dfl = jnp.where(lane < H, dlogf * _sigmoid(-xf), 0.0)
        o_ref[:, 4 * D:4 * D + LANES] = dfl.astype(BF16)
        red_ref[0:1, :] += dqg
        red_ref[1:2, :] += dkg
        red_ref[2:3, :] += jnp.sum(dfl, axis=0, keepdims=True)

    hs = pl.BlockSpec((H, ts, LANES), lambda i: (0, nt - 1 - i, 0))
    vec = pl.BlockSpec((1, LANES), lambda i: (0, 0))
    return pl.pallas_call(
        body, name=name, grid=(nt,),
        in_specs=[pl.BlockSpec((ts, FOX_NP), lambda i: (nt - 1 - i, 0)), hs, hs, hs,
                  pl.BlockSpec((ts, D), lambda i: (nt - 1 - i, 0)), vec, vec, vec],
        out_specs=[pl.BlockSpec((ts, FOX_NP), lambda i: (nt - 1 - i, 0)),
                   pl.BlockSpec((8, LANES), lambda i: (0, 0))],
        out_shape=[jax.ShapeDtypeStruct((s, FOX_NP), BF16), jax.ShapeDtypeStruct((8, LANES), F32)],
        scratch_shapes=[pltpu.VMEM((8, LANES), F32)],
        compiler_params=_cparams(("arbitrary",)),
    )(proj, dqa, dka, dva, dop, qg2, kg2, bf)


HALO = 16
TS = 512
TQ = 512
TR = 256
TP = 256


def _shift_down(x, k):
    return pltpu.roll(x, k, axis=0)


def _shift_up(x, k):
    return pltpu.roll(x, x.shape[0] - k, axis=0)


def _conv_gate(a, cw, cb, ts, name):
    s = a.shape[0]
    hb = ts // HALO

    def body(prev_ref, a_ref, cw_ref, cb_ref, f_ref):
        i = pl.program_id(0)
        prev = jnp.where(i > 0, prev_ref[...].astype(F32), 0.0)
        ext = jnp.concatenate([prev, a_ref[...].astype(F32)], axis=0)
        ap = (_shift_down(ext, 2) * cw_ref[0:1, :] + _shift_down(ext, 1) * cw_ref[1:2, :]
              + ext * cw_ref[2:3, :] + cb_ref[...])[HALO:, :]
        g, val = ap[:, :GT], ap[:, GT:]
        f_ref[...] = (g * _sigmoid(g) * val).astype(BF16)

    return pl.pallas_call(
        body, name=name, grid=(s // ts, NGT),
        in_specs=[pl.BlockSpec((HALO, 2 * GT), lambda i, j: (jnp.maximum(i * hb - 1, 0), j)),
                  pl.BlockSpec((ts, 2 * GT), lambda i, j: (i, j)),
                  pl.BlockSpec((8, 2 * GT), lambda i, j: (0, j)),
                  pl.BlockSpec((1, 2 * GT), lambda i, j: (0, j))],
        out_specs=pl.BlockSpec((ts, GT), lambda i, j: (i, j)),
        out_shape=jax.ShapeDtypeStruct((s, DFF), BF16),
        compiler_params=_cparams(("arbitrary", "arbitrary")),
    )(a, a, cw, cb)


def _conv_gate_bwd(a, df, cw, cb, ts, name):
    s = a.shape[0]
    hb = ts // HALO
    nt = s // ts

    def body(prev_ref, a_ref, next_ref, df_ref, dfn_ref, cw_ref, cb_ref, da_ref, red_ref):
        i = pl.program_id(1)

        @pl.when(i == 0)
        def _():
            red_ref[...] = jnp.zeros_like(red_ref)
        prev = jnp.where(i > 0, prev_ref[...].astype(F32), 0.0)
        ext = jnp.concatenate([prev, a_ref[...].astype(F32), next_ref[...].astype(F32)], axis=0)
        dfn = jnp.where(i < nt - 1, dfn_ref[...].astype(F32), 0.0)
        dfe = jnp.concatenate([jnp.zeros((HALO, GT), F32), df_ref[...].astype(F32), dfn], axis=0)
        am2, am1 = _shift_down(ext, 2), _shift_down(ext, 1)
        ap = am2 * cw_ref[0:1, :] + am1 * cw_ref[1:2, :] + ext * cw_ref[2:3, :] + cb_ref[...]
        g, val = ap[:, :GT], ap[:, GT:]
        sg = _sigmoid(g)
        dap = jnp.concatenate([dfe * val * (sg * (1.0 + g * (1.0 - sg))), dfe * (g * sg)], axis=1)
        da = dap * cw_ref[2:3, :] + _shift_up(dap, 1) * cw_ref[1:2, :] + _shift_up(dap, 2) * cw_ref[0:1, :]
        da_ref[...] = da[HALO:HALO + ts, :].astype(BF16)
        main = slice(HALO, HALO + ts)
        red_ref[0:1, :] += jnp.sum((am2 * dap)[main], axis=0, keepdims=True)
        red_ref[1:2, :] += jnp.sum((am1 * dap)[main], axis=0, keepdims=True)
        red_ref[2:3, :] += jnp.sum((ext * dap)[main], axis=0, keepdims=True)
        red_ref[3:4, :] += jnp.sum(dap[main], axis=0, keepdims=True)

    nhb = s // HALO
    return pl.pallas_call(
        body, name=name, grid=(NGT, nt),
        in_specs=[pl.BlockSpec((HALO, 2 * GT), lambda j, i: (jnp.maximum(i * hb - 1, 0), j)),
                  pl.BlockSpec((ts, 2 * GT), lambda j, i: (i, j)),
                  pl.BlockSpec((HALO, 2 * GT), lambda j, i: (jnp.minimum((i + 1) * hb, nhb - 1), j)),
                  pl.BlockSpec((ts, GT), lambda j, i: (i, j)),
                  pl.BlockSpec((HALO, GT), lambda j, i: (jnp.minimum((i + 1) * hb, nhb - 1), j)),
                  pl.BlockSpec((8, 2 * GT), lambda j, i: (0, j)),
                  pl.BlockSpec((1, 2 * GT), lambda j, i: (0, j))],
        out_specs=[pl.BlockSpec((ts, 2 * GT), lambda j, i: (i, j)),
                   pl.BlockSpec((8, 2 * GT), lambda j, i: (0, j))],
        out_shape=[jax.ShapeDtypeStruct((s, 2 * DFF), BF16), jax.ShapeDtypeStruct((8, 2 * DFF), F32)],
        compiler_params=_cparams(("arbitrary", "arbitrary")),
    )(a, a, a, df, df, cw, cb)


def _chunk_mask(transposed=False):
    t = lax.broadcasted_iota(jnp.int32, (SGB, SGB), 0) // CHUNK
    u = lax.broadcasted_iota(jnp.int32, (SGB, SGB), 1) // CHUNK
    return (t <= u) if transposed else (u <= t)


def _sgu_ln(zv, gain, bias):
    v = _gelu(zv)
    mu = jnp.mean(v, axis=-1, keepdims=True)
    vc = v - mu
    rstd = lax.rsqrt(jnp.mean(vc * vc, axis=-1, keepdims=True) + EPS)
    vhat = vc * rstd
    return vhat * gain + bias, vhat, rstd


def _sgu_fwd(z, vgain, vbias, ws, bst, tr, name):
    s = z.shape[0]

    def body(zu_ref, zv_ref, vg_ref, vb_ref, ws_ref, bs_ref, y_ref):
        u = _gelu(zu_ref[...].astype(F32))
        vn, _, _ = _sgu_ln(zv_ref[...].astype(F32), vg_ref[...], vb_ref[...])
        vn = vn.astype(BF16)
        mask = _chunk_mask()
        for g in range(SGG):
            w = jnp.where(mask, ws_ref[g], 0.0).astype(BF16)
            for b in range(tr // SGB):
                rs, cs = slice(b * SGB, (b + 1) * SGB), slice(g * SGC, (g + 1) * SGC)
                mixed = _dot(w, vn[rs, cs]) + bs_ref[:, g:g + 1]
                y_ref[rs, cs] = (u[rs, cs] * mixed).astype(BF16)

    vec = pl.BlockSpec((1, SGW), lambda i: (0, 0))
    return pl.pallas_call(
        body, name=name, grid=(s // tr,),
        in_specs=[pl.BlockSpec((tr, SGW), lambda i: (i, 0)), pl.BlockSpec((tr, SGW), lambda i: (i, 1)),
                  vec, vec, pl.BlockSpec((SGG, SGB, SGB), lambda i: (0, 0, 0)),
                  pl.BlockSpec((SGB, LANES), lambda i: (0, 0))],
        out_specs=pl.BlockSpec((tr, SGW), lambda i: (i, 0)),
        out_shape=jax.ShapeDtypeStruct((s, SGW), BF16),
        compiler_params=_cparams(("arbitrary",)),
    )(z, z, vgain, vbias, ws, bst)


def _sgu_bwd(z, dy, vgain, vbias, ws, wst, bst, tr, name):
    s = z.shape[0]

    def body(zu_ref, zv_ref, dy_ref, vg_ref, vb_ref, ws_ref, wst_ref, bs_ref,
             dz_ref, rb_ref, rv_ref, dws_ref, dbs_ref, dvn_s):
        @pl.when(pl.program_id(0) == 0)
        def _():
            rb_ref[...] = jnp.zeros_like(rb_ref)
            rv_ref[...] = jnp.zeros_like(rv_ref)
            dws_ref[...] = jnp.zeros_like(dws_ref)
            dbs_ref[...] = jnp.zeros_like(dbs_ref)
        zu = zu_ref[...].astype(F32)
        zv = zv_ref[...].astype(F32)
        u = _gelu(zu)
        vn, vhat, rstd = _sgu_ln(zv, vg_ref[...], vb_ref[...])
        vnb = vn.astype(BF16)
        dyv = dy_ref[...].astype(F32)
        dmix = (dyv * u).astype(BF16)
        mask = _chunk_mask()
        mask_t = _chunk_mask(transposed=True)
        lane = _lane((SGB, LANES))
        dbs = jnp.zeros((SGB, LANES), F32)
        for g in range(SGG):
            w = jnp.where(mask, ws_ref[g], 0.0).astype(BF16)
            wt = jnp.where(mask_t, wst_ref[g], 0.0).astype(BF16)
            dw = jnp.zeros((SGB, SGB), F32)
            for b in range(tr // SGB):
                rs, cs = slice(b * SGB, (b + 1) * SGB), slice(g * SGC, (g + 1) * SGC)
                mixed = _dot(w, vnb[rs, cs]) + bs_ref[:, g:g + 1]
                dz_ref[rs, cs] = (dyv[rs, cs] * mixed * _gelu_grad(zu[rs, cs])).astype(BF16)
                dm = dmix[rs, cs]
                dw = dw + _dot_nt(dm, vnb[rs, cs])
                dbs = dbs + jnp.where(lane == g, jnp.sum(dm.astype(F32), axis=-1, keepdims=True), 0.0)
                dvn_s[rs, cs] = _dot(wt, dm)
            dws_ref[g] += jnp.where(mask, dw, 0.0)
        dbs_ref[...] += dbs
        dvn = dvn_s[...]
        rv_ref[0:1, :] += jnp.sum(dvn * vhat, axis=0, keepdims=True)
        rv_ref[1:2, :] += jnp.sum(dvn, axis=0, keepdims=True)
        dvh = dvn * vg_ref[...]
        dv = rstd * (dvh - jnp.mean(dvh, axis=-1, keepdims=True)
                     - vhat * jnp.mean(dvh * vhat, axis=-1, keepdims=True))
        dz_ref[:, SGW:] = (dv * _gelu_grad(zv)).astype(BF16)
        dzf = dz_ref[...].astype(F32)
        rb_ref[0:1, :] += jnp.sum(dzf, axis=0, keepdims=True)

    vec = pl.BlockSpec((1, SGW), lambda i: (0, 0))
    wsp = pl.BlockSpec((SGG, SGB, SGB), lambda i: (0, 0, 0))
    return pl.pallas_call(
        body, name=name, grid=(s // tr,),
        in_specs=[pl.BlockSpec((tr, SGW), lambda i: (i, 0)), pl.BlockSpec((tr, SGW), lambda i: (i, 1)),
                  pl.BlockSpec((tr, SGW), lambda i: (i, 0)), vec, vec, wsp, wsp,
                  pl.BlockSpec((SGB, LANES), lambda i: (0, 0))],
        out_specs=[pl.BlockSpec((tr, 2 * SGW), lambda i: (i, 0)),
                   pl.BlockSpec((8, 2 * SGW), lambda i: (0, 0)),
                   pl.BlockSpec((8, SGW), lambda i: (0, 0)), wsp,
                   pl.BlockSpec((SGB, LANES), lambda i: (0, 0))],
        out_shape=[jax.ShapeDtypeStruct((s, 2 * SGW), BF16), jax.ShapeDtypeStruct((8, 2 * SGW), F32),
                   jax.ShapeDtypeStruct((8, SGW), F32), jax.ShapeDtypeStruct((SGG, SGB, SGB), F32),
                   jax.ShapeDtypeStruct((SGB, LANES), F32)],
        scratch_shapes=[pltpu.VMEM((tr, SGW), F32)],
        compiler_params=_cparams(("arbitrary",)),
    )(z, z, dy, vgain, vbias, ws, wst, bst)


def _final_loss(x, fg, tgt, gprev, yprev, ts, name):
    s, d = x.shape

    def body(x_ref, fg_ref, t_ref, g_ref, y_ref, l_ref, dx_ref, dy_ref, red_ref):
        @pl.when(pl.program_id(0) == 0)
        def _():
            l_ref[...] = jnp.zeros_like(l_ref)
            red_ref[...] = jnp.zeros_like(red_ref)
        xv = x_ref[...]
        r = _rstd_rows(xv)
        xh = xv * r
        err = xh * fg_ref[...] - t_ref[...]
        l_ref[...] += 0.5 * jnp.sum(jnp.mean(err * err, axis=-1, keepdims=True))
        dyo = err * (1.0 / d)
        dxh = dyo * fg_ref[...]
        dx = r * (dxh - xh * jnp.mean(dxh * xh, axis=-1, keepdims=True))
        dx_ref[...] = dx
        dy_ref[...] = (dx * g_ref[...]).astype(BF16)
        red_ref[0:1, :] += jnp.sum(dyo * xh, axis=0, keepdims=True)
        red_ref[1:2, :] += jnp.sum(dx * y_ref[...].astype(F32), axis=0, keepdims=True)

    row = pl.BlockSpec((ts, d), lambda i: (i, 0))
    vec = pl.BlockSpec((1, d), lambda i: (0, 0))
    return pl.pallas_call(
        body, name=name, grid=(s // ts,),
        in_specs=[row, vec, row, vec, row],
        out_specs=[pl.BlockSpec((8, LANES), lambda i: (0, 0)), row, row, pl.BlockSpec((8, d), lambda i: (0, 0))],
        out_shape=[jax.ShapeDtypeStruct((8, LANES), F32), jax.ShapeDtypeStruct((s, d), F32),
                   jax.ShapeDtypeStruct((s, d), BF16), jax.ShapeDtypeStruct((8, d), F32)],
        compiler_params=_cparams(("arbitrary",)),
    )(x, fg, tgt, gprev, yprev)


def _norm_bwd(xin, dh, dxout, ng, sc, gprev, yprev, ts, name):
    s, d = xin.shape
    has_prev = gprev is not None

    def body(*refs):
        if has_prev:
            x_ref, dh_ref, dxo_ref, ng_ref, sc_ref, g_ref, y_ref, dx_ref, dy_ref, red_ref = refs
        else:
            x_ref, dh_ref, dxo_ref, ng_ref, sc_ref, dx_ref, red_ref = refs

        @pl.when(pl.program_id(0) == 0)
        def _():
            red_ref[...] = jnp.zeros_like(red_ref)
        xv = x_ref[...]
        r = _rstd_rows(xv)
        xh = xv * r
        dhv = dh_ref[...]
        dr = dhv * (1.0 + sc_ref[...])
        t = dr * ng_ref[...]
        dx = dxo_ref[...] + r * (t - xh * jnp.mean(t * xh, axis=-1, keepdims=True))
        dx_ref[...] = dx
        red_ref[0:1, :] += jnp.sum(dhv, axis=0, keepdims=True)
        red_ref[1:2, :] += jnp.sum(dhv * (xh * ng_ref[...]), axis=0, keepdims=True)
        red_ref[2:3, :] += jnp.sum(dr * xh, axis=0, keepdims=True)
        if has_prev:
            dy_ref[...] = (dx * g_ref[...]).astype(BF16)
            red_ref[3:4, :] += jnp.sum(dx * y_ref[...].astype(F32), axis=0, keepdims=True)

    row = pl.BlockSpec((ts, d), lambda i: (i, 0))
    vec = pl.BlockSpec((1, d), lambda i: (0, 0))
    red = pl.BlockSpec((8, d), lambda i: (0, 0))
    if has_prev:
        in_specs, args = [row, row, row, vec, vec, vec, row], (xin, dh, dxout, ng, sc, gprev, yprev)
        out_specs = [row, row, red]
        out_shape = [jax.ShapeDtypeStruct((s, d), F32), jax.ShapeDtypeStruct((s, d), BF16),
                     jax.ShapeDtypeStruct((8, d), F32)]
    else:
        in_specs, args = [row, row, row, vec, vec], (xin, dh, dxout, ng, sc)
        out_specs = [row, red]
        out_shape = [jax.ShapeDtypeStruct((s, d), F32), jax.ShapeDtypeStruct((8, d), F32)]
    return pl.pallas_call(
        body, name=name, grid=(s // ts,), in_specs=in_specs, out_specs=out_specs, out_shape=out_shape,
        compiler_params=_cparams(("arbitrary",)),
    )(*args)


def _ada_mod(c_all, ada_w, ada_b):
    nb = c_all.shape[0]
    da = ada_w.shape[2]

    def body(c_ref, w_ref, b_ref, o_ref, ca_ref):
        cv = c_ref[...]
        ca = cv * _sigmoid(cv)
        ca_ref[...] = ca
        o_ref[0] = lax.dot_general(ca, w_ref[0], (((1,), (0,)), ((), ())), precision=lax.Precision.HIGHEST,
                                   preferred_element_type=F32) + b_ref[0]

    return pl.pallas_call(
        body, name="ada_mod", grid=(2,),
        in_specs=[pl.BlockSpec((nb, D), lambda i: (0, 0)), pl.BlockSpec((1, D, da), lambda i: (i, 0, 0)),
                  pl.BlockSpec((1, 1, da), lambda i: (i, 0, 0))],
        out_specs=[pl.BlockSpec((1, nb, da), lambda i: (i, 0, 0)), pl.BlockSpec((nb, D), lambda i: (0, 0))],
        out_shape=[jax.ShapeDtypeStruct((2, nb, da), F32), jax.ShapeDtypeStruct((nb, D), F32)],
        compiler_params=_cparams(("arbitrary",)),
    )(c_all, ada_w, ada_b)


def _ada_w_grad(c_act_t, dmod):
    nb = c_act_t.shape[1]
    da = dmod.shape[2]
    tn = 512

    def body(c_ref, d_ref, o_ref):
        acc = c_ref[:, 0:1] * d_ref[0, 0:1, :]
        for b in range(1, nb):
            acc = acc + c_ref[:, b:b + 1] * d_ref[0, b:b + 1, :]
        o_ref[0] = acc

    return pl.pallas_call(
        body, name="ada_w_grad", grid=(2, da // tn),
        in_specs=[pl.BlockSpec((D, nb), lambda i, j: (0, 0)), pl.BlockSpec((1, nb, tn), lambda i, j: (i, 0, j))],
        out_specs=pl.BlockSpec((1, D, tn), lambda i, j: (i, 0, j)),
        out_shape=jax.ShapeDtypeStruct((2, D, da), F32),
        compiler_params=_cparams(("arbitrary", "arbitrary")),
    )(c_act_t, dmod)


def _interleave_cols(w):
    lead = w.shape[:-1]
    return w.reshape(lead + (2, NGT, GT)).swapaxes(-3, -2).reshape(lead + (2 * DFF,))


def _deinterleave_cols(w):
    lead = w.shape[:-1]
    return w.reshape(lead + (NGT, 2, GT)).swapaxes(-3, -2).reshape(lead + (2 * DFF,))


def _pad_rows8(v):
    return jnp.pad(v, ((0, 8 - v.shape[0]), (0, 0)))


def _local_step(x, tgt, mod, wts, small):
    s = x.shape[0]
    ts, tq, tr, tp = TS, TQ, TR, TP
    zb = lambda n: jnp.zeros((1, n), F32)
    m6 = mod.reshape(2, 6, 1, D)
    sh1, sc1, g1, sh2, sc2, g2 = ([m6[i, k] for i in range(2)] for k in range(6))
    n1g, n2g = small["norm1_g"], small["norm2_g"]
    row = lambda a, i: a[i:i + 1]

    qg2 = jnp.tile(small["fox_q_gain"], (1, 2))
    kg2 = jnp.tile(small["fox_k_gain"], (1, 2))
    bfp = jnp.pad(small["fox_b_f"], ((0, 0), (0, LANES - H)))
    proj, h1 = _norm_mod_matmul(x, row(n1g, 0), sc1[0], sh1[0], wts["fox_w_in"], zb(FOX_NP), F32, ts, 1408, "fox_in")
    qa, ka, va = _fox_post(proj, qg2, kg2, bfp, tp, "fox_post")
    att, ql = _attn_fwd(qa, ka, va, tq, "attn_fwd")
    gated = _gate(att, proj, ts, "fox_gate")
    x1, y0 = _matmul_residual(gated, wts["fox_w_out"], x, g1[0], ts, "fox_out")

    def ffn_fwd(xin, i, tag):
        cw = _pad_rows8(small["conv_w_il"][i])
        cb = small["conv_b_il"][i:i + 1]
        a, h = _norm_mod_matmul(xin, row(n2g, i), sc2[i], sh2[i], wts["ffn_w_up"][i], zb(2 * DFF), BF16, ts, 1408,
                                "ffn_up" + tag)
        f = _conv_gate(a, cw, cb, ts, "ffn_conv" + tag)
        xo, y = _matmul_residual(f, wts["ffn_w_down"][i], xin, g2[i], ts, "ffn_down" + tag)
        return xo, (a, h, f, y, cw, cb)

    x2, ffn0 = ffn_fwd(x1, 0, "0")

    bst = jnp.pad(small["sgu_b_s"].T, ((0, 0), (0, LANES - SGG)))
    ws = small["sgu_w_s"]
    z, h3 = _norm_mod_matmul(x2, row(n1g, 1), sc1[1], sh1[1], wts["sgu_w_in"], small["sgu_b_in"], BF16, ts, 1024,
                             "sgu_in")
    yy = _sgu_fwd(z, small["sgu_v_gain"], small["sgu_v_bias"], ws, bst, tr, "sgu_mix")
    x3, y1 = _matmul_residual(yy, wts["sgu_w_out"], x2, g1[1], ts, "sgu_out")
    x4, ffn1 = ffn_fwd(x3, 1, "1")

    lsum, dx4, dy, redf = _final_loss(x4, small["final_g"], tgt, g2[1], ffn1[3], ts, "final_loss")
    grads = {"final_g": redf[0]}
    dmod = [[None] * 6, [None] * 6]
    dmod[1][5] = redf[1]

    def ffn_bwd(dxo, dy2, xin, i, saved, gprev, yprev, tag):
        a, h, f, _, cw, cb = saved
        wd, wu = wts["ffn_w_down"][i], wts["ffn_w_up"][i]
        g_wd = _matmul(f, dy2, True, False, 1408, D, ts, F32, "ffn_dwdown" + tag)
        df = _matmul(dy2, wd, False, True, ts, 1408, D, BF16, "ffn_df" + tag)
        da, redc = _conv_gate_bwd(a, df, cw, cb, ts, "ffn_conv_bwd" + tag)
        g_wu = _matmul(h, da, True, False, D, 1408, ts, F32, "ffn_dwup" + tag)
        dh = _matmul(da, wu, False, True, ts, D, 1408, F32, "ffn_dh" + tag)
        outs = _norm_bwd(xin, dh, dxo, row(n2g, i), sc2[i], gprev, yprev, ts, "ffn_norm_bwd" + tag)
        return outs, g_wd, g_wu, redc

    (dx3, dy1, red), g_wd1, g_wu1, redc1 = ffn_bwd(dx4, dy, x3, 1, ffn1, g1[1], y1, "1")
    dmod[1][3], dmod[1][4], dn2g1, dmod[1][2] = red[0], red[1], red[2], red[3]

    g_swo = _matmul(yy, dy1, True, False, 1024, D, ts, F32, "sgu_dwout")
    dyy = _matmul(dy1, wts["sgu_w_out"], False, True, ts, 1024, D, BF16, "sgu_dyy")
    wst = jnp.swapaxes(ws, 1, 2)
    dz, rb, rv, dws, dbst = _sgu_bwd(z, dyy, small["sgu_v_gain"], small["sgu_v_bias"], ws, wst, bst, tr, "sgu_mix_bwd")
    g_swi = _matmul(h3, dz, True, False, D, 1024, ts, F32, "sgu_dwin")
    dh3 = _matmul(dz, wts["sgu_w_in"], False, True, ts, D, 1024, F32, "sgu_dh")
    dx2, dy2_0, red = _norm_bwd(x2, dh3, dx3, row(n1g, 1), sc1[1], g2[0], ffn0[3], ts, "sgu_norm_bwd")
    dmod[1][0], dmod[1][1], dn1g1, dmod[0][5] = red[0], red[1], red[2], red[3]

    (dx1, dy0, red), g_wd0, g_wu0, redc0 = ffn_bwd(dx2, dy2_0, x1, 0, ffn0, g1[0], y0, "0")
    dmod[0][3], dmod[0][4], dn2g0, dmod[0][2] = red[0], red[1], red[2], red[3]

    g_fwo = _matmul(gated, dy0, True, False, D, D, ts, F32, "fox_dwout")
    dgated = _matmul(dy0, wts["fox_w_out"], False, True, ts, D, D, F32, "fox_dgated")
    doa, dop = _attn_bwd_prep(dgated, att, proj, ts, "attn_bwd_prep")
    dqa, dka, dva = _attn_bwd(ql, ka, va, doa, tq, "attn_bwd")
    dproj, redx = _fox_post_bwd(proj, dqa, dka, dva, dop, qg2, kg2, bfp, tp, "fox_post_bwd")
    g_fwi = _matmul(h1, dproj, True, False, D, 1408, ts, F32, "fox_dwin")
    dh1 = _matmul(dproj, wts["fox_w_in"], False, True, ts, D, 1408, F32, "fox_dh")
    dx0, red = _norm_bwd(x, dh1, dx1, row(n1g, 0), sc1[0], None, None, ts, "fox_norm_bwd")
    dmod[0][0], dmod[0][1], dn1g0 = red[0], red[1], red[2]

    grads.update(
        fox_w_in=g_fwi[:, :FOX_N], fox_w_out=g_fwo, sgu_w_in=g_swi, sgu_w_out=g_swo,
        ffn_w_up=jnp.stack([_deinterleave_cols(g_wu0), _deinterleave_cols(g_wu1)]),
        ffn_w_down=jnp.stack([g_wd0, g_wd1]),
        fox_q_gain=redx[0, :DH] + redx[0, DH:], fox_k_gain=redx[1, :DH] + redx[1, DH:], fox_b_f=redx[2, :H],
        sgu_b_in=rb[0], sgu_v_gain=rv[0], sgu_v_bias=rv[1], sgu_w_s=dws, sgu_b_s=dbst[:, :SGG].T,
        ffn_conv_w=jnp.stack([_deinterleave_cols(redc0[0:3]), _deinterleave_cols(redc1[0:3])]),
        ffn_conv_b=jnp.stack([_deinterleave_cols(redc0[3]), _deinterleave_cols(redc1[3])]),
        norm1_g=jnp.stack([dn1g0, dn1g1]), norm2_g=jnp.stack([dn2g0, dn2g1]),
    )
    dmod_arr = jnp.stack([jnp.concatenate(dmod[0]), jnp.concatenate(dmod[1])])
    return lsum[0, 0], dx0, grads, dmod_arr


N_DEV = 8
N_CHIP = 4
HBM_SPEC = pl.BlockSpec(memory_space=pltpu.HBM)
VMEM_SPEC = pl.BlockSpec(memory_space=pltpu.VMEM)


def _mesh_pos():
    return lax.axis_index("x"), lax.axis_index("y"), lax.axis_index("c")


def _other_chips(x, y):
    return [(1 - x, y), (x, 1 - y), (1 - x, 1 - y)]


def _remote(src, dst, ssem, rsem, dev):
    return pltpu.make_async_remote_copy(src_ref=src, dst_ref=dst, send_sem=ssem, recv_sem=rsem,
                                        device_id=dev, device_id_type=MESH)


def _allgather8(xb, name):
    m_per, n = xb.shape

    def body(x_ref, out_ref, send_sems, recv_sems, local_sem):
        x, y, c = _mesh_pos()
        me, sibling = (x, y, c), (x, y, 1 - c)
        chips = _other_chips(x, y)

        def rows(px, py, pc):
            return out_ref.at[pl.ds((4 * px + 2 * py + pc) * m_per, m_per), :]

        def copy(k, block, to, src=None):
            return _remote(rows(*block) if src is None else src, rows(*block),
                           send_sems.at[k], recv_sems.at[k], to)

        mine = pltpu.make_async_copy(x_ref, rows(*me), local_sem)
        mine.start()
        first = [copy(0, me, sibling, src=x_ref)]
        first += [copy(1 + j, me, (*chip, c), src=x_ref) for j, chip in enumerate(chips)]
        for cp in first:
            cp.start()
        passed = [copy(4 + j, (*chip, c), sibling) for j, chip in enumerate(chips)]
        for j, chip in enumerate(chips):
            copy(1 + j, (*chip, c), me).wait_recv()
            passed[j].start()
        copy(0, sibling, me).wait_recv()
        for j, chip in enumerate(chips):
            copy(4 + j, (*chip, 1 - c), me).wait_recv()
        for cp in first + passed:
            cp.wait_send()
        mine.wait()

    return pl.pallas_call(
        body, name=name,
        out_shape=jax.ShapeDtypeStruct((N_DEV * m_per, n), xb.dtype),
        in_specs=[VMEM_SPEC], out_specs=VMEM_SPEC,
        scratch_shapes=[pltpu.SemaphoreType.DMA((7,)), pltpu.SemaphoreType.DMA((7,)), pltpu.SemaphoreType.DMA],
        compiler_params=pltpu.CompilerParams(vmem_limit_bytes=V7X_VMEM_LIMIT),
    )(xb)


def _gather_shards(pack, name):
    r, n = pack.shape
    rh = r // 2

    def body(p_ref, o_ref, send_sems, recv_sems, pass_send, pass_recv, local_sem):
        x, y, c = _mesh_pos()
        me = 2 * x + y
        sibling = (x, y, 1 - c)
        chips = _other_chips(x, y)

        def half(ci, hf):
            return o_ref.at[ci, pl.ds(hf * rh, rh), :]

        mine = pltpu.make_async_copy(p_ref, o_ref.at[me], local_sem)
        mine.start()
        sends = [_remote(p_ref.at[pl.ds(c * rh, rh), :], half(me, c), send_sems.at[k], recv_sems.at[k], (*chip, c))
                 for k, chip in enumerate(chips)]
        for cp in sends:
            cp.start()
        passed = []
        for k, chip in enumerate(chips):
            ci = 2 * chip[0] + chip[1]
            _remote(half(ci, c), half(ci, c), send_sems.at[k], recv_sems.at[k], (*chip, c)).wait_recv()
            cp = _remote(half(ci, c), half(ci, c), pass_send.at[k], pass_recv.at[k], sibling)
            cp.start()
            passed.append(cp)
        for k, chip in enumerate(chips):
            ci = 2 * chip[0] + chip[1]
            _remote(half(ci, 1 - c), half(ci, 1 - c), pass_send.at[k], pass_recv.at[k], sibling).wait_recv()
        for cp in sends + passed:
            cp.wait_send()
        mine.wait()

    return pl.pallas_call(
        body, name=name,
        out_shape=jax.ShapeDtypeStruct((N_CHIP, r, n), pack.dtype),
        in_specs=[HBM_SPEC], out_specs=HBM_SPEC,
        scratch_shapes=[pltpu.SemaphoreType.DMA((3,)), pltpu.SemaphoreType.DMA((3,)),
                        pltpu.SemaphoreType.DMA((3,)), pltpu.SemaphoreType.DMA((3,)), pltpu.SemaphoreType.DMA],
    )(pack)


def _rs_to_sibling(g, name):
    nc, r, n = g.shape
    rh = r // 2

    def body(g_ref, o_ref, ssem, rsem):
        x, y, c = _mesh_pos()
        cp = _remote(g_ref.at[:, pl.ds((1 - c) * rh, rh), :], o_ref, ssem, rsem, (x, y, 1 - c))
        cp.start()
        cp.wait()

    return pl.pallas_call(
        body, name=name, out_shape=jax.ShapeDtypeStruct((nc, rh, n), g.dtype),
        in_specs=[HBM_SPEC], out_specs=HBM_SPEC,
        scratch_shapes=[pltpu.SemaphoreType.DMA, pltpu.SemaphoreType.DMA],
    )(g)


def _rs_chip_sum(g, sib, c_arr, tr, name):
    nc, r, n = g.shape
    rh = r // 2
    g4 = g.reshape(nc, 2, rh, n)

    def body(c_ref, g_ref, s_ref, o_ref):
        o_ref[...] = (g_ref[0].astype(F32) + s_ref[...].astype(F32)).astype(BF16)

    return pl.pallas_call(
        body, name=name, out_shape=jax.ShapeDtypeStruct((nc, rh, n), BF16),
        grid_spec=pltpu.PrefetchScalarGridSpec(
            num_scalar_prefetch=1, grid=(nc, rh // tr),
            in_specs=[pl.BlockSpec((1, 1, tr, n), lambda j, i, cr: (j, cr[0], i, 0)),
                      pl.BlockSpec((1, tr, n), lambda j, i, cr: (j, i, 0))],
            out_specs=pl.BlockSpec((1, tr, n), lambda j, i, cr: (j, i, 0))),
        compiler_params=_cparams(("arbitrary", "arbitrary")),
    )(c_arr, g4, sib)


def _rs_across_chips(cs, name):
    nc, rh, n = cs.shape

    def body(cs_ref, o_ref, send_sems, recv_sems):
        x, y, c = _mesh_pos()
        cps = []
        for k, chip in enumerate(_other_chips(x, y)):
            ci = 2 * chip[0] + chip[1]
            cp = _remote(cs_ref.at[ci], o_ref.at[k], send_sems.at[k], recv_sems.at[k], (*chip, c))
            cp.start()
            cps.append(cp)
        for cp in cps:
            cp.wait()

    return pl.pallas_call(
        body, name=name, out_shape=jax.ShapeDtypeStruct((3, rh, n), cs.dtype),
        in_specs=[HBM_SPEC], out_specs=HBM_SPEC,
        scratch_shapes=[pltpu.SemaphoreType.DMA((3,)), pltpu.SemaphoreType.DMA((3,))],
    )(cs)


def _rs_final_sum(cs, rcv, me_arr, tr, name):
    nc, rh, n = cs.shape

    def body(m_ref, c_ref, r_ref, o_ref):
        acc = c_ref[0].astype(F32)
        for k in range(3):
            acc = acc + r_ref[k].astype(F32)
        o_ref[...] = acc

    return pl.pallas_call(
        body, name=name, out_shape=jax.ShapeDtypeStruct((rh, n), F32),
        grid_spec=pltpu.PrefetchScalarGridSpec(
            num_scalar_prefetch=1, grid=(rh // tr,),
            in_specs=[pl.BlockSpec((1, tr, n), lambda i, mr: (mr[0], i, 0)),
                      pl.BlockSpec((3, tr, n), lambda i, mr: (0, i, 0))],
            out_specs=pl.BlockSpec((tr, n), lambda i, mr: (i, 0))),
        compiler_params=_cparams(("arbitrary",)),
    )(me_arr, cs, rcv)


def _rs_join_halves(half, name):
    rh, n = half.shape

    def body(h_ref, o_ref, ssem, rsem, lsem):
        x, y, c = _mesh_pos()
        sibling = (x, y, 1 - c)
        local = pltpu.make_async_copy(h_ref, o_ref.at[pl.ds(c * rh, rh), :], lsem)
        local.start()
        send = _remote(h_ref, o_ref.at[pl.ds(c * rh, rh), :], ssem, rsem, sibling)
        send.start()
        _remote(h_ref, o_ref.at[pl.ds((1 - c) * rh, rh), :], ssem, rsem, sibling).wait_recv()
        send.wait_send()
        local.wait()

    return pl.pallas_call(
        body, name=name, out_shape=jax.ShapeDtypeStruct((2 * rh, n), half.dtype),
        in_specs=[HBM_SPEC], out_specs=HBM_SPEC,
        scratch_shapes=[pltpu.SemaphoreType.DMA, pltpu.SemaphoreType.DMA, pltpu.SemaphoreType.DMA],
    )(half)


def _sum8(g, name):
    nd, r, n = g.shape

    def body(g_ref, o_ref):
        acc = g_ref[0]
        for k in range(1, nd):
            acc = acc + g_ref[k]
        o_ref[...] = acc

    return pl.pallas_call(
        body, name=name, grid=(r // 8,),
        in_specs=[pl.BlockSpec((nd, 8, n), lambda i: (0, i, 0))],
        out_specs=pl.BlockSpec((8, n), lambda i: (i, 0)),
        out_shape=jax.ShapeDtypeStruct((r, n), F32),
        compiler_params=_cparams(("arbitrary",)),
    )(g)


def _adamw(w, g, m, v, name):
    r, n = w.shape
    tr = 128 if r % 128 == 0 else 8
    bc1 = 1.0 - ADAM_B1 ** ADAM_STEP
    bc2 = 1.0 - ADAM_B2 ** ADAM_STEP

    def body(w_ref, g_ref, m_ref, v_ref, d_ref, mo_ref, vo_ref):
        gv = g_ref[...]
        mn = ADAM_B1 * m_ref[...] + (1.0 - ADAM_B1) * gv
        vn = ADAM_B2 * v_ref[...] + (1.0 - ADAM_B2) * (gv * gv)
        d_ref[...] = -ADAM_LR * ((mn / bc1) / (jnp.sqrt(vn / bc2) + ADAM_EPS) + ADAM_WD * w_ref[...])
        mo_ref[...] = mn
        vo_ref[...] = vn

    blk = pl.BlockSpec((tr, n), lambda i: (i, 0))
    shp = jax.ShapeDtypeStruct((r, n), F32)
    return pl.pallas_call(
        body, name=name, grid=(r // tr,), in_specs=[blk] * 4, out_specs=[blk] * 3, out_shape=[shp] * 3,
        compiler_params=_cparams(("arbitrary",)),
    )(w, g, m, v)


ROW = 1024
PACK_ROWS = 7168
BIG = ("fox_w_in", "fox_w_out", "sgu_w_in", "sgu_w_out", "ffn_w_up", "ffn_w_down")
SMALL_SHARDED = ("sgu_b_in", "sgu_v_gain", "sgu_v_bias", "ffn_conv_w")
SMALL_REPL = ("fox_b_f", "fox_q_gain", "fox_k_gain", "sgu_w_s", "sgu_b_s", "ffn_conv_b", "ada_b",
              "norm1_g", "norm2_g", "final_g")
WEIGHTS = ("fox_w_in", "fox_b_f", "fox_q_gain", "fox_k_gain", "fox_w_out", "sgu_w_in", "sgu_b_in", "sgu_v_gain",
           "sgu_v_bias", "sgu_w_s", "sgu_b_s", "sgu_w_out", "ffn_w_up", "ffn_conv_w", "ffn_conv_b", "ffn_w_down",
           "ada_w", "ada_b", "norm1_g", "norm2_g", "final_g")


def _rows_of(a, mult=1):
    flat = a.reshape(-1)
    rows = -(-flat.shape[0] // ROW)
    rows = -(-rows // mult) * mult
    return jnp.pad(flat, (0, rows * ROW - flat.shape[0])).reshape(rows, ROW)


def _pack(parts, mult, total=None):
    p = jnp.concatenate([_rows_of(a, mult) for a in parts], axis=0)
    if total is not None:
        p = jnp.pad(p, ((0, total - p.shape[0]), (0, 0)))
    return p


def _unpack(pack, shapes, mult):
    out, r0 = [], 0
    for shp in shapes:
        size = int(np.prod(shp))
        rows = -(-(-(-size // ROW)) // mult) * mult
        out.append(pack[r0:r0 + rows].reshape(-1)[:size].reshape(shp))
        r0 += rows
    return out


def _big_parts(t, j):
    return [t["fox_w_in"][:, 1028 * j:1028 * (j + 1)], t["fox_w_out"][256 * j:256 * (j + 1)],
            t["sgu_w_in"][:, 1024 * j:1024 * (j + 1)], t["sgu_w_out"][512 * j:512 * (j + 1)],
            t["ffn_w_up"][:, :, 1408 * j:1408 * (j + 1)], t["ffn_w_down"][:, 704 * j:704 * (j + 1)]]


BIG_SHARD_SHAPES = ((1024, 1028), (256, 1024), (1024, 1024), (512, 1024), (2, 1024, 1408), (2, 704, 1024))


def _full_from_shards(packs):
    per = [_unpack(packs[j], BIG_SHARD_SHAPES, 16) for j in range(N_CHIP)]
    cat = lambda i, axis: jnp.concatenate([per[j][i] for j in range(N_CHIP)], axis=axis)
    return dict(fox_w_in=jnp.pad(cat(0, 1), ((0, 0), (0, FOX_NP - FOX_N))), fox_w_out=cat(1, 0),
                sgu_w_in=cat(2, 1), sgu_w_out=cat(3, 0), ffn_w_up=_interleave_cols(cat(4, 2)), ffn_w_down=cat(5, 1))


def kernel(x, c, fox_w_in, fox_b_f, fox_q_gain, fox_k_gain, fox_w_out, sgu_w_in, sgu_b_in, sgu_v_gain, sgu_v_bias, sgu_w_s, sgu_b_s, sgu_w_out, ffn_w_up, ffn_conv_w, ffn_conv_b, ffn_w_down, ada_w, ada_b, norm1_g, norm2_g, final_g, loss_target, m_fox_w_in, m_fox_b_f, m_fox_q_gain, m_fox_k_gain, m_fox_w_out, m_sgu_w_in, m_sgu_b_in, m_sgu_v_gain, m_sgu_v_bias, m_sgu_w_s, m_sgu_b_s, m_sgu_w_out, m_ffn_w_up, m_ffn_conv_w, m_ffn_conv_b, m_ffn_w_down, m_ada_w, m_ada_b, m_norm1_g, m_norm2_g, m_final_g, v_fox_w_in, v_fox_b_f, v_fox_q_gain, v_fox_k_gain, v_fox_w_out, v_sgu_w_in, v_sgu_b_in, v_sgu_v_gain, v_sgu_v_bias, v_sgu_w_s, v_sgu_b_s, v_sgu_w_out, v_ffn_w_up, v_ffn_conv_w, v_ffn_conv_b, v_ffn_w_down, v_ada_w, v_ada_b, v_norm1_g, v_norm2_g, v_final_g):
    w = dict(fox_w_in=fox_w_in, fox_b_f=fox_b_f, fox_q_gain=fox_q_gain, fox_k_gain=fox_k_gain, fox_w_out=fox_w_out,
             sgu_w_in=sgu_w_in, sgu_b_in=sgu_b_in, sgu_v_gain=sgu_v_gain, sgu_v_bias=sgu_v_bias, sgu_w_s=sgu_w_s,
             sgu_b_s=sgu_b_s, sgu_w_out=sgu_w_out, ffn_w_up=ffn_w_up, ffn_conv_w=ffn_conv_w, ffn_conv_b=ffn_conv_b,
             ffn_w_down=ffn_w_down, ada_w=ada_w, ada_b=ada_b, norm1_g=norm1_g, norm2_g=norm2_g, final_g=final_g)
    mom = dict(fox_w_in=m_fox_w_in, fox_b_f=m_fox_b_f, fox_q_gain=m_fox_q_gain, fox_k_gain=m_fox_k_gain,
               fox_w_out=m_fox_w_out, sgu_w_in=m_sgu_w_in, sgu_b_in=m_sgu_b_in, sgu_v_gain=m_sgu_v_gain,
               sgu_v_bias=m_sgu_v_bias, sgu_w_s=m_sgu_w_s, sgu_b_s=m_sgu_b_s, sgu_w_out=m_sgu_w_out,
               ffn_w_up=m_ffn_w_up, ffn_conv_w=m_ffn_conv_w, ffn_conv_b=m_ffn_conv_b, ffn_w_down=m_ffn_w_down,
               ada_w=m_ada_w, ada_b=m_ada_b, norm1_g=m_norm1_g, norm2_g=m_norm2_g, final_g=m_final_g)
    var = dict(fox_w_in=v_fox_w_in, fox_b_f=v_fox_b_f, fox_q_gain=v_fox_q_gain, fox_k_gain=v_fox_k_gain,
               fox_w_out=v_fox_w_out, sgu_w_in=v_sgu_w_in, sgu_b_in=v_sgu_b_in, sgu_v_gain=v_sgu_v_gain,
               sgu_v_bias=v_sgu_v_bias, sgu_w_s=v_sgu_w_s, sgu_b_s=v_sgu_b_s, sgu_w_out=v_sgu_w_out,
               ffn_w_up=v_ffn_w_up, ffn_conv_w=v_ffn_conv_w, ffn_conv_b=v_ffn_conv_b, ffn_w_down=v_ffn_w_down,
               ada_w=v_ada_w, ada_b=v_ada_b, norm1_g=v_norm1_g, norm2_g=v_norm2_g, final_g=v_final_g)

    ax, ay, ac = _mesh_pos()
    chip = 2 * ax + ay
    dev = 2 * chip + ac

    small_shard_shapes = tuple(w[n].shape for n in SMALL_SHARDED)
    blk = _pack([c] + [w[n] for n in SMALL_SHARDED], 1, 16)
    gat = _allgather8(blk, "gather_small").reshape(N_DEV, 16, ROW)
    c_all = gat[:, 0, :]
    per_chip = [_unpack(gat[2 * j, 1:], small_shard_shapes, 1) for j in range(N_CHIP)]
    full_small = {n: jnp.concatenate([per_chip[j][i] for j in range(N_CHIP)], axis=-1)
                  for i, n in enumerate(SMALL_SHARDED)}

    shard = dict(fox_w_in=fox_w_in[0], fox_w_out=fox_w_out[0], sgu_w_in=sgu_w_in[0], sgu_w_out=sgu_w_out[0],
                 ffn_w_up=ffn_w_up, ffn_w_down=ffn_w_down)
    my_pack = _pack([shard[n] for n in BIG], 16, PACK_ROWS).astype(BF16)
    wts = _full_from_shards(_gather_shards(my_pack, "gather_weights"))

    da = ada_w.shape[2]
    ada_b_cols = lax.dynamic_slice_in_dim(ada_b, chip * da, da, axis=1)[:, None, :]
    mod_cols, c_act = _ada_mod(c_all, ada_w, ada_b_cols)
    mod_all = _allgather8(mod_cols.reshape(-1, ROW), "gather_mod").reshape(N_DEV, 2, N_DEV, da)
    mod_mine = lax.dynamic_index_in_dim(mod_all[0::2], dev, axis=2, keepdims=False)
    mod = jnp.swapaxes(mod_mine, 0, 1).reshape(2, N_CHIP * da)

    small = dict(norm1_g=norm1_g, norm2_g=norm2_g, final_g=final_g[None], fox_q_gain=fox_q_gain,
                 fox_k_gain=fox_k_gain, fox_b_f=fox_b_f, sgu_b_in=full_small["sgu_b_in"],
                 sgu_v_gain=full_small["sgu_v_gain"], sgu_v_bias=full_small["sgu_v_bias"], sgu_w_s=sgu_w_s[0],
                 sgu_b_s=sgu_b_s[0], conv_w_il=_interleave_cols(full_small["ffn_conv_w"]),
                 conv_b_il=_interleave_cols(ffn_conv_b))
    loss_dev, dx, g, dmod = _local_step(x[0], loss_target[0], mod, wts, small)

    g["ada_b"] = dmod
    small_names = ("ada_b",) + SMALL_SHARDED + tuple(n for n in SMALL_REPL if n != "ada_b")
    gs = _pack([g[n] for n in small_names], 1)
    rows_s = -(-gs.shape[0] // 8) * 8
    gs = jnp.pad(gs, ((0, rows_s - gs.shape[0]), (0, 0)))
    gs_all = _allgather8(gs, "gather_small_grads").reshape(N_DEV, rows_s, ROW)
    gsum = _sum8(gs_all, "sum_small_grads")
    full_shapes = {n: w[n].shape for n in SMALL_REPL}
    full_shapes.update({n: w[n].shape[:-1] + (w[n].shape[-1] * N_CHIP,) for n in SMALL_SHARDED})
    gfull = dict(zip(small_names, _unpack(gsum, [full_shapes[n] for n in small_names], 1)))
    grads = {n: gfull[n] for n in SMALL_REPL}
    for n in SMALL_SHARDED:
        width = w[n].shape[-1]
        grads[n] = lax.dynamic_slice_in_dim(gfull[n], chip * width, width, axis=gfull[n].ndim - 1)
    dmod_all = gs_all[:, :12, :].reshape(N_DEV, 2, N_CHIP * da)
    dmod_cols = jnp.swapaxes(lax.dynamic_slice_in_dim(dmod_all, chip * da, da, axis=2), 0, 1)
    grads["ada_w"] = _ada_w_grad(c_act.T, dmod_cols)

    gfull_big = dict(fox_w_in=g["fox_w_in"], fox_w_out=g["fox_w_out"], sgu_w_in=g["sgu_w_in"],
                     sgu_w_out=g["sgu_w_out"], ffn_w_up=g["ffn_w_up"], ffn_w_down=g["ffn_w_down"])
    gpack = jnp.stack([_pack(_big_parts(gfull_big, j), 16, PACK_ROWS) for j in range(N_CHIP)]).astype(BF16)
    c_arr = jnp.reshape(ac, (1,)).astype(jnp.int32)
    me_arr = jnp.reshape(chip, (1,)).astype(jnp.int32)
    sib = _rs_to_sibling(gpack, "rs_sibling")
    cs = _rs_chip_sum(gpack, sib, c_arr, 512, "rs_chip_sum")
    rcv = _rs_across_chips(cs, "rs_chips")
    red_half = _rs_final_sum(cs, rcv, me_arr, 512, "rs_final_sum")
    red = _rs_join_halves(red_half, "rs_join")
    for n, gv in zip(BIG, _unpack(red, BIG_SHARD_SHAPES, 16)):
        grads[n] = gv.reshape(w[n].shape)

    delta, new_m, new_v = {}, {}, {}
    for n in BIG + ("ada_w",):
        shp = w[n].shape
        two_d = lambda a: a.reshape(-1, shp[-1])
        d_, m_, v_ = _adamw(two_d(w[n]), two_d(grads[n]), two_d(mom[n]), two_d(var[n]), "adamw_" + n)
        delta[n], new_m[n], new_v[n] = d_.reshape(shp), m_.reshape(shp), v_.reshape(shp)
    rest = SMALL_SHARDED + SMALL_REPL
    packs = [_pack([t[n] for n in rest], 1) for t in (w, grads, mom, var)]
    rows_r = -(-packs[0].shape[0] // 8) * 8
    packs = [jnp.pad(p, ((0, rows_r - p.shape[0]), (0, 0))) for p in packs]
    outs = _adamw(*packs, "adamw_small")
    for t, o in zip((delta, new_m, new_v), outs):
        t.update(zip(rest, _unpack(o, [w[n].shape for n in rest], 1)))

    loss = lax.psum(loss_dev, ("x", "y", "c"))
    return (loss, dx[None], *[grads[n].reshape(w[n].shape) for n in WEIGHTS], *[delta[n] for n in WEIGHTS],
            *[new_m[n] for n in WEIGHTS], *[new_v[n] for n in WEIGHTS])
```

```python
import functools
import math

import numpy as np
import jax
import jax.numpy as jnp
from jax import lax
from jax.experimental import pallas as pl
from jax.experimental.pallas import tpu as pltpu

F32 = jnp.float32
BF16 = jnp.bfloat16
MESH = pl.DeviceIdType.MESH

D = 1024
H = 16
DH = 64
NP = H // 2
LANES = 128
DFF = 2816
SGW = 2048
SGG = 8
SGC = 256
SGB = 128
CHUNK = 64
EPS = 1e-6
FOX_N = 4 * D + H
FOX_NP = 4224
GT = 256
NGT = DFF // GT
SCALE = DH ** -0.5
LOG2E = 1.4426950408889634

ADAM_LR = 0.001
ADAM_B1 = 0.9
ADAM_B2 = 0.999
ADAM_EPS = 1e-08
ADAM_WD = 0.01
ADAM_STEP = 10

V7X_VMEM_LIMIT = 56 * 1024 * 1024

L_F = 64
L_NF = 67
L_LSE = 70


def _cparams(sem=None):
    return pltpu.CompilerParams(dimension_semantics=sem, vmem_limit_bytes=V7X_VMEM_LIMIT)


def _split3(x):
    hi = x.astype(BF16)
    r = x - hi.astype(F32)
    mid = r.astype(BF16)
    lo = (r - mid.astype(F32)).astype(BF16)
    return hi, mid, lo


def _dot(a, b, dims=(((1,), (0,)), ((), ()))):
    return lax.dot_general(a, b, dims, preferred_element_type=F32)


def _dot_nt(a, b):
    return _dot(a, b, (((1,), (1,)), ((), ())))


def _dot_tn(a, b):
    return _dot(a, b, (((0,), (0,)), ((), ())))


def _exact_dot(m_bf16, x_f32):
    hi, mid, lo = _split3(x_f32)
    return _dot(m_bf16, hi) + _dot(m_bf16, mid) + _dot(m_bf16, lo)


def _exact_dot_r(x_f32, m_bf16):
    hi, mid, lo = _split3(x_f32)
    return _dot(hi, m_bf16) + _dot(mid, m_bf16) + _dot(lo, m_bf16)


def _head_block_ones():
    r = lax.broadcasted_iota(jnp.int32, (LANES, LANES), 0) // DH
    c = lax.broadcasted_iota(jnp.int32, (LANES, LANES), 1) // DH
    return (r == c).astype(BF16)


def _sigmoid(x):
    return 1.0 / (1.0 + jnp.exp(-x))


def _gelu(x):
    c = math.sqrt(2.0 / math.pi)
    return 0.5 * x * (1.0 + jnp.tanh(c * (x + 0.044715 * (x * x * x))))


def _gelu_grad(x):
    c = math.sqrt(2.0 / math.pi)
    t = jnp.tanh(c * (x + 0.044715 * (x * x * x)))
    return 0.5 * (1.0 + t) + 0.5 * x * (1.0 - t * t) * c * (1.0 + 3 * 0.044715 * (x * x))


def _rstd_rows(x):
    return lax.rsqrt(jnp.mean(x * x, axis=-1, keepdims=True) + EPS)


def _norm_mod_matmul(x, ng, sc, sh, w, bias, out_dtype, ts, tn, name, planes=1):
    s, d = x.shape
    n = w.shape[1]
    nc = n // planes

    def body(x_ref, ng_ref, sc_ref, sh_ref, w_ref, b_ref, o_ref, h_ref):
        xv = x_ref[...]
        h = (xv * _rstd_rows(xv) * ng_ref[...] * (1.0 + sc_ref[...]) + sh_ref[...]).astype(BF16)
        h_ref[...] = h
        for e in range(planes):
            for c0 in range(0, nc, tn):
                cols = slice(e * nc + c0, e * nc + c0 + tn)
                val = (_dot(h, w_ref[:, cols]) + b_ref[:, cols]).astype(out_dtype)
                if planes == 1:
                    o_ref[:, c0:c0 + tn] = val
                else:
                    o_ref[e, :, c0:c0 + tn] = val

    vec = pl.BlockSpec((1, d), lambda i: (0, 0))
    if planes == 1:
        o_spec, o_shape = pl.BlockSpec((ts, n), lambda i: (i, 0)), (s, n)
    else:
        o_spec, o_shape = pl.BlockSpec((planes, ts, nc), lambda i: (0, i, 0)), (planes, s, nc)
    return pl.pallas_call(
        body, name=name, grid=(s // ts,),
        in_specs=[pl.BlockSpec((ts, d), lambda i: (i, 0)), vec, vec, vec,
                  pl.BlockSpec((d, n), lambda i: (0, 0)), pl.BlockSpec((1, n), lambda i: (0, 0))],
        out_specs=[o_spec, pl.BlockSpec((ts, d), lambda i: (i, 0))],
        out_shape=[jax.ShapeDtypeStruct(o_shape, out_dtype), jax.ShapeDtypeStruct((s, d), BF16)],
        compiler_params=_cparams(("arbitrary",)),
    )(x, ng, sc, sh, w, bias)


def _matmul(a, b, ta, tb, tm, tn, tk, out_dtype, name):
    planes = a.shape[0] if a.ndim == 3 else 1
    if a.ndim == 3:
        m, k = a.shape[1], planes * a.shape[2]
    else:
        m, k = (a.shape[1], a.shape[0]) if ta else a.shape
    n = b.shape[0] if tb else b.shape[1]
    nk = k // tk
    nkp = nk // planes
    dims = (((0,) if ta else (1,), (1,) if tb else (0,)), ((), ()))

    def body(a_ref, b_ref, o_ref, acc):
        kk = pl.program_id(2)

        @pl.when(kk == 0)
        def _():
            acc[...] = jnp.zeros_like(acc)
        acc[...] += _dot(a_ref[...], b_ref[...], dims)

        @pl.when(kk == nk - 1)
        def _():
            o_ref[...] = acc[...].astype(out_dtype)

    if a.ndim == 3:
        a_spec = pl.BlockSpec((None, tm, tk), lambda i, j, kk: (kk // nkp, i, kk % nkp))
    else:
        a_spec = (pl.BlockSpec((tk, tm), lambda i, j, kk: (kk, i)) if ta
                  else pl.BlockSpec((tm, tk), lambda i, j, kk: (i, kk)))
    b_spec = (pl.BlockSpec((tn, tk), lambda i, j, kk: (j, kk)) if tb
              else pl.BlockSpec((tk, tn), lambda i, j, kk: (kk, j)))
    return pl.pallas_call(
        body, name=name, grid=(m // tm, n // tn, nk),
        in_specs=[a_spec, b_spec],
        out_specs=pl.BlockSpec((tm, tn), lambda i, j, kk: (i, j)),
        out_shape=jax.ShapeDtypeStruct((m, n), out_dtype),
        scratch_shapes=[pltpu.VMEM((tm, tn), F32)],
        compiler_params=_cparams(("arbitrary", "arbitrary", "arbitrary")),
    )(a, b)


def _matmul_residual(a, w, xin, g, ts, name):
    s, k = a.shape
    d = w.shape[1]

    def body(a_ref, w_ref, x_ref, g_ref, o_ref, y_ref):
        y = _dot(a_ref[...], w_ref[...])
        o_ref[...] = x_ref[...] + g_ref[...] * y
        y_ref[...] = y.astype(BF16)

    return pl.pallas_call(
        body, name=name, grid=(s // ts,),
        in_specs=[pl.BlockSpec((ts, k), lambda i: (i, 0)),
                  pl.BlockSpec((k, d), lambda i: (0, 0)),
                  pl.BlockSpec((ts, d), lambda i: (i, 0)),
                  pl.BlockSpec((1, d), lambda i: (0, 0))],
        out_specs=[pl.BlockSpec((ts, d), lambda i: (i, 0)), pl.BlockSpec((ts, d), lambda i: (i, 0))],
        out_shape=[jax.ShapeDtypeStruct((s, d), F32), jax.ShapeDtypeStruct((s, d), BF16)],
        compiler_params=_cparams(("arbitrary",)),
    )(a, w, xin, g)


def _lane(shape):
    return lax.broadcasted_iota(jnp.int32, shape, 1)


def _pair_norm(x, gain2, bones):
    msq = _exact_dot_r(x * x, bones) * (1.0 / DH)
    r = lax.rsqrt(msq + EPS)
    xh = x * r
    return xh * gain2, xh, r


def _fox_post(proj, qg2, kg2, bf, ts, name):
    s = proj.shape[0]

    def body(p_ref, qg_ref, kg_ref, bf_ref, q_ref, k_ref, v_ref, carry):
        @pl.when(pl.program_id(0) == 0)
        def _():
            carry[...] = jnp.zeros_like(carry)
        lane = _lane((ts, LANES))
        bones = _head_block_ones()
        xf = p_ref[:, 4 * D:4 * D + LANES] + bf_ref[...]
        logf = jnp.minimum(xf, 0.0) - jnp.log(1.0 + jnp.exp(-jnp.abs(xf)))
        logf = jnp.where(lane < H, logf, 0.0)
        rr = lax.broadcasted_iota(jnp.int32, (ts, ts), 0)
        cc = lax.broadcasted_iota(jnp.int32, (ts, ts), 1)
        ltri = (cc <= rr).astype(BF16)
        fcum = _exact_dot(ltri, logf) + carry[0:1, :]
        carry[0:1, :] = fcum[ts - 1:ts, :]
        fhi, fmid, flo = _split3(fcum * LOG2E)
        fhi, fmid, flo = fhi.astype(F32), fmid.astype(F32), flo.astype(F32)
        one_q = ((lane >= L_NF) & (lane < L_NF + 3)).astype(F32)
        one_k = (((lane >= L_F) & (lane < L_F + 3)) | ((lane >= L_LSE) & (lane < L_LSE + 3))).astype(F32)
        one_v = ((lane >= L_F) & (lane < L_F + 3)).astype(F32)
        for p in range(NP):
            qn, _, _ = _pair_norm(p_ref[:, p * LANES:(p + 1) * LANES], qg_ref[...], bones)
            kn, _, _ = _pair_norm(p_ref[:, D + p * LANES:D + (p + 1) * LANES], kg_ref[...], bones)
            vv = p_ref[:, 2 * D + p * LANES:2 * D + (p + 1) * LANES]
            qn = qn * (SCALE * LOG2E)
            for e in range(2):
                h = 2 * p + e
                if e == 1:
                    qe, ke, ve = (pltpu.roll(t, DH, axis=1) for t in (qn, kn, vv))
                else:
                    qe, ke, ve = qn, kn, vv
                f0, f1, f2 = fhi[:, h:h + 1], fmid[:, h:h + 1], flo[:, h:h + 1]
                fq = jnp.where(lane == L_F, f0, jnp.where(lane == L_F + 1, f1, jnp.where(lane == L_F + 2, f2, one_q)))
                fk = jnp.where(lane == L_NF, -f0, jnp.where(lane == L_NF + 1, -f1, jnp.where(lane == L_NF + 2, -f2, one_k)))
                q_ref[h] = jnp.where(lane < DH, qe, fq).astype(BF16)
                k_ref[h] = jnp.where(lane < DH, ke, fk).astype(BF16)
                v_ref[h] = jnp.where(lane < DH, ve, one_v).astype(BF16)

    hs = pl.BlockSpec((H, ts, LANES), lambda i: (0, i, 0))
    vec = pl.BlockSpec((1, LANES), lambda i: (0, 0))
    shp = jax.ShapeDtypeStruct((H, s, LANES), BF16)
    return pl.pallas_call(
        body, name=name, grid=(s // ts,),
        in_specs=[pl.BlockSpec((ts, FOX_NP), lambda i: (i, 0)), vec, vec, vec],
        out_specs=[hs, hs, hs], out_shape=[shp, shp, shp],
        scratch_shapes=[pltpu.VMEM((8, LANES), F32)],
        compiler_params=_cparams(("arbitrary",)),
    )(proj, qg2, kg2, bf)


def _attn_fwd(qa, ka, va, tq, name):
    s = qa.shape[1]
    nq = s // tq

    def body(q_ref, k_ref, v_ref, o_ref, ql_ref):
        i = pl.program_id(1)
        lane = _lane((tq, LANES))
        qs_ = [q_ref[0], q_ref[1]]

        def step(j, carry, masked):
            off = pl.multiple_of(j * tq, tq)
            new = []
            for e in range(2):
                m, acc = carry[e]
                kb = k_ref[e, pl.ds(off, tq), :]
                vb = v_ref[e, pl.ds(off, tq), :]
                sc = _dot_nt(qs_[e], kb)
                if masked:
                    rr = lax.broadcasted_iota(jnp.int32, (tq, tq), 0)
                    cc = lax.broadcasted_iota(jnp.int32, (tq, tq), 1)
                    sc = jnp.where(cc <= rr, sc, -jnp.inf)
                m_new = jnp.maximum(m, jnp.max(sc, axis=-1, keepdims=True))
                pr = jnp.exp2(sc - m_new)
                acc = acc * jnp.exp2(m - m_new) + _dot(pr.astype(BF16), vb)
                new.append((m_new, acc))
            return tuple(new)

        one = (jnp.full((tq, 1), -jnp.inf, F32), jnp.zeros((tq, LANES), F32))
        carry = lax.fori_loop(0, i, functools.partial(step, masked=False), (one, one))
        carry = step(i, carry, True)
        outs = []
        for e in range(2):
            m, acc = carry[e]
            l = acc[:, L_F:L_F + 1]
            outs.append(acc / l)
            lse = m + jnp.log2(l)
            h0, h1, h2 = _split3(-lse)
            ql = jnp.where(lane == L_LSE, h0.astype(F32),
                           jnp.where(lane == L_LSE + 1, h1.astype(F32),
                                     jnp.where(lane == L_LSE + 2, h2.astype(F32), qs_[e].astype(F32))))
            ql_ref[e] = ql.astype(BF16)
        o_ref[...] = jnp.where(lane < DH, outs[0], pltpu.roll(outs[1], DH, axis=1))

    res = pl.BlockSpec((2, s, LANES), lambda p, i: (p, 0, 0))
    qs = pl.BlockSpec((2, tq, LANES), lambda p, i: (p, i, 0))
    return pl.pallas_call(
        body, name=name, grid=(NP, nq),
        in_specs=[qs, res, res],
        out_specs=[pl.BlockSpec((tq, LANES), lambda p, i: (i, p)), qs],
        out_shape=[jax.ShapeDtypeStruct((s, D), F32), jax.ShapeDtypeStruct((H, s, LANES), BF16)],
        compiler_params=_cparams(("arbitrary", "arbitrary")),
    )(qa, ka, va)


def _attn_bwd(ql, ka, va, doa, tq, name):
    s = ql.shape[1]
    nq = s // tq

    def body(q_ref, k_ref, v_ref, do_ref, dqo_ref, dk_ref, dv_ref, dq_ref):
        dq_ref[...] = jnp.zeros_like(dq_ref)
        lane = _lane((tq, LANES))

        def kv_block(j, _):
            joff = pl.multiple_of(j * tq, tq)
            kb = k_ref[0, pl.ds(joff, tq), :]
            vb = v_ref[0, pl.ds(joff, tq), :]

            def step(i, carry, masked):
                dk, dv = carry
                ioff = pl.multiple_of(i * tq, tq)
                qb = q_ref[0, pl.ds(ioff, tq), :]
                dob = do_ref[0, pl.ds(ioff, tq), :]
                pr = jnp.exp2(_dot_nt(qb, kb))
                if masked:
                    rr = lax.broadcasted_iota(jnp.int32, (tq, tq), 0)
                    cc = lax.broadcasted_iota(jnp.int32, (tq, tq), 1)
                    pr = jnp.where(cc <= rr, pr, 0.0)
                ds = (pr * _dot_nt(dob, vb)).astype(BF16)
                dv = dv + _dot_tn(pr.astype(BF16), dob)
                dk = dk + _dot_tn(ds, qb)
                dq_ref[pl.ds(ioff, tq), :] += _dot(ds, kb)
                return dk, dv

            zero = jnp.zeros((tq, LANES), F32)
            carry = step(j, (zero, zero), True)
            dk, dv = lax.fori_loop(j + 1, nq, functools.partial(step, masked=False), carry)
            col = dk[:, L_NF:L_NF + 1]
            hi = col.astype(BF16).astype(F32)
            dk = jnp.where(lane == L_NF, hi, jnp.where(lane == L_NF + 1, col - hi, dk))
            dk_ref[0, pl.ds(joff, tq), :] = dk.astype(BF16)
            dv_ref[0, pl.ds(joff, tq), :] = dv.astype(BF16)
            return 0

        lax.fori_loop(0, nq, kv_block, 0)
        dq = dq_ref[...]
        lane_s = _lane((s, LANES))
        col = dq[:, L_F:L_F + 1]
        hi = col.astype(BF16).astype(F32)
        dqo_ref[0] = jnp.where(lane_s == L_F, hi, jnp.where(lane_s == L_F + 1, col - hi, dq)).astype(BF16)

    hs = pl.BlockSpec((1, s, LANES), lambda h: (h, 0, 0))
    shp = jax.ShapeDtypeStruct((H, s, LANES), BF16)
    return pl.pallas_call(
        body, name=name, grid=(H,),
        in_specs=[hs, hs, hs, hs], out_specs=[hs, hs, hs], out_shape=[shp, shp, shp],
        scratch_shapes=[pltpu.VMEM((s, LANES), F32)],
        compiler_params=_cparams(("arbitrary",)),
    )(ql, ka, va, doa)


def _gate(att, proj, ts, name):
    s = att.shape[0]

    def body(a_ref, o_ref, g_ref):
        g_ref[...] = (a_ref[...] * _sigmoid(o_ref[...])).astype(BF16)

    return pl.pallas_call(
        body, name=name, grid=(s // ts,),
        in_specs=[pl.BlockSpec((ts, D), lambda i: (i, 0)), pl.BlockSpec((ts, D), lambda i: (i, 3))],
        out_specs=pl.BlockSpec((ts, D), lambda i: (i, 0)),
        out_shape=jax.ShapeDtypeStruct((s, D), BF16),
        compiler_params=_cparams(("arbitrary",)),
    )(att, proj)


def _attn_bwd_prep(dgated, att, proj, ts, name):
    s = att.shape[0]

    def body(dg_ref, a_ref, o_ref, doa_ref, dop_ref):
        lane = _lane((ts, LANES))
        bones = _head_block_ones()
        for p in range(NP):
            sl = slice(p * LANES, (p + 1) * LANES)
            dg, a = dg_ref[:, sl], a_ref[:, sl]
            sig = _sigmoid(o_ref[:, sl])
            datt = dg * sig
            dop_ref[:, sl] = (dg * a * sig * (1.0 - sig)).astype(BF16)
            delta = _exact_dot_r(datt * a, bones)
            for e in range(2):
                de, dl = (datt, delta) if e == 0 else (pltpu.roll(datt, DH, axis=1), pltpu.roll(delta, DH, axis=1))
                h0, h1, h2 = _split3(-dl[:, 0:1])
                aug = jnp.where(lane == L_F, h0.astype(F32),
                                jnp.where(lane == L_F + 1, h1.astype(F32),
                                          jnp.where(lane == L_F + 2, h2.astype(F32), 0.0)))
                doa_ref[2 * p + e] = jnp.where(lane < DH, de, aug).astype(BF16)

    row = pl.BlockSpec((ts, D), lambda i: (i, 0))
    return pl.pallas_call(
        body, name=name, grid=(s // ts,),
        in_specs=[row, row, pl.BlockSpec((ts, D), lambda i: (i, 3))],
        out_specs=[pl.BlockSpec((H, ts, LANES), lambda i: (0, i, 0)), row],
        out_shape=[jax.ShapeDtypeStruct((H, s, LANES), BF16), jax.ShapeDtypeStruct((s, D), BF16)],
        compiler_params=_cparams(("arbitrary",)),
    )(dgated, att, proj)


def _fox_post_bwd(proj, dqa, dka, dva, dop, qg2, kg2, bf, ts, name):
    s = proj.shape[0]
    nt = s // ts

    def body(p_ref, dq_ref, dk_ref, dv_ref, dop_ref, qg_ref, kg_ref, bf_ref, o_ref, red_ref, carry):
        @pl.when(pl.program_id(0) == 0)
        def _():
            carry[...] = jnp.zeros_like(carry)
            red_ref[...] = jnp.zeros_like(red_ref)
        lane = _lane((ts, LANES))
        bones = _head_block_ones()
        d_f = jnp.zeros((ts, LANES), F32)
        dqg = jnp.zeros((1, LANES), F32)
        dkg = jnp.zeros((1, LANES), F32)
        for p in range(NP):
            heads = [[ref[2 * p + e].astype(F32) for e in range(2)] for ref in (dq_ref, dk_ref, dv_ref)]
            pair = [jnp.where(lane < DH, a, pltpu.roll(b, DH, axis=1)) for a, b in heads]
            for e in range(2):
                dqe, dke = heads[0][e], heads[1][e]
                col = (dqe[:, L_F:L_F + 1] + dqe[:, L_F + 1:L_F + 2]
                       - dke[:, L_NF:L_NF + 1] - dke[:, L_NF + 1:L_NF + 2])
                d_f = jnp.where(lane == 2 * p + e, col, d_f)
            for idx, (g_ref, base) in enumerate(((qg_ref, 0), (kg_ref, D))):
                x = p_ref[:, base + p * LANES:base + (p + 1) * LANES]
                _, xh, r = _pair_norm(x, g_ref[...], bones)
                dn = pair[idx] * (SCALE if idx == 0 else 1.0 / LOG2E)
                t = dn * g_ref[...]
                mean_txh = _exact_dot_r(t * xh, bones) * (1.0 / DH)
                dx = r * (t - xh * mean_txh)
                o_ref[:, base + p * LANES:base + (p + 1) * LANES] = dx.astype(BF16)
                gsum = jnp.sum(dn * xh, axis=0, keepdims=True)
                if idx == 0:
                    dqg = dqg + gsum
                else:
                    dkg = dkg + gsum
            o_ref[:, 2 * D + p * LANES:2 * D + (p + 1) * LANES] = pair[2].astype(BF16)
        o_ref[:, 3 * D:4 * D] = dop_ref[...]
        rr = lax.broadcasted_iota(jnp.int32, (ts, ts), 0)
        cc = lax.broadcasted_iota(jnp.int32, (ts, ts), 1)
        utri = (cc >= rr).astype(BF16)
        dlogf = _exact_dot(utri, d_f) + carry[0:1, :]
        carry[0:1, :] = dlogf[0:1, :]
        xf = p_ref[:, 4 * D:4 * D + LANES] + bf_ref[...]
        dfl = jnp.where(lane < H, dlogf * _sigmoid(-xf), 0.0)
        o_ref[:, 4 * D:4 * D + LANES] = dfl.astype(BF16)
        red_ref[0:1, :] += dqg
        red_ref[1:2, :] += dkg
        red_ref[2:3, :] += jnp.sum(dfl, axis=0, keepdims=True)

    hs = pl.BlockSpec((H, ts, LANES), lambda i: (0, nt - 1 - i, 0))
    vec = pl.BlockSpec((1, LANES), lambda i: (0, 0))
    return pl.pallas_call(
        body, name=name, grid=(nt,),
        in_specs=[pl.BlockSpec((ts, FOX_NP), lambda i: (nt - 1 - i, 0)), hs, hs, hs,
                  pl.BlockSpec((ts, D), lambda i: (nt - 1 - i, 0)), vec, vec, vec],
        out_specs=[pl.BlockSpec((ts, FOX_NP), lambda i: (nt - 1 - i, 0)),
                   pl.BlockSpec((8, LANES), lambda i: (0, 0))],
        out_shape=[jax.ShapeDtypeStruct((s, FOX_NP), BF16), jax.ShapeDtypeStruct((8, LANES), F32)],
        scratch_shapes=[pltpu.VMEM((8, LANES), F32)],
        compiler_params=_cparams(("arbitrary",)),
    )(proj, dqa, dka, dva, dop, qg2, kg2, bf)


HALO = 16
TS = 512
TQ = 512
TR = 256
TP = 256


def _shift_down(x, k):
    return pltpu.roll(x, k, axis=0)


def _shift_up(x, k):
    return pltpu.roll(x, x.shape[0] - k, axis=0)


def _planes(ref):
    return jnp.concatenate([ref[0].astype(F32), ref[1].astype(F32)], axis=1)


def _conv_gate(a, cw, cb, ts, name):
    s = a.shape[1]
    hb = ts // HALO

    def body(prev_ref, a_ref, cw_ref, cb_ref, f_ref):
        i = pl.program_id(0)
        cwv, cbv = _planes(cw_ref), _planes(cb_ref)
        prev = jnp.where(i > 0, _planes(prev_ref), 0.0)
        ext = jnp.concatenate([prev, _planes(a_ref)], axis=0)
        ap = (_shift_down(ext, 2) * cwv[0:1, :] + _shift_down(ext, 1) * cwv[1:2, :]
              + ext * cwv[2:3, :] + cbv)[HALO:, :]
        g, val = ap[:, :GT], ap[:, GT:]
        f_ref[...] = (g * _sigmoid(g) * val).astype(BF16)

    return pl.pallas_call(
        body, name=name, grid=(s // ts, NGT),
        in_specs=[pl.BlockSpec((2, HALO, GT), lambda i, j: (0, jnp.maximum(i * hb - 1, 0), j)),
                  pl.BlockSpec((2, ts, GT), lambda i, j: (0, i, j)),
                  pl.BlockSpec((2, 8, GT), lambda i, j: (0, 0, j)),
                  pl.BlockSpec((2, 1, GT), lambda i, j: (0, 0, j))],
        out_specs=pl.BlockSpec((ts, GT), lambda i, j: (i, j)),
        out_shape=jax.ShapeDtypeStruct((s, DFF), BF16),
        compiler_params=_cparams(("arbitrary", "arbitrary")),
    )(a, a, cw, cb)


def _conv_gate_bwd(a, df, cw, cb, ts, name):
    s = a.shape[1]
    hb = ts // HALO
    nt = s // ts

    def body(prev_ref, a_ref, next_ref, df_ref, dfn_ref, cw_ref, cb_ref, da_ref, red_ref):
        i = pl.program_id(1)

        @pl.when(i == 0)
        def _():
            red_ref[...] = jnp.zeros_like(red_ref)
        cwv, cbv = _planes(cw_ref), _planes(cb_ref)
        prev = jnp.where(i > 0, _planes(prev_ref), 0.0)
        ext = jnp.concatenate([prev, _planes(a_ref), _planes(next_ref)], axis=0)
        dfn = jnp.where(i < nt - 1, dfn_ref[...].astype(F32), 0.0)
        dfe = jnp.concatenate([jnp.zeros((HALO, GT), F32), df_ref[...].astype(F32), dfn], axis=0)
        am2, am1 = _shift_down(ext, 2), _shift_down(ext, 1)
        ap = am2 * cwv[0:1, :] + am1 * cwv[1:2, :] + ext * cwv[2:3, :] + cbv
        g, val = ap[:, :GT], ap[:, GT:]
        sg = _sigmoid(g)
        dap = jnp.concatenate([dfe * val * (sg * (1.0 + g * (1.0 - sg))), dfe * (g * sg)], axis=1)
        da = dap * cwv[2:3, :] + _shift_up(dap, 1) * cwv[1:2, :] + _shift_up(dap, 2) * cwv[0:1, :]
        main = slice(HALO, HALO + ts)
        sums = [jnp.sum((t * dap)[main], axis=0, keepdims=True) for t in (am2, am1, ext)]
        sums.append(jnp.sum(dap[main], axis=0, keepdims=True))
        for e in range(2):
            cols = slice(e * GT, (e + 1) * GT)
            da_ref[e] = da[main, cols].astype(BF16)
            for r, sm in enumerate(sums):
                red_ref[e, r:r + 1, :] += sm[:, cols]

    nhb = s // HALO
    return pl.pallas_call(
        body, name=name, grid=(NGT, nt),
        in_specs=[pl.BlockSpec((2, HALO, GT), lambda j, i: (0, jnp.maximum(i * hb - 1, 0), j)),
                  pl.BlockSpec((2, ts, GT), lambda j, i: (0, i, j)),
                  pl.BlockSpec((2, HALO, GT), lambda j, i: (0, jnp.minimum((i + 1) * hb, nhb - 1), j)),
                  pl.BlockSpec((ts, GT), lambda j, i: (i, j)),
                  pl.BlockSpec((HALO, GT), lambda j, i: (jnp.minimum((i + 1) * hb, nhb - 1), j)),
                  pl.BlockSpec((2, 8, GT), lambda j, i: (0, 0, j)),
                  pl.BlockSpec((2, 1, GT), lambda j, i: (0, 0, j))],
        out_specs=[pl.BlockSpec((2, ts, GT), lambda j, i: (0, i, j)),
                   pl.BlockSpec((2, 8, GT), lambda j, i: (0, 0, j))],
        out_shape=[jax.ShapeDtypeStruct((2, s, DFF), BF16), jax.ShapeDtypeStruct((2, 8, DFF), F32)],
        compiler_params=_cparams(("arbitrary", "arbitrary")),
    )(a, a, a, df, df, cw, cb)


def _chunk_mask(transposed=False):
    t = lax.broadcasted_iota(jnp.int32, (SGB, SGB), 0) // CHUNK
    u = lax.broadcasted_iota(jnp.int32, (SGB, SGB), 1) // CHUNK
    return (t <= u) if transposed else (u <= t)


def _sgu_ln(zv, gain, bias):
    v = _gelu(zv)
    mu = jnp.mean(v, axis=-1, keepdims=True)
    vc = v - mu
    rstd = lax.rsqrt(jnp.mean(vc * vc, axis=-1, keepdims=True) + EPS)
    vhat = vc * rstd
    return vhat * gain + bias, vhat, rstd


def _sgu_fwd(z, vgain, vbias, ws, bst, tr, name):
    s = z.shape[0]

    def body(zu_ref, zv_ref, vg_ref, vb_ref, ws_ref, bs_ref, y_ref):
        u = _gelu(zu_ref[...].astype(F32))
        vn, _, _ = _sgu_ln(zv_ref[...].astype(F32), vg_ref[...], vb_ref[...])
        vn = vn.astype(BF16)
        mask = _chunk_mask()
        for g in range(SGG):
            w = jnp.where(mask, ws_ref[g], 0.0).astype(BF16)
            for b in range(tr // SGB):
                rs, cs = slice(b * SGB, (b + 1) * SGB), slice(g * SGC, (g + 1) * SGC)
                mixed = _dot(w, vn[rs, cs]) + bs_ref[:, g:g + 1]
                y_ref[rs, cs] = (u[rs, cs] * mixed).astype(BF16)

    vec = pl.BlockSpec((1, SGW), lambda i: (0, 0))
    return pl.pallas_call(
        body, name=name, grid=(s // tr,),
        in_specs=[pl.BlockSpec((tr, SGW), lambda i: (i, 0)), pl.BlockSpec((tr, SGW), lambda i: (i, 1)),
                  vec, vec, pl.BlockSpec((SGG, SGB, SGB), lambda i: (0, 0, 0)),
                  pl.BlockSpec((SGB, LANES), lambda i: (0, 0))],
        out_specs=pl.BlockSpec((tr, SGW), lambda i: (i, 0)),
        out_shape=jax.ShapeDtypeStruct((s, SGW), BF16),
        compiler_params=_cparams(("arbitrary",)),
    )(z, z, vgain, vbias, ws, bst)


def _sgu_bwd(z, dy, vgain, vbias, ws, wst, bst, tr, name):
    s = z.shape[0]

    def body(zu_ref, zv_ref, dy_ref, vg_ref, vb_ref, ws_ref, wst_ref, bs_ref,
             dz_ref, rb_ref, rv_ref, dws_ref, dbs_ref, dvn_s):
        @pl.when(pl.program_id(0) == 0)
        def _():
            rb_ref[...] = jnp.zeros_like(rb_ref)
            rv_ref[...] = jnp.zeros_like(rv_ref)
            dws_ref[...] = jnp.zeros_like(dws_ref)
            dbs_ref[...] = jnp.zeros_like(dbs_ref)
        zu = zu_ref[...].astype(F32)
        zv = zv_ref[...].astype(F32)
        u = _gelu(zu)
        vn, vhat, rstd = _sgu_ln(zv, vg_ref[...], vb_ref[...])
        vnb = vn.astype(BF16)
        dyv = dy_ref[...].astype(F32)
        dmix = (dyv * u).astype(BF16)
        mask = _chunk_mask()
        mask_t = _chunk_mask(transposed=True)
        lane = _lane((SGB, LANES))
        dbs = jnp.zeros((SGB, LANES), F32)
        for g in range(SGG):
            w = jnp.where(mask, ws_ref[g], 0.0).astype(BF16)
            wt = jnp.where(mask_t, wst_ref[g], 0.0).astype(BF16)
            dw = jnp.zeros((SGB, SGB), F32)
            for b in range(tr // SGB):
                rs, cs = slice(b * SGB, (b + 1) * SGB), slice(g * SGC, (g + 1) * SGC)
                mixed = _dot(w, vnb[rs, cs]) + bs_ref[:, g:g + 1]
                dz_ref[rs, cs] = (dyv[rs, cs] * mixed * _gelu_grad(zu[rs, cs])).astype(BF16)
                dm = dmix[rs, cs]
                dw = dw + _dot_nt(dm, vnb[rs, cs])
                dbs = dbs + jnp.where(lane == g, jnp.sum(dm.astype(F32), axis=-1, keepdims=True), 0.0)
                dvn_s[rs, cs] = _dot(wt, dm)
            dws_ref[g] += jnp.where(mask, dw, 0.0)
        dbs_ref[...] += dbs
        dvn = dvn_s[...]
        rv_ref[0:1, :] += jnp.sum(dvn * vhat, axis=0, keepdims=True)
        rv_ref[1:2, :] += jnp.sum(dvn, axis=0, keepdims=True)
        dvh = dvn * vg_ref[...]
        dv = rstd * (dvh - jnp.mean(dvh, axis=-1, keepdims=True)
                     - vhat * jnp.mean(dvh * vhat, axis=-1, keepdims=True))
        dz_ref[:, SGW:] = (dv * _gelu_grad(zv)).astype(BF16)
        dzf = dz_ref[...].astype(F32)
        rb_ref[0:1, :] += jnp.sum(dzf, axis=0, keepdims=True)

    vec = pl.BlockSpec((1, SGW), lambda i: (0, 0))
    wsp = pl.BlockSpec((SGG, SGB, SGB), lambda i: (0, 0, 0))
    return pl.pallas_call(
        body, name=name, grid=(s // tr,),
        in_specs=[pl.BlockSpec((tr, SGW), lambda i: (i, 0)), pl.BlockSpec((tr, SGW), lambda i: (i, 1)),
                  pl.BlockSpec((tr, SGW), lambda i: (i, 0)), vec, vec, wsp, wsp,
                  pl.BlockSpec((SGB, LANES), lambda i: (0, 0))],
        out_specs=[pl.BlockSpec((tr, 2 * SGW), lambda i: (i, 0)),
                   pl.BlockSpec((8, 2 * SGW), lambda i: (0, 0)),
                   pl.BlockSpec((8, SGW), lambda i: (0, 0)), wsp,
                   pl.BlockSpec((SGB, LANES), lambda i: (0, 0))],
        out_shape=[jax.ShapeDtypeStruct((s, 2 * SGW), BF16), jax.ShapeDtypeStruct((8, 2 * SGW), F32),
                   jax.ShapeDtypeStruct((8, SGW), F32), jax.ShapeDtypeStruct((SGG, SGB, SGB), F32),
                   jax.ShapeDtypeStruct((SGB, LANES), F32)],
        scratch_shapes=[pltpu.VMEM((tr, SGW), F32)],
        compiler_params=_cparams(("arbitrary",)),
    )(z, z, dy, vgain, vbias, ws, wst, bst)


def _final_loss(x, fg, tgt, gprev, yprev, ts, name):
    s, d = x.shape

    def body(x_ref, fg_ref, t_ref, g_ref, y_ref, l_ref, dx_ref, dy_ref, red_ref):
        @pl.when(pl.program_id(0) == 0)
        def _():
            l_ref[...] = jnp.zeros_like(l_ref)
            red_ref[...] = jnp.zeros_like(red_ref)
        xv = x_ref[...]
        r = _rstd_rows(xv)
        xh = xv * r
        err = xh * fg_ref[...] - t_ref[...]
        l_ref[...] += 0.5 * jnp.sum(jnp.mean(err * err, axis=-1, keepdims=True))
        dyo = err * (1.0 / d)
        dxh = dyo * fg_ref[...]
        dx = r * (dxh - xh * jnp.mean(dxh * xh, axis=-1, keepdims=True))
        dx_ref[...] = dx
        dy_ref[...] = (dx * g_ref[...]).astype(BF16)
        red_ref[0:1, :] += jnp.sum(dyo * xh, axis=0, keepdims=True)
        red_ref[1:2, :] += jnp.sum(dx * y_ref[...].astype(F32), axis=0, keepdims=True)

    row = pl.BlockSpec((ts, d), lambda i: (i, 0))
    vec = pl.BlockSpec((1, d), lambda i: (0, 0))
    return pl.pallas_call(
        body, name=name, grid=(s // ts,),
        in_specs=[row, vec, row, vec, row],
        out_specs=[pl.BlockSpec((8, LANES), lambda i: (0, 0)), row, row, pl.BlockSpec((8, d), lambda i: (0, 0))],
        out_shape=[jax.ShapeDtypeStruct((8, LANES), F32), jax.ShapeDtypeStruct((s, d), F32),
                   jax.ShapeDtypeStruct((s, d), BF16), jax.ShapeDtypeStruct((8, d), F32)],
        compiler_params=_cparams(("arbitrary",)),
    )(x, fg, tgt, gprev, yprev)


def _norm_bwd(xin, dh, dxout, ng, sc, gprev, yprev, ts, name):
    s, d = xin.shape
    has_prev = gprev is not None

    def body(*refs):
        if has_prev:
            x_ref, dh_ref, dxo_ref, ng_ref, sc_ref, g_ref, y_ref, dx_ref, dy_ref, red_ref = refs
        else:
            x_ref, dh_ref, dxo_ref, ng_ref, sc_ref, dx_ref, red_ref = refs

        @pl.when(pl.program_id(0) == 0)
        def _():
            red_ref[...] = jnp.zeros_like(red_ref)
        xv = x_ref[...]
        r = _rstd_rows(xv)
        xh = xv * r
        dhv = dh_ref[...]
        dr = dhv * (1.0 + sc_ref[...])
        t = dr * ng_ref[...]
        dx = dxo_ref[...] + r * (t - xh * jnp.mean(t * xh, axis=-1, keepdims=True))
        dx_ref[...] = dx
        red_ref[0:1, :] += jnp.sum(dhv, axis=0, keepdims=True)
        red_ref[1:2, :] += jnp.sum(dhv * (xh * ng_ref[...]), axis=0, keepdims=True)
        red_ref[2:3, :] += jnp.sum(dr * xh, axis=0, keepdims=True)
        if has_prev:
            dy_ref[...] = (dx * g_ref[...]).astype(BF16)
            red_ref[3:4, :] += jnp.sum(dx * y_ref[...].astype(F32), axis=0, keepdims=True)

    row = pl.BlockSpec((ts, d), lambda i: (i, 0))
    vec = pl.BlockSpec((1, d), lambda i: (0, 0))
    red = pl.BlockSpec((8, d), lambda i: (0, 0))
    if has_prev:
        in_specs, args = [row, row, row, vec, vec, vec, row], (xin, dh, dxout, ng, sc, gprev, yprev)
        out_specs = [row, row, red]
        out_shape = [jax.ShapeDtypeStruct((s, d), F32), jax.ShapeDtypeStruct((s, d), BF16),
                     jax.ShapeDtypeStruct((8, d), F32)]
    else:
        in_specs, args = [row, row, row, vec, vec], (xin, dh, dxout, ng, sc)
        out_specs = [row, red]
        out_shape = [jax.ShapeDtypeStruct((s, d), F32), jax.ShapeDtypeStruct((8, d), F32)]
    return pl.pallas_call(
        body, name=name, grid=(s // ts,), in_specs=in_specs, out_specs=out_specs, out_shape=out_shape,
        compiler_params=_cparams(("arbitrary",)),
    )(*args)


def _ada_mod(c_all, ada_w, ada_b):
    nb = c_all.shape[0]
    da = ada_w.shape[2]

    def body(c_ref, w_ref, b_ref, o_ref, ca_ref):
        cv = c_ref[...]
        ca = cv * _sigmoid(cv)
        ca_ref[...] = ca
        o_ref[0] = lax.dot_general(ca, w_ref[0], (((1,), (0,)), ((), ())), precision=lax.Precision.HIGHEST,
                                   preferred_element_type=F32) + b_ref[0]

    return pl.pallas_call(
        body, name="ada_mod", grid=(2,),
        in_specs=[pl.BlockSpec((nb, D), lambda i: (0, 0)), pl.BlockSpec((1, D, da), lambda i: (i, 0, 0)),
                  pl.BlockSpec((1, 1, da), lambda i: (i, 0, 0))],
        out_specs=[pl.BlockSpec((1, nb, da), lambda i: (i, 0, 0)), pl.BlockSpec((nb, D), lambda i: (0, 0))],
        out_shape=[jax.ShapeDtypeStruct((2, nb, da), F32), jax.ShapeDtypeStruct((nb, D), F32)],
        compiler_params=_cparams(("arbitrary",)),
    )(c_all, ada_w, ada_b)


def _ada_w_grad(c_act_t, dmod):
    nb = c_act_t.shape[1]
    da = dmod.shape[2]
    tn = 512

    def body(c_ref, d_ref, o_ref):
        acc = c_ref[:, 0:1] * d_ref[0, 0:1, :]
        for b in range(1, nb):
            acc = acc + c_ref[:, b:b + 1] * d_ref[0, b:b + 1, :]
        o_ref[0] = acc

    return pl.pallas_call(
        body, name="ada_w_grad", grid=(2, da // tn),
        in_specs=[pl.BlockSpec((D, nb), lambda i, j: (0, 0)), pl.BlockSpec((1, nb, tn), lambda i, j: (i, 0, j))],
        out_specs=pl.BlockSpec((1, D, tn), lambda i, j: (i, 0, j)),
        out_shape=jax.ShapeDtypeStruct((2, D, da), F32),
        compiler_params=_cparams(("arbitrary", "arbitrary")),
    )(c_act_t, dmod)


def _conv_planes(cw, cb):
    cwp = jnp.swapaxes(cw.reshape(3, 2, DFF), 0, 1)
    return jnp.pad(cwp, ((0, 0), (0, 5), (0, 0))), cb.reshape(2, 1, DFF)


def _local_step(x, tgt, mod, wts, small):
    s = x.shape[0]
    ts, tq, tr, tp = TS, TQ, TR, TP
    zb = lambda n: jnp.zeros((1, n), F32)
    m6 = mod.reshape(2, 6, 1, D)
    sh1, sc1, g1, sh2, sc2, g2 = ([m6[i, k] for i in range(2)] for k in range(6))
    n1g, n2g = small["norm1_g"], small["norm2_g"]
    row = lambda a, i: a[i:i + 1]

    qg2 = jnp.tile(small["fox_q_gain"], (1, 2))
    kg2 = jnp.tile(small["fox_k_gain"], (1, 2))
    bfp = jnp.pad(small["fox_b_f"], ((0, 0), (0, LANES - H)))
    proj, h1 = _norm_mod_matmul(x, row(n1g, 0), sc1[0], sh1[0], wts["fox_w_in"], zb(FOX_NP), F32, ts, 1408, "fox_in")
    qa, ka, va = _fox_post(proj, qg2, kg2, bfp, tp, "fox_post")
    att, ql = _attn_fwd(qa, ka, va, tq, "attn_fwd")
    gated = _gate(att, proj, ts, "fox_gate")
    x1, y0 = _matmul_residual(gated, wts["fox_w_out"], x, g1[0], ts, "fox_out")

    def ffn_fwd(xin, i, tag):
        cw, cb = _conv_planes(small["ffn_conv_w"][i], small["ffn_conv_b"][i])
        a, h = _norm_mod_matmul(xin, row(n2g, i), sc2[i], sh2[i], wts["ffn_w_up"][i], zb(2 * DFF), BF16, ts, 1408,
                                "ffn_up" + tag, planes=2)
        f = _conv_gate(a, cw, cb, ts, "ffn_conv" + tag)
        xo, y = _matmul_residual(f, wts["ffn_w_down"][i], xin, g2[i], ts, "ffn_down" + tag)
        return xo, (a, h, f, y, cw, cb)

    x2, ffn0 = ffn_fwd(x1, 0, "0")

    bst = jnp.pad(small["sgu_b_s"].T, ((0, 0), (0, LANES - SGG)))
    ws = small["sgu_w_s"]
    z, h3 = _norm_mod_matmul(x2, row(n1g, 1), sc1[1], sh1[1], wts["sgu_w_in"], small["sgu_b_in"], BF16, ts, 1024,
                             "sgu_in")
    yy = _sgu_fwd(z, small["sgu_v_gain"], small["sgu_v_bias"], ws, bst, tr, "sgu_mix")
    x3, y1 = _matmul_residual(yy, wts["sgu_w_out"], x2, g1[1], ts, "sgu_out")
    x4, ffn1 = ffn_fwd(x3, 1, "1")

    lsum, dx4, dy, redf = _final_loss(x4, small["final_g"], tgt, g2[1], ffn1[3], ts, "final_loss")
    grads = {"final_g": redf[0]}
    dmod = [[None] * 6, [None] * 6]
    dmod[1][5] = redf[1]

    def ffn_bwd(dxo, dy2, xin, i, saved, gprev, yprev, tag):
        a, h, f, _, cw, cb = saved
        wd, wu = wts["ffn_w_down"][i], wts["ffn_w_up"][i]
        g_wd = _matmul(f, dy2, True, False, 1408, D, ts, F32, "ffn_dwdown" + tag)
        df = _matmul(dy2, wd, False, True, ts, 1408, D, BF16, "ffn_df" + tag)
        da, redc = _conv_gate_bwd(a, df, cw, cb, ts, "ffn_conv_bwd" + tag)
        g_wu = jnp.concatenate([_matmul(h, da[e], True, False, D, 1408, ts, F32, "ffn_dwup%s_%d" % (tag, e))
                                for e in range(2)], axis=1)
        dh = _matmul(da, wu, False, True, ts, D, 1408, F32, "ffn_dh" + tag)
        outs = _norm_bwd(xin, dh, dxo, row(n2g, i), sc2[i], gprev, yprev, ts, "ffn_norm_bwd" + tag)
        return outs, g_wd, g_wu, redc

    (dx3, dy1, red), g_wd1, g_wu1, redc1 = ffn_bwd(dx4, dy, x3, 1, ffn1, g1[1], y1, "1")
    dmod[1][3], dmod[1][4], dn2g1, dmod[1][2] = red[0], red[1], red[2], red[3]

    g_swo = _matmul(yy, dy1, True, False, 1024, D, ts, F32, "sgu_dwout")
    dyy = _matmul(dy1, wts["sgu_w_out"], False, True, ts, 1024, D, BF16, "sgu_dyy")
    wst = jnp.swapaxes(ws, 1, 2)
    dz, rb, rv, dws, dbst = _sgu_bwd(z, dyy, small["sgu_v_gain"], small["sgu_v_bias"], ws, wst, bst, tr, "sgu_mix_bwd")
    g_swi = _matmul(h3, dz, True, False, D, 1024, ts, F32, "sgu_dwin")
    dh3 = _matmul(dz, wts["sgu_w_in"], False, True, ts, D, 1024, F32, "sgu_dh")
    dx2, dy2_0, red = _norm_bwd(x2, dh3, dx3, row(n1g, 1), sc1[1], g2[0], ffn0[3], ts, "sgu_norm_bwd")
    dmod[1][0], dmod[1][1], dn1g1, dmod[0][5] = red[0], red[1], red[2], red[3]

    (dx1, dy0, red), g_wd0, g_wu0, redc0 = ffn_bwd(dx2, dy2_0, x1, 0, ffn0, g1[0], y0, "0")
    dmod[0][3], dmod[0][4], dn2g0, dmod[0][2] = red[0], red[1], red[2], red[3]

    g_fwo = _matmul(gated, dy0, True, False, D, D, ts, F32, "fox_dwout")
    dgated = _matmul(dy0, wts["fox_w_out"], False, True, ts, D, D, F32, "fox_dgated")
    doa, dop = _attn_bwd_prep(dgated, att, proj, ts, "attn_bwd_prep")
    dqa, dka, dva = _attn_bwd(ql, ka, va, doa, tq, "attn_bwd")
    dproj, redx = _fox_post_bwd(proj, dqa, dka, dva, dop, qg2, kg2, bfp, tp, "fox_post_bwd")
    g_fwi = _matmul(h1, dproj, True, False, D, 1408, ts, F32, "fox_dwin")
    dh1 = _matmul(dproj, wts["fox_w_in"], False, True, ts, D, 1408, F32, "fox_dh")
    dx0, red = _norm_bwd(x, dh1, dx1, row(n1g, 0), sc1[0], None, None, ts, "fox_norm_bwd")
    dmod[0][0], dmod[0][1], dn1g0 = red[0], red[1], red[2]

    grads.update(
        fox_w_in=g_fwi[:, :FOX_N], fox_w_out=g_fwo, sgu_w_in=g_swi, sgu_w_out=g_swo,
        ffn_w_up=jnp.stack([g_wu0, g_wu1]),
        ffn_w_down=jnp.stack([g_wd0, g_wd1]),
        fox_q_gain=redx[0, :DH] + redx[0, DH:], fox_k_gain=redx[1, :DH] + redx[1, DH:], fox_b_f=redx[2, :H],
        sgu_b_in=rb[0], sgu_v_gain=rv[0], sgu_v_bias=rv[1], sgu_w_s=dws, sgu_b_s=dbst[:, :SGG].T,
        ffn_conv_w=jnp.stack([jnp.swapaxes(r[:, 0:3], 0, 1).reshape(3, 2 * DFF) for r in (redc0, redc1)]),
        ffn_conv_b=jnp.stack([r[:, 3].reshape(2 * DFF) for r in (redc0, redc1)]),
        norm1_g=jnp.stack([dn1g0, dn1g1]), norm2_g=jnp.stack([dn2g0, dn2g1]),
    )
    dmod_arr = jnp.stack([jnp.concatenate(dmod[0]), jnp.concatenate(dmod[1])])
    return lsum[0, 0], dx0, grads, dmod_arr


N_DEV = 8
N_CHIP = 4
HBM_SPEC = pl.BlockSpec(memory_space=pltpu.HBM)
VMEM_SPEC = pl.BlockSpec(memory_space=pltpu.VMEM)


def _mesh_pos():
    return lax.axis_index("x"), lax.axis_index("y"), lax.axis_index("c")


def _other_chips(x, y):
    return [(1 - x, y), (x, 1 - y), (1 - x, 1 - y)]


def _remote(src, dst, ssem, rsem, dev):
    return pltpu.make_async_remote_copy(src_ref=src, dst_ref=dst, send_sem=ssem, recv_sem=rsem,
                                        device_id=dev, device_id_type=MESH)


def _allgather8(xb, name):
    m_per, n = xb.shape

    def body(x_ref, out_ref, send_sems, recv_sems, local_sem):
        x, y, c = _mesh_pos()
        me, sibling = (x, y, c), (x, y, 1 - c)
        chips = _other_chips(x, y)

        def rows(px, py, pc):
            return out_ref.at[pl.ds((4 * px + 2 * py + pc) * m_per, m_per), :]

        def copy(k, block, to, src=None):
            return _remote(rows(*block) if src is None else src, rows(*block),
                           send_sems.at[k], recv_sems.at[k], to)

        mine = pltpu.make_async_copy(x_ref, rows(*me), local_sem)
        mine.start()
        first = [copy(0, me, sibling, src=x_ref)]
        first += [copy(1 + j, me, (*chip, c), src=x_ref) for j, chip in enumerate(chips)]
        for cp in first:
            cp.start()
        passed = [copy(4 + j, (*chip, c), sibling) for j, chip in enumerate(chips)]
        for j, chip in enumerate(chips):
            copy(1 + j, (*chip, c), me).wait_recv()
            passed[j].start()
        copy(0, sibling, me).wait_recv()
        for j, chip in enumerate(chips):
            copy(4 + j, (*chip, 1 - c), me).wait_recv()
        for cp in first + passed:
            cp.wait_send()
        mine.wait()

    return pl.pallas_call(
        body, name=name,
        out_shape=jax.ShapeDtypeStruct((N_DEV * m_per, n), xb.dtype),
        in_specs=[VMEM_SPEC], out_specs=VMEM_SPEC,
        scratch_shapes=[pltpu.SemaphoreType.DMA((7,)), pltpu.SemaphoreType.DMA((7,)), pltpu.SemaphoreType.DMA],
        compiler_params=pltpu.CompilerParams(vmem_limit_bytes=V7X_VMEM_LIMIT),
    )(xb)


def _gather_shards(pack, name):
    r, n = pack.shape
    rh = r // 2

    def body(p_ref, o_ref, send_sems, recv_sems, pass_send, pass_recv):
        x, y, c = _mesh_pos()
        me = 2 * x + y
        sibling = (x, y, 1 - c)
        chips = _other_chips(x, y)

        def half(ci, hf):
            return o_ref.at[ci, pl.ds(hf * rh, rh), :]

        sends = [_remote(p_ref.at[pl.ds(c * rh, rh), :], half(me, c), send_sems.at[k], recv_sems.at[k], (*chip, c))
                 for k, chip in enumerate(chips)]
        for cp in sends:
            cp.start()
        passed = []
        for k, chip in enumerate(chips):
            ci = 2 * chip[0] + chip[1]
            _remote(half(ci, c), half(ci, c), send_sems.at[k], recv_sems.at[k], (*chip, c)).wait_recv()
            cp = _remote(half(ci, c), half(ci, c), pass_send.at[k], pass_recv.at[k], sibling)
            cp.start()
            passed.append(cp)
        for k, chip in enumerate(chips):
            ci = 2 * chip[0] + chip[1]
            _remote(half(ci, 1 - c), half(ci, 1 - c), pass_send.at[k], pass_recv.at[k], sibling).wait_recv()
        for cp in sends + passed:
            cp.wait_send()

    return pl.pallas_call(
        body, name=name,
        out_shape=jax.ShapeDtypeStruct((N_CHIP, r, n), pack.dtype),
        in_specs=[HBM_SPEC], out_specs=HBM_SPEC,
        scratch_shapes=[pltpu.SemaphoreType.DMA((3,)), pltpu.SemaphoreType.DMA((3,)),
                        pltpu.SemaphoreType.DMA((3,)), pltpu.SemaphoreType.DMA((3,))],
    )(pack)


def _rs_to_sibling(g, name):
    nc, r, n = g.shape
    rh = r // 2

    def body(g_ref, o_ref, ssem, rsem):
        x, y, c = _mesh_pos()
        cp = _remote(g_ref.at[:, pl.ds((1 - c) * rh, rh), :], o_ref, ssem, rsem, (x, y, 1 - c))
        cp.start()
        cp.wait()

    return pl.pallas_call(
        body, name=name, out_shape=jax.ShapeDtypeStruct((nc, rh, n), g.dtype),
        in_specs=[HBM_SPEC], out_specs=HBM_SPEC,
        scratch_shapes=[pltpu.SemaphoreType.DMA, pltpu.SemaphoreType.DMA],
    )(g)


def _rs_chip_sum(g, sib, c_arr, tr, name):
    nc, r, n = g.shape
    rh = r // 2
    g4 = g.reshape(nc, 2, rh, n)

    def body(c_ref, g_ref, s_ref, o_ref):
        o_ref[...] = (g_ref[0].astype(F32) + s_ref[...].astype(F32)).astype(BF16)

    return pl.pallas_call(
        body, name=name, out_shape=jax.ShapeDtypeStruct((nc, rh, n), BF16),
        grid_spec=pltpu.PrefetchScalarGridSpec(
            num_scalar_prefetch=1, grid=(nc, rh // tr),
            in_specs=[pl.BlockSpec((1, 1, tr, n), lambda j, i, cr: (j, cr[0], i, 0)),
                      pl.BlockSpec((1, tr, n), lambda j, i, cr: (j, i, 0))],
            out_specs=pl.BlockSpec((1, tr, n), lambda j, i, cr: (j, i, 0))),
        compiler_params=_cparams(("arbitrary", "arbitrary")),
    )(c_arr, g4, sib)


def _rs_across_chips(cs, name):
    nc, rh, n = cs.shape

    def body(cs_ref, o_ref, send_sems, recv_sems):
        x, y, c = _mesh_pos()
        cps = []
        for k, chip in enumerate(_other_chips(x, y)):
            ci = 2 * chip[0] + chip[1]
            cp = _remote(cs_ref.at[ci], o_ref.at[k], send_sems.at[k], recv_sems.at[k], (*chip, c))
            cp.start()
            cps.append(cp)
        for cp in cps:
            cp.wait()

    return pl.pallas_call(
        body, name=name, out_shape=jax.ShapeDtypeStruct((3, rh, n), cs.dtype),
        in_specs=[HBM_SPEC], out_specs=HBM_SPEC,
        scratch_shapes=[pltpu.SemaphoreType.DMA((3,)), pltpu.SemaphoreType.DMA((3,))],
    )(cs)


def _rs_final_sum(cs, rcv, me_arr, tr, name):
    nc, rh, n = cs.shape

    def body(m_ref, c_ref, r_ref, o_ref):
        acc = c_ref[0].astype(F32)
        for k in range(3):
            acc = acc + r_ref[k].astype(F32)
        o_ref[...] = acc

    return pl.pallas_call(
        body, name=name, out_shape=jax.ShapeDtypeStruct((rh, n), F32),
        grid_spec=pltpu.PrefetchScalarGridSpec(
            num_scalar_prefetch=1, grid=(rh // tr,),
            in_specs=[pl.BlockSpec((1, tr, n), lambda i, mr: (mr[0], i, 0)),
                      pl.BlockSpec((3, tr, n), lambda i, mr: (0, i, 0))],
            out_specs=pl.BlockSpec((tr, n), lambda i, mr: (i, 0))),
        compiler_params=_cparams(("arbitrary",)),
    )(me_arr, cs, rcv)


def _rs_swap_halves(half, name):
    rh, n = half.shape

    def body(h_ref, o_ref, ssem, rsem):
        x, y, c = _mesh_pos()
        cp = _remote(h_ref, o_ref, ssem, rsem, (x, y, 1 - c))
        cp.start()
        cp.wait()

    return pl.pallas_call(
        body, name=name, out_shape=jax.ShapeDtypeStruct((rh, n), half.dtype),
        in_specs=[HBM_SPEC], out_specs=HBM_SPEC,
        scratch_shapes=[pltpu.SemaphoreType.DMA, pltpu.SemaphoreType.DMA],
    )(half)


def _sum8(g, name):
    nd, r, n = g.shape

    def body(g_ref, o_ref):
        acc = g_ref[0]
        for k in range(1, nd):
            acc = acc + g_ref[k]
        o_ref[...] = acc

    return pl.pallas_call(
        body, name=name, grid=(r // 8,),
        in_specs=[pl.BlockSpec((nd, 8, n), lambda i: (0, i, 0))],
        out_specs=pl.BlockSpec((8, n), lambda i: (i, 0)),
        out_shape=jax.ShapeDtypeStruct((r, n), F32),
        compiler_params=_cparams(("arbitrary",)),
    )(g)


def _adamw(w, g, m, v, name):
    r, n = w.shape
    tr = 128 if r % 128 == 0 else 8
    bc1 = 1.0 - ADAM_B1 ** ADAM_STEP
    bc2 = 1.0 - ADAM_B2 ** ADAM_STEP

    def body(w_ref, g_ref, m_ref, v_ref, d_ref, mo_ref, vo_ref):
        gv = g_ref[...]
        mn = ADAM_B1 * m_ref[...] + (1.0 - ADAM_B1) * gv
        vn = ADAM_B2 * v_ref[...] + (1.0 - ADAM_B2) * (gv * gv)
        d_ref[...] = -ADAM_LR * ((mn / bc1) / (jnp.sqrt(vn / bc2) + ADAM_EPS) + ADAM_WD * w_ref[...])
        mo_ref[...] = mn
        vo_ref[...] = vn

    blk = pl.BlockSpec((tr, n), lambda i: (i, 0))
    shp = jax.ShapeDtypeStruct((r, n), F32)
    return pl.pallas_call(
        body, name=name, grid=(r // tr,), in_specs=[blk] * 4, out_specs=[blk] * 3, out_shape=[shp] * 3,
        compiler_params=_cparams(("arbitrary",)),
    )(w, g, m, v)


ROW = 1024
PACK_ROWS = 7168
BIG = ("fox_w_in", "fox_w_out", "sgu_w_in", "sgu_w_out", "ffn_w_up", "ffn_w_down")
SMALL_SHARDED = ("sgu_b_in", "sgu_v_gain", "sgu_v_bias", "ffn_conv_w")
SMALL_REPL = ("fox_b_f", "fox_q_gain", "fox_k_gain", "sgu_w_s", "sgu_b_s", "ffn_conv_b", "ada_b",
              "norm1_g", "norm2_g", "final_g")
WEIGHTS = ("fox_w_in", "fox_b_f", "fox_q_gain", "fox_k_gain", "fox_w_out", "sgu_w_in", "sgu_b_in", "sgu_v_gain",
           "sgu_v_bias", "sgu_w_s", "sgu_b_s", "sgu_w_out", "ffn_w_up", "ffn_conv_w", "ffn_conv_b", "ffn_w_down",
           "ada_w", "ada_b", "norm1_g", "norm2_g", "final_g")


def _rows_of(a, mult=1):
    flat = a.reshape(-1)
    rows = -(-flat.shape[0] // ROW)
    rows = -(-rows // mult) * mult
    return jnp.pad(flat, (0, rows * ROW - flat.shape[0])).reshape(rows, ROW)


def _pack(parts, mult, total=None):
    p = jnp.concatenate([_rows_of(a, mult) for a in parts], axis=0)
    if total is not None:
        p = jnp.pad(p, ((0, total - p.shape[0]), (0, 0)))
    return p


def _unpack(pack, shapes, mult):
    out, r0 = [], 0
    for shp in shapes:
        size = int(np.prod(shp))
        rows = -(-(-(-size // ROW)) // mult) * mult
        out.append(pack[r0:r0 + rows].reshape(-1)[:size].reshape(shp))
        r0 += rows
    return out


def _big_parts(t, j):
    return [t["fox_w_in"][:, 1028 * j:1028 * (j + 1)], t["fox_w_out"][256 * j:256 * (j + 1)],
            t["sgu_w_in"][:, 1024 * j:1024 * (j + 1)], t["sgu_w_out"][512 * j:512 * (j + 1)],
            t["ffn_w_up"][:, :, 1408 * j:1408 * (j + 1)], t["ffn_w_down"][:, 704 * j:704 * (j + 1)]]


BIG_SHARD_SHAPES = ((1024, 1028), (256, 1024), (1024, 1024), (512, 1024), (2, 1024, 1408), (2, 704, 1024))


def _full_from_shards(packs):
    per = [_unpack(packs[j], BIG_SHARD_SHAPES, 16) for j in range(N_CHIP)]
    cat = lambda i, axis: jnp.concatenate([per[j][i] for j in range(N_CHIP)], axis=axis)
    return dict(fox_w_in=jnp.pad(cat(0, 1), ((0, 0), (0, FOX_NP - FOX_N))), fox_w_out=cat(1, 0),
                sgu_w_in=cat(2, 1), sgu_w_out=cat(3, 0), ffn_w_up=cat(4, 2), ffn_w_down=cat(5, 1))


def kernel(x, c, fox_w_in, fox_b_f, fox_q_gain, fox_k_gain, fox_w_out, sgu_w_in, sgu_b_in, sgu_v_gain, sgu_v_bias, sgu_w_s, sgu_b_s, sgu_w_out, ffn_w_up, ffn_conv_w, ffn_conv_b, ffn_w_down, ada_w, ada_b, norm1_g, norm2_g, final_g, loss_target, m_fox_w_in, m_fox_b_f, m_fox_q_gain, m_fox_k_gain, m_fox_w_out, m_sgu_w_in, m_sgu_b_in, m_sgu_v_gain, m_sgu_v_bias, m_sgu_w_s, m_sgu_b_s, m_sgu_w_out, m_ffn_w_up, m_ffn_conv_w, m_ffn_conv_b, m_ffn_w_down, m_ada_w, m_ada_b, m_norm1_g, m_norm2_g, m_final_g, v_fox_w_in, v_fox_b_f, v_fox_q_gain, v_fox_k_gain, v_fox_w_out, v_sgu_w_in, v_sgu_b_in, v_sgu_v_gain, v_sgu_v_bias, v_sgu_w_s, v_sgu_b_s, v_sgu_w_out, v_ffn_w_up, v_ffn_conv_w, v_ffn_conv_b, v_ffn_w_down, v_ada_w, v_ada_b, v_norm1_g, v_norm2_g, v_final_g):
    w = dict(fox_w_in=fox_w_in, fox_b_f=fox_b_f, fox_q_gain=fox_q_gain, fox_k_gain=fox_k_gain, fox_w_out=fox_w_out,
             sgu_w_in=sgu_w_in, sgu_b_in=sgu_b_in, sgu_v_gain=sgu_v_gain, sgu_v_bias=sgu_v_bias, sgu_w_s=sgu_w_s,
             sgu_b_s=sgu_b_s, sgu_w_out=sgu_w_out, ffn_w_up=ffn_w_up, ffn_conv_w=ffn_conv_w, ffn_conv_b=ffn_conv_b,
             ffn_w_down=ffn_w_down, ada_w=ada_w, ada_b=ada_b, norm1_g=norm1_g, norm2_g=norm2_g, final_g=final_g)
    mom = dict(fox_w_in=m_fox_w_in, fox_b_f=m_fox_b_f, fox_q_gain=m_fox_q_gain, fox_k_gain=m_fox_k_gain,
               fox_w_out=m_fox_w_out, sgu_w_in=m_sgu_w_in, sgu_b_in=m_sgu_b_in, sgu_v_gain=m_sgu_v_gain,
               sgu_v_bias=m_sgu_v_bias, sgu_w_s=m_sgu_w_s, sgu_b_s=m_sgu_b_s, sgu_w_out=m_sgu_w_out,
               ffn_w_up=m_ffn_w_up, ffn_conv_w=m_ffn_conv_w, ffn_conv_b=m_ffn_conv_b, ffn_w_down=m_ffn_w_down,
               ada_w=m_ada_w, ada_b=m_ada_b, norm1_g=m_norm1_g, norm2_g=m_norm2_g, final_g=m_final_g)
    var = dict(fox_w_in=v_fox_w_in, fox_b_f=v_fox_b_f, fox_q_gain=v_fox_q_gain, fox_k_gain=v_fox_k_gain,
               fox_w_out=v_fox_w_out, sgu_w_in=v_sgu_w_in, sgu_b_in=v_sgu_b_in, sgu_v_gain=v_sgu_v_gain,
               sgu_v_bias=v_sgu_v_bias, sgu_w_s=v_sgu_w_s, sgu_b_s=v_sgu_b_s, sgu_w_out=v_sgu_w_out,
               ffn_w_up=v_ffn_w_up, ffn_conv_w=v_ffn_conv_w, ffn_conv_b=v_ffn_conv_b, ffn_w_down=v_ffn_w_down,
               ada_w=v_ada_w, ada_b=v_ada_b, norm1_g=v_norm1_g, norm2_g=v_norm2_g, final_g=v_final_g)

    ax, ay, ac = _mesh_pos()
    chip = 2 * ax + ay
    dev = 2 * chip + ac

    small_shard_shapes = tuple(w[n].shape for n in SMALL_SHARDED)
    blk = _pack([c] + [w[n] for n in SMALL_SHARDED], 1, 16)
    gat = _allgather8(blk, "gather_small").reshape(N_DEV, 16, ROW)
    c_all = gat[:, 0, :]
    per_chip = [_unpack(gat[2 * j, 1:], small_shard_shapes, 1) for j in range(N_CHIP)]
    full_small = {n: jnp.concatenate([per_chip[j][i] for j in range(N_CHIP)], axis=-1)
                  for i, n in enumerate(SMALL_SHARDED)}

    shard = dict(fox_w_in=fox_w_in[0], fox_w_out=fox_w_out[0], sgu_w_in=sgu_w_in[0], sgu_w_out=sgu_w_out[0],
                 ffn_w_up=ffn_w_up, ffn_w_down=ffn_w_down)
    my_pack = _pack([shard[n] for n in BIG], 16, PACK_ROWS).astype(BF16)
    packs = lax.dynamic_update_slice(_gather_shards(my_pack, "gather_weights"), my_pack[None], (chip, 0, 0))
    wts = _full_from_shards(packs)

    da = ada_w.shape[2]
    ada_b_cols = lax.dynamic_slice_in_dim(ada_b, chip * da, da, axis=1)[:, None, :]
    mod_cols, c_act = _ada_mod(c_all, ada_w, ada_b_cols)
    mod_all = _allgather8(mod_cols.reshape(-1, ROW), "gather_mod").reshape(N_DEV, 2, N_DEV, da)
    mod_mine = lax.dynamic_index_in_dim(mod_all[0::2], dev, axis=2, keepdims=False)
    mod = jnp.swapaxes(mod_mine, 0, 1).reshape(2, N_CHIP * da)

    small = dict(norm1_g=norm1_g, norm2_g=norm2_g, final_g=final_g[None], fox_q_gain=fox_q_gain,
                 fox_k_gain=fox_k_gain, fox_b_f=fox_b_f, sgu_b_in=full_small["sgu_b_in"],
                 sgu_v_gain=full_small["sgu_v_gain"], sgu_v_bias=full_small["sgu_v_bias"], sgu_w_s=sgu_w_s[0],
                 sgu_b_s=sgu_b_s[0], ffn_conv_w=full_small["ffn_conv_w"], ffn_conv_b=ffn_conv_b)
    loss_dev, dx, g, dmod = _local_step(x[0], loss_target[0], mod, wts, small)

    g["ada_b"] = dmod
    small_names = ("ada_b",) + SMALL_SHARDED + tuple(n for n in SMALL_REPL if n != "ada_b")
    gs = _pack([g[n] for n in small_names], 1)
    rows_s = -(-gs.shape[0] // 8) * 8
    gs = jnp.pad(gs, ((0, rows_s - gs.shape[0]), (0, 0)))
    gs_all = _allgather8(gs, "gather_small_grads").reshape(N_DEV, rows_s, ROW)
    gsum = _sum8(gs_all, "sum_small_grads")
    full_shapes = {n: w[n].shape for n in SMALL_REPL}
    full_shapes.update({n: w[n].shape[:-1] + (w[n].shape[-1] * N_CHIP,) for n in SMALL_SHARDED})
    gfull = dict(zip(small_names, _unpack(gsum, [full_shapes[n] for n in small_names], 1)))
    grads = {n: gfull[n] for n in SMALL_REPL}
    for n in SMALL_SHARDED:
        width = w[n].shape[-1]
        grads[n] = lax.dynamic_slice_in_dim(gfull[n], chip * width, width, axis=gfull[n].ndim - 1)
    dmod_all = gs_all[:, :12, :].reshape(N_DEV, 2, N_CHIP * da)
    dmod_cols = jnp.swapaxes(lax.dynamic_slice_in_dim(dmod_all, chip * da, da, axis=2), 0, 1)
    grads["ada_w"] = _ada_w_grad(c_act.T, dmod_cols)

    gfull_big = dict(fox_w_in=g["fox_w_in"], fox_w_out=g["fox_w_out"], sgu_w_in=g["sgu_w_in"],
                     sgu_w_out=g["sgu_w_out"], ffn_w_up=g["ffn_w_up"], ffn_w_down=g["ffn_w_down"])
    gpack = jnp.stack([_pack(_big_parts(gfull_big, j), 16, PACK_ROWS) for j in range(N_CHIP)]).astype(BF16)
    c_arr = jnp.reshape(ac, (1,)).astype(jnp.int32)
    me_arr = jnp.reshape(chip, (1,)).astype(jnp.int32)
    sib = _rs_to_sibling(gpack, "rs_sibling")
    cs = _rs_chip_sum(gpack, sib, c_arr, 512, "rs_chip_sum")
    rcv = _rs_across_chips(cs, "rs_chips")
    red_half = _rs_final_sum(cs, rcv, me_arr, 512, "rs_final_sum")
    other_half = _rs_swap_halves(red_half, "rs_swap")
    red = jnp.concatenate([jnp.where(ac == 0, red_half, other_half), jnp.where(ac == 0, other_half, red_half)])
    for n, gv in zip(BIG, _unpack(red, BIG_SHARD_SHAPES, 16)):
        grads[n] = gv.reshape(w[n].shape)

    delta, new_m, new_v = {}, {}, {}
    for n in BIG + ("ada_w",):
        shp = w[n].shape
        two_d = lambda a: a.reshape(-1, shp[-1])
        d_, m_, v_ = _adamw(two_d(w[n]), two_d(grads[n]), two_d(mom[n]), two_d(var[n]), "adamw_" + n)
        delta[n], new_m[n], new_v[n] = d_.reshape(shp), m_.reshape(shp), v_.reshape(shp)
    rest = SMALL_SHARDED + SMALL_REPL
    packs = [_pack([t[n] for n in rest], 1) for t in (w, grads, mom, var)]
    rows_r = -(-packs[0].shape[0] // 8) * 8
    packs = [jnp.pad(p, ((0, rows_r - p.shape[0]), (0, 0))) for p in packs]
    outs = _adamw(*packs, "adamw_small")
    for t, o in zip((delta, new_m, new_v), outs):
        t.update(zip(rest, _unpack(o, [w[n].shape for n in rest], 1)))

    loss = lax.psum(loss_dev, ("x", "y", "c"))
    return (loss, dx[None], *[grads[n].reshape(w[n].shape) for n in WEIGHTS], *[delta[n] for n in WEIGHTS],
            *[new_m[n] for n in WEIGHTS], *[new_v[n] for n in WEIGHTS])
```

```python
import functools
import math

import numpy as np
import jax
import jax.numpy as jnp
from jax import lax
from jax.experimental import pallas as pl
from jax.experimental.pallas import tpu as pltpu

F32 = jnp.float32
BF16 = jnp.bfloat16
MESH = pl.DeviceIdType.MESH

D = 1024
H = 16
DH = 64
NP = H // 2
LANES = 128
DFF = 2816
SGW = 2048
SGG = 8
SGC = 256
SGB = 128
CHUNK = 64
EPS = 1e-6
FOX_N = 4 * D + H
FOX_NP = 4224
GT = 256
NGT = DFF // GT
SCALE = DH ** -0.5
LOG2E = 1.4426950408889634

ADAM_LR = 0.001
ADAM_B1 = 0.9
ADAM_B2 = 0.999
ADAM_EPS = 1e-08
ADAM_WD = 0.01
ADAM_STEP = 10

V7X_VMEM_LIMIT = 56 * 1024 * 1024

L_F = 64
L_NF = 67
L_LSE = 70


def _cparams(sem=None):
    return pltpu.CompilerParams(dimension_semantics=sem, vmem_limit_bytes=V7X_VMEM_LIMIT)


def _split3(x):
    hi = x.astype(BF16)
    r = x - hi.astype(F32)
    mid = r.astype(BF16)
    lo = (r - mid.astype(F32)).astype(BF16)
    return hi, mid, lo


def _dot(a, b, dims=(((1,), (0,)), ((), ()))):
    return lax.dot_general(a, b, dims, preferred_element_type=F32)


def _dot_nt(a, b):
    return _dot(a, b, (((1,), (1,)), ((), ())))


def _dot_tn(a, b):
    return _dot(a, b, (((0,), (0,)), ((), ())))


def _exact_dot(m_bf16, x_f32):
    hi, mid, lo = _split3(x_f32)
    return _dot(m_bf16, hi) + _dot(m_bf16, mid) + _dot(m_bf16, lo)


def _exact_dot_r(x_f32, m_bf16):
    hi, mid, lo = _split3(x_f32)
    return _dot(hi, m_bf16) + _dot(mid, m_bf16) + _dot(lo, m_bf16)


def _head_block_ones():
    r = lax.broadcasted_iota(jnp.int32, (LANES, LANES), 0) // DH
    c = lax.broadcasted_iota(jnp.int32, (LANES, LANES), 1) // DH
    return (r == c).astype(BF16)


def _sigmoid(x):
    return 1.0 / (1.0 + jnp.exp(-x))


def _gelu(x):
    c = math.sqrt(2.0 / math.pi)
    return 0.5 * x * (1.0 + jnp.tanh(c * (x + 0.044715 * (x * x * x))))


def _gelu_grad(x):
    c = math.sqrt(2.0 / math.pi)
    t = jnp.tanh(c * (x + 0.044715 * (x * x * x)))
    return 0.5 * (1.0 + t) + 0.5 * x * (1.0 - t * t) * c * (1.0 + 3 * 0.044715 * (x * x))


def _rstd_rows(x):
    return lax.rsqrt(jnp.mean(x * x, axis=-1, keepdims=True) + EPS)


def _norm_mod_matmul(x, ng, sc, sh, w, bias, out_dtype, ts, tn, name, planes=1):
    s, d = x.shape
    ns = w.shape[-1]
    n = w.shape[0] * ns if w.ndim == 3 else ns
    nc = n // planes

    def body(x_ref, ng_ref, sc_ref, sh_ref, w_ref, b_ref, o_ref, h_ref):
        xv = x_ref[...]
        h = (xv * _rstd_rows(xv) * ng_ref[...] * (1.0 + sc_ref[...]) + sh_ref[...]).astype(BF16)
        h_ref[...] = h
        for e in range(planes):
            for c0 in range(0, nc, tn):
                g0 = e * nc + c0
                wv = w_ref[g0 // ns, :, g0 % ns:g0 % ns + tn] if w.ndim == 3 else w_ref[:, g0:g0 + tn]
                val = (_dot(h, wv) + b_ref[:, g0:g0 + tn]).astype(out_dtype)
                if planes == 1:
                    o_ref[:, c0:c0 + tn] = val
                else:
                    o_ref[e, :, c0:c0 + tn] = val

    vec = pl.BlockSpec((1, d), lambda i: (0, 0))
    w_spec = (pl.BlockSpec(w.shape, lambda i: (0, 0, 0)) if w.ndim == 3 else pl.BlockSpec((d, n), lambda i: (0, 0)))
    if planes == 1:
        o_spec, o_shape = pl.BlockSpec((ts, n), lambda i: (i, 0)), (s, n)
    else:
        o_spec, o_shape = pl.BlockSpec((planes, ts, nc), lambda i: (0, i, 0)), (planes, s, nc)
    return pl.pallas_call(
        body, name=name, grid=(s // ts,),
        in_specs=[pl.BlockSpec((ts, d), lambda i: (i, 0)), vec, vec, vec, w_spec,
                  pl.BlockSpec((1, n), lambda i: (0, 0))],
        out_specs=[o_spec, pl.BlockSpec((ts, d), lambda i: (i, 0))],
        out_shape=[jax.ShapeDtypeStruct(o_shape, out_dtype), jax.ShapeDtypeStruct((s, d), BF16)],
        compiler_params=_cparams(("arbitrary",)),
    )(x, ng, sc, sh, w, bias)


def _matmul(a, b, ta, tb, tm, tn, tk, out_dtype, name, out_parts=1):
    if a.ndim == 3:
        m, k = a.shape[1], a.shape[0] * a.shape[2]
        nkp = a.shape[2] // tk
    else:
        m, k = (a.shape[1], a.shape[0]) if ta else a.shape
    if b.ndim == 3:
        n = b.shape[1] if tb else b.shape[0] * b.shape[2]
        nbp = b.shape[2] // (tk if tb else tn)
    else:
        n = b.shape[0] if tb else b.shape[1]
    nk = k // tk
    nop = n // out_parts // tn
    dims = (((0,) if ta else (1,), (1,) if tb else (0,)), ((), ()))

    def body(a_ref, b_ref, o_ref, acc):
        kk = pl.program_id(2)

        @pl.when(kk == 0)
        def _():
            acc[...] = jnp.zeros_like(acc)
        acc[...] += _dot(a_ref[...], b_ref[...], dims)

        @pl.when(kk == nk - 1)
        def _():
            o_ref[...] = acc[...].astype(out_dtype)

    if a.ndim == 3:
        a_spec = pl.BlockSpec((None, tm, tk), lambda i, j, kk: (kk // nkp, i, kk % nkp))
    else:
        a_spec = (pl.BlockSpec((tk, tm), lambda i, j, kk: (kk, i)) if ta
                  else pl.BlockSpec((tm, tk), lambda i, j, kk: (i, kk)))
    if b.ndim == 3 and tb:
        b_spec = pl.BlockSpec((None, tn, tk), lambda i, j, kk: (kk // nbp, j, kk % nbp))
    elif b.ndim == 3:
        b_spec = pl.BlockSpec((None, tk, tn), lambda i, j, kk: (j // nbp, kk, j % nbp))
    else:
        b_spec = (pl.BlockSpec((tn, tk), lambda i, j, kk: (j, kk)) if tb
                  else pl.BlockSpec((tk, tn), lambda i, j, kk: (kk, j)))
    if out_parts > 1:
        o_spec = pl.BlockSpec((None, tm, tn), lambda i, j, kk: (j // nop, i, j % nop))
        o_shape = (out_parts, m, n // out_parts)
    else:
        o_spec, o_shape = pl.BlockSpec((tm, tn), lambda i, j, kk: (i, j)), (m, n)
    return pl.pallas_call(
        body, name=name, grid=(m // tm, n // tn, nk),
        in_specs=[a_spec, b_spec],
        out_specs=o_spec,
        out_shape=jax.ShapeDtypeStruct(o_shape, out_dtype),
        scratch_shapes=[pltpu.VMEM((tm, tn), F32)],
        compiler_params=_cparams(("arbitrary", "arbitrary", "arbitrary")),
    )(a, b)


def _matmul_residual(a, w, xin, g, ts, name):
    s, k = a.shape
    d = w.shape[1]

    def body(a_ref, w_ref, x_ref, g_ref, o_ref, y_ref):
        y = _dot(a_ref[...], w_ref[...])
        o_ref[...] = x_ref[...] + g_ref[...] * y
        y_ref[...] = y.astype(BF16)

    return pl.pallas_call(
        body, name=name, grid=(s // ts,),
        in_specs=[pl.BlockSpec((ts, k), lambda i: (i, 0)),
                  pl.BlockSpec((k, d), lambda i: (0, 0)),
                  pl.BlockSpec((ts, d), lambda i: (i, 0)),
                  pl.BlockSpec((1, d), lambda i: (0, 0))],
        out_specs=[pl.BlockSpec((ts, d), lambda i: (i, 0)), pl.BlockSpec((ts, d), lambda i: (i, 0))],
        out_shape=[jax.ShapeDtypeStruct((s, d), F32), jax.ShapeDtypeStruct((s, d), BF16)],
        compiler_params=_cparams(("arbitrary",)),
    )(a, w, xin, g)


def _lane(shape):
    return lax.broadcasted_iota(jnp.int32, shape, 1)


def _pair_norm(x, gain2, bones):
    msq = _exact_dot_r(x * x, bones) * (1.0 / DH)
    r = lax.rsqrt(msq + EPS)
    xh = x * r
    return xh * gain2, xh, r


def _fox_post(proj, qg2, kg2, bf, ts, name):
    s = proj.shape[0]

    def body(p_ref, qg_ref, kg_ref, bf_ref, q_ref, k_ref, v_ref, carry):
        @pl.when(pl.program_id(0) == 0)
        def _():
            carry[...] = jnp.zeros_like(carry)
        lane = _lane((ts, LANES))
        bones = _head_block_ones()
        xf = p_ref[:, 4 * D:4 * D + LANES] + bf_ref[...]
        logf = jnp.minimum(xf, 0.0) - jnp.log(1.0 + jnp.exp(-jnp.abs(xf)))
        logf = jnp.where(lane < H, logf, 0.0)
        rr = lax.broadcasted_iota(jnp.int32, (ts, ts), 0)
        cc = lax.broadcasted_iota(jnp.int32, (ts, ts), 1)
        ltri = (cc <= rr).astype(BF16)
        fcum = _exact_dot(ltri, logf) + carry[0:1, :]
        carry[0:1, :] = fcum[ts - 1:ts, :]
        fhi, fmid, flo = _split3(fcum * LOG2E)
        fhi, fmid, flo = fhi.astype(F32), fmid.astype(F32), flo.astype(F32)
        one_q = ((lane >= L_NF) & (lane < L_NF + 3)).astype(F32)
        one_k = (((lane >= L_F) & (lane < L_F + 3)) | ((lane >= L_LSE) & (lane < L_LSE + 3))).astype(F32)
        one_v = ((lane >= L_F) & (lane < L_F + 3)).astype(F32)
        for p in range(NP):
            qn, _, _ = _pair_norm(p_ref[:, p * LANES:(p + 1) * LANES], qg_ref[...], bones)
            kn, _, _ = _pair_norm(p_ref[:, D + p * LANES:D + (p + 1) * LANES], kg_ref[...], bones)
            vv = p_ref[:, 2 * D + p * LANES:2 * D + (p + 1) * LANES]
            qn = qn * (SCALE * LOG2E)
            for e in range(2):
                h = 2 * p + e
                if e == 1:
                    qe, ke, ve = (pltpu.roll(t, DH, axis=1) for t in (qn, kn, vv))
                else:
                    qe, ke, ve = qn, kn, vv
                f0, f1, f2 = fhi[:, h:h + 1], fmid[:, h:h + 1], flo[:, h:h + 1]
                fq = jnp.where(lane == L_F, f0, jnp.where(lane == L_F + 1, f1, jnp.where(lane == L_F + 2, f2, one_q)))
                fk = jnp.where(lane == L_NF, -f0, jnp.where(lane == L_NF + 1, -f1, jnp.where(lane == L_NF + 2, -f2, one_k)))
                q_ref[h] = jnp.where(lane < DH, qe, fq).astype(BF16)
                k_ref[h] = jnp.where(lane < DH, ke, fk).astype(BF16)
                v_ref[h] = jnp.where(lane < DH, ve, one_v).astype(BF16)

    hs = pl.BlockSpec((H, ts, LANES), lambda i: (0, i, 0))
    vec = pl.BlockSpec((1, LANES), lambda i: (0, 0))
    shp = jax.ShapeDtypeStruct((H, s, LANES), BF16)
    return pl.pallas_call(
        body, name=name, grid=(s // ts,),
        in_specs=[pl.BlockSpec((ts, FOX_NP), lambda i: (i, 0)), vec, vec, vec],
        out_specs=[hs, hs, hs], out_shape=[shp, shp, shp],
        scratch_shapes=[pltpu.VMEM((8, LANES), F32)],
        compiler_params=_cparams(("arbitrary",)),
    )(proj, qg2, kg2, bf)


def _attn_fwd(qa, ka, va, tq, name):
    s = qa.shape[1]
    nq = s // tq

    def body(q_ref, k_ref, v_ref, o_ref, ql_ref):
        i = pl.program_id(1)
        lane = _lane((tq, LANES))
        qs_ = [q_ref[0], q_ref[1]]

        def step(j, carry, masked):
            off = pl.multiple_of(j * tq, tq)
            new = []
            for e in range(2):
                m, acc = carry[e]
                kb = k_ref[e, pl.ds(off, tq), :]
                vb = v_ref[e, pl.ds(off, tq), :]
                sc = _dot_nt(qs_[e], kb)
                if masked:
                    rr = lax.broadcasted_iota(jnp.int32, (tq, tq), 0)
                    cc = lax.broadcasted_iota(jnp.int32, (tq, tq), 1)
                    sc = jnp.where(cc <= rr, sc, -jnp.inf)
                m_new = jnp.maximum(m, jnp.max(sc, axis=-1, keepdims=True))
                pr = jnp.exp2(sc - m_new)
                acc = acc * jnp.exp2(m - m_new) + _dot(pr.astype(BF16), vb)
                new.append((m_new, acc))
            return tuple(new)

        one = (jnp.full((tq, 1), -jnp.inf, F32), jnp.zeros((tq, LANES), F32))
        carry = lax.fori_loop(0, i, functools.partial(step, masked=False), (one, one))
        carry = step(i, carry, True)
        outs = []
        for e in range(2):
            m, acc = carry[e]
            l = acc[:, L_F:L_F + 1]
            outs.append(acc / l)
            lse = m + jnp.log2(l)
            h0, h1, h2 = _split3(-lse)
            ql = jnp.where(lane == L_LSE, h0.astype(F32),
                           jnp.where(lane == L_LSE + 1, h1.astype(F32),
                                     jnp.where(lane == L_LSE + 2, h2.astype(F32), qs_[e].astype(F32))))
            ql_ref[e] = ql.astype(BF16)
        o_ref[...] = jnp.where(lane < DH, outs[0], pltpu.roll(outs[1], DH, axis=1))

    res = pl.BlockSpec((2, s, LANES), lambda p, i: (p, 0, 0))
    qs = pl.BlockSpec((2, tq, LANES), lambda p, i: (p, i, 0))
    return pl.pallas_call(
        body, name=name, grid=(NP, nq),
        in_specs=[qs, res, res],
        out_specs=[pl.BlockSpec((tq, LANES), lambda p, i: (i, p)), qs],
        out_shape=[jax.ShapeDtypeStruct((s, D), F32), jax.ShapeDtypeStruct((H, s, LANES), BF16)],
        compiler_params=_cparams(("arbitrary", "arbitrary")),
    )(qa, ka, va)


def _attn_bwd(ql, ka, va, doa, tq, name):
    s = ql.shape[1]
    nq = s // tq

    def body(q_ref, k_ref, v_ref, do_ref, dqo_ref, dk_ref, dv_ref, dq_ref):
        dq_ref[...] = jnp.zeros_like(dq_ref)
        lane = _lane((tq, LANES))

        def kv_block(j, _):
            joff = pl.multiple_of(j * tq, tq)
            kb = k_ref[0, pl.ds(joff, tq), :]
            vb = v_ref[0, pl.ds(joff, tq), :]

            def step(i, carry, masked):
                dk, dv = carry
                ioff = pl.multiple_of(i * tq, tq)
                qb = q_ref[0, pl.ds(ioff, tq), :]
                dob = do_ref[0, pl.ds(ioff, tq), :]
                pr = jnp.exp2(_dot_nt(qb, kb))
                if masked:
                    rr = lax.broadcasted_iota(jnp.int32, (tq, tq), 0)
                    cc = lax.broadcasted_iota(jnp.int32, (tq, tq), 1)
                    pr = jnp.where(cc <= rr, pr, 0.0)
                ds = (pr * _dot_nt(dob, vb)).astype(BF16)
                dv = dv + _dot_tn(pr.astype(BF16), dob)
                dk = dk + _dot_tn(ds, qb)
                dq_ref[pl.ds(ioff, tq), :] += _dot(ds, kb)
                return dk, dv

            zero = jnp.zeros((tq, LANES), F32)
            carry = step(j, (zero, zero), True)
            dk, dv = lax.fori_loop(j + 1, nq, functools.partial(step, masked=False), carry)
            col = dk[:, L_NF:L_NF + 1]
            hi = col.astype(BF16).astype(F32)
            dk = jnp.where(lane == L_NF, hi, jnp.where(lane == L_NF + 1, col - hi, dk))
            dk_ref[0, pl.ds(joff, tq), :] = dk.astype(BF16)
            dv_ref[0, pl.ds(joff, tq), :] = dv.astype(BF16)
            return 0

        lax.fori_loop(0, nq, kv_block, 0)
        dq = dq_ref[...]
        lane_s = _lane((s, LANES))
        col = dq[:, L_F:L_F + 1]
        hi = col.astype(BF16).astype(F32)
        dqo_ref[0] = jnp.where(lane_s == L_F, hi, jnp.where(lane_s == L_F + 1, col - hi, dq)).astype(BF16)

    hs = pl.BlockSpec((1, s, LANES), lambda h: (h, 0, 0))
    shp = jax.ShapeDtypeStruct((H, s, LANES), BF16)
    return pl.pallas_call(
        body, name=name, grid=(H,),
        in_specs=[hs, hs, hs, hs], out_specs=[hs, hs, hs], out_shape=[shp, shp, shp],
        scratch_shapes=[pltpu.VMEM((s, LANES), F32)],
        compiler_params=_cparams(("arbitrary",)),
    )(ql, ka, va, doa)


def _gate(att, proj, ts, name):
    s = att.shape[0]

    def body(a_ref, o_ref, g_ref):
        g_ref[...] = (a_ref[...] * _sigmoid(o_ref[...])).astype(BF16)

    return pl.pallas_call(
        body, name=name, grid=(s // ts,),
        in_specs=[pl.BlockSpec((ts, D), lambda i: (i, 0)), pl.BlockSpec((ts, D), lambda i: (i, 3))],
        out_specs=pl.BlockSpec((ts, D), lambda i: (i, 0)),
        out_shape=jax.ShapeDtypeStruct((s, D), BF16),
        compiler_params=_cparams(("arbitrary",)),
    )(att, proj)


def _attn_bwd_prep(dgated, att, proj, ts, name):
    s = att.shape[0]

    def body(dg_ref, a_ref, o_ref, doa_ref, dop_ref):
        lane = _lane((ts, LANES))
        bones = _head_block_ones()
        for p in range(NP):
            sl = slice(p * LANES, (p + 1) * LANES)
            dg, a = dg_ref[:, sl], a_ref[:, sl]
            sig = _sigmoid(o_ref[:, sl])
            datt = dg * sig
            dop_ref[:, sl] = (dg * a * sig * (1.0 - sig)).astype(BF16)
            delta = _exact_dot_r(datt * a, bones)
            for e in range(2):
                de, dl = (datt, delta) if e == 0 else (pltpu.roll(datt, DH, axis=1), pltpu.roll(delta, DH, axis=1))
                h0, h1, h2 = _split3(-dl[:, 0:1])
                aug = jnp.where(lane == L_F, h0.astype(F32),
                                jnp.where(lane == L_F + 1, h1.astype(F32),
                                          jnp.where(lane == L_F + 2, h2.astype(F32), 0.0)))
                doa_ref[2 * p + e] = jnp.where(lane < DH, de, aug).astype(BF16)

    row = pl.BlockSpec((ts, D), lambda i: (i, 0))
    return pl.pallas_call(
        body, name=name, grid=(s // ts,),
        in_specs=[row, row, pl.BlockSpec((ts, D), lambda i: (i, 3))],
        out_specs=[pl.BlockSpec((H, ts, LANES), lambda i: (0, i, 0)), row],
        out_shape=[jax.ShapeDtypeStruct((H, s, LANES), BF16), jax.ShapeDtypeStruct((s, D), BF16)],
        compiler_params=_cparams(("arbitrary",)),
    )(dgated, att, proj)


def _fox_post_bwd(proj, dqa, dka, dva, dop, qg2, kg2, bf, ts, name):
    s = proj.shape[0]
    nt = s // ts

    def body(p_ref, dq_ref, dk_ref, dv_ref, dop_ref, qg_ref, kg_ref, bf_ref, o_ref, red_ref, carry):
        @pl.when(pl.program_id(0) == 0)
        def _():
            carry[...] = jnp.zeros_like(carry)
            red_ref[...] = jnp.zeros_like(red_ref)
        lane = _lane((ts, LANES))
        bones = _head_block_ones()
        d_f = jnp.zeros((ts, LANES), F32)
        dqg = jnp.zeros((1, LANES), F32)
        dkg = jnp.zeros((1, LANES), F32)
        for p in range(NP):
            heads = [[ref[2 * p + e].astype(F32) for e in range(2)] for ref in (dq_ref, dk_ref, dv_ref)]
            pair = [jnp.where(lane < DH, a, pltpu.roll(b, DH, axis=1)) for a, b in heads]
            for e in range(2):
                dqe, dke = heads[0][e], heads[1][e]
                col = (dqe[:, L_F:L_F + 1] + dqe[:, L_F + 1:L_F + 2]
                       - dke[:, L_NF:L_NF + 1] - dke[:, L_NF + 1:L_NF + 2])
                d_f = jnp.where(lane == 2 * p + e, col, d_f)
            for idx, (g_ref, base) in enumerate(((qg_ref, 0), (kg_ref, D))):
                x = p_ref[:, base + p * LANES:base + (p + 1) * LANES]
                _, xh, r = _pair_norm(x, g_ref[...], bones)
                dn = pair[idx] * (SCALE if idx == 0 else 1.0 / LOG2E)
                t = dn * g_ref[...]
                mean_txh = _exact_dot_r(t * xh, bones) * (1.0 / DH)
                dx = r * (t - xh * mean_txh)
                o_ref[:, base + p * LANES:base + (p + 1) * LANES] = dx.astype(BF16)
                gsum = jnp.sum(dn * xh, axis=0, keepdims=True)
                if idx == 0:
                    dqg = dqg + gsum
                else:
                    dkg = dkg + gsum
            o_ref[:, 2 * D + p * LANES:2 * D + (p + 1) * LANES] = pair[2].astype(BF16)
        o_ref[:, 3 * D:4 * D] = dop_ref[...]
        rr = lax.broadcasted_iota(jnp.int32, (ts, ts), 0)
        cc = lax.broadcasted_iota(jnp.int32, (ts, ts), 1)
        utri = (cc >= rr).astype(BF16)
        dlogf = _exact_dot(utri, d_f) + carry[0:1, :]
        carry[0:1, :] = dlogf[0:1, :]
        xf = p_ref[:, 4 * D:4 * D + LANES] + bf_ref[...]
        dfl = jnp.where(lane < H, dlogf * _sigmoid(-xf), 0.0)
        o_ref[:, 4 * D:4 * D + LANES] = dfl.astype(BF16)
        red_ref[0:1, :] += dqg
        red_ref[1:2, :] += dkg
        red_ref[2:3, :] += jnp.sum(dfl, axis=0, keepdims=True)

    hs = pl.BlockSpec((H, ts, LANES), lambda i: (0, nt - 1 - i, 0))
    vec = pl.BlockSpec((1, LANES), lambda i: (0, 0))
    return pl.pallas_call(
        body, name=name, grid=(nt,),
        in_specs=[pl.BlockSpec((ts, FOX_NP), lambda i: (nt - 1 - i, 0)), hs, hs, hs,
                  pl.BlockSpec((ts, D), lambda i: (nt - 1 - i, 0)), vec, vec, vec],
        out_specs=[pl.BlockSpec((ts, FOX_NP), lambda i: (nt - 1 - i, 0)),
                   pl.BlockSpec((8, LANES), lambda i: (0, 0))],
        out_shape=[jax.ShapeDtypeStruct((s, FOX_NP), BF16), jax.ShapeDtypeStruct((8, LANES), F32)],
        scratch_shapes=[pltpu.VMEM((8, LANES), F32)],
        compiler_params=_cparams(("arbitrary",)),
    )(proj, dqa, dka, dva, dop, qg2, kg2, bf)


HALO = 16
TS = 512
TQ = 512
TR = 256
TP = 256


def _shift_down(x, k):
    return pltpu.roll(x, k, axis=0)


def _shift_up(x, k):
    return pltpu.roll(x, x.shape[0] - k, axis=0)


def _planes(ref):
    return jnp.concatenate([ref[0].astype(F32), ref[1].astype(F32)], axis=1)


def _conv_gate(a, cw, cb, ts, name):
    s = a.shape[1]
    hb = ts // HALO

    def body(prev_ref, a_ref, cw_ref, cb_ref, f_ref):
        i = pl.program_id(0)
        cwv, cbv = _planes(cw_ref), _planes(cb_ref)
        prev = jnp.where(i > 0, _planes(prev_ref), 0.0)
        ext = jnp.concatenate([prev, _planes(a_ref)], axis=0)
        ap = (_shift_down(ext, 2) * cwv[0:1, :] + _shift_down(ext, 1) * cwv[1:2, :]
              + ext * cwv[2:3, :] + cbv)[HALO:, :]
        g, val = ap[:, :GT], ap[:, GT:]
        f_ref[...] = (g * _sigmoid(g) * val).astype(BF16)

    return pl.pallas_call(
        body, name=name, grid=(s // ts, NGT),
        in_specs=[pl.BlockSpec((2, HALO, GT), lambda i, j: (0, jnp.maximum(i * hb - 1, 0), j)),
                  pl.BlockSpec((2, ts, GT), lambda i, j: (0, i, j)),
                  pl.BlockSpec((2, 8, GT), lambda i, j: (0, 0, j)),
                  pl.BlockSpec((2, 1, GT), lambda i, j: (0, 0, j))],
        out_specs=pl.BlockSpec((ts, GT), lambda i, j: (i, j)),
        out_shape=jax.ShapeDtypeStruct((s, DFF), BF16),
        compiler_params=_cparams(("arbitrary", "arbitrary")),
    )(a, a, cw, cb)


def _conv_gate_bwd(a, df, cw, cb, ts, name):
    s = a.shape[1]
    hb = ts // HALO
    nt = s // ts

    def body(prev_ref, a_ref, next_ref, df_ref, dfn_ref, cw_ref, cb_ref, da_ref, red_ref):
        i = pl.program_id(1)

        @pl.when(i == 0)
        def _():
            red_ref[...] = jnp.zeros_like(red_ref)
        cwv, cbv = _planes(cw_ref), _planes(cb_ref)
        prev = jnp.where(i > 0, _planes(prev_ref), 0.0)
        ext = jnp.concatenate([prev, _planes(a_ref), _planes(next_ref)], axis=0)
        dfn = jnp.where(i < nt - 1, dfn_ref[...].astype(F32), 0.0)
        dfe = jnp.concatenate([jnp.zeros((HALO, GT), F32), df_ref[...].astype(F32), dfn], axis=0)
        am2, am1 = _shift_down(ext, 2), _shift_down(ext, 1)
        ap = am2 * cwv[0:1, :] + am1 * cwv[1:2, :] + ext * cwv[2:3, :] + cbv
        g, val = ap[:, :GT], ap[:, GT:]
        sg = _sigmoid(g)
        dap = jnp.concatenate([dfe * val * (sg * (1.0 + g * (1.0 - sg))), dfe * (g * sg)], axis=1)
        da = dap * cwv[2:3, :] + _shift_up(dap, 1) * cwv[1:2, :] + _shift_up(dap, 2) * cwv[0:1, :]
        main = slice(HALO, HALO + ts)
        sums = [jnp.sum((t * dap)[main], axis=0, keepdims=True) for t in (am2, am1, ext)]
        sums.append(jnp.sum(dap[main], axis=0, keepdims=True))
        for e in range(2):
            cols = slice(e * GT, (e + 1) * GT)
            da_ref[e] = da[main, cols].astype(BF16)
            for r, sm in enumerate(sums):
                red_ref[e, r:r + 1, :] += sm[:, cols]

    nhb = s // HALO
    return pl.pallas_call(
        body, name=name, grid=(NGT, nt),
        in_specs=[pl.BlockSpec((2, HALO, GT), lambda j, i: (0, jnp.maximum(i * hb - 1, 0), j)),
                  pl.BlockSpec((2, ts, GT), lambda j, i: (0, i, j)),
                  pl.BlockSpec((2, HALO, GT), lambda j, i: (0, jnp.minimum((i + 1) * hb, nhb - 1), j)),
                  pl.BlockSpec((ts, GT), lambda j, i: (i, j)),
                  pl.BlockSpec((HALO, GT), lambda j, i: (jnp.minimum((i + 1) * hb, nhb - 1), j)),
                  pl.BlockSpec((2, 8, GT), lambda j, i: (0, 0, j)),
                  pl.BlockSpec((2, 1, GT), lambda j, i: (0, 0, j))],
        out_specs=[pl.BlockSpec((2, ts, GT), lambda j, i: (0, i, j)),
                   pl.BlockSpec((2, 8, GT), lambda j, i: (0, 0, j))],
        out_shape=[jax.ShapeDtypeStruct((2, s, DFF), BF16), jax.ShapeDtypeStruct((2, 8, DFF), F32)],
        compiler_params=_cparams(("arbitrary", "arbitrary")),
    )(a, a, a, df, df, cw, cb)


def _chunk_mask(transposed=False):
    t = lax.broadcasted_iota(jnp.int32, (SGB, SGB), 0) // CHUNK
    u = lax.broadcasted_iota(jnp.int32, (SGB, SGB), 1) // CHUNK
    return (t <= u) if transposed else (u <= t)


def _sgu_ln(zv, gain, bias):
    v = _gelu(zv)
    mu = jnp.mean(v, axis=-1, keepdims=True)
    vc = v - mu
    rstd = lax.rsqrt(jnp.mean(vc * vc, axis=-1, keepdims=True) + EPS)
    vhat = vc * rstd
    return vhat * gain + bias, vhat, rstd


def _sgu_fwd(z, vgain, vbias, ws, bst, tr, name):
    s = z.shape[0]

    def body(zu_ref, zv_ref, vg_ref, vb_ref, ws_ref, bs_ref, y_ref):
        u = _gelu(zu_ref[...].astype(F32))
        vn, _, _ = _sgu_ln(zv_ref[...].astype(F32), vg_ref[...], vb_ref[...])
        vn = vn.astype(BF16)
        mask = _chunk_mask()
        for g in range(SGG):
            w = jnp.where(mask, ws_ref[g], 0.0).astype(BF16)
            for b in range(tr // SGB):
                rs, cs = slice(b * SGB, (b + 1) * SGB), slice(g * SGC, (g + 1) * SGC)
                mixed = _dot(w, vn[rs, cs]) + bs_ref[:, g:g + 1]
                y_ref[rs, cs] = (u[rs, cs] * mixed).astype(BF16)

    vec = pl.BlockSpec((1, SGW), lambda i: (0, 0))
    return pl.pallas_call(
        body, name=name, grid=(s // tr,),
        in_specs=[pl.BlockSpec((tr, SGW), lambda i: (i, 0)), pl.BlockSpec((tr, SGW), lambda i: (i, 1)),
                  vec, vec, pl.BlockSpec((SGG, SGB, SGB), lambda i: (0, 0, 0)),
                  pl.BlockSpec((SGB, LANES), lambda i: (0, 0))],
        out_specs=pl.BlockSpec((tr, SGW), lambda i: (i, 0)),
        out_shape=jax.ShapeDtypeStruct((s, SGW), BF16),
        compiler_params=_cparams(("arbitrary",)),
    )(z, z, vgain, vbias, ws, bst)


def _sgu_bwd(z, dy, vgain, vbias, ws, wst, bst, tr, name):
    s = z.shape[0]

    def body(zu_ref, zv_ref, dy_ref, vg_ref, vb_ref, ws_ref, wst_ref, bs_ref,
             dz_ref, rb_ref, rv_ref, dws_ref, dbs_ref, dvn_s):
        @pl.when(pl.program_id(0) == 0)
        def _():
            rb_ref[...] = jnp.zeros_like(rb_ref)
            rv_ref[...] = jnp.zeros_like(rv_ref)
            dws_ref[...] = jnp.zeros_like(dws_ref)
            dbs_ref[...] = jnp.zeros_like(dbs_ref)
        zu = zu_ref[...].astype(F32)
        zv = zv_ref[...].astype(F32)
        u = _gelu(zu)
        vn, vhat, rstd = _sgu_ln(zv, vg_ref[...], vb_ref[...])
        vnb = vn.astype(BF16)
        dyv = dy_ref[...].astype(F32)
        dmix = (dyv * u).astype(BF16)
        mask = _chunk_mask()
        mask_t = _chunk_mask(transposed=True)
        lane = _lane((SGB, LANES))
        dbs = jnp.zeros((SGB, LANES), F32)
        for g in range(SGG):
            w = jnp.where(mask, ws_ref[g], 0.0).astype(BF16)
            wt = jnp.where(mask_t, wst_ref[g], 0.0).astype(BF16)
            dw = jnp.zeros((SGB, SGB), F32)
            for b in range(tr // SGB):
                rs, cs = slice(b * SGB, (b + 1) * SGB), slice(g * SGC, (g + 1) * SGC)
                mixed = _dot(w, vnb[rs, cs]) + bs_ref[:, g:g + 1]
                dz_ref[rs, cs] = (dyv[rs, cs] * mixed * _gelu_grad(zu[rs, cs])).astype(BF16)
                dm = dmix[rs, cs]
                dw = dw + _dot_nt(dm, vnb[rs, cs])
                dbs = dbs + jnp.where(lane == g, jnp.sum(dm.astype(F32), axis=-1, keepdims=True), 0.0)
                dvn_s[rs, cs] = _dot(wt, dm)
            dws_ref[g] += jnp.where(mask, dw, 0.0)
        dbs_ref[...] += dbs
        dvn = dvn_s[...]
        rv_ref[0:1, :] += jnp.sum(dvn * vhat, axis=0, keepdims=True)
        rv_ref[1:2, :] += jnp.sum(dvn, axis=0, keepdims=True)
        dvh = dvn * vg_ref[...]
        dv = rstd * (dvh - jnp.mean(dvh, axis=-1, keepdims=True)
                     - vhat * jnp.mean(dvh * vhat, axis=-1, keepdims=True))
        dz_ref[:, SGW:] = (dv * _gelu_grad(zv)).astype(BF16)
        dzf = dz_ref[...].astype(F32)
        rb_ref[0:1, :] += jnp.sum(dzf, axis=0, keepdims=True)

    vec = pl.BlockSpec((1, SGW), lambda i: (0, 0))
    wsp = pl.BlockSpec((SGG, SGB, SGB), lambda i: (0, 0, 0))
    return pl.pallas_call(
        body, name=name, grid=(s // tr,),
        in_specs=[pl.BlockSpec((tr, SGW), lambda i: (i, 0)), pl.BlockSpec((tr, SGW), lambda i: (i, 1)),
                  pl.BlockSpec((tr, SGW), lambda i: (i, 0)), vec, vec, wsp, wsp,
                  pl.BlockSpec((SGB, LANES), lambda i: (0, 0))],
        out_specs=[pl.BlockSpec((tr, 2 * SGW), lambda i: (i, 0)),
                   pl.BlockSpec((8, 2 * SGW), lambda i: (0, 0)),
                   pl.BlockSpec((8, SGW), lambda i: (0, 0)), wsp,
                   pl.BlockSpec((SGB, LANES), lambda i: (0, 0))],
        out_shape=[jax.ShapeDtypeStruct((s, 2 * SGW), BF16), jax.ShapeDtypeStruct((8, 2 * SGW), F32),
                   jax.ShapeDtypeStruct((8, SGW), F32), jax.ShapeDtypeStruct((SGG, SGB, SGB), F32),
                   jax.ShapeDtypeStruct((SGB, LANES), F32)],
        scratch_shapes=[pltpu.VMEM((tr, SGW), F32)],
        compiler_params=_cparams(("arbitrary",)),
    )(z, z, dy, vgain, vbias, ws, wst, bst)


def _final_loss(x, fg, tgt, gprev, yprev, ts, name):
    s, d = x.shape

    def body(x_ref, fg_ref, t_ref, g_ref, y_ref, l_ref, dx_ref, dy_ref, red_ref):
        @pl.when(pl.program_id(0) == 0)
        def _():
            l_ref[...] = jnp.zeros_like(l_ref)
            red_ref[...] = jnp.zeros_like(red_ref)
        xv = x_ref[...]
        r = _rstd_rows(xv)
        xh = xv * r
        err = xh * fg_ref[...] - t_ref[...]
        l_ref[...] += 0.5 * jnp.sum(jnp.mean(err * err, axis=-1, keepdims=True))
        dyo = err * (1.0 / d)
        dxh = dyo * fg_ref[...]
        dx = r * (dxh - xh * jnp.mean(dxh * xh, axis=-1, keepdims=True))
        dx_ref[...] = dx
        dy_ref[...] = (dx * g_ref[...]).astype(BF16)
        red_ref[0:1, :] += jnp.sum(dyo * xh, axis=0, keepdims=True)
        red_ref[1:2, :] += jnp.sum(dx * y_ref[...].astype(F32), axis=0, keepdims=True)

    row = pl.BlockSpec((ts, d), lambda i: (i, 0))
    vec = pl.BlockSpec((1, d), lambda i: (0, 0))
    return pl.pallas_call(
        body, name=name, grid=(s // ts,),
        in_specs=[row, vec, row, vec, row],
        out_specs=[pl.BlockSpec((8, LANES), lambda i: (0, 0)), row, row, pl.BlockSpec((8, d), lambda i: (0, 0))],
        out_shape=[jax.ShapeDtypeStruct((8, LANES), F32), jax.ShapeDtypeStruct((s, d), F32),
                   jax.ShapeDtypeStruct((s, d), BF16), jax.ShapeDtypeStruct((8, d), F32)],
        compiler_params=_cparams(("arbitrary",)),
    )(x, fg, tgt, gprev, yprev)


def _norm_bwd(xin, dh, dxout, ng, sc, gprev, yprev, ts, name):
    s, d = xin.shape
    has_prev = gprev is not None

    def body(*refs):
        if has_prev:
            x_ref, dh_ref, dxo_ref, ng_ref, sc_ref, g_ref, y_ref, dx_ref, dy_ref, red_ref = refs
        else:
            x_ref, dh_ref, dxo_ref, ng_ref, sc_ref, dx_ref, red_ref = refs

        @pl.when(pl.program_id(0) == 0)
        def _():
            red_ref[...] = jnp.zeros_like(red_ref)
        xv = x_ref[...]
        r = _rstd_rows(xv)
        xh = xv * r
        dhv = dh_ref[...]
        dr = dhv * (1.0 + sc_ref[...])
        t = dr * ng_ref[...]
        dx = dxo_ref[...] + r * (t - xh * jnp.mean(t * xh, axis=-1, keepdims=True))
        dx_ref[...] = dx
        red_ref[0:1, :] += jnp.sum(dhv, axis=0, keepdims=True)
        red_ref[1:2, :] += jnp.sum(dhv * (xh * ng_ref[...]), axis=0, keepdims=True)
        red_ref[2:3, :] += jnp.sum(dr * xh, axis=0, keepdims=True)
        if has_prev:
            dy_ref[...] = (dx * g_ref[...]).astype(BF16)
            red_ref[3:4, :] += jnp.sum(dx * y_ref[...].astype(F32), axis=0, keepdims=True)

    row = pl.BlockSpec((ts, d), lambda i: (i, 0))
    vec = pl.BlockSpec((1, d), lambda i: (0, 0))
    red = pl.BlockSpec((8, d), lambda i: (0, 0))
    if has_prev:
        in_specs, args = [row, row, row, vec, vec, vec, row], (xin, dh, dxout, ng, sc, gprev, yprev)
        out_specs = [row, row, red]
        out_shape = [jax.ShapeDtypeStruct((s, d), F32), jax.ShapeDtypeStruct((s, d), BF16),
                     jax.ShapeDtypeStruct((8, d), F32)]
    else:
        in_specs, args = [row, row, row, vec, vec], (xin, dh, dxout, ng, sc)
        out_specs = [row, red]
        out_shape = [jax.ShapeDtypeStruct((s, d), F32), jax.ShapeDtypeStruct((8, d), F32)]
    return pl.pallas_call(
        body, name=name, grid=(s // ts,), in_specs=in_specs, out_specs=out_specs, out_shape=out_shape,
        compiler_params=_cparams(("arbitrary",)),
    )(*args)


def _ada_mod(c_all, ada_w, ada_b):
    nb = c_all.shape[0]
    da = ada_w.shape[2]

    def body(c_ref, w_ref, b_ref, o_ref, ca_ref):
        cv = c_ref[...]
        ca = cv * _sigmoid(cv)
        ca_ref[...] = ca
        o_ref[0] = lax.dot_general(ca, w_ref[0], (((1,), (0,)), ((), ())), precision=lax.Precision.HIGHEST,
                                   preferred_element_type=F32) + b_ref[0]

    return pl.pallas_call(
        body, name="ada_mod", grid=(2,),
        in_specs=[pl.BlockSpec((nb, D), lambda i: (0, 0)), pl.BlockSpec((1, D, da), lambda i: (i, 0, 0)),
                  pl.BlockSpec((1, 1, da), lambda i: (i, 0, 0))],
        out_specs=[pl.BlockSpec((1, nb, da), lambda i: (i, 0, 0)), pl.BlockSpec((nb, D), lambda i: (0, 0))],
        out_shape=[jax.ShapeDtypeStruct((2, nb, da), F32), jax.ShapeDtypeStruct((nb, D), F32)],
        compiler_params=_cparams(("arbitrary",)),
    )(c_all, ada_w, ada_b)


def _ada_w_grad(c_act_t, dmod):
    nb = c_act_t.shape[1]
    da = dmod.shape[2]
    tn = 512

    def body(c_ref, d_ref, o_ref):
        acc = c_ref[:, 0:1] * d_ref[0, 0:1, :]
        for b in range(1, nb):
            acc = acc + c_ref[:, b:b + 1] * d_ref[0, b:b + 1, :]
        o_ref[0] = acc

    return pl.pallas_call(
        body, name="ada_w_grad", grid=(2, da // tn),
        in_specs=[pl.BlockSpec((D, nb), lambda i, j: (0, 0)), pl.BlockSpec((1, nb, tn), lambda i, j: (i, 0, j))],
        out_specs=pl.BlockSpec((1, D, tn), lambda i, j: (i, 0, j)),
        out_shape=jax.ShapeDtypeStruct((2, D, da), F32),
        compiler_params=_cparams(("arbitrary", "arbitrary")),
    )(c_act_t, dmod)


def _conv_planes(cw, cb):
    cwp = jnp.swapaxes(cw.reshape(3, 2, DFF), 0, 1)
    return jnp.pad(cwp, ((0, 0), (0, 5), (0, 0))), cb.reshape(2, 1, DFF)


def _local_step(x, tgt, mod, wts, small):
    s = x.shape[0]
    ts, tq, tr, tp = TS, TQ, TR, TP
    zb = lambda n: jnp.zeros((1, n), F32)
    m6 = mod.reshape(2, 6, 1, D)
    sh1, sc1, g1, sh2, sc2, g2 = ([m6[i, k] for i in range(2)] for k in range(6))
    n1g, n2g = small["norm1_g"], small["norm2_g"]
    row = lambda a, i: a[i:i + 1]

    qg2 = jnp.tile(small["fox_q_gain"], (1, 2))
    kg2 = jnp.tile(small["fox_k_gain"], (1, 2))
    bfp = jnp.pad(small["fox_b_f"], ((0, 0), (0, LANES - H)))
    proj, h1 = _norm_mod_matmul(x, row(n1g, 0), sc1[0], sh1[0], wts["fox_w_in"], zb(FOX_NP), F32, ts, 1408, "fox_in")
    qa, ka, va = _fox_post(proj, qg2, kg2, bfp, tp, "fox_post")
    att, ql = _attn_fwd(qa, ka, va, tq, "attn_fwd")
    gated = _gate(att, proj, ts, "fox_gate")
    x1, y0 = _matmul_residual(gated, wts["fox_w_out"], x, g1[0], ts, "fox_out")

    def ffn_fwd(xin, i, tag):
        cw, cb = _conv_planes(small["ffn_conv_w"][i], small["ffn_conv_b"][i])
        a, h = _norm_mod_matmul(xin, row(n2g, i), sc2[i], sh2[i], wts["ffn_w_up"][i], zb(2 * DFF), BF16, ts, 1408,
                                "ffn_up" + tag, planes=2)
        f = _conv_gate(a, cw, cb, ts, "ffn_conv" + tag)
        xo, y = _matmul_residual(f, wts["ffn_w_down"][i], xin, g2[i], ts, "ffn_down" + tag)
        return xo, (a, h, f, y, cw, cb)

    x2, ffn0 = ffn_fwd(x1, 0, "0")

    bst = jnp.pad(small["sgu_b_s"].T, ((0, 0), (0, LANES - SGG)))
    ws = small["sgu_w_s"]
    z, h3 = _norm_mod_matmul(x2, row(n1g, 1), sc1[1], sh1[1], wts["sgu_w_in"], small["sgu_b_in"], BF16, ts, 1024,
                             "sgu_in")
    yy = _sgu_fwd(z, small["sgu_v_gain"], small["sgu_v_bias"], ws, bst, tr, "sgu_mix")
    x3, y1 = _matmul_residual(yy, wts["sgu_w_out"], x2, g1[1], ts, "sgu_out")
    x4, ffn1 = ffn_fwd(x3, 1, "1")

    lsum, dx4, dy, redf = _final_loss(x4, small["final_g"], tgt, g2[1], ffn1[3], ts, "final_loss")
    grads = {"final_g": redf[0]}
    dmod = [[None] * 6, [None] * 6]
    dmod[1][5] = redf[1]

    def ffn_bwd(dxo, dy2, xin, i, saved, gprev, yprev, tag):
        a, h, f, _, cw, cb = saved
        wd, wu = wts["ffn_w_down"][i], wts["ffn_w_up"][i]
        g_wd = _matmul(f, dy2, True, False, 1408, D, ts, BF16, "ffn_dwdown" + tag)
        df = _matmul(dy2, wd, False, True, ts, 1408, D, BF16, "ffn_df" + tag)
        da, redc = _conv_gate_bwd(a, df, cw, cb, ts, "ffn_conv_bwd" + tag)
        g_wu = _matmul(h, da, True, False, D, 1408, ts, BF16, "ffn_dwup" + tag, out_parts=N_CHIP)
        dh = _matmul(da, wu, False, True, ts, D, 1408, F32, "ffn_dh" + tag)
        outs = _norm_bwd(xin, dh, dxo, row(n2g, i), sc2[i], gprev, yprev, ts, "ffn_norm_bwd" + tag)
        return outs, g_wd, g_wu, redc

    (dx3, dy1, red), g_wd1, g_wu1, redc1 = ffn_bwd(dx4, dy, x3, 1, ffn1, g1[1], y1, "1")
    dmod[1][3], dmod[1][4], dn2g1, dmod[1][2] = red[0], red[1], red[2], red[3]

    g_swo = _matmul(yy, dy1, True, False, 1024, D, ts, BF16, "sgu_dwout")
    dyy = _matmul(dy1, wts["sgu_w_out"], False, True, ts, 1024, D, BF16, "sgu_dyy")
    wst = jnp.swapaxes(ws, 1, 2)
    dz, rb, rv, dws, dbst = _sgu_bwd(z, dyy, small["sgu_v_gain"], small["sgu_v_bias"], ws, wst, bst, tr, "sgu_mix_bwd")
    g_swi = _matmul(h3, dz, True, False, D, 1024, ts, BF16, "sgu_dwin", out_parts=N_CHIP)
    dh3 = _matmul(dz, wts["sgu_w_in"], False, True, ts, D, 1024, F32, "sgu_dh")
    dx2, dy2_0, red = _norm_bwd(x2, dh3, dx3, row(n1g, 1), sc1[1], g2[0], ffn0[3], ts, "sgu_norm_bwd")
    dmod[1][0], dmod[1][1], dn1g1, dmod[0][5] = red[0], red[1], red[2], red[3]

    (dx1, dy0, red), g_wd0, g_wu0, redc0 = ffn_bwd(dx2, dy2_0, x1, 0, ffn0, g1[0], y0, "0")
    dmod[0][3], dmod[0][4], dn2g0, dmod[0][2] = red[0], red[1], red[2], red[3]

    g_fwo = _matmul(gated, dy0, True, False, D, D, ts, BF16, "fox_dwout")
    dgated = _matmul(dy0, wts["fox_w_out"], False, True, ts, D, D, F32, "fox_dgated")
    doa, dop = _attn_bwd_prep(dgated, att, proj, ts, "attn_bwd_prep")
    dqa, dka, dva = _attn_bwd(ql, ka, va, doa, tq, "attn_bwd")
    dproj, redx = _fox_post_bwd(proj, dqa, dka, dva, dop, qg2, kg2, bfp, tp, "fox_post_bwd")
    g_fwi = _matmul(h1, dproj, True, False, D, 1408, ts, BF16, "fox_dwin")
    dh1 = _matmul(dproj, wts["fox_w_in"], False, True, ts, D, 1408, F32, "fox_dh")
    dx0, red = _norm_bwd(x, dh1, dx1, row(n1g, 0), sc1[0], None, None, ts, "fox_norm_bwd")
    dmod[0][0], dmod[0][1], dn1g0 = red[0], red[1], red[2]

    grads.update(
        fox_w_in=g_fwi, fox_w_out=g_fwo, sgu_w_in=g_swi, sgu_w_out=g_swo,
        ffn_w_up=[g_wu0, g_wu1], ffn_w_down=[g_wd0, g_wd1],
        fox_q_gain=redx[0, :DH] + redx[0, DH:], fox_k_gain=redx[1, :DH] + redx[1, DH:], fox_b_f=redx[2, :H],
        sgu_b_in=rb[0], sgu_v_gain=rv[0], sgu_v_bias=rv[1], sgu_w_s=dws, sgu_b_s=dbst[:, :SGG].T,
        ffn_conv_w=jnp.stack([jnp.swapaxes(r[:, 0:3], 0, 1).reshape(3, 2 * DFF) for r in (redc0, redc1)]),
        ffn_conv_b=jnp.stack([r[:, 3].reshape(2 * DFF) for r in (redc0, redc1)]),
        norm1_g=jnp.stack([dn1g0, dn1g1]), norm2_g=jnp.stack([dn2g0, dn2g1]),
    )
    dmod_arr = jnp.stack([jnp.concatenate(dmod[0]), jnp.concatenate(dmod[1])])
    return lsum[0, 0], dx0, grads, dmod_arr


N_DEV = 8
N_CHIP = 4
HBM_SPEC = pl.BlockSpec(memory_space=pltpu.HBM)
VMEM_SPEC = pl.BlockSpec(memory_space=pltpu.VMEM)


def _mesh_pos():
    return lax.axis_index("x"), lax.axis_index("y"), lax.axis_index("c")


def _other_chips(x, y):
    return [(1 - x, y), (x, 1 - y), (1 - x, 1 - y)]


def _remote(src, dst, ssem, rsem, dev):
    return pltpu.make_async_remote_copy(src_ref=src, dst_ref=dst, send_sem=ssem, recv_sem=rsem,
                                        device_id=dev, device_id_type=MESH)


def _allgather8(xb, name):
    m_per, n = xb.shape

    def body(x_ref, out_ref, send_sems, recv_sems, local_sem):
        x, y, c = _mesh_pos()
        me, sibling = (x, y, c), (x, y, 1 - c)
        chips = _other_chips(x, y)

        def rows(px, py, pc):
            return out_ref.at[pl.ds((4 * px + 2 * py + pc) * m_per, m_per), :]

        def copy(k, block, to, src=None):
            return _remote(rows(*block) if src is None else src, rows(*block),
                           send_sems.at[k], recv_sems.at[k], to)

        mine = pltpu.make_async_copy(x_ref, rows(*me), local_sem)
        mine.start()
        first = [copy(0, me, sibling, src=x_ref)]
        first += [copy(1 + j, me, (*chip, c), src=x_ref) for j, chip in enumerate(chips)]
        for cp in first:
            cp.start()
        passed = [copy(4 + j, (*chip, c), sibling) for j, chip in enumerate(chips)]
        for j, chip in enumerate(chips):
            copy(1 + j, (*chip, c), me).wait_recv()
            passed[j].start()
        copy(0, sibling, me).wait_recv()
        for j, chip in enumerate(chips):
            copy(4 + j, (*chip, 1 - c), me).wait_recv()
        for cp in first + passed:
            cp.wait_send()
        mine.wait()

    return pl.pallas_call(
        body, name=name,
        out_shape=jax.ShapeDtypeStruct((N_DEV * m_per, n), xb.dtype),
        in_specs=[VMEM_SPEC], out_specs=VMEM_SPEC,
        scratch_shapes=[pltpu.SemaphoreType.DMA((7,)), pltpu.SemaphoreType.DMA((7,)), pltpu.SemaphoreType.DMA],
        compiler_params=pltpu.CompilerParams(vmem_limit_bytes=V7X_VMEM_LIMIT),
    )(xb)


def _gather_shards(shards, name):
    na = len(shards)

    def body(*refs):
        p_refs, o_refs = refs[:na], refs[na:2 * na]
        send_sems, recv_sems, pass_send, pass_recv = refs[2 * na:]
        x, y, c = _mesh_pos()
        me = 2 * x + y
        sibling = (x, y, 1 - c)
        chips = _other_chips(x, y)

        def half(a, ci, hf):
            rh = shards[a].shape[0] // 2
            return o_refs[a].at[ci, pl.ds(hf * rh, rh), :]

        sends = []
        for a in range(na):
            rh = shards[a].shape[0] // 2
            for k, chip in enumerate(chips):
                sends.append(_remote(p_refs[a].at[pl.ds(c * rh, rh), :], half(a, me, c),
                                     send_sems.at[3 * a + k], recv_sems.at[3 * a + k], (*chip, c)))
        for cp in sends:
            cp.start()
        passed = []
        for a in range(na):
            for k, chip in enumerate(chips):
                ci = 2 * chip[0] + chip[1]
                _remote(half(a, ci, c), half(a, ci, c), send_sems.at[3 * a + k], recv_sems.at[3 * a + k],
                        (*chip, c)).wait_recv()
                cp = _remote(half(a, ci, c), half(a, ci, c), pass_send.at[3 * a + k], pass_recv.at[3 * a + k], sibling)
                cp.start()
                passed.append(cp)
        for a in range(na):
            for k, chip in enumerate(chips):
                ci = 2 * chip[0] + chip[1]
                _remote(half(a, ci, 1 - c), half(a, ci, 1 - c), pass_send.at[3 * a + k], pass_recv.at[3 * a + k],
                        sibling).wait_recv()
        for cp in sends + passed:
            cp.wait_send()

    return pl.pallas_call(
        body, name=name,
        out_shape=[jax.ShapeDtypeStruct((N_CHIP,) + p.shape, p.dtype) for p in shards],
        in_specs=[HBM_SPEC] * na, out_specs=[HBM_SPEC] * na,
        scratch_shapes=[pltpu.SemaphoreType.DMA((3 * na,))] * 4,
    )(*shards)


def _rs_to_sibling(gs, name):
    na = len(gs)

    def body(*refs):
        g_refs, o_refs, ssems, rsems = refs[:na], refs[na:2 * na], refs[2 * na], refs[2 * na + 1]
        x, y, c = _mesh_pos()
        cps = []
        for a in range(na):
            rh = gs[a].shape[1] // 2
            cp = _remote(g_refs[a].at[:, pl.ds((1 - c) * rh, rh), :], o_refs[a], ssems.at[a], rsems.at[a],
                         (x, y, 1 - c))
            cp.start()
            cps.append(cp)
        for cp in cps:
            cp.wait()

    return pl.pallas_call(
        body, name=name,
        out_shape=[jax.ShapeDtypeStruct((g.shape[0], g.shape[1] // 2, g.shape[2]), g.dtype) for g in gs],
        in_specs=[HBM_SPEC] * na, out_specs=[HBM_SPEC] * na,
        scratch_shapes=[pltpu.SemaphoreType.DMA((na,)), pltpu.SemaphoreType.DMA((na,))],
    )(*gs)


def _rs_chip_sum(g, sib, c_arr, tr, name):
    nc, r, n = g.shape
    rh = r // 2
    g4 = g.reshape(nc, 2, rh, n)

    def body(c_ref, g_ref, s_ref, o_ref):
        o_ref[...] = (g_ref[0].astype(F32) + s_ref[...].astype(F32)).astype(BF16)

    return pl.pallas_call(
        body, name=name, out_shape=jax.ShapeDtypeStruct((nc, rh, n), BF16),
        grid_spec=pltpu.PrefetchScalarGridSpec(
            num_scalar_prefetch=1, grid=(nc, rh // tr),
            in_specs=[pl.BlockSpec((1, 1, tr, n), lambda j, i, cr: (j, cr[0], i, 0)),
                      pl.BlockSpec((1, tr, n), lambda j, i, cr: (j, i, 0))],
            out_specs=pl.BlockSpec((1, tr, n), lambda j, i, cr: (j, i, 0))),
        compiler_params=_cparams(("arbitrary", "arbitrary")),
    )(c_arr, g4, sib)


def _rs_across_chips(css, name):
    na = len(css)

    def body(*refs):
        cs_refs, o_refs, send_sems, recv_sems = refs[:na], refs[na:2 * na], refs[2 * na], refs[2 * na + 1]
        x, y, c = _mesh_pos()
        cps = []
        for a in range(na):
            for k, chip in enumerate(_other_chips(x, y)):
                ci = 2 * chip[0] + chip[1]
                cp = _remote(cs_refs[a].at[ci], o_refs[a].at[k], send_sems.at[3 * a + k], recv_sems.at[3 * a + k],
                             (*chip, c))
                cp.start()
                cps.append(cp)
        for cp in cps:
            cp.wait()

    return pl.pallas_call(
        body, name=name, out_shape=[jax.ShapeDtypeStruct((3,) + cs.shape[1:], cs.dtype) for cs in css],
        in_specs=[HBM_SPEC] * na, out_specs=[HBM_SPEC] * na,
        scratch_shapes=[pltpu.SemaphoreType.DMA((3 * na,)), pltpu.SemaphoreType.DMA((3 * na,))],
    )(*css)


def _rs_final_sum(cs, rcv, me_arr, tr, name):
    nc, rh, n = cs.shape

    def body(m_ref, c_ref, r_ref, o_ref):
        acc = c_ref[0].astype(F32)
        for k in range(3):
            acc = acc + r_ref[k].astype(F32)
        o_ref[...] = acc

    return pl.pallas_call(
        body, name=name, out_shape=jax.ShapeDtypeStruct((rh, n), F32),
        grid_spec=pltpu.PrefetchScalarGridSpec(
            num_scalar_prefetch=1, grid=(rh // tr,),
            in_specs=[pl.BlockSpec((1, tr, n), lambda i, mr: (mr[0], i, 0)),
                      pl.BlockSpec((3, tr, n), lambda i, mr: (0, i, 0))],
            out_specs=pl.BlockSpec((tr, n), lambda i, mr: (i, 0))),
        compiler_params=_cparams(("arbitrary",)),
    )(me_arr, cs, rcv)


def _rs_swap_halves(halves, name):
    na = len(halves)

    def body(*refs):
        h_refs, o_refs, ssems, rsems = refs[:na], refs[na:2 * na], refs[2 * na], refs[2 * na + 1]
        x, y, c = _mesh_pos()
        cps = []
        for a in range(na):
            cp = _remote(h_refs[a], o_refs[a], ssems.at[a], rsems.at[a], (x, y, 1 - c))
            cp.start()
            cps.append(cp)
        for cp in cps:
            cp.wait()

    return pl.pallas_call(
        body, name=name, out_shape=[jax.ShapeDtypeStruct(h.shape, h.dtype) for h in halves],
        in_specs=[HBM_SPEC] * na, out_specs=[HBM_SPEC] * na,
        scratch_shapes=[pltpu.SemaphoreType.DMA((na,)), pltpu.SemaphoreType.DMA((na,))],
    )(*halves)


def _sum8(g, name):
    nd, r, n = g.shape

    def body(g_ref, o_ref):
        acc = g_ref[0]
        for k in range(1, nd):
            acc = acc + g_ref[k]
        o_ref[...] = acc

    return pl.pallas_call(
        body, name=name, grid=(r // 8,),
        in_specs=[pl.BlockSpec((nd, 8, n), lambda i: (0, i, 0))],
        out_specs=pl.BlockSpec((8, n), lambda i: (i, 0)),
        out_shape=jax.ShapeDtypeStruct((r, n), F32),
        compiler_params=_cparams(("arbitrary",)),
    )(g)


def _adamw(w, g, m, v, name):
    r, n = w.shape
    tr = 128 if r % 128 == 0 else 8
    bc1 = 1.0 - ADAM_B1 ** ADAM_STEP
    bc2 = 1.0 - ADAM_B2 ** ADAM_STEP

    def body(w_ref, g_ref, m_ref, v_ref, d_ref, mo_ref, vo_ref):
        gv = g_ref[...]
        mn = ADAM_B1 * m_ref[...] + (1.0 - ADAM_B1) * gv
        vn = ADAM_B2 * v_ref[...] + (1.0 - ADAM_B2) * (gv * gv)
        d_ref[...] = -ADAM_LR * ((mn / bc1) / (jnp.sqrt(vn / bc2) + ADAM_EPS) + ADAM_WD * w_ref[...])
        mo_ref[...] = mn
        vo_ref[...] = vn

    blk = pl.BlockSpec((tr, n), lambda i: (i, 0))
    shp = jax.ShapeDtypeStruct((r, n), F32)
    return pl.pallas_call(
        body, name=name, grid=(r // tr,), in_specs=[blk] * 4, out_specs=[blk] * 3, out_shape=[shp] * 3,
        compiler_params=_cparams(("arbitrary",)),
    )(w, g, m, v)


ROW = 1024
PACK_ROWS = 7168
BIG = ("fox_w_in", "fox_w_out", "sgu_w_in", "sgu_w_out", "ffn_w_up", "ffn_w_down")
SMALL_SHARDED = ("sgu_b_in", "sgu_v_gain", "sgu_v_bias", "ffn_conv_w")
SMALL_REPL = ("fox_b_f", "fox_q_gain", "fox_k_gain", "sgu_w_s", "sgu_b_s", "ffn_conv_b", "ada_b",
              "norm1_g", "norm2_g", "final_g")
WEIGHTS = ("fox_w_in", "fox_b_f", "fox_q_gain", "fox_k_gain", "fox_w_out", "sgu_w_in", "sgu_b_in", "sgu_v_gain",
           "sgu_v_bias", "sgu_w_s", "sgu_b_s", "sgu_w_out", "ffn_w_up", "ffn_conv_w", "ffn_conv_b", "ffn_w_down",
           "ada_w", "ada_b", "norm1_g", "norm2_g", "final_g")


def _rows_of(a, mult=1):
    flat = a.reshape(-1)
    rows = -(-flat.shape[0] // ROW)
    rows = -(-rows // mult) * mult
    return jnp.pad(flat, (0, rows * ROW - flat.shape[0])).reshape(rows, ROW)


def _pack(parts, mult, total=None):
    p = jnp.concatenate([_rows_of(a, mult) for a in parts], axis=0)
    if total is not None:
        p = jnp.pad(p, ((0, total - p.shape[0]), (0, 0)))
    return p


def _unpack(pack, shapes, mult):
    out, r0 = [], 0
    for shp in shapes:
        size = int(np.prod(shp))
        rows = -(-(-(-size // ROW)) // mult) * mult
        out.append(pack[r0:r0 + rows].reshape(-1)[:size].reshape(shp))
        r0 += rows
    return out


def _big_shards(t):
    return [t["fox_w_in"][0], t["fox_w_out"][0], t["sgu_w_in"][0], t["sgu_w_out"][0],
            t["ffn_w_up"][0], t["ffn_w_up"][1], t["ffn_w_down"][0], t["ffn_w_down"][1]]


def _row_tile(rows):
    return next(t for t in (512, 352, 256, 128, 64) if rows % t == 0)


def kernel(x, c, fox_w_in, fox_b_f, fox_q_gain, fox_k_gain, fox_w_out, sgu_w_in, sgu_b_in, sgu_v_gain, sgu_v_bias, sgu_w_s, sgu_b_s, sgu_w_out, ffn_w_up, ffn_conv_w, ffn_conv_b, ffn_w_down, ada_w, ada_b, norm1_g, norm2_g, final_g, loss_target, m_fox_w_in, m_fox_b_f, m_fox_q_gain, m_fox_k_gain, m_fox_w_out, m_sgu_w_in, m_sgu_b_in, m_sgu_v_gain, m_sgu_v_bias, m_sgu_w_s, m_sgu_b_s, m_sgu_w_out, m_ffn_w_up, m_ffn_conv_w, m_ffn_conv_b, m_ffn_w_down, m_ada_w, m_ada_b, m_norm1_g, m_norm2_g, m_final_g, v_fox_w_in, v_fox_b_f, v_fox_q_gain, v_fox_k_gain, v_fox_w_out, v_sgu_w_in, v_sgu_b_in, v_sgu_v_gain, v_sgu_v_bias, v_sgu_w_s, v_sgu_b_s, v_sgu_w_out, v_ffn_w_up, v_ffn_conv_w, v_ffn_conv_b, v_ffn_w_down, v_ada_w, v_ada_b, v_norm1_g, v_norm2_g, v_final_g):
    w = dict(fox_w_in=fox_w_in, fox_b_f=fox_b_f, fox_q_gain=fox_q_gain, fox_k_gain=fox_k_gain, fox_w_out=fox_w_out,
             sgu_w_in=sgu_w_in, sgu_b_in=sgu_b_in, sgu_v_gain=sgu_v_gain, sgu_v_bias=sgu_v_bias, sgu_w_s=sgu_w_s,
             sgu_b_s=sgu_b_s, sgu_w_out=sgu_w_out, ffn_w_up=ffn_w_up, ffn_conv_w=ffn_conv_w, ffn_conv_b=ffn_conv_b,
             ffn_w_down=ffn_w_down, ada_w=ada_w, ada_b=ada_b, norm1_g=norm1_g, norm2_g=norm2_g, final_g=final_g)
    mom = dict(fox_w_in=m_fox_w_in, fox_b_f=m_fox_b_f, fox_q_gain=m_fox_q_gain, fox_k_gain=m_fox_k_gain,
               fox_w_out=m_fox_w_out, sgu_w_in=m_sgu_w_in, sgu_b_in=m_sgu_b_in, sgu_v_gain=m_sgu_v_gain,
               sgu_v_bias=m_sgu_v_bias, sgu_w_s=m_sgu_w_s, sgu_b_s=m_sgu_b_s, sgu_w_out=m_sgu_w_out,
               ffn_w_up=m_ffn_w_up, ffn_conv_w=m_ffn_conv_w, ffn_conv_b=m_ffn_conv_b, ffn_w_down=m_ffn_w_down,
               ada_w=m_ada_w, ada_b=m_ada_b, norm1_g=m_norm1_g, norm2_g=m_norm2_g, final_g=m_final_g)
    var = dict(fox_w_in=v_fox_w_in, fox_b_f=v_fox_b_f, fox_q_gain=v_fox_q_gain, fox_k_gain=v_fox_k_gain,
               fox_w_out=v_fox_w_out, sgu_w_in=v_sgu_w_in, sgu_b_in=v_sgu_b_in, sgu_v_gain=v_sgu_v_gain,
               sgu_v_bias=v_sgu_v_bias, sgu_w_s=v_sgu_w_s, sgu_b_s=v_sgu_b_s, sgu_w_out=v_sgu_w_out,
               ffn_w_up=v_ffn_w_up, ffn_conv_w=v_ffn_conv_w, ffn_conv_b=v_ffn_conv_b, ffn_w_down=v_ffn_w_down,
               ada_w=v_ada_w, ada_b=v_ada_b, norm1_g=v_norm1_g, norm2_g=v_norm2_g, final_g=v_final_g)

    ax, ay, ac = _mesh_pos()
    chip = 2 * ax + ay
    dev = 2 * chip + ac

    small_shard_shapes = tuple(w[n].shape for n in SMALL_SHARDED)
    blk = _pack([c] + [w[n] for n in SMALL_SHARDED], 1, 16)
    gat = _allgather8(blk, "gather_small").reshape(N_DEV, 16, ROW)
    c_all = gat[:, 0, :]
    per_chip = [_unpack(gat[2 * j, 1:], small_shard_shapes, 1) for j in range(N_CHIP)]
    full_small = {n: jnp.concatenate([per_chip[j][i] for j in range(N_CHIP)], axis=-1)
                  for i, n in enumerate(SMALL_SHARDED)}

    mine = [a.astype(BF16) for a in _big_shards(w)]
    gathered = _gather_shards(mine, "gather_weights")
    fwi, fwo, swi, swo, up0, up1, dn0, dn1 = [lax.dynamic_update_slice(g_, m_[None], (chip, 0, 0))
                                              for g_, m_ in zip(gathered, mine)]
    fwi_full = jnp.concatenate([fwi[j] for j in range(N_CHIP)] + [jnp.zeros((D, FOX_NP - FOX_N), BF16)], axis=1)
    wts = dict(fox_w_in=fwi_full, fox_w_out=fwo.reshape(D, D), sgu_w_in=swi, sgu_w_out=swo.reshape(SGW, D),
               ffn_w_up=[up0, up1], ffn_w_down=[dn0.reshape(DFF, D), dn1.reshape(DFF, D)])

    da = ada_w.shape[2]
    ada_b_cols = lax.dynamic_slice_in_dim(ada_b, chip * da, da, axis=1)[:, None, :]
    mod_cols, c_act = _ada_mod(c_all, ada_w, ada_b_cols)
    mod_all = _allgather8(mod_cols.reshape(-1, ROW), "gather_mod").reshape(N_DEV, 2, N_DEV, da)
    mod_mine = lax.dynamic_index_in_dim(mod_all[0::2], dev, axis=2, keepdims=False)
    mod = jnp.swapaxes(mod_mine, 0, 1).reshape(2, N_CHIP * da)

    small = dict(norm1_g=norm1_g, norm2_g=norm2_g, final_g=final_g[None], fox_q_gain=fox_q_gain,
                 fox_k_gain=fox_k_gain, fox_b_f=fox_b_f, sgu_b_in=full_small["sgu_b_in"],
                 sgu_v_gain=full_small["sgu_v_gain"], sgu_v_bias=full_small["sgu_v_bias"], sgu_w_s=sgu_w_s[0],
                 sgu_b_s=sgu_b_s[0], ffn_conv_w=full_small["ffn_conv_w"], ffn_conv_b=ffn_conv_b)
    loss_dev, dx, g, dmod = _local_step(x[0], loss_target[0], mod, wts, small)

    g["ada_b"] = dmod
    small_names = ("ada_b",) + SMALL_SHARDED + tuple(n for n in SMALL_REPL if n != "ada_b")
    gs = _pack([g[n] for n in small_names], 1)
    rows_s = -(-gs.shape[0] // 8) * 8
    gs = jnp.pad(gs, ((0, rows_s - gs.shape[0]), (0, 0)))
    gs_all = _allgather8(gs, "gather_small_grads").reshape(N_DEV, rows_s, ROW)
    gsum = _sum8(gs_all, "sum_small_grads")
    full_shapes = {n: w[n].shape for n in SMALL_REPL}
    full_shapes.update({n: w[n].shape[:-1] + (w[n].shape[-1] * N_CHIP,) for n in SMALL_SHARDED})
    gfull = dict(zip(small_names, _unpack(gsum, [full_shapes[n] for n in small_names], 1)))
    grads = {n: gfull[n] for n in SMALL_REPL}
    for n in SMALL_SHARDED:
        width = w[n].shape[-1]
        grads[n] = lax.dynamic_slice_in_dim(gfull[n], chip * width, width, axis=gfull[n].ndim - 1)
    dmod_all = gs_all[:, :12, :].reshape(N_DEV, 2, N_CHIP * da)
    dmod_cols = jnp.swapaxes(lax.dynamic_slice_in_dim(dmod_all, chip * da, da, axis=2), 0, 1)
    grads["ada_w"] = _ada_w_grad(c_act.T, dmod_cols)

    gfi = jnp.stack([g["fox_w_in"][:, 1028 * j:1028 * (j + 1)] for j in range(N_CHIP)])
    glist = [gfi, g["fox_w_out"].reshape(N_CHIP, 256, D), g["sgu_w_in"], g["sgu_w_out"].reshape(N_CHIP, 512, D),
             g["ffn_w_up"][0], g["ffn_w_up"][1], g["ffn_w_down"][0].reshape(N_CHIP, 704, D),
             g["ffn_w_down"][1].reshape(N_CHIP, 704, D)]
    c_arr = jnp.reshape(ac, (1,)).astype(jnp.int32)
    me_arr = jnp.reshape(chip, (1,)).astype(jnp.int32)
    sibs = _rs_to_sibling(glist, "rs_sibling")
    css = [_rs_chip_sum(g_, s_, c_arr, _row_tile(s_.shape[1]), "rs_chip_sum%d" % a)
           for a, (g_, s_) in enumerate(zip(glist, sibs))]
    rcvs = _rs_across_chips(css, "rs_chips")
    halves = [_rs_final_sum(cs_, r_, me_arr, _row_tile(cs_.shape[1]), "rs_final_sum%d" % a)
              for a, (cs_, r_) in enumerate(zip(css, rcvs))]
    others = _rs_swap_halves(halves, "rs_swap")
    red = [jnp.concatenate([jnp.where(ac == 0, h_, o_), jnp.where(ac == 0, o_, h_)]) for h_, o_ in zip(halves, others)]
    grads.update(fox_w_in=red[0], fox_w_out=red[1], sgu_w_in=red[2], sgu_w_out=red[3],
                 ffn_w_up=jnp.stack([red[4], red[5]]), ffn_w_down=jnp.stack([red[6], red[7]]))

    delta, new_m, new_v = {}, {}, {}
    for n in BIG + ("ada_w",):
        shp = w[n].shape
        two_d = lambda a: a.reshape(-1, shp[-1])
        d_, m_, v_ = _adamw(two_d(w[n]), two_d(grads[n]), two_d(mom[n]), two_d(var[n]), "adamw_" + n)
        delta[n], new_m[n], new_v[n] = d_.reshape(shp), m_.reshape(shp), v_.reshape(shp)
    rest = SMALL_SHARDED + SMALL_REPL
    packs = [_pack([t[n] for n in rest], 1) for t in (w, grads, mom, var)]
    rows_r = -(-packs[0].shape[0] // 8) * 8
    packs = [jnp.pad(p, ((0, rows_r - p.shape[0]), (0, 0))) for p in packs]
    outs = _adamw(*packs, "adamw_small")
    for t, o in zip((delta, new_m, new_v), outs):
        t.update(zip(rest, _unpack(o, [w[n].shape for n in rest], 1)))

    loss = lax.psum(loss_dev, ("x", "y", "c"))
    return (loss, dx[None], *[grads[n].reshape(w[n].shape) for n in WEIGHTS], *[delta[n] for n in WEIGHTS],
            *[new_m[n] for n in WEIGHTS], *[new_v[n] for n in WEIGHTS])
```

```python
import functools
import math

import numpy as np
import jax
import jax.numpy as jnp
from jax import lax
from jax.experimental import pallas as pl
from jax.experimental.pallas import tpu as pltpu

F32 = jnp.float32
BF16 = jnp.bfloat16
MESH = pl.DeviceIdType.MESH

D = 1024
H = 16
DH = 64
NP = H // 2
LANES = 128
DFF = 2816
SGW = 2048
SGG = 8
SGC = 256
SGB = 128
CHUNK = 64
EPS = 1e-6
FOX_N = 4 * D + H
FOX_NP = 4224
GT = 256
NGT = DFF // GT
SCALE = DH ** -0.5
LOG2E = 1.4426950408889634

ADAM_LR = 0.001
ADAM_B1 = 0.9
ADAM_B2 = 0.999
ADAM_EPS = 1e-08
ADAM_WD = 0.01
ADAM_STEP = 10

V7X_VMEM_LIMIT = 56 * 1024 * 1024

L_F = 64
L_NF = 67
L_LSE = 70


def _cparams(sem=None):
    return pltpu.CompilerParams(dimension_semantics=sem, vmem_limit_bytes=V7X_VMEM_LIMIT)


def _split3(x):
    hi = x.astype(BF16)
    r = x - hi.astype(F32)
    mid = r.astype(BF16)
    lo = (r - mid.astype(F32)).astype(BF16)
    return hi, mid, lo


def _dot(a, b, dims=(((1,), (0,)), ((), ()))):
    return lax.dot_general(a, b, dims, preferred_element_type=F32)


def _dot_nt(a, b):
    return _dot(a, b, (((1,), (1,)), ((), ())))


def _dot_tn(a, b):
    return _dot(a, b, (((0,), (0,)), ((), ())))


def _exact_dot(m_bf16, x_f32):
    hi, mid, lo = _split3(x_f32)
    return _dot(m_bf16, hi) + _dot(m_bf16, mid) + _dot(m_bf16, lo)


def _exact_dot_r(x_f32, m_bf16):
    hi, mid, lo = _split3(x_f32)
    return _dot(hi, m_bf16) + _dot(mid, m_bf16) + _dot(lo, m_bf16)


def _head_block_ones():
    r = lax.broadcasted_iota(jnp.int32, (LANES, LANES), 0) // DH
    c = lax.broadcasted_iota(jnp.int32, (LANES, LANES), 1) // DH
    return (r == c).astype(BF16)


def _sigmoid(x):
    return 1.0 / (1.0 + jnp.exp(-x))


def _gelu(x):
    c = math.sqrt(2.0 / math.pi)
    return 0.5 * x * (1.0 + jnp.tanh(c * (x + 0.044715 * (x * x * x))))


def _gelu_grad(x):
    c = math.sqrt(2.0 / math.pi)
    t = jnp.tanh(c * (x + 0.044715 * (x * x * x)))
    return 0.5 * (1.0 + t) + 0.5 * x * (1.0 - t * t) * c * (1.0 + 3 * 0.044715 * (x * x))


def _rstd_rows(x):
    return lax.rsqrt(jnp.mean(x * x, axis=-1, keepdims=True) + EPS)


def _norm_mod_matmul(x, ng, sc, sh, w, bias, out_dtype, ts, tn, name, planes=1):
    s, d = x.shape
    ns = w.shape[-1]
    n = w.shape[0] * ns if w.ndim == 3 else ns
    nc = n // planes

    def body(x_ref, ng_ref, sc_ref, sh_ref, w_ref, b_ref, o_ref, h_ref):
        xv = x_ref[...]
        h = (xv * _rstd_rows(xv) * ng_ref[...] * (1.0 + sc_ref[...]) + sh_ref[...]).astype(BF16)
        h_ref[...] = h
        for e in range(planes):
            for c0 in range(0, nc, tn):
                g0 = e * nc + c0
                wv = w_ref[g0 // ns, :, g0 % ns:g0 % ns + tn] if w.ndim == 3 else w_ref[:, g0:g0 + tn]
                val = (_dot(h, wv) + b_ref[:, g0:g0 + tn]).astype(out_dtype)
                if planes == 1:
                    o_ref[:, c0:c0 + tn] = val
                else:
                    o_ref[e, :, c0:c0 + tn] = val

    vec = pl.BlockSpec((1, d), lambda i: (0, 0))
    w_spec = (pl.BlockSpec(w.shape, lambda i: (0, 0, 0)) if w.ndim == 3 else pl.BlockSpec((d, n), lambda i: (0, 0)))
    if planes == 1:
        o_spec, o_shape = pl.BlockSpec((ts, n), lambda i: (i, 0)), (s, n)
    else:
        o_spec, o_shape = pl.BlockSpec((planes, ts, nc), lambda i: (0, i, 0)), (planes, s, nc)
    return pl.pallas_call(
        body, name=name, grid=(s // ts,),
        in_specs=[pl.BlockSpec((ts, d), lambda i: (i, 0)), vec, vec, vec, w_spec,
                  pl.BlockSpec((1, n), lambda i: (0, 0))],
        out_specs=[o_spec, pl.BlockSpec((ts, d), lambda i: (i, 0))],
        out_shape=[jax.ShapeDtypeStruct(o_shape, out_dtype), jax.ShapeDtypeStruct((s, d), BF16)],
        compiler_params=_cparams(("arbitrary",)),
    )(x, ng, sc, sh, w, bias)


def _matmul(a, b, ta, tb, tm, tn, tk, out_dtype, name, out_parts=1):
    if a.ndim == 3:
        m, k = a.shape[1], a.shape[0] * a.shape[2]
        nkp = a.shape[2] // tk
    else:
        m, k = (a.shape[1], a.shape[0]) if ta else a.shape
    if b.ndim == 3:
        n = b.shape[1] if tb else b.shape[0] * b.shape[2]
        nbp = b.shape[2] // (tk if tb else tn)
    else:
        n = b.shape[0] if tb else b.shape[1]
    nk = k // tk
    nop = n // out_parts // tn
    dims = (((0,) if ta else (1,), (1,) if tb else (0,)), ((), ()))

    def body(a_ref, b_ref, o_ref, acc):
        kk = pl.program_id(2)

        @pl.when(kk == 0)
        def _():
            acc[...] = jnp.zeros_like(acc)
        acc[...] += _dot(a_ref[...], b_ref[...], dims)

        @pl.when(kk == nk - 1)
        def _():
            o_ref[...] = acc[...].astype(out_dtype)

    if a.ndim == 3:
        a_spec = pl.BlockSpec((None, tm, tk), lambda i, j, kk: (kk // nkp, i, kk % nkp))
    else:
        a_spec = (pl.BlockSpec((tk, tm), lambda i, j, kk: (kk, i)) if ta
                  else pl.BlockSpec((tm, tk), lambda i, j, kk: (i, kk)))
    if b.ndim == 3 and tb:
        b_spec = pl.BlockSpec((None, tn, tk), lambda i, j, kk: (kk // nbp, j, kk % nbp))
    elif b.ndim == 3:
        b_spec = pl.BlockSpec((None, tk, tn), lambda i, j, kk: (j // nbp, kk, j % nbp))
    else:
        b_spec = (pl.BlockSpec((tn, tk), lambda i, j, kk: (j, kk)) if tb
                  else pl.BlockSpec((tk, tn), lambda i, j, kk: (kk, j)))
    if out_parts > 1:
        o_spec = pl.BlockSpec((None, tm, tn), lambda i, j, kk: (j // nop, i, j % nop))
        o_shape = (out_parts, m, n // out_parts)
    else:
        o_spec, o_shape = pl.BlockSpec((tm, tn), lambda i, j, kk: (i, j)), (m, n)
    return pl.pallas_call(
        body, name=name, grid=(m // tm, n // tn, nk),
        in_specs=[a_spec, b_spec],
        out_specs=o_spec,
        out_shape=jax.ShapeDtypeStruct(o_shape, out_dtype),
        scratch_shapes=[pltpu.VMEM((tm, tn), F32)],
        compiler_params=_cparams(("arbitrary", "arbitrary", "arbitrary")),
    )(a, b)


def _matmul_residual(a, w, xin, g, ts, name):
    s, k = a.shape
    d = w.shape[1]

    def body(a_ref, w_ref, x_ref, g_ref, o_ref, y_ref):
        y = _dot(a_ref[...], w_ref[...])
        o_ref[...] = x_ref[...] + g_ref[...] * y
        y_ref[...] = y.astype(BF16)

    return pl.pallas_call(
        body, name=name, grid=(s // ts,),
        in_specs=[pl.BlockSpec((ts, k), lambda i: (i, 0)),
                  pl.BlockSpec((k, d), lambda i: (0, 0)),
                  pl.BlockSpec((ts, d), lambda i: (i, 0)),
                  pl.BlockSpec((1, d), lambda i: (0, 0))],
        out_specs=[pl.BlockSpec((ts, d), lambda i: (i, 0)), pl.BlockSpec((ts, d), lambda i: (i, 0))],
        out_shape=[jax.ShapeDtypeStruct((s, d), F32), jax.ShapeDtypeStruct((s, d), BF16)],
        compiler_params=_cparams(("arbitrary",)),
    )(a, w, xin, g)


def _lane(shape):
    return lax.broadcasted_iota(jnp.int32, shape, 1)


def _pair_norm(x, gain2, bones):
    msq = _exact_dot_r(x * x, bones) * (1.0 / DH)
    r = lax.rsqrt(msq + EPS)
    xh = x * r
    return xh * gain2, xh, r


def _fox_post(proj, qg2, kg2, bf, ts, name):
    s = proj.shape[0]

    def body(p_ref, qg_ref, kg_ref, bf_ref, q_ref, k_ref, v_ref, carry):
        @pl.when(pl.program_id(0) == 0)
        def _():
            carry[...] = jnp.zeros_like(carry)
        lane = _lane((ts, LANES))
        bones = _head_block_ones()
        xf = p_ref[:, 4 * D:4 * D + LANES] + bf_ref[...]
        logf = jnp.minimum(xf, 0.0) - jnp.log(1.0 + jnp.exp(-jnp.abs(xf)))
        logf = jnp.where(lane < H, logf, 0.0)
        rr = lax.broadcasted_iota(jnp.int32, (ts, ts), 0)
        cc = lax.broadcasted_iota(jnp.int32, (ts, ts), 1)
        ltri = (cc <= rr).astype(BF16)
        fcum = _exact_dot(ltri, logf) + carry[0:1, :]
        carry[0:1, :] = fcum[ts - 1:ts, :]
        fhi, fmid, flo = _split3(fcum * LOG2E)
        fhi, fmid, flo = fhi.astype(F32), fmid.astype(F32), flo.astype(F32)
        one_q = ((lane >= L_NF) & (lane < L_NF + 3)).astype(F32)
        one_k = (((lane >= L_F) & (lane < L_F + 3)) | ((lane >= L_LSE) & (lane < L_LSE + 3))).astype(F32)
        one_v = ((lane >= L_F) & (lane < L_F + 3)).astype(F32)
        for p in range(NP):
            qn, _, _ = _pair_norm(p_ref[:, p * LANES:(p + 1) * LANES], qg_ref[...], bones)
            kn, _, _ = _pair_norm(p_ref[:, D + p * LANES:D + (p + 1) * LANES], kg_ref[...], bones)
            vv = p_ref[:, 2 * D + p * LANES:2 * D + (p + 1) * LANES]
            qn = qn * (SCALE * LOG2E)
            for e in range(2):
                h = 2 * p + e
                if e == 1:
                    qe, ke, ve = (pltpu.roll(t, DH, axis=1) for t in (qn, kn, vv))
                else:
                    qe, ke, ve = qn, kn, vv
                f0, f1, f2 = fhi[:, h:h + 1], fmid[:, h:h + 1], flo[:, h:h + 1]
                fq = jnp.where(lane == L_F, f0, jnp.where(lane == L_F + 1, f1, jnp.where(lane == L_F + 2, f2, one_q)))
                fk = jnp.where(lane == L_NF, -f0, jnp.where(lane == L_NF + 1, -f1, jnp.where(lane == L_NF + 2, -f2, one_k)))
                q_ref[h] = jnp.where(lane < DH, qe, fq).astype(BF16)
                k_ref[h] = jnp.where(lane < DH, ke, fk).astype(BF16)
                v_ref[h] = jnp.where(lane < DH, ve, one_v).astype(BF16)

    hs = pl.BlockSpec((H, ts, LANES), lambda i: (0, i, 0))
    vec = pl.BlockSpec((1, LANES), lambda i: (0, 0))
    shp = jax.ShapeDtypeStruct((H, s, LANES), BF16)
    return pl.pallas_call(
        body, name=name, grid=(s // ts,),
        in_specs=[pl.BlockSpec((ts, FOX_NP), lambda i: (i, 0)), vec, vec, vec],
        out_specs=[hs, hs, hs], out_shape=[shp, shp, shp],
        scratch_shapes=[pltpu.VMEM((8, LANES), F32)],
        compiler_params=_cparams(("arbitrary",)),
    )(proj, qg2, kg2, bf)


def _gather_copies(p_refs, o_refs, send_sems, recv_sems):
    x, y, c = _mesh_pos()
    me = 2 * x + y
    sends, arrivals = [], []
    for a, (p_ref, o_ref) in enumerate(zip(p_refs, o_refs)):
        rh = p_ref.shape[0] // 2
        for k, chip in enumerate(_other_chips(x, y)):
            ci = 2 * chip[0] + chip[1]
            for cc in range(2):
                sends.append(_remote(p_ref.at[pl.ds(c * rh, rh), :], o_ref.at[me, pl.ds(c * rh, rh), :],
                                     send_sems.at[6 * a + 2 * k + cc], recv_sems.at[6 * a + 2 * k + c], (*chip, cc)))
                arrivals.append(_remote(o_ref.at[ci, pl.ds(cc * rh, rh), :], o_ref.at[ci, pl.ds(cc * rh, rh), :],
                                        send_sems.at[6 * a + 2 * k + cc], recv_sems.at[6 * a + 2 * k + cc],
                                        (*chip, cc)))
    return sends, arrivals


def _attn_fwd(qa, ka, va, tq, name, shards=()):
    s = qa.shape[1]
    nq = s // tq
    na = len(shards)

    def body(*refs):
        q_ref, k_ref, v_ref = refs[:3]
        p_refs = refs[3:3 + na]
        o_ref, ql_ref = refs[3 + na:5 + na]
        g_refs = refs[5 + na:5 + 2 * na]
        i = pl.program_id(1)
        if na:
            send_sems, recv_sems = refs[5 + 2 * na:]

            @pl.when((pl.program_id(0) == 0) & (i == 0))
            def _():
                for cp in _gather_copies(p_refs, g_refs, send_sems, recv_sems)[0]:
                    cp.start()
        lane = _lane((tq, LANES))
        qs_ = [q_ref[0], q_ref[1]]

        def step(j, carry, masked):
            off = pl.multiple_of(j * tq, tq)
            new = []
            for e in range(2):
                m, acc = carry[e]
                kb = k_ref[e, pl.ds(off, tq), :]
                vb = v_ref[e, pl.ds(off, tq), :]
                sc = _dot_nt(qs_[e], kb)
                if masked:
                    rr = lax.broadcasted_iota(jnp.int32, (tq, tq), 0)
                    cc = lax.broadcasted_iota(jnp.int32, (tq, tq), 1)
                    sc = jnp.where(cc <= rr, sc, -jnp.inf)
                m_new = jnp.maximum(m, jnp.max(sc, axis=-1, keepdims=True))
                pr = jnp.exp2(sc - m_new)
                acc = acc * jnp.exp2(m - m_new) + _dot(pr.astype(BF16), vb)
                new.append((m_new, acc))
            return tuple(new)

        one = (jnp.full((tq, 1), -jnp.inf, F32), jnp.zeros((tq, LANES), F32))
        carry = lax.fori_loop(0, i, functools.partial(step, masked=False), (one, one))
        carry = step(i, carry, True)
        outs = []
        for e in range(2):
            m, acc = carry[e]
            l = acc[:, L_F:L_F + 1]
            outs.append(acc / l)
            lse = m + jnp.log2(l)
            h0, h1, h2 = _split3(-lse)
            ql = jnp.where(lane == L_LSE, h0.astype(F32),
                           jnp.where(lane == L_LSE + 1, h1.astype(F32),
                                     jnp.where(lane == L_LSE + 2, h2.astype(F32), qs_[e].astype(F32))))
            ql_ref[e] = ql.astype(BF16)
        o_ref[...] = jnp.where(lane < DH, outs[0], pltpu.roll(outs[1], DH, axis=1))
        if na:
            @pl.when((pl.program_id(0) == NP - 1) & (i == nq - 1))
            def _():
                sends, arrivals = _gather_copies(p_refs, g_refs, send_sems, recv_sems)
                for cp in arrivals:
                    cp.wait_recv()
                for cp in sends:
                    cp.wait_send()

    res = pl.BlockSpec((2, s, LANES), lambda p, i: (p, 0, 0))
    qs = pl.BlockSpec((2, tq, LANES), lambda p, i: (p, i, 0))
    outs = pl.pallas_call(
        body, name=name, grid=(NP, nq),
        in_specs=[qs, res, res] + [HBM_SPEC] * na,
        out_specs=[pl.BlockSpec((tq, LANES), lambda p, i: (i, p)), qs] + [HBM_SPEC] * na,
        out_shape=[jax.ShapeDtypeStruct((s, D), F32), jax.ShapeDtypeStruct((H, s, LANES), BF16)]
        + [jax.ShapeDtypeStruct((N_CHIP,) + p.shape, p.dtype) for p in shards],
        scratch_shapes=[pltpu.SemaphoreType.DMA((6 * na,))] * 2 if na else [],
        compiler_params=_cparams(("arbitrary", "arbitrary")),
    )(qa, ka, va, *shards)
    return outs[0], outs[1], list(outs[2:])


def _chip_exchange_copies(cs_refs, o_refs, send_sems, recv_sems):
    x, y, c = _mesh_pos()
    cps = []
    for a, (cs_ref, o_ref) in enumerate(zip(cs_refs, o_refs)):
        for k, chip in enumerate(_other_chips(x, y)):
            ci = 2 * chip[0] + chip[1]
            cps.append(_remote(cs_ref.at[ci], o_ref.at[k], send_sems.at[3 * a + k], recv_sems.at[3 * a + k],
                               (*chip, c)))
    return cps


def _attn_bwd(ql, ka, va, doa, tq, name, css=()):
    s = ql.shape[1]
    nq = s // tq
    na = len(css)

    def body(*refs):
        q_ref, k_ref, v_ref, do_ref = refs[:4]
        cs_refs = refs[4:4 + na]
        dqo_ref, dk_ref, dv_ref = refs[4 + na:7 + na]
        r_refs = refs[7 + na:7 + 2 * na]
        dq_ref = refs[7 + 2 * na]
        if na:
            send_sems, recv_sems = refs[8 + 2 * na:]

            @pl.when(pl.program_id(0) == 0)
            def _():
                for cp in _chip_exchange_copies(cs_refs, r_refs, send_sems, recv_sems):
                    cp.start()
        dq_ref[...] = jnp.zeros_like(dq_ref)
        lane = _lane((tq, LANES))

        def kv_block(j, _):
            joff = pl.multiple_of(j * tq, tq)
            kb = k_ref[0, pl.ds(joff, tq), :]
            vb = v_ref[0, pl.ds(joff, tq), :]

            def step(i, carry, masked):
                dk, dv = carry
                ioff = pl.multiple_of(i * tq, tq)
                qb = q_ref[0, pl.ds(ioff, tq), :]
                dob = do_ref[0, pl.ds(ioff, tq), :]
                pr = jnp.exp2(_dot_nt(qb, kb))
                if masked:
                    rr = lax.broadcasted_iota(jnp.int32, (tq, tq), 0)
                    cc = lax.broadcasted_iota(jnp.int32, (tq, tq), 1)
                    pr = jnp.where(cc <= rr, pr, 0.0)
                ds = (pr * _dot_nt(dob, vb)).astype(BF16)
                dv = dv + _dot_tn(pr.astype(BF16), dob)
                dk = dk + _dot_tn(ds, qb)
                dq_ref[pl.ds(ioff, tq), :] += _dot(ds, kb)
                return dk, dv

            zero = jnp.zeros((tq, LANES), F32)
            carry = step(j, (zero, zero), True)
            dk, dv = lax.fori_loop(j + 1, nq, functools.partial(step, masked=False), carry)
            col = dk[:, L_NF:L_NF + 1]
            hi = col.astype(BF16).astype(F32)
            dk = jnp.where(lane == L_NF, hi, jnp.where(lane == L_NF + 1, col - hi, dk))
            dk_ref[0, pl.ds(joff, tq), :] = dk.astype(BF16)
            dv_ref[0, pl.ds(joff, tq), :] = dv.astype(BF16)
            return 0

        lax.fori_loop(0, nq, kv_block, 0)
        dq = dq_ref[...]
        lane_s = _lane((s, LANES))
        col = dq[:, L_F:L_F + 1]
        hi = col.astype(BF16).astype(F32)
        dqo_ref[0] = jnp.where(lane_s == L_F, hi, jnp.where(lane_s == L_F + 1, col - hi, dq)).astype(BF16)
        if na:
            @pl.when(pl.program_id(0) == H - 1)
            def _():
                for cp in _chip_exchange_copies(cs_refs, r_refs, send_sems, recv_sems):
                    cp.wait()

    hs = pl.BlockSpec((1, s, LANES), lambda h: (h, 0, 0))
    shp = jax.ShapeDtypeStruct((H, s, LANES), BF16)
    outs = pl.pallas_call(
        body, name=name, grid=(H,),
        in_specs=[hs, hs, hs, hs] + [HBM_SPEC] * na, out_specs=[hs, hs, hs] + [HBM_SPEC] * na,
        out_shape=[shp, shp, shp] + [jax.ShapeDtypeStruct((3,) + cs.shape[1:], cs.dtype) for cs in css],
        scratch_shapes=[pltpu.VMEM((s, LANES), F32)] + ([pltpu.SemaphoreType.DMA((3 * na,))] * 2 if na else []),
        compiler_params=_cparams(("arbitrary",)),
    )(ql, ka, va, doa, *css)
    return outs[0], outs[1], outs[2], list(outs[3:])


def _gate(att, proj, ts, name):
    s = att.shape[0]

    def body(a_ref, o_ref, g_ref):
        g_ref[...] = (a_ref[...] * _sigmoid(o_ref[...])).astype(BF16)

    return pl.pallas_call(
        body, name=name, grid=(s // ts,),
        in_specs=[pl.BlockSpec((ts, D), lambda i: (i, 0)), pl.BlockSpec((ts, D), lambda i: (i, 3))],
        out_specs=pl.BlockSpec((ts, D), lambda i: (i, 0)),
        out_shape=jax.ShapeDtypeStruct((s, D), BF16),
        compiler_params=_cparams(("arbitrary",)),
    )(att, proj)


def _attn_bwd_prep(dgated, att, proj, ts, name):
    s = att.shape[0]

    def body(dg_ref, a_ref, o_ref, doa_ref, dop_ref):
        lane = _lane((ts, LANES))
        bones = _head_block_ones()
        for p in range(NP):
            sl = slice(p * LANES, (p + 1) * LANES)
            dg, a = dg_ref[:, sl], a_ref[:, sl]
            sig = _sigmoid(o_ref[:, sl])
            datt = dg * sig
            dop_ref[:, sl] = (dg * a * sig * (1.0 - sig)).astype(BF16)
            delta = _exact_dot_r(datt * a, bones)
            for e in range(2):
                de, dl = (datt, delta) if e == 0 else (pltpu.roll(datt, DH, axis=1), pltpu.roll(delta, DH, axis=1))
                h0, h1, h2 = _split3(-dl[:, 0:1])
                aug = jnp.where(lane == L_F, h0.astype(F32),
                                jnp.where(lane == L_F + 1, h1.astype(F32),
                                          jnp.where(lane == L_F + 2, h2.astype(F32), 0.0)))
                doa_ref[2 * p + e] = jnp.where(lane < DH, de, aug).astype(BF16)

    row = pl.BlockSpec((ts, D), lambda i: (i, 0))
    return pl.pallas_call(
        body, name=name, grid=(s // ts,),
        in_specs=[row, row, pl.BlockSpec((ts, D), lambda i: (i, 3))],
        out_specs=[pl.BlockSpec((H, ts, LANES), lambda i: (0, i, 0)), row],
        out_shape=[jax.ShapeDtypeStruct((H, s, LANES), BF16), jax.ShapeDtypeStruct((s, D), BF16)],
        compiler_params=_cparams(("arbitrary",)),
    )(dgated, att, proj)


def _fox_post_bwd(proj, dqa, dka, dva, dop, qg2, kg2, bf, ts, name):
    s = proj.shape[0]
    nt = s // ts

    def body(p_ref, dq_ref, dk_ref, dv_ref, dop_ref, qg_ref, kg_ref, bf_ref, o_ref, red_ref, carry):
        @pl.when(pl.program_id(0) == 0)
        def _():
            carry[...] = jnp.zeros_like(carry)
            red_ref[...] = jnp.zeros_like(red_ref)
        lane = _lane((ts, LANES))
        bones = _head_block_ones()
        d_f = jnp.zeros((ts, LANES), F32)
        dqg = jnp.zeros((1, LANES), F32)
        dkg = jnp.zeros((1, LANES), F32)
        for p in range(NP):
            heads = [[ref[2 * p + e].astype(F32) for e in range(2)] for ref in (dq_ref, dk_ref, dv_ref)]
            pair = [jnp.where(lane < DH, a, pltpu.roll(b, DH, axis=1)) for a, b in heads]
            for e in range(2):
                dqe, dke = heads[0][e], heads[1][e]
                col = (dqe[:, L_F:L_F + 1] + dqe[:, L_F + 1:L_F + 2]
                       - dke[:, L_NF:L_NF + 1] - dke[:, L_NF + 1:L_NF + 2])
                d_f = jnp.where(lane == 2 * p + e, col, d_f)
            for idx, (g_ref, base) in enumerate(((qg_ref, 0), (kg_ref, D))):
                x = p_ref[:, base + p * LANES:base + (p + 1) * LANES]
                _, xh, r = _pair_norm(x, g_ref[...], bones)
                dn = pair[idx] * (SCALE if idx == 0 else 1.0 / LOG2E)
                t = dn * g_ref[...]
                mean_txh = _exact_dot_r(t * xh, bones) * (1.0 / DH)
                dx = r * (t - xh * mean_txh)
                o_ref[:, base + p * LANES:base + (p + 1) * LANES] = dx.astype(BF16)
                gsum = jnp.sum(dn * xh, axis=0, keepdims=True)
                if idx == 0:
                    dqg = dqg + gsum
                else:
                    dkg = dkg + gsum
            o_ref[:, 2 * D + p * LANES:2 * D + (p + 1) * LANES] = pair[2].astype(BF16)
        o_ref[:, 3 * D:4 * D] = dop_ref[...]
        rr = lax.broadcasted_iota(jnp.int32, (ts, ts), 0)
        cc = lax.broadcasted_iota(jnp.int32, (ts, ts), 1)
        utri = (cc >= rr).astype(BF16)
        dlogf = _exact_dot(utri, d_f) + carry[0:1, :]
        carry[0:1, :] = dlogf[0:1, :]
        xf = p_ref[:, 4 * D:4 * D + LANES] + bf_ref[...]
        dfl = jnp.where(lane < H, dlogf * _sigmoid(-xf), 0.0)
        o_ref[:, 4 * D:4 * D + LANES] = dfl.astype(BF16)
        red_ref[0:1, :] += dqg
        red_ref[1:2, :] += dkg
        red_ref[2:3, :] += jnp.sum(dfl, axis=0, keepdims=True)

    hs = pl.BlockSpec((H, ts, LANES), lambda i: (0, nt - 1 - i, 0))
    vec = pl.BlockSpec((1, LANES), lambda i: (0, 0))
    return pl.pallas_call(
        body, name=name, grid=(nt,),
        in_specs=[pl.BlockSpec((ts, FOX_NP), lambda i: (nt - 1 - i, 0)), hs, hs, hs,
                  pl.BlockSpec((ts, D), lambda i: (nt - 1 - i, 0)), vec, vec, vec],
        out_specs=[pl.BlockSpec((ts, FOX_NP), lambda i: (nt - 1 - i, 0)),
                   pl.BlockSpec((8, LANES), lambda i: (0, 0))],
        out_shape=[jax.ShapeDtypeStruct((s, FOX_NP), BF16), jax.ShapeDtypeStruct((8, LANES), F32)],
        scratch_shapes=[pltpu.VMEM((8, LANES), F32)],
        compiler_params=_cparams(("arbitrary",)),
    )(proj, dqa, dka, dva, dop, qg2, kg2, bf)


HALO = 16
TS = 512
TQ = 512
TR = 256
TP = 256


def _shift_down(x, k):
    return pltpu.roll(x, k, axis=0)


def _shift_up(x, k):
    return pltpu.roll(x, x.shape[0] - k, axis=0)


def _planes(ref):
    return jnp.concatenate([ref[0].astype(F32), ref[1].astype(F32)], axis=1)


def _conv_gate(a, cw, cb, ts, name):
    s = a.shape[1]
    hb = ts // HALO

    def body(prev_ref, a_ref, cw_ref, cb_ref, f_ref):
        i = pl.program_id(0)
        cwv, cbv = _planes(cw_ref), _planes(cb_ref)
        prev = jnp.where(i > 0, _planes(prev_ref), 0.0)
        ext = jnp.concatenate([prev, _planes(a_ref)], axis=0)
        ap = (_shift_down(ext, 2) * cwv[0:1, :] + _shift_down(ext, 1) * cwv[1:2, :]
              + ext * cwv[2:3, :] + cbv)[HALO:, :]
        g, val = ap[:, :GT], ap[:, GT:]
        f_ref[...] = (g * _sigmoid(g) * val).astype(BF16)

    return pl.pallas_call(
        body, name=name, grid=(s // ts, NGT),
        in_specs=[pl.BlockSpec((2, HALO, GT), lambda i, j: (0, jnp.maximum(i * hb - 1, 0), j)),
                  pl.BlockSpec((2, ts, GT), lambda i, j: (0, i, j)),
                  pl.BlockSpec((2, 8, GT), lambda i, j: (0, 0, j)),
                  pl.BlockSpec((2, 1, GT), lambda i, j: (0, 0, j))],
        out_specs=pl.BlockSpec((ts, GT), lambda i, j: (i, j)),
        out_shape=jax.ShapeDtypeStruct((s, DFF), BF16),
        compiler_params=_cparams(("arbitrary", "arbitrary")),
    )(a, a, cw, cb)


def _conv_gate_bwd(a, df, cw, cb, ts, name):
    s = a.shape[1]
    hb = ts // HALO
    nt = s // ts

    def body(prev_ref, a_ref, next_ref, df_ref, dfn_ref, cw_ref, cb_ref, da_ref, red_ref):
        i = pl.program_id(1)

        @pl.when(i == 0)
        def _():
            red_ref[...] = jnp.zeros_like(red_ref)
        cwv, cbv = _planes(cw_ref), _planes(cb_ref)
        prev = jnp.where(i > 0, _planes(prev_ref), 0.0)
        ext = jnp.concatenate([prev, _planes(a_ref), _planes(next_ref)], axis=0)
        dfn = jnp.where(i < nt - 1, dfn_ref[...].astype(F32), 0.0)
        dfe = jnp.concatenate([jnp.zeros((HALO, GT), F32), df_ref[...].astype(F32), dfn], axis=0)
        am2, am1 = _shift_down(ext, 2), _shift_down(ext, 1)
        ap = am2 * cwv[0:1, :] + am1 * cwv[1:2, :] + ext * cwv[2:3, :] + cbv
        g, val = ap[:, :GT], ap[:, GT:]
        sg = _sigmoid(g)
        dap = jnp.concatenate([dfe * val * (sg * (1.0 + g * (1.0 - sg))), dfe * (g * sg)], axis=1)
        da = dap * cwv[2:3, :] + _shift_up(dap, 1) * cwv[1:2, :] + _shift_up(dap, 2) * cwv[0:1, :]
        main = slice(HALO, HALO + ts)
        sums = [jnp.sum((t * dap)[main], axis=0, keepdims=True) for t in (am2, am1, ext)]
        sums.append(jnp.sum(dap[main], axis=0, keepdims=True))
        for e in range(2):
            cols = slice(e * GT, (e + 1) * GT)
            da_ref[e] = da[main, cols].astype(BF16)
            for r, sm in enumerate(sums):
                red_ref[e, r:r + 1, :] += sm[:, cols]

    nhb = s // HALO
    return pl.pallas_call(
        body, name=name, grid=(NGT, nt),
        in_specs=[pl.BlockSpec((2, HALO, GT), lambda j, i: (0, jnp.maximum(i * hb - 1, 0), j)),
                  pl.BlockSpec((2, ts, GT), lambda j, i: (0, i, j)),
                  pl.BlockSpec((2, HALO, GT), lambda j, i: (0, jnp.minimum((i + 1) * hb, nhb - 1), j)),
                  pl.BlockSpec((ts, GT), lambda j, i: (i, j)),
                  pl.BlockSpec((HALO, GT), lambda j, i: (jnp.minimum((i + 1) * hb, nhb - 1), j)),
                  pl.BlockSpec((2, 8, GT), lambda j, i: (0, 0, j)),
                  pl.BlockSpec((2, 1, GT), lambda j, i: (0, 0, j))],
        out_specs=[pl.BlockSpec((2, ts, GT), lambda j, i: (0, i, j)),
                   pl.BlockSpec((2, 8, GT), lambda j, i: (0, 0, j))],
        out_shape=[jax.ShapeDtypeStruct((2, s, DFF), BF16), jax.ShapeDtypeStruct((2, 8, DFF), F32)],
        compiler_params=_cparams(("arbitrary", "arbitrary")),
    )(a, a, a, df, df, cw, cb)


def _chunk_mask(transposed=False):
    t = lax.broadcasted_iota(jnp.int32, (SGB, SGB), 0) // CHUNK
    u = lax.broadcasted_iota(jnp.int32, (SGB, SGB), 1) // CHUNK
    return (t <= u) if transposed else (u <= t)


def _sgu_ln(zv, gain, bias):
    v = _gelu(zv)
    mu = jnp.mean(v, axis=-1, keepdims=True)
    vc = v - mu
    rstd = lax.rsqrt(jnp.mean(vc * vc, axis=-1, keepdims=True) + EPS)
    vhat = vc * rstd
    return vhat * gain + bias, vhat, rstd


def _sgu_fwd(z, vgain, vbias, ws, bst, tr, name):
    s = z.shape[0]

    def body(zu_ref, zv_ref, vg_ref, vb_ref, ws_ref, bs_ref, y_ref):
        u = _gelu(zu_ref[...].astype(F32))
        vn, _, _ = _sgu_ln(zv_ref[...].astype(F32), vg_ref[...], vb_ref[...])
        vn = vn.astype(BF16)
        mask = _chunk_mask()
        for g in range(SGG):
            w = jnp.where(mask, ws_ref[g], 0.0).astype(BF16)
            for b in range(tr // SGB):
                rs, cs = slice(b * SGB, (b + 1) * SGB), slice(g * SGC, (g + 1) * SGC)
                mixed = _dot(w, vn[rs, cs]) + bs_ref[:, g:g + 1]
                y_ref[rs, cs] = (u[rs, cs] * mixed).astype(BF16)

    vec = pl.BlockSpec((1, SGW), lambda i: (0, 0))
    return pl.pallas_call(
        body, name=name, grid=(s // tr,),
        in_specs=[pl.BlockSpec((tr, SGW), lambda i: (i, 0)), pl.BlockSpec((tr, SGW), lambda i: (i, 1)),
                  vec, vec, pl.BlockSpec((SGG, SGB, SGB), lambda i: (0, 0, 0)),
                  pl.BlockSpec((SGB, LANES), lambda i: (0, 0))],
        out_specs=pl.BlockSpec((tr, SGW), lambda i: (i, 0)),
        out_shape=jax.ShapeDtypeStruct((s, SGW), BF16),
        compiler_params=_cparams(("arbitrary",)),
    )(z, z, vgain, vbias, ws, bst)


def _sgu_bwd(z, dy, vgain, vbias, ws, wst, bst, tr, name):
    s = z.shape[0]

    def body(zu_ref, zv_ref, dy_ref, vg_ref, vb_ref, ws_ref, wst_ref, bs_ref,
             dz_ref, rb_ref, rv_ref, dws_ref, dbs_ref, dvn_s):
        @pl.when(pl.program_id(0) == 0)
        def _():
            rb_ref[...] = jnp.zeros_like(rb_ref)
            rv_ref[...] = jnp.zeros_like(rv_ref)
            dws_ref[...] = jnp.zeros_like(dws_ref)
            dbs_ref[...] = jnp.zeros_like(dbs_ref)
        zu = zu_ref[...].astype(F32)
        zv = zv_ref[...].astype(F32)
        u = _gelu(zu)
        vn, vhat, rstd = _sgu_ln(zv, vg_ref[...], vb_ref[...])
        vnb = vn.astype(BF16)
        dyv = dy_ref[...].astype(F32)
        dmix = (dyv * u).astype(BF16)
        mask = _chunk_mask()
        mask_t = _chunk_mask(transposed=True)
        lane = _lane((SGB, LANES))
        dbs = jnp.zeros((SGB, LANES), F32)
        for g in range(SGG):
            w = jnp.where(mask, ws_ref[g], 0.0).astype(BF16)
            wt = jnp.where(mask_t, wst_ref[g], 0.0).astype(BF16)
            dw = jnp.zeros((SGB, SGB), F32)
            for b in range(tr // SGB):
                rs, cs = slice(b * SGB, (b + 1) * SGB), slice(g * SGC, (g + 1) * SGC)
                mixed = _dot(w, vnb[rs, cs]) + bs_ref[:, g:g + 1]
                dz_ref[rs, cs] = (dyv[rs, cs] * mixed * _gelu_grad(zu[rs, cs])).astype(BF16)
                dm = dmix[rs, cs]
                dw = dw + _dot_nt(dm, vnb[rs, cs])
                dbs = dbs + jnp.where(lane == g, jnp.sum(dm.astype(F32), axis=-1, keepdims=True), 0.0)
                dvn_s[rs, cs] = _dot(wt, dm)
            dws_ref[g] += jnp.where(mask, dw, 0.0)
        dbs_ref[...] += dbs
        dvn = dvn_s[...]
        rv_ref[0:1, :] += jnp.sum(dvn * vhat, axis=0, keepdims=True)
        rv_ref[1:2, :] += jnp.sum(dvn, axis=0, keepdims=True)
        dvh = dvn * vg_ref[...]
        dv = rstd * (dvh - jnp.mean(dvh, axis=-1, keepdims=True)
                     - vhat * jnp.mean(dvh * vhat, axis=-1, keepdims=True))
        dz_ref[:, SGW:] = (dv * _gelu_grad(zv)).astype(BF16)
        dzf = dz_ref[...].astype(F32)
        rb_ref[0:1, :] += jnp.sum(dzf, axis=0, keepdims=True)

    vec = pl.BlockSpec((1, SGW), lambda i: (0, 0))
    wsp = pl.BlockSpec((SGG, SGB, SGB), lambda i: (0, 0, 0))
    return pl.pallas_call(
        body, name=name, grid=(s // tr,),
        in_specs=[pl.BlockSpec((tr, SGW), lambda i: (i, 0)), pl.BlockSpec((tr, SGW), lambda i: (i, 1)),
                  pl.BlockSpec((tr, SGW), lambda i: (i, 0)), vec, vec, wsp, wsp,
                  pl.BlockSpec((SGB, LANES), lambda i: (0, 0))],
        out_specs=[pl.BlockSpec((tr, 2 * SGW), lambda i: (i, 0)),
                   pl.BlockSpec((8, 2 * SGW), lambda i: (0, 0)),
                   pl.BlockSpec((8, SGW), lambda i: (0, 0)), wsp,
                   pl.BlockSpec((SGB, LANES), lambda i: (0, 0))],
        out_shape=[jax.ShapeDtypeStruct((s, 2 * SGW), BF16), jax.ShapeDtypeStruct((8, 2 * SGW), F32),
                   jax.ShapeDtypeStruct((8, SGW), F32), jax.ShapeDtypeStruct((SGG, SGB, SGB), F32),
                   jax.ShapeDtypeStruct((SGB, LANES), F32)],
        scratch_shapes=[pltpu.VMEM((tr, SGW), F32)],
        compiler_params=_cparams(("arbitrary",)),
    )(z, z, dy, vgain, vbias, ws, wst, bst)


def _final_loss(x, fg, tgt, gprev, yprev, ts, name):
    s, d = x.shape

    def body(x_ref, fg_ref, t_ref, g_ref, y_ref, l_ref, dx_ref, dy_ref, red_ref):
        @pl.when(pl.program_id(0) == 0)
        def _():
            l_ref[...] = jnp.zeros_like(l_ref)
            red_ref[...] = jnp.zeros_like(red_ref)
        xv = x_ref[...]
        r = _rstd_rows(xv)
        xh = xv * r
        err = xh * fg_ref[...] - t_ref[...]
        l_ref[...] += 0.5 * jnp.sum(jnp.mean(err * err, axis=-1, keepdims=True))
        dyo = err * (1.0 / d)
        dxh = dyo * fg_ref[...]
        dx = r * (dxh - xh * jnp.mean(dxh * xh, axis=-1, keepdims=True))
        dx_ref[...] = dx
        dy_ref[...] = (dx * g_ref[...]).astype(BF16)
        red_ref[0:1, :] += jnp.sum(dyo * xh, axis=0, keepdims=True)
        red_ref[1:2, :] += jnp.sum(dx * y_ref[...].astype(F32), axis=0, keepdims=True)

    row = pl.BlockSpec((ts, d), lambda i: (i, 0))
    vec = pl.BlockSpec((1, d), lambda i: (0, 0))
    return pl.pallas_call(
        body, name=name, grid=(s // ts,),
        in_specs=[row, vec, row, vec, row],
        out_specs=[pl.BlockSpec((8, LANES), lambda i: (0, 0)), row, row, pl.BlockSpec((8, d), lambda i: (0, 0))],
        out_shape=[jax.ShapeDtypeStruct((8, LANES), F32), jax.ShapeDtypeStruct((s, d), F32),
                   jax.ShapeDtypeStruct((s, d), BF16), jax.ShapeDtypeStruct((8, d), F32)],
        compiler_params=_cparams(("arbitrary",)),
    )(x, fg, tgt, gprev, yprev)


def _norm_bwd(xin, dh, dxout, ng, sc, gprev, yprev, ts, name):
    s, d = xin.shape
    has_prev = gprev is not None

    def body(*refs):
        if has_prev:
            x_ref, dh_ref, dxo_ref, ng_ref, sc_ref, g_ref, y_ref, dx_ref, dy_ref, red_ref = refs
        else:
            x_ref, dh_ref, dxo_ref, ng_ref, sc_ref, dx_ref, red_ref = refs

        @pl.when(pl.program_id(0) == 0)
        def _():
            red_ref[...] = jnp.zeros_like(red_ref)
        xv = x_ref[...]
        r = _rstd_rows(xv)
        xh = xv * r
        dhv = dh_ref[...]
        dr = dhv * (1.0 + sc_ref[...])
        t = dr * ng_ref[...]
        dx = dxo_ref[...] + r * (t - xh * jnp.mean(t * xh, axis=-1, keepdims=True))
        dx_ref[...] = dx
        red_ref[0:1, :] += jnp.sum(dhv, axis=0, keepdims=True)
        red_ref[1:2, :] += jnp.sum(dhv * (xh * ng_ref[...]), axis=0, keepdims=True)
        red_ref[2:3, :] += jnp.sum(dr * xh, axis=0, keepdims=True)
        if has_prev:
            dy_ref[...] = (dx * g_ref[...]).astype(BF16)
            red_ref[3:4, :] += jnp.sum(dx * y_ref[...].astype(F32), axis=0, keepdims=True)

    row = pl.BlockSpec((ts, d), lambda i: (i, 0))
    vec = pl.BlockSpec((1, d), lambda i: (0, 0))
    red = pl.BlockSpec((8, d), lambda i: (0, 0))
    if has_prev:
        in_specs, args = [row, row, row, vec, vec, vec, row], (xin, dh, dxout, ng, sc, gprev, yprev)
        out_specs = [row, row, red]
        out_shape = [jax.ShapeDtypeStruct((s, d), F32), jax.ShapeDtypeStruct((s, d), BF16),
                     jax.ShapeDtypeStruct((8, d), F32)]
    else:
        in_specs, args = [row, row, row, vec, vec], (xin, dh, dxout, ng, sc)
        out_specs = [row, red]
        out_shape = [jax.ShapeDtypeStruct((s, d), F32), jax.ShapeDtypeStruct((8, d), F32)]
    return pl.pallas_call(
        body, name=name, grid=(s // ts,), in_specs=in_specs, out_specs=out_specs, out_shape=out_shape,
        compiler_params=_cparams(("arbitrary",)),
    )(*args)


def _ada_mod(c_all, ada_w, ada_b):
    nb = c_all.shape[0]
    da = ada_w.shape[2]

    def body(c_ref, w_ref, b_ref, o_ref, ca_ref):
        cv = c_ref[...]
        ca = cv * _sigmoid(cv)
        ca_ref[...] = ca
        o_ref[0] = lax.dot_general(ca, w_ref[0], (((1,), (0,)), ((), ())), precision=lax.Precision.HIGHEST,
                                   preferred_element_type=F32) + b_ref[0]

    return pl.pallas_call(
        body, name="ada_mod", grid=(2,),
        in_specs=[pl.BlockSpec((nb, D), lambda i: (0, 0)), pl.BlockSpec((1, D, da), lambda i: (i, 0, 0)),
                  pl.BlockSpec((1, 1, da), lambda i: (i, 0, 0))],
        out_specs=[pl.BlockSpec((1, nb, da), lambda i: (i, 0, 0)), pl.BlockSpec((nb, D), lambda i: (0, 0))],
        out_shape=[jax.ShapeDtypeStruct((2, nb, da), F32), jax.ShapeDtypeStruct((nb, D), F32)],
        compiler_params=_cparams(("arbitrary",)),
    )(c_all, ada_w, ada_b)


def _ada_w_grad(c_act_t, dmod):
    nb = c_act_t.shape[1]
    da = dmod.shape[2]
    tn = 512

    def body(c_ref, d_ref, o_ref):
        acc = c_ref[:, 0:1] * d_ref[0, 0:1, :]
        for b in range(1, nb):
            acc = acc + c_ref[:, b:b + 1] * d_ref[0, b:b + 1, :]
        o_ref[0] = acc

    return pl.pallas_call(
        body, name="ada_w_grad", grid=(2, da // tn),
        in_specs=[pl.BlockSpec((D, nb), lambda i, j: (0, 0)), pl.BlockSpec((1, nb, tn), lambda i, j: (i, 0, j))],
        out_specs=pl.BlockSpec((1, D, tn), lambda i, j: (i, 0, j)),
        out_shape=jax.ShapeDtypeStruct((2, D, da), F32),
        compiler_params=_cparams(("arbitrary", "arbitrary")),
    )(c_act_t, dmod)


def _conv_planes(cw, cb):
    cwp = jnp.swapaxes(cw.reshape(3, 2, DFF), 0, 1)
    return jnp.pad(cwp, ((0, 0), (0, 5), (0, 0))), cb.reshape(2, 1, DFF)


def _local_step(x, tgt, mod, wts, small, comm=None):
    wts = dict(wts)
    s = x.shape[0]
    ts, tq, tr, tp = TS, TQ, TR, TP
    zb = lambda n: jnp.zeros((1, n), F32)
    m6 = mod.reshape(2, 6, 1, D)
    sh1, sc1, g1, sh2, sc2, g2 = ([m6[i, k] for i in range(2)] for k in range(6))
    n1g, n2g = small["norm1_g"], small["norm2_g"]
    row = lambda a, i: a[i:i + 1]

    qg2 = jnp.tile(small["fox_q_gain"], (1, 2))
    kg2 = jnp.tile(small["fox_k_gain"], (1, 2))
    bfp = jnp.pad(small["fox_b_f"], ((0, 0), (0, LANES - H)))
    proj, h1 = _norm_mod_matmul(x, row(n1g, 0), sc1[0], sh1[0], wts["fox_w_in"], zb(FOX_NP), F32, ts, 1408, "fox_in")
    qa, ka, va = _fox_post(proj, qg2, kg2, bfp, tp, "fox_post")
    att, ql, gathered = _attn_fwd(qa, ka, va, tq, "attn_fwd", shards=comm["shards"] if comm else ())
    if comm:
        wts.update(comm["make_wts"](gathered))
    gated = _gate(att, proj, ts, "fox_gate")
    x1, y0 = _matmul_residual(gated, wts["fox_w_out"], x, g1[0], ts, "fox_out")

    def ffn_fwd(xin, i, tag):
        cw, cb = _conv_planes(small["ffn_conv_w"][i], small["ffn_conv_b"][i])
        a, h = _norm_mod_matmul(xin, row(n2g, i), sc2[i], sh2[i], wts["ffn_w_up"][i], zb(2 * DFF), BF16, ts, 1408,
                                "ffn_up" + tag, planes=2)
        f = _conv_gate(a, cw, cb, ts, "ffn_conv" + tag)
        xo, y = _matmul_residual(f, wts["ffn_w_down"][i], xin, g2[i], ts, "ffn_down" + tag)
        return xo, (a, h, f, y, cw, cb)

    x2, ffn0 = ffn_fwd(x1, 0, "0")

    bst = jnp.pad(small["sgu_b_s"].T, ((0, 0), (0, LANES - SGG)))
    ws = small["sgu_w_s"]
    z, h3 = _norm_mod_matmul(x2, row(n1g, 1), sc1[1], sh1[1], wts["sgu_w_in"], small["sgu_b_in"], BF16, ts, 1024,
                             "sgu_in")
    yy = _sgu_fwd(z, small["sgu_v_gain"], small["sgu_v_bias"], ws, bst, tr, "sgu_mix")
    x3, y1 = _matmul_residual(yy, wts["sgu_w_out"], x2, g1[1], ts, "sgu_out")
    x4, ffn1 = ffn_fwd(x3, 1, "1")

    lsum, dx4, dy, redf = _final_loss(x4, small["final_g"], tgt, g2[1], ffn1[3], ts, "final_loss")
    grads = {"final_g": redf[0]}
    dmod = [[None] * 6, [None] * 6]
    dmod[1][5] = redf[1]

    def ffn_bwd(dxo, dy2, xin, i, saved, gprev, yprev, tag):
        a, h, f, _, cw, cb = saved
        wd, wu = wts["ffn_w_down"][i], wts["ffn_w_up"][i]
        g_wd = _matmul(f, dy2, True, False, 1408, D, ts, BF16, "ffn_dwdown" + tag)
        df = _matmul(dy2, wd, False, True, ts, 1408, D, BF16, "ffn_df" + tag)
        da, redc = _conv_gate_bwd(a, df, cw, cb, ts, "ffn_conv_bwd" + tag)
        g_wu = _matmul(h, da, True, False, D, 1408, ts, BF16, "ffn_dwup" + tag, out_parts=N_CHIP)
        dh = _matmul(da, wu, False, True, ts, D, 1408, F32, "ffn_dh" + tag)
        outs = _norm_bwd(xin, dh, dxo, row(n2g, i), sc2[i], gprev, yprev, ts, "ffn_norm_bwd" + tag)
        return outs, g_wd, g_wu, redc

    (dx3, dy1, red), g_wd1, g_wu1, redc1 = ffn_bwd(dx4, dy, x3, 1, ffn1, g1[1], y1, "1")
    dmod[1][3], dmod[1][4], dn2g1, dmod[1][2] = red[0], red[1], red[2], red[3]

    g_swo = _matmul(yy, dy1, True, False, 1024, D, ts, BF16, "sgu_dwout")
    dyy = _matmul(dy1, wts["sgu_w_out"], False, True, ts, 1024, D, BF16, "sgu_dyy")
    wst = jnp.swapaxes(ws, 1, 2)
    dz, rb, rv, dws, dbst = _sgu_bwd(z, dyy, small["sgu_v_gain"], small["sgu_v_bias"], ws, wst, bst, tr, "sgu_mix_bwd")
    g_swi = _matmul(h3, dz, True, False, D, 1024, ts, BF16, "sgu_dwin", out_parts=N_CHIP)
    dh3 = _matmul(dz, wts["sgu_w_in"], False, True, ts, D, 1024, F32, "sgu_dh")
    dx2, dy2_0, red = _norm_bwd(x2, dh3, dx3, row(n1g, 1), sc1[1], g2[0], ffn0[3], ts, "sgu_norm_bwd")
    dmod[1][0], dmod[1][1], dn1g1, dmod[0][5] = red[0], red[1], red[2], red[3]

    (dx1, dy0, red), g_wd0, g_wu0, redc0 = ffn_bwd(dx2, dy2_0, x1, 0, ffn0, g1[0], y0, "0")
    dmod[0][3], dmod[0][4], dn2g0, dmod[0][2] = red[0], red[1], red[2], red[3]

    g_fwo = _matmul(gated, dy0, True, False, D, D, ts, BF16, "fox_dwout")
    dgated = _matmul(dy0, wts["fox_w_out"], False, True, ts, D, D, F32, "fox_dgated")
    doa, dop = _attn_bwd_prep(dgated, att, proj, ts, "attn_bwd_prep")
    css = comm["rs_prepare"]([g_fwo, g_swi, g_swo, g_wu0, g_wu1, g_wd0, g_wd1]) if comm else []
    dqa, dka, dva, rcvs = _attn_bwd(ql, ka, va, doa, tq, "attn_bwd", css=css)
    dproj, redx = _fox_post_bwd(proj, dqa, dka, dva, dop, qg2, kg2, bfp, tp, "fox_post_bwd")
    g_fwi = _matmul(h1, dproj, True, False, D, 1408, ts, BF16, "fox_dwin")
    dh1 = _matmul(dproj, wts["fox_w_in"], False, True, ts, D, 1408, F32, "fox_dh")
    dx0, red = _norm_bwd(x, dh1, dx1, row(n1g, 0), sc1[0], None, None, ts, "fox_norm_bwd")
    dmod[0][0], dmod[0][1], dn1g0 = red[0], red[1], red[2]

    grads.update(
        fox_w_in=g_fwi, fox_w_out=g_fwo, sgu_w_in=g_swi, sgu_w_out=g_swo,
        ffn_w_up=[g_wu0, g_wu1], ffn_w_down=[g_wd0, g_wd1],
        fox_q_gain=redx[0, :DH] + redx[0, DH:], fox_k_gain=redx[1, :DH] + redx[1, DH:], fox_b_f=redx[2, :H],
        sgu_b_in=rb[0], sgu_v_gain=rv[0], sgu_v_bias=rv[1], sgu_w_s=dws, sgu_b_s=dbst[:, :SGG].T,
        ffn_conv_w=jnp.stack([jnp.swapaxes(r[:, 0:3], 0, 1).reshape(3, 2 * DFF) for r in (redc0, redc1)]),
        ffn_conv_b=jnp.stack([r[:, 3].reshape(2 * DFF) for r in (redc0, redc1)]),
        norm1_g=jnp.stack([dn1g0, dn1g1]), norm2_g=jnp.stack([dn2g0, dn2g1]),
    )
    dmod_arr = jnp.stack([jnp.concatenate(dmod[0]), jnp.concatenate(dmod[1])])
    return lsum[0, 0], dx0, grads, dmod_arr, (css, rcvs)


N_DEV = 8
N_CHIP = 4
HBM_SPEC = pl.BlockSpec(memory_space=pltpu.HBM)
VMEM_SPEC = pl.BlockSpec(memory_space=pltpu.VMEM)


def _mesh_pos():
    return lax.axis_index("x"), lax.axis_index("y"), lax.axis_index("c")


def _other_chips(x, y):
    return [(1 - x, y), (x, 1 - y), (1 - x, 1 - y)]


def _remote(src, dst, ssem, rsem, dev):
    return pltpu.make_async_remote_copy(src_ref=src, dst_ref=dst, send_sem=ssem, recv_sem=rsem,
                                        device_id=dev, device_id_type=MESH)


def _allgather8(xb, name):
    m_per, n = xb.shape

    def body(x_ref, out_ref, send_sems, recv_sems, local_sem):
        x, y, c = _mesh_pos()
        me, sibling = (x, y, c), (x, y, 1 - c)
        chips = _other_chips(x, y)

        def rows(px, py, pc):
            return out_ref.at[pl.ds((4 * px + 2 * py + pc) * m_per, m_per), :]

        def copy(k, block, to, src=None):
            return _remote(rows(*block) if src is None else src, rows(*block),
                           send_sems.at[k], recv_sems.at[k], to)

        mine = pltpu.make_async_copy(x_ref, rows(*me), local_sem)
        mine.start()
        first = [copy(0, me, sibling, src=x_ref)]
        first += [copy(1 + j, me, (*chip, c), src=x_ref) for j, chip in enumerate(chips)]
        for cp in first:
            cp.start()
        passed = [copy(4 + j, (*chip, c), sibling) for j, chip in enumerate(chips)]
        for j, chip in enumerate(chips):
            copy(1 + j, (*chip, c), me).wait_recv()
            passed[j].start()
        copy(0, sibling, me).wait_recv()
        for j, chip in enumerate(chips):
            copy(4 + j, (*chip, 1 - c), me).wait_recv()
        for cp in first + passed:
            cp.wait_send()
        mine.wait()

    return pl.pallas_call(
        body, name=name,
        out_shape=jax.ShapeDtypeStruct((N_DEV * m_per, n), xb.dtype),
        in_specs=[VMEM_SPEC], out_specs=VMEM_SPEC,
        scratch_shapes=[pltpu.SemaphoreType.DMA((7,)), pltpu.SemaphoreType.DMA((7,)), pltpu.SemaphoreType.DMA],
        compiler_params=pltpu.CompilerParams(vmem_limit_bytes=V7X_VMEM_LIMIT),
    )(xb)


def _gather_shards(shards, name):
    na = len(shards)

    def body(*refs):
        p_refs, o_refs = refs[:na], refs[na:2 * na]
        send_sems, recv_sems, pass_send, pass_recv = refs[2 * na:]
        x, y, c = _mesh_pos()
        me = 2 * x + y
        sibling = (x, y, 1 - c)
        chips = _other_chips(x, y)

        def half(a, ci, hf):
            rh = shards[a].shape[0] // 2
            return o_refs[a].at[ci, pl.ds(hf * rh, rh), :]

        sends = []
        for a in range(na):
            rh = shards[a].shape[0] // 2
            for k, chip in enumerate(chips):
                sends.append(_remote(p_refs[a].at[pl.ds(c * rh, rh), :], half(a, me, c),
                                     send_sems.at[3 * a + k], recv_sems.at[3 * a + k], (*chip, c)))
        for cp in sends:
            cp.start()
        passed = []
        for a in range(na):
            for k, chip in enumerate(chips):
                ci = 2 * chip[0] + chip[1]
                _remote(half(a, ci, c), half(a, ci, c), send_sems.at[3 * a + k], recv_sems.at[3 * a + k],
                        (*chip, c)).wait_recv()
                cp = _remote(half(a, ci, c), half(a, ci, c), pass_send.at[3 * a + k], pass_recv.at[3 * a + k], sibling)
                cp.start()
                passed.append(cp)
        for a in range(na):
            for k, chip in enumerate(chips):
                ci = 2 * chip[0] + chip[1]
                _remote(half(a, ci, 1 - c), half(a, ci, 1 - c), pass_send.at[3 * a + k], pass_recv.at[3 * a + k],
                        sibling).wait_recv()
        for cp in sends + passed:
            cp.wait_send()

    return pl.pallas_call(
        body, name=name,
        out_shape=[jax.ShapeDtypeStruct((N_CHIP,) + p.shape, p.dtype) for p in shards],
        in_specs=[HBM_SPEC] * na, out_specs=[HBM_SPEC] * na,
        scratch_shapes=[pltpu.SemaphoreType.DMA((3 * na,))] * 4,
    )(*shards)


def _rs_to_sibling(gs, name):
    na = len(gs)

    def body(*refs):
        g_refs, o_refs, ssems, rsems = refs[:na], refs[na:2 * na], refs[2 * na], refs[2 * na + 1]
        x, y, c = _mesh_pos()
        cps = []
        for a in range(na):
            rh = gs[a].shape[1] // 2
            cp = _remote(g_refs[a].at[:, pl.ds((1 - c) * rh, rh), :], o_refs[a], ssems.at[a], rsems.at[a],
                         (x, y, 1 - c))
            cp.start()
            cps.append(cp)
        for cp in cps:
            cp.wait()

    return pl.pallas_call(
        body, name=name,
        out_shape=[jax.ShapeDtypeStruct((g.shape[0], g.shape[1] // 2, g.shape[2]), g.dtype) for g in gs],
        in_specs=[HBM_SPEC] * na, out_specs=[HBM_SPEC] * na,
        scratch_shapes=[pltpu.SemaphoreType.DMA((na,)), pltpu.SemaphoreType.DMA((na,))],
    )(*gs)


def _rs_chip_sum(g, sib, c_arr, tr, name):
    nc, r, n = g.shape
    rh = r // 2
    g4 = g.reshape(nc, 2, rh, n)

    def body(c_ref, g_ref, s_ref, o_ref):
        o_ref[...] = (g_ref[0].astype(F32) + s_ref[...].astype(F32)).astype(BF16)

    return pl.pallas_call(
        body, name=name, out_shape=jax.ShapeDtypeStruct((nc, rh, n), BF16),
        grid_spec=pltpu.PrefetchScalarGridSpec(
            num_scalar_prefetch=1, grid=(nc, rh // tr),
            in_specs=[pl.BlockSpec((1, 1, tr, n), lambda j, i, cr: (j, cr[0], i, 0)),
                      pl.BlockSpec((1, tr, n), lambda j, i, cr: (j, i, 0))],
            out_specs=pl.BlockSpec((1, tr, n), lambda j, i, cr: (j, i, 0))),
        compiler_params=_cparams(("arbitrary", "arbitrary")),
    )(c_arr, g4, sib)


def _rs_across_chips(css, name):
    na = len(css)

    def body(*refs):
        cs_refs, o_refs, send_sems, recv_sems = refs[:na], refs[na:2 * na], refs[2 * na], refs[2 * na + 1]
        x, y, c = _mesh_pos()
        cps = []
        for a in range(na):
            for k, chip in enumerate(_other_chips(x, y)):
                ci = 2 * chip[0] + chip[1]
                cp = _remote(cs_refs[a].at[ci], o_refs[a].at[k], send_sems.at[3 * a + k], recv_sems.at[3 * a + k],
                             (*chip, c))
                cp.start()
                cps.append(cp)
        for cp in cps:
            cp.wait()

    return pl.pallas_call(
        body, name=name, out_shape=[jax.ShapeDtypeStruct((3,) + cs.shape[1:], cs.dtype) for cs in css],
        in_specs=[HBM_SPEC] * na, out_specs=[HBM_SPEC] * na,
        scratch_shapes=[pltpu.SemaphoreType.DMA((3 * na,)), pltpu.SemaphoreType.DMA((3 * na,))],
    )(*css)


def _rs_final_sum(cs, rcv, me_arr, tr, name):
    nc, rh, n = cs.shape

    def body(m_ref, c_ref, r_ref, o_ref):
        acc = c_ref[0].astype(F32)
        for k in range(3):
            acc = acc + r_ref[k].astype(F32)
        o_ref[...] = acc

    return pl.pallas_call(
        body, name=name, out_shape=jax.ShapeDtypeStruct((rh, n), F32),
        grid_spec=pltpu.PrefetchScalarGridSpec(
            num_scalar_prefetch=1, grid=(rh // tr,),
            in_specs=[pl.BlockSpec((1, tr, n), lambda i, mr: (mr[0], i, 0)),
                      pl.BlockSpec((3, tr, n), lambda i, mr: (0, i, 0))],
            out_specs=pl.BlockSpec((tr, n), lambda i, mr: (i, 0))),
        compiler_params=_cparams(("arbitrary",)),
    )(me_arr, cs, rcv)


def _rs_swap_halves(halves, name):
    na = len(halves)

    def body(*refs):
        h_refs, o_refs, ssems, rsems = refs[:na], refs[na:2 * na], refs[2 * na], refs[2 * na + 1]
        x, y, c = _mesh_pos()
        cps = []
        for a in range(na):
            cp = _remote(h_refs[a], o_refs[a], ssems.at[a], rsems.at[a], (x, y, 1 - c))
            cp.start()
            cps.append(cp)
        for cp in cps:
            cp.wait()

    return pl.pallas_call(
        body, name=name, out_shape=[jax.ShapeDtypeStruct(h.shape, h.dtype) for h in halves],
        in_specs=[HBM_SPEC] * na, out_specs=[HBM_SPEC] * na,
        scratch_shapes=[pltpu.SemaphoreType.DMA((na,)), pltpu.SemaphoreType.DMA((na,))],
    )(*halves)


def _sum8(g, name):
    nd, r, n = g.shape

    def body(g_ref, o_ref):
        acc = g_ref[0]
        for k in range(1, nd):
            acc = acc + g_ref[k]
        o_ref[...] = acc

    return pl.pallas_call(
        body, name=name, grid=(r // 8,),
        in_specs=[pl.BlockSpec((nd, 8, n), lambda i: (0, i, 0))],
        out_specs=pl.BlockSpec((8, n), lambda i: (i, 0)),
        out_shape=jax.ShapeDtypeStruct((r, n), F32),
        compiler_params=_cparams(("arbitrary",)),
    )(g)


def _adamw(w, g, m, v, name):
    r, n = w.shape
    tr = 128 if r % 128 == 0 else 8
    bc1 = 1.0 - ADAM_B1 ** ADAM_STEP
    bc2 = 1.0 - ADAM_B2 ** ADAM_STEP

    def body(w_ref, g_ref, m_ref, v_ref, d_ref, mo_ref, vo_ref):
        gv = g_ref[...]
        mn = ADAM_B1 * m_ref[...] + (1.0 - ADAM_B1) * gv
        vn = ADAM_B2 * v_ref[...] + (1.0 - ADAM_B2) * (gv * gv)
        d_ref[...] = -ADAM_LR * ((mn / bc1) / (jnp.sqrt(vn / bc2) + ADAM_EPS) + ADAM_WD * w_ref[...])
        mo_ref[...] = mn
        vo_ref[...] = vn

    blk = pl.BlockSpec((tr, n), lambda i: (i, 0))
    shp = jax.ShapeDtypeStruct((r, n), F32)
    return pl.pallas_call(
        body, name=name, grid=(r // tr,), in_specs=[blk] * 4, out_specs=[blk] * 3, out_shape=[shp] * 3,
        compiler_params=_cparams(("arbitrary",)),
    )(w, g, m, v)


ROW = 1024
PACK_ROWS = 7168
BIG = ("fox_w_in", "fox_w_out", "sgu_w_in", "sgu_w_out", "ffn_w_up", "ffn_w_down")
SMALL_SHARDED = ("sgu_b_in", "sgu_v_gain", "sgu_v_bias", "ffn_conv_w")
SMALL_REPL = ("fox_b_f", "fox_q_gain", "fox_k_gain", "sgu_w_s", "sgu_b_s", "ffn_conv_b", "ada_b",
              "norm1_g", "norm2_g", "final_g")
WEIGHTS = ("fox_w_in", "fox_b_f", "fox_q_gain", "fox_k_gain", "fox_w_out", "sgu_w_in", "sgu_b_in", "sgu_v_gain",
           "sgu_v_bias", "sgu_w_s", "sgu_b_s", "sgu_w_out", "ffn_w_up", "ffn_conv_w", "ffn_conv_b", "ffn_w_down",
           "ada_w", "ada_b", "norm1_g", "norm2_g", "final_g")


def _rows_of(a, mult=1):
    flat = a.reshape(-1)
    rows = -(-flat.shape[0] // ROW)
    rows = -(-rows // mult) * mult
    return jnp.pad(flat, (0, rows * ROW - flat.shape[0])).reshape(rows, ROW)


def _pack(parts, mult, total=None):
    p = jnp.concatenate([_rows_of(a, mult) for a in parts], axis=0)
    if total is not None:
        p = jnp.pad(p, ((0, total - p.shape[0]), (0, 0)))
    return p


def _unpack(pack, shapes, mult):
    out, r0 = [], 0
    for shp in shapes:
        size = int(np.prod(shp))
        rows = -(-(-(-size // ROW)) // mult) * mult
        out.append(pack[r0:r0 + rows].reshape(-1)[:size].reshape(shp))
        r0 += rows
    return out


def _big_shards(t):
    return [t["fox_w_in"][0], t["fox_w_out"][0], t["sgu_w_in"][0], t["sgu_w_out"][0],
            t["ffn_w_up"][0], t["ffn_w_up"][1], t["ffn_w_down"][0], t["ffn_w_down"][1]]


def _row_tile(rows):
    return next(t for t in (512, 352, 256, 128, 64) if rows % t == 0)


def kernel(x, c, fox_w_in, fox_b_f, fox_q_gain, fox_k_gain, fox_w_out, sgu_w_in, sgu_b_in, sgu_v_gain, sgu_v_bias, sgu_w_s, sgu_b_s, sgu_w_out, ffn_w_up, ffn_conv_w, ffn_conv_b, ffn_w_down, ada_w, ada_b, norm1_g, norm2_g, final_g, loss_target, m_fox_w_in, m_fox_b_f, m_fox_q_gain, m_fox_k_gain, m_fox_w_out, m_sgu_w_in, m_sgu_b_in, m_sgu_v_gain, m_sgu_v_bias, m_sgu_w_s, m_sgu_b_s, m_sgu_w_out, m_ffn_w_up, m_ffn_conv_w, m_ffn_conv_b, m_ffn_w_down, m_ada_w, m_ada_b, m_norm1_g, m_norm2_g, m_final_g, v_fox_w_in, v_fox_b_f, v_fox_q_gain, v_fox_k_gain, v_fox_w_out, v_sgu_w_in, v_sgu_b_in, v_sgu_v_gain, v_sgu_v_bias, v_sgu_w_s, v_sgu_b_s, v_sgu_w_out, v_ffn_w_up, v_ffn_conv_w, v_ffn_conv_b, v_ffn_w_down, v_ada_w, v_ada_b, v_norm1_g, v_norm2_g, v_final_g):
    w = dict(fox_w_in=fox_w_in, fox_b_f=fox_b_f, fox_q_gain=fox_q_gain, fox_k_gain=fox_k_gain, fox_w_out=fox_w_out,
             sgu_w_in=sgu_w_in, sgu_b_in=sgu_b_in, sgu_v_gain=sgu_v_gain, sgu_v_bias=sgu_v_bias, sgu_w_s=sgu_w_s,
             sgu_b_s=sgu_b_s, sgu_w_out=sgu_w_out, ffn_w_up=ffn_w_up, ffn_conv_w=ffn_conv_w, ffn_conv_b=ffn_conv_b,
             ffn_w_down=ffn_w_down, ada_w=ada_w, ada_b=ada_b, norm1_g=norm1_g, norm2_g=norm2_g, final_g=final_g)
    mom = dict(fox_w_in=m_fox_w_in, fox_b_f=m_fox_b_f, fox_q_gain=m_fox_q_gain, fox_k_gain=m_fox_k_gain,
               fox_w_out=m_fox_w_out, sgu_w_in=m_sgu_w_in, sgu_b_in=m_sgu_b_in, sgu_v_gain=m_sgu_v_gain,
               sgu_v_bias=m_sgu_v_bias, sgu_w_s=m_sgu_w_s, sgu_b_s=m_sgu_b_s, sgu_w_out=m_sgu_w_out,
               ffn_w_up=m_ffn_w_up, ffn_conv_w=m_ffn_conv_w, ffn_conv_b=m_ffn_conv_b, ffn_w_down=m_ffn_w_down,
               ada_w=m_ada_w, ada_b=m_ada_b, norm1_g=m_norm1_g, norm2_g=m_norm2_g, final_g=m_final_g)
    var = dict(fox_w_in=v_fox_w_in, fox_b_f=v_fox_b_f, fox_q_gain=v_fox_q_gain, fox_k_gain=v_fox_k_gain,
               fox_w_out=v_fox_w_out, sgu_w_in=v_sgu_w_in, sgu_b_in=v_sgu_b_in, sgu_v_gain=v_sgu_v_gain,
               sgu_v_bias=v_sgu_v_bias, sgu_w_s=v_sgu_w_s, sgu_b_s=v_sgu_b_s, sgu_w_out=v_sgu_w_out,
               ffn_w_up=v_ffn_w_up, ffn_conv_w=v_ffn_conv_w, ffn_conv_b=v_ffn_conv_b, ffn_w_down=v_ffn_w_down,
               ada_w=v_ada_w, ada_b=v_ada_b, norm1_g=v_norm1_g, norm2_g=v_norm2_g, final_g=v_final_g)

    ax, ay, ac = _mesh_pos()
    chip = 2 * ax + ay
    dev = 2 * chip + ac

    small_shard_shapes = tuple(w[n].shape for n in SMALL_SHARDED)
    blk = _pack([c] + [w[n] for n in SMALL_SHARDED], 1, 16)
    gat = _allgather8(blk, "gather_small").reshape(N_DEV, 16, ROW)
    c_all = gat[:, 0, :]
    per_chip = [_unpack(gat[2 * j, 1:], small_shard_shapes, 1) for j in range(N_CHIP)]
    full_small = {n: jnp.concatenate([per_chip[j][i] for j in range(N_CHIP)], axis=-1)
                  for i, n in enumerate(SMALL_SHARDED)}

    mine = [a.astype(BF16) for a in _big_shards(w)]
    with_own = lambda gat, own: [lax.dynamic_update_slice(g_, m_[None], (chip, 0, 0)) for g_, m_ in zip(gat, own)]
    fwi, = with_own(_gather_shards(mine[:1], "gather_fox_w_in"), mine[:1])
    fwi_full = jnp.concatenate([fwi[j] for j in range(N_CHIP)] + [jnp.zeros((D, FOX_NP - FOX_N), BF16)], axis=1)
    wts = dict(fox_w_in=fwi_full)

    def make_wts(gathered):
        fwo, swi, swo, up0, up1, dn0, dn1 = with_own(gathered, mine[1:])
        return dict(fox_w_out=fwo.reshape(D, D), sgu_w_in=swi, sgu_w_out=swo.reshape(SGW, D),
                    ffn_w_up=[up0, up1], ffn_w_down=[dn0.reshape(DFF, D), dn1.reshape(DFF, D)])

    c_arr = jnp.reshape(ac, (1,)).astype(jnp.int32)
    me_arr = jnp.reshape(chip, (1,)).astype(jnp.int32)

    def chip_sums(glist, tag):
        sibs = _rs_to_sibling(glist, "rs_sibling" + tag)
        return [_rs_chip_sum(g_, s_, c_arr, _row_tile(s_.shape[1]), "rs_chip_sum%s%d" % (tag, a))
                for a, (g_, s_) in enumerate(zip(glist, sibs))]

    def rs_prepare(gl):
        g_fwo, g_swi, g_swo, g_wu0, g_wu1, g_wd0, g_wd1 = gl
        return chip_sums([g_fwo.reshape(N_CHIP, 256, D), g_swi, g_swo.reshape(N_CHIP, 512, D), g_wu0, g_wu1,
                          g_wd0.reshape(N_CHIP, 704, D), g_wd1.reshape(N_CHIP, 704, D)], "")

    comm = dict(shards=mine[1:], make_wts=make_wts, rs_prepare=rs_prepare)

    da = ada_w.shape[2]
    ada_b_cols = lax.dynamic_slice_in_dim(ada_b, chip * da, da, axis=1)[:, None, :]
    mod_cols, c_act = _ada_mod(c_all, ada_w, ada_b_cols)
    mod_all = _allgather8(mod_cols.reshape(-1, ROW), "gather_mod").reshape(N_DEV, 2, N_DEV, da)
    mod_mine = lax.dynamic_index_in_dim(mod_all[0::2], dev, axis=2, keepdims=False)
    mod = jnp.swapaxes(mod_mine, 0, 1).reshape(2, N_CHIP * da)

    small = dict(norm1_g=norm1_g, norm2_g=norm2_g, final_g=final_g[None], fox_q_gain=fox_q_gain,
                 fox_k_gain=fox_k_gain, fox_b_f=fox_b_f, sgu_b_in=full_small["sgu_b_in"],
                 sgu_v_gain=full_small["sgu_v_gain"], sgu_v_bias=full_small["sgu_v_bias"], sgu_w_s=sgu_w_s[0],
                 sgu_b_s=sgu_b_s[0], ffn_conv_w=full_small["ffn_conv_w"], ffn_conv_b=ffn_conv_b)
    loss_dev, dx, g, dmod, (css, rcvs) = _local_step(x[0], loss_target[0], mod, wts, small, comm)

    g["ada_b"] = dmod
    small_names = ("ada_b",) + SMALL_SHARDED + tuple(n for n in SMALL_REPL if n != "ada_b")
    gs = _pack([g[n] for n in small_names], 1)
    rows_s = -(-gs.shape[0] // 8) * 8
    gs = jnp.pad(gs, ((0, rows_s - gs.shape[0]), (0, 0)))
    gs_all = _allgather8(gs, "gather_small_grads").reshape(N_DEV, rows_s, ROW)
    gsum = _sum8(gs_all, "sum_small_grads")
    full_shapes = {n: w[n].shape for n in SMALL_REPL}
    full_shapes.update({n: w[n].shape[:-1] + (w[n].shape[-1] * N_CHIP,) for n in SMALL_SHARDED})
    gfull = dict(zip(small_names, _unpack(gsum, [full_shapes[n] for n in small_names], 1)))
    grads = {n: gfull[n] for n in SMALL_REPL}
    for n in SMALL_SHARDED:
        width = w[n].shape[-1]
        grads[n] = lax.dynamic_slice_in_dim(gfull[n], chip * width, width, axis=gfull[n].ndim - 1)
    dmod_all = gs_all[:, :12, :].reshape(N_DEV, 2, N_CHIP * da)
    dmod_cols = jnp.swapaxes(lax.dynamic_slice_in_dim(dmod_all, chip * da, da, axis=2), 0, 1)
    grads["ada_w"] = _ada_w_grad(c_act.T, dmod_cols)

    gfi = jnp.stack([g["fox_w_in"][:, 1028 * j:1028 * (j + 1)] for j in range(N_CHIP)])
    cs_fox = chip_sums([gfi], "_fox")
    css = cs_fox + list(css)
    rcvs = list(_rs_across_chips(cs_fox, "rs_chips_fox")) + list(rcvs)
    halves = [_rs_final_sum(cs_, r_, me_arr, _row_tile(cs_.shape[1]), "rs_final_sum%d" % a)
              for a, (cs_, r_) in enumerate(zip(css, rcvs))]
    others = _rs_swap_halves(halves, "rs_swap")
    red = [jnp.concatenate([jnp.where(ac == 0, h_, o_), jnp.where(ac == 0, o_, h_)]) for h_, o_ in zip(halves, others)]
    grads.update(fox_w_in=red[0], fox_w_out=red[1], sgu_w_in=red[2], sgu_w_out=red[3],
                 ffn_w_up=jnp.stack([red[4], red[5]]), ffn_w_down=jnp.stack([red[6], red[7]]))

    delta, new_m, new_v = {}, {}, {}
    for n in BIG + ("ada_w",):
        shp = w[n].shape
        two_d = lambda a: a.reshape(-1, shp[-1])
        d_, m_, v_ = _adamw(two_d(w[n]), two_d(grads[n]), two_d(mom[n]), two_d(var[n]), "adamw_" + n)
        delta[n], new_m[n], new_v[n] = d_.reshape(shp), m_.reshape(shp), v_.reshape(shp)
    rest = SMALL_SHARDED + SMALL_REPL
    packs = [_pack([t[n] for n in rest], 1) for t in (w, grads, mom, var)]
    rows_r = -(-packs[0].shape[0] // 8) * 8
    packs = [jnp.pad(p, ((0, rows_r - p.shape[0]), (0, 0))) for p in packs]
    outs = _adamw(*packs, "adamw_small")
    for t, o in zip((delta, new_m, new_v), outs):
        t.update(zip(rest, _unpack(o, [w[n].shape for n in rest], 1)))

    loss = lax.psum(loss_dev, ("x", "y", "c"))
    return (loss, dx[None], *[grads[n].reshape(w[n].shape) for n in WEIGHTS], *[delta[n] for n in WEIGHTS],
            *[new_m[n] for n in WEIGHTS], *[new_v[n] for n in WEIGHTS])
```

```python
import functools
import math

import numpy as np
import jax
import jax.numpy as jnp
from jax import lax
from jax.experimental import pallas as pl
from jax.experimental.pallas import tpu as pltpu

F32 = jnp.float32
BF16 = jnp.bfloat16
MESH = pl.DeviceIdType.MESH

D = 1024
H = 16
DH = 64
NP = H // 2
LANES = 128
DFF = 2816
SGW = 2048
SGG = 8
SGC = 256
SGB = 128
CHUNK = 64
EPS = 1e-6
FOX_N = 4 * D + H
FOX_NP = 4224
GT = 256
NGT = DFF // GT
SCALE = DH ** -0.5
LOG2E = 1.4426950408889634

ADAM_LR = 0.001
ADAM_B1 = 0.9
ADAM_B2 = 0.999
ADAM_EPS = 1e-08
ADAM_WD = 0.01
ADAM_STEP = 10

V7X_VMEM_LIMIT = 56 * 1024 * 1024

L_F = 64
L_NF = 67
L_LSE = 70


def _cparams(sem=None):
    return pltpu.CompilerParams(dimension_semantics=sem, vmem_limit_bytes=V7X_VMEM_LIMIT)


def _split3(x):
    hi = x.astype(BF16)
    r = x - hi.astype(F32)
    mid = r.astype(BF16)
    lo = (r - mid.astype(F32)).astype(BF16)
    return hi, mid, lo


def _dot(a, b, dims=(((1,), (0,)), ((), ()))):
    return lax.dot_general(a, b, dims, preferred_element_type=F32)


def _dot_nt(a, b):
    return _dot(a, b, (((1,), (1,)), ((), ())))


def _dot_tn(a, b):
    return _dot(a, b, (((0,), (0,)), ((), ())))


def _exact_dot(m_bf16, x_f32):
    hi, mid, lo = _split3(x_f32)
    return _dot(m_bf16, hi) + _dot(m_bf16, mid) + _dot(m_bf16, lo)


def _exact_dot_r(x_f32, m_bf16):
    hi, mid, lo = _split3(x_f32)
    return _dot(hi, m_bf16) + _dot(mid, m_bf16) + _dot(lo, m_bf16)


def _head_block_ones():
    r = lax.broadcasted_iota(jnp.int32, (LANES, LANES), 0) // DH
    c = lax.broadcasted_iota(jnp.int32, (LANES, LANES), 1) // DH
    return (r == c).astype(BF16)


def _sigmoid(x):
    return 1.0 / (1.0 + jnp.exp(-x))


def _gelu(x):
    c = math.sqrt(2.0 / math.pi)
    return 0.5 * x * (1.0 + jnp.tanh(c * (x + 0.044715 * (x * x * x))))


def _gelu_grad(x):
    c = math.sqrt(2.0 / math.pi)
    t = jnp.tanh(c * (x + 0.044715 * (x * x * x)))
    return 0.5 * (1.0 + t) + 0.5 * x * (1.0 - t * t) * c * (1.0 + 3 * 0.044715 * (x * x))


def _rstd_rows(x):
    return lax.rsqrt(jnp.mean(x * x, axis=-1, keepdims=True) + EPS)


def _norm_mod_matmul(x, ng, sc, sh, w, bias, out_dtype, ts, tn, name, planes=1):
    s, d = x.shape
    ns = w.shape[-1]
    n = w.shape[0] * ns if w.ndim == 3 else ns
    nc = n // planes

    def body(x_ref, ng_ref, sc_ref, sh_ref, w_ref, b_ref, o_ref, h_ref):
        xv = x_ref[...]
        h = (xv * _rstd_rows(xv) * ng_ref[...] * (1.0 + sc_ref[...]) + sh_ref[...]).astype(BF16)
        h_ref[...] = h
        for e in range(planes):
            for c0 in range(0, nc, tn):
                g0 = e * nc + c0
                wv = w_ref[g0 // ns, :, g0 % ns:g0 % ns + tn] if w.ndim == 3 else w_ref[:, g0:g0 + tn]
                val = (_dot(h, wv) + b_ref[:, g0:g0 + tn]).astype(out_dtype)
                if planes == 1:
                    o_ref[:, c0:c0 + tn] = val
                else:
                    o_ref[e, :, c0:c0 + tn] = val

    vec = pl.BlockSpec((1, d), lambda i: (0, 0))
    w_spec = (pl.BlockSpec(w.shape, lambda i: (0, 0, 0)) if w.ndim == 3 else pl.BlockSpec((d, n), lambda i: (0, 0)))
    if planes == 1:
        o_spec, o_shape = pl.BlockSpec((ts, n), lambda i: (i, 0)), (s, n)
    else:
        o_spec, o_shape = pl.BlockSpec((planes, ts, nc), lambda i: (0, i, 0)), (planes, s, nc)
    return pl.pallas_call(
        body, name=name, grid=(s // ts,),
        in_specs=[pl.BlockSpec((ts, d), lambda i: (i, 0)), vec, vec, vec, w_spec,
                  pl.BlockSpec((1, n), lambda i: (0, 0))],
        out_specs=[o_spec, pl.BlockSpec((ts, d), lambda i: (i, 0))],
        out_shape=[jax.ShapeDtypeStruct(o_shape, out_dtype), jax.ShapeDtypeStruct((s, d), BF16)],
        compiler_params=_cparams(("arbitrary",)),
    )(x, ng, sc, sh, w, bias)


def _matmul(a, b, ta, tb, tm, tn, tk, out_dtype, name, out_parts=1):
    if a.ndim == 3:
        m, k = a.shape[1], a.shape[0] * a.shape[2]
        nkp = a.shape[2] // tk
    else:
        m, k = (a.shape[1], a.shape[0]) if ta else a.shape
    if b.ndim == 3:
        n = b.shape[1] if tb else b.shape[0] * b.shape[2]
        nbp = b.shape[2] // (tk if tb else tn)
    else:
        n = b.shape[0] if tb else b.shape[1]
    nk = k // tk
    nop = n // out_parts // tn
    dims = (((0,) if ta else (1,), (1,) if tb else (0,)), ((), ()))

    def body(a_ref, b_ref, o_ref, acc):
        kk = pl.program_id(2)

        @pl.when(kk == 0)
        def _():
            acc[...] = jnp.zeros_like(acc)
        acc[...] += _dot(a_ref[...], b_ref[...], dims)

        @pl.when(kk == nk - 1)
        def _():
            o_ref[...] = acc[...].astype(out_dtype)

    if a.ndim == 3:
        a_spec = pl.BlockSpec((None, tm, tk), lambda i, j, kk: (kk // nkp, i, kk % nkp))
    else:
        a_spec = (pl.BlockSpec((tk, tm), lambda i, j, kk: (kk, i)) if ta
                  else pl.BlockSpec((tm, tk), lambda i, j, kk: (i, kk)))
    if b.ndim == 3 and tb:
        b_spec = pl.BlockSpec((None, tn, tk), lambda i, j, kk: (kk // nbp, j, kk % nbp))
    elif b.ndim == 3:
        b_spec = pl.BlockSpec((None, tk, tn), lambda i, j, kk: (j // nbp, kk, j % nbp))
    else:
        b_spec = (pl.BlockSpec((tn, tk), lambda i, j, kk: (j, kk)) if tb
                  else pl.BlockSpec((tk, tn), lambda i, j, kk: (kk, j)))
    if out_parts > 1:
        o_spec = pl.BlockSpec((None, tm, tn), lambda i, j, kk: (j // nop, i, j % nop))
        o_shape = (out_parts, m, n // out_parts)
    else:
        o_spec, o_shape = pl.BlockSpec((tm, tn), lambda i, j, kk: (i, j)), (m, n)
    return pl.pallas_call(
        body, name=name, grid=(m // tm, n // tn, nk),
        in_specs=[a_spec, b_spec],
        out_specs=o_spec,
        out_shape=jax.ShapeDtypeStruct(o_shape, out_dtype),
        scratch_shapes=[pltpu.VMEM((tm, tn), F32)],
        compiler_params=_cparams(("arbitrary", "arbitrary", "arbitrary")),
    )(a, b)


def _matmul_residual(a, w, xin, g, ts, name):
    s, k = a.shape
    d = w.shape[1]

    def body(a_ref, w_ref, x_ref, g_ref, o_ref, y_ref):
        y = _dot(a_ref[...], w_ref[...])
        o_ref[...] = x_ref[...] + g_ref[...] * y
        y_ref[...] = y.astype(BF16)

    return pl.pallas_call(
        body, name=name, grid=(s // ts,),
        in_specs=[pl.BlockSpec((ts, k), lambda i: (i, 0)),
                  pl.BlockSpec((k, d), lambda i: (0, 0)),
                  pl.BlockSpec((ts, d), lambda i: (i, 0)),
                  pl.BlockSpec((1, d), lambda i: (0, 0))],
        out_specs=[pl.BlockSpec((ts, d), lambda i: (i, 0)), pl.BlockSpec((ts, d), lambda i: (i, 0))],
        out_shape=[jax.ShapeDtypeStruct((s, d), F32), jax.ShapeDtypeStruct((s, d), BF16)],
        compiler_params=_cparams(("arbitrary",)),
    )(a, w, xin, g)


def _lane(shape):
    return lax.broadcasted_iota(jnp.int32, shape, 1)


def _pair_norm(x, gain2, bones):
    msq = _exact_dot_r(x * x, bones) * (1.0 / DH)
    r = lax.rsqrt(msq + EPS)
    xh = x * r
    return xh * gain2, xh, r


def _fox_post(proj, qg2, kg2, bf, ts, name):
    s = proj.shape[0]

    def body(p_ref, qg_ref, kg_ref, bf_ref, q_ref, k_ref, v_ref, carry):
        @pl.when(pl.program_id(0) == 0)
        def _():
            carry[...] = jnp.zeros_like(carry)
        lane = _lane((ts, LANES))
        bones = _head_block_ones()
        xf = p_ref[:, 4 * D:4 * D + LANES] + bf_ref[...]
        logf = jnp.minimum(xf, 0.0) - jnp.log(1.0 + jnp.exp(-jnp.abs(xf)))
        logf = jnp.where(lane < H, logf, 0.0)
        rr = lax.broadcasted_iota(jnp.int32, (ts, ts), 0)
        cc = lax.broadcasted_iota(jnp.int32, (ts, ts), 1)
        ltri = (cc <= rr).astype(BF16)
        fcum = _exact_dot(ltri, logf) + carry[0:1, :]
        carry[0:1, :] = fcum[ts - 1:ts, :]
        fhi, fmid, flo = _split3(fcum * LOG2E)
        fhi, fmid, flo = fhi.astype(F32), fmid.astype(F32), flo.astype(F32)
        one_q = ((lane >= L_NF) & (lane < L_NF + 3)).astype(F32)
        one_k = (((lane >= L_F) & (lane < L_F + 3)) | ((lane >= L_LSE) & (lane < L_LSE + 3))).astype(F32)
        one_v = ((lane >= L_F) & (lane < L_F + 3)).astype(F32)
        for p in range(NP):
            qn, _, _ = _pair_norm(p_ref[:, p * LANES:(p + 1) * LANES], qg_ref[...], bones)
            kn, _, _ = _pair_norm(p_ref[:, D + p * LANES:D + (p + 1) * LANES], kg_ref[...], bones)
            vv = p_ref[:, 2 * D + p * LANES:2 * D + (p + 1) * LANES]
            qn = qn * (SCALE * LOG2E)
            for e in range(2):
                h = 2 * p + e
                if e == 1:
                    qe, ke, ve = (pltpu.roll(t, DH, axis=1) for t in (qn, kn, vv))
                else:
                    qe, ke, ve = qn, kn, vv
                f0, f1, f2 = fhi[:, h:h + 1], fmid[:, h:h + 1], flo[:, h:h + 1]
                fq = jnp.where(lane == L_F, f0, jnp.where(lane == L_F + 1, f1, jnp.where(lane == L_F + 2, f2, one_q)))
                fk = jnp.where(lane == L_NF, -f0, jnp.where(lane == L_NF + 1, -f1, jnp.where(lane == L_NF + 2, -f2, one_k)))
                q_ref[h] = jnp.where(lane < DH, qe, fq).astype(BF16)
                k_ref[h] = jnp.where(lane < DH, ke, fk).astype(BF16)
                v_ref[h] = jnp.where(lane < DH, ve, one_v).astype(BF16)

    hs = pl.BlockSpec((H, ts, LANES), lambda i: (0, i, 0))
    vec = pl.BlockSpec((1, LANES), lambda i: (0, 0))
    shp = jax.ShapeDtypeStruct((H, s, LANES), BF16)
    return pl.pallas_call(
        body, name=name, grid=(s // ts,),
        in_specs=[pl.BlockSpec((ts, FOX_NP), lambda i: (i, 0)), vec, vec, vec],
        out_specs=[hs, hs, hs], out_shape=[shp, shp, shp],
        scratch_shapes=[pltpu.VMEM((8, LANES), F32)],
        compiler_params=_cparams(("arbitrary",)),
    )(proj, qg2, kg2, bf)


def _gather_copies(p_refs, o_refs, send_sems, recv_sems):
    x, y, c = _mesh_pos()
    me = 2 * x + y
    sends, arrivals = [], []
    for a, (p_ref, o_ref) in enumerate(zip(p_refs, o_refs)):
        rh = p_ref.shape[0] // 2
        for k, chip in enumerate(_other_chips(x, y)):
            ci = 2 * chip[0] + chip[1]
            for cc in range(2):
                sends.append(_remote(p_ref.at[pl.ds(c * rh, rh), :], o_ref.at[me, pl.ds(c * rh, rh), :],
                                     send_sems.at[6 * a + 2 * k + cc], recv_sems.at[6 * a + 2 * k + c], (*chip, cc)))
                arrivals.append(_remote(o_ref.at[ci, pl.ds(cc * rh, rh), :], o_ref.at[ci, pl.ds(cc * rh, rh), :],
                                        send_sems.at[6 * a + 2 * k + cc], recv_sems.at[6 * a + 2 * k + cc],
                                        (*chip, cc)))
    return sends, arrivals


def _attn_fwd(qa, ka, va, tq, name, shards=()):
    s = qa.shape[1]
    nq = s // tq
    na = len(shards)

    def body(*refs):
        q_ref, k_ref, v_ref = refs[:3]
        p_refs = refs[3:3 + na]
        o_ref, ql_ref = refs[3 + na:5 + na]
        g_refs = refs[5 + na:5 + 2 * na]
        i = pl.program_id(1)
        if na:
            send_sems, recv_sems = refs[5 + 2 * na:]

            @pl.when((pl.program_id(0) == 0) & (i == 0))
            def _():
                for cp in _gather_copies(p_refs, g_refs, send_sems, recv_sems)[0]:
                    cp.start()
        lane = _lane((tq, LANES))
        qs_ = [q_ref[0], q_ref[1]]

        def step(j, carry, masked):
            off = pl.multiple_of(j * tq, tq)
            new = []
            for e in range(2):
                m, acc = carry[e]
                kb = k_ref[e, pl.ds(off, tq), :]
                vb = v_ref[e, pl.ds(off, tq), :]
                sc = _dot_nt(qs_[e], kb)
                if masked:
                    rr = lax.broadcasted_iota(jnp.int32, (tq, tq), 0)
                    cc = lax.broadcasted_iota(jnp.int32, (tq, tq), 1)
                    sc = jnp.where(cc <= rr, sc, -jnp.inf)
                m_new = jnp.maximum(m, jnp.max(sc, axis=-1, keepdims=True))
                pr = jnp.exp2(sc - m_new)
                acc = acc * jnp.exp2(m - m_new) + _dot(pr.astype(BF16), vb)
                new.append((m_new, acc))
            return tuple(new)

        one = (jnp.full((tq, 1), -jnp.inf, F32), jnp.zeros((tq, LANES), F32))
        carry = lax.fori_loop(0, i, functools.partial(step, masked=False), (one, one))
        carry = step(i, carry, True)
        outs = []
        for e in range(2):
            m, acc = carry[e]
            l = acc[:, L_F:L_F + 1]
            outs.append(acc / l)
            lse = m + jnp.log2(l)
            h0, h1, h2 = _split3(-lse)
            ql = jnp.where(lane == L_LSE, h0.astype(F32),
                           jnp.where(lane == L_LSE + 1, h1.astype(F32),
                                     jnp.where(lane == L_LSE + 2, h2.astype(F32), qs_[e].astype(F32))))
            ql_ref[e] = ql.astype(BF16)
        o_ref[...] = jnp.where(lane < DH, outs[0], pltpu.roll(outs[1], DH, axis=1))
        if na:
            @pl.when((pl.program_id(0) == NP - 1) & (i == nq - 1))
            def _():
                sends, arrivals = _gather_copies(p_refs, g_refs, send_sems, recv_sems)
                for cp in arrivals:
                    cp.wait_recv()
                for cp in sends:
                    cp.wait_send()

    res = pl.BlockSpec((2, s, LANES), lambda p, i: (p, 0, 0))
    qs = pl.BlockSpec((2, tq, LANES), lambda p, i: (p, i, 0))
    outs = pl.pallas_call(
        body, name=name, grid=(NP, nq),
        in_specs=[qs, res, res] + [HBM_SPEC] * na,
        out_specs=[pl.BlockSpec((tq, LANES), lambda p, i: (i, p)), qs] + [HBM_SPEC] * na,
        out_shape=[jax.ShapeDtypeStruct((s, D), F32), jax.ShapeDtypeStruct((H, s, LANES), BF16)]
        + [jax.ShapeDtypeStruct((N_CHIP,) + p.shape, p.dtype) for p in shards],
        scratch_shapes=[pltpu.SemaphoreType.DMA((6 * na,))] * 2 if na else [],
        compiler_params=_cparams(("arbitrary", "arbitrary")),
    )(qa, ka, va, *shards)
    return outs[0], outs[1], list(outs[2:])


def _chip_exchange_copies(cs_refs, o_refs, send_sems, recv_sems):
    x, y, c = _mesh_pos()
    cps = []
    for a, (cs_ref, o_ref) in enumerate(zip(cs_refs, o_refs)):
        for k, chip in enumerate(_other_chips(x, y)):
            ci = 2 * chip[0] + chip[1]
            cps.append(_remote(cs_ref.at[ci], o_ref.at[k], send_sems.at[3 * a + k], recv_sems.at[3 * a + k],
                               (*chip, c)))
    return cps


def _attn_bwd(ql, ka, va, doa, tq, name, css=()):
    s = ql.shape[1]
    nq = s // tq
    na = len(css)

    def body(*refs):
        q_ref, k_ref, v_ref, do_ref = refs[:4]
        cs_refs = refs[4:4 + na]
        dqo_ref, dk_ref, dv_ref = refs[4 + na:7 + na]
        r_refs = refs[7 + na:7 + 2 * na]
        dq_ref = refs[7 + 2 * na]
        j = pl.program_id(1)
        if na:
            send_sems, recv_sems = refs[8 + 2 * na:]

            @pl.when((pl.program_id(0) == 0) & (j == 0))
            def _():
                for cp in _chip_exchange_copies(cs_refs, r_refs, send_sems, recv_sems):
                    cp.start()

        @pl.when(j == 0)
        def _():
            dq_ref[...] = jnp.zeros_like(dq_ref)
        lane = _lane((tq, LANES))
        kbs = [k_ref[0], k_ref[1]]
        vbs = [v_ref[0], v_ref[1]]

        def step(i, carry, masked):
            ioff = pl.multiple_of(i * tq, tq)
            new = []
            for e in range(2):
                dk, dv = carry[e]
                qb = q_ref[e, pl.ds(ioff, tq), :]
                dob = do_ref[e, pl.ds(ioff, tq), :]
                pr = jnp.exp2(_dot_nt(qb, kbs[e]))
                if masked:
                    rr = lax.broadcasted_iota(jnp.int32, (tq, tq), 0)
                    cc = lax.broadcasted_iota(jnp.int32, (tq, tq), 1)
                    pr = jnp.where(cc <= rr, pr, 0.0)
                ds = (pr * _dot_nt(dob, vbs[e])).astype(BF16)
                dv = dv + _dot_tn(pr.astype(BF16), dob)
                dk = dk + _dot_tn(ds, qb)
                dq_ref[e, pl.ds(ioff, tq), :] += _dot(ds, kbs[e])
                new.append((dk, dv))
            return tuple(new)

        zero = jnp.zeros((tq, LANES), F32)
        carry = step(j, ((zero, zero), (zero, zero)), True)
        carry = lax.fori_loop(j + 1, nq, functools.partial(step, masked=False), carry)
        for e in range(2):
            dk, dv = carry[e]
            col = dk[:, L_NF:L_NF + 1]
            hi = col.astype(BF16).astype(F32)
            dk_ref[e] = jnp.where(lane == L_NF, hi, jnp.where(lane == L_NF + 1, col - hi, dk)).astype(BF16)
            dv_ref[e] = dv.astype(BF16)

        @pl.when(j == nq - 1)
        def _():
            lane_s = _lane((s, LANES))
            for e in range(2):
                dq = dq_ref[e]
                col = dq[:, L_F:L_F + 1]
                hi = col.astype(BF16).astype(F32)
                dqo_ref[e] = jnp.where(lane_s == L_F, hi, jnp.where(lane_s == L_F + 1, col - hi, dq)).astype(BF16)
        if na:
            @pl.when((pl.program_id(0) == NP - 1) & (j == nq - 1))
            def _():
                for cp in _chip_exchange_copies(cs_refs, r_refs, send_sems, recv_sems):
                    cp.wait()

    res = pl.BlockSpec((2, s, LANES), lambda p, j: (p, 0, 0))
    tile = pl.BlockSpec((2, tq, LANES), lambda p, j: (p, j, 0))
    shp = jax.ShapeDtypeStruct((H, s, LANES), BF16)
    outs = pl.pallas_call(
        body, name=name, grid=(NP, nq),
        in_specs=[res, tile, tile, res] + [HBM_SPEC] * na, out_specs=[res, tile, tile] + [HBM_SPEC] * na,
        out_shape=[shp, shp, shp] + [jax.ShapeDtypeStruct((3,) + cs.shape[1:], cs.dtype) for cs in css],
        scratch_shapes=[pltpu.VMEM((2, s, LANES), F32)] + ([pltpu.SemaphoreType.DMA((3 * na,))] * 2 if na else []),
        compiler_params=_cparams(("arbitrary", "arbitrary")),
    )(ql, ka, va, doa, *css)
    return outs[0], outs[1], outs[2], list(outs[3:])


def _gate(att, proj, ts, name):
    s = att.shape[0]

    def body(a_ref, o_ref, g_ref):
        g_ref[...] = (a_ref[...] * _sigmoid(o_ref[...])).astype(BF16)

    return pl.pallas_call(
        body, name=name, grid=(s // ts,),
        in_specs=[pl.BlockSpec((ts, D), lambda i: (i, 0)), pl.BlockSpec((ts, D), lambda i: (i, 3))],
        out_specs=pl.BlockSpec((ts, D), lambda i: (i, 0)),
        out_shape=jax.ShapeDtypeStruct((s, D), BF16),
        compiler_params=_cparams(("arbitrary",)),
    )(att, proj)


def _attn_bwd_prep(dgated, att, proj, ts, name):
    s = att.shape[0]

    def body(dg_ref, a_ref, o_ref, doa_ref, dop_ref):
        lane = _lane((ts, LANES))
        bones = _head_block_ones()
        for p in range(NP):
            sl = slice(p * LANES, (p + 1) * LANES)
            dg, a = dg_ref[:, sl], a_ref[:, sl]
            sig = _sigmoid(o_ref[:, sl])
            datt = dg * sig
            dop_ref[:, sl] = (dg * a * sig * (1.0 - sig)).astype(BF16)
            delta = _exact_dot_r(datt * a, bones)
            for e in range(2):
                de, dl = (datt, delta) if e == 0 else (pltpu.roll(datt, DH, axis=1), pltpu.roll(delta, DH, axis=1))
                h0, h1, h2 = _split3(-dl[:, 0:1])
                aug = jnp.where(lane == L_F, h0.astype(F32),
                                jnp.where(lane == L_F + 1, h1.astype(F32),
                                          jnp.where(lane == L_F + 2, h2.astype(F32), 0.0)))
                doa_ref[2 * p + e] = jnp.where(lane < DH, de, aug).astype(BF16)

    row = pl.BlockSpec((ts, D), lambda i: (i, 0))
    return pl.pallas_call(
        body, name=name, grid=(s // ts,),
        in_specs=[row, row, pl.BlockSpec((ts, D), lambda i: (i, 3))],
        out_specs=[pl.BlockSpec((H, ts, LANES), lambda i: (0, i, 0)), row],
        out_shape=[jax.ShapeDtypeStruct((H, s, LANES), BF16), jax.ShapeDtypeStruct((s, D), BF16)],
        compiler_params=_cparams(("arbitrary",)),
    )(dgated, att, proj)


def _fox_post_bwd(proj, dqa, dka, dva, dop, qg2, kg2, bf, ts, name):
    s = proj.shape[0]
    nt = s // ts

    def body(p_ref, dq_ref, dk_ref, dv_ref, dop_ref, qg_ref, kg_ref, bf_ref, o_ref, red_ref, carry):
        @pl.when(pl.program_id(0) == 0)
        def _():
            carry[...] = jnp.zeros_like(carry)
            red_ref[...] = jnp.zeros_like(red_ref)
        lane = _lane((ts, LANES))
        bones = _head_block_ones()
        d_f = jnp.zeros((ts, LANES), F32)
        dqg = jnp.zeros((1, LANES), F32)
        dkg = jnp.zeros((1, LANES), F32)
        for p in range(NP):
            heads = [[ref[2 * p + e].astype(F32) for e in range(2)] for ref in (dq_ref, dk_ref, dv_ref)]
            pair = [jnp.where(lane < DH, a, pltpu.roll(b, DH, axis=1)) for a, b in heads]
            for e in range(2):
                dqe, dke = heads[0][e], heads[1][e]
                col = (dqe[:, L_F:L_F + 1] + dqe[:, L_F + 1:L_F + 2]
                       - dke[:, L_NF:L_NF + 1] - dke[:, L_NF + 1:L_NF + 2])
                d_f = jnp.where(lane == 2 * p + e, col, d_f)
            for idx, (g_ref, base) in enumerate(((qg_ref, 0), (kg_ref, D))):
                x = p_ref[:, base + p * LANES:base + (p + 1) * LANES]
                _, xh, r = _pair_norm(x, g_ref[...], bones)
                dn = pair[idx] * (SCALE if idx == 0 else 1.0 / LOG2E)
                t = dn * g_ref[...]
                mean_txh = _exact_dot_r(t * xh, bones) * (1.0 / DH)
                dx = r * (t - xh * mean_txh)
                o_ref[:, base + p * LANES:base + (p + 1) * LANES] = dx.astype(BF16)
                gsum = jnp.sum(dn * xh, axis=0, keepdims=True)
                if idx == 0:
                    dqg = dqg + gsum
                else:
                    dkg = dkg + gsum
            o_ref[:, 2 * D + p * LANES:2 * D + (p + 1) * LANES] = pair[2].astype(BF16)
        o_ref[:, 3 * D:4 * D] = dop_ref[...]
        rr = lax.broadcasted_iota(jnp.int32, (ts, ts), 0)
        cc = lax.broadcasted_iota(jnp.int32, (ts, ts), 1)
        utri = (cc >= rr).astype(BF16)
        dlogf = _exact_dot(utri, d_f) + carry[0:1, :]
        carry[0:1, :] = dlogf[0:1, :]
        xf = p_ref[:, 4 * D:4 * D + LANES] + bf_ref[...]
        dfl = jnp.where(lane < H, dlogf * _sigmoid(-xf), 0.0)
        o_ref[:, 4 * D:4 * D + LANES] = dfl.astype(BF16)
        red_ref[0:1, :] += dqg
        red_ref[1:2, :] += dkg
        red_ref[2:3, :] += jnp.sum(dfl, axis=0, keepdims=True)

    hs = pl.BlockSpec((H, ts, LANES), lambda i: (0, nt - 1 - i, 0))
    vec = pl.BlockSpec((1, LANES), lambda i: (0, 0))
    return pl.pallas_call(
        body, name=name, grid=(nt,),
        in_specs=[pl.BlockSpec((ts, FOX_NP), lambda i: (nt - 1 - i, 0)), hs, hs, hs,
                  pl.BlockSpec((ts, D), lambda i: (nt - 1 - i, 0)), vec, vec, vec],
        out_specs=[pl.BlockSpec((ts, FOX_NP), lambda i: (nt - 1 - i, 0)),
                   pl.BlockSpec((8, LANES), lambda i: (0, 0))],
        out_shape=[jax.ShapeDtypeStruct((s, FOX_NP), BF16), jax.ShapeDtypeStruct((8, LANES), F32)],
        scratch_shapes=[pltpu.VMEM((8, LANES), F32)],
        compiler_params=_cparams(("arbitrary",)),
    )(proj, dqa, dka, dva, dop, qg2, kg2, bf)


HALO = 16
TS = 512
TQ = 512
TR = 256
TP = 256


def _shift_down(x, k):
    return pltpu.roll(x, k, axis=0)


def _shift_up(x, k):
    return pltpu.roll(x, x.shape[0] - k, axis=0)


def _planes(ref):
    return jnp.concatenate([ref[0].astype(F32), ref[1].astype(F32)], axis=1)


def _conv_gate(a, cw, cb, ts, name):
    s = a.shape[1]
    hb = ts // HALO

    def body(prev_ref, a_ref, cw_ref, cb_ref, f_ref):
        i = pl.program_id(0)
        cwv, cbv = _planes(cw_ref), _planes(cb_ref)
        prev = jnp.where(i > 0, _planes(prev_ref), 0.0)
        ext = jnp.concatenate([prev, _planes(a_ref)], axis=0)
        ap = (_shift_down(ext, 2) * cwv[0:1, :] + _shift_down(ext, 1) * cwv[1:2, :]
              + ext * cwv[2:3, :] + cbv)[HALO:, :]
        g, val = ap[:, :GT], ap[:, GT:]
        f_ref[...] = (g * _sigmoid(g) * val).astype(BF16)

    return pl.pallas_call(
        body, name=name, grid=(s // ts, NGT),
        in_specs=[pl.BlockSpec((2, HALO, GT), lambda i, j: (0, jnp.maximum(i * hb - 1, 0), j)),
                  pl.BlockSpec((2, ts, GT), lambda i, j: (0, i, j)),
                  pl.BlockSpec((2, 8, GT), lambda i, j: (0, 0, j)),
                  pl.BlockSpec((2, 1, GT), lambda i, j: (0, 0, j))],
        out_specs=pl.BlockSpec((ts, GT), lambda i, j: (i, j)),
        out_shape=jax.ShapeDtypeStruct((s, DFF), BF16),
        compiler_params=_cparams(("arbitrary", "arbitrary")),
    )(a, a, cw, cb)


def _conv_gate_bwd(a, df, cw, cb, ts, name):
    s = a.shape[1]
    hb = ts // HALO
    nt = s // ts

    def body(prev_ref, a_ref, next_ref, df_ref, dfn_ref, cw_ref, cb_ref, da_ref, red_ref):
        i = pl.program_id(1)

        @pl.when(i == 0)
        def _():
            red_ref[...] = jnp.zeros_like(red_ref)
        cwv, cbv = _planes(cw_ref), _planes(cb_ref)
        prev = jnp.where(i > 0, _planes(prev_ref), 0.0)
        ext = jnp.concatenate([prev, _planes(a_ref), _planes(next_ref)], axis=0)
        dfn = jnp.where(i < nt - 1, dfn_ref[...].astype(F32), 0.0)
        dfe = jnp.concatenate([jnp.zeros((HALO, GT), F32), df_ref[...].astype(F32), dfn], axis=0)
        am2, am1 = _shift_down(ext, 2), _shift_down(ext, 1)
        ap = am2 * cwv[0:1, :] + am1 * cwv[1:2, :] + ext * cwv[2:3, :] + cbv
        g, val = ap[:, :GT], ap[:, GT:]
        sg = _sigmoid(g)
        dap = jnp.concatenate([dfe * val * (sg * (1.0 + g * (1.0 - sg))), dfe * (g * sg)], axis=1)
        da = dap * cwv[2:3, :] + _shift_up(dap, 1) * cwv[1:2, :] + _shift_up(dap, 2) * cwv[0:1, :]
        main = slice(HALO, HALO + ts)
        sums = [jnp.sum((t * dap)[main], axis=0, keepdims=True) for t in (am2, am1, ext)]
        sums.append(jnp.sum(dap[main], axis=0, keepdims=True))
        for e in range(2):
            cols = slice(e * GT, (e + 1) * GT)
            da_ref[e] = da[main, cols].astype(BF16)
            for r, sm in enumerate(sums):
                red_ref[e, r:r + 1, :] += sm[:, cols]

    nhb = s // HALO
    return pl.pallas_call(
        body, name=name, grid=(NGT, nt),
        in_specs=[pl.BlockSpec((2, HALO, GT), lambda j, i: (0, jnp.maximum(i * hb - 1, 0), j)),
                  pl.BlockSpec((2, ts, GT), lambda j, i: (0, i, j)),
                  pl.BlockSpec((2, HALO, GT), lambda j, i: (0, jnp.minimum((i + 1) * hb, nhb - 1), j)),
                  pl.BlockSpec((ts, GT), lambda j, i: (i, j)),
                  pl.BlockSpec((HALO, GT), lambda j, i: (jnp.minimum((i + 1) * hb, nhb - 1), j)),
                  pl.BlockSpec((2, 8, GT), lambda j, i: (0, 0, j)),
                  pl.BlockSpec((2, 1, GT), lambda j, i: (0, 0, j))],
        out_specs=[pl.BlockSpec((2, ts, GT), lambda j, i: (0, i, j)),
                   pl.BlockSpec((2, 8, GT), lambda j, i: (0, 0, j))],
        out_shape=[jax.ShapeDtypeStruct((2, s, DFF), BF16), jax.ShapeDtypeStruct((2, 8, DFF), F32)],
        compiler_params=_cparams(("arbitrary", "arbitrary")),
    )(a, a, a, df, df, cw, cb)


def _chunk_mask(transposed=False):
    t = lax.broadcasted_iota(jnp.int32, (SGB, SGB), 0) // CHUNK
    u = lax.broadcasted_iota(jnp.int32, (SGB, SGB), 1) // CHUNK
    return (t <= u) if transposed else (u <= t)


def _sgu_ln(zv, gain, bias):
    v = _gelu(zv)
    mu = jnp.mean(v, axis=-1, keepdims=True)
    vc = v - mu
    rstd = lax.rsqrt(jnp.mean(vc * vc, axis=-1, keepdims=True) + EPS)
    vhat = vc * rstd
    return vhat * gain + bias, vhat, rstd


def _sgu_fwd(z, vgain, vbias, ws, bst, tr, name):
    s = z.shape[0]

    def body(zu_ref, zv_ref, vg_ref, vb_ref, ws_ref, bs_ref, y_ref):
        u = _gelu(zu_ref[...].astype(F32))
        vn, _, _ = _sgu_ln(zv_ref[...].astype(F32), vg_ref[...], vb_ref[...])
        vn = vn.astype(BF16)
        mask = _chunk_mask()
        for g in range(SGG):
            w = jnp.where(mask, ws_ref[g], 0.0).astype(BF16)
            for b in range(tr // SGB):
                rs, cs = slice(b * SGB, (b + 1) * SGB), slice(g * SGC, (g + 1) * SGC)
                mixed = _dot(w, vn[rs, cs]) + bs_ref[:, g:g + 1]
                y_ref[rs, cs] = (u[rs, cs] * mixed).astype(BF16)

    vec = pl.BlockSpec((1, SGW), lambda i: (0, 0))
    return pl.pallas_call(
        body, name=name, grid=(s // tr,),
        in_specs=[pl.BlockSpec((tr, SGW), lambda i: (i, 0)), pl.BlockSpec((tr, SGW), lambda i: (i, 1)),
                  vec, vec, pl.BlockSpec((SGG, SGB, SGB), lambda i: (0, 0, 0)),
                  pl.BlockSpec((SGB, LANES), lambda i: (0, 0))],
        out_specs=pl.BlockSpec((tr, SGW), lambda i: (i, 0)),
        out_shape=jax.ShapeDtypeStruct((s, SGW), BF16),
        compiler_params=_cparams(("arbitrary",)),
    )(z, z, vgain, vbias, ws, bst)


def _sgu_bwd(z, dy, vgain, vbias, ws, wst, bst, tr, name):
    s = z.shape[0]

    def body(zu_ref, zv_ref, dy_ref, vg_ref, vb_ref, ws_ref, wst_ref, bs_ref,
             dz_ref, rb_ref, rv_ref, dws_ref, dbs_ref, dvn_s):
        @pl.when(pl.program_id(0) == 0)
        def _():
            rb_ref[...] = jnp.zeros_like(rb_ref)
            rv_ref[...] = jnp.zeros_like(rv_ref)
            dws_ref[...] = jnp.zeros_like(dws_ref)
            dbs_ref[...] = jnp.zeros_like(dbs_ref)
        zu = zu_ref[...].astype(F32)
        zv = zv_ref[...].astype(F32)
        u = _gelu(zu)
        vn, vhat, rstd = _sgu_ln(zv, vg_ref[...], vb_ref[...])
        vnb = vn.astype(BF16)
        dyv = dy_ref[...].astype(F32)
        dmix = (dyv * u).astype(BF16)
        mask = _chunk_mask()
        mask_t = _chunk_mask(transposed=True)
        lane = _lane((SGB, LANES))
        dbs = jnp.zeros((SGB, LANES), F32)
        for g in range(SGG):
            w = jnp.where(mask, ws_ref[g], 0.0).astype(BF16)
            wt = jnp.where(mask_t, wst_ref[g], 0.0).astype(BF16)
            dw = jnp.zeros((SGB, SGB), F32)
            for b in range(tr // SGB):
                rs, cs = slice(b * SGB, (b + 1) * SGB), slice(g * SGC, (g + 1) * SGC)
                mixed = _dot(w, vnb[rs, cs]) + bs_ref[:, g:g + 1]
                dz_ref[rs, cs] = (dyv[rs, cs] * mixed * _gelu_grad(zu[rs, cs])).astype(BF16)
                dm = dmix[rs, cs]
                dw = dw + _dot_nt(dm, vnb[rs, cs])
                dbs = dbs + jnp.where(lane == g, jnp.sum(dm.astype(F32), axis=-1, keepdims=True), 0.0)
                dvn_s[rs, cs] = _dot(wt, dm)
            dws_ref[g] += jnp.where(mask, dw, 0.0)
        dbs_ref[...] += dbs
        dvn = dvn_s[...]
        rv_ref[0:1, :] += jnp.sum(dvn * vhat, axis=0, keepdims=True)
        rv_ref[1:2, :] += jnp.sum(dvn, axis=0, keepdims=True)
        dvh = dvn * vg_ref[...]
        dv = rstd * (dvh - jnp.mean(dvh, axis=-1, keepdims=True)
                     - vhat * jnp.mean(dvh * vhat, axis=-1, keepdims=True))
        dz_ref[:, SGW:] = (dv * _gelu_grad(zv)).astype(BF16)
        dzf = dz_ref[...].astype(F32)
        rb_ref[0:1, :] += jnp.sum(dzf, axis=0, keepdims=True)

    vec = pl.BlockSpec((1, SGW), lambda i: (0, 0))
    wsp = pl.BlockSpec((SGG, SGB, SGB), lambda i: (0, 0, 0))
    return pl.pallas_call(
        body, name=name, grid=(s // tr,),
        in_specs=[pl.BlockSpec((tr, SGW), lambda i: (i, 0)), pl.BlockSpec((tr, SGW), lambda i: (i, 1)),
                  pl.BlockSpec((tr, SGW), lambda i: (i, 0)), vec, vec, wsp, wsp,
                  pl.BlockSpec((SGB, LANES), lambda i: (0, 0))],
        out_specs=[pl.BlockSpec((tr, 2 * SGW), lambda i: (i, 0)),
                   pl.BlockSpec((8, 2 * SGW), lambda i: (0, 0)),
                   pl.BlockSpec((8, SGW), lambda i: (0, 0)), wsp,
                   pl.BlockSpec((SGB, LANES), lambda i: (0, 0))],
        out_shape=[jax.ShapeDtypeStruct((s, 2 * SGW), BF16), jax.ShapeDtypeStruct((8, 2 * SGW), F32),
                   jax.ShapeDtypeStruct((8, SGW), F32), jax.ShapeDtypeStruct((SGG, SGB, SGB), F32),
                   jax.ShapeDtypeStruct((SGB, LANES), F32)],
        scratch_shapes=[pltpu.VMEM((tr, SGW), F32)],
        compiler_params=_cparams(("arbitrary",)),
    )(z, z, dy, vgain, vbias, ws, wst, bst)


def _final_loss(x, fg, tgt, gprev, yprev, ts, name):
    s, d = x.shape

    def body(x_ref, fg_ref, t_ref, g_ref, y_ref, l_ref, dx_ref, dy_ref, red_ref):
        @pl.when(pl.program_id(0) == 0)
        def _():
            l_ref[...] = jnp.zeros_like(l_ref)
            red_ref[...] = jnp.zeros_like(red_ref)
        xv = x_ref[...]
        r = _rstd_rows(xv)
        xh = xv * r
        err = xh * fg_ref[...] - t_ref[...]
        l_ref[...] += 0.5 * jnp.sum(jnp.mean(err * err, axis=-1, keepdims=True))
        dyo = err * (1.0 / d)
        dxh = dyo * fg_ref[...]
        dx = r * (dxh - xh * jnp.mean(dxh * xh, axis=-1, keepdims=True))
        dx_ref[...] = dx
        dy_ref[...] = (dx * g_ref[...]).astype(BF16)
        red_ref[0:1, :] += jnp.sum(dyo * xh, axis=0, keepdims=True)
        red_ref[1:2, :] += jnp.sum(dx * y_ref[...].astype(F32), axis=0, keepdims=True)

    row = pl.BlockSpec((ts, d), lambda i: (i, 0))
    vec = pl.BlockSpec((1, d), lambda i: (0, 0))
    return pl.pallas_call(
        body, name=name, grid=(s // ts,),
        in_specs=[row, vec, row, vec, row],
        out_specs=[pl.BlockSpec((8, LANES), lambda i: (0, 0)), row, row, pl.BlockSpec((8, d), lambda i: (0, 0))],
        out_shape=[jax.ShapeDtypeStruct((8, LANES), F32), jax.ShapeDtypeStruct((s, d), F32),
                   jax.ShapeDtypeStruct((s, d), BF16), jax.ShapeDtypeStruct((8, d), F32)],
        compiler_params=_cparams(("arbitrary",)),
    )(x, fg, tgt, gprev, yprev)


def _norm_bwd(xin, dh, dxout, ng, sc, gprev, yprev, ts, name):
    s, d = xin.shape
    has_prev = gprev is not None

    def body(*refs):
        if has_prev:
            x_ref, dh_ref, dxo_ref, ng_ref, sc_ref, g_ref, y_ref, dx_ref, dy_ref, red_ref = refs
        else:
            x_ref, dh_ref, dxo_ref, ng_ref, sc_ref, dx_ref, red_ref = refs

        @pl.when(pl.program_id(0) == 0)
        def _():
            red_ref[...] = jnp.zeros_like(red_ref)
        xv = x_ref[...]
        r = _rstd_rows(xv)
        xh = xv * r
        dhv = dh_ref[...]
        dr = dhv * (1.0 + sc_ref[...])
        t = dr * ng_ref[...]
        dx = dxo_ref[...] + r * (t - xh * jnp.mean(t * xh, axis=-1, keepdims=True))
        dx_ref[...] = dx
        red_ref[0:1, :] += jnp.sum(dhv, axis=0, keepdims=True)
        red_ref[1:2, :] += jnp.sum(dhv * (xh * ng_ref[...]), axis=0, keepdims=True)
        red_ref[2:3, :] += jnp.sum(dr * xh, axis=0, keepdims=True)
        if has_prev:
            dy_ref[...] = (dx * g_ref[...]).astype(BF16)
            red_ref[3:4, :] += jnp.sum(dx * y_ref[...].astype(F32), axis=0, keepdims=True)

    row = pl.BlockSpec((ts, d), lambda i: (i, 0))
    vec = pl.BlockSpec((1, d), lambda i: (0, 0))
    red = pl.BlockSpec((8, d), lambda i: (0, 0))
    if has_prev:
        in_specs, args = [row, row, row, vec, vec, vec, row], (xin, dh, dxout, ng, sc, gprev, yprev)
        out_specs = [row, row, red]
        out_shape = [jax.ShapeDtypeStruct((s, d), F32), jax.ShapeDtypeStruct((s, d), BF16),
                     jax.ShapeDtypeStruct((8, d), F32)]
    else:
        in_specs, args = [row, row, row, vec, vec], (xin, dh, dxout, ng, sc)
        out_specs = [row, red]
        out_shape = [jax.ShapeDtypeStruct((s, d), F32), jax.ShapeDtypeStruct((8, d), F32)]
    return pl.pallas_call(
        body, name=name, grid=(s // ts,), in_specs=in_specs, out_specs=out_specs, out_shape=out_shape,
        compiler_params=_cparams(("arbitrary",)),
    )(*args)


def _ada_mod(c_all, ada_w, ada_b):
    nb = c_all.shape[0]
    da = ada_w.shape[2]

    def body(c_ref, w_ref, b_ref, o_ref, ca_ref):
        cv = c_ref[...]
        ca = cv * _sigmoid(cv)
        ca_ref[...] = ca
        o_ref[0] = lax.dot_general(ca, w_ref[0], (((1,), (0,)), ((), ())), precision=lax.Precision.HIGHEST,
                                   preferred_element_type=F32) + b_ref[0]

    return pl.pallas_call(
        body, name="ada_mod", grid=(2,),
        in_specs=[pl.BlockSpec((nb, D), lambda i: (0, 0)), pl.BlockSpec((1, D, da), lambda i: (i, 0, 0)),
                  pl.BlockSpec((1, 1, da), lambda i: (i, 0, 0))],
        out_specs=[pl.BlockSpec((1, nb, da), lambda i: (i, 0, 0)), pl.BlockSpec((nb, D), lambda i: (0, 0))],
        out_shape=[jax.ShapeDtypeStruct((2, nb, da), F32), jax.ShapeDtypeStruct((nb, D), F32)],
        compiler_params=_cparams(("arbitrary",)),
    )(c_all, ada_w, ada_b)


def _ada_w_grad(c_act_t, dmod):
    nb = c_act_t.shape[1]
    da = dmod.shape[2]
    tn = 512

    def body(c_ref, d_ref, o_ref):
        acc = c_ref[:, 0:1] * d_ref[0, 0:1, :]
        for b in range(1, nb):
            acc = acc + c_ref[:, b:b + 1] * d_ref[0, b:b + 1, :]
        o_ref[0] = acc

    return pl.pallas_call(
        body, name="ada_w_grad", grid=(2, da // tn),
        in_specs=[pl.BlockSpec((D, nb), lambda i, j: (0, 0)), pl.BlockSpec((1, nb, tn), lambda i, j: (i, 0, j))],
        out_specs=pl.BlockSpec((1, D, tn), lambda i, j: (i, 0, j)),
        out_shape=jax.ShapeDtypeStruct((2, D, da), F32),
        compiler_params=_cparams(("arbitrary", "arbitrary")),
    )(c_act_t, dmod)


def _wt(w):
    return w.T if w.ndim == 2 else jnp.swapaxes(w, 1, 2).reshape(-1, w.shape[1])


def _conv_planes(cw, cb):
    cwp = jnp.swapaxes(cw.reshape(3, 2, DFF), 0, 1)
    return jnp.pad(cwp, ((0, 0), (0, 5), (0, 0))), cb.reshape(2, 1, DFF)


def _local_step(x, tgt, mod, wts, small, comm=None):
    wts = dict(wts)
    s = x.shape[0]
    ts, tq, tr, tp = TS, TQ, TR, TP
    zb = lambda n: jnp.zeros((1, n), F32)
    m6 = mod.reshape(2, 6, 1, D)
    sh1, sc1, g1, sh2, sc2, g2 = ([m6[i, k] for i in range(2)] for k in range(6))
    n1g, n2g = small["norm1_g"], small["norm2_g"]
    row = lambda a, i: a[i:i + 1]

    qg2 = jnp.tile(small["fox_q_gain"], (1, 2))
    kg2 = jnp.tile(small["fox_k_gain"], (1, 2))
    bfp = jnp.pad(small["fox_b_f"], ((0, 0), (0, LANES - H)))
    proj, h1 = _norm_mod_matmul(x, row(n1g, 0), sc1[0], sh1[0], wts["fox_w_in"], zb(FOX_NP), F32, ts, 1408, "fox_in")
    qa, ka, va = _fox_post(proj, qg2, kg2, bfp, tp, "fox_post")
    att, ql, gathered = _attn_fwd(qa, ka, va, tq, "attn_fwd", shards=comm["shards"] if comm else ())
    if comm:
        wts.update(comm["make_wts"](gathered))
    gated = _gate(att, proj, ts, "fox_gate")
    x1, y0 = _matmul_residual(gated, wts["fox_w_out"], x, g1[0], ts, "fox_out")

    def ffn_fwd(xin, i, tag):
        cw, cb = _conv_planes(small["ffn_conv_w"][i], small["ffn_conv_b"][i])
        a, h = _norm_mod_matmul(xin, row(n2g, i), sc2[i], sh2[i], wts["ffn_w_up"][i], zb(2 * DFF), BF16, ts, 1408,
                                "ffn_up" + tag, planes=2)
        f = _conv_gate(a, cw, cb, ts, "ffn_conv" + tag)
        xo, y = _matmul_residual(f, wts["ffn_w_down"][i], xin, g2[i], ts, "ffn_down" + tag)
        return xo, (a, h, f, y, cw, cb)

    x2, ffn0 = ffn_fwd(x1, 0, "0")

    bst = jnp.pad(small["sgu_b_s"].T, ((0, 0), (0, LANES - SGG)))
    ws = small["sgu_w_s"]
    z, h3 = _norm_mod_matmul(x2, row(n1g, 1), sc1[1], sh1[1], wts["sgu_w_in"], small["sgu_b_in"], BF16, ts, 1024,
                             "sgu_in")
    yy = _sgu_fwd(z, small["sgu_v_gain"], small["sgu_v_bias"], ws, bst, tr, "sgu_mix")
    x3, y1 = _matmul_residual(yy, wts["sgu_w_out"], x2, g1[1], ts, "sgu_out")
    x4, ffn1 = ffn_fwd(x3, 1, "1")

    lsum, dx4, dy, redf = _final_loss(x4, small["final_g"], tgt, g2[1], ffn1[3], ts, "final_loss")
    grads = {"final_g": redf[0]}
    dmod = [[None] * 6, [None] * 6]
    dmod[1][5] = redf[1]

    def ffn_bwd(dxo, dy2, xin, i, saved, gprev, yprev, tag):
        a, h, f, _, cw, cb = saved
        wd, wu = wts["ffn_w_down"][i], wts["ffn_w_up"][i]
        g_wd = _matmul(f, dy2, True, False, 1408, D, ts, BF16, "ffn_dwdown" + tag)
        df = _matmul(dy2, _wt(wd), False, False, ts, 1408, D, BF16, "ffn_df" + tag)
        da, redc = _conv_gate_bwd(a, df, cw, cb, ts, "ffn_conv_bwd" + tag)
        g_wu = _matmul(h, da, True, False, D, 1408, ts, BF16, "ffn_dwup" + tag, out_parts=N_CHIP)
        dh = _matmul(da, _wt(wu), False, False, ts, D, 1408, F32, "ffn_dh" + tag)
        outs = _norm_bwd(xin, dh, dxo, row(n2g, i), sc2[i], gprev, yprev, ts, "ffn_norm_bwd" + tag)
        return outs, g_wd, g_wu, redc

    (dx3, dy1, red), g_wd1, g_wu1, redc1 = ffn_bwd(dx4, dy, x3, 1, ffn1, g1[1], y1, "1")
    dmod[1][3], dmod[1][4], dn2g1, dmod[1][2] = red[0], red[1], red[2], red[3]

    g_swo = _matmul(yy, dy1, True, False, 1024, D, ts, BF16, "sgu_dwout")
    dyy = _matmul(dy1, _wt(wts["sgu_w_out"]), False, False, ts, 1024, D, BF16, "sgu_dyy")
    wst = jnp.swapaxes(ws, 1, 2)
    dz, rb, rv, dws, dbst = _sgu_bwd(z, dyy, small["sgu_v_gain"], small["sgu_v_bias"], ws, wst, bst, tr, "sgu_mix_bwd")
    g_swi = _matmul(h3, dz, True, False, D, 1024, ts, BF16, "sgu_dwin", out_parts=N_CHIP)
    dh3 = _matmul(dz, _wt(wts["sgu_w_in"]), False, False, ts, D, 1024, F32, "sgu_dh")
    dx2, dy2_0, red = _norm_bwd(x2, dh3, dx3, row(n1g, 1), sc1[1], g2[0], ffn0[3], ts, "sgu_norm_bwd")
    dmod[1][0], dmod[1][1], dn1g1, dmod[0][5] = red[0], red[1], red[2], red[3]

    (dx1, dy0, red), g_wd0, g_wu0, redc0 = ffn_bwd(dx2, dy2_0, x1, 0, ffn0, g1[0], y0, "0")
    dmod[0][3], dmod[0][4], dn2g0, dmod[0][2] = red[0], red[1], red[2], red[3]

    g_fwo = _matmul(gated, dy0, True, False, D, D, ts, BF16, "fox_dwout")
    dgated = _matmul(dy0, _wt(wts["fox_w_out"]), False, False, ts, D, D, F32, "fox_dgated")
    doa, dop = _attn_bwd_prep(dgated, att, proj, ts, "attn_bwd_prep")
    css = comm["rs_prepare"]([g_fwo, g_swi, g_swo, g_wu0, g_wu1, g_wd0, g_wd1]) if comm else []
    dqa, dka, dva, rcvs = _attn_bwd(ql, ka, va, doa, tq, "attn_bwd", css=css)
    dproj, redx = _fox_post_bwd(proj, dqa, dka, dva, dop, qg2, kg2, bfp, tp, "fox_post_bwd")
    g_fwi = _matmul(h1, dproj, True, False, D, 1408, ts, BF16, "fox_dwin")
    dh1 = _matmul(dproj, _wt(wts["fox_w_in"]), False, False, ts, D, 1408, F32, "fox_dh")
    dx0, red = _norm_bwd(x, dh1, dx1, row(n1g, 0), sc1[0], None, None, ts, "fox_norm_bwd")
    dmod[0][0], dmod[0][1], dn1g0 = red[0], red[1], red[2]

    grads.update(
        fox_w_in=g_fwi, fox_w_out=g_fwo, sgu_w_in=g_swi, sgu_w_out=g_swo,
        ffn_w_up=[g_wu0, g_wu1], ffn_w_down=[g_wd0, g_wd1],
        fox_q_gain=redx[0, :DH] + redx[0, DH:], fox_k_gain=redx[1, :DH] + redx[1, DH:], fox_b_f=redx[2, :H],
        sgu_b_in=rb[0], sgu_v_gain=rv[0], sgu_v_bias=rv[1], sgu_w_s=dws, sgu_b_s=dbst[:, :SGG].T,
        ffn_conv_w=jnp.stack([jnp.swapaxes(r[:, 0:3], 0, 1).reshape(3, 2 * DFF) for r in (redc0, redc1)]),
        ffn_conv_b=jnp.stack([r[:, 3].reshape(2 * DFF) for r in (redc0, redc1)]),
        norm1_g=jnp.stack([dn1g0, dn1g1]), norm2_g=jnp.stack([dn2g0, dn2g1]),
    )
    dmod_arr = jnp.stack([jnp.concatenate(dmod[0]), jnp.concatenate(dmod[1])])
    return lsum[0, 0], dx0, grads, dmod_arr, (css, rcvs)


N_DEV = 8
N_CHIP = 4
HBM_SPEC = pl.BlockSpec(memory_space=pltpu.HBM)
VMEM_SPEC = pl.BlockSpec(memory_space=pltpu.VMEM)


def _mesh_pos():
    return lax.axis_index("x"), lax.axis_index("y"), lax.axis_index("c")


def _other_chips(x, y):
    return [(1 - x, y), (x, 1 - y), (1 - x, 1 - y)]


def _remote(src, dst, ssem, rsem, dev):
    return pltpu.make_async_remote_copy(src_ref=src, dst_ref=dst, send_sem=ssem, recv_sem=rsem,
                                        device_id=dev, device_id_type=MESH)


def _allgather8(xb, name):
    m_per, n = xb.shape

    def body(x_ref, out_ref, send_sems, recv_sems, local_sem):
        x, y, c = _mesh_pos()
        me, sibling = (x, y, c), (x, y, 1 - c)
        chips = _other_chips(x, y)

        def rows(px, py, pc):
            return out_ref.at[pl.ds((4 * px + 2 * py + pc) * m_per, m_per), :]

        def copy(k, block, to, src=None):
            return _remote(rows(*block) if src is None else src, rows(*block),
                           send_sems.at[k], recv_sems.at[k], to)

        mine = pltpu.make_async_copy(x_ref, rows(*me), local_sem)
        mine.start()
        first = [copy(0, me, sibling, src=x_ref)]
        first += [copy(1 + j, me, (*chip, c), src=x_ref) for j, chip in enumerate(chips)]
        for cp in first:
            cp.start()
        passed = [copy(4 + j, (*chip, c), sibling) for j, chip in enumerate(chips)]
        for j, chip in enumerate(chips):
            copy(1 + j, (*chip, c), me).wait_recv()
            passed[j].start()
        copy(0, sibling, me).wait_recv()
        for j, chip in enumerate(chips):
            copy(4 + j, (*chip, 1 - c), me).wait_recv()
        for cp in first + passed:
            cp.wait_send()
        mine.wait()

    return pl.pallas_call(
        body, name=name,
        out_shape=jax.ShapeDtypeStruct((N_DEV * m_per, n), xb.dtype),
        in_specs=[VMEM_SPEC], out_specs=VMEM_SPEC,
        scratch_shapes=[pltpu.SemaphoreType.DMA((7,)), pltpu.SemaphoreType.DMA((7,)), pltpu.SemaphoreType.DMA],
        compiler_params=pltpu.CompilerParams(vmem_limit_bytes=V7X_VMEM_LIMIT),
    )(xb)


def _gather_shards(shards, name):
    na = len(shards)

    def body(*refs):
        p_refs, o_refs = refs[:na], refs[na:2 * na]
        send_sems, recv_sems, pass_send, pass_recv = refs[2 * na:]
        x, y, c = _mesh_pos()
        me = 2 * x + y
        sibling = (x, y, 1 - c)
        chips = _other_chips(x, y)

        def half(a, ci, hf):
            rh = shards[a].shape[0] // 2
            return o_refs[a].at[ci, pl.ds(hf * rh, rh), :]

        sends = []
        for a in range(na):
            rh = shards[a].shape[0] // 2
            for k, chip in enumerate(chips):
                sends.append(_remote(p_refs[a].at[pl.ds(c * rh, rh), :], half(a, me, c),
                                     send_sems.at[3 * a + k], recv_sems.at[3 * a + k], (*chip, c)))
        for cp in sends:
            cp.start()
        passed = []
        for a in range(na):
            for k, chip in enumerate(chips):
                ci = 2 * chip[0] + chip[1]
                _remote(half(a, ci, c), half(a, ci, c), send_sems.at[3 * a + k], recv_sems.at[3 * a + k],
                        (*chip, c)).wait_recv()
                cp = _remote(half(a, ci, c), half(a, ci, c), pass_send.at[3 * a + k], pass_recv.at[3 * a + k], sibling)
                cp.start()
                passed.append(cp)
        for a in range(na):
            for k, chip in enumerate(chips):
                ci = 2 * chip[0] + chip[1]
                _remote(half(a, ci, 1 - c), half(a, ci, 1 - c), pass_send.at[3 * a + k], pass_recv.at[3 * a + k],
                        sibling).wait_recv()
        for cp in sends + passed:
            cp.wait_send()

    return pl.pallas_call(
        body, name=name,
        out_shape=[jax.ShapeDtypeStruct((N_CHIP,) + p.shape, p.dtype) for p in shards],
        in_specs=[HBM_SPEC] * na, out_specs=[HBM_SPEC] * na,
        scratch_shapes=[pltpu.SemaphoreType.DMA((3 * na,))] * 4,
    )(*shards)


def _rs_to_sibling(gs, name):
    na = len(gs)

    def body(*refs):
        g_refs, o_refs, ssems, rsems = refs[:na], refs[na:2 * na], refs[2 * na], refs[2 * na + 1]
        x, y, c = _mesh_pos()
        cps = []
        for a in range(na):
            rh = gs[a].shape[1] // 2
            cp = _remote(g_refs[a].at[:, pl.ds((1 - c) * rh, rh), :], o_refs[a], ssems.at[a], rsems.at[a],
                         (x, y, 1 - c))
            cp.start()
            cps.append(cp)
        for cp in cps:
            cp.wait()

    return pl.pallas_call(
        body, name=name,
        out_shape=[jax.ShapeDtypeStruct((g.shape[0], g.shape[1] // 2, g.shape[2]), g.dtype) for g in gs],
        in_specs=[HBM_SPEC] * na, out_specs=[HBM_SPEC] * na,
        scratch_shapes=[pltpu.SemaphoreType.DMA((na,)), pltpu.SemaphoreType.DMA((na,))],
    )(*gs)


def _rs_chip_sum(g, sib, c_arr, tr, name):
    nc, r, n = g.shape
    rh = r // 2
    g4 = g.reshape(nc, 2, rh, n)

    def body(c_ref, g_ref, s_ref, o_ref):
        o_ref[...] = (g_ref[0].astype(F32) + s_ref[...].astype(F32)).astype(BF16)

    return pl.pallas_call(
        body, name=name, out_shape=jax.ShapeDtypeStruct((nc, rh, n), BF16),
        grid_spec=pltpu.PrefetchScalarGridSpec(
            num_scalar_prefetch=1, grid=(nc, rh // tr),
            in_specs=[pl.BlockSpec((1, 1, tr, n), lambda j, i, cr: (j, cr[0], i, 0)),
                      pl.BlockSpec((1, tr, n), lambda j, i, cr: (j, i, 0))],
            out_specs=pl.BlockSpec((1, tr, n), lambda j, i, cr: (j, i, 0))),
        compiler_params=_cparams(("arbitrary", "arbitrary")),
    )(c_arr, g4, sib)


def _rs_across_chips(css, name):
    na = len(css)

    def body(*refs):
        cs_refs, o_refs, send_sems, recv_sems = refs[:na], refs[na:2 * na], refs[2 * na], refs[2 * na + 1]
        x, y, c = _mesh_pos()
        cps = []
        for a in range(na):
            for k, chip in enumerate(_other_chips(x, y)):
                ci = 2 * chip[0] + chip[1]
                cp = _remote(cs_refs[a].at[ci], o_refs[a].at[k], send_sems.at[3 * a + k], recv_sems.at[3 * a + k],
                             (*chip, c))
                cp.start()
                cps.append(cp)
        for cp in cps:
            cp.wait()

    return pl.pallas_call(
        body, name=name, out_shape=[jax.ShapeDtypeStruct((3,) + cs.shape[1:], cs.dtype) for cs in css],
        in_specs=[HBM_SPEC] * na, out_specs=[HBM_SPEC] * na,
        scratch_shapes=[pltpu.SemaphoreType.DMA((3 * na,)), pltpu.SemaphoreType.DMA((3 * na,))],
    )(*css)


def _rs_final_sum(cs, rcv, me_arr, tr, name):
    nc, rh, n = cs.shape

    def body(m_ref, c_ref, r_ref, o_ref):
        acc = c_ref[0].astype(F32)
        for k in range(3):
            acc = acc + r_ref[k].astype(F32)
        o_ref[...] = acc

    return pl.pallas_call(
        body, name=name, out_shape=jax.ShapeDtypeStruct((rh, n), F32),
        grid_spec=pltpu.PrefetchScalarGridSpec(
            num_scalar_prefetch=1, grid=(rh // tr,),
            in_specs=[pl.BlockSpec((1, tr, n), lambda i, mr: (mr[0], i, 0)),
                      pl.BlockSpec((3, tr, n), lambda i, mr: (0, i, 0))],
            out_specs=pl.BlockSpec((tr, n), lambda i, mr: (i, 0))),
        compiler_params=_cparams(("arbitrary",)),
    )(me_arr, cs, rcv)


def _rs_swap_halves(halves, name):
    na = len(halves)

    def body(*refs):
        h_refs, o_refs, ssems, rsems = refs[:na], refs[na:2 * na], refs[2 * na], refs[2 * na + 1]
        x, y, c = _mesh_pos()
        cps = []
        for a in range(na):
            cp = _remote(h_refs[a], o_refs[a], ssems.at[a], rsems.at[a], (x, y, 1 - c))
            cp.start()
            cps.append(cp)
        for cp in cps:
            cp.wait()

    return pl.pallas_call(
        body, name=name, out_shape=[jax.ShapeDtypeStruct(h.shape, h.dtype) for h in halves],
        in_specs=[HBM_SPEC] * na, out_specs=[HBM_SPEC] * na,
        scratch_shapes=[pltpu.SemaphoreType.DMA((na,)), pltpu.SemaphoreType.DMA((na,))],
    )(*halves)


def _sum8(g, name):
    nd, r, n = g.shape

    def body(g_ref, o_ref):
        acc = g_ref[0]
        for k in range(1, nd):
            acc = acc + g_ref[k]
        o_ref[...] = acc

    return pl.pallas_call(
        body, name=name, grid=(r // 8,),
        in_specs=[pl.BlockSpec((nd, 8, n), lambda i: (0, i, 0))],
        out_specs=pl.BlockSpec((8, n), lambda i: (i, 0)),
        out_shape=jax.ShapeDtypeStruct((r, n), F32),
        compiler_params=_cparams(("arbitrary",)),
    )(g)


def _adamw(w, g, m, v, name):
    r, n = w.shape
    tr = 128 if r % 128 == 0 else 8
    bc1 = 1.0 - ADAM_B1 ** ADAM_STEP
    bc2 = 1.0 - ADAM_B2 ** ADAM_STEP

    def body(w_ref, g_ref, m_ref, v_ref, d_ref, mo_ref, vo_ref):
        gv = g_ref[...]
        mn = ADAM_B1 * m_ref[...] + (1.0 - ADAM_B1) * gv
        vn = ADAM_B2 * v_ref[...] + (1.0 - ADAM_B2) * (gv * gv)
        d_ref[...] = -ADAM_LR * ((mn / bc1) / (jnp.sqrt(vn / bc2) + ADAM_EPS) + ADAM_WD * w_ref[...])
        mo_ref[...] = mn
        vo_ref[...] = vn

    blk = pl.BlockSpec((tr, n), lambda i: (i, 0))
    shp = jax.ShapeDtypeStruct((r, n), F32)
    return pl.pallas_call(
        body, name=name, grid=(r // tr,), in_specs=[blk] * 4, out_specs=[blk] * 3, out_shape=[shp] * 3,
        compiler_params=_cparams(("arbitrary",)),
    )(w, g, m, v)


ROW = 1024
PACK_ROWS = 7168
BIG = ("fox_w_in", "fox_w_out", "sgu_w_in", "sgu_w_out", "ffn_w_up", "ffn_w_down")
SMALL_SHARDED = ("sgu_b_in", "sgu_v_gain", "sgu_v_bias", "ffn_conv_w")
SMALL_REPL = ("fox_b_f", "fox_q_gain", "fox_k_gain", "sgu_w_s", "sgu_b_s", "ffn_conv_b", "ada_b",
              "norm1_g", "norm2_g", "final_g")
WEIGHTS = ("fox_w_in", "fox_b_f", "fox_q_gain", "fox_k_gain", "fox_w_out", "sgu_w_in", "sgu_b_in", "sgu_v_gain",
           "sgu_v_bias", "sgu_w_s", "sgu_b_s", "sgu_w_out", "ffn_w_up", "ffn_conv_w", "ffn_conv_b", "ffn_w_down",
           "ada_w", "ada_b", "norm1_g", "norm2_g", "final_g")


def _rows_of(a, mult=1):
    flat = a.reshape(-1)
    rows = -(-flat.shape[0] // ROW)
    rows = -(-rows // mult) * mult
    return jnp.pad(flat, (0, rows * ROW - flat.shape[0])).reshape(rows, ROW)


def _pack(parts, mult, total=None):
    p = jnp.concatenate([_rows_of(a, mult) for a in parts], axis=0)
    if total is not None:
        p = jnp.pad(p, ((0, total - p.shape[0]), (0, 0)))
    return p


def _unpack(pack, shapes, mult):
    out, r0 = [], 0
    for shp in shapes:
        size = int(np.prod(shp))
        rows = -(-(-(-size // ROW)) // mult) * mult
        out.append(pack[r0:r0 + rows].reshape(-1)[:size].reshape(shp))
        r0 += rows
    return out


def _big_shards(t):
    return [t["fox_w_in"][0], t["fox_w_out"][0], t["sgu_w_in"][0], t["sgu_w_out"][0],
            t["ffn_w_up"][0], t["ffn_w_up"][1], t["ffn_w_down"][0], t["ffn_w_down"][1]]


def _row_tile(rows):
    return next(t for t in (512, 352, 256, 128, 64) if rows % t == 0)


def kernel(x, c, fox_w_in, fox_b_f, fox_q_gain, fox_k_gain, fox_w_out, sgu_w_in, sgu_b_in, sgu_v_gain, sgu_v_bias, sgu_w_s, sgu_b_s, sgu_w_out, ffn_w_up, ffn_conv_w, ffn_conv_b, ffn_w_down, ada_w, ada_b, norm1_g, norm2_g, final_g, loss_target, m_fox_w_in, m_fox_b_f, m_fox_q_gain, m_fox_k_gain, m_fox_w_out, m_sgu_w_in, m_sgu_b_in, m_sgu_v_gain, m_sgu_v_bias, m_sgu_w_s, m_sgu_b_s, m_sgu_w_out, m_ffn_w_up, m_ffn_conv_w, m_ffn_conv_b, m_ffn_w_down, m_ada_w, m_ada_b, m_norm1_g, m_norm2_g, m_final_g, v_fox_w_in, v_fox_b_f, v_fox_q_gain, v_fox_k_gain, v_fox_w_out, v_sgu_w_in, v_sgu_b_in, v_sgu_v_gain, v_sgu_v_bias, v_sgu_w_s, v_sgu_b_s, v_sgu_w_out, v_ffn_w_up, v_ffn_conv_w, v_ffn_conv_b, v_ffn_w_down, v_ada_w, v_ada_b, v_norm1_g, v_norm2_g, v_final_g):
    w = dict(fox_w_in=fox_w_in, fox_b_f=fox_b_f, fox_q_gain=fox_q_gain, fox_k_gain=fox_k_gain, fox_w_out=fox_w_out,
             sgu_w_in=sgu_w_in, sgu_b_in=sgu_b_in, sgu_v_gain=sgu_v_gain, sgu_v_bias=sgu_v_bias, sgu_w_s=sgu_w_s,
             sgu_b_s=sgu_b_s, sgu_w_out=sgu_w_out, ffn_w_up=ffn_w_up, ffn_conv_w=ffn_conv_w, ffn_conv_b=ffn_conv_b,
             ffn_w_down=ffn_w_down, ada_w=ada_w, ada_b=ada_b, norm1_g=norm1_g, norm2_g=norm2_g, final_g=final_g)
    mom = dict(fox_w_in=m_fox_w_in, fox_b_f=m_fox_b_f, fox_q_gain=m_fox_q_gain, fox_k_gain=m_fox_k_gain,
               fox_w_out=m_fox_w_out, sgu_w_in=m_sgu_w_in, sgu_b_in=m_sgu_b_in, sgu_v_gain=m_sgu_v_gain,
               sgu_v_bias=m_sgu_v_bias, sgu_w_s=m_sgu_w_s, sgu_b_s=m_sgu_b_s, sgu_w_out=m_sgu_w_out,
               ffn_w_up=m_ffn_w_up, ffn_conv_w=m_ffn_conv_w, ffn_conv_b=m_ffn_conv_b, ffn_w_down=m_ffn_w_down,
               ada_w=m_ada_w, ada_b=m_ada_b, norm1_g=m_norm1_g, norm2_g=m_norm2_g, final_g=m_final_g)
    var = dict(fox_w_in=v_fox_w_in, fox_b_f=v_fox_b_f, fox_q_gain=v_fox_q_gain, fox_k_gain=v_fox_k_gain,
               fox_w_out=v_fox_w_out, sgu_w_in=v_sgu_w_in, sgu_b_in=v_sgu_b_in, sgu_v_gain=v_sgu_v_gain,
               sgu_v_bias=v_sgu_v_bias, sgu_w_s=v_sgu_w_s, sgu_b_s=v_sgu_b_s, sgu_w_out=v_sgu_w_out,
               ffn_w_up=v_ffn_w_up, ffn_conv_w=v_ffn_conv_w, ffn_conv_b=v_ffn_conv_b, ffn_w_down=v_ffn_w_down,
               ada_w=v_ada_w, ada_b=v_ada_b, norm1_g=v_norm1_g, norm2_g=v_norm2_g, final_g=v_final_g)

    ax, ay, ac = _mesh_pos()
    chip = 2 * ax + ay
    dev = 2 * chip + ac

    small_shard_shapes = tuple(w[n].shape for n in SMALL_SHARDED)
    blk = _pack([c] + [w[n] for n in SMALL_SHARDED], 1, 16)
    gat = _allgather8(blk, "gather_small").reshape(N_DEV, 16, ROW)
    c_all = gat[:, 0, :]
    per_chip = [_unpack(gat[2 * j, 1:], small_shard_shapes, 1) for j in range(N_CHIP)]
    full_small = {n: jnp.concatenate([per_chip[j][i] for j in range(N_CHIP)], axis=-1)
                  for i, n in enumerate(SMALL_SHARDED)}

    mine = [a.astype(BF16) for a in _big_shards(w)]
    with_own = lambda gat, own: [lax.dynamic_update_slice(g_, m_[None], (chip, 0, 0)) for g_, m_ in zip(gat, own)]
    fwi, = with_own(_gather_shards(mine[:1], "gather_fox_w_in"), mine[:1])
    fwi_full = jnp.concatenate([fwi[j] for j in range(N_CHIP)] + [jnp.zeros((D, FOX_NP - FOX_N), BF16)], axis=1)
    wts = dict(fox_w_in=fwi_full)

    def make_wts(gathered):
        fwo, swi, swo, up0, up1, dn0, dn1 = with_own(gathered, mine[1:])
        return dict(fox_w_out=fwo.reshape(D, D), sgu_w_in=swi, sgu_w_out=swo.reshape(SGW, D),
                    ffn_w_up=[up0, up1], ffn_w_down=[dn0.reshape(DFF, D), dn1.reshape(DFF, D)])

    c_arr = jnp.reshape(ac, (1,)).astype(jnp.int32)
    me_arr = jnp.reshape(chip, (1,)).astype(jnp.int32)

    def chip_sums(glist, tag):
        sibs = _rs_to_sibling(glist, "rs_sibling" + tag)
        return [_rs_chip_sum(g_, s_, c_arr, _row_tile(s_.shape[1]), "rs_chip_sum%s%d" % (tag, a))
                for a, (g_, s_) in enumerate(zip(glist, sibs))]

    def rs_prepare(gl):
        g_fwo, g_swi, g_swo, g_wu0, g_wu1, g_wd0, g_wd1 = gl
        return chip_sums([g_fwo.reshape(N_CHIP, 256, D), g_swi, g_swo.reshape(N_CHIP, 512, D), g_wu0, g_wu1,
                          g_wd0.reshape(N_CHIP, 704, D), g_wd1.reshape(N_CHIP, 704, D)], "")

    comm = dict(shards=mine[1:], make_wts=make_wts, rs_prepare=rs_prepare)

    da = ada_w.shape[2]
    ada_b_cols = lax.dynamic_slice_in_dim(ada_b, chip * da, da, axis=1)[:, None, :]
    mod_cols, c_act = _ada_mod(c_all, ada_w, ada_b_cols)
    mod_all = _allgather8(mod_cols.reshape(-1, ROW), "gather_mod").reshape(N_DEV, 2, N_DEV, da)
    mod_mine = lax.dynamic_index_in_dim(mod_all[0::2], dev, axis=2, keepdims=False)
    mod = jnp.swapaxes(mod_mine, 0, 1).reshape(2, N_CHIP * da)

    small = dict(norm1_g=norm1_g, norm2_g=norm2_g, final_g=final_g[None], fox_q_gain=fox_q_gain,
                 fox_k_gain=fox_k_gain, fox_b_f=fox_b_f, sgu_b_in=full_small["sgu_b_in"],
                 sgu_v_gain=full_small["sgu_v_gain"], sgu_v_bias=full_small["sgu_v_bias"], sgu_w_s=sgu_w_s[0],
                 sgu_b_s=sgu_b_s[0], ffn_conv_w=full_small["ffn_conv_w"], ffn_conv_b=ffn_conv_b)
    loss_dev, dx, g, dmod, (css, rcvs) = _local_step(x[0], loss_target[0], mod, wts, small, comm)

    g["ada_b"] = dmod
    g["loss"] = loss_dev
    small_names = ("ada_b",) + SMALL_SHARDED + tuple(n for n in SMALL_REPL if n != "ada_b") + ("loss",)
    gs = _pack([g[n] for n in small_names], 1)
    rows_s = -(-gs.shape[0] // 8) * 8
    gs = jnp.pad(gs, ((0, rows_s - gs.shape[0]), (0, 0)))
    gs_all = _allgather8(gs, "gather_small_grads").reshape(N_DEV, rows_s, ROW)
    gsum = _sum8(gs_all, "sum_small_grads")
    full_shapes = {n: w[n].shape for n in SMALL_REPL}
    full_shapes.update({n: w[n].shape[:-1] + (w[n].shape[-1] * N_CHIP,) for n in SMALL_SHARDED})
    full_shapes["loss"] = ()
    gfull = dict(zip(small_names, _unpack(gsum, [full_shapes[n] for n in small_names], 1)))
    grads = {n: gfull[n] for n in SMALL_REPL}
    for n in SMALL_SHARDED:
        width = w[n].shape[-1]
        grads[n] = lax.dynamic_slice_in_dim(gfull[n], chip * width, width, axis=gfull[n].ndim - 1)
    dmod_all = gs_all[:, :12, :].reshape(N_DEV, 2, N_CHIP * da)
    dmod_cols = jnp.swapaxes(lax.dynamic_slice_in_dim(dmod_all, chip * da, da, axis=2), 0, 1)
    grads["ada_w"] = _ada_w_grad(c_act.T, dmod_cols)

    gfi = jnp.stack([g["fox_w_in"][:, 1028 * j:1028 * (j + 1)] for j in range(N_CHIP)])
    cs_fox = chip_sums([gfi], "_fox")
    css = cs_fox + list(css)
    rcvs = list(_rs_across_chips(cs_fox, "rs_chips_fox")) + list(rcvs)
    halves = [_rs_final_sum(cs_, r_, me_arr, _row_tile(cs_.shape[1]), "rs_final_sum%d" % a)
              for a, (cs_, r_) in enumerate(zip(css, rcvs))]
    others = _rs_swap_halves(halves, "rs_swap")
    red = [jnp.concatenate([jnp.where(ac == 0, h_, o_), jnp.where(ac == 0, o_, h_)]) for h_, o_ in zip(halves, others)]
    grads.update(fox_w_in=red[0], fox_w_out=red[1], sgu_w_in=red[2], sgu_w_out=red[3],
                 ffn_w_up=jnp.stack([red[4], red[5]]), ffn_w_down=jnp.stack([red[6], red[7]]))

    delta, new_m, new_v = {}, {}, {}
    for n in BIG + ("ada_w",):
        shp = w[n].shape
        two_d = lambda a: a.reshape(-1, shp[-1])
        d_, m_, v_ = _adamw(two_d(w[n]), two_d(grads[n]), two_d(mom[n]), two_d(var[n]), "adamw_" + n)
        delta[n], new_m[n], new_v[n] = d_.reshape(shp), m_.reshape(shp), v_.reshape(shp)
    rest = SMALL_SHARDED + SMALL_REPL
    packs = [_pack([t[n] for n in rest], 1) for t in (w, grads, mom, var)]
    rows_r = -(-packs[0].shape[0] // 8) * 8
    packs = [jnp.pad(p, ((0, rows_r - p.shape[0]), (0, 0))) for p in packs]
    outs = _adamw(*packs, "adamw_small")
    for t, o in zip((delta, new_m, new_v), outs):
        t.update(zip(rest, _unpack(o, [w[n].shape for n in rest], 1)))

    loss = gfull["loss"]
    return (loss, dx[None], *[grads[n].reshape(w[n].shape) for n in WEIGHTS], *[delta[n] for n in WEIGHTS],
            *[new_m[n] for n in WEIGHTS], *[new_v[n] for n in WEIGHTS])
```

```python
import functools
import math

import numpy as np
import jax
import jax.numpy as jnp
from jax import lax
from jax.experimental import pallas as pl
from jax.experimental.pallas import tpu as pltpu

F32 = jnp.float32
BF16 = jnp.bfloat16
MESH = pl.DeviceIdType.MESH

D = 1024
H = 16
DH = 64
NP = H // 2
LANES = 128
DFF = 2816
SGW = 2048
SGG = 8
SGC = 256
SGB = 128
CHUNK = 64
EPS = 1e-6
FOX_N = 4 * D + H
FOX_NP = 4224
GT = 256
NGT = DFF // GT
SCALE = DH ** -0.5
LOG2E = 1.4426950408889634

ADAM_LR = 0.001
ADAM_B1 = 0.9
ADAM_B2 = 0.999
ADAM_EPS = 1e-08
ADAM_WD = 0.01
ADAM_STEP = 10

V7X_VMEM_LIMIT = 56 * 1024 * 1024

L_F = 64
L_NF = 67
L_LSE = 70


def _cparams(sem=None):
    return pltpu.CompilerParams(dimension_semantics=sem, vmem_limit_bytes=V7X_VMEM_LIMIT)


def _split3(x):
    hi = x.astype(BF16)
    r = x - hi.astype(F32)
    mid = r.astype(BF16)
    lo = (r - mid.astype(F32)).astype(BF16)
    return hi, mid, lo


def _dot(a, b, dims=(((1,), (0,)), ((), ()))):
    return lax.dot_general(a, b, dims, preferred_element_type=F32)


def _dot_nt(a, b):
    return _dot(a, b, (((1,), (1,)), ((), ())))


def _dot_tn(a, b):
    return _dot(a, b, (((0,), (0,)), ((), ())))


def _exact_dot(m_bf16, x_f32):
    hi, mid, lo = _split3(x_f32)
    return _dot(m_bf16, hi) + _dot(m_bf16, mid) + _dot(m_bf16, lo)


def _exact_dot_r(x_f32, m_bf16):
    hi, mid, lo = _split3(x_f32)
    return _dot(hi, m_bf16) + _dot(mid, m_bf16) + _dot(lo, m_bf16)


def _head_block_ones():
    r = lax.broadcasted_iota(jnp.int32, (LANES, LANES), 0) // DH
    c = lax.broadcasted_iota(jnp.int32, (LANES, LANES), 1) // DH
    return (r == c).astype(BF16)


def _sigmoid(x):
    return 1.0 / (1.0 + jnp.exp(-x))


def _gelu(x):
    c = math.sqrt(2.0 / math.pi)
    return 0.5 * x * (1.0 + jnp.tanh(c * (x + 0.044715 * (x * x * x))))


def _gelu_and_grad(x):
    c = math.sqrt(2.0 / math.pi)
    x2 = x * x
    t = jnp.tanh(c * (x + 0.044715 * (x2 * x)))
    half = 0.5 * (1.0 + t)
    return x * half, half + 0.5 * x * (1.0 - t * t) * c * (1.0 + 3 * 0.044715 * x2)


def _rstd_rows(x):
    return lax.rsqrt(jnp.mean(x * x, axis=-1, keepdims=True) + EPS)


def _norm_mod_matmul(x, ng, sc, sh, w, bias, out_dtype, ts, tn, name, planes=1):
    s, d = x.shape
    ns = w.shape[-1]
    n = w.shape[0] * ns if w.ndim == 3 else ns
    nc = n // planes

    def body(x_ref, ng_ref, sc_ref, sh_ref, w_ref, b_ref, o_ref, h_ref):
        xv = x_ref[...]
        h = (xv * _rstd_rows(xv) * ng_ref[...] * (1.0 + sc_ref[...]) + sh_ref[...]).astype(BF16)
        h_ref[...] = h
        for e in range(planes):
            for c0 in range(0, nc, tn):
                g0 = e * nc + c0
                wv = w_ref[g0 // ns, :, g0 % ns:g0 % ns + tn] if w.ndim == 3 else w_ref[:, g0:g0 + tn]
                val = (_dot(h, wv) + b_ref[:, g0:g0 + tn]).astype(out_dtype)
                if planes == 1:
                    o_ref[:, c0:c0 + tn] = val
                else:
                    o_ref[e, :, c0:c0 + tn] = val

    vec = pl.BlockSpec((1, d), lambda i: (0, 0))
    w_spec = (pl.BlockSpec(w.shape, lambda i: (0, 0, 0)) if w.ndim == 3 else pl.BlockSpec((d, n), lambda i: (0, 0)))
    if planes == 1:
        o_spec, o_shape = pl.BlockSpec((ts, n), lambda i: (i, 0)), (s, n)
    else:
        o_spec, o_shape = pl.BlockSpec((planes, ts, nc), lambda i: (0, i, 0)), (planes, s, nc)
    return pl.pallas_call(
        body, name=name, grid=(s // ts,),
        in_specs=[pl.BlockSpec((ts, d), lambda i: (i, 0)), vec, vec, vec, w_spec,
                  pl.BlockSpec((1, n), lambda i: (0, 0))],
        out_specs=[o_spec, pl.BlockSpec((ts, d), lambda i: (i, 0))],
        out_shape=[jax.ShapeDtypeStruct(o_shape, out_dtype), jax.ShapeDtypeStruct((s, d), BF16)],
        compiler_params=_cparams(("arbitrary",)),
    )(x, ng, sc, sh, w, bias)


def _matmul(a, b, ta, tb, tm, tn, tk, out_dtype, name, out_parts=1):
    if a.ndim == 3:
        m, k = a.shape[1], a.shape[0] * a.shape[2]
        nkp = a.shape[2] // tk
    else:
        m, k = (a.shape[1], a.shape[0]) if ta else a.shape
    if b.ndim == 3:
        n = b.shape[1] if tb else b.shape[0] * b.shape[2]
        nbp = b.shape[2] // (tk if tb else tn)
    else:
        n = b.shape[0] if tb else b.shape[1]
    nk = k // tk
    nop = n // out_parts // tn
    dims = (((0,) if ta else (1,), (1,) if tb else (0,)), ((), ()))

    def body(a_ref, b_ref, o_ref, acc):
        kk = pl.program_id(2)

        @pl.when(kk == 0)
        def _():
            acc[...] = jnp.zeros_like(acc)
        acc[...] += _dot(a_ref[...], b_ref[...], dims)

        @pl.when(kk == nk - 1)
        def _():
            o_ref[...] = acc[...].astype(out_dtype)

    if a.ndim == 3:
        a_spec = pl.BlockSpec((None, tm, tk), lambda i, j, kk: (kk // nkp, i, kk % nkp))
    else:
        a_spec = (pl.BlockSpec((tk, tm), lambda i, j, kk: (kk, i)) if ta
                  else pl.BlockSpec((tm, tk), lambda i, j, kk: (i, kk)))
    if b.ndim == 3 and tb:
        b_spec = pl.BlockSpec((None, tn, tk), lambda i, j, kk: (kk // nbp, j, kk % nbp))
    elif b.ndim == 3:
        b_spec = pl.BlockSpec((None, tk, tn), lambda i, j, kk: (j // nbp, kk, j % nbp))
    else:
        b_spec = (pl.BlockSpec((tn, tk), lambda i, j, kk: (j, kk)) if tb
                  else pl.BlockSpec((tk, tn), lambda i, j, kk: (kk, j)))
    if out_parts > 1:
        o_spec = pl.BlockSpec((None, tm, tn), lambda i, j, kk: (j // nop, i, j % nop))
        o_shape = (out_parts, m, n // out_parts)
    else:
        o_spec, o_shape = pl.BlockSpec((tm, tn), lambda i, j, kk: (i, j)), (m, n)
    return pl.pallas_call(
        body, name=name, grid=(m // tm, n // tn, nk),
        in_specs=[a_spec, b_spec],
        out_specs=o_spec,
        out_shape=jax.ShapeDtypeStruct(o_shape, out_dtype),
        scratch_shapes=[pltpu.VMEM((tm, tn), F32)],
        compiler_params=_cparams(("arbitrary", "arbitrary", "arbitrary")),
    )(a, b)


def _matmul_wt(a, w, tn, tk, out_dtype, ts, name):
    s = a.shape[-2]
    ka, kw = a.shape[-1], w.shape[-1]
    k = ka * (a.shape[0] if a.ndim == 3 else 1)
    n = w.shape[-2]

    def body(a_ref, w_ref, o_ref):
        for n0 in range(0, n, tn):
            acc = None
            for g0 in range(0, k, tk):
                av = a_ref[g0 // ka, :, g0 % ka:g0 % ka + tk] if a.ndim == 3 else a_ref[:, g0:g0 + tk]
                wv = (w_ref[g0 // kw, n0:n0 + tn, g0 % kw:g0 % kw + tk] if w.ndim == 3
                      else w_ref[n0:n0 + tn, g0:g0 + tk])
                part = _dot_nt(av, wv)
                acc = part if acc is None else acc + part
            o_ref[:, n0:n0 + tn] = acc.astype(out_dtype)

    a_spec = (pl.BlockSpec((a.shape[0], ts, ka), lambda i: (0, i, 0)) if a.ndim == 3
              else pl.BlockSpec((ts, ka), lambda i: (i, 0)))
    w_spec = pl.BlockSpec(w.shape, (lambda i: (0, 0, 0)) if w.ndim == 3 else (lambda i: (0, 0)))
    return pl.pallas_call(
        body, name=name, grid=(s // ts,),
        in_specs=[a_spec, w_spec], out_specs=pl.BlockSpec((ts, n), lambda i: (i, 0)),
        out_shape=jax.ShapeDtypeStruct((s, n), out_dtype),
        compiler_params=_cparams(("arbitrary",)),
    )(a, w)


def _matmul_residual(a, w, xin, g, ts, name):
    s, k = a.shape
    d = w.shape[1]

    def body(a_ref, w_ref, x_ref, g_ref, o_ref, y_ref):
        y = _dot(a_ref[...], w_ref[...])
        o_ref[...] = x_ref[...] + g_ref[...] * y
        y_ref[...] = y.astype(BF16)

    return pl.pallas_call(
        body, name=name, grid=(s // ts,),
        in_specs=[pl.BlockSpec((ts, k), lambda i: (i, 0)),
                  pl.BlockSpec((k, d), lambda i: (0, 0)),
                  pl.BlockSpec((ts, d), lambda i: (i, 0)),
                  pl.BlockSpec((1, d), lambda i: (0, 0))],
        out_specs=[pl.BlockSpec((ts, d), lambda i: (i, 0)), pl.BlockSpec((ts, d), lambda i: (i, 0))],
        out_shape=[jax.ShapeDtypeStruct((s, d), F32), jax.ShapeDtypeStruct((s, d), BF16)],
        compiler_params=_cparams(("arbitrary",)),
    )(a, w, xin, g)


def _lane(shape):
    return lax.broadcasted_iota(jnp.int32, shape, 1)


def _pair_norm(x, gain2, bones):
    msq = _exact_dot_r(x * x, bones) * (1.0 / DH)
    r = lax.rsqrt(msq + EPS)
    xh = x * r
    return xh * gain2, xh, r


def _fox_post(proj, qg2, kg2, bf, ts, name):
    s = proj.shape[0]

    def body(p_ref, qg_ref, kg_ref, bf_ref, q_ref, k_ref, v_ref, carry):
        @pl.when(pl.program_id(0) == 0)
        def _():
            carry[...] = jnp.zeros_like(carry)
        lane = _lane((ts, LANES))
        bones = _head_block_ones()
        xf = p_ref[:, 4 * D:4 * D + LANES] + bf_ref[...]
        logf = jnp.minimum(xf, 0.0) - jnp.log(1.0 + jnp.exp(-jnp.abs(xf)))
        logf = jnp.where(lane < H, logf, 0.0)
        rr = lax.broadcasted_iota(jnp.int32, (ts, ts), 0)
        cc = lax.broadcasted_iota(jnp.int32, (ts, ts), 1)
        ltri = (cc <= rr).astype(BF16)
        fcum = _exact_dot(ltri, logf) + carry[0:1, :]
        carry[0:1, :] = fcum[ts - 1:ts, :]
        fhi, fmid, flo = _split3(fcum * LOG2E)
        fhi, fmid, flo = fhi.astype(F32), fmid.astype(F32), flo.astype(F32)
        one_q = ((lane >= L_NF) & (lane < L_NF + 3)).astype(F32)
        one_k = (((lane >= L_F) & (lane < L_F + 3)) | ((lane >= L_LSE) & (lane < L_LSE + 3))).astype(F32)
        one_v = ((lane >= L_F) & (lane < L_F + 3)).astype(F32)
        for p in range(NP):
            qn, _, _ = _pair_norm(p_ref[:, p * LANES:(p + 1) * LANES], qg_ref[...], bones)
            kn, _, _ = _pair_norm(p_ref[:, D + p * LANES:D + (p + 1) * LANES], kg_ref[...], bones)
            vv = p_ref[:, 2 * D + p * LANES:2 * D + (p + 1) * LANES]
            qn = qn * (SCALE * LOG2E)
            for e in range(2):
                h = 2 * p + e
                if e == 1:
                    qe, ke, ve = (pltpu.roll(t, DH, axis=1) for t in (qn, kn, vv))
                else:
                    qe, ke, ve = qn, kn, vv
                f0, f1, f2 = fhi[:, h:h + 1], fmid[:, h:h + 1], flo[:, h:h + 1]
                fq = jnp.where(lane == L_F, f0, jnp.where(lane == L_F + 1, f1, jnp.where(lane == L_F + 2, f2, one_q)))
                fk = jnp.where(lane == L_NF, -f0, jnp.where(lane == L_NF + 1, -f1, jnp.where(lane == L_NF + 2, -f2, one_k)))
                q_ref[h] = jnp.where(lane < DH, qe, fq).astype(BF16)
                k_ref[h] = jnp.where(lane < DH, ke, fk).astype(BF16)
                v_ref[h] = jnp.where(lane < DH, ve, one_v).astype(BF16)

    hs = pl.BlockSpec((H, ts, LANES), lambda i: (0, i, 0))
    vec = pl.BlockSpec((1, LANES), lambda i: (0, 0))
    shp = jax.ShapeDtypeStruct((H, s, LANES), BF16)
    return pl.pallas_call(
        body, name=name, grid=(s // ts,),
        in_specs=[pl.BlockSpec((ts, FOX_NP), lambda i: (i, 0)), vec, vec, vec],
        out_specs=[hs, hs, hs], out_shape=[shp, shp, shp],
        scratch_shapes=[pltpu.VMEM((8, LANES), F32)],
        compiler_params=_cparams(("arbitrary",)),
    )(proj, qg2, kg2, bf)


def _gather_copies(p_refs, o_refs, send_sems, recv_sems):
    x, y, c = _mesh_pos()
    me = 2 * x + y
    sends, arrivals = [], []
    for a, (p_ref, o_ref) in enumerate(zip(p_refs, o_refs)):
        rh = p_ref.shape[0] // 2
        for k, chip in enumerate(_other_chips(x, y)):
            ci = 2 * chip[0] + chip[1]
            for cc in range(2):
                sends.append(_remote(p_ref.at[pl.ds(c * rh, rh), :], o_ref.at[me, pl.ds(c * rh, rh), :],
                                     send_sems.at[6 * a + 2 * k + cc], recv_sems.at[6 * a + 2 * k + c], (*chip, cc)))
                arrivals.append(_remote(o_ref.at[ci, pl.ds(cc * rh, rh), :], o_ref.at[ci, pl.ds(cc * rh, rh), :],
                                        send_sems.at[6 * a + 2 * k + cc], recv_sems.at[6 * a + 2 * k + cc],
                                        (*chip, cc)))
    return sends, arrivals


def _attn_fwd(qa, ka, va, tq, name, shards=()):
    s = qa.shape[1]
    nq = s // tq
    na = len(shards)

    def body(*refs):
        q_ref, k_ref, v_ref = refs[:3]
        p_refs = refs[3:3 + na]
        o_ref, ql_ref = refs[3 + na:5 + na]
        g_refs = refs[5 + na:5 + 2 * na]
        i = pl.program_id(1)
        if na:
            send_sems, recv_sems = refs[5 + 2 * na:]

            @pl.when((pl.program_id(0) == 0) & (i == 0))
            def _():
                for cp in _gather_copies(p_refs, g_refs, send_sems, recv_sems)[0]:
                    cp.start()
        lane = _lane((tq, LANES))
        qs_ = [q_ref[0], q_ref[1]]

        th = tq // QSPLIT

        def step(j, carry, masked):
            off = pl.multiple_of(j * tq, tq)
            new = []
            for e in range(2):
                kb = k_ref[e, pl.ds(off, tq), :]
                vb = v_ref[e, pl.ds(off, tq), :]
                for r in range(QSPLIT):
                    m, acc = carry[e * QSPLIT + r]
                    sc = _dot_nt(qs_[e][r * th:(r + 1) * th], kb)
                    if masked:
                        rr = lax.broadcasted_iota(jnp.int32, (th, tq), 0) + r * th
                        cc = lax.broadcasted_iota(jnp.int32, (th, tq), 1)
                        sc = jnp.where(cc <= rr, sc, -jnp.inf)
                    m_new = jnp.maximum(m, jnp.max(sc, axis=-1, keepdims=True))
                    pr = jnp.exp2(sc - m_new)
                    acc = acc * jnp.exp2(m - m_new) + _dot(pr.astype(BF16), vb)
                    new.append((m_new, acc))
            return tuple(new)

        one = (jnp.full((th, 1), -jnp.inf, F32), jnp.zeros((th, LANES), F32))
        carry = lax.fori_loop(0, i, functools.partial(step, masked=False), (one,) * (2 * QSPLIT))
        carry = step(i, carry, True)
        carry = [tuple(jnp.concatenate([carry[e * QSPLIT + r][t] for r in range(QSPLIT)], axis=0) for t in range(2))
                 for e in range(2)]
        outs = []
        for e in range(2):
            m, acc = carry[e]
            l = acc[:, L_F:L_F + 1]
            outs.append(acc / l)
            lse = m + jnp.log2(l)
            h0, h1, h2 = _split3(-lse)
            ql = jnp.where(lane == L_LSE, h0.astype(F32),
                           jnp.where(lane == L_LSE + 1, h1.astype(F32),
                                     jnp.where(lane == L_LSE + 2, h2.astype(F32), qs_[e].astype(F32))))
            ql_ref[e] = ql.astype(BF16)
        o_ref[...] = jnp.where(lane < DH, outs[0], pltpu.roll(outs[1], DH, axis=1))
        if na:
            @pl.when((pl.program_id(0) == NP - 1) & (i == nq - 1))
            def _():
                sends, arrivals = _gather_copies(p_refs, g_refs, send_sems, recv_sems)
                for cp in arrivals:
                    cp.wait_recv()
                for cp in sends:
                    cp.wait_send()

    res = pl.BlockSpec((2, s, LANES), lambda p, i: (p, 0, 0))
    qs = pl.BlockSpec((2, tq, LANES), lambda p, i: (p, i, 0))
    outs = pl.pallas_call(
        body, name=name, grid=(NP, nq),
        in_specs=[qs, res, res] + [HBM_SPEC] * na,
        out_specs=[pl.BlockSpec((tq, LANES), lambda p, i: (i, p)), qs] + [HBM_SPEC] * na,
        out_shape=[jax.ShapeDtypeStruct((s, D), F32), jax.ShapeDtypeStruct((H, s, LANES), BF16)]
        + [jax.ShapeDtypeStruct((N_CHIP,) + p.shape, p.dtype) for p in shards],
        scratch_shapes=[pltpu.SemaphoreType.DMA((6 * na,))] * 2 if na else [],
        compiler_params=_cparams(("arbitrary", "arbitrary")),
    )(qa, ka, va, *shards)
    return outs[0], outs[1], list(outs[2:])


def _chip_exchange_copies(cs_refs, o_refs, send_sems, recv_sems):
    x, y, c = _mesh_pos()
    cps = []
    for a, (cs_ref, o_ref) in enumerate(zip(cs_refs, o_refs)):
        for k, chip in enumerate(_other_chips(x, y)):
            ci = 2 * chip[0] + chip[1]
            cps.append(_remote(cs_ref.at[ci], o_ref.at[k], send_sems.at[3 * a + k], recv_sems.at[3 * a + k],
                               (*chip, c)))
    return cps


def _attn_bwd(ql, ka, va, doa, tq, name, css=()):
    s = ql.shape[1]
    nq = s // tq
    na = len(css)

    def body(*refs):
        q_ref, k_ref, v_ref, do_ref = refs[:4]
        cs_refs = refs[4:4 + na]
        dqo_ref, dk_ref, dv_ref = refs[4 + na:7 + na]
        r_refs = refs[7 + na:7 + 2 * na]
        dq_ref = refs[7 + 2 * na]
        j = pl.program_id(1)
        if na:
            send_sems, recv_sems = refs[8 + 2 * na:]

            @pl.when((pl.program_id(0) == 0) & (j == 0))
            def _():
                for cp in _chip_exchange_copies(cs_refs, r_refs, send_sems, recv_sems):
                    cp.start()

        @pl.when(j == 0)
        def _():
            dq_ref[...] = jnp.zeros_like(dq_ref)
        lane = _lane((tq, LANES))
        kbs = [k_ref[0], k_ref[1]]
        vbs = [v_ref[0], v_ref[1]]

        def step(i, carry, masked):
            ioff = pl.multiple_of(i * tq, tq)
            new = []
            for e in range(2):
                dk, dv = carry[e]
                qb = q_ref[e, pl.ds(ioff, tq), :]
                dob = do_ref[e, pl.ds(ioff, tq), :]
                pr = jnp.exp2(_dot_nt(qb, kbs[e]))
                if masked:
                    rr = lax.broadcasted_iota(jnp.int32, (tq, tq), 0)
                    cc = lax.broadcasted_iota(jnp.int32, (tq, tq), 1)
                    pr = jnp.where(cc <= rr, pr, 0.0)
                ds = (pr * _dot_nt(dob, vbs[e])).astype(BF16)
                dv = dv + _dot_tn(pr.astype(BF16), dob)
                dk = dk + _dot_tn(ds, qb)
                dq_ref[e, pl.ds(ioff, tq), :] += _dot(ds, kbs[e])
                new.append((dk, dv))
            return tuple(new)

        zero = jnp.zeros((tq, LANES), F32)
        carry = step(j, ((zero, zero), (zero, zero)), True)
        carry = lax.fori_loop(j + 1, nq, functools.partial(step, masked=False), carry)
        for e in range(2):
            dk, dv = carry[e]
            col = dk[:, L_NF:L_NF + 1]
            hi = col.astype(BF16).astype(F32)
            dk_ref[e] = jnp.where(lane == L_NF, hi, jnp.where(lane == L_NF + 1, col - hi, dk)).astype(BF16)
            dv_ref[e] = dv.astype(BF16)

        @pl.when(j == nq - 1)
        def _():
            lane_s = _lane((s, LANES))
            for e in range(2):
                dq = dq_ref[e]
                col = dq[:, L_F:L_F + 1]
                hi = col.astype(BF16).astype(F32)
                dqo_ref[e] = jnp.where(lane_s == L_F, hi, jnp.where(lane_s == L_F + 1, col - hi, dq)).astype(BF16)
        if na:
            @pl.when((pl.program_id(0) == NP - 1) & (j == nq - 1))
            def _():
                for cp in _chip_exchange_copies(cs_refs, r_refs, send_sems, recv_sems):
                    cp.wait()

    res = pl.BlockSpec((2, s, LANES), lambda p, j: (p, 0, 0))
    tile = pl.BlockSpec((2, tq, LANES), lambda p, j: (p, j, 0))
    shp = jax.ShapeDtypeStruct((H, s, LANES), BF16)
    outs = pl.pallas_call(
        body, name=name, grid=(NP, nq),
        in_specs=[res, tile, tile, res] + [HBM_SPEC] * na, out_specs=[res, tile, tile] + [HBM_SPEC] * na,
        out_shape=[shp, shp, shp] + [jax.ShapeDtypeStruct((3,) + cs.shape[1:], cs.dtype) for cs in css],
        scratch_shapes=[pltpu.VMEM((2, s, LANES), F32)] + ([pltpu.SemaphoreType.DMA((3 * na,))] * 2 if na else []),
        compiler_params=_cparams(("arbitrary", "arbitrary")),
    )(ql, ka, va, doa, *css)
    return outs[0], outs[1], outs[2], list(outs[3:])


def _gate(att, proj, ts, name):
    s = att.shape[0]

    def body(a_ref, o_ref, g_ref):
        g_ref[...] = (a_ref[...] * _sigmoid(o_ref[...])).astype(BF16)

    return pl.pallas_call(
        body, name=name, grid=(s // ts,),
        in_specs=[pl.BlockSpec((ts, D), lambda i: (i, 0)), pl.BlockSpec((ts, D), lambda i: (i, 3))],
        out_specs=pl.BlockSpec((ts, D), lambda i: (i, 0)),
        out_shape=jax.ShapeDtypeStruct((s, D), BF16),
        compiler_params=_cparams(("arbitrary",)),
    )(att, proj)


def _attn_bwd_prep(dgated, att, proj, ts, name):
    s = att.shape[0]

    def body(dg_ref, a_ref, o_ref, doa_ref, dop_ref):
        lane = _lane((ts, LANES))
        bones = _head_block_ones()
        for p in range(NP):
            sl = slice(p * LANES, (p + 1) * LANES)
            dg, a = dg_ref[:, sl], a_ref[:, sl]
            sig = _sigmoid(o_ref[:, sl])
            datt = dg * sig
            dop_ref[:, sl] = (dg * a * sig * (1.0 - sig)).astype(BF16)
            delta = _exact_dot_r(datt * a, bones)
            for e in range(2):
                de, dl = (datt, delta) if e == 0 else (pltpu.roll(datt, DH, axis=1), pltpu.roll(delta, DH, axis=1))
                h0, h1, h2 = _split3(-dl[:, 0:1])
                aug = jnp.where(lane == L_F, h0.astype(F32),
                                jnp.where(lane == L_F + 1, h1.astype(F32),
                                          jnp.where(lane == L_F + 2, h2.astype(F32), 0.0)))
                doa_ref[2 * p + e] = jnp.where(lane < DH, de, aug).astype(BF16)

    row = pl.BlockSpec((ts, D), lambda i: (i, 0))
    return pl.pallas_call(
        body, name=name, grid=(s // ts,),
        in_specs=[row, row, pl.BlockSpec((ts, D), lambda i: (i, 3))],
        out_specs=[pl.BlockSpec((H, ts, LANES), lambda i: (0, i, 0)), row],
        out_shape=[jax.ShapeDtypeStruct((H, s, LANES), BF16), jax.ShapeDtypeStruct((s, D), BF16)],
        compiler_params=_cparams(("arbitrary",)),
    )(dgated, att, proj)


def _fox_post_bwd(proj, dqa, dka, dva, dop, qg2, kg2, bf, ts, name):
    s = proj.shape[0]
    nt = s // ts

    def body(p_ref, dq_ref, dk_ref, dv_ref, dop_ref, qg_ref, kg_ref, bf_ref, o_ref, red_ref, carry):
        @pl.when(pl.program_id(0) == 0)
        def _():
            carry[...] = jnp.zeros_like(carry)
            red_ref[...] = jnp.zeros_like(red_ref)
        lane = _lane((ts, LANES))
        bones = _head_block_ones()
        d_f = jnp.zeros((ts, LANES), F32)
        dqg = jnp.zeros((1, LANES), F32)
        dkg = jnp.zeros((1, LANES), F32)
        for p in range(NP):
            heads = [[ref[2 * p + e].astype(F32) for e in range(2)] for ref in (dq_ref, dk_ref, dv_ref)]
            pair = [jnp.where(lane < DH, a, pltpu.roll(b, DH, axis=1)) for a, b in heads]
            for e in range(2):
                dqe, dke = heads[0][e], heads[1][e]
                col = (dqe[:, L_F:L_F + 1] + dqe[:, L_F + 1:L_F + 2]
                       - dke[:, L_NF:L_NF + 1] - dke[:, L_NF + 1:L_NF + 2])
                d_f = jnp.where(lane == 2 * p + e, col, d_f)
            for idx, (g_ref, base) in enumerate(((qg_ref, 0), (kg_ref, D))):
                x = p_ref[:, base + p * LANES:base + (p + 1) * LANES]
                _, xh, r = _pair_norm(x, g_ref[...], bones)
                dn = pair[idx] * (SCALE if idx == 0 else 1.0 / LOG2E)
                t = dn * g_ref[...]
                mean_txh = _exact_dot_r(t * xh, bones) * (1.0 / DH)
                dx = r * (t - xh * mean_txh)
                o_ref[:, base + p * LANES:base + (p + 1) * LANES] = dx.astype(BF16)
                gsum = jnp.sum(dn * xh, axis=0, keepdims=True)
                if idx == 0:
                    dqg = dqg + gsum
                else:
                    dkg = dkg + gsum
            o_ref[:, 2 * D + p * LANES:2 * D + (p + 1) * LANES] = pair[2].astype(BF16)
        o_ref[:, 3 * D:4 * D] = dop_ref[...]
        rr = lax.broadcasted_iota(jnp.int32, (ts, ts), 0)
        cc = lax.broadcasted_iota(jnp.int32, (ts, ts), 1)
        utri = (cc >= rr).astype(BF16)
        dlogf = _exact_dot(utri, d_f) + carry[0:1, :]
        carry[0:1, :] = dlogf[0:1, :]
        xf = p_ref[:, 4 * D:4 * D + LANES] + bf_ref[...]
        dfl = jnp.where(lane < H, dlogf * _sigmoid(-xf), 0.0)
        o_ref[:, 4 * D:4 * D + LANES] = dfl.astype(BF16)
        red_ref[0:1, :] += dqg
        red_ref[1:2, :] += dkg
        red_ref[2:3, :] += jnp.sum(dfl, axis=0, keepdims=True)

    hs = pl.BlockSpec((H, ts, LANES), lambda i: (0, nt - 1 - i, 0))
    vec = pl.BlockSpec((1, LANES), lambda i: (0, 0))
    return pl.pallas_call(
        body, name=name, grid=(nt,),
        in_specs=[pl.BlockSpec((ts, FOX_NP), lambda i: (nt - 1 - i, 0)), hs, hs, hs,
                  pl.BlockSpec((ts, D), lambda i: (nt - 1 - i, 0)), vec, vec, vec],
        out_specs=[pl.BlockSpec((ts, FOX_NP), lambda i: (nt - 1 - i, 0)),
                   pl.BlockSpec((8, LANES), lambda i: (0, 0))],
        out_shape=[jax.ShapeDtypeStruct((s, FOX_NP), BF16), jax.ShapeDtypeStruct((8, LANES), F32)],
        scratch_shapes=[pltpu.VMEM((8, LANES), F32)],
        compiler_params=_cparams(("arbitrary",)),
    )(proj, dqa, dka, dva, dop, qg2, kg2, bf)


HALO = 16
TS = 512
TQ = 512
TR = 256
TP = 256
QSPLIT = 1


def _shift_down(x, k):
    return pltpu.roll(x, k, axis=0)


def _shift_up(x, k):
    return pltpu.roll(x, x.shape[0] - k, axis=0)


def _planes(ref):
    return jnp.concatenate([ref[0].astype(F32), ref[1].astype(F32)], axis=1)


def _conv_gate(a, cw, cb, ts, name):
    s = a.shape[1]
    hb = ts // HALO

    def body(prev_ref, a_ref, cw_ref, cb_ref, f_ref):
        i = pl.program_id(0)
        cwv, cbv = _planes(cw_ref), _planes(cb_ref)
        prev = jnp.where(i > 0, _planes(prev_ref), 0.0)
        ext = jnp.concatenate([prev, _planes(a_ref)], axis=0)
        ap = (_shift_down(ext, 2) * cwv[0:1, :] + _shift_down(ext, 1) * cwv[1:2, :]
              + ext * cwv[2:3, :] + cbv)[HALO:, :]
        g, val = ap[:, :GT], ap[:, GT:]
        f_ref[...] = (g * _sigmoid(g) * val).astype(BF16)

    return pl.pallas_call(
        body, name=name, grid=(s // ts, NGT),
        in_specs=[pl.BlockSpec((2, HALO, GT), lambda i, j: (0, jnp.maximum(i * hb - 1, 0), j)),
                  pl.BlockSpec((2, ts, GT), lambda i, j: (0, i, j)),
                  pl.BlockSpec((2, 8, GT), lambda i, j: (0, 0, j)),
                  pl.BlockSpec((2, 1, GT), lambda i, j: (0, 0, j))],
        out_specs=pl.BlockSpec((ts, GT), lambda i, j: (i, j)),
        out_shape=jax.ShapeDtypeStruct((s, DFF), BF16),
        compiler_params=_cparams(("arbitrary", "arbitrary")),
    )(a, a, cw, cb)


def _conv_gate_bwd(a, df, cw, cb, ts, name):
    s = a.shape[1]
    hb = ts // HALO
    nt = s // ts

    def body(prev_ref, a_ref, next_ref, df_ref, dfn_ref, cw_ref, cb_ref, da_ref, red_ref):
        i = pl.program_id(1)

        @pl.when(i == 0)
        def _():
            red_ref[...] = jnp.zeros_like(red_ref)
        cwv, cbv = _planes(cw_ref), _planes(cb_ref)
        prev = jnp.where(i > 0, _planes(prev_ref), 0.0)
        ext = jnp.concatenate([prev, _planes(a_ref), _planes(next_ref)], axis=0)
        dfn = jnp.where(i < nt - 1, dfn_ref[...].astype(F32), 0.0)
        dfe = jnp.concatenate([jnp.zeros((HALO, GT), F32), df_ref[...].astype(F32), dfn], axis=0)
        am2, am1 = _shift_down(ext, 2), _shift_down(ext, 1)
        ap = am2 * cwv[0:1, :] + am1 * cwv[1:2, :] + ext * cwv[2:3, :] + cbv
        g, val = ap[:, :GT], ap[:, GT:]
        sg = _sigmoid(g)
        dap = jnp.concatenate([dfe * val * (sg * (1.0 + g * (1.0 - sg))), dfe * (g * sg)], axis=1)
        da = dap * cwv[2:3, :] + _shift_up(dap, 1) * cwv[1:2, :] + _shift_up(dap, 2) * cwv[0:1, :]
        main = slice(HALO, HALO + ts)
        sums = [jnp.sum((t * dap)[main], axis=0, keepdims=True) for t in (am2, am1, ext)]
        sums.append(jnp.sum(dap[main], axis=0, keepdims=True))
        for e in range(2):
            cols = slice(e * GT, (e + 1) * GT)
            da_ref[e] = da[main, cols].astype(BF16)
            for r, sm in enumerate(sums):
                red_ref[e, r:r + 1, :] += sm[:, cols]

    nhb = s // HALO
    return pl.pallas_call(
        body, name=name, grid=(NGT, nt),
        in_specs=[pl.BlockSpec((2, HALO, GT), lambda j, i: (0, jnp.maximum(i * hb - 1, 0), j)),
                  pl.BlockSpec((2, ts, GT), lambda j, i: (0, i, j)),
                  pl.BlockSpec((2, HALO, GT), lambda j, i: (0, jnp.minimum((i + 1) * hb, nhb - 1), j)),
                  pl.BlockSpec((ts, GT), lambda j, i: (i, j)),
                  pl.BlockSpec((HALO, GT), lambda j, i: (jnp.minimum((i + 1) * hb, nhb - 1), j)),
                  pl.BlockSpec((2, 8, GT), lambda j, i: (0, 0, j)),
                  pl.BlockSpec((2, 1, GT), lambda j, i: (0, 0, j))],
        out_specs=[pl.BlockSpec((2, ts, GT), lambda j, i: (0, i, j)),
                   pl.BlockSpec((2, 8, GT), lambda j, i: (0, 0, j))],
        out_shape=[jax.ShapeDtypeStruct((2, s, DFF), BF16), jax.ShapeDtypeStruct((2, 8, DFF), F32)],
        compiler_params=_cparams(("arbitrary", "arbitrary")),
    )(a, a, a, df, df, cw, cb)


def _chunk_mask(transposed=False):
    t = lax.broadcasted_iota(jnp.int32, (SGB, SGB), 0) // CHUNK
    u = lax.broadcasted_iota(jnp.int32, (SGB, SGB), 1) // CHUNK
    return (t <= u) if transposed else (u <= t)


def _sgu_ln(v, gain, bias):
    mu = jnp.mean(v, axis=-1, keepdims=True)
    vc = v - mu
    rstd = lax.rsqrt(jnp.mean(vc * vc, axis=-1, keepdims=True) + EPS)
    vhat = vc * rstd
    return vhat * gain + bias, vhat, rstd


def _sgu_fwd(z, vgain, vbias, ws, bst, tr, name):
    s = z.shape[0]

    def body(zu_ref, zv_ref, vg_ref, vb_ref, ws_ref, bs_ref, y_ref):
        u = _gelu(zu_ref[...].astype(F32))
        vn, _, _ = _sgu_ln(_gelu(zv_ref[...].astype(F32)), vg_ref[...], vb_ref[...])
        vn = vn.astype(BF16)
        mask = _chunk_mask()
        for g in range(SGG):
            w = jnp.where(mask, ws_ref[g], 0.0).astype(BF16)
            for b in range(tr // SGB):
                rs, cs = slice(b * SGB, (b + 1) * SGB), slice(g * SGC, (g + 1) * SGC)
                mixed = _dot(w, vn[rs, cs]) + bs_ref[:, g:g + 1]
                y_ref[rs, cs] = (u[rs, cs] * mixed).astype(BF16)

    vec = pl.BlockSpec((1, SGW), lambda i: (0, 0))
    return pl.pallas_call(
        body, name=name, grid=(s // tr,),
        in_specs=[pl.BlockSpec((tr, SGW), lambda i: (i, 0)), pl.BlockSpec((tr, SGW), lambda i: (i, 1)),
                  vec, vec, pl.BlockSpec((SGG, SGB, SGB), lambda i: (0, 0, 0)),
                  pl.BlockSpec((SGB, LANES), lambda i: (0, 0))],
        out_specs=pl.BlockSpec((tr, SGW), lambda i: (i, 0)),
        out_shape=jax.ShapeDtypeStruct((s, SGW), BF16),
        compiler_params=_cparams(("arbitrary",)),
    )(z, z, vgain, vbias, ws, bst)


def _sgu_bwd(z, dy, vgain, vbias, ws, wst, bst, tr, name):
    s = z.shape[0]

    def body(zu_ref, zv_ref, dy_ref, vg_ref, vb_ref, ws_ref, wst_ref, bs_ref,
             dz_ref, rb_ref, rv_ref, dws_ref, dbs_ref, dvn_s):
        @pl.when(pl.program_id(0) == 0)
        def _():
            rb_ref[...] = jnp.zeros_like(rb_ref)
            rv_ref[...] = jnp.zeros_like(rv_ref)
            dws_ref[...] = jnp.zeros_like(dws_ref)
            dbs_ref[...] = jnp.zeros_like(dbs_ref)
        zu = zu_ref[...].astype(F32)
        zv = zv_ref[...].astype(F32)
        u, gu = _gelu_and_grad(zu)
        v, gv = _gelu_and_grad(zv)
        vn, vhat, rstd = _sgu_ln(v, vg_ref[...], vb_ref[...])
        vnb = vn.astype(BF16)
        dyv = dy_ref[...].astype(F32)
        dmix = (dyv * u).astype(BF16)
        mask = _chunk_mask()
        mask_t = _chunk_mask(transposed=True)
        lane = _lane((SGB, LANES))
        dbs = jnp.zeros((SGB, LANES), F32)
        for g in range(SGG):
            w = jnp.where(mask, ws_ref[g], 0.0).astype(BF16)
            wt = jnp.where(mask_t, wst_ref[g], 0.0).astype(BF16)
            dw = jnp.zeros((SGB, SGB), F32)
            for b in range(tr // SGB):
                rs, cs = slice(b * SGB, (b + 1) * SGB), slice(g * SGC, (g + 1) * SGC)
                mixed = _dot(w, vnb[rs, cs]) + bs_ref[:, g:g + 1]
                dz_ref[rs, cs] = (dyv[rs, cs] * mixed * gu[rs, cs]).astype(BF16)
                dm = dmix[rs, cs]
                dw = dw + _dot_nt(dm, vnb[rs, cs])
                dbs = dbs + jnp.where(lane == g, jnp.sum(dm.astype(F32), axis=-1, keepdims=True), 0.0)
                dvn_s[rs, cs] = _dot(wt, dm)
            dws_ref[g] += jnp.where(mask, dw, 0.0)
        dbs_ref[...] += dbs
        dvn = dvn_s[...]
        rv_ref[0:1, :] += jnp.sum(dvn * vhat, axis=0, keepdims=True)
        rv_ref[1:2, :] += jnp.sum(dvn, axis=0, keepdims=True)
        dvh = dvn * vg_ref[...]
        dv = rstd * (dvh - jnp.mean(dvh, axis=-1, keepdims=True)
                     - vhat * jnp.mean(dvh * vhat, axis=-1, keepdims=True))
        dz_ref[:, SGW:] = (dv * gv).astype(BF16)
        dzf = dz_ref[...].astype(F32)
        rb_ref[0:1, :] += jnp.sum(dzf, axis=0, keepdims=True)

    vec = pl.BlockSpec((1, SGW), lambda i: (0, 0))
    wsp = pl.BlockSpec((SGG, SGB, SGB), lambda i: (0, 0, 0))
    return pl.pallas_call(
        body, name=name, grid=(s // tr,),
        in_specs=[pl.BlockSpec((tr, SGW), lambda i: (i, 0)), pl.BlockSpec((tr, SGW), lambda i: (i, 1)),
                  pl.BlockSpec((tr, SGW), lambda i: (i, 0)), vec, vec, wsp, wsp,
                  pl.BlockSpec((SGB, LANES), lambda i: (0, 0))],
        out_specs=[pl.BlockSpec((tr, 2 * SGW), lambda i: (i, 0)),
                   pl.BlockSpec((8, 2 * SGW), lambda i: (0, 0)),
                   pl.BlockSpec((8, SGW), lambda i: (0, 0)), wsp,
                   pl.BlockSpec((SGB, LANES), lambda i: (0, 0))],
        out_shape=[jax.ShapeDtypeStruct((s, 2 * SGW), BF16), jax.ShapeDtypeStruct((8, 2 * SGW), F32),
                   jax.ShapeDtypeStruct((8, SGW), F32), jax.ShapeDtypeStruct((SGG, SGB, SGB), F32),
                   jax.ShapeDtypeStruct((SGB, LANES), F32)],
        scratch_shapes=[pltpu.VMEM((tr, SGW), F32)],
        compiler_params=_cparams(("arbitrary",)),
    )(z, z, dy, vgain, vbias, ws, wst, bst)


def _final_loss(x, fg, tgt, gprev, yprev, ts, name):
    s, d = x.shape

    def body(x_ref, fg_ref, t_ref, g_ref, y_ref, l_ref, dx_ref, dy_ref, red_ref):
        @pl.when(pl.program_id(0) == 0)
        def _():
            l_ref[...] = jnp.zeros_like(l_ref)
            red_ref[...] = jnp.zeros_like(red_ref)
        xv = x_ref[...]
        r = _rstd_rows(xv)
        xh = xv * r
        err = xh * fg_ref[...] - t_ref[...]
        l_ref[...] += 0.5 * jnp.sum(jnp.mean(err * err, axis=-1, keepdims=True))
        dyo = err * (1.0 / d)
        dxh = dyo * fg_ref[...]
        dx = r * (dxh - xh * jnp.mean(dxh * xh, axis=-1, keepdims=True))
        dx_ref[...] = dx
        dy_ref[...] = (dx * g_ref[...]).astype(BF16)
        red_ref[0:1, :] += jnp.sum(dyo * xh, axis=0, keepdims=True)
        red_ref[1:2, :] += jnp.sum(dx * y_ref[...].astype(F32), axis=0, keepdims=True)

    row = pl.BlockSpec((ts, d), lambda i: (i, 0))
    vec = pl.BlockSpec((1, d), lambda i: (0, 0))
    return pl.pallas_call(
        body, name=name, grid=(s // ts,),
        in_specs=[row, vec, row, vec, row],
        out_specs=[pl.BlockSpec((8, LANES), lambda i: (0, 0)), row, row, pl.BlockSpec((8, d), lambda i: (0, 0))],
        out_shape=[jax.ShapeDtypeStruct((8, LANES), F32), jax.ShapeDtypeStruct((s, d), F32),
                   jax.ShapeDtypeStruct((s, d), BF16), jax.ShapeDtypeStruct((8, d), F32)],
        compiler_params=_cparams(("arbitrary",)),
    )(x, fg, tgt, gprev, yprev)


def _norm_bwd(xin, dh, dxout, ng, sc, gprev, yprev, ts, name):
    s, d = xin.shape
    has_prev = gprev is not None

    def body(*refs):
        if has_prev:
            x_ref, dh_ref, dxo_ref, ng_ref, sc_ref, g_ref, y_ref, dx_ref, dy_ref, red_ref = refs
        else:
            x_ref, dh_ref, dxo_ref, ng_ref, sc_ref, dx_ref, red_ref = refs

        @pl.when(pl.program_id(0) == 0)
        def _():
            red_ref[...] = jnp.zeros_like(red_ref)
        xv = x_ref[...]
        r = _rstd_rows(xv)
        xh = xv * r
        dhv = dh_ref[...]
        dr = dhv * (1.0 + sc_ref[...])
        t = dr * ng_ref[...]
        dx = dxo_ref[...] + r * (t - xh * jnp.mean(t * xh, axis=-1, keepdims=True))
        dx_ref[...] = dx
        red_ref[0:1, :] += jnp.sum(dhv, axis=0, keepdims=True)
        red_ref[1:2, :] += jnp.sum(dhv * (xh * ng_ref[...]), axis=0, keepdims=True)
        red_ref[2:3, :] += jnp.sum(dr * xh, axis=0, keepdims=True)
        if has_prev:
            dy_ref[...] = (dx * g_ref[...]).astype(BF16)
            red_ref[3:4, :] += jnp.sum(dx * y_ref[...].astype(F32), axis=0, keepdims=True)

    row = pl.BlockSpec((ts, d), lambda i: (i, 0))
    vec = pl.BlockSpec((1, d), lambda i: (0, 0))
    red = pl.BlockSpec((8, d), lambda i: (0, 0))
    if has_prev:
        in_specs, args = [row, row, row, vec, vec, vec, row], (xin, dh, dxout, ng, sc, gprev, yprev)
        out_specs = [row, row, red]
        out_shape = [jax.ShapeDtypeStruct((s, d), F32), jax.ShapeDtypeStruct((s, d), BF16),
                     jax.ShapeDtypeStruct((8, d), F32)]
    else:
        in_specs, args = [row, row, row, vec, vec], (xin, dh, dxout, ng, sc)
        out_specs = [row, red]
        out_shape = [jax.ShapeDtypeStruct((s, d), F32), jax.ShapeDtypeStruct((8, d), F32)]
    return pl.pallas_call(
        body, name=name, grid=(s // ts,), in_specs=in_specs, out_specs=out_specs, out_shape=out_shape,
        compiler_params=_cparams(("arbitrary",)),
    )(*args)


def _ada_mod(c_all, ada_w, ada_b):
    nb = c_all.shape[0]
    da = ada_w.shape[2]

    def body(c_ref, w_ref, b_ref, o_ref, ca_ref):
        cv = c_ref[...]
        ca = cv * _sigmoid(cv)
        ca_ref[...] = ca
        o_ref[0] = lax.dot_general(ca, w_ref[0], (((1,), (0,)), ((), ())), precision=lax.Precision.HIGHEST,
                                   preferred_element_type=F32) + b_ref[0]

    return pl.pallas_call(
        body, name="ada_mod", grid=(2,),
        in_specs=[pl.BlockSpec((nb, D), lambda i: (0, 0)), pl.BlockSpec((1, D, da), lambda i: (i, 0, 0)),
                  pl.BlockSpec((1, 1, da), lambda i: (i, 0, 0))],
        out_specs=[pl.BlockSpec((1, nb, da), lambda i: (i, 0, 0)), pl.BlockSpec((nb, D), lambda i: (0, 0))],
        out_shape=[jax.ShapeDtypeStruct((2, nb, da), F32), jax.ShapeDtypeStruct((nb, D), F32)],
        compiler_params=_cparams(("arbitrary",)),
    )(c_all, ada_w, ada_b)


def _ada_w_grad(c_act_t, dmod):
    nb = c_act_t.shape[1]
    da = dmod.shape[2]
    tn = 512

    def body(c_ref, d_ref, o_ref):
        acc = c_ref[:, 0:1] * d_ref[0, 0:1, :]
        for b in range(1, nb):
            acc = acc + c_ref[:, b:b + 1] * d_ref[0, b:b + 1, :]
        o_ref[0] = acc

    return pl.pallas_call(
        body, name="ada_w_grad", grid=(2, da // tn),
        in_specs=[pl.BlockSpec((D, nb), lambda i, j: (0, 0)), pl.BlockSpec((1, nb, tn), lambda i, j: (i, 0, j))],
        out_specs=pl.BlockSpec((1, D, tn), lambda i, j: (i, 0, j)),
        out_shape=jax.ShapeDtypeStruct((2, D, da), F32),
        compiler_params=_cparams(("arbitrary", "arbitrary")),
    )(c_act_t, dmod)


def _conv_planes(cw, cb):
    cwp = jnp.swapaxes(cw.reshape(3, 2, DFF), 0, 1)
    return jnp.pad(cwp, ((0, 0), (0, 5), (0, 0))), cb.reshape(2, 1, DFF)


def _local_step(x, tgt, mod, wts, small, comm=None):
    wts = dict(wts)
    s = x.shape[0]
    ts, tq, tr, tp = TS, TQ, TR, TP
    zb = lambda n: jnp.zeros((1, n), F32)
    m6 = mod.reshape(2, 6, 1, D)
    sh1, sc1, g1, sh2, sc2, g2 = ([m6[i, k] for i in range(2)] for k in range(6))
    n1g, n2g = small["norm1_g"], small["norm2_g"]
    row = lambda a, i: a[i:i + 1]

    qg2 = jnp.tile(small["fox_q_gain"], (1, 2))
    kg2 = jnp.tile(small["fox_k_gain"], (1, 2))
    bfp = jnp.pad(small["fox_b_f"], ((0, 0), (0, LANES - H)))
    proj, h1 = _norm_mod_matmul(x, row(n1g, 0), sc1[0], sh1[0], wts["fox_w_in"], zb(FOX_NP), F32, ts, 1408, "fox_in")
    qa, ka, va = _fox_post(proj, qg2, kg2, bfp, tp, "fox_post")
    att, ql, gathered = _attn_fwd(qa, ka, va, tq, "attn_fwd", shards=comm["shards"] if comm else ())
    if comm:
        wts.update(comm["make_wts"](gathered))
    gated = _gate(att, proj, ts, "fox_gate")
    x1, y0 = _matmul_residual(gated, wts["fox_w_out"], x, g1[0], ts, "fox_out")

    def ffn_fwd(xin, i, tag):
        cw, cb = _conv_planes(small["ffn_conv_w"][i], small["ffn_conv_b"][i])
        a, h = _norm_mod_matmul(xin, row(n2g, i), sc2[i], sh2[i], wts["ffn_w_up"][i], zb(2 * DFF), BF16, ts, 1408,
                                "ffn_up" + tag, planes=2)
        f = _conv_gate(a, cw, cb, min(2 * ts, s), "ffn_conv" + tag)
        xo, y = _matmul_residual(f, wts["ffn_w_down"][i], xin, g2[i], ts, "ffn_down" + tag)
        return xo, (a, h, f, y, cw, cb)

    x2, ffn0 = ffn_fwd(x1, 0, "0")

    bst = jnp.pad(small["sgu_b_s"].T, ((0, 0), (0, LANES - SGG)))
    ws = small["sgu_w_s"]
    z, h3 = _norm_mod_matmul(x2, row(n1g, 1), sc1[1], sh1[1], wts["sgu_w_in"], small["sgu_b_in"], BF16, ts, 1024,
                             "sgu_in")
    yy = _sgu_fwd(z, small["sgu_v_gain"], small["sgu_v_bias"], ws, bst, tr, "sgu_mix")
    x3, y1 = _matmul_residual(yy, wts["sgu_w_out"], x2, g1[1], ts, "sgu_out")
    x4, ffn1 = ffn_fwd(x3, 1, "1")

    lsum, dx4, dy, redf = _final_loss(x4, small["final_g"], tgt, g2[1], ffn1[3], ts, "final_loss")
    grads = {"final_g": redf[0]}
    dmod = [[None] * 6, [None] * 6]
    dmod[1][5] = redf[1]

    def ffn_bwd(dxo, dy2, xin, i, saved, gprev, yprev, tag):
        a, h, f, _, cw, cb = saved
        wd, wu = wts["ffn_w_down"][i], wts["ffn_w_up"][i]
        g_wd = _matmul(f, dy2, True, False, 1408, D, ts, BF16, "ffn_dwdown" + tag)
        df = _matmul_wt(dy2, wd, 1408, D, BF16, ts, "ffn_df" + tag)
        da, redc = _conv_gate_bwd(a, df, cw, cb, ts, "ffn_conv_bwd" + tag)
        g_wu = _matmul(h, da, True, False, D, 1408, ts, BF16, "ffn_dwup" + tag, out_parts=N_CHIP)
        dh = _matmul_wt(da, wu, D, 1408, F32, ts, "ffn_dh" + tag)
        outs = _norm_bwd(xin, dh, dxo, row(n2g, i), sc2[i], gprev, yprev, ts, "ffn_norm_bwd" + tag)
        return outs, g_wd, g_wu, redc

    (dx3, dy1, red), g_wd1, g_wu1, redc1 = ffn_bwd(dx4, dy, x3, 1, ffn1, g1[1], y1, "1")
    dmod[1][3], dmod[1][4], dn2g1, dmod[1][2] = red[0], red[1], red[2], red[3]

    g_swo = _matmul(yy, dy1, True, False, 1024, D, ts, BF16, "sgu_dwout")
    dyy = _matmul_wt(dy1, wts["sgu_w_out"], 1024, D, BF16, ts, "sgu_dyy")
    wst = jnp.swapaxes(ws, 1, 2)
    dz, rb, rv, dws, dbst = _sgu_bwd(z, dyy, small["sgu_v_gain"], small["sgu_v_bias"], ws, wst, bst, tr, "sgu_mix_bwd")
    g_swi = _matmul(h3, dz, True, False, D, 1024, ts, BF16, "sgu_dwin", out_parts=N_CHIP)
    dh3 = _matmul_wt(dz, wts["sgu_w_in"], D, 1024, F32, ts, "sgu_dh")
    dx2, dy2_0, red = _norm_bwd(x2, dh3, dx3, row(n1g, 1), sc1[1], g2[0], ffn0[3], ts, "sgu_norm_bwd")
    dmod[1][0], dmod[1][1], dn1g1, dmod[0][5] = red[0], red[1], red[2], red[3]

    (dx1, dy0, red), g_wd0, g_wu0, redc0 = ffn_bwd(dx2, dy2_0, x1, 0, ffn0, g1[0], y0, "0")
    dmod[0][3], dmod[0][4], dn2g0, dmod[0][2] = red[0], red[1], red[2], red[3]

    g_fwo = _matmul(gated, dy0, True, False, D, D, ts, BF16, "fox_dwout")
    dgated = _matmul_wt(dy0, wts["fox_w_out"], D, D, F32, ts, "fox_dgated")
    doa, dop = _attn_bwd_prep(dgated, att, proj, ts, "attn_bwd_prep")
    css = comm["rs_prepare"]([g_fwo, g_swi, g_swo, g_wu0, g_wu1, g_wd0, g_wd1]) if comm else []
    dqa, dka, dva, rcvs = _attn_bwd(ql, ka, va, doa, tq, "attn_bwd", css=css)
    dproj, redx = _fox_post_bwd(proj, dqa, dka, dva, dop, qg2, kg2, bfp, tp, "fox_post_bwd")
    g_fwi = _matmul(h1, dproj, True, False, D, 1408, ts, BF16, "fox_dwin")
    dh1 = _matmul_wt(dproj, wts["fox_w_in"], D, 1408, F32, ts, "fox_dh")
    dx0, red = _norm_bwd(x, dh1, dx1, row(n1g, 0), sc1[0], None, None, ts, "fox_norm_bwd")
    dmod[0][0], dmod[0][1], dn1g0 = red[0], red[1], red[2]

    grads.update(
        fox_w_in=g_fwi, fox_w_out=g_fwo, sgu_w_in=g_swi, sgu_w_out=g_swo,
        ffn_w_up=[g_wu0, g_wu1], ffn_w_down=[g_wd0, g_wd1],
        fox_q_gain=redx[0, :DH] + redx[0, DH:], fox_k_gain=redx[1, :DH] + redx[1, DH:], fox_b_f=redx[2, :H],
        sgu_b_in=rb[0], sgu_v_gain=rv[0], sgu_v_bias=rv[1], sgu_w_s=dws, sgu_b_s=dbst[:, :SGG].T,
        ffn_conv_w=jnp.stack([jnp.swapaxes(r[:, 0:3], 0, 1).reshape(3, 2 * DFF) for r in (redc0, redc1)]),
        ffn_conv_b=jnp.stack([r[:, 3].reshape(2 * DFF) for r in (redc0, redc1)]),
        norm1_g=jnp.stack([dn1g0, dn1g1]), norm2_g=jnp.stack([dn2g0, dn2g1]),
    )
    dmod_arr = jnp.stack([jnp.concatenate(dmod[0]), jnp.concatenate(dmod[1])])
    return lsum[0, 0], dx0, grads, dmod_arr, (css, rcvs)


N_DEV = 8
N_CHIP = 4
HBM_SPEC = pl.BlockSpec(memory_space=pltpu.HBM)
VMEM_SPEC = pl.BlockSpec(memory_space=pltpu.VMEM)


def _mesh_pos():
    return lax.axis_index("x"), lax.axis_index("y"), lax.axis_index("c")


def _other_chips(x, y):
    return [(1 - x, y), (x, 1 - y), (1 - x, 1 - y)]


def _remote(src, dst, ssem, rsem, dev):
    return pltpu.make_async_remote_copy(src_ref=src, dst_ref=dst, send_sem=ssem, recv_sem=rsem,
                                        device_id=dev, device_id_type=MESH)


def _allgather8(xb, name):
    m_per, n = xb.shape

    def body(x_ref, out_ref, send_sems, recv_sems, local_sem):
        x, y, c = _mesh_pos()
        me, sibling = (x, y, c), (x, y, 1 - c)
        chips = _other_chips(x, y)

        def rows(px, py, pc):
            return out_ref.at[pl.ds((4 * px + 2 * py + pc) * m_per, m_per), :]

        def copy(k, block, to, src=None):
            return _remote(rows(*block) if src is None else src, rows(*block),
                           send_sems.at[k], recv_sems.at[k], to)

        mine = pltpu.make_async_copy(x_ref, rows(*me), local_sem)
        mine.start()
        first = [copy(0, me, sibling, src=x_ref)]
        first += [copy(1 + j, me, (*chip, c), src=x_ref) for j, chip in enumerate(chips)]
        for cp in first:
            cp.start()
        passed = [copy(4 + j, (*chip, c), sibling) for j, chip in enumerate(chips)]
        for j, chip in enumerate(chips):
            copy(1 + j, (*chip, c), me).wait_recv()
            passed[j].start()
        copy(0, sibling, me).wait_recv()
        for j, chip in enumerate(chips):
            copy(4 + j, (*chip, 1 - c), me).wait_recv()
        for cp in first + passed:
            cp.wait_send()
        mine.wait()

    return pl.pallas_call(
        body, name=name,
        out_shape=jax.ShapeDtypeStruct((N_DEV * m_per, n), xb.dtype),
        in_specs=[VMEM_SPEC], out_specs=VMEM_SPEC,
        scratch_shapes=[pltpu.SemaphoreType.DMA((7,)), pltpu.SemaphoreType.DMA((7,)), pltpu.SemaphoreType.DMA],
        compiler_params=pltpu.CompilerParams(vmem_limit_bytes=V7X_VMEM_LIMIT),
    )(xb)


def _gather_shards(shards, name):
    na = len(shards)

    def body(*refs):
        p_refs, o_refs = refs[:na], refs[na:2 * na]
        send_sems, recv_sems, pass_send, pass_recv = refs[2 * na:]
        x, y, c = _mesh_pos()
        me = 2 * x + y
        sibling = (x, y, 1 - c)
        chips = _other_chips(x, y)

        def half(a, ci, hf):
            rh = shards[a].shape[0] // 2
            return o_refs[a].at[ci, pl.ds(hf * rh, rh), :]

        sends = []
        for a in range(na):
            rh = shards[a].shape[0] // 2
            for k, chip in enumerate(chips):
                sends.append(_remote(p_refs[a].at[pl.ds(c * rh, rh), :], half(a, me, c),
                                     send_sems.at[3 * a + k], recv_sems.at[3 * a + k], (*chip, c)))
        for cp in sends:
            cp.start()
        passed = []
        for a in range(na):
            for k, chip in enumerate(chips):
                ci = 2 * chip[0] + chip[1]
                _remote(half(a, ci, c), half(a, ci, c), send_sems.at[3 * a + k], recv_sems.at[3 * a + k],
                        (*chip, c)).wait_recv()
                cp = _remote(half(a, ci, c), half(a, ci, c), pass_send.at[3 * a + k], pass_recv.at[3 * a + k], sibling)
                cp.start()
                passed.append(cp)
        for a in range(na):
            for k, chip in enumerate(chips):
                ci = 2 * chip[0] + chip[1]
                _remote(half(a, ci, 1 - c), half(a, ci, 1 - c), pass_send.at[3 * a + k], pass_recv.at[3 * a + k],
                        sibling).wait_recv()
        for cp in sends + passed:
            cp.wait_send()

    return pl.pallas_call(
        body, name=name,
        out_shape=[jax.ShapeDtypeStruct((N_CHIP,) + p.shape, p.dtype) for p in shards],
        in_specs=[HBM_SPEC] * na, out_specs=[HBM_SPEC] * na,
        scratch_shapes=[pltpu.SemaphoreType.DMA((3 * na,))] * 4,
    )(*shards)


def _rs_to_sibling(gs, name):
    na = len(gs)

    def body(*refs):
        g_refs, o_refs, ssems, rsems = refs[:na], refs[na:2 * na], refs[2 * na], refs[2 * na + 1]
        x, y, c = _mesh_pos()
        cps = []
        for a in range(na):
            rh = gs[a].shape[1] // 2
            cp = _remote(g_refs[a].at[:, pl.ds((1 - c) * rh, rh), :], o_refs[a], ssems.at[a], rsems.at[a],
                         (x, y, 1 - c))
            cp.start()
            cps.append(cp)
        for cp in cps:
            cp.wait()

    return pl.pallas_call(
        body, name=name,
        out_shape=[jax.ShapeDtypeStruct((g.shape[0], g.shape[1] // 2, g.shape[2]), g.dtype) for g in gs],
        in_specs=[HBM_SPEC] * na, out_specs=[HBM_SPEC] * na,
        scratch_shapes=[pltpu.SemaphoreType.DMA((na,)), pltpu.SemaphoreType.DMA((na,))],
    )(*gs)


def _rs_chip_sum(g, sib, c_arr, tr, name):
    nc, r, n = g.shape
    rh = r // 2
    g4 = g.reshape(nc, 2, rh, n)

    def body(c_ref, g_ref, s_ref, o_ref):
        o_ref[...] = (g_ref[0].astype(F32) + s_ref[...].astype(F32)).astype(BF16)

    return pl.pallas_call(
        body, name=name, out_shape=jax.ShapeDtypeStruct((nc, rh, n), BF16),
        grid_spec=pltpu.PrefetchScalarGridSpec(
            num_scalar_prefetch=1, grid=(nc, rh // tr),
            in_specs=[pl.BlockSpec((1, 1, tr, n), lambda j, i, cr: (j, cr[0], i, 0)),
                      pl.BlockSpec((1, tr, n), lambda j, i, cr: (j, i, 0))],
            out_specs=pl.BlockSpec((1, tr, n), lambda j, i, cr: (j, i, 0))),
        compiler_params=_cparams(("arbitrary", "arbitrary")),
    )(c_arr, g4, sib)


def _rs_across_chips(css, name):
    na = len(css)

    def body(*refs):
        cs_refs, o_refs, send_sems, recv_sems = refs[:na], refs[na:2 * na], refs[2 * na], refs[2 * na + 1]
        x, y, c = _mesh_pos()
        cps = []
        for a in range(na):
            for k, chip in enumerate(_other_chips(x, y)):
                ci = 2 * chip[0] + chip[1]
                cp = _remote(cs_refs[a].at[ci], o_refs[a].at[k], send_sems.at[3 * a + k], recv_sems.at[3 * a + k],
                             (*chip, c))
                cp.start()
                cps.append(cp)
        for cp in cps:
            cp.wait()

    return pl.pallas_call(
        body, name=name, out_shape=[jax.ShapeDtypeStruct((3,) + cs.shape[1:], cs.dtype) for cs in css],
        in_specs=[HBM_SPEC] * na, out_specs=[HBM_SPEC] * na,
        scratch_shapes=[pltpu.SemaphoreType.DMA((3 * na,)), pltpu.SemaphoreType.DMA((3 * na,))],
    )(*css)


def _rs_final_sum(cs, rcv, me_arr, tr, name):
    nc, rh, n = cs.shape

    def body(m_ref, c_ref, r_ref, o_ref):
        acc = c_ref[0].astype(F32)
        for k in range(3):
            acc = acc + r_ref[k].astype(F32)
        o_ref[...] = acc

    return pl.pallas_call(
        body, name=name, out_shape=jax.ShapeDtypeStruct((rh, n), F32),
        grid_spec=pltpu.PrefetchScalarGridSpec(
            num_scalar_prefetch=1, grid=(rh // tr,),
            in_specs=[pl.BlockSpec((1, tr, n), lambda i, mr: (mr[0], i, 0)),
                      pl.BlockSpec((3, tr, n), lambda i, mr: (0, i, 0))],
            out_specs=pl.BlockSpec((tr, n), lambda i, mr: (i, 0))),
        compiler_params=_cparams(("arbitrary",)),
    )(me_arr, cs, rcv)


def _rs_swap_halves(halves, name):
    na = len(halves)

    def body(*refs):
        h_refs, o_refs, ssems, rsems = refs[:na], refs[na:2 * na], refs[2 * na], refs[2 * na + 1]
        x, y, c = _mesh_pos()
        cps = []
        for a in range(na):
            cp = _remote(h_refs[a], o_refs[a], ssems.at[a], rsems.at[a], (x, y, 1 - c))
            cp.start()
            cps.append(cp)
        for cp in cps:
            cp.wait()

    return pl.pallas_call(
        body, name=name, out_shape=[jax.ShapeDtypeStruct(h.shape, h.dtype) for h in halves],
        in_specs=[HBM_SPEC] * na, out_specs=[HBM_SPEC] * na,
        scratch_shapes=[pltpu.SemaphoreType.DMA((na,)), pltpu.SemaphoreType.DMA((na,))],
    )(*halves)


def _join_columns(parts, n_out, name):
    p, k, c = parts.shape
    tr = 128

    def body(w_ref, o_ref):
        for j in range(p):
            o_ref[:, j * c:(j + 1) * c] = w_ref[j]
        o_ref[:, p * c:] = jnp.zeros((tr, n_out - p * c), parts.dtype)

    return pl.pallas_call(
        body, name=name, grid=(k // tr,),
        in_specs=[pl.BlockSpec((p, tr, c), lambda i: (0, i, 0))],
        out_specs=pl.BlockSpec((tr, n_out), lambda i: (i, 0)),
        out_shape=jax.ShapeDtypeStruct((k, n_out), parts.dtype),
        compiler_params=_cparams(("arbitrary",)),
    )(parts)


def _split_columns(g, p, c, name):
    k, n = g.shape
    tr = 128

    def body(g_ref, o_ref):
        for j in range(p):
            o_ref[j] = g_ref[:, j * c:(j + 1) * c]

    return pl.pallas_call(
        body, name=name, grid=(k // tr,),
        in_specs=[pl.BlockSpec((tr, n), lambda i: (i, 0))],
        out_specs=pl.BlockSpec((p, tr, c), lambda i: (0, i, 0)),
        out_shape=jax.ShapeDtypeStruct((p, k, c), g.dtype),
        compiler_params=_cparams(("arbitrary",)),
    )(g)


def _sum8(g, name):
    nd, r, n = g.shape

    def body(g_ref, o_ref):
        acc = g_ref[0]
        for k in range(1, nd):
            acc = acc + g_ref[k]
        o_ref[...] = acc

    return pl.pallas_call(
        body, name=name, grid=(r // 8,),
        in_specs=[pl.BlockSpec((nd, 8, n), lambda i: (0, i, 0))],
        out_specs=pl.BlockSpec((8, n), lambda i: (i, 0)),
        out_shape=jax.ShapeDtypeStruct((r, n), F32),
        compiler_params=_cparams(("arbitrary",)),
    )(g)


def _adamw(w, g, m, v, name):
    r, n = w.shape
    tr = 128 if r % 128 == 0 else 8
    bc1 = 1.0 - ADAM_B1 ** ADAM_STEP
    bc2 = 1.0 - ADAM_B2 ** ADAM_STEP

    def body(w_ref, g_ref, m_ref, v_ref, d_ref, mo_ref, vo_ref):
        gv = g_ref[...]
        mn = ADAM_B1 * m_ref[...] + (1.0 - ADAM_B1) * gv
        vn = ADAM_B2 * v_ref[...] + (1.0 - ADAM_B2) * (gv * gv)
        d_ref[...] = -ADAM_LR * ((mn / bc1) / (jnp.sqrt(vn / bc2) + ADAM_EPS) + ADAM_WD * w_ref[...])
        mo_ref[...] = mn
        vo_ref[...] = vn

    blk = pl.BlockSpec((tr, n), lambda i: (i, 0))
    shp = jax.ShapeDtypeStruct((r, n), F32)
    return pl.pallas_call(
        body, name=name, grid=(r // tr,), in_specs=[blk] * 4, out_specs=[blk] * 3, out_shape=[shp] * 3,
        compiler_params=_cparams(("arbitrary",)),
    )(w, g, m, v)


ROW = 1024
PACK_ROWS = 7168
BIG = ("fox_w_in", "fox_w_out", "sgu_w_in", "sgu_w_out", "ffn_w_up", "ffn_w_down")
SMALL_SHARDED = ("sgu_b_in", "sgu_v_gain", "sgu_v_bias", "ffn_conv_w")
SMALL_REPL = ("fox_b_f", "fox_q_gain", "fox_k_gain", "sgu_w_s", "sgu_b_s", "ffn_conv_b", "ada_b",
              "norm1_g", "norm2_g", "final_g")
WEIGHTS = ("fox_w_in", "fox_b_f", "fox_q_gain", "fox_k_gain", "fox_w_out", "sgu_w_in", "sgu_b_in", "sgu_v_gain",
           "sgu_v_bias", "sgu_w_s", "sgu_b_s", "sgu_w_out", "ffn_w_up", "ffn_conv_w", "ffn_conv_b", "ffn_w_down",
           "ada_w", "ada_b", "norm1_g", "norm2_g", "final_g")


def _rows_of(a, mult=1):
    flat = a.reshape(-1)
    rows = -(-flat.shape[0] // ROW)
    rows = -(-rows // mult) * mult
    return jnp.pad(flat, (0, rows * ROW - flat.shape[0])).reshape(rows, ROW)


def _pack(parts, mult, total=None):
    p = jnp.concatenate([_rows_of(a, mult) for a in parts], axis=0)
    if total is not None:
        p = jnp.pad(p, ((0, total - p.shape[0]), (0, 0)))
    return p


def _unpack(pack, shapes, mult):
    out, r0 = [], 0
    for shp in shapes:
        size = int(np.prod(shp))
        rows = -(-(-(-size // ROW)) // mult) * mult
        out.append(pack[r0:r0 + rows].reshape(-1)[:size].reshape(shp))
        r0 += rows
    return out


def _big_shards(t):
    return [t["fox_w_in"][0], t["fox_w_out"][0], t["sgu_w_in"][0], t["sgu_w_out"][0],
            t["ffn_w_up"][0], t["ffn_w_up"][1], t["ffn_w_down"][0], t["ffn_w_down"][1]]


def _row_tile(rows):
    return next(t for t in (512, 352, 256, 128, 64) if rows % t == 0)


def kernel(x, c, fox_w_in, fox_b_f, fox_q_gain, fox_k_gain, fox_w_out, sgu_w_in, sgu_b_in, sgu_v_gain, sgu_v_bias, sgu_w_s, sgu_b_s, sgu_w_out, ffn_w_up, ffn_conv_w, ffn_conv_b, ffn_w_down, ada_w, ada_b, norm1_g, norm2_g, final_g, loss_target, m_fox_w_in, m_fox_b_f, m_fox_q_gain, m_fox_k_gain, m_fox_w_out, m_sgu_w_in, m_sgu_b_in, m_sgu_v_gain, m_sgu_v_bias, m_sgu_w_s, m_sgu_b_s, m_sgu_w_out, m_ffn_w_up, m_ffn_conv_w, m_ffn_conv_b, m_ffn_w_down, m_ada_w, m_ada_b, m_norm1_g, m_norm2_g, m_final_g, v_fox_w_in, v_fox_b_f, v_fox_q_gain, v_fox_k_gain, v_fox_w_out, v_sgu_w_in, v_sgu_b_in, v_sgu_v_gain, v_sgu_v_bias, v_sgu_w_s, v_sgu_b_s, v_sgu_w_out, v_ffn_w_up, v_ffn_conv_w, v_ffn_conv_b, v_ffn_w_down, v_ada_w, v_ada_b, v_norm1_g, v_norm2_g, v_final_g):
    w = dict(fox_w_in=fox_w_in, fox_b_f=fox_b_f, fox_q_gain=fox_q_gain, fox_k_gain=fox_k_gain, fox_w_out=fox_w_out,
             sgu_w_in=sgu_w_in, sgu_b_in=sgu_b_in, sgu_v_gain=sgu_v_gain, sgu_v_bias=sgu_v_bias, sgu_w_s=sgu_w_s,
             sgu_b_s=sgu_b_s, sgu_w_out=sgu_w_out, ffn_w_up=ffn_w_up, ffn_conv_w=ffn_conv_w, ffn_conv_b=ffn_conv_b,
             ffn_w_down=ffn_w_down, ada_w=ada_w, ada_b=ada_b, norm1_g=norm1_g, norm2_g=norm2_g, final_g=final_g)
    mom = dict(fox_w_in=m_fox_w_in, fox_b_f=m_fox_b_f, fox_q_gain=m_fox_q_gain, fox_k_gain=m_fox_k_gain,
               fox_w_out=m_fox_w_out, sgu_w_in=m_sgu_w_in, sgu_b_in=m_sgu_b_in, sgu_v_gain=m_sgu_v_gain,
               sgu_v_bias=m_sgu_v_bias, sgu_w_s=m_sgu_w_s, sgu_b_s=m_sgu_b_s, sgu_w_out=m_sgu_w_out,
               ffn_w_up=m_ffn_w_up, ffn_conv_w=m_ffn_conv_w, ffn_conv_b=m_ffn_conv_b, ffn_w_down=m_ffn_w_down,
               ada_w=m_ada_w, ada_b=m_ada_b, norm1_g=m_norm1_g, norm2_g=m_norm2_g, final_g=m_final_g)
    var = dict(fox_w_in=v_fox_w_in, fox_b_f=v_fox_b_f, fox_q_gain=v_fox_q_gain, fox_k_gain=v_fox_k_gain,
               fox_w_out=v_fox_w_out, sgu_w_in=v_sgu_w_in, sgu_b_in=v_sgu_b_in, sgu_v_gain=v_sgu_v_gain,
               sgu_v_bias=v_sgu_v_bias, sgu_w_s=v_sgu_w_s, sgu_b_s=v_sgu_b_s, sgu_w_out=v_sgu_w_out,
               ffn_w_up=v_ffn_w_up, ffn_conv_w=v_ffn_conv_w, ffn_conv_b=v_ffn_conv_b, ffn_w_down=v_ffn_w_down,
               ada_w=v_ada_w, ada_b=v_ada_b, norm1_g=v_norm1_g, norm2_g=v_norm2_g, final_g=v_final_g)

    ax, ay, ac = _mesh_pos()
    chip = 2 * ax + ay
    dev = 2 * chip + ac

    small_shard_shapes = tuple(w[n].shape for n in SMALL_SHARDED)
    blk = _pack([c] + [w[n] for n in SMALL_SHARDED], 1, 16)
    gat = _allgather8(blk, "gather_small").reshape(N_DEV, 16, ROW)
    c_all = gat[:, 0, :]
    per_chip = [_unpack(gat[2 * j, 1:], small_shard_shapes, 1) for j in range(N_CHIP)]
    full_small = {n: jnp.concatenate([per_chip[j][i] for j in range(N_CHIP)], axis=-1)
                  for i, n in enumerate(SMALL_SHARDED)}

    mine = [a.astype(BF16) for a in _big_shards(w)]
    with_own = lambda gat, own: [lax.dynamic_update_slice(g_, m_[None], (chip, 0, 0)) for g_, m_ in zip(gat, own)]
    fwi, = with_own(_gather_shards(mine[:1], "gather_fox_w_in"), mine[:1])
    fwi_full = _join_columns(fwi, FOX_NP, "join_fox_w_in")
    wts = dict(fox_w_in=fwi_full)

    def make_wts(gathered):
        fwo, swi, swo, up0, up1, dn0, dn1 = with_own(gathered, mine[1:])
        return dict(fox_w_out=fwo.reshape(D, D), sgu_w_in=swi, sgu_w_out=swo.reshape(SGW, D),
                    ffn_w_up=[up0, up1], ffn_w_down=[dn0.reshape(DFF, D), dn1.reshape(DFF, D)])

    c_arr = jnp.reshape(ac, (1,)).astype(jnp.int32)
    me_arr = jnp.reshape(chip, (1,)).astype(jnp.int32)

    def chip_sums(glist, tag):
        sibs = _rs_to_sibling(glist, "rs_sibling" + tag)
        return [_rs_chip_sum(g_, s_, c_arr, _row_tile(s_.shape[1]), "rs_chip_sum%s%d" % (tag, a))
                for a, (g_, s_) in enumerate(zip(glist, sibs))]

    def rs_prepare(gl):
        g_fwo, g_swi, g_swo, g_wu0, g_wu1, g_wd0, g_wd1 = gl
        return chip_sums([g_fwo.reshape(N_CHIP, 256, D), g_swi, g_swo.reshape(N_CHIP, 512, D), g_wu0, g_wu1,
                          g_wd0.reshape(N_CHIP, 704, D), g_wd1.reshape(N_CHIP, 704, D)], "")

    comm = dict(shards=mine[1:], make_wts=make_wts, rs_prepare=rs_prepare)

    da = ada_w.shape[2]
    ada_b_cols = lax.dynamic_slice_in_dim(ada_b, chip * da, da, axis=1)[:, None, :]
    mod_cols, c_act = _ada_mod(c_all, ada_w, ada_b_cols)
    mod_all = _allgather8(mod_cols.reshape(-1, ROW), "gather_mod").reshape(N_DEV, 2, N_DEV, da)
    mod_mine = lax.dynamic_index_in_dim(mod_all[0::2], dev, axis=2, keepdims=False)
    mod = jnp.swapaxes(mod_mine, 0, 1).reshape(2, N_CHIP * da)

    small = dict(norm1_g=norm1_g, norm2_g=norm2_g, final_g=final_g[None], fox_q_gain=fox_q_gain,
                 fox_k_gain=fox_k_gain, fox_b_f=fox_b_f, sgu_b_in=full_small["sgu_b_in"],
                 sgu_v_gain=full_small["sgu_v_gain"], sgu_v_bias=full_small["sgu_v_bias"], sgu_w_s=sgu_w_s[0],
                 sgu_b_s=sgu_b_s[0], ffn_conv_w=full_small["ffn_conv_w"], ffn_conv_b=ffn_conv_b)
    loss_dev, dx, g, dmod, (css, rcvs) = _local_step(x[0], loss_target[0], mod, wts, small, comm)

    g["ada_b"] = dmod
    g["loss"] = loss_dev
    small_names = ("ada_b",) + SMALL_SHARDED + tuple(n for n in SMALL_REPL if n != "ada_b") + ("loss",)
    gs = _pack([g[n] for n in small_names], 1)
    rows_s = -(-gs.shape[0] // 8) * 8
    gs = jnp.pad(gs, ((0, rows_s - gs.shape[0]), (0, 0)))
    gs_all = _allgather8(gs, "gather_small_grads").reshape(N_DEV, rows_s, ROW)
    gsum = _sum8(gs_all, "sum_small_grads")
    full_shapes = {n: w[n].shape for n in SMALL_REPL}
    full_shapes.update({n: w[n].shape[:-1] + (w[n].shape[-1] * N_CHIP,) for n in SMALL_SHARDED})
    full_shapes["loss"] = ()
    gfull = dict(zip(small_names, _unpack(gsum, [full_shapes[n] for n in small_names], 1)))
    grads = {n: gfull[n] for n in SMALL_REPL}
    for n in SMALL_SHARDED:
        width = w[n].shape[-1]
        grads[n] = lax.dynamic_slice_in_dim(gfull[n], chip * width, width, axis=gfull[n].ndim - 1)
    dmod_all = gs_all[:, :12, :].reshape(N_DEV, 2, N_CHIP * da)
    dmod_cols = jnp.swapaxes(lax.dynamic_slice_in_dim(dmod_all, chip * da, da, axis=2), 0, 1)
    grads["ada_w"] = _ada_w_grad(c_act.T, dmod_cols)

    gfi = _split_columns(g["fox_w_in"], N_CHIP, FOX_N // N_CHIP, "split_fox_w_in")
    cs_fox = chip_sums([gfi], "_fox")
    css = cs_fox + list(css)
    rcvs = list(_rs_across_chips(cs_fox, "rs_chips_fox")) + list(rcvs)
    halves = [_rs_final_sum(cs_, r_, me_arr, _row_tile(cs_.shape[1]), "rs_final_sum%d" % a)
              for a, (cs_, r_) in enumerate(zip(css, rcvs))]
    others = _rs_swap_halves(halves, "rs_swap")
    red = [jnp.concatenate([jnp.where(ac == 0, h_, o_), jnp.where(ac == 0, o_, h_)]) for h_, o_ in zip(halves, others)]
    grads.update(fox_w_in=red[0], fox_w_out=red[1], sgu_w_in=red[2], sgu_w_out=red[3],
                 ffn_w_up=jnp.stack([red[4], red[5]]), ffn_w_down=jnp.stack([red[6], red[7]]))

    delta, new_m, new_v = {}, {}, {}
    for n in BIG + ("ada_w",):
        shp = w[n].shape
        two_d = lambda a: a.reshape(-1, shp[-1])
        d_, m_, v_ = _adamw(two_d(w[n]), two_d(grads[n]), two_d(mom[n]), two_d(var[n]), "adamw_" + n)
        delta[n], new_m[n], new_v[n] = d_.reshape(shp), m_.reshape(shp), v_.reshape(shp)
    rest = SMALL_SHARDED + SMALL_REPL
    packs = [_pack([t[n] for n in rest], 1) for t in (w, grads, mom, var)]
    rows_r = -(-packs[0].shape[0] // 8) * 8
    packs = [jnp.pad(p, ((0, rows_r - p.shape[0]), (0, 0))) for p in packs]
    outs = _adamw(*packs, "adamw_small")
    for t, o in zip((delta, new_m, new_v), outs):
        t.update(zip(rest, _unpack(o, [w[n].shape for n in rest], 1)))

    loss = gfull["loss"]
    return (loss, dx[None], *[grads[n].reshape(w[n].shape) for n in WEIGHTS], *[delta[n] for n in WEIGHTS],
            *[new_m[n] for n in WEIGHTS], *[new_v[n] for n in WEIGHTS])
```

```python
import functools
import math

import numpy as np
import jax
import jax.numpy as jnp
from jax import lax
from jax.experimental import pallas as pl
from jax.experimental.pallas import tpu as pltpu

F32 = jnp.float32
BF16 = jnp.bfloat16
MESH = pl.DeviceIdType.MESH

D = 1024
H = 16
DH = 64
NP = H // 2
LANES = 128
DFF = 2816
SGW = 2048
SGG = 8
SGC = 256
SGB = 128
CHUNK = 64
EPS = 1e-6
FOX_N = 4 * D + H
FOX_NP = 4224
GT = 256
NGT = DFF // GT
SCALE = DH ** -0.5
LOG2E = 1.4426950408889634

ADAM_LR = 0.001
ADAM_B1 = 0.9
ADAM_B2 = 0.999
ADAM_EPS = 1e-08
ADAM_WD = 0.01
ADAM_STEP = 10

V7X_VMEM_LIMIT = 56 * 1024 * 1024

L_F = 64
L_NF = 67
L_LSE = 70


def _cparams(sem=None):
    return pltpu.CompilerParams(dimension_semantics=sem, vmem_limit_bytes=V7X_VMEM_LIMIT)


def _split3(x):
    hi = x.astype(BF16)
    r = x - hi.astype(F32)
    mid = r.astype(BF16)
    lo = (r - mid.astype(F32)).astype(BF16)
    return hi, mid, lo


def _dot(a, b, dims=(((1,), (0,)), ((), ()))):
    return lax.dot_general(a, b, dims, preferred_element_type=F32)


def _dot_nt(a, b):
    return _dot(a, b, (((1,), (1,)), ((), ())))


def _dot_tn(a, b):
    return _dot(a, b, (((0,), (0,)), ((), ())))


def _exact_dot(m_bf16, x_f32):
    hi, mid, lo = _split3(x_f32)
    return _dot(m_bf16, hi) + _dot(m_bf16, mid) + _dot(m_bf16, lo)


def _exact_dot_r(x_f32, m_bf16):
    hi, mid, lo = _split3(x_f32)
    return _dot(hi, m_bf16) + _dot(mid, m_bf16) + _dot(lo, m_bf16)


def _head_block_ones():
    r = lax.broadcasted_iota(jnp.int32, (LANES, LANES), 0) // DH
    c = lax.broadcasted_iota(jnp.int32, (LANES, LANES), 1) // DH
    return (r == c).astype(BF16)


def _sigmoid(x):
    return 1.0 / (1.0 + jnp.exp(-x))


def _gelu(x):
    c = math.sqrt(2.0 / math.pi)
    return 0.5 * x * (1.0 + jnp.tanh(c * (x + 0.044715 * (x * x * x))))


def _gelu_and_grad(x):
    c = math.sqrt(2.0 / math.pi)
    x2 = x * x
    t = jnp.tanh(c * (x + 0.044715 * (x2 * x)))
    half = 0.5 * (1.0 + t)
    return x * half, half + 0.5 * x * (1.0 - t * t) * c * (1.0 + 3 * 0.044715 * x2)


def _rstd_rows(x):
    return lax.rsqrt(jnp.mean(x * x, axis=-1, keepdims=True) + EPS)


def _norm_mod_matmul(x, ng, sc, sh, w, bias, out_dtype, ts, tn, name, planes=1):
    s, d = x.shape
    ns = w.shape[-1]
    n = w.shape[0] * ns if w.ndim == 3 else ns
    nc = n // planes

    def body(x_ref, ng_ref, sc_ref, sh_ref, w_ref, b_ref, o_ref, h_ref):
        xv = x_ref[...]
        h = (xv * _rstd_rows(xv) * ng_ref[...] * (1.0 + sc_ref[...]) + sh_ref[...]).astype(BF16)
        h_ref[...] = h
        for e in range(planes):
            for c0 in range(0, nc, tn):
                g0 = e * nc + c0
                wv = w_ref[g0 // ns, :, g0 % ns:g0 % ns + tn] if w.ndim == 3 else w_ref[:, g0:g0 + tn]
                val = (_dot(h, wv) + b_ref[:, g0:g0 + tn]).astype(out_dtype)
                if planes == 1:
                    o_ref[:, c0:c0 + tn] = val
                else:
                    o_ref[e, :, c0:c0 + tn] = val

    vec = pl.BlockSpec((1, d), lambda i: (0, 0))
    w_spec = (pl.BlockSpec(w.shape, lambda i: (0, 0, 0)) if w.ndim == 3 else pl.BlockSpec((d, n), lambda i: (0, 0)))
    if planes == 1:
        o_spec, o_shape = pl.BlockSpec((ts, n), lambda i: (i, 0)), (s, n)
    else:
        o_spec, o_shape = pl.BlockSpec((planes, ts, nc), lambda i: (0, i, 0)), (planes, s, nc)
    return pl.pallas_call(
        body, name=name, grid=(s // ts,),
        in_specs=[pl.BlockSpec((ts, d), lambda i: (i, 0)), vec, vec, vec, w_spec,
                  pl.BlockSpec((1, n), lambda i: (0, 0))],
        out_specs=[o_spec, pl.BlockSpec((ts, d), lambda i: (i, 0))],
        out_shape=[jax.ShapeDtypeStruct(o_shape, out_dtype), jax.ShapeDtypeStruct((s, d), BF16)],
        compiler_params=_cparams(("arbitrary",)),
    )(x, ng, sc, sh, w, bias)


def _matmul(a, b, ta, tb, tm, tn, tk, out_dtype, name, out_parts=1):
    if a.ndim == 3:
        m, k = a.shape[1], a.shape[0] * a.shape[2]
        nkp = a.shape[2] // tk
    else:
        m, k = (a.shape[1], a.shape[0]) if ta else a.shape
    if b.ndim == 3:
        n = b.shape[1] if tb else b.shape[0] * b.shape[2]
        nbp = b.shape[2] // (tk if tb else tn)
    else:
        n = b.shape[0] if tb else b.shape[1]
    nk = k // tk
    nop = n // out_parts // tn
    dims = (((0,) if ta else (1,), (1,) if tb else (0,)), ((), ()))

    def body(a_ref, b_ref, o_ref, acc):
        kk = pl.program_id(2)

        @pl.when(kk == 0)
        def _():
            acc[...] = jnp.zeros_like(acc)
        acc[...] += _dot(a_ref[...], b_ref[...], dims)

        @pl.when(kk == nk - 1)
        def _():
            o_ref[...] = acc[...].astype(out_dtype)

    if a.ndim == 3:
        a_spec = pl.BlockSpec((None, tm, tk), lambda i, j, kk: (kk // nkp, i, kk % nkp))
    else:
        a_spec = (pl.BlockSpec((tk, tm), lambda i, j, kk: (kk, i)) if ta
                  else pl.BlockSpec((tm, tk), lambda i, j, kk: (i, kk)))
    if b.ndim == 3 and tb:
        b_spec = pl.BlockSpec((None, tn, tk), lambda i, j, kk: (kk // nbp, j, kk % nbp))
    elif b.ndim == 3:
        b_spec = pl.BlockSpec((None, tk, tn), lambda i, j, kk: (j // nbp, kk, j % nbp))
    else:
        b_spec = (pl.BlockSpec((tn, tk), lambda i, j, kk: (j, kk)) if tb
                  else pl.BlockSpec((tk, tn), lambda i, j, kk: (kk, j)))
    if out_parts > 1:
        o_spec = pl.BlockSpec((None, tm, tn), lambda i, j, kk: (j // nop, i, j % nop))
        o_shape = (out_parts, m, n // out_parts)
    else:
        o_spec, o_shape = pl.BlockSpec((tm, tn), lambda i, j, kk: (i, j)), (m, n)
    return pl.pallas_call(
        body, name=name, grid=(m // tm, n // tn, nk),
        in_specs=[a_spec, b_spec],
        out_specs=o_spec,
        out_shape=jax.ShapeDtypeStruct(o_shape, out_dtype),
        scratch_shapes=[pltpu.VMEM((tm, tn), F32)],
        compiler_params=_cparams(("arbitrary", "arbitrary", "arbitrary")),
    )(a, b)


def _matmul_wt(a, w, tn, tk, out_dtype, ts, name):
    s = a.shape[-2]
    ka, kw = a.shape[-1], w.shape[-1]
    k = ka * (a.shape[0] if a.ndim == 3 else 1)
    n = w.shape[-2]

    def body(a_ref, w_ref, o_ref):
        for n0 in range(0, n, tn):
            acc = None
            for g0 in range(0, k, tk):
                av = a_ref[g0 // ka, :, g0 % ka:g0 % ka + tk] if a.ndim == 3 else a_ref[:, g0:g0 + tk]
                wv = (w_ref[g0 // kw, n0:n0 + tn, g0 % kw:g0 % kw + tk] if w.ndim == 3
                      else w_ref[n0:n0 + tn, g0:g0 + tk])
                part = _dot_nt(av, wv)
                acc = part if acc is None else acc + part
            o_ref[:, n0:n0 + tn] = acc.astype(out_dtype)

    a_spec = (pl.BlockSpec((a.shape[0], ts, ka), lambda i: (0, i, 0)) if a.ndim == 3
              else pl.BlockSpec((ts, ka), lambda i: (i, 0)))
    w_spec = pl.BlockSpec(w.shape, (lambda i: (0, 0, 0)) if w.ndim == 3 else (lambda i: (0, 0)))
    return pl.pallas_call(
        body, name=name, grid=(s // ts,),
        in_specs=[a_spec, w_spec], out_specs=pl.BlockSpec((ts, n), lambda i: (i, 0)),
        out_shape=jax.ShapeDtypeStruct((s, n), out_dtype),
        compiler_params=_cparams(("arbitrary",)),
    )(a, w)


def _matmul_residual(a, w, xin, g, ts, name):
    s, k = a.shape
    d = w.shape[1]

    def body(a_ref, w_ref, x_ref, g_ref, o_ref, y_ref):
        y = _dot(a_ref[...], w_ref[...])
        o_ref[...] = x_ref[...] + g_ref[...] * y
        y_ref[...] = y.astype(BF16)

    return pl.pallas_call(
        body, name=name, grid=(s // ts,),
        in_specs=[pl.BlockSpec((ts, k), lambda i: (i, 0)),
                  pl.BlockSpec((k, d), lambda i: (0, 0)),
                  pl.BlockSpec((ts, d), lambda i: (i, 0)),
                  pl.BlockSpec((1, d), lambda i: (0, 0))],
        out_specs=[pl.BlockSpec((ts, d), lambda i: (i, 0)), pl.BlockSpec((ts, d), lambda i: (i, 0))],
        out_shape=[jax.ShapeDtypeStruct((s, d), F32), jax.ShapeDtypeStruct((s, d), BF16)],
        compiler_params=_cparams(("arbitrary",)),
    )(a, w, xin, g)


def _lane(shape):
    return lax.broadcasted_iota(jnp.int32, shape, 1)


def _pair_norm(x, gain2, bones):
    msq = _exact_dot_r(x * x, bones) * (1.0 / DH)
    r = lax.rsqrt(msq + EPS)
    xh = x * r
    return xh * gain2, xh, r


def _fox_post(proj, qg2, kg2, bf, ts, name):
    s = proj.shape[0]

    def body(p_ref, qg_ref, kg_ref, bf_ref, q_ref, k_ref, v_ref, carry):
        @pl.when(pl.program_id(0) == 0)
        def _():
            carry[...] = jnp.zeros_like(carry)
        lane = _lane((ts, LANES))
        bones = _head_block_ones()
        xf = p_ref[:, 4 * D:4 * D + LANES] + bf_ref[...]
        logf = jnp.minimum(xf, 0.0) - jnp.log(1.0 + jnp.exp(-jnp.abs(xf)))
        logf = jnp.where(lane < H, logf, 0.0)
        rr = lax.broadcasted_iota(jnp.int32, (ts, ts), 0)
        cc = lax.broadcasted_iota(jnp.int32, (ts, ts), 1)
        ltri = (cc <= rr).astype(BF16)
        fcum = _exact_dot(ltri, logf) + carry[0:1, :]
        carry[0:1, :] = fcum[ts - 1:ts, :]
        fhi, fmid, flo = _split3(fcum * LOG2E)
        fhi, fmid, flo = fhi.astype(F32), fmid.astype(F32), flo.astype(F32)
        one_q = ((lane >= L_NF) & (lane < L_NF + 3)).astype(F32)
        one_k = (((lane >= L_F) & (lane < L_F + 3)) | ((lane >= L_LSE) & (lane < L_LSE + 3))).astype(F32)
        one_v = ((lane >= L_F) & (lane < L_F + 3)).astype(F32)
        for p in range(NP):
            qn, _, _ = _pair_norm(p_ref[:, p * LANES:(p + 1) * LANES], qg_ref[...], bones)
            kn, _, _ = _pair_norm(p_ref[:, D + p * LANES:D + (p + 1) * LANES], kg_ref[...], bones)
            vv = p_ref[:, 2 * D + p * LANES:2 * D + (p + 1) * LANES]
            qn = qn * (SCALE * LOG2E)
            for e in range(2):
                h = 2 * p + e
                if e == 1:
                    qe, ke, ve = (pltpu.roll(t, DH, axis=1) for t in (qn, kn, vv))
                else:
                    qe, ke, ve = qn, kn, vv
                f0, f1, f2 = fhi[:, h:h + 1], fmid[:, h:h + 1], flo[:, h:h + 1]
                fq = jnp.where(lane == L_F, f0, jnp.where(lane == L_F + 1, f1, jnp.where(lane == L_F + 2, f2, one_q)))
                fk = jnp.where(lane == L_NF, -f0, jnp.where(lane == L_NF + 1, -f1, jnp.where(lane == L_NF + 2, -f2, one_k)))
                q_ref[h] = jnp.where(lane < DH, qe, fq).astype(BF16)
                k_ref[h] = jnp.where(lane < DH, ke, fk).astype(BF16)
                v_ref[h] = jnp.where(lane < DH, ve, one_v).astype(BF16)

    hs = pl.BlockSpec((H, ts, LANES), lambda i: (0, i, 0))
    vec = pl.BlockSpec((1, LANES), lambda i: (0, 0))
    shp = jax.ShapeDtypeStruct((H, s, LANES), BF16)
    return pl.pallas_call(
        body, name=name, grid=(s // ts,),
        in_specs=[pl.BlockSpec((ts, FOX_NP), lambda i: (i, 0)), vec, vec, vec],
        out_specs=[hs, hs, hs], out_shape=[shp, shp, shp],
        scratch_shapes=[pltpu.VMEM((8, LANES), F32)],
        compiler_params=_cparams(("arbitrary",)),
    )(proj, qg2, kg2, bf)


def _gather_copies(p_refs, o_refs, send_sems, recv_sems):
    x, y, c = _mesh_pos()
    me = 2 * x + y
    sends, arrivals = [], []
    for a, (p_ref, o_ref) in enumerate(zip(p_refs, o_refs)):
        rh = p_ref.shape[0] // 2
        for k, chip in enumerate(_other_chips(x, y)):
            ci = 2 * chip[0] + chip[1]
            for cc in range(2):
                sends.append(_remote(p_ref.at[pl.ds(c * rh, rh), :], o_ref.at[me, pl.ds(c * rh, rh), :],
                                     send_sems.at[6 * a + 2 * k + cc], recv_sems.at[6 * a + 2 * k + c], (*chip, cc)))
                arrivals.append(_remote(o_ref.at[ci, pl.ds(cc * rh, rh), :], o_ref.at[ci, pl.ds(cc * rh, rh), :],
                                        send_sems.at[6 * a + 2 * k + cc], recv_sems.at[6 * a + 2 * k + cc],
                                        (*chip, cc)))
    return sends, arrivals


def _attn_fwd(qa, ka, va, tq, name, shards=()):
    s = qa.shape[1]
    nq = s // tq
    na = len(shards)

    def body(*refs):
        q_ref, k_ref, v_ref = refs[:3]
        p_refs = refs[3:3 + na]
        o_ref, ql_ref = refs[3 + na:5 + na]
        g_refs = refs[5 + na:5 + 2 * na]
        i = pl.program_id(1)
        if na:
            send_sems, recv_sems = refs[5 + 2 * na:]

            @pl.when((pl.program_id(0) == 0) & (i == 0))
            def _():
                for cp in _gather_copies(p_refs, g_refs, send_sems, recv_sems)[0]:
                    cp.start()
        lane = _lane((tq, LANES))
        qs_ = [q_ref[0], q_ref[1]]

        th = tq // QSPLIT

        def step(j, carry, masked):
            off = pl.multiple_of(j * tq, tq)
            new = []
            for e in range(2):
                kb = k_ref[e, pl.ds(off, tq), :]
                vb = v_ref[e, pl.ds(off, tq), :]
                for r in range(QSPLIT):
                    m, acc = carry[e * QSPLIT + r]
                    sc = _dot_nt(qs_[e][r * th:(r + 1) * th], kb)
                    if masked:
                        rr = lax.broadcasted_iota(jnp.int32, (th, tq), 0) + r * th
                        cc = lax.broadcasted_iota(jnp.int32, (th, tq), 1)
                        sc = jnp.where(cc <= rr, sc, -jnp.inf)
                    m_new = jnp.maximum(m, jnp.max(sc, axis=-1, keepdims=True))
                    pr = jnp.exp2(sc - m_new)
                    acc = acc * jnp.exp2(m - m_new) + _dot(pr.astype(BF16), vb)
                    new.append((m_new, acc))
            return tuple(new)

        one = (jnp.full((th, 1), -jnp.inf, F32), jnp.zeros((th, LANES), F32))
        carry = lax.fori_loop(0, i, functools.partial(step, masked=False), (one,) * (2 * QSPLIT))
        carry = step(i, carry, True)
        carry = [tuple(jnp.concatenate([carry[e * QSPLIT + r][t] for r in range(QSPLIT)], axis=0) for t in range(2))
                 for e in range(2)]
        outs = []
        for e in range(2):
            m, acc = carry[e]
            l = acc[:, L_F:L_F + 1]
            outs.append(acc / l)
            lse = m + jnp.log2(l)
            h0, h1, h2 = _split3(-lse)
            ql = jnp.where(lane == L_LSE, h0.astype(F32),
                           jnp.where(lane == L_LSE + 1, h1.astype(F32),
                                     jnp.where(lane == L_LSE + 2, h2.astype(F32), qs_[e].astype(F32))))
            ql_ref[e] = ql.astype(BF16)
        o_ref[...] = jnp.where(lane < DH, outs[0], pltpu.roll(outs[1], DH, axis=1))
        if na:
            @pl.when((pl.program_id(0) == NP - 1) & (i == nq - 1))
            def _():
                sends, arrivals = _gather_copies(p_refs, g_refs, send_sems, recv_sems)
                for cp in arrivals:
                    cp.wait_recv()
                for cp in sends:
                    cp.wait_send()

    res = pl.BlockSpec((2, s, LANES), lambda p, i: (p, 0, 0))
    qs = pl.BlockSpec((2, tq, LANES), lambda p, i: (p, i, 0))
    outs = pl.pallas_call(
        body, name=name, grid=(NP, nq),
        in_specs=[qs, res, res] + [HBM_SPEC] * na,
        out_specs=[pl.BlockSpec((tq, LANES), lambda p, i: (i, p)), qs] + [HBM_SPEC] * na,
        out_shape=[jax.ShapeDtypeStruct((s, D), F32), jax.ShapeDtypeStruct((H, s, LANES), BF16)]
        + [jax.ShapeDtypeStruct((N_CHIP,) + p.shape, p.dtype) for p in shards],
        scratch_shapes=[pltpu.SemaphoreType.DMA((6 * na,))] * 2 if na else [],
        compiler_params=_cparams(("arbitrary", "arbitrary")),
    )(qa, ka, va, *shards)
    return outs[0], outs[1], list(outs[2:])


def _chip_exchange_copies(cs_refs, o_refs, send_sems, recv_sems):
    x, y, c = _mesh_pos()
    cps = []
    for a, (cs_ref, o_ref) in enumerate(zip(cs_refs, o_refs)):
        for k, chip in enumerate(_other_chips(x, y)):
            ci = 2 * chip[0] + chip[1]
            cps.append(_remote(cs_ref.at[ci], o_ref.at[k], send_sems.at[3 * a + k], recv_sems.at[3 * a + k],
                               (*chip, c)))
    return cps


def _attn_bwd(ql, ka, va, doa, tq, name, css=()):
    s = ql.shape[1]
    nq = s // tq
    na = len(css)

    def body(*refs):
        q_ref, k_ref, v_ref, do_ref = refs[:4]
        cs_refs = refs[4:4 + na]
        dqo_ref, dk_ref, dv_ref = refs[4 + na:7 + na]
        r_refs = refs[7 + na:7 + 2 * na]
        dq_ref = refs[7 + 2 * na]
        j = pl.program_id(1)
        if na:
            send_sems, recv_sems = refs[8 + 2 * na:]

            @pl.when((pl.program_id(0) == 0) & (j == 0))
            def _():
                for cp in _chip_exchange_copies(cs_refs, r_refs, send_sems, recv_sems):
                    cp.start()

        @pl.when(j == 0)
        def _():
            dq_ref[...] = jnp.zeros_like(dq_ref)
        lane = _lane((tq, LANES))
        kbs = [k_ref[0], k_ref[1]]
        vbs = [v_ref[0], v_ref[1]]

        def step(i, carry, masked):
            ioff = pl.multiple_of(i * tq, tq)
            new = []
            for e in range(2):
                dk, dv = carry[e]
                qb = q_ref[e, pl.ds(ioff, tq), :]
                dob = do_ref[e, pl.ds(ioff, tq), :]
                pr = jnp.exp2(_dot_nt(qb, kbs[e]))
                if masked:
                    rr = lax.broadcasted_iota(jnp.int32, (tq, tq), 0)
                    cc = lax.broadcasted_iota(jnp.int32, (tq, tq), 1)
                    pr = jnp.where(cc <= rr, pr, 0.0)
                ds = (pr * _dot_nt(dob, vbs[e])).astype(BF16)
                dv = dv + _dot_tn(pr.astype(BF16), dob)
                dk = dk + _dot_tn(ds, qb)
                dq_ref[e, pl.ds(ioff, tq), :] += _dot(ds, kbs[e])
                new.append((dk, dv))
            return tuple(new)

        zero = jnp.zeros((tq, LANES), F32)
        carry = step(j, ((zero, zero), (zero, zero)), True)
        carry = lax.fori_loop(j + 1, nq, functools.partial(step, masked=False), carry)
        for e in range(2):
            dk, dv = carry[e]
            col = dk[:, L_NF:L_NF + 1]
            hi = col.astype(BF16).astype(F32)
            dk_ref[e] = jnp.where(lane == L_NF, hi, jnp.where(lane == L_NF + 1, col - hi, dk)).astype(BF16)
            dv_ref[e] = dv.astype(BF16)

        @pl.when(j == nq - 1)
        def _():
            lane_s = _lane((s, LANES))
            for e in range(2):
                dq = dq_ref[e]
                col = dq[:, L_F:L_F + 1]
                hi = col.astype(BF16).astype(F32)
                dqo_ref[e] = jnp.where(lane_s == L_F, hi, jnp.where(lane_s == L_F + 1, col - hi, dq)).astype(BF16)
        if na:
            @pl.when((pl.program_id(0) == NP - 1) & (j == nq - 1))
            def _():
                for cp in _chip_exchange_copies(cs_refs, r_refs, send_sems, recv_sems):
                    cp.wait()

    res = pl.BlockSpec((2, s, LANES), lambda p, j: (p, 0, 0))
    tile = pl.BlockSpec((2, tq, LANES), lambda p, j: (p, j, 0))
    shp = jax.ShapeDtypeStruct((H, s, LANES), BF16)
    outs = pl.pallas_call(
        body, name=name, grid=(NP, nq),
        in_specs=[res, tile, tile, res] + [HBM_SPEC] * na, out_specs=[res, tile, tile] + [HBM_SPEC] * na,
        out_shape=[shp, shp, shp] + [jax.ShapeDtypeStruct((3,) + cs.shape[1:], cs.dtype) for cs in css],
        scratch_shapes=[pltpu.VMEM((2, s, LANES), F32)] + ([pltpu.SemaphoreType.DMA((3 * na,))] * 2 if na else []),
        compiler_params=_cparams(("arbitrary", "arbitrary")),
    )(ql, ka, va, doa, *css)
    return outs[0], outs[1], outs[2], list(outs[3:])


def _gate(att, proj, ts, name):
    s = att.shape[0]

    def body(a_ref, o_ref, g_ref):
        g_ref[...] = (a_ref[...] * _sigmoid(o_ref[...])).astype(BF16)

    return pl.pallas_call(
        body, name=name, grid=(s // ts,),
        in_specs=[pl.BlockSpec((ts, D), lambda i: (i, 0)), pl.BlockSpec((ts, D), lambda i: (i, 3))],
        out_specs=pl.BlockSpec((ts, D), lambda i: (i, 0)),
        out_shape=jax.ShapeDtypeStruct((s, D), BF16),
        compiler_params=_cparams(("arbitrary",)),
    )(att, proj)


def _attn_bwd_prep(dgated, att, proj, ts, name):
    s = att.shape[0]

    def body(dg_ref, a_ref, o_ref, doa_ref, dop_ref):
        lane = _lane((ts, LANES))
        bones = _head_block_ones()
        for p in range(NP):
            sl = slice(p * LANES, (p + 1) * LANES)
            dg, a = dg_ref[:, sl], a_ref[:, sl]
            sig = _sigmoid(o_ref[:, sl])
            datt = dg * sig
            dop_ref[:, sl] = (dg * a * sig * (1.0 - sig)).astype(BF16)
            delta = _exact_dot_r(datt * a, bones)
            for e in range(2):
                de, dl = (datt, delta) if e == 0 else (pltpu.roll(datt, DH, axis=1), pltpu.roll(delta, DH, axis=1))
                h0, h1, h2 = _split3(-dl[:, 0:1])
                aug = jnp.where(lane == L_F, h0.astype(F32),
                                jnp.where(lane == L_F + 1, h1.astype(F32),
                                          jnp.where(lane == L_F + 2, h2.astype(F32), 0.0)))
                doa_ref[2 * p + e] = jnp.where(lane < DH, de, aug).astype(BF16)

    row = pl.BlockSpec((ts, D), lambda i: (i, 0))
    return pl.pallas_call(
        body, name=name, grid=(s // ts,),
        in_specs=[row, row, pl.BlockSpec((ts, D), lambda i: (i, 3))],
        out_specs=[pl.BlockSpec((H, ts, LANES), lambda i: (0, i, 0)), row],
        out_shape=[jax.ShapeDtypeStruct((H, s, LANES), BF16), jax.ShapeDtypeStruct((s, D), BF16)],
        compiler_params=_cparams(("arbitrary",)),
    )(dgated, att, proj)


def _fox_post_bwd(proj, dqa, dka, dva, dop, qg2, kg2, bf, ts, name):
    s = proj.shape[0]
    nt = s // ts

    def body(p_ref, dq_ref, dk_ref, dv_ref, dop_ref, qg_ref, kg_ref, bf_ref, o_ref, red_ref, carry):
        @pl.when(pl.program_id(0) == 0)
        def _():
            carry[...] = jnp.zeros_like(carry)
            red_ref[...] = jnp.zeros_like(red_ref)
        lane = _lane((ts, LANES))
        bones = _head_block_ones()
        d_f = jnp.zeros((ts, LANES), F32)
        dqg = jnp.zeros((1, LANES), F32)
        dkg = jnp.zeros((1, LANES), F32)
        for p in range(NP):
            heads = [[ref[2 * p + e].astype(F32) for e in range(2)] for ref in (dq_ref, dk_ref, dv_ref)]
            pair = [jnp.where(lane < DH, a, pltpu.roll(b, DH, axis=1)) for a, b in heads]
            for e in range(2):
                dqe, dke = heads[0][e], heads[1][e]
                col = (dqe[:, L_F:L_F + 1] + dqe[:, L_F + 1:L_F + 2]
                       - dke[:, L_NF:L_NF + 1] - dke[:, L_NF + 1:L_NF + 2])
                d_f = jnp.where(lane == 2 * p + e, col, d_f)
            for idx, (g_ref, base) in enumerate(((qg_ref, 0), (kg_ref, D))):
                x = p_ref[:, base + p * LANES:base + (p + 1) * LANES]
                _, xh, r = _pair_norm(x, g_ref[...], bones)
                dn = pair[idx] * (SCALE if idx == 0 else 1.0 / LOG2E)
                t = dn * g_ref[...]
                mean_txh = _exact_dot_r(t * xh, bones) * (1.0 / DH)
                dx = r * (t - xh * mean_txh)
                o_ref[:, base + p * LANES:base + (p + 1) * LANES] = dx.astype(BF16)
                gsum = jnp.sum(dn * xh, axis=0, keepdims=True)
                if idx == 0:
                    dqg = dqg + gsum
                else:
                    dkg = dkg + gsum
            o_ref[:, 2 * D + p * LANES:2 * D + (p + 1) * LANES] = pair[2].astype(BF16)
        o_ref[:, 3 * D:4 * D] = dop_ref[...]
        rr = lax.broadcasted_iota(jnp.int32, (ts, ts), 0)
        cc = lax.broadcasted_iota(jnp.int32, (ts, ts), 1)
        utri = (cc >= rr).astype(BF16)
        dlogf = _exact_dot(utri, d_f) + carry[0:1, :]
        carry[0:1, :] = dlogf[0:1, :]
        xf = p_ref[:, 4 * D:4 * D + LANES] + bf_ref[...]
        dfl = jnp.where(lane < H, dlogf * _sigmoid(-xf), 0.0)
        o_ref[:, 4 * D:4 * D + LANES] = dfl.astype(BF16)
        red_ref[0:1, :] += dqg
        red_ref[1:2, :] += dkg
        red_ref[2:3, :] += jnp.sum(dfl, axis=0, keepdims=True)

    hs = pl.BlockSpec((H, ts, LANES), lambda i: (0, nt - 1 - i, 0))
    vec = pl.BlockSpec((1, LANES), lambda i: (0, 0))
    return pl.pallas_call(
        body, name=name, grid=(nt,),
        in_specs=[pl.BlockSpec((ts, FOX_NP), lambda i: (nt - 1 - i, 0)), hs, hs, hs,
                  pl.BlockSpec((ts, D), lambda i: (nt - 1 - i, 0)), vec, vec, vec],
        out_specs=[pl.BlockSpec((ts, FOX_NP), lambda i: (nt - 1 - i, 0)),
                   pl.BlockSpec((8, LANES), lambda i: (0, 0))],
        out_shape=[jax.ShapeDtypeStruct((s, FOX_NP), BF16), jax.ShapeDtypeStruct((8, LANES), F32)],
        scratch_shapes=[pltpu.VMEM((8, LANES), F32)],
        compiler_params=_cparams(("arbitrary",)),
    )(proj, dqa, dka, dva, dop, qg2, kg2, bf)


HALO = 16
TS = 512
TQ = 512
TR = 256
TP = 256
QSPLIT = 1
TKW = 2048


def _shift_down(x, k):
    return pltpu.roll(x, k, axis=0)


def _shift_up(x, k):
    return pltpu.roll(x, x.shape[0] - k, axis=0)


def _planes(ref):
    return jnp.concatenate([ref[0].astype(F32), ref[1].astype(F32)], axis=1)


def _conv_gate(a, cw, cb, ts, name):
    s = a.shape[1]
    hb = ts // HALO

    def body(prev_ref, a_ref, cw_ref, cb_ref, f_ref, ap_ref):
        i = pl.program_id(0)
        cwv, cbv = _planes(cw_ref), _planes(cb_ref)
        prev = jnp.where(i > 0, _planes(prev_ref), 0.0)
        ext = jnp.concatenate([prev, _planes(a_ref)], axis=0)
        ap = (_shift_down(ext, 2) * cwv[0:1, :] + _shift_down(ext, 1) * cwv[1:2, :]
              + ext * cwv[2:3, :] + cbv)[HALO:, :]
        g, val = ap[:, :GT], ap[:, GT:]
        f_ref[...] = (g * _sigmoid(g) * val).astype(BF16)
        ap_ref[0] = g.astype(BF16)
        ap_ref[1] = val.astype(BF16)

    tile = pl.BlockSpec((2, ts, GT), lambda i, j: (0, i, j))
    return pl.pallas_call(
        body, name=name, grid=(s // ts, NGT),
        in_specs=[pl.BlockSpec((2, HALO, GT), lambda i, j: (0, jnp.maximum(i * hb - 1, 0), j)), tile,
                  pl.BlockSpec((2, 8, GT), lambda i, j: (0, 0, j)),
                  pl.BlockSpec((2, 1, GT), lambda i, j: (0, 0, j))],
        out_specs=[pl.BlockSpec((ts, GT), lambda i, j: (i, j)), tile],
        out_shape=[jax.ShapeDtypeStruct((s, DFF), BF16), jax.ShapeDtypeStruct((2, s, DFF), BF16)],
        compiler_params=_cparams(("arbitrary", "arbitrary")),
    )(a, a, cw, cb)


def _conv_gate_bwd(a, ap, df, cw, ts, name):
    s = a.shape[1]
    hb = ts // HALO
    nt = s // ts

    def body(a_ref, ap_ref, apn_ref, df_ref, dfn_ref, cw_ref, da_ref, red_ref):
        i = pl.program_id(1)

        @pl.when(i == 0)
        def _():
            red_ref[...] = jnp.zeros_like(red_ref)
        cwv = _planes(cw_ref)
        apv = jnp.concatenate([_planes(ap_ref), _planes(apn_ref)], axis=0)
        dfn = jnp.where(i < nt - 1, dfn_ref[...].astype(F32), 0.0)
        dfe = jnp.concatenate([df_ref[...].astype(F32), dfn], axis=0)
        g, val = apv[:, :GT], apv[:, GT:]
        sg = _sigmoid(g)
        dap = jnp.concatenate([dfe * val * (sg * (1.0 + g * (1.0 - sg))), dfe * (g * sg)], axis=1)
        shifted = [_shift_up(dap, 2)[:ts], _shift_up(dap, 1)[:ts], dap[:ts]]
        da = shifted[0] * cwv[0:1, :] + shifted[1] * cwv[1:2, :] + shifted[2] * cwv[2:3, :]
        av = _planes(a_ref)
        sums = [jnp.sum(av * t, axis=0, keepdims=True) for t in shifted]
        sums.append(jnp.sum(shifted[2], axis=0, keepdims=True))
        for e in range(2):
            cols = slice(e * GT, (e + 1) * GT)
            da_ref[e] = da[:, cols].astype(BF16)
            for r, sm in enumerate(sums):
                red_ref[e, r:r + 1, :] += sm[:, cols]

    nhb = s // HALO
    tile = pl.BlockSpec((2, ts, GT), lambda j, i: (0, i, j))
    return pl.pallas_call(
        body, name=name, grid=(NGT, nt),
        in_specs=[tile, tile,
                  pl.BlockSpec((2, HALO, GT), lambda j, i: (0, jnp.minimum((i + 1) * hb, nhb - 1), j)),
                  pl.BlockSpec((ts, GT), lambda j, i: (i, j)),
                  pl.BlockSpec((HALO, GT), lambda j, i: (jnp.minimum((i + 1) * hb, nhb - 1), j)),
                  pl.BlockSpec((2, 8, GT), lambda j, i: (0, 0, j))],
        out_specs=[tile, pl.BlockSpec((2, 8, GT), lambda j, i: (0, 0, j))],
        out_shape=[jax.ShapeDtypeStruct((2, s, DFF), BF16), jax.ShapeDtypeStruct((2, 8, DFF), F32)],
        compiler_params=_cparams(("arbitrary", "arbitrary")),
    )(a, ap, ap, df, df, cw)


def _chunk_mask(transposed=False):
    t = lax.broadcasted_iota(jnp.int32, (SGB, SGB), 0) // CHUNK
    u = lax.broadcasted_iota(jnp.int32, (SGB, SGB), 1) // CHUNK
    return (t <= u) if transposed else (u <= t)


def _sgu_ln(v, gain, bias):
    mu = jnp.mean(v, axis=-1, keepdims=True)
    vc = v - mu
    rstd = lax.rsqrt(jnp.mean(vc * vc, axis=-1, keepdims=True) + EPS)
    vhat = vc * rstd
    return vhat * gain + bias, vhat, rstd


def _sgu_fwd(z, vgain, vbias, ws, bst, tr, name):
    s = z.shape[0]

    def body(zu_ref, zv_ref, vg_ref, vb_ref, ws_ref, bs_ref, y_ref):
        u = _gelu(zu_ref[...].astype(F32))
        vn, _, _ = _sgu_ln(_gelu(zv_ref[...].astype(F32)), vg_ref[...], vb_ref[...])
        vn = vn.astype(BF16)
        mask = _chunk_mask()
        for g in range(SGG):
            w = jnp.where(mask, ws_ref[g], 0.0).astype(BF16)
            for b in range(tr // SGB):
                rs, cs = slice(b * SGB, (b + 1) * SGB), slice(g * SGC, (g + 1) * SGC)
                mixed = _dot(w, vn[rs, cs]) + bs_ref[:, g:g + 1]
                y_ref[rs, cs] = (u[rs, cs] * mixed).astype(BF16)

    vec = pl.BlockSpec((1, SGW), lambda i: (0, 0))
    return pl.pallas_call(
        body, name=name, grid=(s // tr,),
        in_specs=[pl.BlockSpec((tr, SGW), lambda i: (i, 0)), pl.BlockSpec((tr, SGW), lambda i: (i, 1)),
                  vec, vec, pl.BlockSpec((SGG, SGB, SGB), lambda i: (0, 0, 0)),
                  pl.BlockSpec((SGB, LANES), lambda i: (0, 0))],
        out_specs=pl.BlockSpec((tr, SGW), lambda i: (i, 0)),
        out_shape=jax.ShapeDtypeStruct((s, SGW), BF16),
        compiler_params=_cparams(("arbitrary",)),
    )(z, z, vgain, vbias, ws, bst)


def _sgu_bwd(z, dy, vgain, vbias, ws, wst, bst, tr, name):
    s = z.shape[0]

    def body(zu_ref, zv_ref, dy_ref, vg_ref, vb_ref, ws_ref, wst_ref, bs_ref,
             dz_ref, rb_ref, rv_ref, dws_ref, dbs_ref, dvn_s):
        @pl.when(pl.program_id(0) == 0)
        def _():
            rb_ref[...] = jnp.zeros_like(rb_ref)
            rv_ref[...] = jnp.zeros_like(rv_ref)
            dws_ref[...] = jnp.zeros_like(dws_ref)
            dbs_ref[...] = jnp.zeros_like(dbs_ref)
        zu = zu_ref[...].astype(F32)
        zv = zv_ref[...].astype(F32)
        u, gu = _gelu_and_grad(zu)
        v, gv = _gelu_and_grad(zv)
        vn, vhat, rstd = _sgu_ln(v, vg_ref[...], vb_ref[...])
        vnb = vn.astype(BF16)
        dyv = dy_ref[...].astype(F32)
        dmix = (dyv * u).astype(BF16)
        mask = _chunk_mask()
        mask_t = _chunk_mask(transposed=True)
        lane = _lane((SGB, LANES))
        dbs = jnp.zeros((SGB, LANES), F32)
        for g in range(SGG):
            w = jnp.where(mask, ws_ref[g], 0.0).astype(BF16)
            wt = jnp.where(mask_t, wst_ref[g], 0.0).astype(BF16)
            dw = jnp.zeros((SGB, SGB), F32)
            for b in range(tr // SGB):
                rs, cs = slice(b * SGB, (b + 1) * SGB), slice(g * SGC, (g + 1) * SGC)
                mixed = _dot(w, vnb[rs, cs]) + bs_ref[:, g:g + 1]
                dz_ref[rs, cs] = (dyv[rs, cs] * mixed * gu[rs, cs]).astype(BF16)
                dm = dmix[rs, cs]
                dw = dw + _dot_nt(dm, vnb[rs, cs])
                dbs = dbs + jnp.where(lane == g, jnp.sum(dm.astype(F32), axis=-1, keepdims=True), 0.0)
                dvn_s[rs, cs] = _dot(wt, dm)
            dws_ref[g] += jnp.where(mask, dw, 0.0)
        dbs_ref[...] += dbs
        dvn = dvn_s[...]
        rv_ref[0:1, :] += jnp.sum(dvn * vhat, axis=0, keepdims=True)
        rv_ref[1:2, :] += jnp.sum(dvn, axis=0, keepdims=True)
        dvh = dvn * vg_ref[...]
        dv = rstd * (dvh - jnp.mean(dvh, axis=-1, keepdims=True)
                     - vhat * jnp.mean(dvh * vhat, axis=-1, keepdims=True))
        dz_ref[:, SGW:] = (dv * gv).astype(BF16)
        dzf = dz_ref[...].astype(F32)
        rb_ref[0:1, :] += jnp.sum(dzf, axis=0, keepdims=True)

    vec = pl.BlockSpec((1, SGW), lambda i: (0, 0))
    wsp = pl.BlockSpec((SGG, SGB, SGB), lambda i: (0, 0, 0))
    return pl.pallas_call(
        body, name=name, grid=(s // tr,),
        in_specs=[pl.BlockSpec((tr, SGW), lambda i: (i, 0)), pl.BlockSpec((tr, SGW), lambda i: (i, 1)),
                  pl.BlockSpec((tr, SGW), lambda i: (i, 0)), vec, vec, wsp, wsp,
                  pl.BlockSpec((SGB, LANES), lambda i: (0, 0))],
        out_specs=[pl.BlockSpec((tr, 2 * SGW), lambda i: (i, 0)),
                   pl.BlockSpec((8, 2 * SGW), lambda i: (0, 0)),
                   pl.BlockSpec((8, SGW), lambda i: (0, 0)), wsp,
                   pl.BlockSpec((SGB, LANES), lambda i: (0, 0))],
        out_shape=[jax.ShapeDtypeStruct((s, 2 * SGW), BF16), jax.ShapeDtypeStruct((8, 2 * SGW), F32),
                   jax.ShapeDtypeStruct((8, SGW), F32), jax.ShapeDtypeStruct((SGG, SGB, SGB), F32),
                   jax.ShapeDtypeStruct((SGB, LANES), F32)],
        scratch_shapes=[pltpu.VMEM((tr, SGW), F32)],
        compiler_params=_cparams(("arbitrary",)),
    )(z, z, dy, vgain, vbias, ws, wst, bst)


def _final_loss(x, fg, tgt, gprev, yprev, ts, name):
    s, d = x.shape

    def body(x_ref, fg_ref, t_ref, g_ref, y_ref, l_ref, dx_ref, dy_ref, red_ref):
        @pl.when(pl.program_id(0) == 0)
        def _():
            l_ref[...] = jnp.zeros_like(l_ref)
            red_ref[...] = jnp.zeros_like(red_ref)
        xv = x_ref[...]
        r = _rstd_rows(xv)
        xh = xv * r
        err = xh * fg_ref[...] - t_ref[...]
        l_ref[...] += 0.5 * jnp.sum(jnp.mean(err * err, axis=-1, keepdims=True))
        dyo = err * (1.0 / d)
        dxh = dyo * fg_ref[...]
        dx = r * (dxh - xh * jnp.mean(dxh * xh, axis=-1, keepdims=True))
        dx_ref[...] = dx
        dy_ref[...] = (dx * g_ref[...]).astype(BF16)
        red_ref[0:1, :] += jnp.sum(dyo * xh, axis=0, keepdims=True)
        red_ref[1:2, :] += jnp.sum(dx * y_ref[...].astype(F32), axis=0, keepdims=True)

    row = pl.BlockSpec((ts, d), lambda i: (i, 0))
    vec = pl.BlockSpec((1, d), lambda i: (0, 0))
    return pl.pallas_call(
        body, name=name, grid=(s // ts,),
        in_specs=[row, vec, row, vec, row],
        out_specs=[pl.BlockSpec((8, LANES), lambda i: (0, 0)), row, row, pl.BlockSpec((8, d), lambda i: (0, 0))],
        out_shape=[jax.ShapeDtypeStruct((8, LANES), F32), jax.ShapeDtypeStruct((s, d), F32),
                   jax.ShapeDtypeStruct((s, d), BF16), jax.ShapeDtypeStruct((8, d), F32)],
        compiler_params=_cparams(("arbitrary",)),
    )(x, fg, tgt, gprev, yprev)


def _norm_bwd(xin, dh, dxout, ng, sc, gprev, yprev, ts, name):
    s, d = xin.shape
    has_prev = gprev is not None

    def body(*refs):
        if has_prev:
            x_ref, dh_ref, dxo_ref, ng_ref, sc_ref, g_ref, y_ref, dx_ref, dy_ref, red_ref = refs
        else:
            x_ref, dh_ref, dxo_ref, ng_ref, sc_ref, dx_ref, red_ref = refs

        @pl.when(pl.program_id(0) == 0)
        def _():
            red_ref[...] = jnp.zeros_like(red_ref)
        xv = x_ref[...]
        r = _rstd_rows(xv)
        xh = xv * r
        dhv = dh_ref[...]
        dr = dhv * (1.0 + sc_ref[...])
        t = dr * ng_ref[...]
        dx = dxo_ref[...] + r * (t - xh * jnp.mean(t * xh, axis=-1, keepdims=True))
        dx_ref[...] = dx
        red_ref[0:1, :] += jnp.sum(dhv, axis=0, keepdims=True)
        red_ref[1:2, :] += jnp.sum(dhv * (xh * ng_ref[...]), axis=0, keepdims=True)
        red_ref[2:3, :] += jnp.sum(dr * xh, axis=0, keepdims=True)
        if has_prev:
            dy_ref[...] = (dx * g_ref[...]).astype(BF16)
            red_ref[3:4, :] += jnp.sum(dx * y_ref[...].astype(F32), axis=0, keepdims=True)

    row = pl.BlockSpec((ts, d), lambda i: (i, 0))
    vec = pl.BlockSpec((1, d), lambda i: (0, 0))
    red = pl.BlockSpec((8, d), lambda i: (0, 0))
    if has_prev:
        in_specs, args = [row, row, row, vec, vec, vec, row], (xin, dh, dxout, ng, sc, gprev, yprev)
        out_specs = [row, row, red]
        out_shape = [jax.ShapeDtypeStruct((s, d), F32), jax.ShapeDtypeStruct((s, d), BF16),
                     jax.ShapeDtypeStruct((8, d), F32)]
    else:
        in_specs, args = [row, row, row, vec, vec], (xin, dh, dxout, ng, sc)
        out_specs = [row, red]
        out_shape = [jax.ShapeDtypeStruct((s, d), F32), jax.ShapeDtypeStruct((8, d), F32)]
    return pl.pallas_call(
        body, name=name, grid=(s // ts,), in_specs=in_specs, out_specs=out_specs, out_shape=out_shape,
        compiler_params=_cparams(("arbitrary",)),
    )(*args)


def _ada_mod(c_all, ada_w, ada_b):
    nb = c_all.shape[0]
    da = ada_w.shape[2]

    def body(c_ref, w_ref, b_ref, o_ref, ca_ref):
        cv = c_ref[...]
        ca = cv * _sigmoid(cv)
        ca_ref[...] = ca
        o_ref[0] = lax.dot_general(ca, w_ref[0], (((1,), (0,)), ((), ())), precision=lax.Precision.HIGHEST,
                                   preferred_element_type=F32) + b_ref[0]

    return pl.pallas_call(
        body, name="ada_mod", grid=(2,),
        in_specs=[pl.BlockSpec((nb, D), lambda i: (0, 0)), pl.BlockSpec((1, D, da), lambda i: (i, 0, 0)),
                  pl.BlockSpec((1, 1, da), lambda i: (i, 0, 0))],
        out_specs=[pl.BlockSpec((1, nb, da), lambda i: (i, 0, 0)), pl.BlockSpec((nb, D), lambda i: (0, 0))],
        out_shape=[jax.ShapeDtypeStruct((2, nb, da), F32), jax.ShapeDtypeStruct((nb, D), F32)],
        compiler_params=_cparams(("arbitrary",)),
    )(c_all, ada_w, ada_b)


def _ada_w_grad(c_act_t, dmod):
    nb = c_act_t.shape[1]
    da = dmod.shape[2]
    tn = 512

    def body(c_ref, d_ref, o_ref):
        acc = c_ref[:, 0:1] * d_ref[0, 0:1, :]
        for b in range(1, nb):
            acc = acc + c_ref[:, b:b + 1] * d_ref[0, b:b + 1, :]
        o_ref[0] = acc

    return pl.pallas_call(
        body, name="ada_w_grad", grid=(2, da // tn),
        in_specs=[pl.BlockSpec((D, nb), lambda i, j: (0, 0)), pl.BlockSpec((1, nb, tn), lambda i, j: (i, 0, j))],
        out_specs=pl.BlockSpec((1, D, tn), lambda i, j: (i, 0, j)),
        out_shape=jax.ShapeDtypeStruct((2, D, da), F32),
        compiler_params=_cparams(("arbitrary", "arbitrary")),
    )(c_act_t, dmod)


def _conv_planes(cw, cb):
    cwp = jnp.swapaxes(cw.reshape(3, 2, DFF), 0, 1)
    return jnp.pad(cwp, ((0, 0), (0, 5), (0, 0))), cb.reshape(2, 1, DFF)


def _local_step(x, tgt, mod, wts, small, comm=None):
    wts = dict(wts)
    s = x.shape[0]
    ts, tq, tr, tp = TS, TQ, TR, TP
    tkw = min(TKW, s)
    zb = lambda n: jnp.zeros((1, n), F32)
    m6 = mod.reshape(2, 6, 1, D)
    sh1, sc1, g1, sh2, sc2, g2 = ([m6[i, k] for i in range(2)] for k in range(6))
    n1g, n2g = small["norm1_g"], small["norm2_g"]
    row = lambda a, i: a[i:i + 1]

    qg2 = jnp.tile(small["fox_q_gain"], (1, 2))
    kg2 = jnp.tile(small["fox_k_gain"], (1, 2))
    bfp = jnp.pad(small["fox_b_f"], ((0, 0), (0, LANES - H)))
    proj, h1 = _norm_mod_matmul(x, row(n1g, 0), sc1[0], sh1[0], wts["fox_w_in"], zb(FOX_NP), F32, ts, 1408, "fox_in")
    qa, ka, va = _fox_post(proj, qg2, kg2, bfp, tp, "fox_post")
    att, ql, gathered = _attn_fwd(qa, ka, va, tq, "attn_fwd", shards=comm["shards"] if comm else ())
    if comm:
        wts.update(comm["make_wts"](gathered))
    gated = _gate(att, proj, ts, "fox_gate")
    x1, y0 = _matmul_residual(gated, wts["fox_w_out"], x, g1[0], ts, "fox_out")

    def ffn_fwd(xin, i, tag):
        cw, cb = _conv_planes(small["ffn_conv_w"][i], small["ffn_conv_b"][i])
        a, h = _norm_mod_matmul(xin, row(n2g, i), sc2[i], sh2[i], wts["ffn_w_up"][i], zb(2 * DFF), BF16, ts, 1408,
                                "ffn_up" + tag, planes=2)
        f, ap = _conv_gate(a, cw, cb, min(2 * ts, s), "ffn_conv" + tag)
        xo, y = _matmul_residual(f, wts["ffn_w_down"][i], xin, g2[i], ts, "ffn_down" + tag)
        return xo, (a, h, f, y, cw, ap)

    x2, ffn0 = ffn_fwd(x1, 0, "0")

    bst = jnp.pad(small["sgu_b_s"].T, ((0, 0), (0, LANES - SGG)))
    ws = small["sgu_w_s"]
    z, h3 = _norm_mod_matmul(x2, row(n1g, 1), sc1[1], sh1[1], wts["sgu_w_in"], small["sgu_b_in"], BF16, ts, 1024,
                             "sgu_in")
    yy = _sgu_fwd(z, small["sgu_v_gain"], small["sgu_v_bias"], ws, bst, tr, "sgu_mix")
    x3, y1 = _matmul_residual(yy, wts["sgu_w_out"], x2, g1[1], ts, "sgu_out")
    x4, ffn1 = ffn_fwd(x3, 1, "1")

    lsum, dx4, dy, redf = _final_loss(x4, small["final_g"], tgt, g2[1], ffn1[3], ts, "final_loss")
    grads = {"final_g": redf[0]}
    dmod = [[None] * 6, [None] * 6]
    dmod[1][5] = redf[1]

    def ffn_bwd(dxo, dy2, xin, i, saved, gprev, yprev, tag):
        a, h, f, _, cw, ap = saved
        wd, wu = wts["ffn_w_down"][i], wts["ffn_w_up"][i]
        g_wd = _matmul(f, dy2, True, False, 1408, D, tkw, BF16, "ffn_dwdown" + tag)
        df = _matmul_wt(dy2, wd, 1408, D, BF16, ts, "ffn_df" + tag)
        da, redc = _conv_gate_bwd(a, ap, df, cw, ts, "ffn_conv_bwd" + tag)
        g_wu = _matmul(h, da, True, False, D, 1408, tkw, BF16, "ffn_dwup" + tag, out_parts=N_CHIP)
        dh = _matmul_wt(da, wu, D, 1408, F32, ts, "ffn_dh" + tag)
        outs = _norm_bwd(xin, dh, dxo, row(n2g, i), sc2[i], gprev, yprev, ts, "ffn_norm_bwd" + tag)
        return outs, g_wd, g_wu, redc

    (dx3, dy1, red), g_wd1, g_wu1, redc1 = ffn_bwd(dx4, dy, x3, 1, ffn1, g1[1], y1, "1")
    dmod[1][3], dmod[1][4], dn2g1, dmod[1][2] = red[0], red[1], red[2], red[3]

    g_swo = _matmul(yy, dy1, True, False, 1024, D, tkw, BF16, "sgu_dwout")
    dyy = _matmul_wt(dy1, wts["sgu_w_out"], 1024, D, BF16, ts, "sgu_dyy")
    wst = jnp.swapaxes(ws, 1, 2)
    dz, rb, rv, dws, dbst = _sgu_bwd(z, dyy, small["sgu_v_gain"], small["sgu_v_bias"], ws, wst, bst, tr, "sgu_mix_bwd")
    g_swi = _matmul(h3, dz, True, False, D, 1024, tkw, BF16, "sgu_dwin", out_parts=N_CHIP)
    dh3 = _matmul_wt(dz, wts["sgu_w_in"], D, 1024, F32, ts, "sgu_dh")
    dx2, dy2_0, red = _norm_bwd(x2, dh3, dx3, row(n1g, 1), sc1[1], g2[0], ffn0[3], ts, "sgu_norm_bwd")
    dmod[1][0], dmod[1][1], dn1g1, dmod[0][5] = red[0], red[1], red[2], red[3]

    (dx1, dy0, red), g_wd0, g_wu0, redc0 = ffn_bwd(dx2, dy2_0, x1, 0, ffn0, g1[0], y0, "0")
    dmod[0][3], dmod[0][4], dn2g0, dmod[0][2] = red[0], red[1], red[2], red[3]

    g_fwo = _matmul(gated, dy0, True, False, D, D, tkw, BF16, "fox_dwout")
    dgated = _matmul_wt(dy0, wts["fox_w_out"], D, D, F32, ts, "fox_dgated")
    doa, dop = _attn_bwd_prep(dgated, att, proj, ts, "attn_bwd_prep")
    css = comm["rs_prepare"]([g_fwo, g_swi, g_swo, g_wu0, g_wu1, g_wd0, g_wd1]) if comm else []
    dqa, dka, dva, rcvs = _attn_bwd(ql, ka, va, doa, tq, "attn_bwd", css=css)
    dproj, redx = _fox_post_bwd(proj, dqa, dka, dva, dop, qg2, kg2, bfp, tp, "fox_post_bwd")
    g_fwi = _matmul(h1, dproj, True, False, D, 1408, tkw, BF16, "fox_dwin")
    dh1 = _matmul_wt(dproj, wts["fox_w_in"], D, 1408, F32, ts, "fox_dh")
    dx0, red = _norm_bwd(x, dh1, dx1, row(n1g, 0), sc1[0], None, None, ts, "fox_norm_bwd")
    dmod[0][0], dmod[0][1], dn1g0 = red[0], red[1], red[2]

    grads.update(
        fox_w_in=g_fwi, fox_w_out=g_fwo, sgu_w_in=g_swi, sgu_w_out=g_swo,
        ffn_w_up=[g_wu0, g_wu1], ffn_w_down=[g_wd0, g_wd1],
        fox_q_gain=redx[0, :DH] + redx[0, DH:], fox_k_gain=redx[1, :DH] + redx[1, DH:], fox_b_f=redx[2, :H],
        sgu_b_in=rb[0], sgu_v_gain=rv[0], sgu_v_bias=rv[1], sgu_w_s=dws, sgu_b_s=dbst[:, :SGG].T,
        ffn_conv_w=jnp.stack([jnp.swapaxes(r[:, 0:3], 0, 1).reshape(3, 2 * DFF) for r in (redc0, redc1)]),
        ffn_conv_b=jnp.stack([r[:, 3].reshape(2 * DFF) for r in (redc0, redc1)]),
        norm1_g=jnp.stack([dn1g0, dn1g1]), norm2_g=jnp.stack([dn2g0, dn2g1]),
    )
    dmod_arr = jnp.stack([jnp.concatenate(dmod[0]), jnp.concatenate(dmod[1])])
    return lsum[0, 0], dx0, grads, dmod_arr, (css, rcvs)


N_DEV = 8
N_CHIP = 4
HBM_SPEC = pl.BlockSpec(memory_space=pltpu.HBM)
VMEM_SPEC = pl.BlockSpec(memory_space=pltpu.VMEM)


def _mesh_pos():
    return lax.axis_index("x"), lax.axis_index("y"), lax.axis_index("c")


def _other_chips(x, y):
    return [(1 - x, y), (x, 1 - y), (1 - x, 1 - y)]


def _remote(src, dst, ssem, rsem, dev):
    return pltpu.make_async_remote_copy(src_ref=src, dst_ref=dst, send_sem=ssem, recv_sem=rsem,
                                        device_id=dev, device_id_type=MESH)


def _allgather8(xb, name):
    m_per, n = xb.shape

    def body(x_ref, out_ref, send_sems, recv_sems, local_sem):
        x, y, c = _mesh_pos()
        me, sibling = (x, y, c), (x, y, 1 - c)
        chips = _other_chips(x, y)

        def rows(px, py, pc):
            return out_ref.at[pl.ds((4 * px + 2 * py + pc) * m_per, m_per), :]

        def copy(k, block, to, src=None):
            return _remote(rows(*block) if src is None else src, rows(*block),
                           send_sems.at[k], recv_sems.at[k], to)

        mine = pltpu.make_async_copy(x_ref, rows(*me), local_sem)
        mine.start()
        first = [copy(0, me, sibling, src=x_ref)]
        first += [copy(1 + j, me, (*chip, c), src=x_ref) for j, chip in enumerate(chips)]
        for cp in first:
            cp.start()
        passed = [copy(4 + j, (*chip, c), sibling) for j, chip in enumerate(chips)]
        for j, chip in enumerate(chips):
            copy(1 + j, (*chip, c), me).wait_recv()
            passed[j].start()
        copy(0, sibling, me).wait_recv()
        for j, chip in enumerate(chips):
            copy(4 + j, (*chip, 1 - c), me).wait_recv()
        for cp in first + passed:
            cp.wait_send()
        mine.wait()

    return pl.pallas_call(
        body, name=name,
        out_shape=jax.ShapeDtypeStruct((N_DEV * m_per, n), xb.dtype),
        in_specs=[VMEM_SPEC], out_specs=VMEM_SPEC,
        scratch_shapes=[pltpu.SemaphoreType.DMA((7,)), pltpu.SemaphoreType.DMA((7,)), pltpu.SemaphoreType.DMA],
        compiler_params=pltpu.CompilerParams(vmem_limit_bytes=V7X_VMEM_LIMIT),
    )(xb)


def _gather_shards(shards, name):
    na = len(shards)

    def body(*refs):
        p_refs, o_refs = refs[:na], refs[na:2 * na]
        send_sems, recv_sems, pass_send, pass_recv = refs[2 * na:]
        x, y, c = _mesh_pos()
        me = 2 * x + y
        sibling = (x, y, 1 - c)
        chips = _other_chips(x, y)

        def half(a, ci, hf):
            rh = shards[a].shape[0] // 2
            return o_refs[a].at[ci, pl.ds(hf * rh, rh), :]

        sends = []
        for a in range(na):
            rh = shards[a].shape[0] // 2
            for k, chip in enumerate(chips):
                sends.append(_remote(p_refs[a].at[pl.ds(c * rh, rh), :], half(a, me, c),
                                     send_sems.at[3 * a + k], recv_sems.at[3 * a + k], (*chip, c)))
        for cp in sends:
            cp.start()
        passed = []
        for a in range(na):
            for k, chip in enumerate(chips):
                ci = 2 * chip[0] + chip[1]
                _remote(half(a, ci, c), half(a, ci, c), send_sems.at[3 * a + k], recv_sems.at[3 * a + k],
                        (*chip, c)).wait_recv()
                cp = _remote(half(a, ci, c), half(a, ci, c), pass_send.at[3 * a + k], pass_recv.at[3 * a + k], sibling)
                cp.start()
                passed.append(cp)
        for a in range(na):
            for k, chip in enumerate(chips):
                ci = 2 * chip[0] + chip[1]
                _remote(half(a, ci, 1 - c), half(a, ci, 1 - c), pass_send.at[3 * a + k], pass_recv.at[3 * a + k],
                        sibling).wait_recv()
        for cp in sends + passed:
            cp.wait_send()

    return pl.pallas_call(
        body, name=name,
        out_shape=[jax.ShapeDtypeStruct((N_CHIP,) + p.shape, p.dtype) for p in shards],
        in_specs=[HBM_SPEC] * na, out_specs=[HBM_SPEC] * na,
        scratch_shapes=[pltpu.SemaphoreType.DMA((3 * na,))] * 4,
    )(*shards)


def _rs_to_sibling(gs, name):
    na = len(gs)

    def body(*refs):
        g_refs, o_refs, ssems, rsems = refs[:na], refs[na:2 * na], refs[2 * na], refs[2 * na + 1]
        x, y, c = _mesh_pos()
        cps = []
        for a in range(na):
            rh = gs[a].shape[1] // 2
            cp = _remote(g_refs[a].at[:, pl.ds((1 - c) * rh, rh), :], o_refs[a], ssems.at[a], rsems.at[a],
                         (x, y, 1 - c))
            cp.start()
            cps.append(cp)
        for cp in cps:
            cp.wait()

    return pl.pallas_call(
        body, name=name,
        out_shape=[jax.ShapeDtypeStruct((g.shape[0], g.shape[1] // 2, g.shape[2]), g.dtype) for g in gs],
        in_specs=[HBM_SPEC] * na, out_specs=[HBM_SPEC] * na,
        scratch_shapes=[pltpu.SemaphoreType.DMA((na,)), pltpu.SemaphoreType.DMA((na,))],
    )(*gs)


def _rs_chip_sum(g, sib, c_arr, tr, name):
    nc, r, n = g.shape
    rh = r // 2
    g4 = g.reshape(nc, 2, rh, n)

    def body(c_ref, g_ref, s_ref, o_ref):
        o_ref[...] = (g_ref[0].astype(F32) + s_ref[...].astype(F32)).astype(BF16)

    return pl.pallas_call(
        body, name=name, out_shape=jax.ShapeDtypeStruct((nc, rh, n), BF16),
        grid_spec=pltpu.PrefetchScalarGridSpec(
            num_scalar_prefetch=1, grid=(nc, rh // tr),
            in_specs=[pl.BlockSpec((1, 1, tr, n), lambda j, i, cr: (j, cr[0], i, 0)),
                      pl.BlockSpec((1, tr, n), lambda j, i, cr: (j, i, 0))],
            out_specs=pl.BlockSpec((1, tr, n), lambda j, i, cr: (j, i, 0))),
        compiler_params=_cparams(("arbitrary", "arbitrary")),
    )(c_arr, g4, sib)


def _rs_across_chips(css, name):
    na = len(css)

    def body(*refs):
        cs_refs, o_refs, send_sems, recv_sems = refs[:na], refs[na:2 * na], refs[2 * na], refs[2 * na + 1]
        x, y, c = _mesh_pos()
        cps = []
        for a in range(na):
            for k, chip in enumerate(_other_chips(x, y)):
                ci = 2 * chip[0] + chip[1]
                cp = _remote(cs_refs[a].at[ci], o_refs[a].at[k], send_sems.at[3 * a + k], recv_sems.at[3 * a + k],
                             (*chip, c))
                cp.start()
                cps.append(cp)
        for cp in cps:
            cp.wait()

    return pl.pallas_call(
        body, name=name, out_shape=[jax.ShapeDtypeStruct((3,) + cs.shape[1:], cs.dtype) for cs in css],
        in_specs=[HBM_SPEC] * na, out_specs=[HBM_SPEC] * na,
        scratch_shapes=[pltpu.SemaphoreType.DMA((3 * na,)), pltpu.SemaphoreType.DMA((3 * na,))],
    )(*css)


def _rs_final_sum(cs, rcv, me_arr, tr, name):
    nc, rh, n = cs.shape

    def body(m_ref, c_ref, r_ref, o_ref):
        acc = c_ref[0].astype(F32)
        for k in range(3):
            acc = acc + r_ref[k].astype(F32)
        o_ref[...] = acc

    return pl.pallas_call(
        body, name=name, out_shape=jax.ShapeDtypeStruct((rh, n), F32),
        grid_spec=pltpu.PrefetchScalarGridSpec(
            num_scalar_prefetch=1, grid=(rh // tr,),
            in_specs=[pl.BlockSpec((1, tr, n), lambda i, mr: (mr[0], i, 0)),
                      pl.BlockSpec((3, tr, n), lambda i, mr: (0, i, 0))],
            out_specs=pl.BlockSpec((tr, n), lambda i, mr: (i, 0))),
        compiler_params=_cparams(("arbitrary",)),
    )(me_arr, cs, rcv)


def _rs_swap_halves(halves, name):
    na = len(halves)

    def body(*refs):
        h_refs, o_refs, ssems, rsems = refs[:na], refs[na:2 * na], refs[2 * na], refs[2 * na + 1]
        x, y, c = _mesh_pos()
        cps = []
        for a in range(na):
            cp = _remote(h_refs[a], o_refs[a], ssems.at[a], rsems.at[a], (x, y, 1 - c))
            cp.start()
            cps.append(cp)
        for cp in cps:
            cp.wait()

    return pl.pallas_call(
        body, name=name, out_shape=[jax.ShapeDtypeStruct(h.shape, h.dtype) for h in halves],
        in_specs=[HBM_SPEC] * na, out_specs=[HBM_SPEC] * na,
        scratch_shapes=[pltpu.SemaphoreType.DMA((na,)), pltpu.SemaphoreType.DMA((na,))],
    )(*halves)


def _join_columns(parts, n_out, name):
    p, k, c = parts.shape
    tr = 128

    def body(w_ref, o_ref):
        for j in range(p):
            o_ref[:, j * c:(j + 1) * c] = w_ref[j]
        o_ref[:, p * c:] = jnp.zeros((tr, n_out - p * c), parts.dtype)

    return pl.pallas_call(
        body, name=name, grid=(k // tr,),
        in_specs=[pl.BlockSpec((p, tr, c), lambda i: (0, i, 0))],
        out_specs=pl.BlockSpec((tr, n_out), lambda i: (i, 0)),
        out_shape=jax.ShapeDtypeStruct((k, n_out), parts.dtype),
        compiler_params=_cparams(("arbitrary",)),
    )(parts)


def _split_columns(g, p, c, name):
    k, n = g.shape
    tr = 128

    def body(g_ref, o_ref):
        for j in range(p):
            o_ref[j] = g_ref[:, j * c:(j + 1) * c]

    return pl.pallas_call(
        body, name=name, grid=(k // tr,),
        in_specs=[pl.BlockSpec((tr, n), lambda i: (i, 0))],
        out_specs=pl.BlockSpec((p, tr, c), lambda i: (0, i, 0)),
        out_shape=jax.ShapeDtypeStruct((p, k, c), g.dtype),
        compiler_params=_cparams(("arbitrary",)),
    )(g)


def _sum8(g, name):
    nd, r, n = g.shape

    def body(g_ref, o_ref):
        acc = g_ref[0]
        for k in range(1, nd):
            acc = acc + g_ref[k]
        o_ref[...] = acc

    return pl.pallas_call(
        body, name=name, grid=(r // 8,),
        in_specs=[pl.BlockSpec((nd, 8, n), lambda i: (0, i, 0))],
        out_specs=pl.BlockSpec((8, n), lambda i: (i, 0)),
        out_shape=jax.ShapeDtypeStruct((r, n), F32),
        compiler_params=_cparams(("arbitrary",)),
    )(g)


def _adamw(w, g, m, v, name):
    r, n = w.shape
    tr = 128 if r % 128 == 0 else 8
    bc1 = 1.0 - ADAM_B1 ** ADAM_STEP
    bc2 = 1.0 - ADAM_B2 ** ADAM_STEP

    def body(w_ref, g_ref, m_ref, v_ref, d_ref, mo_ref, vo_ref):
        gv = g_ref[...]
        mn = ADAM_B1 * m_ref[...] + (1.0 - ADAM_B1) * gv
        vn = ADAM_B2 * v_ref[...] + (1.0 - ADAM_B2) * (gv * gv)
        d_ref[...] = -ADAM_LR * ((mn / bc1) / (jnp.sqrt(vn / bc2) + ADAM_EPS) + ADAM_WD * w_ref[...])
        mo_ref[...] = mn
        vo_ref[...] = vn

    blk = pl.BlockSpec((tr, n), lambda i: (i, 0))
    shp = jax.ShapeDtypeStruct((r, n), F32)
    return pl.pallas_call(
        body, name=name, grid=(r // tr,), in_specs=[blk] * 4, out_specs=[blk] * 3, out_shape=[shp] * 3,
        compiler_params=_cparams(("arbitrary",)),
    )(w, g, m, v)


ROW = 1024
PACK_ROWS = 7168
BIG = ("fox_w_in", "fox_w_out", "sgu_w_in", "sgu_w_out", "ffn_w_up", "ffn_w_down")
SMALL_SHARDED = ("sgu_b_in", "sgu_v_gain", "sgu_v_bias", "ffn_conv_w")
SMALL_REPL = ("fox_b_f", "fox_q_gain", "fox_k_gain", "sgu_w_s", "sgu_b_s", "ffn_conv_b", "ada_b",
              "norm1_g", "norm2_g", "final_g")
WEIGHTS = ("fox_w_in", "fox_b_f", "fox_q_gain", "fox_k_gain", "fox_w_out", "sgu_w_in", "sgu_b_in", "sgu_v_gain",
           "sgu_v_bias", "sgu_w_s", "sgu_b_s", "sgu_w_out", "ffn_w_up", "ffn_conv_w", "ffn_conv_b", "ffn_w_down",
           "ada_w", "ada_b", "norm1_g", "norm2_g", "final_g")


def _rows_of(a, mult=1):
    flat = a.reshape(-1)
    rows = -(-flat.shape[0] // ROW)
    rows = -(-rows // mult) * mult
    return jnp.pad(flat, (0, rows * ROW - flat.shape[0])).reshape(rows, ROW)


def _pack(parts, mult, total=None):
    p = jnp.concatenate([_rows_of(a, mult) for a in parts], axis=0)
    if total is not None:
        p = jnp.pad(p, ((0, total - p.shape[0]), (0, 0)))
    return p


def _unpack(pack, shapes, mult):
    out, r0 = [], 0
    for shp in shapes:
        size = int(np.prod(shp))
        rows = -(-(-(-size // ROW)) // mult) * mult
        out.append(pack[r0:r0 + rows].reshape(-1)[:size].reshape(shp))
        r0 += rows
    return out


def _big_shards(t):
    return [t["fox_w_in"][0], t["fox_w_out"][0], t["sgu_w_in"][0], t["sgu_w_out"][0],
            t["ffn_w_up"][0], t["ffn_w_up"][1], t["ffn_w_down"][0], t["ffn_w_down"][1]]


def _row_tile(rows):
    return next(t for t in (512, 352, 256, 128, 64) if rows % t == 0)


def kernel(x, c, fox_w_in, fox_b_f, fox_q_gain, fox_k_gain, fox_w_out, sgu_w_in, sgu_b_in, sgu_v_gain, sgu_v_bias, sgu_w_s, sgu_b_s, sgu_w_out, ffn_w_up, ffn_conv_w, ffn_conv_b, ffn_w_down, ada_w, ada_b, norm1_g, norm2_g, final_g, loss_target, m_fox_w_in, m_fox_b_f, m_fox_q_gain, m_fox_k_gain, m_fox_w_out, m_sgu_w_in, m_sgu_b_in, m_sgu_v_gain, m_sgu_v_bias, m_sgu_w_s, m_sgu_b_s, m_sgu_w_out, m_ffn_w_up, m_ffn_conv_w, m_ffn_conv_b, m_ffn_w_down, m_ada_w, m_ada_b, m_norm1_g, m_norm2_g, m_final_g, v_fox_w_in, v_fox_b_f, v_fox_q_gain, v_fox_k_gain, v_fox_w_out, v_sgu_w_in, v_sgu_b_in, v_sgu_v_gain, v_sgu_v_bias, v_sgu_w_s, v_sgu_b_s, v_sgu_w_out, v_ffn_w_up, v_ffn_conv_w, v_ffn_conv_b, v_ffn_w_down, v_ada_w, v_ada_b, v_norm1_g, v_norm2_g, v_final_g):
    w = dict(fox_w_in=fox_w_in, fox_b_f=fox_b_f, fox_q_gain=fox_q_gain, fox_k_gain=fox_k_gain, fox_w_out=fox_w_out,
             sgu_w_in=sgu_w_in, sgu_b_in=sgu_b_in, sgu_v_gain=sgu_v_gain, sgu_v_bias=sgu_v_bias, sgu_w_s=sgu_w_s,
             sgu_b_s=sgu_b_s, sgu_w_out=sgu_w_out, ffn_w_up=ffn_w_up, ffn_conv_w=ffn_conv_w, ffn_conv_b=ffn_conv_b,
             ffn_w_down=ffn_w_down, ada_w=ada_w, ada_b=ada_b, norm1_g=norm1_g, norm2_g=norm2_g, final_g=final_g)
    mom = dict(fox_w_in=m_fox_w_in, fox_b_f=m_fox_b_f, fox_q_gain=m_fox_q_gain, fox_k_gain=m_fox_k_gain,
               fox_w_out=m_fox_w_out, sgu_w_in=m_sgu_w_in, sgu_b_in=m_sgu_b_in, sgu_v_gain=m_sgu_v_gain,
               sgu_v_bias=m_sgu_v_bias, sgu_w_s=m_sgu_w_s, sgu_b_s=m_sgu_b_s, sgu_w_out=m_sgu_w_out,
               ffn_w_up=m_ffn_w_up, ffn_conv_w=m_ffn_conv_w, ffn_conv_b=m_ffn_conv_b, ffn_w_down=m_ffn_w_down,
               ada_w=m_ada_w, ada_b=m_ada_b, norm1_g=m_norm1_g, norm2_g=m_norm2_g, final_g=m_final_g)
    var = dict(fox_w_in=v_fox_w_in, fox_b_f=v_fox_b_f, fox_q_gain=v_fox_q_gain, fox_k_gain=v_fox_k_gain,
               fox_w_out=v_fox_w_out, sgu_w_in=v_sgu_w_in, sgu_b_in=v_sgu_b_in, sgu_v_gain=v_sgu_v_gain,
               sgu_v_bias=v_sgu_v_bias, sgu_w_s=v_sgu_w_s, sgu_b_s=v_sgu_b_s, sgu_w_out=v_sgu_w_out,
               ffn_w_up=v_ffn_w_up, ffn_conv_w=v_ffn_conv_w, ffn_conv_b=v_ffn_conv_b, ffn_w_down=v_ffn_w_down,
               ada_w=v_ada_w, ada_b=v_ada_b, norm1_g=v_norm1_g, norm2_g=v_norm2_g, final_g=v_final_g)

    ax, ay, ac = _mesh_pos()
    chip = 2 * ax + ay
    dev = 2 * chip + ac

    small_shard_shapes = tuple(w[n].shape for n in SMALL_SHARDED)
    blk = _pack([c] + [w[n] for n in SMALL_SHARDED], 1, 16)
    gat = _allgather8(blk, "gather_small").reshape(N_DEV, 16, ROW)
    c_all = gat[:, 0, :]
    per_chip = [_unpack(gat[2 * j, 1:], small_shard_shapes, 1) for j in range(N_CHIP)]
    full_small = {n: jnp.concatenate([per_chip[j][i] for j in range(N_CHIP)], axis=-1)
                  for i, n in enumerate(SMALL_SHARDED)}

    mine = [a.astype(BF16) for a in _big_shards(w)]
    with_own = lambda gat, own: [lax.dynamic_update_slice(g_, m_[None], (chip, 0, 0)) for g_, m_ in zip(gat, own)]
    fwi, = with_own(_gather_shards(mine[:1], "gather_fox_w_in"), mine[:1])
    fwi_full = _join_columns(fwi, FOX_NP, "join_fox_w_in")
    wts = dict(fox_w_in=fwi_full)

    def make_wts(gathered):
        fwo, swi, swo, up0, up1, dn0, dn1 = with_own(gathered, mine[1:])
        return dict(fox_w_out=fwo.reshape(D, D), sgu_w_in=swi, sgu_w_out=swo.reshape(SGW, D),
                    ffn_w_up=[up0, up1], ffn_w_down=[dn0.reshape(DFF, D), dn1.reshape(DFF, D)])

    c_arr = jnp.reshape(ac, (1,)).astype(jnp.int32)
    me_arr = jnp.reshape(chip, (1,)).astype(jnp.int32)

    def chip_sums(glist, tag):
        sibs = _rs_to_sibling(glist, "rs_sibling" + tag)
        return [_rs_chip_sum(g_, s_, c_arr, _row_tile(s_.shape[1]), "rs_chip_sum%s%d" % (tag, a))
                for a, (g_, s_) in enumerate(zip(glist, sibs))]

    def rs_prepare(gl):
        g_fwo, g_swi, g_swo, g_wu0, g_wu1, g_wd0, g_wd1 = gl
        return chip_sums([g_fwo.reshape(N_CHIP, 256, D), g_swi, g_swo.reshape(N_CHIP, 512, D), g_wu0, g_wu1,
                          g_wd0.reshape(N_CHIP, 704, D), g_wd1.reshape(N_CHIP, 704, D)], "")

    comm = dict(shards=mine[1:], make_wts=make_wts, rs_prepare=rs_prepare)

    da = ada_w.shape[2]
    ada_b_cols = lax.dynamic_slice_in_dim(ada_b, chip * da, da, axis=1)[:, None, :]
    mod_cols, c_act = _ada_mod(c_all, ada_w, ada_b_cols)
    mod_all = _allgather8(mod_cols.reshape(-1, ROW), "gather_mod").reshape(N_DEV, 2, N_DEV, da)
    mod_mine = lax.dynamic_index_in_dim(mod_all[0::2], dev, axis=2, keepdims=False)
    mod = jnp.swapaxes(mod_mine, 0, 1).reshape(2, N_CHIP * da)

    small = dict(norm1_g=norm1_g, norm2_g=norm2_g, final_g=final_g[None], fox_q_gain=fox_q_gain,
                 fox_k_gain=fox_k_gain, fox_b_f=fox_b_f, sgu_b_in=full_small["sgu_b_in"],
                 sgu_v_gain=full_small["sgu_v_gain"], sgu_v_bias=full_small["sgu_v_bias"], sgu_w_s=sgu_w_s[0],
                 sgu_b_s=sgu_b_s[0], ffn_conv_w=full_small["ffn_conv_w"], ffn_conv_b=ffn_conv_b)
    loss_dev, dx, g, dmod, (css, rcvs) = _local_step(x[0], loss_target[0], mod, wts, small, comm)

    g["ada_b"] = dmod
    g["loss"] = loss_dev
    small_names = ("ada_b",) + SMALL_SHARDED + tuple(n for n in SMALL_REPL if n != "ada_b") + ("loss",)
    gs = _pack([g[n] for n in small_names], 1)
    rows_s = -(-gs.shape[0] // 8) * 8
    gs = jnp.pad(gs, ((0, rows_s - gs.shape[0]), (0, 0)))
    gs_all = _allgather8(gs, "gather_small_grads").reshape(N_DEV, rows_s, ROW)
    gsum = _sum8(gs_all, "sum_small_grads")
    full_shapes = {n: w[n].shape for n in SMALL_REPL}
    full_shapes.update({n: w[n].shape[:-1] + (w[n].shape[-1] * N_CHIP,) for n in SMALL_SHARDED})
    full_shapes["loss"] = ()
    gfull = dict(zip(small_names, _unpack(gsum, [full_shapes[n] for n in small_names], 1)))
    grads = {n: gfull[n] for n in SMALL_REPL}
    for n in SMALL_SHARDED:
        width = w[n].shape[-1]
        grads[n] = lax.dynamic_slice_in_dim(gfull[n], chip * width, width, axis=gfull[n].ndim - 1)
    dmod_all = gs_all[:, :12, :].reshape(N_DEV, 2, N_CHIP * da)
    dmod_cols = jnp.swapaxes(lax.dynamic_slice_in_dim(dmod_all, chip * da, da, axis=2), 0, 1)
    grads["ada_w"] = _ada_w_grad(c_act.T, dmod_cols)

    gfi = _split_columns(g["fox_w_in"], N_CHIP, FOX_N // N_CHIP, "split_fox_w_in")
    cs_fox = chip_sums([gfi], "_fox")
    css = cs_fox + list(css)
    rcvs = list(_rs_across_chips(cs_fox, "rs_chips_fox")) + list(rcvs)
    halves = [_rs_final_sum(cs_, r_, me_arr, _row_tile(cs_.shape[1]), "rs_final_sum%d" % a)
              for a, (cs_, r_) in enumerate(zip(css, rcvs))]
    others = _rs_swap_halves(halves, "rs_swap")
    red = [jnp.concatenate([jnp.where(ac == 0, h_, o_), jnp.where(ac == 0, o_, h_)]) for h_, o_ in zip(halves, others)]
    grads.update(fox_w_in=red[0], fox_w_out=red[1], sgu_w_in=red[2], sgu_w_out=red[3],
                 ffn_w_up=jnp.stack([red[4], red[5]]), ffn_w_down=jnp.stack([red[6], red[7]]))

    delta, new_m, new_v = {}, {}, {}
    for n in BIG + ("ada_w",):
        shp = w[n].shape
        two_d = lambda a: a.reshape(-1, shp[-1])
        d_, m_, v_ = _adamw(two_d(w[n]), two_d(grads[n]), two_d(mom[n]), two_d(var[n]), "adamw_" + n)
        delta[n], new_m[n], new_v[n] = d_.reshape(shp), m_.reshape(shp), v_.reshape(shp)
    rest = SMALL_SHARDED + SMALL_REPL
    packs = [_pack([t[n] for n in rest], 1) for t in (w, grads, mom, var)]
    rows_r = -(-packs[0].shape[0] // 8) * 8
    packs = [jnp.pad(p, ((0, rows_r - p.shape[0]), (0, 0))) for p in packs]
    outs = _adamw(*packs, "adamw_small")
    for t, o in zip((delta, new_m, new_v), outs):
        t.update(zip(rest, _unpack(o, [w[n].shape for n in rest], 1)))

    loss = gfull["loss"]
    return (loss, dx[None], *[grads[n].reshape(w[n].shape) for n in WEIGHTS], *[delta[n] for n in WEIGHTS],
            *[new_m[n] for n in WEIGHTS], *[new_v[n] for n in WEIGHTS])
```

```python
import functools
import math

import numpy as np
import jax
import jax.numpy as jnp
from jax import lax
from jax.experimental import pallas as pl
from jax.experimental.pallas import tpu as pltpu

F32 = jnp.float32
BF16 = jnp.bfloat16
MESH = pl.DeviceIdType.MESH

D = 1024
H = 16
DH = 64
NP = H // 2
LANES = 128
DFF = 2816
SGW = 2048
SGG = 8
SGC = 256
SGB = 128
CHUNK = 64
EPS = 1e-6
FOX_N = 4 * D + H
FOX_NP = 4224
GT = 256
NGT = DFF // GT
SCALE = DH ** -0.5
LOG2E = 1.4426950408889634

ADAM_LR = 0.001
ADAM_B1 = 0.9
ADAM_B2 = 0.999
ADAM_EPS = 1e-08
ADAM_WD = 0.01
ADAM_STEP = 10

V7X_VMEM_LIMIT = 56 * 1024 * 1024

L_F = 64
L_NF = 67
L_LSE = 70


def _cparams(sem=None):
    return pltpu.CompilerParams(dimension_semantics=sem, vmem_limit_bytes=V7X_VMEM_LIMIT)


def _split3(x):
    hi = x.astype(BF16)
    r = x - hi.astype(F32)
    mid = r.astype(BF16)
    lo = (r - mid.astype(F32)).astype(BF16)
    return hi, mid, lo


def _dot(a, b, dims=(((1,), (0,)), ((), ()))):
    return lax.dot_general(a, b, dims, preferred_element_type=F32)


def _dot_nt(a, b):
    return _dot(a, b, (((1,), (1,)), ((), ())))


def _dot_tn(a, b):
    return _dot(a, b, (((0,), (0,)), ((), ())))


def _exact_dot(m_bf16, x_f32):
    hi, mid, lo = _split3(x_f32)
    return _dot(m_bf16, hi) + _dot(m_bf16, mid) + _dot(m_bf16, lo)


def _exact_dot_r(x_f32, m_bf16):
    hi, mid, lo = _split3(x_f32)
    return _dot(hi, m_bf16) + _dot(mid, m_bf16) + _dot(lo, m_bf16)


def _head_block_ones():
    r = lax.broadcasted_iota(jnp.int32, (LANES, LANES), 0) // DH
    c = lax.broadcasted_iota(jnp.int32, (LANES, LANES), 1) // DH
    return (r == c).astype(BF16)


def _sigmoid(x):
    return 1.0 / (1.0 + jnp.exp(-x))


def _gelu(x):
    c = math.sqrt(2.0 / math.pi)
    return 0.5 * x * (1.0 + jnp.tanh(c * (x + 0.044715 * (x * x * x))))


def _gelu_and_grad(x):
    c = math.sqrt(2.0 / math.pi)
    x2 = x * x
    t = jnp.tanh(c * (x + 0.044715 * (x2 * x)))
    half = 0.5 * (1.0 + t)
    return x * half, half + 0.5 * x * (1.0 - t * t) * c * (1.0 + 3 * 0.044715 * x2)


def _rstd_rows(x):
    return lax.rsqrt(jnp.mean(x * x, axis=-1, keepdims=True) + EPS)


def _norm_mod_matmul(x, ng, sc, sh, w, bias, out_dtype, ts, tn, name, planes=1):
    s, d = x.shape
    ns = w.shape[-1]
    n = w.shape[0] * ns if w.ndim == 3 else ns
    nc = n // planes

    def body(x_ref, ng_ref, sc_ref, sh_ref, w_ref, b_ref, o_ref, h_ref):
        xv = x_ref[...]
        h = (xv * _rstd_rows(xv) * ng_ref[...] * (1.0 + sc_ref[...]) + sh_ref[...]).astype(BF16)
        h_ref[...] = h
        for e in range(planes):
            for c0 in range(0, nc, tn):
                g0 = e * nc + c0
                wv = w_ref[g0 // ns, :, g0 % ns:g0 % ns + tn] if w.ndim == 3 else w_ref[:, g0:g0 + tn]
                val = (_dot(h, wv) + b_ref[:, g0:g0 + tn]).astype(out_dtype)
                if planes == 1:
                    o_ref[:, c0:c0 + tn] = val
                else:
                    o_ref[e, :, c0:c0 + tn] = val

    vec = pl.BlockSpec((1, d), lambda i: (0, 0))
    w_spec = (pl.BlockSpec(w.shape, lambda i: (0, 0, 0)) if w.ndim == 3 else pl.BlockSpec((d, n), lambda i: (0, 0)))
    if planes == 1:
        o_spec, o_shape = pl.BlockSpec((ts, n), lambda i: (i, 0)), (s, n)
    else:
        o_spec, o_shape = pl.BlockSpec((planes, ts, nc), lambda i: (0, i, 0)), (planes, s, nc)
    return pl.pallas_call(
        body, name=name, grid=(s // ts,),
        in_specs=[pl.BlockSpec((ts, d), lambda i: (i, 0)), vec, vec, vec, w_spec,
                  pl.BlockSpec((1, n), lambda i: (0, 0))],
        out_specs=[o_spec, pl.BlockSpec((ts, d), lambda i: (i, 0))],
        out_shape=[jax.ShapeDtypeStruct(o_shape, out_dtype), jax.ShapeDtypeStruct((s, d), BF16)],
        compiler_params=_cparams(("arbitrary",)),
    )(x, ng, sc, sh, w, bias)


def _matmul(a, b, ta, tb, tm, tn, tk, out_dtype, name, out_parts=1):
    if a.ndim == 3:
        m, k = a.shape[1], a.shape[0] * a.shape[2]
        nkp = a.shape[2] // tk
    else:
        m, k = (a.shape[1], a.shape[0]) if ta else a.shape
    if b.ndim == 3:
        n = b.shape[1] if tb else b.shape[0] * b.shape[2]
        nbp = b.shape[2] // (tk if tb else tn)
    else:
        n = b.shape[0] if tb else b.shape[1]
    nk = k // tk
    nop = n // out_parts // tn
    dims = (((0,) if ta else (1,), (1,) if tb else (0,)), ((), ()))

    def body(a_ref, b_ref, o_ref, acc):
        kk = pl.program_id(2)

        @pl.when(kk == 0)
        def _():
            acc[...] = jnp.zeros_like(acc)
        acc[...] += _dot(a_ref[...], b_ref[...], dims)

        @pl.when(kk == nk - 1)
        def _():
            o_ref[...] = acc[...].astype(out_dtype)

    if a.ndim == 3:
        a_spec = pl.BlockSpec((None, tm, tk), lambda i, j, kk: (kk // nkp, i, kk % nkp))
    else:
        a_spec = (pl.BlockSpec((tk, tm), lambda i, j, kk: (kk, i)) if ta
                  else pl.BlockSpec((tm, tk), lambda i, j, kk: (i, kk)))
    if b.ndim == 3 and tb:
        b_spec = pl.BlockSpec((None, tn, tk), lambda i, j, kk: (kk // nbp, j, kk % nbp))
    elif b.ndim == 3:
        b_spec = pl.BlockSpec((None, tk, tn), lambda i, j, kk: (j // nbp, kk, j % nbp))
    else:
        b_spec = (pl.BlockSpec((tn, tk), lambda i, j, kk: (j, kk)) if tb
                  else pl.BlockSpec((tk, tn), lambda i, j, kk: (kk, j)))
    if out_parts > 1:
        o_spec = pl.BlockSpec((None, tm, tn), lambda i, j, kk: (j // nop, i, j % nop))
        o_shape = (out_parts, m, n // out_parts)
    else:
        o_spec, o_shape = pl.BlockSpec((tm, tn), lambda i, j, kk: (i, j)), (m, n)
    return pl.pallas_call(
        body, name=name, grid=(m // tm, n // tn, nk),
        in_specs=[a_spec, b_spec],
        out_specs=o_spec,
        out_shape=jax.ShapeDtypeStruct(o_shape, out_dtype),
        scratch_shapes=[pltpu.VMEM((tm, tn), F32)],
        compiler_params=_cparams(("arbitrary", "arbitrary", "arbitrary")),
    )(a, b)


def _matmul_wt(a, w, tn, tk, out_dtype, ts, name):
    s = a.shape[-2]
    ka, kw = a.shape[-1], w.shape[-1]
    k = ka * (a.shape[0] if a.ndim == 3 else 1)
    n = w.shape[-2]

    def body(a_ref, w_ref, o_ref):
        for n0 in range(0, n, tn):
            acc = None
            for g0 in range(0, k, tk):
                av = a_ref[g0 // ka, :, g0 % ka:g0 % ka + tk] if a.ndim == 3 else a_ref[:, g0:g0 + tk]
                wv = (w_ref[g0 // kw, n0:n0 + tn, g0 % kw:g0 % kw + tk] if w.ndim == 3
                      else w_ref[n0:n0 + tn, g0:g0 + tk])
                part = _dot_nt(av, wv)
                acc = part if acc is None else acc + part
            o_ref[:, n0:n0 + tn] = acc.astype(out_dtype)

    a_spec = (pl.BlockSpec((a.shape[0], ts, ka), lambda i: (0, i, 0)) if a.ndim == 3
              else pl.BlockSpec((ts, ka), lambda i: (i, 0)))
    w_spec = pl.BlockSpec(w.shape, (lambda i: (0, 0, 0)) if w.ndim == 3 else (lambda i: (0, 0)))
    return pl.pallas_call(
        body, name=name, grid=(s // ts,),
        in_specs=[a_spec, w_spec], out_specs=pl.BlockSpec((ts, n), lambda i: (i, 0)),
        out_shape=jax.ShapeDtypeStruct((s, n), out_dtype),
        compiler_params=_cparams(("arbitrary",)),
    )(a, w)


def _matmul_residual(a, w, xin, g, ts, name):
    s, k = a.shape
    d = w.shape[1]

    def body(a_ref, w_ref, x_ref, g_ref, o_ref, y_ref):
        y = _dot(a_ref[...], w_ref[...])
        o_ref[...] = x_ref[...] + g_ref[...] * y
        y_ref[...] = y.astype(BF16)

    return pl.pallas_call(
        body, name=name, grid=(s // ts,),
        in_specs=[pl.BlockSpec((ts, k), lambda i: (i, 0)),
                  pl.BlockSpec((k, d), lambda i: (0, 0)),
                  pl.BlockSpec((ts, d), lambda i: (i, 0)),
                  pl.BlockSpec((1, d), lambda i: (0, 0))],
        out_specs=[pl.BlockSpec((ts, d), lambda i: (i, 0)), pl.BlockSpec((ts, d), lambda i: (i, 0))],
        out_shape=[jax.ShapeDtypeStruct((s, d), F32), jax.ShapeDtypeStruct((s, d), BF16)],
        compiler_params=_cparams(("arbitrary",)),
    )(a, w, xin, g)


def _lane(shape):
    return lax.broadcasted_iota(jnp.int32, shape, 1)


def _pair_norm(x, gain2, bones):
    msq = _exact_dot_r(x * x, bones) * (1.0 / DH)
    r = lax.rsqrt(msq + EPS)
    xh = x * r
    return xh * gain2, xh, r


def _fox_post(proj, qg2, kg2, bf, ts, name):
    s = proj.shape[0]

    def body(p_ref, qg_ref, kg_ref, bf_ref, q_ref, k_ref, v_ref, carry):
        @pl.when(pl.program_id(0) == 0)
        def _():
            carry[...] = jnp.zeros_like(carry)
        lane = _lane((ts, LANES))
        bones = _head_block_ones()
        xf = p_ref[:, 4 * D:4 * D + LANES] + bf_ref[...]
        logf = jnp.minimum(xf, 0.0) - jnp.log(1.0 + jnp.exp(-jnp.abs(xf)))
        logf = jnp.where(lane < H, logf, 0.0)
        rr = lax.broadcasted_iota(jnp.int32, (ts, ts), 0)
        cc = lax.broadcasted_iota(jnp.int32, (ts, ts), 1)
        ltri = (cc <= rr).astype(BF16)
        fcum = _exact_dot(ltri, logf) + carry[0:1, :]
        carry[0:1, :] = fcum[ts - 1:ts, :]
        fhi, fmid, flo = _split3(fcum * LOG2E)
        fhi, fmid, flo = fhi.astype(F32), fmid.astype(F32), flo.astype(F32)
        one_q = ((lane >= L_NF) & (lane < L_NF + 3)).astype(F32)
        one_k = (((lane >= L_F) & (lane < L_F + 3)) | ((lane >= L_LSE) & (lane < L_LSE + 3))).astype(F32)
        one_v = ((lane >= L_F) & (lane < L_F + 3)).astype(F32)
        for p in range(NP):
            qn, _, _ = _pair_norm(p_ref[:, p * LANES:(p + 1) * LANES], qg_ref[...], bones)
            kn, _, _ = _pair_norm(p_ref[:, D + p * LANES:D + (p + 1) * LANES], kg_ref[...], bones)
            vv = p_ref[:, 2 * D + p * LANES:2 * D + (p + 1) * LANES]
            qn = qn * (SCALE * LOG2E)
            for e in range(2):
                h = 2 * p + e
                if e == 1:
                    qe, ke, ve = (pltpu.roll(t, DH, axis=1) for t in (qn, kn, vv))
                else:
                    qe, ke, ve = qn, kn, vv
                f0, f1, f2 = fhi[:, h:h + 1], fmid[:, h:h + 1], flo[:, h:h + 1]
                fq = jnp.where(lane == L_F, f0, jnp.where(lane == L_F + 1, f1, jnp.where(lane == L_F + 2, f2, one_q)))
                fk = jnp.where(lane == L_NF, -f0, jnp.where(lane == L_NF + 1, -f1, jnp.where(lane == L_NF + 2, -f2, one_k)))
                q_ref[h] = jnp.where(lane < DH, qe, fq).astype(BF16)
                k_ref[h] = jnp.where(lane < DH, ke, fk).astype(BF16)
                v_ref[h] = jnp.where(lane < DH, ve, one_v).astype(BF16)

    hs = pl.BlockSpec((H, ts, LANES), lambda i: (0, i, 0))
    vec = pl.BlockSpec((1, LANES), lambda i: (0, 0))
    shp = jax.ShapeDtypeStruct((H, s, LANES), BF16)
    return pl.pallas_call(
        body, name=name, grid=(s // ts,),
        in_specs=[pl.BlockSpec((ts, FOX_NP), lambda i: (i, 0)), vec, vec, vec],
        out_specs=[hs, hs, hs], out_shape=[shp, shp, shp],
        scratch_shapes=[pltpu.VMEM((8, LANES), F32)],
        compiler_params=_cparams(("arbitrary",)),
    )(proj, qg2, kg2, bf)


def _gather_copies(p_refs, o_refs, send_sems, recv_sems):
    x, y, c = _mesh_pos()
    me = 2 * x + y
    sends, arrivals = [], []
    for a, (p_ref, o_ref) in enumerate(zip(p_refs, o_refs)):
        rh = p_ref.shape[0] // 2
        for k, chip in enumerate(_other_chips(x, y)):
            ci = 2 * chip[0] + chip[1]
            for cc in range(2):
                sends.append(_remote(p_ref.at[pl.ds(c * rh, rh), :], o_ref.at[me, pl.ds(c * rh, rh), :],
                                     send_sems.at[6 * a + 2 * k + cc], recv_sems.at[6 * a + 2 * k + c], (*chip, cc)))
                arrivals.append(_remote(o_ref.at[ci, pl.ds(cc * rh, rh), :], o_ref.at[ci, pl.ds(cc * rh, rh), :],
                                        send_sems.at[6 * a + 2 * k + cc], recv_sems.at[6 * a + 2 * k + cc],
                                        (*chip, cc)))
    return sends, arrivals


def _attn_fwd(qa, ka, va, tq, name, shards=()):
    s = qa.shape[1]
    nq = s // tq
    na = len(shards)
    hps = HPS_FWD

    def body(*refs):
        q_ref, k_ref, v_ref = refs[:3]
        p_refs = refs[3:3 + na]
        o_ref, ql_ref = refs[3 + na:5 + na]
        g_refs = refs[5 + na:5 + 2 * na]
        i = pl.program_id(1)
        if na:
            send_sems, recv_sems = refs[5 + 2 * na:]

            @pl.when((pl.program_id(0) == 0) & (i == 0))
            def _():
                for cp in _gather_copies(p_refs, g_refs, send_sems, recv_sems)[0]:
                    cp.start()
        lane = _lane((tq, LANES))
        qs_ = [q_ref[e] for e in range(hps)]

        tk = min(TK_FWD, tq)
        nks = tq // tk

        def step(j, carry, diag=None):
            off = pl.multiple_of(j * tk, tk)
            scs = [_dot_nt(qs_[e], k_ref[e, pl.ds(off, tk), :]) for e in range(hps)]
            probs = []
            for e in range(hps):
                m, sc = carry[e][0], scs[e]
                if diag is not None:
                    rr = lax.broadcasted_iota(jnp.int32, (tq, tk), 0)
                    cc = lax.broadcasted_iota(jnp.int32, (tq, tk), 1) + diag * tk
                    sc = jnp.where(cc <= rr, sc, -jnp.inf)
                m_new = jnp.maximum(m, jnp.max(sc, axis=-1, keepdims=True))
                probs.append((m_new, jnp.exp2(sc - m_new).astype(BF16), jnp.exp2(m - m_new)))
            return tuple((m_new, carry[e][1] * alpha + _dot(pr, v_ref[e, pl.ds(off, tk), :]))
                         for e, (m_new, pr, alpha) in enumerate(probs))

        one = (jnp.full((tq, 1), -jnp.inf, F32), jnp.zeros((tq, LANES), F32))
        carry = lax.fori_loop(0, i * nks, step, (one,) * hps)
        for r in range(nks):
            carry = step(i * nks + r, carry, diag=r)
        outs = []
        for e in range(hps):
            m, acc = carry[e]
            l = acc[:, L_F:L_F + 1]
            outs.append(acc / l)
            lse = m + jnp.log2(l)
            h0, h1, h2 = _split3(-lse)
            ql = jnp.where(lane == L_LSE, h0.astype(F32),
                           jnp.where(lane == L_LSE + 1, h1.astype(F32),
                                     jnp.where(lane == L_LSE + 2, h2.astype(F32), qs_[e].astype(F32))))
            ql_ref[e] = ql.astype(BF16)
        for e in range(0, hps, 2):
            o_ref[:, e * DH:(e + 2) * DH] = jnp.where(lane < DH, outs[e], pltpu.roll(outs[e + 1], DH, axis=1))
        if na:
            @pl.when((pl.program_id(0) == H // hps - 1) & (i == nq - 1))
            def _():
                sends, arrivals = _gather_copies(p_refs, g_refs, send_sems, recv_sems)
                for cp in arrivals:
                    cp.wait_recv()
                for cp in sends:
                    cp.wait_send()

    res = pl.BlockSpec((hps, s, LANES), lambda p, i: (p, 0, 0))
    qs = pl.BlockSpec((hps, tq, LANES), lambda p, i: (p, i, 0))
    outs = pl.pallas_call(
        body, name=name, grid=(H // hps, nq),
        in_specs=[qs, res, res] + [HBM_SPEC] * na,
        out_specs=[pl.BlockSpec((tq, hps * DH), lambda p, i: (i, p)), qs] + [HBM_SPEC] * na,
        out_shape=[jax.ShapeDtypeStruct((s, D), F32), jax.ShapeDtypeStruct((H, s, LANES), BF16)]
        + [jax.ShapeDtypeStruct((N_CHIP,) + p.shape, p.dtype) for p in shards],
        scratch_shapes=[pltpu.SemaphoreType.DMA((6 * na,))] * 2 if na else [],
        compiler_params=_cparams(("arbitrary", "arbitrary")),
    )(qa, ka, va, *shards)
    return outs[0], outs[1], list(outs[2:])


def _chip_exchange_copies(cs_refs, o_refs, send_sems, recv_sems):
    x, y, c = _mesh_pos()
    cps = []
    for a, (cs_ref, o_ref) in enumerate(zip(cs_refs, o_refs)):
        for k, chip in enumerate(_other_chips(x, y)):
            ci = 2 * chip[0] + chip[1]
            cps.append(_remote(cs_ref.at[ci], o_ref.at[k], send_sems.at[3 * a + k], recv_sems.at[3 * a + k],
                               (*chip, c)))
    return cps


def _attn_bwd(ql, ka, va, doa, tq, name, css=()):
    s = ql.shape[1]
    nq = s // tq
    na = len(css)

    def body(*refs):
        q_ref, k_ref, v_ref, do_ref = refs[:4]
        cs_refs = refs[4:4 + na]
        dqo_ref, dk_ref, dv_ref = refs[4 + na:7 + na]
        r_refs = refs[7 + na:7 + 2 * na]
        dq_ref = refs[7 + 2 * na]
        j = pl.program_id(1)
        if na:
            send_sems, recv_sems = refs[8 + 2 * na:]

            @pl.when((pl.program_id(0) == 0) & (j == 0))
            def _():
                for cp in _chip_exchange_copies(cs_refs, r_refs, send_sems, recv_sems):
                    cp.start()

        @pl.when(j == 0)
        def _():
            dq_ref[...] = jnp.zeros_like(dq_ref)
        lane = _lane((tq, LANES))
        kbs = [k_ref[0], k_ref[1]]
        vbs = [v_ref[0], v_ref[1]]

        def step(i, carry, masked):
            ioff = pl.multiple_of(i * tq, tq)
            qbs = [q_ref[e, pl.ds(ioff, tq), :] for e in range(2)]
            dobs = [do_ref[e, pl.ds(ioff, tq), :] for e in range(2)]
            scs = [_dot_nt(qbs[e], kbs[e]) for e in range(2)]
            dps = [_dot_nt(dobs[e], vbs[e]) for e in range(2)]
            prs, dss = [], []
            for e in range(2):
                pr = jnp.exp2(scs[e])
                if masked:
                    rr = lax.broadcasted_iota(jnp.int32, (tq, tq), 0)
                    cc = lax.broadcasted_iota(jnp.int32, (tq, tq), 1)
                    pr = jnp.where(cc <= rr, pr, 0.0)
                dss.append((pr * dps[e]).astype(BF16))
                prs.append(pr.astype(BF16))
            new = []
            for e in range(2):
                dk, dv = carry[e]
                dv = dv + _dot_tn(prs[e], dobs[e])
                dk = dk + _dot_tn(dss[e], qbs[e])
                dq_ref[e, pl.ds(ioff, tq), :] += _dot(dss[e], kbs[e])
                new.append((dk, dv))
            return tuple(new)

        zero = jnp.zeros((tq, LANES), F32)
        carry = step(j, ((zero, zero), (zero, zero)), True)
        carry = lax.fori_loop(j + 1, nq, functools.partial(step, masked=False), carry)
        for e in range(2):
            dk, dv = carry[e]
            col = dk[:, L_NF:L_NF + 1]
            hi = col.astype(BF16).astype(F32)
            dk_ref[e] = jnp.where(lane == L_NF, hi, jnp.where(lane == L_NF + 1, col - hi, dk)).astype(BF16)
            dv_ref[e] = dv.astype(BF16)

        @pl.when(j == nq - 1)
        def _():
            lane_s = _lane((s, LANES))
            for e in range(2):
                dq = dq_ref[e]
                col = dq[:, L_F:L_F + 1]
                hi = col.astype(BF16).astype(F32)
                dqo_ref[e] = jnp.where(lane_s == L_F, hi, jnp.where(lane_s == L_F + 1, col - hi, dq)).astype(BF16)
        if na:
            @pl.when((pl.program_id(0) == NP - 1) & (j == nq - 1))
            def _():
                for cp in _chip_exchange_copies(cs_refs, r_refs, send_sems, recv_sems):
                    cp.wait()

    res = pl.BlockSpec((2, s, LANES), lambda p, j: (p, 0, 0))
    tile = pl.BlockSpec((2, tq, LANES), lambda p, j: (p, j, 0))
    shp = jax.ShapeDtypeStruct((H, s, LANES), BF16)
    outs = pl.pallas_call(
        body, name=name, grid=(NP, nq),
        in_specs=[res, tile, tile, res] + [HBM_SPEC] * na, out_specs=[res, tile, tile] + [HBM_SPEC] * na,
        out_shape=[shp, shp, shp] + [jax.ShapeDtypeStruct((3,) + cs.shape[1:], cs.dtype) for cs in css],
        scratch_shapes=[pltpu.VMEM((2, s, LANES), F32)] + ([pltpu.SemaphoreType.DMA((3 * na,))] * 2 if na else []),
        compiler_params=_cparams(("arbitrary", "arbitrary")),
    )(ql, ka, va, doa, *css)
    return outs[0], outs[1], outs[2], list(outs[3:])


def _gate(att, proj, ts, name):
    s = att.shape[0]

    def body(a_ref, o_ref, g_ref):
        g_ref[...] = (a_ref[...] * _sigmoid(o_ref[...])).astype(BF16)

    return pl.pallas_call(
        body, name=name, grid=(s // ts,),
        in_specs=[pl.BlockSpec((ts, D), lambda i: (i, 0)), pl.BlockSpec((ts, D), lambda i: (i, 3))],
        out_specs=pl.BlockSpec((ts, D), lambda i: (i, 0)),
        out_shape=jax.ShapeDtypeStruct((s, D), BF16),
        compiler_params=_cparams(("arbitrary",)),
    )(att, proj)


def _attn_bwd_prep(dgated, att, proj, ts, name):
    s = att.shape[0]

    def body(dg_ref, a_ref, o_ref, doa_ref, dop_ref):
        lane = _lane((ts, LANES))
        bones = _head_block_ones()
        for p in range(NP):
            sl = slice(p * LANES, (p + 1) * LANES)
            dg, a = dg_ref[:, sl], a_ref[:, sl]
            sig = _sigmoid(o_ref[:, sl])
            datt = dg * sig
            dop_ref[:, sl] = (dg * a * sig * (1.0 - sig)).astype(BF16)
            delta = _exact_dot_r(datt * a, bones)
            for e in range(2):
                de, dl = (datt, delta) if e == 0 else (pltpu.roll(datt, DH, axis=1), pltpu.roll(delta, DH, axis=1))
                h0, h1, h2 = _split3(-dl[:, 0:1])
                aug = jnp.where(lane == L_F, h0.astype(F32),
                                jnp.where(lane == L_F + 1, h1.astype(F32),
                                          jnp.where(lane == L_F + 2, h2.astype(F32), 0.0)))
                doa_ref[2 * p + e] = jnp.where(lane < DH, de, aug).astype(BF16)

    row = pl.BlockSpec((ts, D), lambda i: (i, 0))
    return pl.pallas_call(
        body, name=name, grid=(s // ts,),
        in_specs=[row, row, pl.BlockSpec((ts, D), lambda i: (i, 3))],
        out_specs=[pl.BlockSpec((H, ts, LANES), lambda i: (0, i, 0)), row],
        out_shape=[jax.ShapeDtypeStruct((H, s, LANES), BF16), jax.ShapeDtypeStruct((s, D), BF16)],
        compiler_params=_cparams(("arbitrary",)),
    )(dgated, att, proj)


def _fox_post_bwd(proj, dqa, dka, dva, dop, qg2, kg2, bf, ts, name):
    s = proj.shape[0]
    nt = s // ts

    def body(p_ref, dq_ref, dk_ref, dv_ref, dop_ref, qg_ref, kg_ref, bf_ref, o_ref, red_ref, carry):
        @pl.when(pl.program_id(0) == 0)
        def _():
            carry[...] = jnp.zeros_like(carry)
            red_ref[...] = jnp.zeros_like(red_ref)
        lane = _lane((ts, LANES))
        bones = _head_block_ones()
        d_f = jnp.zeros((ts, LANES), F32)
        dqg = jnp.zeros((1, LANES), F32)
        dkg = jnp.zeros((1, LANES), F32)
        for p in range(NP):
            heads = [[ref[2 * p + e].astype(F32) for e in range(2)] for ref in (dq_ref, dk_ref, dv_ref)]
            pair = [jnp.where(lane < DH, a, pltpu.roll(b, DH, axis=1)) for a, b in heads]
            for e in range(2):
                dqe, dke = heads[0][e], heads[1][e]
                col = (dqe[:, L_F:L_F + 1] + dqe[:, L_F + 1:L_F + 2]
                       - dke[:, L_NF:L_NF + 1] - dke[:, L_NF + 1:L_NF + 2])
                d_f = jnp.where(lane == 2 * p + e, col, d_f)
            for idx, (g_ref, base) in enumerate(((qg_ref, 0), (kg_ref, D))):
                x = p_ref[:, base + p * LANES:base + (p + 1) * LANES]
                _, xh, r = _pair_norm(x, g_ref[...], bones)
                dn = pair[idx] * (SCALE if idx == 0 else 1.0 / LOG2E)
                t = dn * g_ref[...]
                mean_txh = _exact_dot_r(t * xh, bones) * (1.0 / DH)
                dx = r * (t - xh * mean_txh)
                o_ref[:, base + p * LANES:base + (p + 1) * LANES] = dx.astype(BF16)
                gsum = jnp.sum(dn * xh, axis=0, keepdims=True)
                if idx == 0:
                    dqg = dqg + gsum
                else:
                    dkg = dkg + gsum
            o_ref[:, 2 * D + p * LANES:2 * D + (p + 1) * LANES] = pair[2].astype(BF16)
        o_ref[:, 3 * D:4 * D] = dop_ref[...]
        rr = lax.broadcasted_iota(jnp.int32, (ts, ts), 0)
        cc = lax.broadcasted_iota(jnp.int32, (ts, ts), 1)
        utri = (cc >= rr).astype(BF16)
        dlogf = _exact_dot(utri, d_f) + carry[0:1, :]
        carry[0:1, :] = dlogf[0:1, :]
        xf = p_ref[:, 4 * D:4 * D + LANES] + bf_ref[...]
        dfl = jnp.where(lane < H, dlogf * _sigmoid(-xf), 0.0)
        o_ref[:, 4 * D:4 * D + LANES] = dfl.astype(BF16)
        red_ref[0:1, :] += dqg
        red_ref[1:2, :] += dkg
        red_ref[2:3, :] += jnp.sum(dfl, axis=0, keepdims=True)

    hs = pl.BlockSpec((H, ts, LANES), lambda i: (0, nt - 1 - i, 0))
    vec = pl.BlockSpec((1, LANES), lambda i: (0, 0))
    return pl.pallas_call(
        body, name=name, grid=(nt,),
        in_specs=[pl.BlockSpec((ts, FOX_NP), lambda i: (nt - 1 - i, 0)), hs, hs, hs,
                  pl.BlockSpec((ts, D), lambda i: (nt - 1 - i, 0)), vec, vec, vec],
        out_specs=[pl.BlockSpec((ts, FOX_NP), lambda i: (nt - 1 - i, 0)),
                   pl.BlockSpec((8, LANES), lambda i: (0, 0))],
        out_shape=[jax.ShapeDtypeStruct((s, FOX_NP), BF16), jax.ShapeDtypeStruct((8, LANES), F32)],
        scratch_shapes=[pltpu.VMEM((8, LANES), F32)],
        compiler_params=_cparams(("arbitrary",)),
    )(proj, dqa, dka, dva, dop, qg2, kg2, bf)


HALO = 16
TS = 512
TQ = 512
TR = 256
TP = 256
HPS_FWD = 4
TK_FWD = 512
TKW = 2048


def _shift_down(x, k):
    return pltpu.roll(x, k, axis=0)


def _shift_up(x, k):
    return pltpu.roll(x, x.shape[0] - k, axis=0)


def _planes(ref):
    return jnp.concatenate([ref[0].astype(F32), ref[1].astype(F32)], axis=1)


def _conv_gate(a, cw, cb, ts, name):
    s = a.shape[1]
    hb = ts // HALO

    def body(prev_ref, a_ref, cw_ref, cb_ref, f_ref, ap_ref):
        i = pl.program_id(0)
        cwv, cbv = _planes(cw_ref), _planes(cb_ref)
        prev = jnp.where(i > 0, _planes(prev_ref), 0.0)
        ext = jnp.concatenate([prev, _planes(a_ref)], axis=0)
        ap = (_shift_down(ext, 2) * cwv[0:1, :] + _shift_down(ext, 1) * cwv[1:2, :]
              + ext * cwv[2:3, :] + cbv)[HALO:, :]
        g, val = ap[:, :GT], ap[:, GT:]
        f_ref[...] = (g * _sigmoid(g) * val).astype(BF16)
        ap_ref[0] = g.astype(BF16)
        ap_ref[1] = val.astype(BF16)

    tile = pl.BlockSpec((2, ts, GT), lambda i, j: (0, i, j))
    return pl.pallas_call(
        body, name=name, grid=(s // ts, NGT),
        in_specs=[pl.BlockSpec((2, HALO, GT), lambda i, j: (0, jnp.maximum(i * hb - 1, 0), j)), tile,
                  pl.BlockSpec((2, 8, GT), lambda i, j: (0, 0, j)),
                  pl.BlockSpec((2, 1, GT), lambda i, j: (0, 0, j))],
        out_specs=[pl.BlockSpec((ts, GT), lambda i, j: (i, j)), tile],
        out_shape=[jax.ShapeDtypeStruct((s, DFF), BF16), jax.ShapeDtypeStruct((2, s, DFF), BF16)],
        compiler_params=_cparams(("arbitrary", "arbitrary")),
    )(a, a, cw, cb)


def _conv_gate_bwd(a, ap, df, cw, ts, name):
    s = a.shape[1]
    hb = ts // HALO
    nt = s // ts

    def body(a_ref, ap_ref, apn_ref, df_ref, dfn_ref, cw_ref, da_ref, red_ref):
        i = pl.program_id(1)

        @pl.when(i == 0)
        def _():
            red_ref[...] = jnp.zeros_like(red_ref)
        cwv = _planes(cw_ref)
        apv = jnp.concatenate([_planes(ap_ref), _planes(apn_ref)], axis=0)
        dfn = jnp.where(i < nt - 1, dfn_ref[...].astype(F32), 0.0)
        dfe = jnp.concatenate([df_ref[...].astype(F32), dfn], axis=0)
        g, val = apv[:, :GT], apv[:, GT:]
        sg = _sigmoid(g)
        dap = jnp.concatenate([dfe * val * (sg * (1.0 + g * (1.0 - sg))), dfe * (g * sg)], axis=1)
        shifted = [_shift_up(dap, 2)[:ts], _shift_up(dap, 1)[:ts], dap[:ts]]
        da = shifted[0] * cwv[0:1, :] + shifted[1] * cwv[1:2, :] + shifted[2] * cwv[2:3, :]
        av = _planes(a_ref)
        sums = [jnp.sum(av * t, axis=0, keepdims=True) for t in shifted]
        sums.append(jnp.sum(shifted[2], axis=0, keepdims=True))
        for e in range(2):
            cols = slice(e * GT, (e + 1) * GT)
            da_ref[e] = da[:, cols].astype(BF16)
            for r, sm in enumerate(sums):
                red_ref[e, r:r + 1, :] += sm[:, cols]

    nhb = s // HALO
    tile = pl.BlockSpec((2, ts, GT), lambda j, i: (0, i, j))
    return pl.pallas_call(
        body, name=name, grid=(NGT, nt),
        in_specs=[tile, tile,
                  pl.BlockSpec((2, HALO, GT), lambda j, i: (0, jnp.minimum((i + 1) * hb, nhb - 1), j)),
                  pl.BlockSpec((ts, GT), lambda j, i: (i, j)),
                  pl.BlockSpec((HALO, GT), lambda j, i: (jnp.minimum((i + 1) * hb, nhb - 1), j)),
                  pl.BlockSpec((2, 8, GT), lambda j, i: (0, 0, j))],
        out_specs=[tile, pl.BlockSpec((2, 8, GT), lambda j, i: (0, 0, j))],
        out_shape=[jax.ShapeDtypeStruct((2, s, DFF), BF16), jax.ShapeDtypeStruct((2, 8, DFF), F32)],
        compiler_params=_cparams(("arbitrary", "arbitrary")),
    )(a, ap, ap, df, df, cw)


def _chunk_mask(transposed=False):
    t = lax.broadcasted_iota(jnp.int32, (SGB, SGB), 0) // CHUNK
    u = lax.broadcasted_iota(jnp.int32, (SGB, SGB), 1) // CHUNK
    return (t <= u) if transposed else (u <= t)


def _sgu_ln(v, gain, bias):
    mu = jnp.mean(v, axis=-1, keepdims=True)
    vc = v - mu
    rstd = lax.rsqrt(jnp.mean(vc * vc, axis=-1, keepdims=True) + EPS)
    vhat = vc * rstd
    return vhat * gain + bias, vhat, rstd


def _sgu_fwd(z, vgain, vbias, ws, bst, tr, name):
    s = z.shape[0]

    def body(zu_ref, zv_ref, vg_ref, vb_ref, ws_ref, bs_ref, y_ref):
        u = _gelu(zu_ref[...].astype(F32))
        vn, _, _ = _sgu_ln(_gelu(zv_ref[...].astype(F32)), vg_ref[...], vb_ref[...])
        vn = vn.astype(BF16)
        mask = _chunk_mask()
        for g in range(SGG):
            w = jnp.where(mask, ws_ref[g], 0.0).astype(BF16)
            for b in range(tr // SGB):
                rs, cs = slice(b * SGB, (b + 1) * SGB), slice(g * SGC, (g + 1) * SGC)
                mixed = _dot(w, vn[rs, cs]) + bs_ref[:, g:g + 1]
                y_ref[rs, cs] = (u[rs, cs] * mixed).astype(BF16)

    vec = pl.BlockSpec((1, SGW), lambda i: (0, 0))
    return pl.pallas_call(
        body, name=name, grid=(s // tr,),
        in_specs=[pl.BlockSpec((tr, SGW), lambda i: (i, 0)), pl.BlockSpec((tr, SGW), lambda i: (i, 1)),
                  vec, vec, pl.BlockSpec((SGG, SGB, SGB), lambda i: (0, 0, 0)),
                  pl.BlockSpec((SGB, LANES), lambda i: (0, 0))],
        out_specs=pl.BlockSpec((tr, SGW), lambda i: (i, 0)),
        out_shape=jax.ShapeDtypeStruct((s, SGW), BF16),
        compiler_params=_cparams(("arbitrary",)),
    )(z, z, vgain, vbias, ws, bst)


def _sgu_bwd(z, dy, vgain, vbias, ws, wst, bst, tr, name):
    s = z.shape[0]

    def body(zu_ref, zv_ref, dy_ref, vg_ref, vb_ref, ws_ref, wst_ref, bs_ref,
             dz_ref, rb_ref, rv_ref, dws_ref, dbs_ref, dvn_s):
        @pl.when(pl.program_id(0) == 0)
        def _():
            rb_ref[...] = jnp.zeros_like(rb_ref)
            rv_ref[...] = jnp.zeros_like(rv_ref)
            dws_ref[...] = jnp.zeros_like(dws_ref)
            dbs_ref[...] = jnp.zeros_like(dbs_ref)
        zu = zu_ref[...].astype(F32)
        zv = zv_ref[...].astype(F32)
        u, gu = _gelu_and_grad(zu)
        v, gv = _gelu_and_grad(zv)
        vn, vhat, rstd = _sgu_ln(v, vg_ref[...], vb_ref[...])
        vnb = vn.astype(BF16)
        dyv = dy_ref[...].astype(F32)
        dmix = (dyv * u).astype(BF16)
        mask = _chunk_mask()
        mask_t = _chunk_mask(transposed=True)
        lane = _lane((SGB, LANES))
        dbs = jnp.zeros((SGB, LANES), F32)
        for g in range(SGG):
            w = jnp.where(mask, ws_ref[g], 0.0).astype(BF16)
            wt = jnp.where(mask_t, wst_ref[g], 0.0).astype(BF16)
            dw = jnp.zeros((SGB, SGB), F32)
            for b in range(tr // SGB):
                rs, cs = slice(b * SGB, (b + 1) * SGB), slice(g * SGC, (g + 1) * SGC)
                mixed = _dot(w, vnb[rs, cs]) + bs_ref[:, g:g + 1]
                dz_ref[rs, cs] = (dyv[rs, cs] * mixed * gu[rs, cs]).astype(BF16)
                dm = dmix[rs, cs]
                dw = dw + _dot_nt(dm, vnb[rs, cs])
                dbs = dbs + jnp.where(lane == g, jnp.sum(dm.astype(F32), axis=-1, keepdims=True), 0.0)
                dvn_s[rs, cs] = _dot(wt, dm)
            dws_ref[g] += jnp.where(mask, dw, 0.0)
        dbs_ref[...] += dbs
        dvn = dvn_s[...]
        rv_ref[0:1, :] += jnp.sum(dvn * vhat, axis=0, keepdims=True)
        rv_ref[1:2, :] += jnp.sum(dvn, axis=0, keepdims=True)
        dvh = dvn * vg_ref[...]
        dv = rstd * (dvh - jnp.mean(dvh, axis=-1, keepdims=True)
                     - vhat * jnp.mean(dvh * vhat, axis=-1, keepdims=True))
        dz_ref[:, SGW:] = (dv * gv).astype(BF16)
        dzf = dz_ref[...].astype(F32)
        rb_ref[0:1, :] += jnp.sum(dzf, axis=0, keepdims=True)

    vec = pl.BlockSpec((1, SGW), lambda i: (0, 0))
    wsp = pl.BlockSpec((SGG, SGB, SGB), lambda i: (0, 0, 0))
    return pl.pallas_call(
        body, name=name, grid=(s // tr,),
        in_specs=[pl.BlockSpec((tr, SGW), lambda i: (i, 0)), pl.BlockSpec((tr, SGW), lambda i: (i, 1)),
                  pl.BlockSpec((tr, SGW), lambda i: (i, 0)), vec, vec, wsp, wsp,
                  pl.BlockSpec((SGB, LANES), lambda i: (0, 0))],
        out_specs=[pl.BlockSpec((tr, 2 * SGW), lambda i: (i, 0)),
                   pl.BlockSpec((8, 2 * SGW), lambda i: (0, 0)),
                   pl.BlockSpec((8, SGW), lambda i: (0, 0)), wsp,
                   pl.BlockSpec((SGB, LANES), lambda i: (0, 0))],
        out_shape=[jax.ShapeDtypeStruct((s, 2 * SGW), BF16), jax.ShapeDtypeStruct((8, 2 * SGW), F32),
                   jax.ShapeDtypeStruct((8, SGW), F32), jax.ShapeDtypeStruct((SGG, SGB, SGB), F32),
                   jax.ShapeDtypeStruct((SGB, LANES), F32)],
        scratch_shapes=[pltpu.VMEM((tr, SGW), F32)],
        compiler_params=_cparams(("arbitrary",)),
    )(z, z, dy, vgain, vbias, ws, wst, bst)


def _final_loss(x, fg, tgt, gprev, yprev, ts, name):
    s, d = x.shape

    def body(x_ref, fg_ref, t_ref, g_ref, y_ref, l_ref, dx_ref, dy_ref, red_ref):
        @pl.when(pl.program_id(0) == 0)
        def _():
            l_ref[...] = jnp.zeros_like(l_ref)
            red_ref[...] = jnp.zeros_like(red_ref)
        xv = x_ref[...]
        r = _rstd_rows(xv)
        xh = xv * r
        err = xh * fg_ref[...] - t_ref[...]
        l_ref[...] += 0.5 * jnp.sum(jnp.mean(err * err, axis=-1, keepdims=True))
        dyo = err * (1.0 / d)
        dxh = dyo * fg_ref[...]
        dx = r * (dxh - xh * jnp.mean(dxh * xh, axis=-1, keepdims=True))
        dx_ref[...] = dx
        dy_ref[...] = (dx * g_ref[...]).astype(BF16)
        red_ref[0:1, :] += jnp.sum(dyo * xh, axis=0, keepdims=True)
        red_ref[1:2, :] += jnp.sum(dx * y_ref[...].astype(F32), axis=0, keepdims=True)

    row = pl.BlockSpec((ts, d), lambda i: (i, 0))
    vec = pl.BlockSpec((1, d), lambda i: (0, 0))
    return pl.pallas_call(
        body, name=name, grid=(s // ts,),
        in_specs=[row, vec, row, vec, row],
        out_specs=[pl.BlockSpec((8, LANES), lambda i: (0, 0)), row, row, pl.BlockSpec((8, d), lambda i: (0, 0))],
        out_shape=[jax.ShapeDtypeStruct((8, LANES), F32), jax.ShapeDtypeStruct((s, d), F32),
                   jax.ShapeDtypeStruct((s, d), BF16), jax.ShapeDtypeStruct((8, d), F32)],
        compiler_params=_cparams(("arbitrary",)),
    )(x, fg, tgt, gprev, yprev)


def _norm_bwd(xin, dh, dxout, ng, sc, gprev, yprev, ts, name):
    s, d = xin.shape
    has_prev = gprev is not None

    def body(*refs):
        if has_prev:
            x_ref, dh_ref, dxo_ref, ng_ref, sc_ref, g_ref, y_ref, dx_ref, dy_ref, red_ref = refs
        else:
            x_ref, dh_ref, dxo_ref, ng_ref, sc_ref, dx_ref, red_ref = refs

        @pl.when(pl.program_id(0) == 0)
        def _():
            red_ref[...] = jnp.zeros_like(red_ref)
        xv = x_ref[...]
        r = _rstd_rows(xv)
        xh = xv * r
        dhv = dh_ref[...]
        dr = dhv * (1.0 + sc_ref[...])
        t = dr * ng_ref[...]
        dx = dxo_ref[...] + r * (t - xh * jnp.mean(t * xh, axis=-1, keepdims=True))
        dx_ref[...] = dx
        red_ref[0:1, :] += jnp.sum(dhv, axis=0, keepdims=True)
        red_ref[1:2, :] += jnp.sum(dhv * (xh * ng_ref[...]), axis=0, keepdims=True)
        red_ref[2:3, :] += jnp.sum(dr * xh, axis=0, keepdims=True)
        if has_prev:
            dy_ref[...] = (dx * g_ref[...]).astype(BF16)
            red_ref[3:4, :] += jnp.sum(dx * y_ref[...].astype(F32), axis=0, keepdims=True)

    row = pl.BlockSpec((ts, d), lambda i: (i, 0))
    vec = pl.BlockSpec((1, d), lambda i: (0, 0))
    red = pl.BlockSpec((8, d), lambda i: (0, 0))
    if has_prev:
        in_specs, args = [row, row, row, vec, vec, vec, row], (xin, dh, dxout, ng, sc, gprev, yprev)
        out_specs = [row, row, red]
        out_shape = [jax.ShapeDtypeStruct((s, d), F32), jax.ShapeDtypeStruct((s, d), BF16),
                     jax.ShapeDtypeStruct((8, d), F32)]
    else:
        in_specs, args = [row, row, row, vec, vec], (xin, dh, dxout, ng, sc)
        out_specs = [row, red]
        out_shape = [jax.ShapeDtypeStruct((s, d), F32), jax.ShapeDtypeStruct((8, d), F32)]
    return pl.pallas_call(
        body, name=name, grid=(s // ts,), in_specs=in_specs, out_specs=out_specs, out_shape=out_shape,
        compiler_params=_cparams(("arbitrary",)),
    )(*args)


def _ada_mod(c_all, ada_w, ada_b):
    nb = c_all.shape[0]
    da = ada_w.shape[2]

    def body(c_ref, w_ref, b_ref, o_ref, ca_ref):
        cv = c_ref[...]
        ca = cv * _sigmoid(cv)
        ca_ref[...] = ca
        o_ref[0] = lax.dot_general(ca, w_ref[0], (((1,), (0,)), ((), ())), precision=lax.Precision.HIGHEST,
                                   preferred_element_type=F32) + b_ref[0]

    return pl.pallas_call(
        body, name="ada_mod", grid=(2,),
        in_specs=[pl.BlockSpec((nb, D), lambda i: (0, 0)), pl.BlockSpec((1, D, da), lambda i: (i, 0, 0)),
                  pl.BlockSpec((1, 1, da), lambda i: (i, 0, 0))],
        out_specs=[pl.BlockSpec((1, nb, da), lambda i: (i, 0, 0)), pl.BlockSpec((nb, D), lambda i: (0, 0))],
        out_shape=[jax.ShapeDtypeStruct((2, nb, da), F32), jax.ShapeDtypeStruct((nb, D), F32)],
        compiler_params=_cparams(("arbitrary",)),
    )(c_all, ada_w, ada_b)


def _ada_w_grad(c_act_t, dmod):
    nb = c_act_t.shape[1]
    da = dmod.shape[2]
    tn = 512

    def body(c_ref, d_ref, o_ref):
        acc = c_ref[:, 0:1] * d_ref[0, 0:1, :]
        for b in range(1, nb):
            acc = acc + c_ref[:, b:b + 1] * d_ref[0, b:b + 1, :]
        o_ref[0] = acc

    return pl.pallas_call(
        body, name="ada_w_grad", grid=(2, da // tn),
        in_specs=[pl.BlockSpec((D, nb), lambda i, j: (0, 0)), pl.BlockSpec((1, nb, tn), lambda i, j: (i, 0, j))],
        out_specs=pl.BlockSpec((1, D, tn), lambda i, j: (i, 0, j)),
        out_shape=jax.ShapeDtypeStruct((2, D, da), F32),
        compiler_params=_cparams(("arbitrary", "arbitrary")),
    )(c_act_t, dmod)


def _conv_planes(cw, cb):
    cwp = jnp.swapaxes(cw.reshape(3, 2, DFF), 0, 1)
    return jnp.pad(cwp, ((0, 0), (0, 5), (0, 0))), cb.reshape(2, 1, DFF)


def _local_step(x, tgt, mod, wts, small, comm=None):
    wts = dict(wts)
    s = x.shape[0]
    ts, tq, tr, tp = TS, TQ, TR, TP
    tkw = min(TKW, s)
    zb = lambda n: jnp.zeros((1, n), F32)
    m6 = mod.reshape(2, 6, 1, D)
    sh1, sc1, g1, sh2, sc2, g2 = ([m6[i, k] for i in range(2)] for k in range(6))
    n1g, n2g = small["norm1_g"], small["norm2_g"]
    row = lambda a, i: a[i:i + 1]

    qg2 = jnp.tile(small["fox_q_gain"], (1, 2))
    kg2 = jnp.tile(small["fox_k_gain"], (1, 2))
    bfp = jnp.pad(small["fox_b_f"], ((0, 0), (0, LANES - H)))
    proj, h1 = _norm_mod_matmul(x, row(n1g, 0), sc1[0], sh1[0], wts["fox_w_in"], zb(FOX_NP), F32, ts, 1408, "fox_in")
    qa, ka, va = _fox_post(proj, qg2, kg2, bfp, tp, "fox_post")
    att, ql, gathered = _attn_fwd(qa, ka, va, tq, "attn_fwd", shards=comm["shards"] if comm else ())
    if comm:
        wts.update(comm["make_wts"](gathered))
    gated = _gate(att, proj, ts, "fox_gate")
    x1, y0 = _matmul_residual(gated, wts["fox_w_out"], x, g1[0], ts, "fox_out")

    def ffn_fwd(xin, i, tag):
        cw, cb = _conv_planes(small["ffn_conv_w"][i], small["ffn_conv_b"][i])
        a, h = _norm_mod_matmul(xin, row(n2g, i), sc2[i], sh2[i], wts["ffn_w_up"][i], zb(2 * DFF), BF16, ts, 1408,
                                "ffn_up" + tag, planes=2)
        f, ap = _conv_gate(a, cw, cb, min(2 * ts, s), "ffn_conv" + tag)
        xo, y = _matmul_residual(f, wts["ffn_w_down"][i], xin, g2[i], ts, "ffn_down" + tag)
        return xo, (a, h, f, y, cw, ap)

    x2, ffn0 = ffn_fwd(x1, 0, "0")

    bst = jnp.pad(small["sgu_b_s"].T, ((0, 0), (0, LANES - SGG)))
    ws = small["sgu_w_s"]
    z, h3 = _norm_mod_matmul(x2, row(n1g, 1), sc1[1], sh1[1], wts["sgu_w_in"], small["sgu_b_in"], BF16, ts, 1024,
                             "sgu_in")
    yy = _sgu_fwd(z, small["sgu_v_gain"], small["sgu_v_bias"], ws, bst, tr, "sgu_mix")
    x3, y1 = _matmul_residual(yy, wts["sgu_w_out"], x2, g1[1], ts, "sgu_out")
    x4, ffn1 = ffn_fwd(x3, 1, "1")

    lsum, dx4, dy, redf = _final_loss(x4, small["final_g"], tgt, g2[1], ffn1[3], ts, "final_loss")
    grads = {"final_g": redf[0]}
    dmod = [[None] * 6, [None] * 6]
    dmod[1][5] = redf[1]

    def ffn_bwd(dxo, dy2, xin, i, saved, gprev, yprev, tag):
        a, h, f, _, cw, ap = saved
        wd, wu = wts["ffn_w_down"][i], wts["ffn_w_up"][i]
        g_wd = _matmul(f, dy2, True, False, 1408, D, tkw, BF16, "ffn_dwdown" + tag)
        df = _matmul_wt(dy2, wd, 1408, D, BF16, ts, "ffn_df" + tag)
        da, redc = _conv_gate_bwd(a, ap, df, cw, min(2 * ts, s), "ffn_conv_bwd" + tag)
        g_wu = _matmul(h, da, True, False, D, 1408, tkw, BF16, "ffn_dwup" + tag, out_parts=N_CHIP)
        dh = _matmul_wt(da, wu, D, 1408, F32, ts, "ffn_dh" + tag)
        outs = _norm_bwd(xin, dh, dxo, row(n2g, i), sc2[i], gprev, yprev, ts, "ffn_norm_bwd" + tag)
        return outs, g_wd, g_wu, redc

    (dx3, dy1, red), g_wd1, g_wu1, redc1 = ffn_bwd(dx4, dy, x3, 1, ffn1, g1[1], y1, "1")
    dmod[1][3], dmod[1][4], dn2g1, dmod[1][2] = red[0], red[1], red[2], red[3]

    g_swo = _matmul(yy, dy1, True, False, 1024, D, tkw, BF16, "sgu_dwout")
    dyy = _matmul_wt(dy1, wts["sgu_w_out"], 1024, D, BF16, ts, "sgu_dyy")
    wst = jnp.swapaxes(ws, 1, 2)
    dz, rb, rv, dws, dbst = _sgu_bwd(z, dyy, small["sgu_v_gain"], small["sgu_v_bias"], ws, wst, bst, tr, "sgu_mix_bwd")
    g_swi = _matmul(h3, dz, True, False, D, 1024, tkw, BF16, "sgu_dwin", out_parts=N_CHIP)
    dh3 = _matmul_wt(dz, wts["sgu_w_in"], D, 1024, F32, ts, "sgu_dh")
    dx2, dy2_0, red = _norm_bwd(x2, dh3, dx3, row(n1g, 1), sc1[1], g2[0], ffn0[3], ts, "sgu_norm_bwd")
    dmod[1][0], dmod[1][1], dn1g1, dmod[0][5] = red[0], red[1], red[2], red[3]

    (dx1, dy0, red), g_wd0, g_wu0, redc0 = ffn_bwd(dx2, dy2_0, x1, 0, ffn0, g1[0], y0, "0")
    dmod[0][3], dmod[0][4], dn2g0, dmod[0][2] = red[0], red[1], red[2], red[3]

    g_fwo = _matmul(gated, dy0, True, False, D, D, tkw, BF16, "fox_dwout")
    dgated = _matmul_wt(dy0, wts["fox_w_out"], D, D, F32, ts, "fox_dgated")
    doa, dop = _attn_bwd_prep(dgated, att, proj, ts, "attn_bwd_prep")
    css = comm["rs_prepare"]([g_fwo, g_swi, g_swo, g_wu0, g_wu1, g_wd0, g_wd1]) if comm else []
    dqa, dka, dva, rcvs = _attn_bwd(ql, ka, va, doa, tq, "attn_bwd", css=css)
    dproj, redx = _fox_post_bwd(proj, dqa, dka, dva, dop, qg2, kg2, bfp, tp, "fox_post_bwd")
    g_fwi = _matmul(h1, dproj, True, False, D, 1408, tkw, BF16, "fox_dwin")
    dh1 = _matmul_wt(dproj, wts["fox_w_in"], D, 1408, F32, ts, "fox_dh")
    dx0, red = _norm_bwd(x, dh1, dx1, row(n1g, 0), sc1[0], None, None, ts, "fox_norm_bwd")
    dmod[0][0], dmod[0][1], dn1g0 = red[0], red[1], red[2]

    grads.update(
        fox_w_in=g_fwi, fox_w_out=g_fwo, sgu_w_in=g_swi, sgu_w_out=g_swo,
        ffn_w_up=[g_wu0, g_wu1], ffn_w_down=[g_wd0, g_wd1],
        fox_q_gain=redx[0, :DH] + redx[0, DH:], fox_k_gain=redx[1, :DH] + redx[1, DH:], fox_b_f=redx[2, :H],
        sgu_b_in=rb[0], sgu_v_gain=rv[0], sgu_v_bias=rv[1], sgu_w_s=dws, sgu_b_s=dbst[:, :SGG].T,
        ffn_conv_w=jnp.stack([jnp.swapaxes(r[:, 0:3], 0, 1).reshape(3, 2 * DFF) for r in (redc0, redc1)]),
        ffn_conv_b=jnp.stack([r[:, 3].reshape(2 * DFF) for r in (redc0, redc1)]),
        norm1_g=jnp.stack([dn1g0, dn1g1]), norm2_g=jnp.stack([dn2g0, dn2g1]),
    )
    dmod_arr = jnp.stack([jnp.concatenate(dmod[0]), jnp.concatenate(dmod[1])])
    return lsum[0, 0], dx0, grads, dmod_arr, (css, rcvs)


N_DEV = 8
N_CHIP = 4
HBM_SPEC = pl.BlockSpec(memory_space=pltpu.HBM)
VMEM_SPEC = pl.BlockSpec(memory_space=pltpu.VMEM)


def _mesh_pos():
    return lax.axis_index("x"), lax.axis_index("y"), lax.axis_index("c")


def _other_chips(x, y):
    return [(1 - x, y), (x, 1 - y), (1 - x, 1 - y)]


def _remote(src, dst, ssem, rsem, dev):
    return pltpu.make_async_remote_copy(src_ref=src, dst_ref=dst, send_sem=ssem, recv_sem=rsem,
                                        device_id=dev, device_id_type=MESH)


def _allgather8(xb, name):
    m_per, n = xb.shape

    def body(x_ref, out_ref, send_sems, recv_sems, local_sem):
        x, y, c = _mesh_pos()
        me, sibling = (x, y, c), (x, y, 1 - c)
        chips = _other_chips(x, y)

        def rows(px, py, pc):
            return out_ref.at[pl.ds((4 * px + 2 * py + pc) * m_per, m_per), :]

        def copy(k, block, to, src=None):
            return _remote(rows(*block) if src is None else src, rows(*block),
                           send_sems.at[k], recv_sems.at[k], to)

        mine = pltpu.make_async_copy(x_ref, rows(*me), local_sem)
        mine.start()
        first = [copy(0, me, sibling, src=x_ref)]
        first += [copy(1 + j, me, (*chip, c), src=x_ref) for j, chip in enumerate(chips)]
        for cp in first:
            cp.start()
        passed = [copy(4 + j, (*chip, c), sibling) for j, chip in enumerate(chips)]
        for j, chip in enumerate(chips):
            copy(1 + j, (*chip, c), me).wait_recv()
            passed[j].start()
        copy(0, sibling, me).wait_recv()
        for j, chip in enumerate(chips):
            copy(4 + j, (*chip, 1 - c), me).wait_recv()
        for cp in first + passed:
            cp.wait_send()
        mine.wait()

    return pl.pallas_call(
        body, name=name,
        out_shape=jax.ShapeDtypeStruct((N_DEV * m_per, n), xb.dtype),
        in_specs=[VMEM_SPEC], out_specs=VMEM_SPEC,
        scratch_shapes=[pltpu.SemaphoreType.DMA((7,)), pltpu.SemaphoreType.DMA((7,)), pltpu.SemaphoreType.DMA],
        compiler_params=pltpu.CompilerParams(vmem_limit_bytes=V7X_VMEM_LIMIT),
    )(xb)


def _gather_shards(shards, name):
    na = len(shards)

    def body(*refs):
        p_refs, o_refs = refs[:na], refs[na:2 * na]
        send_sems, recv_sems, pass_send, pass_recv = refs[2 * na:]
        x, y, c = _mesh_pos()
        me = 2 * x + y
        sibling = (x, y, 1 - c)
        chips = _other_chips(x, y)

        def half(a, ci, hf):
            rh = shards[a].shape[0] // 2
            return o_refs[a].at[ci, pl.ds(hf * rh, rh), :]

        sends = []
        for a in range(na):
            rh = shards[a].shape[0] // 2
            for k, chip in enumerate(chips):
                sends.append(_remote(p_refs[a].at[pl.ds(c * rh, rh), :], half(a, me, c),
                                     send_sems.at[3 * a + k], recv_sems.at[3 * a + k], (*chip, c)))
        for cp in sends:
            cp.start()
        passed = []
        for a in range(na):
            for k, chip in enumerate(chips):
                ci = 2 * chip[0] + chip[1]
                _remote(half(a, ci, c), half(a, ci, c), send_sems.at[3 * a + k], recv_sems.at[3 * a + k],
                        (*chip, c)).wait_recv()
                cp = _remote(half(a, ci, c), half(a, ci, c), pass_send.at[3 * a + k], pass_recv.at[3 * a + k], sibling)
                cp.start()
                passed.append(cp)
        for a in range(na):
            for k, chip in enumerate(chips):
                ci = 2 * chip[0] + chip[1]
                _remote(half(a, ci, 1 - c), half(a, ci, 1 - c), pass_send.at[3 * a + k], pass_recv.at[3 * a + k],
                        sibling).wait_recv()
        for cp in sends + passed:
            cp.wait_send()

    return pl.pallas_call(
        body, name=name,
        out_shape=[jax.ShapeDtypeStruct((N_CHIP,) + p.shape, p.dtype) for p in shards],
        in_specs=[HBM_SPEC] * na, out_specs=[HBM_SPEC] * na,
        scratch_shapes=[pltpu.SemaphoreType.DMA((3 * na,))] * 4,
    )(*shards)


def _rs_to_sibling(gs, name):
    na = len(gs)

    def body(*refs):
        g_refs, o_refs, ssems, rsems = refs[:na], refs[na:2 * na], refs[2 * na], refs[2 * na + 1]
        x, y, c = _mesh_pos()
        cps = []
        for a in range(na):
            rh = gs[a].shape[1] // 2
            cp = _remote(g_refs[a].at[:, pl.ds((1 - c) * rh, rh), :], o_refs[a], ssems.at[a], rsems.at[a],
                         (x, y, 1 - c))
            cp.start()
            cps.append(cp)
        for cp in cps:
            cp.wait()

    return pl.pallas_call(
        body, name=name,
        out_shape=[jax.ShapeDtypeStruct((g.shape[0], g.shape[1] // 2, g.shape[2]), g.dtype) for g in gs],
        in_specs=[HBM_SPEC] * na, out_specs=[HBM_SPEC] * na,
        scratch_shapes=[pltpu.SemaphoreType.DMA((na,)), pltpu.SemaphoreType.DMA((na,))],
    )(*gs)


def _rs_chip_sum(g, sib, c_arr, tr, name):
    nc, r, n = g.shape
    rh = r // 2
    g4 = g.reshape(nc, 2, rh, n)

    def body(c_ref, g_ref, s_ref, o_ref):
        o_ref[...] = (g_ref[0].astype(F32) + s_ref[...].astype(F32)).astype(BF16)

    return pl.pallas_call(
        body, name=name, out_shape=jax.ShapeDtypeStruct((nc, rh, n), BF16),
        grid_spec=pltpu.PrefetchScalarGridSpec(
            num_scalar_prefetch=1, grid=(nc, rh // tr),
            in_specs=[pl.BlockSpec((1, 1, tr, n), lambda j, i, cr: (j, cr[0], i, 0)),
                      pl.BlockSpec((1, tr, n), lambda j, i, cr: (j, i, 0))],
            out_specs=pl.BlockSpec((1, tr, n), lambda j, i, cr: (j, i, 0))),
        compiler_params=_cparams(("arbitrary", "arbitrary")),
    )(c_arr, g4, sib)


def _rs_across_chips(css, name):
    na = len(css)

    def body(*refs):
        cs_refs, o_refs, send_sems, recv_sems = refs[:na], refs[na:2 * na], refs[2 * na], refs[2 * na + 1]
        x, y, c = _mesh_pos()
        cps = []
        for a in range(na):
            for k, chip in enumerate(_other_chips(x, y)):
                ci = 2 * chip[0] + chip[1]
                cp = _remote(cs_refs[a].at[ci], o_refs[a].at[k], send_sems.at[3 * a + k], recv_sems.at[3 * a + k],
                             (*chip, c))
                cp.start()
                cps.append(cp)
        for cp in cps:
            cp.wait()

    return pl.pallas_call(
        body, name=name, out_shape=[jax.ShapeDtypeStruct((3,) + cs.shape[1:], cs.dtype) for cs in css],
        in_specs=[HBM_SPEC] * na, out_specs=[HBM_SPEC] * na,
        scratch_shapes=[pltpu.SemaphoreType.DMA((3 * na,)), pltpu.SemaphoreType.DMA((3 * na,))],
    )(*css)


def _rs_final_sum(cs, rcv, me_arr, tr, name):
    nc, rh, n = cs.shape

    def body(m_ref, c_ref, r_ref, o_ref):
        acc = c_ref[0].astype(F32)
        for k in range(3):
            acc = acc + r_ref[k].astype(F32)
        o_ref[...] = acc

    return pl.pallas_call(
        body, name=name, out_shape=jax.ShapeDtypeStruct((rh, n), F32),
        grid_spec=pltpu.PrefetchScalarGridSpec(
            num_scalar_prefetch=1, grid=(rh // tr,),
            in_specs=[pl.BlockSpec((1, tr, n), lambda i, mr: (mr[0], i, 0)),
                      pl.BlockSpec((3, tr, n), lambda i, mr: (0, i, 0))],
            out_specs=pl.BlockSpec((tr, n), lambda i, mr: (i, 0))),
        compiler_params=_cparams(("arbitrary",)),
    )(me_arr, cs, rcv)


def _rs_swap_halves(halves, name):
    na = len(halves)

    def body(*refs):
        h_refs, o_refs, ssems, rsems = refs[:na], refs[na:2 * na], refs[2 * na], refs[2 * na + 1]
        x, y, c = _mesh_pos()
        cps = []
        for a in range(na):
            cp = _remote(h_refs[a], o_refs[a], ssems.at[a], rsems.at[a], (x, y, 1 - c))
            cp.start()
            cps.append(cp)
        for cp in cps:
            cp.wait()

    return pl.pallas_call(
        body, name=name, out_shape=[jax.ShapeDtypeStruct(h.shape, h.dtype) for h in halves],
        in_specs=[HBM_SPEC] * na, out_specs=[HBM_SPEC] * na,
        scratch_shapes=[pltpu.SemaphoreType.DMA((na,)), pltpu.SemaphoreType.DMA((na,))],
    )(*halves)


def _join_columns(parts, n_out, name):
    p, k, c = parts.shape
    tr = 128

    def body(w_ref, o_ref):
        for j in range(p):
            o_ref[:, j * c:(j + 1) * c] = w_ref[j]
        o_ref[:, p * c:] = jnp.zeros((tr, n_out - p * c), parts.dtype)

    return pl.pallas_call(
        body, name=name, grid=(k // tr,),
        in_specs=[pl.BlockSpec((p, tr, c), lambda i: (0, i, 0))],
        out_specs=pl.BlockSpec((tr, n_out), lambda i: (i, 0)),
        out_shape=jax.ShapeDtypeStruct((k, n_out), parts.dtype),
        compiler_params=_cparams(("arbitrary",)),
    )(parts)


def _split_columns(g, p, c, name):
    k, n = g.shape
    tr = 128

    def body(g_ref, o_ref):
        for j in range(p):
            o_ref[j] = g_ref[:, j * c:(j + 1) * c]

    return pl.pallas_call(
        body, name=name, grid=(k // tr,),
        in_specs=[pl.BlockSpec((tr, n), lambda i: (i, 0))],
        out_specs=pl.BlockSpec((p, tr, c), lambda i: (0, i, 0)),
        out_shape=jax.ShapeDtypeStruct((p, k, c), g.dtype),
        compiler_params=_cparams(("arbitrary",)),
    )(g)


def _sum8(g, name):
    nd, r, n = g.shape

    def body(g_ref, o_ref):
        acc = g_ref[0]
        for k in range(1, nd):
            acc = acc + g_ref[k]
        o_ref[...] = acc

    return pl.pallas_call(
        body, name=name, grid=(r // 8,),
        in_specs=[pl.BlockSpec((nd, 8, n), lambda i: (0, i, 0))],
        out_specs=pl.BlockSpec((8, n), lambda i: (i, 0)),
        out_shape=jax.ShapeDtypeStruct((r, n), F32),
        compiler_params=_cparams(("arbitrary",)),
    )(g)


def _adamw(w, g, m, v, name):
    r, n = w.shape
    tr = 128 if r % 128 == 0 else 8
    bc1 = 1.0 - ADAM_B1 ** ADAM_STEP
    bc2 = 1.0 - ADAM_B2 ** ADAM_STEP

    def body(w_ref, g_ref, m_ref, v_ref, d_ref, mo_ref, vo_ref):
        gv = g_ref[...]
        mn = ADAM_B1 * m_ref[...] + (1.0 - ADAM_B1) * gv
        vn = ADAM_B2 * v_ref[...] + (1.0 - ADAM_B2) * (gv * gv)
        d_ref[...] = -ADAM_LR * ((mn / bc1) / (jnp.sqrt(vn / bc2) + ADAM_EPS) + ADAM_WD * w_ref[...])
        mo_ref[...] = mn
        vo_ref[...] = vn

    blk = pl.BlockSpec((tr, n), lambda i: (i, 0))
    shp = jax.ShapeDtypeStruct((r, n), F32)
    return pl.pallas_call(
        body, name=name, grid=(r // tr,), in_specs=[blk] * 4, out_specs=[blk] * 3, out_shape=[shp] * 3,
        compiler_params=_cparams(("arbitrary",)),
    )(w, g, m, v)


ROW = 1024
PACK_ROWS = 7168
BIG = ("fox_w_in", "fox_w_out", "sgu_w_in", "sgu_w_out", "ffn_w_up", "ffn_w_down")
SMALL_SHARDED = ("sgu_b_in", "sgu_v_gain", "sgu_v_bias", "ffn_conv_w")
SMALL_REPL = ("fox_b_f", "fox_q_gain", "fox_k_gain", "sgu_w_s", "sgu_b_s", "ffn_conv_b", "ada_b",
              "norm1_g", "norm2_g", "final_g")
WEIGHTS = ("fox_w_in", "fox_b_f", "fox_q_gain", "fox_k_gain", "fox_w_out", "sgu_w_in", "sgu_b_in", "sgu_v_gain",
           "sgu_v_bias", "sgu_w_s", "sgu_b_s", "sgu_w_out", "ffn_w_up", "ffn_conv_w", "ffn_conv_b", "ffn_w_down",
           "ada_w", "ada_b", "norm1_g", "norm2_g", "final_g")


def _rows_of(a, mult=1):
    flat = a.reshape(-1)
    rows = -(-flat.shape[0] // ROW)
    rows = -(-rows // mult) * mult
    return jnp.pad(flat, (0, rows * ROW - flat.shape[0])).reshape(rows, ROW)


def _pack(parts, mult, total=None):
    p = jnp.concatenate([_rows_of(a, mult) for a in parts], axis=0)
    if total is not None:
        p = jnp.pad(p, ((0, total - p.shape[0]), (0, 0)))
    return p


def _unpack(pack, shapes, mult):
    out, r0 = [], 0
    for shp in shapes:
        size = int(np.prod(shp))
        rows = -(-(-(-size // ROW)) // mult) * mult
        out.append(pack[r0:r0 + rows].reshape(-1)[:size].reshape(shp))
        r0 += rows
    return out


def _big_shards(t):
    return [t["fox_w_in"][0], t["fox_w_out"][0], t["sgu_w_in"][0], t["sgu_w_out"][0],
            t["ffn_w_up"][0], t["ffn_w_up"][1], t["ffn_w_down"][0], t["ffn_w_down"][1]]


def _row_tile(rows):
    return next(t for t in (512, 352, 256, 128, 64) if rows % t == 0)


def kernel(x, c, fox_w_in, fox_b_f, fox_q_gain, fox_k_gain, fox_w_out, sgu_w_in, sgu_b_in, sgu_v_gain, sgu_v_bias, sgu_w_s, sgu_b_s, sgu_w_out, ffn_w_up, ffn_conv_w, ffn_conv_b, ffn_w_down, ada_w, ada_b, norm1_g, norm2_g, final_g, loss_target, m_fox_w_in, m_fox_b_f, m_fox_q_gain, m_fox_k_gain, m_fox_w_out, m_sgu_w_in, m_sgu_b_in, m_sgu_v_gain, m_sgu_v_bias, m_sgu_w_s, m_sgu_b_s, m_sgu_w_out, m_ffn_w_up, m_ffn_conv_w, m_ffn_conv_b, m_ffn_w_down, m_ada_w, m_ada_b, m_norm1_g, m_norm2_g, m_final_g, v_fox_w_in, v_fox_b_f, v_fox_q_gain, v_fox_k_gain, v_fox_w_out, v_sgu_w_in, v_sgu_b_in, v_sgu_v_gain, v_sgu_v_bias, v_sgu_w_s, v_sgu_b_s, v_sgu_w_out, v_ffn_w_up, v_ffn_conv_w, v_ffn_conv_b, v_ffn_w_down, v_ada_w, v_ada_b, v_norm1_g, v_norm2_g, v_final_g):
    w = dict(fox_w_in=fox_w_in, fox_b_f=fox_b_f, fox_q_gain=fox_q_gain, fox_k_gain=fox_k_gain, fox_w_out=fox_w_out,
             sgu_w_in=sgu_w_in, sgu_b_in=sgu_b_in, sgu_v_gain=sgu_v_gain, sgu_v_bias=sgu_v_bias, sgu_w_s=sgu_w_s,
             sgu_b_s=sgu_b_s, sgu_w_out=sgu_w_out, ffn_w_up=ffn_w_up, ffn_conv_w=ffn_conv_w, ffn_conv_b=ffn_conv_b,
             ffn_w_down=ffn_w_down, ada_w=ada_w, ada_b=ada_b, norm1_g=norm1_g, norm2_g=norm2_g, final_g=final_g)
    mom = dict(fox_w_in=m_fox_w_in, fox_b_f=m_fox_b_f, fox_q_gain=m_fox_q_gain, fox_k_gain=m_fox_k_gain,
               fox_w_out=m_fox_w_out, sgu_w_in=m_sgu_w_in, sgu_b_in=m_sgu_b_in, sgu_v_gain=m_sgu_v_gain,
               sgu_v_bias=m_sgu_v_bias, sgu_w_s=m_sgu_w_s, sgu_b_s=m_sgu_b_s, sgu_w_out=m_sgu_w_out,
               ffn_w_up=m_ffn_w_up, ffn_conv_w=m_ffn_conv_w, ffn_conv_b=m_ffn_conv_b, ffn_w_down=m_ffn_w_down,
               ada_w=m_ada_w, ada_b=m_ada_b, norm1_g=m_norm1_g, norm2_g=m_norm2_g, final_g=m_final_g)
    var = dict(fox_w_in=v_fox_w_in, fox_b_f=v_fox_b_f, fox_q_gain=v_fox_q_gain, fox_k_gain=v_fox_k_gain,
               fox_w_out=v_fox_w_out, sgu_w_in=v_sgu_w_in, sgu_b_in=v_sgu_b_in, sgu_v_gain=v_sgu_v_gain,
               sgu_v_bias=v_sgu_v_bias, sgu_w_s=v_sgu_w_s, sgu_b_s=v_sgu_b_s, sgu_w_out=v_sgu_w_out,
               ffn_w_up=v_ffn_w_up, ffn_conv_w=v_ffn_conv_w, ffn_conv_b=v_ffn_conv_b, ffn_w_down=v_ffn_w_down,
               ada_w=v_ada_w, ada_b=v_ada_b, norm1_g=v_norm1_g, norm2_g=v_norm2_g, final_g=v_final_g)

    ax, ay, ac = _mesh_pos()
    chip = 2 * ax + ay
    dev = 2 * chip + ac

    small_shard_shapes = tuple(w[n].shape for n in SMALL_SHARDED)
    blk = _pack([c] + [w[n] for n in SMALL_SHARDED], 1, 16)
    gat = _allgather8(blk, "gather_small").reshape(N_DEV, 16, ROW)
    c_all = gat[:, 0, :]
    per_chip = [_unpack(gat[2 * j, 1:], small_shard_shapes, 1) for j in range(N_CHIP)]
    full_small = {n: jnp.concatenate([per_chip[j][i] for j in range(N_CHIP)], axis=-1)
                  for i, n in enumerate(SMALL_SHARDED)}

    mine = [a.astype(BF16) for a in _big_shards(w)]
    with_own = lambda gat, own: [lax.dynamic_update_slice(g_, m_[None], (chip, 0, 0)) for g_, m_ in zip(gat, own)]
    fwi, = with_own(_gather_shards(mine[:1], "gather_fox_w_in"), mine[:1])
    fwi_full = _join_columns(fwi, FOX_NP, "join_fox_w_in")
    wts = dict(fox_w_in=fwi_full)

    def make_wts(gathered):
        fwo, swi, swo, up0, up1, dn0, dn1 = with_own(gathered, mine[1:])
        return dict(fox_w_out=fwo.reshape(D, D), sgu_w_in=swi, sgu_w_out=swo.reshape(SGW, D),
                    ffn_w_up=[up0, up1], ffn_w_down=[dn0.reshape(DFF, D), dn1.reshape(DFF, D)])

    c_arr = jnp.reshape(ac, (1,)).astype(jnp.int32)
    me_arr = jnp.reshape(chip, (1,)).astype(jnp.int32)

    def chip_sums(glist, tag):
        sibs = _rs_to_sibling(glist, "rs_sibling" + tag)
        return [_rs_chip_sum(g_, s_, c_arr, _row_tile(s_.shape[1]), "rs_chip_sum%s%d" % (tag, a))
                for a, (g_, s_) in enumerate(zip(glist, sibs))]

    def rs_prepare(gl):
        g_fwo, g_swi, g_swo, g_wu0, g_wu1, g_wd0, g_wd1 = gl
        return chip_sums([g_fwo.reshape(N_CHIP, 256, D), g_swi, g_swo.reshape(N_CHIP, 512, D), g_wu0, g_wu1,
                          g_wd0.reshape(N_CHIP, 704, D), g_wd1.reshape(N_CHIP, 704, D)], "")

    comm = dict(shards=mine[1:], make_wts=make_wts, rs_prepare=rs_prepare)

    da = ada_w.shape[2]
    ada_b_cols = lax.dynamic_slice_in_dim(ada_b, chip * da, da, axis=1)[:, None, :]
    mod_cols, c_act = _ada_mod(c_all, ada_w, ada_b_cols)
    mod_all = _allgather8(mod_cols.reshape(-1, ROW), "gather_mod").reshape(N_DEV, 2, N_DEV, da)
    mod_mine = lax.dynamic_index_in_dim(mod_all[0::2], dev, axis=2, keepdims=False)
    mod = jnp.swapaxes(mod_mine, 0, 1).reshape(2, N_CHIP * da)

    small = dict(norm1_g=norm1_g, norm2_g=norm2_g, final_g=final_g[None], fox_q_gain=fox_q_gain,
                 fox_k_gain=fox_k_gain, fox_b_f=fox_b_f, sgu_b_in=full_small["sgu_b_in"],
                 sgu_v_gain=full_small["sgu_v_gain"], sgu_v_bias=full_small["sgu_v_bias"], sgu_w_s=sgu_w_s[0],
                 sgu_b_s=sgu_b_s[0], ffn_conv_w=full_small["ffn_conv_w"], ffn_conv_b=ffn_conv_b)
    loss_dev, dx, g, dmod, (css, rcvs) = _local_step(x[0], loss_target[0], mod, wts, small, comm)

    g["ada_b"] = dmod
    g["loss"] = loss_dev
    small_names = ("ada_b",) + SMALL_SHARDED + tuple(n for n in SMALL_REPL if n != "ada_b") + ("loss",)
    gs = _pack([g[n] for n in small_names], 1)
    rows_s = -(-gs.shape[0] // 8) * 8
    gs = jnp.pad(gs, ((0, rows_s - gs.shape[0]), (0, 0)))
    gs_all = _allgather8(gs, "gather_small_grads").reshape(N_DEV, rows_s, ROW)
    gsum = _sum8(gs_all, "sum_small_grads")
    full_shapes = {n: w[n].shape for n in SMALL_REPL}
    full_shapes.update({n: w[n].shape[:-1] + (w[n].shape[-1] * N_CHIP,) for n in SMALL_SHARDED})
    full_shapes["loss"] = ()
    gfull = dict(zip(small_names, _unpack(gsum, [full_shapes[n] for n in small_names], 1)))
    grads = {n: gfull[n] for n in SMALL_REPL}
    for n in SMALL_SHARDED:
        width = w[n].shape[-1]
        grads[n] = lax.dynamic_slice_in_dim(gfull[n], chip * width, width, axis=gfull[n].ndim - 1)
    dmod_all = gs_all[:, :12, :].reshape(N_DEV, 2, N_CHIP * da)
    dmod_cols = jnp.swapaxes(lax.dynamic_slice_in_dim(dmod_all, chip * da, da, axis=2), 0, 1)
    grads["ada_w"] = _ada_w_grad(c_act.T, dmod_cols)

    gfi = _split_columns(g["fox_w_in"], N_CHIP, FOX_N // N_CHIP, "split_fox_w_in")
    cs_fox = chip_sums([gfi], "_fox")
    css = cs_fox + list(css)
    rcvs = list(_rs_across_chips(cs_fox, "rs_chips_fox")) + list(rcvs)
    halves = [_rs_final_sum(cs_, r_, me_arr, _row_tile(cs_.shape[1]), "rs_final_sum%d" % a)
              for a, (cs_, r_) in enumerate(zip(css, rcvs))]
    others = _rs_swap_halves(halves, "rs_swap")
    red = [jnp.concatenate([jnp.where(ac == 0, h_, o_), jnp.where(ac == 0, o_, h_)]) for h_, o_ in zip(halves, others)]
    grads.update(fox_w_in=red[0], fox_w_out=red[1], sgu_w_in=red[2], sgu_w_out=red[3],
                 ffn_w_up=jnp.stack([red[4], red[5]]), ffn_w_down=jnp.stack([red[6], red[7]]))

    delta, new_m, new_v = {}, {}, {}
    for n in BIG + ("ada_w",):
        shp = w[n].shape
        two_d = lambda a: a.reshape(-1, shp[-1])
        d_, m_, v_ = _adamw(two_d(w[n]), two_d(grads[n]), two_d(mom[n]), two_d(var[n]), "adamw_" + n)
        delta[n], new_m[n], new_v[n] = d_.reshape(shp), m_.reshape(shp), v_.reshape(shp)
    rest = SMALL_SHARDED + SMALL_REPL
    packs = [_pack([t[n] for n in rest], 1) for t in (w, grads, mom, var)]
    rows_r = -(-packs[0].shape[0] // 8) * 8
    packs = [jnp.pad(p, ((0, rows_r - p.shape[0]), (0, 0))) for p in packs]
    outs = _adamw(*packs, "adamw_small")
    for t, o in zip((delta, new_m, new_v), outs):
        t.update(zip(rest, _unpack(o, [w[n].shape for n in rest], 1)))

    loss = gfull["loss"]
    return (loss, dx[None], *[grads[n].reshape(w[n].shape) for n in WEIGHTS], *[delta[n] for n in WEIGHTS],
            *[new_m[n] for n in WEIGHTS], *[new_v[n] for n in WEIGHTS])
```

```python
import functools
import math

import numpy as np
import jax
import jax.numpy as jnp
from jax import lax
from jax.experimental import pallas as pl
from jax.experimental.pallas import tpu as pltpu

F32 = jnp.float32
BF16 = jnp.bfloat16
MESH = pl.DeviceIdType.MESH

D = 1024
H = 16
DH = 64
NP = H // 2
LANES = 128
DFF = 2816
SGW = 2048
SGG = 8
SGC = 256
SGB = 128
CHUNK = 64
EPS = 1e-6
FOX_N = 4 * D + H
FOX_NP = 4224
GT = 256
NGT = DFF // GT
SCALE = DH ** -0.5
LOG2E = 1.4426950408889634

ADAM_LR = 0.001
ADAM_B1 = 0.9
ADAM_B2 = 0.999
ADAM_EPS = 1e-08
ADAM_WD = 0.01
ADAM_STEP = 10

V7X_VMEM_LIMIT = 56 * 1024 * 1024

L_F = 64
L_NF = 67
L_LSE = 70


def _cparams(sem=None):
    return pltpu.CompilerParams(dimension_semantics=sem, vmem_limit_bytes=V7X_VMEM_LIMIT)


def _split3(x):
    hi = x.astype(BF16)
    r = x - hi.astype(F32)
    mid = r.astype(BF16)
    lo = (r - mid.astype(F32)).astype(BF16)
    return hi, mid, lo


def _dot(a, b, dims=(((1,), (0,)), ((), ()))):
    return lax.dot_general(a, b, dims, preferred_element_type=F32)


def _dot_nt(a, b):
    return _dot(a, b, (((1,), (1,)), ((), ())))


def _dot_tn(a, b):
    return _dot(a, b, (((0,), (0,)), ((), ())))


def _exact_dot(m_bf16, x_f32):
    hi, mid, lo = _split3(x_f32)
    return _dot(m_bf16, hi) + _dot(m_bf16, mid) + _dot(m_bf16, lo)


def _exact_dot_r(x_f32, m_bf16):
    hi, mid, lo = _split3(x_f32)
    return _dot(hi, m_bf16) + _dot(mid, m_bf16) + _dot(lo, m_bf16)


def _head_block_ones():
    r = lax.broadcasted_iota(jnp.int32, (LANES, LANES), 0) // DH
    c = lax.broadcasted_iota(jnp.int32, (LANES, LANES), 1) // DH
    return (r == c).astype(BF16)


def _sigmoid(x):
    return 1.0 / (1.0 + jnp.exp(-x))


def _gelu(x):
    c = math.sqrt(2.0 / math.pi)
    return 0.5 * x * (1.0 + jnp.tanh(c * (x + 0.044715 * (x * x * x))))


def _gelu_and_grad(x):
    c = math.sqrt(2.0 / math.pi)
    x2 = x * x
    t = jnp.tanh(c * (x + 0.044715 * (x2 * x)))
    half = 0.5 * (1.0 + t)
    return x * half, half + 0.5 * x * (1.0 - t * t) * c * (1.0 + 3 * 0.044715 * x2)


def _rstd_rows(x):
    return lax.rsqrt(jnp.mean(x * x, axis=-1, keepdims=True) + EPS)


def _norm_mod_matmul(x, ng, sc, sh, w, bias, out_dtype, ts, tn, name, planes=1):
    s, d = x.shape
    ns = w.shape[-1]
    n = w.shape[0] * ns if w.ndim == 3 else ns
    nc = n // planes

    def body(x_ref, ng_ref, sc_ref, sh_ref, w_ref, b_ref, o_ref, h_ref):
        xv = x_ref[...]
        h = (xv * _rstd_rows(xv) * ng_ref[...] * (1.0 + sc_ref[...]) + sh_ref[...]).astype(BF16)
        h_ref[...] = h
        for e in range(planes):
            for c0 in range(0, nc, tn):
                g0 = e * nc + c0
                wv = w_ref[g0 // ns, :, g0 % ns:g0 % ns + tn] if w.ndim == 3 else w_ref[:, g0:g0 + tn]
                val = (_dot(h, wv) + b_ref[:, g0:g0 + tn]).astype(out_dtype)
                if planes == 1:
                    o_ref[:, c0:c0 + tn] = val
                else:
                    o_ref[e, :, c0:c0 + tn] = val

    vec = pl.BlockSpec((1, d), lambda i: (0, 0))
    w_spec = (pl.BlockSpec(w.shape, lambda i: (0, 0, 0)) if w.ndim == 3 else pl.BlockSpec((d, n), lambda i: (0, 0)))
    if planes == 1:
        o_spec, o_shape = pl.BlockSpec((ts, n), lambda i: (i, 0)), (s, n)
    else:
        o_spec, o_shape = pl.BlockSpec((planes, ts, nc), lambda i: (0, i, 0)), (planes, s, nc)
    return pl.pallas_call(
        body, name=name, grid=(s // ts,),
        in_specs=[pl.BlockSpec((ts, d), lambda i: (i, 0)), vec, vec, vec, w_spec,
                  pl.BlockSpec((1, n), lambda i: (0, 0))],
        out_specs=[o_spec, pl.BlockSpec((ts, d), lambda i: (i, 0))],
        out_shape=[jax.ShapeDtypeStruct(o_shape, out_dtype), jax.ShapeDtypeStruct((s, d), BF16)],
        compiler_params=_cparams(("arbitrary",)),
    )(x, ng, sc, sh, w, bias)


def _matmul(a, b, ta, tb, tm, tn, tk, out_dtype, name, out_parts=1):
    if a.ndim == 3:
        m, k = a.shape[1], a.shape[0] * a.shape[2]
        nkp = a.shape[2] // tk
    else:
        m, k = (a.shape[1], a.shape[0]) if ta else a.shape
    if b.ndim == 3:
        n = b.shape[1] if tb else b.shape[0] * b.shape[2]
        nbp = b.shape[2] // (tk if tb else tn)
    else:
        n = b.shape[0] if tb else b.shape[1]
    nk = k // tk
    nop = n // out_parts // tn
    dims = (((0,) if ta else (1,), (1,) if tb else (0,)), ((), ()))

    def body(a_ref, b_ref, o_ref, acc):
        kk = pl.program_id(2)

        @pl.when(kk == 0)
        def _():
            acc[...] = jnp.zeros_like(acc)
        acc[...] += _dot(a_ref[...], b_ref[...], dims)

        @pl.when(kk == nk - 1)
        def _():
            o_ref[...] = acc[...].astype(out_dtype)

    if a.ndim == 3:
        a_spec = pl.BlockSpec((None, tm, tk), lambda i, j, kk: (kk // nkp, i, kk % nkp))
    else:
        a_spec = (pl.BlockSpec((tk, tm), lambda i, j, kk: (kk, i)) if ta
                  else pl.BlockSpec((tm, tk), lambda i, j, kk: (i, kk)))
    if b.ndim == 3 and tb:
        b_spec = pl.BlockSpec((None, tn, tk), lambda i, j, kk: (kk // nbp, j, kk % nbp))
    elif b.ndim == 3:
        b_spec = pl.BlockSpec((None, tk, tn), lambda i, j, kk: (j // nbp, kk, j % nbp))
    else:
        b_spec = (pl.BlockSpec((tn, tk), lambda i, j, kk: (j, kk)) if tb
                  else pl.BlockSpec((tk, tn), lambda i, j, kk: (kk, j)))
    if out_parts > 1:
        o_spec = pl.BlockSpec((None, tm, tn), lambda i, j, kk: (j // nop, i, j % nop))
        o_shape = (out_parts, m, n // out_parts)
    else:
        o_spec, o_shape = pl.BlockSpec((tm, tn), lambda i, j, kk: (i, j)), (m, n)
    return pl.pallas_call(
        body, name=name, grid=(m // tm, n // tn, nk),
        in_specs=[a_spec, b_spec],
        out_specs=o_spec,
        out_shape=jax.ShapeDtypeStruct(o_shape, out_dtype),
        scratch_shapes=[pltpu.VMEM((tm, tn), F32)],
        compiler_params=_cparams(("arbitrary", "arbitrary", "arbitrary")),
    )(a, b)


def _matmul_wt(a, w, tn, tk, out_dtype, ts, name):
    s = a.shape[-2]
    ka, kw = a.shape[-1], w.shape[-1]
    k = ka * (a.shape[0] if a.ndim == 3 else 1)
    n = w.shape[-2]

    def body(a_ref, w_ref, o_ref):
        for n0 in range(0, n, tn):
            acc = None
            for g0 in range(0, k, tk):
                av = a_ref[g0 // ka, :, g0 % ka:g0 % ka + tk] if a.ndim == 3 else a_ref[:, g0:g0 + tk]
                wv = (w_ref[g0 // kw, n0:n0 + tn, g0 % kw:g0 % kw + tk] if w.ndim == 3
                      else w_ref[n0:n0 + tn, g0:g0 + tk])
                part = _dot_nt(av, wv)
                acc = part if acc is None else acc + part
            o_ref[:, n0:n0 + tn] = acc.astype(out_dtype)

    a_spec = (pl.BlockSpec((a.shape[0], ts, ka), lambda i: (0, i, 0)) if a.ndim == 3
              else pl.BlockSpec((ts, ka), lambda i: (i, 0)))
    w_spec = pl.BlockSpec(w.shape, (lambda i: (0, 0, 0)) if w.ndim == 3 else (lambda i: (0, 0)))
    return pl.pallas_call(
        body, name=name, grid=(s // ts,),
        in_specs=[a_spec, w_spec], out_specs=pl.BlockSpec((ts, n), lambda i: (i, 0)),
        out_shape=jax.ShapeDtypeStruct((s, n), out_dtype),
        compiler_params=_cparams(("arbitrary",)),
    )(a, w)


def _matmul_residual(a, w, xin, g, ts, name):
    s, k = a.shape
    d = w.shape[1]

    def body(a_ref, w_ref, x_ref, g_ref, o_ref, y_ref):
        y = _dot(a_ref[...], w_ref[...])
        o_ref[...] = x_ref[...] + g_ref[...] * y
        y_ref[...] = y.astype(BF16)

    return pl.pallas_call(
        body, name=name, grid=(s // ts,),
        in_specs=[pl.BlockSpec((ts, k), lambda i: (i, 0)),
                  pl.BlockSpec((k, d), lambda i: (0, 0)),
                  pl.BlockSpec((ts, d), lambda i: (i, 0)),
                  pl.BlockSpec((1, d), lambda i: (0, 0))],
        out_specs=[pl.BlockSpec((ts, d), lambda i: (i, 0)), pl.BlockSpec((ts, d), lambda i: (i, 0))],
        out_shape=[jax.ShapeDtypeStruct((s, d), F32), jax.ShapeDtypeStruct((s, d), BF16)],
        compiler_params=_cparams(("arbitrary",)),
    )(a, w, xin, g)


def _lane(shape):
    return lax.broadcasted_iota(jnp.int32, shape, 1)


def _pair_norm(x, gain2, bones):
    msq = _exact_dot_r(x * x, bones) * (1.0 / DH)
    r = lax.rsqrt(msq + EPS)
    xh = x * r
    return xh * gain2, xh, r


def _fox_post(proj, qg2, kg2, bf, ts, name):
    s = proj.shape[0]

    def body(p_ref, qg_ref, kg_ref, bf_ref, q_ref, k_ref, v_ref, carry):
        @pl.when(pl.program_id(0) == 0)
        def _():
            carry[...] = jnp.zeros_like(carry)
        lane = _lane((ts, LANES))
        bones = _head_block_ones()
        xf = p_ref[:, 4 * D:4 * D + LANES] + bf_ref[...]
        logf = jnp.minimum(xf, 0.0) - jnp.log(1.0 + jnp.exp(-jnp.abs(xf)))
        logf = jnp.where(lane < H, logf, 0.0)
        rr = lax.broadcasted_iota(jnp.int32, (ts, ts), 0)
        cc = lax.broadcasted_iota(jnp.int32, (ts, ts), 1)
        ltri = (cc <= rr).astype(BF16)
        fcum = _exact_dot(ltri, logf) + carry[0:1, :]
        carry[0:1, :] = fcum[ts - 1:ts, :]
        fhi, fmid, flo = _split3(fcum * LOG2E)
        fhi, fmid, flo = fhi.astype(F32), fmid.astype(F32), flo.astype(F32)
        one_q = ((lane >= L_NF) & (lane < L_NF + 3)).astype(F32)
        one_k = (((lane >= L_F) & (lane < L_F + 3)) | ((lane >= L_LSE) & (lane < L_LSE + 3))).astype(F32)
        one_v = ((lane >= L_F) & (lane < L_F + 3)).astype(F32)
        for p in range(NP):
            qn, _, _ = _pair_norm(p_ref[:, p * LANES:(p + 1) * LANES], qg_ref[...], bones)
            kn, _, _ = _pair_norm(p_ref[:, D + p * LANES:D + (p + 1) * LANES], kg_ref[...], bones)
            vv = p_ref[:, 2 * D + p * LANES:2 * D + (p + 1) * LANES]
            qn = qn * (SCALE * LOG2E)
            for e in range(2):
                h = 2 * p + e
                if e == 1:
                    qe, ke, ve = (pltpu.roll(t, DH, axis=1) for t in (qn, kn, vv))
                else:
                    qe, ke, ve = qn, kn, vv
                f0, f1, f2 = fhi[:, h:h + 1], fmid[:, h:h + 1], flo[:, h:h + 1]
                fq = jnp.where(lane == L_F, f0, jnp.where(lane == L_F + 1, f1, jnp.where(lane == L_F + 2, f2, one_q)))
                fk = jnp.where(lane == L_NF, -f0, jnp.where(lane == L_NF + 1, -f1, jnp.where(lane == L_NF + 2, -f2, one_k)))
                q_ref[h] = jnp.where(lane < DH, qe, fq).astype(BF16)
                k_ref[h] = jnp.where(lane < DH, ke, fk).astype(BF16)
                v_ref[h] = jnp.where(lane < DH, ve, one_v).astype(BF16)

    hs = pl.BlockSpec((H, ts, LANES), lambda i: (0, i, 0))
    vec = pl.BlockSpec((1, LANES), lambda i: (0, 0))
    shp = jax.ShapeDtypeStruct((H, s, LANES), BF16)
    return pl.pallas_call(
        body, name=name, grid=(s // ts,),
        in_specs=[pl.BlockSpec((ts, FOX_NP), lambda i: (i, 0)), vec, vec, vec],
        out_specs=[hs, hs, hs], out_shape=[shp, shp, shp],
        scratch_shapes=[pltpu.VMEM((8, LANES), F32)],
        compiler_params=_cparams(("arbitrary",)),
    )(proj, qg2, kg2, bf)


def _gather_copies(p_refs, o_refs, send_sems, recv_sems):
    x, y, c = _mesh_pos()
    me = 2 * x + y
    sends, arrivals = [], []
    for a, (p_ref, o_ref) in enumerate(zip(p_refs, o_refs)):
        rh = p_ref.shape[0] // 2
        for k, chip in enumerate(_other_chips(x, y)):
            ci = 2 * chip[0] + chip[1]
            for cc in range(2):
                sends.append(_remote(p_ref.at[pl.ds(c * rh, rh), :], o_ref.at[me, pl.ds(c * rh, rh), :],
                                     send_sems.at[6 * a + 2 * k + cc], recv_sems.at[6 * a + 2 * k + c], (*chip, cc)))
                arrivals.append(_remote(o_ref.at[ci, pl.ds(cc * rh, rh), :], o_ref.at[ci, pl.ds(cc * rh, rh), :],
                                        send_sems.at[6 * a + 2 * k + cc], recv_sems.at[6 * a + 2 * k + cc],
                                        (*chip, cc)))
    return sends, arrivals


def _attn_fwd(qa, ka, va, tq, name, shards=()):
    s = qa.shape[1]
    nq = s // tq
    na = len(shards)
    hps = HPS_FWD

    def body(*refs):
        q_ref, k_ref, v_ref = refs[:3]
        p_refs = refs[3:3 + na]
        o_ref, ql_ref = refs[3 + na:5 + na]
        g_refs = refs[5 + na:5 + 2 * na]
        i = pl.program_id(1)
        if na:
            send_sems, recv_sems = refs[5 + 2 * na:]

            @pl.when((pl.program_id(0) == 0) & (i == 0))
            def _():
                for cp in _gather_copies(p_refs, g_refs, send_sems, recv_sems)[0]:
                    cp.start()
        lane = _lane((tq, LANES))
        qs_ = [q_ref[e] for e in range(hps)]

        tk = min(TK_FWD, tq)
        nks = tq // tk

        def step(j, carry, diag=None):
            off = pl.multiple_of(j * tk, tk)
            scs = [_dot_nt(qs_[e], k_ref[e, pl.ds(off, tk), :]) for e in range(hps)]
            probs = []
            for e in range(hps):
                m, sc = carry[e][0], scs[e]
                if diag is not None:
                    rr = lax.broadcasted_iota(jnp.int32, (tq, tk), 0)
                    cc = lax.broadcasted_iota(jnp.int32, (tq, tk), 1) + diag * tk
                    sc = jnp.where(cc <= rr, sc, -jnp.inf)
                m_new = jnp.maximum(m, jnp.max(sc, axis=-1, keepdims=True))
                probs.append((m_new, jnp.exp2(sc - m_new).astype(BF16), jnp.exp2(m - m_new)))
            return tuple((m_new, carry[e][1] * alpha + _dot(pr, v_ref[e, pl.ds(off, tk), :]))
                         for e, (m_new, pr, alpha) in enumerate(probs))

        one = (jnp.full((tq, 1), -jnp.inf, F32), jnp.zeros((tq, LANES), F32))
        carry = lax.fori_loop(0, i * nks, step, (one,) * hps)
        for r in range(nks):
            carry = step(i * nks + r, carry, diag=r)
        outs = []
        for e in range(hps):
            m, acc = carry[e]
            l = acc[:, L_F:L_F + 1]
            outs.append(acc / l)
            lse = m + jnp.log2(l)
            h0, h1, h2 = _split3(-lse)
            ql = jnp.where(lane == L_LSE, h0.astype(F32),
                           jnp.where(lane == L_LSE + 1, h1.astype(F32),
                                     jnp.where(lane == L_LSE + 2, h2.astype(F32), qs_[e].astype(F32))))
            ql_ref[e] = ql.astype(BF16)
        for e in range(0, hps, 2):
            o_ref[:, e * DH:(e + 2) * DH] = jnp.where(lane < DH, outs[e], pltpu.roll(outs[e + 1], DH, axis=1))
        if na:
            @pl.when((pl.program_id(0) == H // hps - 1) & (i == nq - 1))
            def _():
                sends, arrivals = _gather_copies(p_refs, g_refs, send_sems, recv_sems)
                for cp in arrivals:
                    cp.wait_recv()
                for cp in sends:
                    cp.wait_send()

    res = pl.BlockSpec((hps, s, LANES), lambda p, i: (p, 0, 0))
    qs = pl.BlockSpec((hps, tq, LANES), lambda p, i: (p, i, 0))
    outs = pl.pallas_call(
        body, name=name, grid=(H // hps, nq),
        in_specs=[qs, res, res] + [HBM_SPEC] * na,
        out_specs=[pl.BlockSpec((tq, hps * DH), lambda p, i: (i, p)), qs] + [HBM_SPEC] * na,
        out_shape=[jax.ShapeDtypeStruct((s, D), F32), jax.ShapeDtypeStruct((H, s, LANES), BF16)]
        + [jax.ShapeDtypeStruct((N_CHIP,) + p.shape, p.dtype) for p in shards],
        scratch_shapes=[pltpu.SemaphoreType.DMA((6 * na,))] * 2 if na else [],
        compiler_params=_cparams(("arbitrary", "arbitrary")),
    )(qa, ka, va, *shards)
    return outs[0], outs[1], list(outs[2:])


def _chip_exchange_copies(cs_refs, o_refs, send_sems, recv_sems):
    x, y, c = _mesh_pos()
    cps = []
    for a, (cs_ref, o_ref) in enumerate(zip(cs_refs, o_refs)):
        for k, chip in enumerate(_other_chips(x, y)):
            ci = 2 * chip[0] + chip[1]
            cps.append(_remote(cs_ref.at[ci], o_ref.at[k], send_sems.at[3 * a + k], recv_sems.at[3 * a + k],
                               (*chip, c)))
    return cps


def _attn_bwd(ql, ka, va, doa, tq, name, css=()):
    s = ql.shape[1]
    nq = s // tq
    na = len(css)

    def body(*refs):
        q_ref, k_ref, v_ref, do_ref = refs[:4]
        cs_refs = refs[4:4 + na]
        dqo_ref, dk_ref, dv_ref = refs[4 + na:7 + na]
        r_refs = refs[7 + na:7 + 2 * na]
        dq_ref = refs[7 + 2 * na]
        j = pl.program_id(1)
        if na:
            send_sems, recv_sems = refs[8 + 2 * na:]

            @pl.when((pl.program_id(0) == 0) & (j == 0))
            def _():
                for cp in _chip_exchange_copies(cs_refs, r_refs, send_sems, recv_sems):
                    cp.start()

        @pl.when(j == 0)
        def _():
            dq_ref[...] = jnp.zeros_like(dq_ref)
        lane = _lane((tq, LANES))
        kbs = [k_ref[0], k_ref[1]]
        vbs = [v_ref[0], v_ref[1]]

        def step(i, carry, masked):
            ioff = pl.multiple_of(i * tq, tq)
            qbs = [q_ref[e, pl.ds(ioff, tq), :] for e in range(2)]
            dobs = [do_ref[e, pl.ds(ioff, tq), :] for e in range(2)]
            scs = [_dot_nt(qbs[e], kbs[e]) for e in range(2)]
            dps = [_dot_nt(dobs[e], vbs[e]) for e in range(2)]
            prs, dss = [], []
            for e in range(2):
                pr = jnp.exp2(scs[e])
                if masked:
                    rr = lax.broadcasted_iota(jnp.int32, (tq, tq), 0)
                    cc = lax.broadcasted_iota(jnp.int32, (tq, tq), 1)
                    pr = jnp.where(cc <= rr, pr, 0.0)
                dss.append((pr * dps[e]).astype(BF16))
                prs.append(pr.astype(BF16))
            new = []
            for e in range(2):
                dk, dv = carry[e]
                dv = dv + _dot_tn(prs[e], dobs[e])
                dk = dk + _dot_tn(dss[e], qbs[e])
                dq_ref[e, pl.ds(ioff, tq), :] += _dot(dss[e], kbs[e])
                new.append((dk, dv))
            return tuple(new)

        zero = jnp.zeros((tq, LANES), F32)
        carry = step(j, ((zero, zero), (zero, zero)), True)
        carry = lax.fori_loop(j + 1, nq, functools.partial(step, masked=False), carry)
        for e in range(2):
            dk, dv = carry[e]
            col = dk[:, L_NF:L_NF + 1]
            hi = col.astype(BF16).astype(F32)
            dk_ref[e] = jnp.where(lane == L_NF, hi, jnp.where(lane == L_NF + 1, col - hi, dk)).astype(BF16)
            dv_ref[e] = dv.astype(BF16)

        @pl.when(j == nq - 1)
        def _():
            lane_s = _lane((s, LANES))
            for e in range(2):
                dq = dq_ref[e]
                col = dq[:, L_F:L_F + 1]
                hi = col.astype(BF16).astype(F32)
                dqo_ref[e] = jnp.where(lane_s == L_F, hi, jnp.where(lane_s == L_F + 1, col - hi, dq)).astype(BF16)
        if na:
            @pl.when((pl.program_id(0) == NP - 1) & (j == nq - 1))
            def _():
                for cp in _chip_exchange_copies(cs_refs, r_refs, send_sems, recv_sems):
                    cp.wait()

    res = pl.BlockSpec((2, s, LANES), lambda p, j: (p, 0, 0))
    tile = pl.BlockSpec((2, tq, LANES), lambda p, j: (p, j, 0))
    shp = jax.ShapeDtypeStruct((H, s, LANES), BF16)
    outs = pl.pallas_call(
        body, name=name, grid=(NP, nq),
        in_specs=[res, tile, tile, res] + [HBM_SPEC] * na, out_specs=[res, tile, tile] + [HBM_SPEC] * na,
        out_shape=[shp, shp, shp] + [jax.ShapeDtypeStruct((3,) + cs.shape[1:], cs.dtype) for cs in css],
        scratch_shapes=[pltpu.VMEM((2, s, LANES), F32)] + ([pltpu.SemaphoreType.DMA((3 * na,))] * 2 if na else []),
        compiler_params=_cparams(("arbitrary", "arbitrary")),
    )(ql, ka, va, doa, *css)
    return outs[0], outs[1], outs[2], list(outs[3:])


def _gate(att, proj, ts, name):
    s = att.shape[0]

    def body(a_ref, o_ref, g_ref):
        g_ref[...] = (a_ref[...] * _sigmoid(o_ref[...])).astype(BF16)

    return pl.pallas_call(
        body, name=name, grid=(s // ts,),
        in_specs=[pl.BlockSpec((ts, D), lambda i: (i, 0)), pl.BlockSpec((ts, D), lambda i: (i, 3))],
        out_specs=pl.BlockSpec((ts, D), lambda i: (i, 0)),
        out_shape=jax.ShapeDtypeStruct((s, D), BF16),
        compiler_params=_cparams(("arbitrary",)),
    )(att, proj)


def _attn_bwd_prep(dgated, att, proj, ts, name):
    s = att.shape[0]

    def body(dg_ref, a_ref, o_ref, doa_ref, dop_ref):
        lane = _lane((ts, LANES))
        bones = _head_block_ones()
        for p in range(NP):
            sl = slice(p * LANES, (p + 1) * LANES)
            dg, a = dg_ref[:, sl], a_ref[:, sl]
            sig = _sigmoid(o_ref[:, sl])
            datt = dg * sig
            dop_ref[:, sl] = (dg * a * sig * (1.0 - sig)).astype(BF16)
            delta = _exact_dot_r(datt * a, bones)
            for e in range(2):
                de, dl = (datt, delta) if e == 0 else (pltpu.roll(datt, DH, axis=1), pltpu.roll(delta, DH, axis=1))
                h0, h1, h2 = _split3(-dl[:, 0:1])
                aug = jnp.where(lane == L_F, h0.astype(F32),
                                jnp.where(lane == L_F + 1, h1.astype(F32),
                                          jnp.where(lane == L_F + 2, h2.astype(F32), 0.0)))
                doa_ref[2 * p + e] = jnp.where(lane < DH, de, aug).astype(BF16)

    row = pl.BlockSpec((ts, D), lambda i: (i, 0))
    return pl.pallas_call(
        body, name=name, grid=(s // ts,),
        in_specs=[row, row, pl.BlockSpec((ts, D), lambda i: (i, 3))],
        out_specs=[pl.BlockSpec((H, ts, LANES), lambda i: (0, i, 0)), row],
        out_shape=[jax.ShapeDtypeStruct((H, s, LANES), BF16), jax.ShapeDtypeStruct((s, D), BF16)],
        compiler_params=_cparams(("arbitrary",)),
    )(dgated, att, proj)


def _fox_post_bwd(proj, dqa, dka, dva, dop, qg2, kg2, bf, ts, name):
    s = proj.shape[0]
    nt = s // ts

    def body(p_ref, dq_ref, dk_ref, dv_ref, dop_ref, qg_ref, kg_ref, bf_ref, o_ref, red_ref, carry):
        @pl.when(pl.program_id(0) == 0)
        def _():
            carry[...] = jnp.zeros_like(carry)
            red_ref[...] = jnp.zeros_like(red_ref)
        lane = _lane((ts, LANES))
        bones = _head_block_ones()
        d_f = jnp.zeros((ts, LANES), F32)
        dqg = jnp.zeros((1, LANES), F32)
        dkg = jnp.zeros((1, LANES), F32)
        for p in range(NP):
            heads = [[ref[2 * p + e].astype(F32) for e in range(2)] for ref in (dq_ref, dk_ref, dv_ref)]
            pair = [jnp.where(lane < DH, a, pltpu.roll(b, DH, axis=1)) for a, b in heads]
            for e in range(2):
                dqe, dke = heads[0][e], heads[1][e]
                col = (dqe[:, L_F:L_F + 1] + dqe[:, L_F + 1:L_F + 2]
                       - dke[:, L_NF:L_NF + 1] - dke[:, L_NF + 1:L_NF + 2])
                d_f = jnp.where(lane == 2 * p + e, col, d_f)
            for idx, (g_ref, base) in enumerate(((qg_ref, 0), (kg_ref, D))):
                x = p_ref[:, base + p * LANES:base + (p + 1) * LANES]
                _, xh, r = _pair_norm(x, g_ref[...], bones)
                dn = pair[idx] * (SCALE if idx == 0 else 1.0 / LOG2E)
                t = dn * g_ref[...]
                mean_txh = _exact_dot_r(t * xh, bones) * (1.0 / DH)
                dx = r * (t - xh * mean_txh)
                o_ref[:, base + p * LANES:base + (p + 1) * LANES] = dx.astype(BF16)
                gsum = jnp.sum(dn * xh, axis=0, keepdims=True)
                if idx == 0:
                    dqg = dqg + gsum
                else:
                    dkg = dkg + gsum
            o_ref[:, 2 * D + p * LANES:2 * D + (p + 1) * LANES] = pair[2].astype(BF16)
        o_ref[:, 3 * D:4 * D] = dop_ref[...]
        rr = lax.broadcasted_iota(jnp.int32, (ts, ts), 0)
        cc = lax.broadcasted_iota(jnp.int32, (ts, ts), 1)
        utri = (cc >= rr).astype(BF16)
        dlogf = _exact_dot(utri, d_f) + carry[0:1, :]
        carry[0:1, :] = dlogf[0:1, :]
        xf = p_ref[:, 4 * D:4 * D + LANES] + bf_ref[...]
        dfl = jnp.where(lane < H, dlogf * _sigmoid(-xf), 0.0)
        o_ref[:, 4 * D:4 * D + LANES] = dfl.astype(BF16)
        red_ref[0:1, :] += dqg
        red_ref[1:2, :] += dkg
        red_ref[2:3, :] += jnp.sum(dfl, axis=0, keepdims=True)

    hs = pl.BlockSpec((H, ts, LANES), lambda i: (0, nt - 1 - i, 0))
    vec = pl.BlockSpec((1, LANES), lambda i: (0, 0))
    return pl.pallas_call(
        body, name=name, grid=(nt,),
        in_specs=[pl.BlockSpec((ts, FOX_NP), lambda i: (nt - 1 - i, 0)), hs, hs, hs,
                  pl.BlockSpec((ts, D), lambda i: (nt - 1 - i, 0)), vec, vec, vec],
        out_specs=[pl.BlockSpec((ts, FOX_NP), lambda i: (nt - 1 - i, 0)),
                   pl.BlockSpec((8, LANES), lambda i: (0, 0))],
        out_shape=[jax.ShapeDtypeStruct((s, FOX_NP), BF16), jax.ShapeDtypeStruct((8, LANES), F32)],
        scratch_shapes=[pltpu.VMEM((8, LANES), F32)],
        compiler_params=_cparams(("arbitrary",)),
    )(proj, dqa, dka, dva, dop, qg2, kg2, bf)


HALO = 16
TS = 512
TQ = 512
TR = 256
TP = 256
HPS_FWD = 4
TK_FWD = 512
TKW = 2048


def _shift_down(x, k):
    return pltpu.roll(x, k, axis=0)


def _shift_up(x, k):
    return pltpu.roll(x, x.shape[0] - k, axis=0)


def _planes(ref):
    return jnp.concatenate([ref[0].astype(F32), ref[1].astype(F32)], axis=1)


def _conv_gate(a, cw, cb, ts, name):
    s = a.shape[1]
    hb = ts // HALO

    def body(prev_ref, a_ref, cw_ref, cb_ref, f_ref, ap_ref):
        i = pl.program_id(0)
        cwv, cbv = _planes(cw_ref), _planes(cb_ref)
        prev = jnp.where(i > 0, _planes(prev_ref), 0.0)
        ext = jnp.concatenate([prev, _planes(a_ref)], axis=0)
        ap = (_shift_down(ext, 2) * cwv[0:1, :] + _shift_down(ext, 1) * cwv[1:2, :]
              + ext * cwv[2:3, :] + cbv)[HALO:, :]
        g, val = ap[:, :GT], ap[:, GT:]
        f_ref[...] = (g * _sigmoid(g) * val).astype(BF16)
        ap_ref[0] = g.astype(BF16)
        ap_ref[1] = val.astype(BF16)

    tile = pl.BlockSpec((2, ts, GT), lambda i, j: (0, i, j))
    return pl.pallas_call(
        body, name=name, grid=(s // ts, NGT),
        in_specs=[pl.BlockSpec((2, HALO, GT), lambda i, j: (0, jnp.maximum(i * hb - 1, 0), j)), tile,
                  pl.BlockSpec((2, 8, GT), lambda i, j: (0, 0, j)),
                  pl.BlockSpec((2, 1, GT), lambda i, j: (0, 0, j))],
        out_specs=[pl.BlockSpec((ts, GT), lambda i, j: (i, j)), tile],
        out_shape=[jax.ShapeDtypeStruct((s, DFF), BF16), jax.ShapeDtypeStruct((2, s, DFF), BF16)],
        compiler_params=_cparams(("arbitrary", "arbitrary")),
    )(a, a, cw, cb)


def _conv_down(a, cw, cb, w, xin, gate, ts, name):
    s = a.shape[1]
    d = w.shape[1]
    hb = ts // HALO

    def body(prev_ref, a_ref, cw_ref, cb_ref, w_ref, x_ref, g_ref, o_ref, y_ref, f_ref, ap_ref):
        i = pl.program_id(0)
        acc = None
        for c in range(NGT):
            cols = slice(c * GT, (c + 1) * GT)
            both = lambda ref: jnp.concatenate([ref[0, :, cols].astype(F32), ref[1, :, cols].astype(F32)], axis=1)
            cwv, cbv = both(cw_ref), both(cb_ref)
            ext = jnp.concatenate([jnp.where(i > 0, both(prev_ref), 0.0), both(a_ref)], axis=0)
            ap = (_shift_down(ext, 2) * cwv[0:1, :] + _shift_down(ext, 1) * cwv[1:2, :]
                  + ext * cwv[2:3, :] + cbv)[HALO:, :]
            g, val = ap[:, :GT], ap[:, GT:]
            fch = (g * _sigmoid(g) * val).astype(BF16)
            f_ref[:, cols] = fch
            ap_ref[0, :, cols] = g.astype(BF16)
            ap_ref[1, :, cols] = val.astype(BF16)
            part = _dot(fch, w_ref[cols, :])
            acc = part if acc is None else acc + part
        y_ref[...] = acc.astype(BF16)
        o_ref[...] = x_ref[...] + g_ref[...] * acc

    row = pl.BlockSpec((ts, d), lambda i: (i, 0))
    planes = pl.BlockSpec((2, ts, DFF), lambda i: (0, i, 0))
    return pl.pallas_call(
        body, name=name, grid=(s // ts,),
        in_specs=[pl.BlockSpec((2, HALO, DFF), lambda i: (0, jnp.maximum(i * hb - 1, 0), 0)), planes,
                  pl.BlockSpec((2, 8, DFF), lambda i: (0, 0, 0)), pl.BlockSpec((2, 1, DFF), lambda i: (0, 0, 0)),
                  pl.BlockSpec((DFF, d), lambda i: (0, 0)), row, pl.BlockSpec((1, d), lambda i: (0, 0))],
        out_specs=[row, row, pl.BlockSpec((ts, DFF), lambda i: (i, 0)), planes],
        out_shape=[jax.ShapeDtypeStruct((s, d), F32), jax.ShapeDtypeStruct((s, d), BF16),
                   jax.ShapeDtypeStruct((s, DFF), BF16), jax.ShapeDtypeStruct((2, s, DFF), BF16)],
        compiler_params=_cparams(("arbitrary",)),
    )(a, a, cw, cb, w, xin, gate)


def _down_bwd_conv(dy, w, a, ap, cw, ts, name):
    s, d = dy.shape
    hb = ts // HALO
    nt = s // ts
    nhb = s // HALO

    def body(dy_ref, dyn_ref, w_ref, a_ref, ap_ref, apn_ref, cw_ref, da_ref, red_ref):
        i = pl.program_id(0)

        @pl.when(i == 0)
        def _():
            red_ref[...] = jnp.zeros_like(red_ref)
        dyn = jnp.where(i < nt - 1, dyn_ref[...], jnp.zeros_like(dyn_ref))
        dye = jnp.concatenate([dy_ref[...], dyn], axis=0)
        for c in range(NGT):
            cols = slice(c * GT, (c + 1) * GT)
            both = lambda ref: jnp.concatenate([ref[0, :, cols].astype(F32), ref[1, :, cols].astype(F32)], axis=1)
            cwv = both(cw_ref)
            dfe = _dot_nt(dye, w_ref[cols, :])
            apv = jnp.concatenate([both(ap_ref), both(apn_ref)], axis=0)
            g, val = apv[:, :GT], apv[:, GT:]
            sg = _sigmoid(g)
            dap = jnp.concatenate([dfe * val * (sg * (1.0 + g * (1.0 - sg))), dfe * (g * sg)], axis=1)
            shifted = [_shift_up(dap, 2)[:ts], _shift_up(dap, 1)[:ts], dap[:ts]]
            da = shifted[0] * cwv[0:1, :] + shifted[1] * cwv[1:2, :] + shifted[2] * cwv[2:3, :]
            av = both(a_ref)
            sums = [jnp.sum(av * t, axis=0, keepdims=True) for t in shifted]
            sums.append(jnp.sum(shifted[2], axis=0, keepdims=True))
            for e in range(2):
                half = slice(e * GT, (e + 1) * GT)
                da_ref[e, :, cols] = da[:, half].astype(BF16)
                for r, sm in enumerate(sums):
                    red_ref[e, r:r + 1, cols] += sm[:, half]

    planes = pl.BlockSpec((2, ts, DFF), lambda i: (0, i, 0))
    nxt = lambda i: jnp.minimum((i + 1) * hb, nhb - 1)
    return pl.pallas_call(
        body, name=name, grid=(nt,),
        in_specs=[pl.BlockSpec((ts, d), lambda i: (i, 0)), pl.BlockSpec((HALO, d), lambda i: (nxt(i), 0)),
                  pl.BlockSpec((DFF, d), lambda i: (0, 0)), planes, planes,
                  pl.BlockSpec((2, HALO, DFF), lambda i: (0, nxt(i), 0)),
                  pl.BlockSpec((2, 8, DFF), lambda i: (0, 0, 0))],
        out_specs=[planes, pl.BlockSpec((2, 8, DFF), lambda i: (0, 0, 0))],
        out_shape=[jax.ShapeDtypeStruct((2, s, DFF), BF16), jax.ShapeDtypeStruct((2, 8, DFF), F32)],
        compiler_params=_cparams(("arbitrary",)),
    )(dy, dy, w, a, ap, ap, cw)


def _conv_gate_bwd(a, ap, df, cw, ts, name):
    s = a.shape[1]
    hb = ts // HALO
    nt = s // ts

    def body(a_ref, ap_ref, apn_ref, df_ref, dfn_ref, cw_ref, da_ref, red_ref):
        i = pl.program_id(1)

        @pl.when(i == 0)
        def _():
            red_ref[...] = jnp.zeros_like(red_ref)
        cwv = _planes(cw_ref)
        apv = jnp.concatenate([_planes(ap_ref), _planes(apn_ref)], axis=0)
        dfn = jnp.where(i < nt - 1, dfn_ref[...].astype(F32), 0.0)
        dfe = jnp.concatenate([df_ref[...].astype(F32), dfn], axis=0)
        g, val = apv[:, :GT], apv[:, GT:]
        sg = _sigmoid(g)
        dap = jnp.concatenate([dfe * val * (sg * (1.0 + g * (1.0 - sg))), dfe * (g * sg)], axis=1)
        shifted = [_shift_up(dap, 2)[:ts], _shift_up(dap, 1)[:ts], dap[:ts]]
        da = shifted[0] * cwv[0:1, :] + shifted[1] * cwv[1:2, :] + shifted[2] * cwv[2:3, :]
        av = _planes(a_ref)
        sums = [jnp.sum(av * t, axis=0, keepdims=True) for t in shifted]
        sums.append(jnp.sum(shifted[2], axis=0, keepdims=True))
        for e in range(2):
            cols = slice(e * GT, (e + 1) * GT)
            da_ref[e] = da[:, cols].astype(BF16)
            for r, sm in enumerate(sums):
                red_ref[e, r:r + 1, :] += sm[:, cols]

    nhb = s // HALO
    tile = pl.BlockSpec((2, ts, GT), lambda j, i: (0, i, j))
    return pl.pallas_call(
        body, name=name, grid=(NGT, nt),
        in_specs=[tile, tile,
                  pl.BlockSpec((2, HALO, GT), lambda j, i: (0, jnp.minimum((i + 1) * hb, nhb - 1), j)),
                  pl.BlockSpec((ts, GT), lambda j, i: (i, j)),
                  pl.BlockSpec((HALO, GT), lambda j, i: (jnp.minimum((i + 1) * hb, nhb - 1), j)),
                  pl.BlockSpec((2, 8, GT), lambda j, i: (0, 0, j))],
        out_specs=[tile, pl.BlockSpec((2, 8, GT), lambda j, i: (0, 0, j))],
        out_shape=[jax.ShapeDtypeStruct((2, s, DFF), BF16), jax.ShapeDtypeStruct((2, 8, DFF), F32)],
        compiler_params=_cparams(("arbitrary", "arbitrary")),
    )(a, ap, ap, df, df, cw)


def _chunk_mask(transposed=False):
    t = lax.broadcasted_iota(jnp.int32, (SGB, SGB), 0) // CHUNK
    u = lax.broadcasted_iota(jnp.int32, (SGB, SGB), 1) // CHUNK
    return (t <= u) if transposed else (u <= t)


def _sgu_ln(v, gain, bias):
    mu = jnp.mean(v, axis=-1, keepdims=True)
    vc = v - mu
    rstd = lax.rsqrt(jnp.mean(vc * vc, axis=-1, keepdims=True) + EPS)
    vhat = vc * rstd
    return vhat * gain + bias, vhat, rstd


def _sgu_fwd(z, vgain, vbias, ws, bst, tr, name):
    s = z.shape[0]

    def body(zu_ref, zv_ref, vg_ref, vb_ref, ws_ref, bs_ref, y_ref):
        u = _gelu(zu_ref[...].astype(F32))
        vn, _, _ = _sgu_ln(_gelu(zv_ref[...].astype(F32)), vg_ref[...], vb_ref[...])
        vn = vn.astype(BF16)
        mask = _chunk_mask()
        for g in range(SGG):
            w = jnp.where(mask, ws_ref[g], 0.0).astype(BF16)
            for b in range(tr // SGB):
                rs, cs = slice(b * SGB, (b + 1) * SGB), slice(g * SGC, (g + 1) * SGC)
                mixed = _dot(w, vn[rs, cs]) + bs_ref[:, g:g + 1]
                y_ref[rs, cs] = (u[rs, cs] * mixed).astype(BF16)

    vec = pl.BlockSpec((1, SGW), lambda i: (0, 0))
    return pl.pallas_call(
        body, name=name, grid=(s // tr,),
        in_specs=[pl.BlockSpec((tr, SGW), lambda i: (i, 0)), pl.BlockSpec((tr, SGW), lambda i: (i, 1)),
                  vec, vec, pl.BlockSpec((SGG, SGB, SGB), lambda i: (0, 0, 0)),
                  pl.BlockSpec((SGB, LANES), lambda i: (0, 0))],
        out_specs=pl.BlockSpec((tr, SGW), lambda i: (i, 0)),
        out_shape=jax.ShapeDtypeStruct((s, SGW), BF16),
        compiler_params=_cparams(("arbitrary",)),
    )(z, z, vgain, vbias, ws, bst)


def _sgu_bwd(z, dy, vgain, vbias, ws, wst, bst, tr, name):
    s = z.shape[0]

    def body(zu_ref, zv_ref, dy_ref, vg_ref, vb_ref, ws_ref, wst_ref, bs_ref,
             dz_ref, rb_ref, rv_ref, dws_ref, dbs_ref, dvn_s):
        @pl.when(pl.program_id(0) == 0)
        def _():
            rb_ref[...] = jnp.zeros_like(rb_ref)
            rv_ref[...] = jnp.zeros_like(rv_ref)
            dws_ref[...] = jnp.zeros_like(dws_ref)
            dbs_ref[...] = jnp.zeros_like(dbs_ref)
        zu = zu_ref[...].astype(F32)
        zv = zv_ref[...].astype(F32)
        u, gu = _gelu_and_grad(zu)
        v, gv = _gelu_and_grad(zv)
        vn, vhat, rstd = _sgu_ln(v, vg_ref[...], vb_ref[...])
        vnb = vn.astype(BF16)
        dyv = dy_ref[...].astype(F32)
        dmix = (dyv * u).astype(BF16)
        mask = _chunk_mask()
        mask_t = _chunk_mask(transposed=True)
        lane = _lane((SGB, LANES))
        dbs = jnp.zeros((SGB, LANES), F32)
        for g in range(SGG):
            w = jnp.where(mask, ws_ref[g], 0.0).astype(BF16)
            wt = jnp.where(mask_t, wst_ref[g], 0.0).astype(BF16)
            dw = jnp.zeros((SGB, SGB), F32)
            for b in range(tr // SGB):
                rs, cs = slice(b * SGB, (b + 1) * SGB), slice(g * SGC, (g + 1) * SGC)
                mixed = _dot(w, vnb[rs, cs]) + bs_ref[:, g:g + 1]
                dz_ref[rs, cs] = (dyv[rs, cs] * mixed * gu[rs, cs]).astype(BF16)
                dm = dmix[rs, cs]
                dw = dw + _dot_nt(dm, vnb[rs, cs])
                dbs = dbs + jnp.where(lane == g, jnp.sum(dm.astype(F32), axis=-1, keepdims=True), 0.0)
                dvn_s[rs, cs] = _dot(wt, dm)
            dws_ref[g] += jnp.where(mask, dw, 0.0)
        dbs_ref[...] += dbs
        dvn = dvn_s[...]
        rv_ref[0:1, :] += jnp.sum(dvn * vhat, axis=0, keepdims=True)
        rv_ref[1:2, :] += jnp.sum(dvn, axis=0, keepdims=True)
        dvh = dvn * vg_ref[...]
        dv = rstd * (dvh - jnp.mean(dvh, axis=-1, keepdims=True)
                     - vhat * jnp.mean(dvh * vhat, axis=-1, keepdims=True))
        dz_ref[:, SGW:] = (dv * gv).astype(BF16)
        dzf = dz_ref[...].astype(F32)
        rb_ref[0:1, :] += jnp.sum(dzf, axis=0, keepdims=True)

    vec = pl.BlockSpec((1, SGW), lambda i: (0, 0))
    wsp = pl.BlockSpec((SGG, SGB, SGB), lambda i: (0, 0, 0))
    return pl.pallas_call(
        body, name=name, grid=(s // tr,),
        in_specs=[pl.BlockSpec((tr, SGW), lambda i: (i, 0)), pl.BlockSpec((tr, SGW), lambda i: (i, 1)),
                  pl.BlockSpec((tr, SGW), lambda i: (i, 0)), vec, vec, wsp, wsp,
                  pl.BlockSpec((SGB, LANES), lambda i: (0, 0))],
        out_specs=[pl.BlockSpec((tr, 2 * SGW), lambda i: (i, 0)),
                   pl.BlockSpec((8, 2 * SGW), lambda i: (0, 0)),
                   pl.BlockSpec((8, SGW), lambda i: (0, 0)), wsp,
                   pl.BlockSpec((SGB, LANES), lambda i: (0, 0))],
        out_shape=[jax.ShapeDtypeStruct((s, 2 * SGW), BF16), jax.ShapeDtypeStruct((8, 2 * SGW), F32),
                   jax.ShapeDtypeStruct((8, SGW), F32), jax.ShapeDtypeStruct((SGG, SGB, SGB), F32),
                   jax.ShapeDtypeStruct((SGB, LANES), F32)],
        scratch_shapes=[pltpu.VMEM((tr, SGW), F32)],
        compiler_params=_cparams(("arbitrary",)),
    )(z, z, dy, vgain, vbias, ws, wst, bst)


def _final_loss(x, fg, tgt, gprev, yprev, ts, name):
    s, d = x.shape

    def body(x_ref, fg_ref, t_ref, g_ref, y_ref, l_ref, dx_ref, dy_ref, red_ref):
        @pl.when(pl.program_id(0) == 0)
        def _():
            l_ref[...] = jnp.zeros_like(l_ref)
            red_ref[...] = jnp.zeros_like(red_ref)
        xv = x_ref[...]
        r = _rstd_rows(xv)
        xh = xv * r
        err = xh * fg_ref[...] - t_ref[...]
        l_ref[...] += 0.5 * jnp.sum(jnp.mean(err * err, axis=-1, keepdims=True))
        dyo = err * (1.0 / d)
        dxh = dyo * fg_ref[...]
        dx = r * (dxh - xh * jnp.mean(dxh * xh, axis=-1, keepdims=True))
        dx_ref[...] = dx
        dy_ref[...] = (dx * g_ref[...]).astype(BF16)
        red_ref[0:1, :] += jnp.sum(dyo * xh, axis=0, keepdims=True)
        red_ref[1:2, :] += jnp.sum(dx * y_ref[...].astype(F32), axis=0, keepdims=True)

    row = pl.BlockSpec((ts, d), lambda i: (i, 0))
    vec = pl.BlockSpec((1, d), lambda i: (0, 0))
    return pl.pallas_call(
        body, name=name, grid=(s // ts,),
        in_specs=[row, vec, row, vec, row],
        out_specs=[pl.BlockSpec((8, LANES), lambda i: (0, 0)), row, row, pl.BlockSpec((8, d), lambda i: (0, 0))],
        out_shape=[jax.ShapeDtypeStruct((8, LANES), F32), jax.ShapeDtypeStruct((s, d), F32),
                   jax.ShapeDtypeStruct((s, d), BF16), jax.ShapeDtypeStruct((8, d), F32)],
        compiler_params=_cparams(("arbitrary",)),
    )(x, fg, tgt, gprev, yprev)


def _norm_bwd(xin, dh, dxout, ng, sc, gprev, yprev, ts, name):
    s, d = xin.shape
    has_prev = gprev is not None
    fused = isinstance(dh, tuple)
    if fused:
        a, w, tk = dh
        ka, kw = a.shape[-1], w.shape[-1]
        k = ka * (a.shape[0] if a.ndim == 3 else 1)

    def body(*refs):
        if fused:
            x_ref, a_ref, w_ref, dxo_ref, ng_ref, sc_ref = refs[:6]
            rest = refs[6:]
        else:
            x_ref, dh_ref, dxo_ref, ng_ref, sc_ref = refs[:5]
            rest = refs[5:]
        if has_prev:
            g_ref, y_ref, dx_ref, dy_ref, red_ref = rest
        else:
            dx_ref, red_ref = rest

        @pl.when(pl.program_id(0) == 0)
        def _():
            red_ref[...] = jnp.zeros_like(red_ref)
        if fused:
            dhv = None
            for g0 in range(0, k, tk):
                av = a_ref[g0 // ka, :, g0 % ka:g0 % ka + tk] if a.ndim == 3 else a_ref[:, g0:g0 + tk]
                wv = w_ref[g0 // kw, :, g0 % kw:g0 % kw + tk] if w.ndim == 3 else w_ref[:, g0:g0 + tk]
                part = _dot_nt(av, wv)
                dhv = part if dhv is None else dhv + part
        else:
            dhv = dh_ref[...]
        xv = x_ref[...]
        r = _rstd_rows(xv)
        xh = xv * r
        dr = dhv * (1.0 + sc_ref[...])
        t = dr * ng_ref[...]
        dx = dxo_ref[...] + r * (t - xh * jnp.mean(t * xh, axis=-1, keepdims=True))
        dx_ref[...] = dx
        red_ref[0:1, :] += jnp.sum(dhv, axis=0, keepdims=True)
        red_ref[1:2, :] += jnp.sum(dhv * (xh * ng_ref[...]), axis=0, keepdims=True)
        red_ref[2:3, :] += jnp.sum(dr * xh, axis=0, keepdims=True)
        if has_prev:
            dy_ref[...] = (dx * g_ref[...]).astype(BF16)
            red_ref[3:4, :] += jnp.sum(dx * y_ref[...].astype(F32), axis=0, keepdims=True)

    row = pl.BlockSpec((ts, d), lambda i: (i, 0))
    vec = pl.BlockSpec((1, d), lambda i: (0, 0))
    red = pl.BlockSpec((8, d), lambda i: (0, 0))
    if fused:
        a_spec = (pl.BlockSpec((a.shape[0], ts, ka), lambda i: (0, i, 0)) if a.ndim == 3
                  else pl.BlockSpec((ts, ka), lambda i: (i, 0)))
        w_spec = pl.BlockSpec(w.shape, (lambda i: (0, 0, 0)) if w.ndim == 3 else (lambda i: (0, 0)))
        dh_specs, dh_args = [a_spec, w_spec], (a, w)
    else:
        dh_specs, dh_args = [row], (dh,)
    if has_prev:
        in_specs, args = [row] + dh_specs + [row, vec, vec, vec, row], (xin,) + dh_args + (dxout, ng, sc, gprev, yprev)
        out_specs = [row, row, red]
        out_shape = [jax.ShapeDtypeStruct((s, d), F32), jax.ShapeDtypeStruct((s, d), BF16),
                     jax.ShapeDtypeStruct((8, d), F32)]
    else:
        in_specs, args = [row] + dh_specs + [row, vec, vec], (xin,) + dh_args + (dxout, ng, sc)
        out_specs = [row, red]
        out_shape = [jax.ShapeDtypeStruct((s, d), F32), jax.ShapeDtypeStruct((8, d), F32)]
    return pl.pallas_call(
        body, name=name, grid=(s // ts,), in_specs=in_specs, out_specs=out_specs, out_shape=out_shape,
        compiler_params=_cparams(("arbitrary",)),
    )(*args)


def _ada_mod(c_all, ada_w, ada_b):
    nb = c_all.shape[0]
    da = ada_w.shape[2]

    def body(c_ref, w_ref, b_ref, o_ref, ca_ref):
        cv = c_ref[...]
        ca = cv * _sigmoid(cv)
        ca_ref[...] = ca
        o_ref[0] = lax.dot_general(ca, w_ref[0], (((1,), (0,)), ((), ())), precision=lax.Precision.HIGHEST,
                                   preferred_element_type=F32) + b_ref[0]

    return pl.pallas_call(
        body, name="ada_mod", grid=(2,),
        in_specs=[pl.BlockSpec((nb, D), lambda i: (0, 0)), pl.BlockSpec((1, D, da), lambda i: (i, 0, 0)),
                  pl.BlockSpec((1, 1, da), lambda i: (i, 0, 0))],
        out_specs=[pl.BlockSpec((1, nb, da), lambda i: (i, 0, 0)), pl.BlockSpec((nb, D), lambda i: (0, 0))],
        out_shape=[jax.ShapeDtypeStruct((2, nb, da), F32), jax.ShapeDtypeStruct((nb, D), F32)],
        compiler_params=_cparams(("arbitrary",)),
    )(c_all, ada_w, ada_b)


def _ada_w_grad(c_act_t, dmod):
    nb = c_act_t.shape[1]
    da = dmod.shape[2]
    tn = 512

    def body(c_ref, d_ref, o_ref):
        acc = c_ref[:, 0:1] * d_ref[0, 0:1, :]
        for b in range(1, nb):
            acc = acc + c_ref[:, b:b + 1] * d_ref[0, b:b + 1, :]
        o_ref[0] = acc

    return pl.pallas_call(
        body, name="ada_w_grad", grid=(2, da // tn),
        in_specs=[pl.BlockSpec((D, nb), lambda i, j: (0, 0)), pl.BlockSpec((1, nb, tn), lambda i, j: (i, 0, j))],
        out_specs=pl.BlockSpec((1, D, tn), lambda i, j: (i, 0, j)),
        out_shape=jax.ShapeDtypeStruct((2, D, da), F32),
        compiler_params=_cparams(("arbitrary", "arbitrary")),
    )(c_act_t, dmod)


def _conv_planes(cw, cb):
    cwp = jnp.swapaxes(cw.reshape(3, 2, DFF), 0, 1)
    return jnp.pad(cwp, ((0, 0), (0, 5), (0, 0))), cb.reshape(2, 1, DFF)


def _local_step(x, tgt, mod, wts, small, comm=None):
    wts = dict(wts)
    s = x.shape[0]
    ts, tq, tr, tp = TS, TQ, TR, TP
    tkw = min(TKW, s)
    tf = min(256, s)
    zb = lambda n: jnp.zeros((1, n), F32)
    m6 = mod.reshape(2, 6, 1, D)
    sh1, sc1, g1, sh2, sc2, g2 = ([m6[i, k] for i in range(2)] for k in range(6))
    n1g, n2g = small["norm1_g"], small["norm2_g"]
    row = lambda a, i: a[i:i + 1]

    qg2 = jnp.tile(small["fox_q_gain"], (1, 2))
    kg2 = jnp.tile(small["fox_k_gain"], (1, 2))
    bfp = jnp.pad(small["fox_b_f"], ((0, 0), (0, LANES - H)))
    proj, h1 = _norm_mod_matmul(x, row(n1g, 0), sc1[0], sh1[0], wts["fox_w_in"], zb(FOX_NP), F32, ts, 1408, "fox_in")
    qa, ka, va = _fox_post(proj, qg2, kg2, bfp, tp, "fox_post")
    att, ql, gathered = _attn_fwd(qa, ka, va, tq, "attn_fwd", shards=comm["shards"] if comm else ())
    if comm:
        wts.update(comm["make_wts"](gathered))
    gated = _gate(att, proj, ts, "fox_gate")
    x1, y0 = _matmul_residual(gated, wts["fox_w_out"], x, g1[0], ts, "fox_out")

    def ffn_fwd(xin, i, tag):
        cw, cb = _conv_planes(small["ffn_conv_w"][i], small["ffn_conv_b"][i])
        a, h = _norm_mod_matmul(xin, row(n2g, i), sc2[i], sh2[i], wts["ffn_w_up"][i], zb(2 * DFF), BF16, ts, 1408,
                                "ffn_up" + tag, planes=2)
        xo, y, f, ap = _conv_down(a, cw, cb, wts["ffn_w_down"][i], xin, g2[i], min(256, s), "ffn_conv_down" + tag)
        return xo, (a, h, f, y, cw, ap)

    x2, ffn0 = ffn_fwd(x1, 0, "0")

    bst = jnp.pad(small["sgu_b_s"].T, ((0, 0), (0, LANES - SGG)))
    ws = small["sgu_w_s"]
    z, h3 = _norm_mod_matmul(x2, row(n1g, 1), sc1[1], sh1[1], wts["sgu_w_in"], small["sgu_b_in"], BF16, ts, 1024,
                             "sgu_in")
    yy = _sgu_fwd(z, small["sgu_v_gain"], small["sgu_v_bias"], ws, bst, tr, "sgu_mix")
    x3, y1 = _matmul_residual(yy, wts["sgu_w_out"], x2, g1[1], ts, "sgu_out")
    x4, ffn1 = ffn_fwd(x3, 1, "1")

    lsum, dx4, dy, redf = _final_loss(x4, small["final_g"], tgt, g2[1], ffn1[3], ts, "final_loss")
    grads = {"final_g": redf[0]}
    dmod = [[None] * 6, [None] * 6]
    dmod[1][5] = redf[1]

    def ffn_bwd(dxo, dy2, xin, i, saved, gprev, yprev, tag):
        a, h, f, _, cw, ap = saved
        wd, wu = wts["ffn_w_down"][i], wts["ffn_w_up"][i]
        g_wd = _matmul(f, dy2, True, False, 1408, D, tkw, BF16, "ffn_dwdown" + tag)
        da, redc = _down_bwd_conv(dy2, wd, a, ap, cw, min(256, s), "ffn_down_bwd_conv" + tag)
        g_wu = _matmul(h, da, True, False, D, 1408, tkw, BF16, "ffn_dwup" + tag, out_parts=N_CHIP)
        outs = _norm_bwd(xin, (da, wu, 1408), dxo, row(n2g, i), sc2[i], gprev, yprev, tf, "ffn_dh_norm_bwd" + tag)
        return outs, g_wd, g_wu, redc

    (dx3, dy1, red), g_wd1, g_wu1, redc1 = ffn_bwd(dx4, dy, x3, 1, ffn1, g1[1], y1, "1")
    dmod[1][3], dmod[1][4], dn2g1, dmod[1][2] = red[0], red[1], red[2], red[3]

    g_swo = _matmul(yy, dy1, True, False, 1024, D, tkw, BF16, "sgu_dwout")
    dyy = _matmul_wt(dy1, wts["sgu_w_out"], 1024, D, BF16, ts, "sgu_dyy")
    wst = jnp.swapaxes(ws, 1, 2)
    dz, rb, rv, dws, dbst = _sgu_bwd(z, dyy, small["sgu_v_gain"], small["sgu_v_bias"], ws, wst, bst, tr, "sgu_mix_bwd")
    g_swi = _matmul(h3, dz, True, False, D, 1024, tkw, BF16, "sgu_dwin", out_parts=N_CHIP)
    dx2, dy2_0, red = _norm_bwd(x2, (dz, wts["sgu_w_in"], 1024), dx3, row(n1g, 1), sc1[1], g2[0], ffn0[3], tf,
                                "sgu_dh_norm_bwd")
    dmod[1][0], dmod[1][1], dn1g1, dmod[0][5] = red[0], red[1], red[2], red[3]

    (dx1, dy0, red), g_wd0, g_wu0, redc0 = ffn_bwd(dx2, dy2_0, x1, 0, ffn0, g1[0], y0, "0")
    dmod[0][3], dmod[0][4], dn2g0, dmod[0][2] = red[0], red[1], red[2], red[3]

    g_fwo = _matmul(gated, dy0, True, False, D, D, tkw, BF16, "fox_dwout")
    dgated = _matmul_wt(dy0, wts["fox_w_out"], D, D, F32, ts, "fox_dgated")
    doa, dop = _attn_bwd_prep(dgated, att, proj, ts, "attn_bwd_prep")
    css = comm["rs_prepare"]([g_fwo, g_swi, g_swo, g_wu0, g_wu1, g_wd0, g_wd1]) if comm else []
    dqa, dka, dva, rcvs = _attn_bwd(ql, ka, va, doa, tq, "attn_bwd", css=css)
    dproj, redx = _fox_post_bwd(proj, dqa, dka, dva, dop, qg2, kg2, bfp, tp, "fox_post_bwd")
    g_fwi = _matmul(h1, dproj, True, False, D, 1408, tkw, BF16, "fox_dwin")
    dx0, red = _norm_bwd(x, (dproj, wts["fox_w_in"], 1408), dx1, row(n1g, 0), sc1[0], None, None, tf,
                         "fox_dh_norm_bwd")
    dmod[0][0], dmod[0][1], dn1g0 = red[0], red[1], red[2]

    grads.update(
        fox_w_in=g_fwi, fox_w_out=g_fwo, sgu_w_in=g_swi, sgu_w_out=g_swo,
        ffn_w_up=[g_wu0, g_wu1], ffn_w_down=[g_wd0, g_wd1],
        fox_q_gain=redx[0, :DH] + redx[0, DH:], fox_k_gain=redx[1, :DH] + redx[1, DH:], fox_b_f=redx[2, :H],
        sgu_b_in=rb[0], sgu_v_gain=rv[0], sgu_v_bias=rv[1], sgu_w_s=dws, sgu_b_s=dbst[:, :SGG].T,
        ffn_conv_w=jnp.stack([jnp.swapaxes(r[:, 0:3], 0, 1).reshape(3, 2 * DFF) for r in (redc0, redc1)]),
        ffn_conv_b=jnp.stack([r[:, 3].reshape(2 * DFF) for r in (redc0, redc1)]),
        norm1_g=jnp.stack([dn1g0, dn1g1]), norm2_g=jnp.stack([dn2g0, dn2g1]),
    )
    dmod_arr = jnp.stack([jnp.concatenate(dmod[0]), jnp.concatenate(dmod[1])])
    return lsum[0, 0], dx0, grads, dmod_arr, (css, rcvs)


N_DEV = 8
N_CHIP = 4
HBM_SPEC = pl.BlockSpec(memory_space=pltpu.HBM)
VMEM_SPEC = pl.BlockSpec(memory_space=pltpu.VMEM)


def _mesh_pos():
    return lax.axis_index("x"), lax.axis_index("y"), lax.axis_index("c")


def _other_chips(x, y):
    return [(1 - x, y), (x, 1 - y), (1 - x, 1 - y)]


def _remote(src, dst, ssem, rsem, dev):
    return pltpu.make_async_remote_copy(src_ref=src, dst_ref=dst, send_sem=ssem, recv_sem=rsem,
                                        device_id=dev, device_id_type=MESH)


def _allgather8(xb, name):
    m_per, n = xb.shape

    def body(x_ref, out_ref, send_sems, recv_sems, local_sem):
        x, y, c = _mesh_pos()
        me, sibling = (x, y, c), (x, y, 1 - c)
        chips = _other_chips(x, y)

        def rows(px, py, pc):
            return out_ref.at[pl.ds((4 * px + 2 * py + pc) * m_per, m_per), :]

        def copy(k, block, to, src=None):
            return _remote(rows(*block) if src is None else src, rows(*block),
                           send_sems.at[k], recv_sems.at[k], to)

        mine = pltpu.make_async_copy(x_ref, rows(*me), local_sem)
        mine.start()
        first = [copy(0, me, sibling, src=x_ref)]
        first += [copy(1 + j, me, (*chip, c), src=x_ref) for j, chip in enumerate(chips)]
        for cp in first:
            cp.start()
        passed = [copy(4 + j, (*chip, c), sibling) for j, chip in enumerate(chips)]
        for j, chip in enumerate(chips):
            copy(1 + j, (*chip, c), me).wait_recv()
            passed[j].start()
        copy(0, sibling, me).wait_recv()
        for j, chip in enumerate(chips):
            copy(4 + j, (*chip, 1 - c), me).wait_recv()
        for cp in first + passed:
            cp.wait_send()
        mine.wait()

    return pl.pallas_call(
        body, name=name,
        out_shape=jax.ShapeDtypeStruct((N_DEV * m_per, n), xb.dtype),
        in_specs=[VMEM_SPEC], out_specs=VMEM_SPEC,
        scratch_shapes=[pltpu.SemaphoreType.DMA((7,)), pltpu.SemaphoreType.DMA((7,)), pltpu.SemaphoreType.DMA],
        compiler_params=pltpu.CompilerParams(vmem_limit_bytes=V7X_VMEM_LIMIT),
    )(xb)


def _gather_shards(shards, name):
    na = len(shards)

    def body(*refs):
        p_refs, o_refs = refs[:na], refs[na:2 * na]
        send_sems, recv_sems, pass_send, pass_recv = refs[2 * na:]
        x, y, c = _mesh_pos()
        me = 2 * x + y
        sibling = (x, y, 1 - c)
        chips = _other_chips(x, y)

        def half(a, ci, hf):
            rh = shards[a].shape[0] // 2
            return o_refs[a].at[ci, pl.ds(hf * rh, rh), :]

        sends = []
        for a in range(na):
            rh = shards[a].shape[0] // 2
            for k, chip in enumerate(chips):
                sends.append(_remote(p_refs[a].at[pl.ds(c * rh, rh), :], half(a, me, c),
                                     send_sems.at[3 * a + k], recv_sems.at[3 * a + k], (*chip, c)))
        for cp in sends:
            cp.start()
        passed = []
        for a in range(na):
            for k, chip in enumerate(chips):
                ci = 2 * chip[0] + chip[1]
                _remote(half(a, ci, c), half(a, ci, c), send_sems.at[3 * a + k], recv_sems.at[3 * a + k],
                        (*chip, c)).wait_recv()
                cp = _remote(half(a, ci, c), half(a, ci, c), pass_send.at[3 * a + k], pass_recv.at[3 * a + k], sibling)
                cp.start()
                passed.append(cp)
        for a in range(na):
            for k, chip in enumerate(chips):
                ci = 2 * chip[0] + chip[1]
                _remote(half(a, ci, 1 - c), half(a, ci, 1 - c), pass_send.at[3 * a + k], pass_recv.at[3 * a + k],
                        sibling).wait_recv()
        for cp in sends + passed:
            cp.wait_send()

    return pl.pallas_call(
        body, name=name,
        out_shape=[jax.ShapeDtypeStruct((N_CHIP,) + p.shape, p.dtype) for p in shards],
        in_specs=[HBM_SPEC] * na, out_specs=[HBM_SPEC] * na,
        scratch_shapes=[pltpu.SemaphoreType.DMA((3 * na,))] * 4,
    )(*shards)


def _rs_to_sibling(gs, name):
    na = len(gs)

    def body(*refs):
        g_refs, o_refs, ssems, rsems = refs[:na], refs[na:2 * na], refs[2 * na], refs[2 * na + 1]
        x, y, c = _mesh_pos()
        cps = []
        for a in range(na):
            rh = gs[a].shape[1] // 2
            cp = _remote(g_refs[a].at[:, pl.ds((1 - c) * rh, rh), :], o_refs[a], ssems.at[a], rsems.at[a],
                         (x, y, 1 - c))
            cp.start()
            cps.append(cp)
        for cp in cps:
            cp.wait()

    return pl.pallas_call(
        body, name=name,
        out_shape=[jax.ShapeDtypeStruct((g.shape[0], g.shape[1] // 2, g.shape[2]), g.dtype) for g in gs],
        in_specs=[HBM_SPEC] * na, out_specs=[HBM_SPEC] * na,
        scratch_shapes=[pltpu.SemaphoreType.DMA((na,)), pltpu.SemaphoreType.DMA((na,))],
    )(*gs)


def _rs_chip_sum(g, sib, c_arr, tr, name):
    nc, r, n = g.shape
    rh = r // 2
    g4 = g.reshape(nc, 2, rh, n)

    def body(c_ref, g_ref, s_ref, o_ref):
        o_ref[...] = (g_ref[0].astype(F32) + s_ref[...].astype(F32)).astype(BF16)

    return pl.pallas_call(
        body, name=name, out_shape=jax.ShapeDtypeStruct((nc, rh, n), BF16),
        grid_spec=pltpu.PrefetchScalarGridSpec(
            num_scalar_prefetch=1, grid=(nc, rh // tr),
            in_specs=[pl.BlockSpec((1, 1, tr, n), lambda j, i, cr: (j, cr[0], i, 0)),
                      pl.BlockSpec((1, tr, n), lambda j, i, cr: (j, i, 0))],
            out_specs=pl.BlockSpec((1, tr, n), lambda j, i, cr: (j, i, 0))),
        compiler_params=_cparams(("arbitrary", "arbitrary")),
    )(c_arr, g4, sib)


def _rs_across_chips(css, name):
    na = len(css)

    def body(*refs):
        cs_refs, o_refs, send_sems, recv_sems = refs[:na], refs[na:2 * na], refs[2 * na], refs[2 * na + 1]
        x, y, c = _mesh_pos()
        cps = []
        for a in range(na):
            for k, chip in enumerate(_other_chips(x, y)):
                ci = 2 * chip[0] + chip[1]
                cp = _remote(cs_refs[a].at[ci], o_refs[a].at[k], send_sems.at[3 * a + k], recv_sems.at[3 * a + k],
                             (*chip, c))
                cp.start()
                cps.append(cp)
        for cp in cps:
            cp.wait()

    return pl.pallas_call(
        body, name=name, out_shape=[jax.ShapeDtypeStruct((3,) + cs.shape[1:], cs.dtype) for cs in css],
        in_specs=[HBM_SPEC] * na, out_specs=[HBM_SPEC] * na,
        scratch_shapes=[pltpu.SemaphoreType.DMA((3 * na,)), pltpu.SemaphoreType.DMA((3 * na,))],
    )(*css)


def _rs_final_sum(cs, rcv, me_arr, tr, name):
    nc, rh, n = cs.shape

    def body(m_ref, c_ref, r_ref, o_ref):
        acc = c_ref[0].astype(F32)
        for k in range(3):
            acc = acc + r_ref[k].astype(F32)
        o_ref[...] = acc

    return pl.pallas_call(
        body, name=name, out_shape=jax.ShapeDtypeStruct((rh, n), F32),
        grid_spec=pltpu.PrefetchScalarGridSpec(
            num_scalar_prefetch=1, grid=(rh // tr,),
            in_specs=[pl.BlockSpec((1, tr, n), lambda i, mr: (mr[0], i, 0)),
                      pl.BlockSpec((3, tr, n), lambda i, mr: (0, i, 0))],
            out_specs=pl.BlockSpec((tr, n), lambda i, mr: (i, 0))),
        compiler_params=_cparams(("arbitrary",)),
    )(me_arr, cs, rcv)


def _rs_swap_halves(halves, name):
    na = len(halves)

    def body(*refs):
        h_refs, o_refs, ssems, rsems = refs[:na], refs[na:2 * na], refs[2 * na], refs[2 * na + 1]
        x, y, c = _mesh_pos()
        cps = []
        for a in range(na):
            cp = _remote(h_refs[a], o_refs[a], ssems.at[a], rsems.at[a], (x, y, 1 - c))
            cp.start()
            cps.append(cp)
        for cp in cps:
            cp.wait()

    return pl.pallas_call(
        body, name=name, out_shape=[jax.ShapeDtypeStruct(h.shape, h.dtype) for h in halves],
        in_specs=[HBM_SPEC] * na, out_specs=[HBM_SPEC] * na,
        scratch_shapes=[pltpu.SemaphoreType.DMA((na,)), pltpu.SemaphoreType.DMA((na,))],
    )(*halves)


def _join_columns(parts, n_out, name):
    p, k, c = parts.shape
    tr = 128

    def body(w_ref, o_ref):
        for j in range(p):
            o_ref[:, j * c:(j + 1) * c] = w_ref[j]
        o_ref[:, p * c:] = jnp.zeros((tr, n_out - p * c), parts.dtype)

    return pl.pallas_call(
        body, name=name, grid=(k // tr,),
        in_specs=[pl.BlockSpec((p, tr, c), lambda i: (0, i, 0))],
        out_specs=pl.BlockSpec((tr, n_out), lambda i: (i, 0)),
        out_shape=jax.ShapeDtypeStruct((k, n_out), parts.dtype),
        compiler_params=_cparams(("arbitrary",)),
    )(parts)


def _split_columns(g, p, c, name):
    k, n = g.shape
    tr = 128

    def body(g_ref, o_ref):
        for j in range(p):
            o_ref[j] = g_ref[:, j * c:(j + 1) * c]

    return pl.pallas_call(
        body, name=name, grid=(k // tr,),
        in_specs=[pl.BlockSpec((tr, n), lambda i: (i, 0))],
        out_specs=pl.BlockSpec((p, tr, c), lambda i: (0, i, 0)),
        out_shape=jax.ShapeDtypeStruct((p, k, c), g.dtype),
        compiler_params=_cparams(("arbitrary",)),
    )(g)


def _sum8(g, name):
    nd, r, n = g.shape

    def body(g_ref, o_ref):
        acc = g_ref[0]
        for k in range(1, nd):
            acc = acc + g_ref[k]
        o_ref[...] = acc

    return pl.pallas_call(
        body, name=name, grid=(r // 8,),
        in_specs=[pl.BlockSpec((nd, 8, n), lambda i: (0, i, 0))],
        out_specs=pl.BlockSpec((8, n), lambda i: (i, 0)),
        out_shape=jax.ShapeDtypeStruct((r, n), F32),
        compiler_params=_cparams(("arbitrary",)),
    )(g)


def _adamw(w, g, m, v, name):
    r, n = w.shape
    tr = 128 if r % 128 == 0 else 8
    bc1 = 1.0 - ADAM_B1 ** ADAM_STEP
    bc2 = 1.0 - ADAM_B2 ** ADAM_STEP

    def body(w_ref, g_ref, m_ref, v_ref, d_ref, mo_ref, vo_ref):
        gv = g_ref[...]
        mn = ADAM_B1 * m_ref[...] + (1.0 - ADAM_B1) * gv
        vn = ADAM_B2 * v_ref[...] + (1.0 - ADAM_B2) * (gv * gv)
        d_ref[...] = -ADAM_LR * ((mn / bc1) / (jnp.sqrt(vn / bc2) + ADAM_EPS) + ADAM_WD * w_ref[...])
        mo_ref[...] = mn
        vo_ref[...] = vn

    blk = pl.BlockSpec((tr, n), lambda i: (i, 0))
    shp = jax.ShapeDtypeStruct((r, n), F32)
    return pl.pallas_call(
        body, name=name, grid=(r // tr,), in_specs=[blk] * 4, out_specs=[blk] * 3, out_shape=[shp] * 3,
        compiler_params=_cparams(("arbitrary",)),
    )(w, g, m, v)


ROW = 1024
PACK_ROWS = 7168
BIG = ("fox_w_in", "fox_w_out", "sgu_w_in", "sgu_w_out", "ffn_w_up", "ffn_w_down")
SMALL_SHARDED = ("sgu_b_in", "sgu_v_gain", "sgu_v_bias", "ffn_conv_w")
SMALL_REPL = ("fox_b_f", "fox_q_gain", "fox_k_gain", "sgu_w_s", "sgu_b_s", "ffn_conv_b", "ada_b",
              "norm1_g", "norm2_g", "final_g")
WEIGHTS = ("fox_w_in", "fox_b_f", "fox_q_gain", "fox_k_gain", "fox_w_out", "sgu_w_in", "sgu_b_in", "sgu_v_gain",
           "sgu_v_bias", "sgu_w_s", "sgu_b_s", "sgu_w_out", "ffn_w_up", "ffn_conv_w", "ffn_conv_b", "ffn_w_down",
           "ada_w", "ada_b", "norm1_g", "norm2_g", "final_g")


def _rows_of(a, mult=1):
    flat = a.reshape(-1)
    rows = -(-flat.shape[0] // ROW)
    rows = -(-rows // mult) * mult
    return jnp.pad(flat, (0, rows * ROW - flat.shape[0])).reshape(rows, ROW)


def _pack(parts, mult, total=None):
    p = jnp.concatenate([_rows_of(a, mult) for a in parts], axis=0)
    if total is not None:
        p = jnp.pad(p, ((0, total - p.shape[0]), (0, 0)))
    return p


def _unpack(pack, shapes, mult):
    out, r0 = [], 0
    for shp in shapes:
        size = int(np.prod(shp))
        rows = -(-(-(-size // ROW)) // mult) * mult
        out.append(pack[r0:r0 + rows].reshape(-1)[:size].reshape(shp))
        r0 += rows
    return out


def _big_shards(t):
    return [t["fox_w_in"][0], t["fox_w_out"][0], t["sgu_w_in"][0], t["sgu_w_out"][0],
            t["ffn_w_up"][0], t["ffn_w_up"][1], t["ffn_w_down"][0], t["ffn_w_down"][1]]


def _row_tile(rows):
    return next(t for t in (512, 352, 256, 128, 64) if rows % t == 0)


def kernel(x, c, fox_w_in, fox_b_f, fox_q_gain, fox_k_gain, fox_w_out, sgu_w_in, sgu_b_in, sgu_v_gain, sgu_v_bias, sgu_w_s, sgu_b_s, sgu_w_out, ffn_w_up, ffn_conv_w, ffn_conv_b, ffn_w_down, ada_w, ada_b, norm1_g, norm2_g, final_g, loss_target, m_fox_w_in, m_fox_b_f, m_fox_q_gain, m_fox_k_gain, m_fox_w_out, m_sgu_w_in, m_sgu_b_in, m_sgu_v_gain, m_sgu_v_bias, m_sgu_w_s, m_sgu_b_s, m_sgu_w_out, m_ffn_w_up, m_ffn_conv_w, m_ffn_conv_b, m_ffn_w_down, m_ada_w, m_ada_b, m_norm1_g, m_norm2_g, m_final_g, v_fox_w_in, v_fox_b_f, v_fox_q_gain, v_fox_k_gain, v_fox_w_out, v_sgu_w_in, v_sgu_b_in, v_sgu_v_gain, v_sgu_v_bias, v_sgu_w_s, v_sgu_b_s, v_sgu_w_out, v_ffn_w_up, v_ffn_conv_w, v_ffn_conv_b, v_ffn_w_down, v_ada_w, v_ada_b, v_norm1_g, v_norm2_g, v_final_g):
    w = dict(fox_w_in=fox_w_in, fox_b_f=fox_b_f, fox_q_gain=fox_q_gain, fox_k_gain=fox_k_gain, fox_w_out=fox_w_out,
             sgu_w_in=sgu_w_in, sgu_b_in=sgu_b_in, sgu_v_gain=sgu_v_gain, sgu_v_bias=sgu_v_bias, sgu_w_s=sgu_w_s,
             sgu_b_s=sgu_b_s, sgu_w_out=sgu_w_out, ffn_w_up=ffn_w_up, ffn_conv_w=ffn_conv_w, ffn_conv_b=ffn_conv_b,
             ffn_w_down=ffn_w_down, ada_w=ada_w, ada_b=ada_b, norm1_g=norm1_g, norm2_g=norm2_g, final_g=final_g)
    mom = dict(fox_w_in=m_fox_w_in, fox_b_f=m_fox_b_f, fox_q_gain=m_fox_q_gain, fox_k_gain=m_fox_k_gain,
               fox_w_out=m_fox_w_out, sgu_w_in=m_sgu_w_in, sgu_b_in=m_sgu_b_in, sgu_v_gain=m_sgu_v_gain,
               sgu_v_bias=m_sgu_v_bias, sgu_w_s=m_sgu_w_s, sgu_b_s=m_sgu_b_s, sgu_w_out=m_sgu_w_out,
               ffn_w_up=m_ffn_w_up, ffn_conv_w=m_ffn_conv_w, ffn_conv_b=m_ffn_conv_b, ffn_w_down=m_ffn_w_down,
               ada_w=m_ada_w, ada_b=m_ada_b, norm1_g=m_norm1_g, norm2_g=m_norm2_g, final_g=m_final_g)
    var = dict(fox_w_in=v_fox_w_in, fox_b_f=v_fox_b_f, fox_q_gain=v_fox_q_gain, fox_k_gain=v_fox_k_gain,
               fox_w_out=v_fox_w_out, sgu_w_in=v_sgu_w_in, sgu_b_in=v_sgu_b_in, sgu_v_gain=v_sgu_v_gain,
               sgu_v_bias=v_sgu_v_bias, sgu_w_s=v_sgu_w_s, sgu_b_s=v_sgu_b_s, sgu_w_out=v_sgu_w_out,
               ffn_w_up=v_ffn_w_up, ffn_conv_w=v_ffn_conv_w, ffn_conv_b=v_ffn_conv_b, ffn_w_down=v_ffn_w_down,
               ada_w=v_ada_w, ada_b=v_ada_b, norm1_g=v_norm1_g, norm2_g=v_norm2_g, final_g=v_final_g)

    ax, ay, ac = _mesh_pos()
    chip = 2 * ax + ay
    dev = 2 * chip + ac

    small_shard_shapes = tuple(w[n].shape for n in SMALL_SHARDED)
    blk = _pack([c] + [w[n] for n in SMALL_SHARDED], 1, 16)
    gat = _allgather8(blk, "gather_small").reshape(N_DEV, 16, ROW)
    c_all = gat[:, 0, :]
    per_chip = [_unpack(gat[2 * j, 1:], small_shard_shapes, 1) for j in range(N_CHIP)]
    full_small = {n: jnp.concatenate([per_chip[j][i] for j in range(N_CHIP)], axis=-1)
                  for i, n in enumerate(SMALL_SHARDED)}

    mine = [a.astype(BF16) for a in _big_shards(w)]
    with_own = lambda gat, own: [lax.dynamic_update_slice(g_, m_[None], (chip, 0, 0)) for g_, m_ in zip(gat, own)]
    fwi, = with_own(_gather_shards(mine[:1], "gather_fox_w_in"), mine[:1])
    fwi_full = _join_columns(fwi, FOX_NP, "join_fox_w_in")
    wts = dict(fox_w_in=fwi_full)

    def make_wts(gathered):
        fwo, swi, swo, up0, up1, dn0, dn1 = with_own(gathered, mine[1:])
        return dict(fox_w_out=fwo.reshape(D, D), sgu_w_in=swi, sgu_w_out=swo.reshape(SGW, D),
                    ffn_w_up=[up0, up1], ffn_w_down=[dn0.reshape(DFF, D), dn1.reshape(DFF, D)])

    c_arr = jnp.reshape(ac, (1,)).astype(jnp.int32)
    me_arr = jnp.reshape(chip, (1,)).astype(jnp.int32)

    def chip_sums(glist, tag):
        sibs = _rs_to_sibling(glist, "rs_sibling" + tag)
        return [_rs_chip_sum(g_, s_, c_arr, _row_tile(s_.shape[1]), "rs_chip_sum%s%d" % (tag, a))
                for a, (g_, s_) in enumerate(zip(glist, sibs))]

    def rs_prepare(gl):
        g_fwo, g_swi, g_swo, g_wu0, g_wu1, g_wd0, g_wd1 = gl
        return chip_sums([g_fwo.reshape(N_CHIP, 256, D), g_swi, g_swo.reshape(N_CHIP, 512, D), g_wu0, g_wu1,
                          g_wd0.reshape(N_CHIP, 704, D), g_wd1.reshape(N_CHIP, 704, D)], "")

    comm = dict(shards=mine[1:], make_wts=make_wts, rs_prepare=rs_prepare)

    da = ada_w.shape[2]
    ada_b_cols = lax.dynamic_slice_in_dim(ada_b, chip * da, da, axis=1)[:, None, :]
    mod_cols, c_act = _ada_mod(c_all, ada_w, ada_b_cols)
    mod_all = _allgather8(mod_cols.reshape(-1, ROW), "gather_mod").reshape(N_DEV, 2, N_DEV, da)
    mod_mine = lax.dynamic_index_in_dim(mod_all[0::2], dev, axis=2, keepdims=False)
    mod = jnp.swapaxes(mod_mine, 0, 1).reshape(2, N_CHIP * da)

    small = dict(norm1_g=norm1_g, norm2_g=norm2_g, final_g=final_g[None], fox_q_gain=fox_q_gain,
                 fox_k_gain=fox_k_gain, fox_b_f=fox_b_f, sgu_b_in=full_small["sgu_b_in"],
                 sgu_v_gain=full_small["sgu_v_gain"], sgu_v_bias=full_small["sgu_v_bias"], sgu_w_s=sgu_w_s[0],
                 sgu_b_s=sgu_b_s[0], ffn_conv_w=full_small["ffn_conv_w"], ffn_conv_b=ffn_conv_b)
    loss_dev, dx, g, dmod, (css, rcvs) = _local_step(x[0], loss_target[0], mod, wts, small, comm)

    g["ada_b"] = dmod
    g["loss"] = loss_dev
    small_names = ("ada_b",) + SMALL_SHARDED + tuple(n for n in SMALL_REPL if n != "ada_b") + ("loss",)
    gs = _pack([g[n] for n in small_names], 1)
    rows_s = -(-gs.shape[0] // 8) * 8
    gs = jnp.pad(gs, ((0, rows_s - gs.shape[0]), (0, 0)))
    gs_all = _allgather8(gs, "gather_small_grads").reshape(N_DEV, rows_s, ROW)
    gsum = _sum8(gs_all, "sum_small_grads")
    full_shapes = {n: w[n].shape for n in SMALL_REPL}
    full_shapes.update({n: w[n].shape[:-1] + (w[n].shape[-1] * N_CHIP,) for n in SMALL_SHARDED})
    full_shapes["loss"] = ()
    gfull = dict(zip(small_names, _unpack(gsum, [full_shapes[n] for n in small_names], 1)))
    grads = {n: gfull[n] for n in SMALL_REPL}
    for n in SMALL_SHARDED:
        width = w[n].shape[-1]
        grads[n] = lax.dynamic_slice_in_dim(gfull[n], chip * width, width, axis=gfull[n].ndim - 1)
    dmod_all = gs_all[:, :12, :].reshape(N_DEV, 2, N_CHIP * da)
    dmod_cols = jnp.swapaxes(lax.dynamic_slice_in_dim(dmod_all, chip * da, da, axis=2), 0, 1)
    grads["ada_w"] = _ada_w_grad(c_act.T, dmod_cols)

    gfi = _split_columns(g["fox_w_in"], N_CHIP, FOX_N // N_CHIP, "split_fox_w_in")
    cs_fox = chip_sums([gfi], "_fox")
    css = cs_fox + list(css)
    rcvs = list(_rs_across_chips(cs_fox, "rs_chips_fox")) + list(rcvs)
    halves = [_rs_final_sum(cs_, r_, me_arr, _row_tile(cs_.shape[1]), "rs_final_sum%d" % a)
              for a, (cs_, r_) in enumerate(zip(css, rcvs))]
    others = _rs_swap_halves(halves, "rs_swap")
    red = [jnp.concatenate([jnp.where(ac == 0, h_, o_), jnp.where(ac == 0, o_, h_)]) for h_, o_ in zip(halves, others)]
    grads.update(fox_w_in=red[0], fox_w_out=red[1], sgu_w_in=red[2], sgu_w_out=red[3],
                 ffn_w_up=jnp.stack([red[4], red[5]]), ffn_w_down=jnp.stack([red[6], red[7]]))

    delta, new_m, new_v = {}, {}, {}
    for n in BIG + ("ada_w",):
        shp = w[n].shape
        two_d = lambda a: a.reshape(-1, shp[-1])
        d_, m_, v_ = _adamw(two_d(w[n]), two_d(grads[n]), two_d(mom[n]), two_d(var[n]), "adamw_" + n)
        delta[n], new_m[n], new_v[n] = d_.reshape(shp), m_.reshape(shp), v_.reshape(shp)
    rest = SMALL_SHARDED + SMALL_REPL
    packs = [_pack([t[n] for n in rest], 1) for t in (w, grads, mom, var)]
    rows_r = -(-packs[0].shape[0] // 8) * 8
    packs = [jnp.pad(p, ((0, rows_r - p.shape[0]), (0, 0))) for p in packs]
    outs = _adamw(*packs, "adamw_small")
    for t, o in zip((delta, new_m, new_v), outs):
        t.update(zip(rest, _unpack(o, [w[n].shape for n in rest], 1)))

    loss = gfull["loss"]
    return (loss, dx[None], *[grads[n].reshape(w[n].shape) for n in WEIGHTS], *[delta[n] for n in WEIGHTS],
            *[new_m[n] for n in WEIGHTS], *[new_v[n] for n in WEIGHTS])
```

```python
import functools
import math

import numpy as np
import jax
import jax.numpy as jnp
from jax import lax
from jax.experimental import pallas as pl
from jax.experimental.pallas import tpu as pltpu

F32 = jnp.float32
BF16 = jnp.bfloat16
MESH = pl.DeviceIdType.MESH

D = 1024
H = 16
DH = 64
NP = H // 2
LANES = 128
DFF = 2816
SGW = 2048
SGG = 8
SGC = 256
SGB = 128
CHUNK = 64
EPS = 1e-6
FOX_N = 4 * D + H
FOX_NP = 4224
GT = 256
NGT = DFF // GT
SCALE = DH ** -0.5
LOG2E = 1.4426950408889634

ADAM_LR = 0.001
ADAM_B1 = 0.9
ADAM_B2 = 0.999
ADAM_EPS = 1e-08
ADAM_WD = 0.01
ADAM_STEP = 10

V7X_VMEM_LIMIT = 56 * 1024 * 1024

L_F = 64
L_NF = 67
L_LSE = 70


def _cparams(sem=None):
    return pltpu.CompilerParams(dimension_semantics=sem, vmem_limit_bytes=V7X_VMEM_LIMIT)


def _split3(x):
    hi = x.astype(BF16)
    r = x - hi.astype(F32)
    mid = r.astype(BF16)
    lo = (r - mid.astype(F32)).astype(BF16)
    return hi, mid, lo


def _dot(a, b, dims=(((1,), (0,)), ((), ()))):
    return lax.dot_general(a, b, dims, preferred_element_type=F32)


def _dot_nt(a, b):
    return _dot(a, b, (((1,), (1,)), ((), ())))


def _dot_tn(a, b):
    return _dot(a, b, (((0,), (0,)), ((), ())))


def _exact_dot(m_bf16, x_f32):
    hi, mid, lo = _split3(x_f32)
    return _dot(m_bf16, hi) + _dot(m_bf16, mid) + _dot(m_bf16, lo)


def _exact_dot_r(x_f32, m_bf16):
    hi, mid, lo = _split3(x_f32)
    return _dot(hi, m_bf16) + _dot(mid, m_bf16) + _dot(lo, m_bf16)


def _head_block_ones():
    r = lax.broadcasted_iota(jnp.int32, (LANES, LANES), 0) // DH
    c = lax.broadcasted_iota(jnp.int32, (LANES, LANES), 1) // DH
    return (r == c).astype(BF16)


def _sigmoid(x):
    return 1.0 / (1.0 + jnp.exp(-x))


def _gelu(x):
    c = math.sqrt(2.0 / math.pi)
    return 0.5 * x * (1.0 + jnp.tanh(c * (x + 0.044715 * (x * x * x))))


def _gelu_and_grad(x):
    c = math.sqrt(2.0 / math.pi)
    x2 = x * x
    t = jnp.tanh(c * (x + 0.044715 * (x2 * x)))
    half = 0.5 * (1.0 + t)
    return x * half, half + 0.5 * x * (1.0 - t * t) * c * (1.0 + 3 * 0.044715 * x2)


def _rstd_rows(x):
    return lax.rsqrt(jnp.mean(x * x, axis=-1, keepdims=True) + EPS)


def _norm_mod_matmul(x, ng, sc, sh, w, bias, out_dtype, ts, tn, name, planes=1):
    s, d = x.shape
    ns = w.shape[-1]
    n = w.shape[0] * ns if w.ndim == 3 else ns
    nc = n // planes

    def body(x_ref, ng_ref, sc_ref, sh_ref, w_ref, b_ref, o_ref, h_ref):
        xv = x_ref[...]
        h = (xv * _rstd_rows(xv) * ng_ref[...] * (1.0 + sc_ref[...]) + sh_ref[...]).astype(BF16)
        h_ref[...] = h
        for e in range(planes):
            for c0 in range(0, nc, tn):
                g0 = e * nc + c0
                wv = w_ref[g0 // ns, :, g0 % ns:g0 % ns + tn] if w.ndim == 3 else w_ref[:, g0:g0 + tn]
                val = (_dot(h, wv) + b_ref[:, g0:g0 + tn]).astype(out_dtype)
                if planes == 1:
                    o_ref[:, c0:c0 + tn] = val
                else:
                    o_ref[e, :, c0:c0 + tn] = val

    vec = pl.BlockSpec((1, d), lambda i: (0, 0))
    w_spec = (pl.BlockSpec(w.shape, lambda i: (0, 0, 0)) if w.ndim == 3 else pl.BlockSpec((d, n), lambda i: (0, 0)))
    if planes == 1:
        o_spec, o_shape = pl.BlockSpec((ts, n), lambda i: (i, 0)), (s, n)
    else:
        o_spec, o_shape = pl.BlockSpec((planes, ts, nc), lambda i: (0, i, 0)), (planes, s, nc)
    return pl.pallas_call(
        body, name=name, grid=(s // ts,),
        in_specs=[pl.BlockSpec((ts, d), lambda i: (i, 0)), vec, vec, vec, w_spec,
                  pl.BlockSpec((1, n), lambda i: (0, 0))],
        out_specs=[o_spec, pl.BlockSpec((ts, d), lambda i: (i, 0))],
        out_shape=[jax.ShapeDtypeStruct(o_shape, out_dtype), jax.ShapeDtypeStruct((s, d), BF16)],
        compiler_params=_cparams(("arbitrary",)),
    )(x, ng, sc, sh, w, bias)


def _matmul(a, b, ta, tb, tm, tn, tk, out_dtype, name, out_parts=1):
    if a.ndim == 3:
        m, k = a.shape[1], a.shape[0] * a.shape[2]
        nkp = a.shape[2] // tk
    else:
        m, k = (a.shape[1], a.shape[0]) if ta else a.shape
    if b.ndim == 3:
        n = b.shape[1] if tb else b.shape[0] * b.shape[2]
        nbp = b.shape[2] // (tk if tb else tn)
    else:
        n = b.shape[0] if tb else b.shape[1]
    nk = k // tk
    nop = n // out_parts // tn
    dims = (((0,) if ta else (1,), (1,) if tb else (0,)), ((), ()))

    def body(a_ref, b_ref, o_ref, acc):
        kk = pl.program_id(2)

        @pl.when(kk == 0)
        def _():
            acc[...] = jnp.zeros_like(acc)
        acc[...] += _dot(a_ref[...], b_ref[...], dims)

        @pl.when(kk == nk - 1)
        def _():
            o_ref[...] = acc[...].astype(out_dtype)

    if a.ndim == 3:
        a_spec = pl.BlockSpec((None, tm, tk), lambda i, j, kk: (kk // nkp, i, kk % nkp))
    else:
        a_spec = (pl.BlockSpec((tk, tm), lambda i, j, kk: (kk, i)) if ta
                  else pl.BlockSpec((tm, tk), lambda i, j, kk: (i, kk)))
    if b.ndim == 3 and tb:
        b_spec = pl.BlockSpec((None, tn, tk), lambda i, j, kk: (kk // nbp, j, kk % nbp))
    elif b.ndim == 3:
        b_spec = pl.BlockSpec((None, tk, tn), lambda i, j, kk: (j // nbp, kk, j % nbp))
    else:
        b_spec = (pl.BlockSpec((tn, tk), lambda i, j, kk: (j, kk)) if tb
                  else pl.BlockSpec((tk, tn), lambda i, j, kk: (kk, j)))
    if out_parts > 1:
        o_spec = pl.BlockSpec((None, tm, tn), lambda i, j, kk: (j // nop, i, j % nop))
        o_shape = (out_parts, m, n // out_parts)
    else:
        o_spec, o_shape = pl.BlockSpec((tm, tn), lambda i, j, kk: (i, j)), (m, n)
    return pl.pallas_call(
        body, name=name, grid=(m // tm, n // tn, nk),
        in_specs=[a_spec, b_spec],
        out_specs=o_spec,
        out_shape=jax.ShapeDtypeStruct(o_shape, out_dtype),
        scratch_shapes=[pltpu.VMEM((tm, tn), F32)],
        compiler_params=_cparams(("arbitrary", "arbitrary", "arbitrary")),
    )(a, b)


def _matmul_wt(a, w, tn, tk, out_dtype, ts, name):
    s = a.shape[-2]
    ka, kw = a.shape[-1], w.shape[-1]
    k = ka * (a.shape[0] if a.ndim == 3 else 1)
    n = w.shape[-2]

    def body(a_ref, w_ref, o_ref):
        for n0 in range(0, n, tn):
            acc = None
            for g0 in range(0, k, tk):
                av = a_ref[g0 // ka, :, g0 % ka:g0 % ka + tk] if a.ndim == 3 else a_ref[:, g0:g0 + tk]
                wv = (w_ref[g0 // kw, n0:n0 + tn, g0 % kw:g0 % kw + tk] if w.ndim == 3
                      else w_ref[n0:n0 + tn, g0:g0 + tk])
                part = _dot_nt(av, wv)
                acc = part if acc is None else acc + part
            o_ref[:, n0:n0 + tn] = acc.astype(out_dtype)

    a_spec = (pl.BlockSpec((a.shape[0], ts, ka), lambda i: (0, i, 0)) if a.ndim == 3
              else pl.BlockSpec((ts, ka), lambda i: (i, 0)))
    w_spec = pl.BlockSpec(w.shape, (lambda i: (0, 0, 0)) if w.ndim == 3 else (lambda i: (0, 0)))
    return pl.pallas_call(
        body, name=name, grid=(s // ts,),
        in_specs=[a_spec, w_spec], out_specs=pl.BlockSpec((ts, n), lambda i: (i, 0)),
        out_shape=jax.ShapeDtypeStruct((s, n), out_dtype),
        compiler_params=_cparams(("arbitrary",)),
    )(a, w)


def _matmul_residual(a, w, xin, g, ts, name):
    s, k = a.shape
    d = w.shape[1]

    def body(a_ref, w_ref, x_ref, g_ref, o_ref, y_ref):
        y = _dot(a_ref[...], w_ref[...])
        o_ref[...] = x_ref[...] + g_ref[...] * y
        y_ref[...] = y.astype(BF16)

    return pl.pallas_call(
        body, name=name, grid=(s // ts,),
        in_specs=[pl.BlockSpec((ts, k), lambda i: (i, 0)),
                  pl.BlockSpec((k, d), lambda i: (0, 0)),
                  pl.BlockSpec((ts, d), lambda i: (i, 0)),
                  pl.BlockSpec((1, d), lambda i: (0, 0))],
        out_specs=[pl.BlockSpec((ts, d), lambda i: (i, 0)), pl.BlockSpec((ts, d), lambda i: (i, 0))],
        out_shape=[jax.ShapeDtypeStruct((s, d), F32), jax.ShapeDtypeStruct((s, d), BF16)],
        compiler_params=_cparams(("arbitrary",)),
    )(a, w, xin, g)


def _lane(shape):
    return lax.broadcasted_iota(jnp.int32, shape, 1)


def _pair_norm(x, gain2, bones):
    msq = _exact_dot_r(x * x, bones) * (1.0 / DH)
    r = lax.rsqrt(msq + EPS)
    xh = x * r
    return xh * gain2, xh, r


def _fox_post(proj, qg2, kg2, bf, ts, name):
    s = proj.shape[0]

    def body(p_ref, qg_ref, kg_ref, bf_ref, q_ref, k_ref, v_ref, carry):
        @pl.when(pl.program_id(0) == 0)
        def _():
            carry[...] = jnp.zeros_like(carry)
        lane = _lane((ts, LANES))
        bones = _head_block_ones()
        xf = p_ref[:, 4 * D:4 * D + LANES] + bf_ref[...]
        logf = jnp.minimum(xf, 0.0) - jnp.log(1.0 + jnp.exp(-jnp.abs(xf)))
        logf = jnp.where(lane < H, logf, 0.0)
        rr = lax.broadcasted_iota(jnp.int32, (ts, ts), 0)
        cc = lax.broadcasted_iota(jnp.int32, (ts, ts), 1)
        ltri = (cc <= rr).astype(BF16)
        fcum = _exact_dot(ltri, logf) + carry[0:1, :]
        carry[0:1, :] = fcum[ts - 1:ts, :]
        fhi, fmid, flo = _split3(fcum * LOG2E)
        fhi, fmid, flo = fhi.astype(F32), fmid.astype(F32), flo.astype(F32)
        one_q = ((lane >= L_NF) & (lane < L_NF + 3)).astype(F32)
        one_k = (((lane >= L_F) & (lane < L_F + 3)) | ((lane >= L_LSE) & (lane < L_LSE + 3))).astype(F32)
        one_v = ((lane >= L_F) & (lane < L_F + 3)).astype(F32)
        for p in range(NP):
            qn, _, _ = _pair_norm(p_ref[:, p * LANES:(p + 1) * LANES], qg_ref[...], bones)
            kn, _, _ = _pair_norm(p_ref[:, D + p * LANES:D + (p + 1) * LANES], kg_ref[...], bones)
            vv = p_ref[:, 2 * D + p * LANES:2 * D + (p + 1) * LANES]
            qn = qn * (SCALE * LOG2E)
            for e in range(2):
                h = 2 * p + e
                if e == 1:
                    qe, ke, ve = (pltpu.roll(t, DH, axis=1) for t in (qn, kn, vv))
                else:
                    qe, ke, ve = qn, kn, vv
                f0, f1, f2 = fhi[:, h:h + 1], fmid[:, h:h + 1], flo[:, h:h + 1]
                fq = jnp.where(lane == L_F, f0, jnp.where(lane == L_F + 1, f1, jnp.where(lane == L_F + 2, f2, one_q)))
                fk = jnp.where(lane == L_NF, -f0, jnp.where(lane == L_NF + 1, -f1, jnp.where(lane == L_NF + 2, -f2, one_k)))
                q_ref[h] = jnp.where(lane < DH, qe, fq).astype(BF16)
                k_ref[h] = jnp.where(lane < DH, ke, fk).astype(BF16)
                v_ref[h] = jnp.where(lane < DH, ve, one_v).astype(BF16)

    hs = pl.BlockSpec((H, ts, LANES), lambda i: (0, i, 0))
    vec = pl.BlockSpec((1, LANES), lambda i: (0, 0))
    shp = jax.ShapeDtypeStruct((H, s, LANES), BF16)
    return pl.pallas_call(
        body, name=name, grid=(s // ts,),
        in_specs=[pl.BlockSpec((ts, FOX_NP), lambda i: (i, 0)), vec, vec, vec],
        out_specs=[hs, hs, hs], out_shape=[shp, shp, shp],
        scratch_shapes=[pltpu.VMEM((8, LANES), F32)],
        compiler_params=_cparams(("arbitrary",)),
    )(proj, qg2, kg2, bf)


def _gather_copies(p_refs, o_refs, send_sems, recv_sems):
    x, y, c = _mesh_pos()
    me = 2 * x + y
    sends, arrivals = [], []
    for a, (p_ref, o_ref) in enumerate(zip(p_refs, o_refs)):
        rh = p_ref.shape[0] // 2
        for k, chip in enumerate(_other_chips(x, y)):
            ci = 2 * chip[0] + chip[1]
            for cc in range(2):
                sends.append(_remote(p_ref.at[pl.ds(c * rh, rh), :], o_ref.at[me, pl.ds(c * rh, rh), :],
                                     send_sems.at[6 * a + 2 * k + cc], recv_sems.at[6 * a + 2 * k + c], (*chip, cc)))
                arrivals.append(_remote(o_ref.at[ci, pl.ds(cc * rh, rh), :], o_ref.at[ci, pl.ds(cc * rh, rh), :],
                                        send_sems.at[6 * a + 2 * k + cc], recv_sems.at[6 * a + 2 * k + cc],
                                        (*chip, cc)))
    return sends, arrivals


def _attn_fwd(qa, ka, va, tq, name, shards=()):
    s = qa.shape[1]
    nq = s // tq
    na = len(shards)
    hps = HPS_FWD

    def body(*refs):
        q_ref, k_ref, v_ref = refs[:3]
        p_refs = refs[3:3 + na]
        o_ref, ql_ref = refs[3 + na:5 + na]
        g_refs = refs[5 + na:5 + 2 * na]
        i = pl.program_id(1)
        if na:
            send_sems, recv_sems = refs[5 + 2 * na:]

            @pl.when((pl.program_id(0) == 0) & (i == 0))
            def _():
                for cp in _gather_copies(p_refs, g_refs, send_sems, recv_sems)[0]:
                    cp.start()
        lane = _lane((tq, LANES))
        qs_ = [q_ref[e] for e in range(hps)]

        tk = min(TK_FWD, tq)
        nks = tq // tk

        def step(j, carry, diag=None):
            off = pl.multiple_of(j * tk, tk)
            scs = [_dot_nt(qs_[e], k_ref[e, pl.ds(off, tk), :]) for e in range(hps)]
            probs = []
            for e in range(hps):
                m, sc = carry[e][0], scs[e]
                if diag is not None:
                    rr = lax.broadcasted_iota(jnp.int32, (tq, tk), 0)
                    cc = lax.broadcasted_iota(jnp.int32, (tq, tk), 1) + diag * tk
                    sc = jnp.where(cc <= rr, sc, -jnp.inf)
                m_new = jnp.maximum(m, jnp.max(sc, axis=-1, keepdims=True))
                probs.append((m_new, jnp.exp2(sc - m_new).astype(BF16), jnp.exp2(m - m_new)))
            return tuple((m_new, carry[e][1] * alpha + _dot(pr, v_ref[e, pl.ds(off, tk), :]))
                         for e, (m_new, pr, alpha) in enumerate(probs))

        one = (jnp.full((tq, 1), -jnp.inf, F32), jnp.zeros((tq, LANES), F32))
        carry = lax.fori_loop(0, i * nks, step, (one,) * hps)
        for r in range(nks):
            carry = step(i * nks + r, carry, diag=r)
        outs = []
        for e in range(hps):
            m, acc = carry[e]
            l = acc[:, L_F:L_F + 1]
            outs.append(acc / l)
            lse = m + jnp.log2(l)
            h0, h1, h2 = _split3(-lse)
            ql = jnp.where(lane == L_LSE, h0.astype(F32),
                           jnp.where(lane == L_LSE + 1, h1.astype(F32),
                                     jnp.where(lane == L_LSE + 2, h2.astype(F32), qs_[e].astype(F32))))
            ql_ref[e] = ql.astype(BF16)
        for e in range(0, hps, 2):
            o_ref[:, e * DH:(e + 2) * DH] = jnp.where(lane < DH, outs[e], pltpu.roll(outs[e + 1], DH, axis=1))
        if na:
            @pl.when((pl.program_id(0) == H // hps - 1) & (i == nq - 1))
            def _():
                sends, arrivals = _gather_copies(p_refs, g_refs, send_sems, recv_sems)
                for cp in arrivals:
                    cp.wait_recv()
                for cp in sends:
                    cp.wait_send()

    res = pl.BlockSpec((hps, s, LANES), lambda p, i: (p, 0, 0))
    qs = pl.BlockSpec((hps, tq, LANES), lambda p, i: (p, i, 0))
    outs = pl.pallas_call(
        body, name=name, grid=(H // hps, nq),
        in_specs=[qs, res, res] + [HBM_SPEC] * na,
        out_specs=[pl.BlockSpec((tq, hps * DH), lambda p, i: (i, p)), qs] + [HBM_SPEC] * na,
        out_shape=[jax.ShapeDtypeStruct((s, D), F32), jax.ShapeDtypeStruct((H, s, LANES), BF16)]
        + [jax.ShapeDtypeStruct((N_CHIP,) + p.shape, p.dtype) for p in shards],
        scratch_shapes=[pltpu.SemaphoreType.DMA((6 * na,))] * 2 if na else [],
        compiler_params=_cparams(("arbitrary", "arbitrary")),
    )(qa, ka, va, *shards)
    return outs[0], outs[1], list(outs[2:])


def _chip_exchange_copies(cs_refs, o_refs, send_sems, recv_sems):
    x, y, c = _mesh_pos()
    cps = []
    for a, (cs_ref, o_ref) in enumerate(zip(cs_refs, o_refs)):
        for k, chip in enumerate(_other_chips(x, y)):
            ci = 2 * chip[0] + chip[1]
            cps.append(_remote(cs_ref.at[ci], o_ref.at[k], send_sems.at[3 * a + k], recv_sems.at[3 * a + k],
                               (*chip, c)))
    return cps


def _attn_bwd(ql, ka, va, doa, tq, name, css=()):
    s = ql.shape[1]
    nq = s // tq
    na = len(css)

    def body(*refs):
        q_ref, k_ref, v_ref, do_ref = refs[:4]
        cs_refs = refs[4:4 + na]
        dqo_ref, dk_ref, dv_ref = refs[4 + na:7 + na]
        r_refs = refs[7 + na:7 + 2 * na]
        dq_ref = refs[7 + 2 * na]
        j = pl.program_id(1)
        if na:
            send_sems, recv_sems = refs[8 + 2 * na:]

            @pl.when((pl.program_id(0) == 0) & (j == 0))
            def _():
                for cp in _chip_exchange_copies(cs_refs, r_refs, send_sems, recv_sems):
                    cp.start()

        @pl.when(j == 0)
        def _():
            dq_ref[...] = jnp.zeros_like(dq_ref)
        lane = _lane((tq, LANES))
        kbs = [k_ref[0], k_ref[1]]
        vbs = [v_ref[0], v_ref[1]]

        def step(i, carry, masked):
            ioff = pl.multiple_of(i * tq, tq)
            qbs = [q_ref[e, pl.ds(ioff, tq), :] for e in range(2)]
            dobs = [do_ref[e, pl.ds(ioff, tq), :] for e in range(2)]
            scs = [_dot_nt(qbs[e], kbs[e]) for e in range(2)]
            dps = [_dot_nt(dobs[e], vbs[e]) for e in range(2)]
            prs, dss = [], []
            for e in range(2):
                pr = jnp.exp2(scs[e])
                if masked:
                    rr = lax.broadcasted_iota(jnp.int32, (tq, tq), 0)
                    cc = lax.broadcasted_iota(jnp.int32, (tq, tq), 1)
                    pr = jnp.where(cc <= rr, pr, 0.0)
                dss.append((pr * dps[e]).astype(BF16))
                prs.append(pr.astype(BF16))
            new = []
            for e in range(2):
                dk, dv = carry[e]
                dv = dv + _dot_tn(prs[e], dobs[e])
                dk = dk + _dot_tn(dss[e], qbs[e])
                dq_ref[e, pl.ds(ioff, tq), :] += _dot(dss[e], kbs[e])
                new.append((dk, dv))
            return tuple(new)

        zero = jnp.zeros((tq, LANES), F32)
        carry = step(j, ((zero, zero), (zero, zero)), True)
        carry = lax.fori_loop(j + 1, nq, functools.partial(step, masked=False), carry)
        for e in range(2):
            dk, dv = carry[e]
            col = dk[:, L_NF:L_NF + 1]
            hi = col.astype(BF16).astype(F32)
            dk_ref[e] = jnp.where(lane == L_NF, hi, jnp.where(lane == L_NF + 1, col - hi, dk)).astype(BF16)
            dv_ref[e] = dv.astype(BF16)

        @pl.when(j == nq - 1)
        def _():
            lane_s = _lane((s, LANES))
            for e in range(2):
                dq = dq_ref[e]
                col = dq[:, L_F:L_F + 1]
                hi = col.astype(BF16).astype(F32)
                dqo_ref[e] = jnp.where(lane_s == L_F, hi, jnp.where(lane_s == L_F + 1, col - hi, dq)).astype(BF16)
        if na:
            @pl.when((pl.program_id(0) == NP - 1) & (j == nq - 1))
            def _():
                for cp in _chip_exchange_copies(cs_refs, r_refs, send_sems, recv_sems):
                    cp.wait()

    res = pl.BlockSpec((2, s, LANES), lambda p, j: (p, 0, 0))
    tile = pl.BlockSpec((2, tq, LANES), lambda p, j: (p, j, 0))
    shp = jax.ShapeDtypeStruct((H, s, LANES), BF16)
    outs = pl.pallas_call(
        body, name=name, grid=(NP, nq),
        in_specs=[res, tile, tile, res] + [HBM_SPEC] * na, out_specs=[res, tile, tile] + [HBM_SPEC] * na,
        out_shape=[shp, shp, shp] + [jax.ShapeDtypeStruct((3,) + cs.shape[1:], cs.dtype) for cs in css],
        scratch_shapes=[pltpu.VMEM((2, s, LANES), F32)] + ([pltpu.SemaphoreType.DMA((3 * na,))] * 2 if na else []),
        compiler_params=_cparams(("arbitrary", "arbitrary")),
    )(ql, ka, va, doa, *css)
    return outs[0], outs[1], outs[2], list(outs[3:])


def _gate_out(att, proj, w, xin, g, ts, name):
    s = att.shape[0]

    def body(a_ref, o_ref, w_ref, x_ref, g_ref, xo_ref, y_ref, gt_ref):
        gated = (a_ref[...] * _sigmoid(o_ref[...])).astype(BF16)
        gt_ref[...] = gated
        y = _dot(gated, w_ref[...])
        xo_ref[...] = x_ref[...] + g_ref[...] * y
        y_ref[...] = y.astype(BF16)

    row = pl.BlockSpec((ts, D), lambda i: (i, 0))
    return pl.pallas_call(
        body, name=name, grid=(s // ts,),
        in_specs=[row, pl.BlockSpec((ts, D), lambda i: (i, 3)), pl.BlockSpec((D, D), lambda i: (0, 0)), row,
                  pl.BlockSpec((1, D), lambda i: (0, 0))],
        out_specs=[row, row, row],
        out_shape=[jax.ShapeDtypeStruct((s, D), F32), jax.ShapeDtypeStruct((s, D), BF16),
                   jax.ShapeDtypeStruct((s, D), BF16)],
        compiler_params=_cparams(("arbitrary",)),
    )(att, proj, w, xin, g)


def _attn_bwd_prep(dy, w_out, att, proj, ts, name):
    s = att.shape[0]

    def body(dy_ref, w_ref, a_ref, o_ref, doa_ref, dop_ref):
        lane = _lane((ts, LANES))
        bones = _head_block_ones()
        dgv = _dot_nt(dy_ref[...], w_ref[...])
        for p in range(NP):
            sl = slice(p * LANES, (p + 1) * LANES)
            dg, a = dgv[:, sl], a_ref[:, sl]
            sig = _sigmoid(o_ref[:, sl])
            datt = dg * sig
            dop_ref[:, sl] = (dg * a * sig * (1.0 - sig)).astype(BF16)
            delta = _exact_dot_r(datt * a, bones)
            for e in range(2):
                de, dl = (datt, delta) if e == 0 else (pltpu.roll(datt, DH, axis=1), pltpu.roll(delta, DH, axis=1))
                h0, h1, h2 = _split3(-dl[:, 0:1])
                aug = jnp.where(lane == L_F, h0.astype(F32),
                                jnp.where(lane == L_F + 1, h1.astype(F32),
                                          jnp.where(lane == L_F + 2, h2.astype(F32), 0.0)))
                doa_ref[2 * p + e] = jnp.where(lane < DH, de, aug).astype(BF16)

    row = pl.BlockSpec((ts, D), lambda i: (i, 0))
    return pl.pallas_call(
        body, name=name, grid=(s // ts,),
        in_specs=[row, pl.BlockSpec((D, D), lambda i: (0, 0)), row, pl.BlockSpec((ts, D), lambda i: (i, 3))],
        out_specs=[pl.BlockSpec((H, ts, LANES), lambda i: (0, i, 0)), row],
        out_shape=[jax.ShapeDtypeStruct((H, s, LANES), BF16), jax.ShapeDtypeStruct((s, D), BF16)],
        compiler_params=_cparams(("arbitrary",)),
    )(dy, w_out, att, proj)


def _fox_post_bwd(proj, dqa, dka, dva, dop, qg2, kg2, bf, ts, name):
    s = proj.shape[0]
    nt = s // ts

    def body(p_ref, dq_ref, dk_ref, dv_ref, dop_ref, qg_ref, kg_ref, bf_ref, o_ref, red_ref, carry):
        @pl.when(pl.program_id(0) == 0)
        def _():
            carry[...] = jnp.zeros_like(carry)
            red_ref[...] = jnp.zeros_like(red_ref)
        lane = _lane((ts, LANES))
        bones = _head_block_ones()
        d_f = jnp.zeros((ts, LANES), F32)
        dqg = jnp.zeros((1, LANES), F32)
        dkg = jnp.zeros((1, LANES), F32)
        for p in range(NP):
            heads = [[ref[2 * p + e].astype(F32) for e in range(2)] for ref in (dq_ref, dk_ref, dv_ref)]
            pair = [jnp.where(lane < DH, a, pltpu.roll(b, DH, axis=1)) for a, b in heads]
            for e in range(2):
                dqe, dke = heads[0][e], heads[1][e]
                col = (dqe[:, L_F:L_F + 1] + dqe[:, L_F + 1:L_F + 2]
                       - dke[:, L_NF:L_NF + 1] - dke[:, L_NF + 1:L_NF + 2])
                d_f = jnp.where(lane == 2 * p + e, col, d_f)
            for idx, (g_ref, base) in enumerate(((qg_ref, 0), (kg_ref, D))):
                x = p_ref[:, base + p * LANES:base + (p + 1) * LANES]
                _, xh, r = _pair_norm(x, g_ref[...], bones)
                dn = pair[idx] * (SCALE if idx == 0 else 1.0 / LOG2E)
                t = dn * g_ref[...]
                mean_txh = _exact_dot_r(t * xh, bones) * (1.0 / DH)
                dx = r * (t - xh * mean_txh)
                o_ref[:, base + p * LANES:base + (p + 1) * LANES] = dx.astype(BF16)
                gsum = jnp.sum(dn * xh, axis=0, keepdims=True)
                if idx == 0:
                    dqg = dqg + gsum
                else:
                    dkg = dkg + gsum
            o_ref[:, 2 * D + p * LANES:2 * D + (p + 1) * LANES] = pair[2].astype(BF16)
        o_ref[:, 3 * D:4 * D] = dop_ref[...]
        rr = lax.broadcasted_iota(jnp.int32, (ts, ts), 0)
        cc = lax.broadcasted_iota(jnp.int32, (ts, ts), 1)
        utri = (cc >= rr).astype(BF16)
        dlogf = _exact_dot(utri, d_f) + carry[0:1, :]
        carry[0:1, :] = dlogf[0:1, :]
        xf = p_ref[:, 4 * D:4 * D + LANES] + bf_ref[...]
        dfl = jnp.where(lane < H, dlogf * _sigmoid(-xf), 0.0)
        o_ref[:, 4 * D:4 * D + LANES] = dfl.astype(BF16)
        red_ref[0:1, :] += dqg
        red_ref[1:2, :] += dkg
        red_ref[2:3, :] += jnp.sum(dfl, axis=0, keepdims=True)

    hs = pl.BlockSpec((H, ts, LANES), lambda i: (0, nt - 1 - i, 0))
    vec = pl.BlockSpec((1, LANES), lambda i: (0, 0))
    return pl.pallas_call(
        body, name=name, grid=(nt,),
        in_specs=[pl.BlockSpec((ts, FOX_NP), lambda i: (nt - 1 - i, 0)), hs, hs, hs,
                  pl.BlockSpec((ts, D), lambda i: (nt - 1 - i, 0)), vec, vec, vec],
        out_specs=[pl.BlockSpec((ts, FOX_NP), lambda i: (nt - 1 - i, 0)),
                   pl.BlockSpec((8, LANES), lambda i: (0, 0))],
        out_shape=[jax.ShapeDtypeStruct((s, FOX_NP), BF16), jax.ShapeDtypeStruct((8, LANES), F32)],
        scratch_shapes=[pltpu.VMEM((8, LANES), F32)],
        compiler_params=_cparams(("arbitrary",)),
    )(proj, dqa, dka, dva, dop, qg2, kg2, bf)


HALO = 16
TS = 512
TQ = 512
TR = 256
TP = 256
HPS_FWD = 4
TK_FWD = 512
TKW = 2048


def _shift_down(x, k):
    return pltpu.roll(x, k, axis=0)


def _shift_up(x, k):
    return pltpu.roll(x, x.shape[0] - k, axis=0)


def _planes(ref):
    return jnp.concatenate([ref[0].astype(F32), ref[1].astype(F32)], axis=1)


def _conv_gate(a, cw, cb, ts, name):
    s = a.shape[1]
    hb = ts // HALO

    def body(prev_ref, a_ref, cw_ref, cb_ref, f_ref, ap_ref):
        i = pl.program_id(0)
        cwv, cbv = _planes(cw_ref), _planes(cb_ref)
        prev = jnp.where(i > 0, _planes(prev_ref), 0.0)
        ext = jnp.concatenate([prev, _planes(a_ref)], axis=0)
        ap = (_shift_down(ext, 2) * cwv[0:1, :] + _shift_down(ext, 1) * cwv[1:2, :]
              + ext * cwv[2:3, :] + cbv)[HALO:, :]
        g, val = ap[:, :GT], ap[:, GT:]
        f_ref[...] = (g * _sigmoid(g) * val).astype(BF16)
        ap_ref[0] = g.astype(BF16)
        ap_ref[1] = val.astype(BF16)

    tile = pl.BlockSpec((2, ts, GT), lambda i, j: (0, i, j))
    return pl.pallas_call(
        body, name=name, grid=(s // ts, NGT),
        in_specs=[pl.BlockSpec((2, HALO, GT), lambda i, j: (0, jnp.maximum(i * hb - 1, 0), j)), tile,
                  pl.BlockSpec((2, 8, GT), lambda i, j: (0, 0, j)),
                  pl.BlockSpec((2, 1, GT), lambda i, j: (0, 0, j))],
        out_specs=[pl.BlockSpec((ts, GT), lambda i, j: (i, j)), tile],
        out_shape=[jax.ShapeDtypeStruct((s, DFF), BF16), jax.ShapeDtypeStruct((2, s, DFF), BF16)],
        compiler_params=_cparams(("arbitrary", "arbitrary")),
    )(a, a, cw, cb)


def _conv_down(a, cw, cb, w, xin, gate, ts, name):
    s = a.shape[1]
    d = w.shape[1]
    hb = ts // HALO

    def body(prev_ref, a_ref, cw_ref, cb_ref, w_ref, x_ref, g_ref, o_ref, y_ref, f_ref, ap_ref):
        i = pl.program_id(0)
        acc = None
        for c in range(NGT):
            cols = slice(c * GT, (c + 1) * GT)
            both = lambda ref: jnp.concatenate([ref[0, :, cols].astype(F32), ref[1, :, cols].astype(F32)], axis=1)
            cwv, cbv = both(cw_ref), both(cb_ref)
            ext = jnp.concatenate([jnp.where(i > 0, both(prev_ref), 0.0), both(a_ref)], axis=0)
            ap = (_shift_down(ext, 2) * cwv[0:1, :] + _shift_down(ext, 1) * cwv[1:2, :]
                  + ext * cwv[2:3, :] + cbv)[HALO:, :]
            g, val = ap[:, :GT], ap[:, GT:]
            fch = (g * _sigmoid(g) * val).astype(BF16)
            f_ref[:, cols] = fch
            ap_ref[0, :, cols] = g.astype(BF16)
            ap_ref[1, :, cols] = val.astype(BF16)
            part = _dot(fch, w_ref[cols, :])
            acc = part if acc is None else acc + part
        y_ref[...] = acc.astype(BF16)
        o_ref[...] = x_ref[...] + g_ref[...] * acc

    row = pl.BlockSpec((ts, d), lambda i: (i, 0))
    planes = pl.BlockSpec((2, ts, DFF), lambda i: (0, i, 0))
    return pl.pallas_call(
        body, name=name, grid=(s // ts,),
        in_specs=[pl.BlockSpec((2, HALO, DFF), lambda i: (0, jnp.maximum(i * hb - 1, 0), 0)), planes,
                  pl.BlockSpec((2, 8, DFF), lambda i: (0, 0, 0)), pl.BlockSpec((2, 1, DFF), lambda i: (0, 0, 0)),
                  pl.BlockSpec((DFF, d), lambda i: (0, 0)), row, pl.BlockSpec((1, d), lambda i: (0, 0))],
        out_specs=[row, row, pl.BlockSpec((ts, DFF), lambda i: (i, 0)), planes],
        out_shape=[jax.ShapeDtypeStruct((s, d), F32), jax.ShapeDtypeStruct((s, d), BF16),
                   jax.ShapeDtypeStruct((s, DFF), BF16), jax.ShapeDtypeStruct((2, s, DFF), BF16)],
        compiler_params=_cparams(("arbitrary",)),
    )(a, a, cw, cb, w, xin, gate)


def _down_bwd_conv(dy, w, a, ap, cw, ts, name):
    s, d = dy.shape
    hb = ts // HALO
    nt = s // ts
    nhb = s // HALO

    def body(dy_ref, dyn_ref, w_ref, a_ref, ap_ref, apn_ref, cw_ref, da_ref, red_ref):
        i = pl.program_id(0)

        @pl.when(i == 0)
        def _():
            red_ref[...] = jnp.zeros_like(red_ref)
        dyn = jnp.where(i < nt - 1, dyn_ref[...], jnp.zeros_like(dyn_ref))
        dye = jnp.concatenate([dy_ref[...], dyn], axis=0)
        for c in range(NGT):
            cols = slice(c * GT, (c + 1) * GT)
            both = lambda ref: jnp.concatenate([ref[0, :, cols].astype(F32), ref[1, :, cols].astype(F32)], axis=1)
            cwv = both(cw_ref)
            dfe = _dot_nt(dye, w_ref[cols, :])
            apv = jnp.concatenate([both(ap_ref), both(apn_ref)], axis=0)
            g, val = apv[:, :GT], apv[:, GT:]
            sg = _sigmoid(g)
            dap = jnp.concatenate([dfe * val * (sg * (1.0 + g * (1.0 - sg))), dfe * (g * sg)], axis=1)
            shifted = [_shift_up(dap, 2)[:ts], _shift_up(dap, 1)[:ts], dap[:ts]]
            da = shifted[0] * cwv[0:1, :] + shifted[1] * cwv[1:2, :] + shifted[2] * cwv[2:3, :]
            av = both(a_ref)
            sums = [jnp.sum(av * t, axis=0, keepdims=True) for t in shifted]
            sums.append(jnp.sum(shifted[2], axis=0, keepdims=True))
            for e in range(2):
                half = slice(e * GT, (e + 1) * GT)
                da_ref[e, :, cols] = da[:, half].astype(BF16)
                for r, sm in enumerate(sums):
                    red_ref[e, r:r + 1, cols] += sm[:, half]

    planes = pl.BlockSpec((2, ts, DFF), lambda i: (0, i, 0))
    nxt = lambda i: jnp.minimum((i + 1) * hb, nhb - 1)
    return pl.pallas_call(
        body, name=name, grid=(nt,),
        in_specs=[pl.BlockSpec((ts, d), lambda i: (i, 0)), pl.BlockSpec((HALO, d), lambda i: (nxt(i), 0)),
                  pl.BlockSpec((DFF, d), lambda i: (0, 0)), planes, planes,
                  pl.BlockSpec((2, HALO, DFF), lambda i: (0, nxt(i), 0)),
                  pl.BlockSpec((2, 8, DFF), lambda i: (0, 0, 0))],
        out_specs=[planes, pl.BlockSpec((2, 8, DFF), lambda i: (0, 0, 0))],
        out_shape=[jax.ShapeDtypeStruct((2, s, DFF), BF16), jax.ShapeDtypeStruct((2, 8, DFF), F32)],
        compiler_params=_cparams(("arbitrary",)),
    )(dy, dy, w, a, ap, ap, cw)


def _conv_gate_bwd(a, ap, df, cw, ts, name):
    s = a.shape[1]
    hb = ts // HALO
    nt = s // ts

    def body(a_ref, ap_ref, apn_ref, df_ref, dfn_ref, cw_ref, da_ref, red_ref):
        i = pl.program_id(1)

        @pl.when(i == 0)
        def _():
            red_ref[...] = jnp.zeros_like(red_ref)
        cwv = _planes(cw_ref)
        apv = jnp.concatenate([_planes(ap_ref), _planes(apn_ref)], axis=0)
        dfn = jnp.where(i < nt - 1, dfn_ref[...].astype(F32), 0.0)
        dfe = jnp.concatenate([df_ref[...].astype(F32), dfn], axis=0)
        g, val = apv[:, :GT], apv[:, GT:]
        sg = _sigmoid(g)
        dap = jnp.concatenate([dfe * val * (sg * (1.0 + g * (1.0 - sg))), dfe * (g * sg)], axis=1)
        shifted = [_shift_up(dap, 2)[:ts], _shift_up(dap, 1)[:ts], dap[:ts]]
        da = shifted[0] * cwv[0:1, :] + shifted[1] * cwv[1:2, :] + shifted[2] * cwv[2:3, :]
        av = _planes(a_ref)
        sums = [jnp.sum(av * t, axis=0, keepdims=True) for t in shifted]
        sums.append(jnp.sum(shifted[2], axis=0, keepdims=True))
        for e in range(2):
            cols = slice(e * GT, (e + 1) * GT)
            da_ref[e] = da[:, cols].astype(BF16)
            for r, sm in enumerate(sums):
                red_ref[e, r:r + 1, :] += sm[:, cols]

    nhb = s // HALO
    tile = pl.BlockSpec((2, ts, GT), lambda j, i: (0, i, j))
    return pl.pallas_call(
        body, name=name, grid=(NGT, nt),
        in_specs=[tile, tile,
                  pl.BlockSpec((2, HALO, GT), lambda j, i: (0, jnp.minimum((i + 1) * hb, nhb - 1), j)),
                  pl.BlockSpec((ts, GT), lambda j, i: (i, j)),
                  pl.BlockSpec((HALO, GT), lambda j, i: (jnp.minimum((i + 1) * hb, nhb - 1), j)),
                  pl.BlockSpec((2, 8, GT), lambda j, i: (0, 0, j))],
        out_specs=[tile, pl.BlockSpec((2, 8, GT), lambda j, i: (0, 0, j))],
        out_shape=[jax.ShapeDtypeStruct((2, s, DFF), BF16), jax.ShapeDtypeStruct((2, 8, DFF), F32)],
        compiler_params=_cparams(("arbitrary", "arbitrary")),
    )(a, ap, ap, df, df, cw)


def _chunk_mask(transposed=False):
    t = lax.broadcasted_iota(jnp.int32, (SGB, SGB), 0) // CHUNK
    u = lax.broadcasted_iota(jnp.int32, (SGB, SGB), 1) // CHUNK
    return (t <= u) if transposed else (u <= t)


def _sgu_ln(v, gain, bias):
    mu = jnp.mean(v, axis=-1, keepdims=True)
    vc = v - mu
    rstd = lax.rsqrt(jnp.mean(vc * vc, axis=-1, keepdims=True) + EPS)
    vhat = vc * rstd
    return vhat * gain + bias, vhat, rstd


def _sgu_fwd(z, vgain, vbias, ws, bst, w_out, xin, gate, tr, name):
    s = z.shape[0]

    def body(zu_ref, zv_ref, vg_ref, vb_ref, ws_ref, bs_ref, wo_ref, x_ref, gt_ref, xo_ref, yo_ref, y_ref):
        u = _gelu(zu_ref[...].astype(F32))
        vn, _, _ = _sgu_ln(_gelu(zv_ref[...].astype(F32)), vg_ref[...], vb_ref[...])
        vn = vn.astype(BF16)
        mask = _chunk_mask()
        for g in range(SGG):
            w = jnp.where(mask, ws_ref[g], 0.0).astype(BF16)
            for b in range(tr // SGB):
                rs, cs = slice(b * SGB, (b + 1) * SGB), slice(g * SGC, (g + 1) * SGC)
                mixed = _dot(w, vn[rs, cs]) + bs_ref[:, g:g + 1]
                y_ref[rs, cs] = (u[rs, cs] * mixed).astype(BF16)
        yo = _dot(y_ref[...], wo_ref[...])
        xo_ref[...] = x_ref[...] + gt_ref[...] * yo
        yo_ref[...] = yo.astype(BF16)

    vec = pl.BlockSpec((1, SGW), lambda i: (0, 0))
    row = pl.BlockSpec((tr, D), lambda i: (i, 0))
    return pl.pallas_call(
        body, name=name, grid=(s // tr,),
        in_specs=[pl.BlockSpec((tr, SGW), lambda i: (i, 0)), pl.BlockSpec((tr, SGW), lambda i: (i, 1)),
                  vec, vec, pl.BlockSpec((SGG, SGB, SGB), lambda i: (0, 0, 0)),
                  pl.BlockSpec((SGB, LANES), lambda i: (0, 0)), pl.BlockSpec((SGW, D), lambda i: (0, 0)), row,
                  pl.BlockSpec((1, D), lambda i: (0, 0))],
        out_specs=[row, row, pl.BlockSpec((tr, SGW), lambda i: (i, 0))],
        out_shape=[jax.ShapeDtypeStruct((s, D), F32), jax.ShapeDtypeStruct((s, D), BF16),
                   jax.ShapeDtypeStruct((s, SGW), BF16)],
        compiler_params=_cparams(("arbitrary",)),
    )(z, z, vgain, vbias, ws, bst, w_out, xin, gate)


def _sgu_bwd(z, dy, vgain, vbias, ws, wst, bst, tr, name):
    s = z.shape[0]

    def body(zu_ref, zv_ref, dy_ref, vg_ref, vb_ref, ws_ref, wst_ref, bs_ref,
             dz_ref, rb_ref, rv_ref, dws_ref, dbs_ref, dvn_s):
        @pl.when(pl.program_id(0) == 0)
        def _():
            rb_ref[...] = jnp.zeros_like(rb_ref)
            rv_ref[...] = jnp.zeros_like(rv_ref)
            dws_ref[...] = jnp.zeros_like(dws_ref)
            dbs_ref[...] = jnp.zeros_like(dbs_ref)
        zu = zu_ref[...].astype(F32)
        zv = zv_ref[...].astype(F32)
        u, gu = _gelu_and_grad(zu)
        v, gv = _gelu_and_grad(zv)
        vn, vhat, rstd = _sgu_ln(v, vg_ref[...], vb_ref[...])
        vnb = vn.astype(BF16)
        dyv = dy_ref[...].astype(F32)
        dmix = (dyv * u).astype(BF16)
        mask = _chunk_mask()
        mask_t = _chunk_mask(transposed=True)
        lane = _lane((SGB, LANES))
        dbs = jnp.zeros((SGB, LANES), F32)
        for g in range(SGG):
            w = jnp.where(mask, ws_ref[g], 0.0).astype(BF16)
            wt = jnp.where(mask_t, wst_ref[g], 0.0).astype(BF16)
            dw = jnp.zeros((SGB, SGB), F32)
            for b in range(tr // SGB):
                rs, cs = slice(b * SGB, (b + 1) * SGB), slice(g * SGC, (g + 1) * SGC)
                mixed = _dot(w, vnb[rs, cs]) + bs_ref[:, g:g + 1]
                dz_ref[rs, cs] = (dyv[rs, cs] * mixed * gu[rs, cs]).astype(BF16)
                dm = dmix[rs, cs]
                dw = dw + _dot_nt(dm, vnb[rs, cs])
                dbs = dbs + jnp.where(lane == g, jnp.sum(dm.astype(F32), axis=-1, keepdims=True), 0.0)
                dvn_s[rs, cs] = _dot(wt, dm)
            dws_ref[g] += jnp.where(mask, dw, 0.0)
        dbs_ref[...] += dbs
        dvn = dvn_s[...]
        rv_ref[0:1, :] += jnp.sum(dvn * vhat, axis=0, keepdims=True)
        rv_ref[1:2, :] += jnp.sum(dvn, axis=0, keepdims=True)
        dvh = dvn * vg_ref[...]
        dv = rstd * (dvh - jnp.mean(dvh, axis=-1, keepdims=True)
                     - vhat * jnp.mean(dvh * vhat, axis=-1, keepdims=True))
        dz_ref[:, SGW:] = (dv * gv).astype(BF16)
        dzf = dz_ref[...].astype(F32)
        rb_ref[0:1, :] += jnp.sum(dzf, axis=0, keepdims=True)

    vec = pl.BlockSpec((1, SGW), lambda i: (0, 0))
    wsp = pl.BlockSpec((SGG, SGB, SGB), lambda i: (0, 0, 0))
    return pl.pallas_call(
        body, name=name, grid=(s // tr,),
        in_specs=[pl.BlockSpec((tr, SGW), lambda i: (i, 0)), pl.BlockSpec((tr, SGW), lambda i: (i, 1)),
                  pl.BlockSpec((tr, SGW), lambda i: (i, 0)), vec, vec, wsp, wsp,
                  pl.BlockSpec((SGB, LANES), lambda i: (0, 0))],
        out_specs=[pl.BlockSpec((tr, 2 * SGW), lambda i: (i, 0)),
                   pl.BlockSpec((8, 2 * SGW), lambda i: (0, 0)),
                   pl.BlockSpec((8, SGW), lambda i: (0, 0)), wsp,
                   pl.BlockSpec((SGB, LANES), lambda i: (0, 0))],
        out_shape=[jax.ShapeDtypeStruct((s, 2 * SGW), BF16), jax.ShapeDtypeStruct((8, 2 * SGW), F32),
                   jax.ShapeDtypeStruct((8, SGW), F32), jax.ShapeDtypeStruct((SGG, SGB, SGB), F32),
                   jax.ShapeDtypeStruct((SGB, LANES), F32)],
        scratch_shapes=[pltpu.VMEM((tr, SGW), F32)],
        compiler_params=_cparams(("arbitrary",)),
    )(z, z, dy, vgain, vbias, ws, wst, bst)


def _final_loss(x, fg, tgt, gprev, yprev, ts, name):
    s, d = x.shape

    def body(x_ref, fg_ref, t_ref, g_ref, y_ref, l_ref, dx_ref, dy_ref, red_ref):
        @pl.when(pl.program_id(0) == 0)
        def _():
            l_ref[...] = jnp.zeros_like(l_ref)
            red_ref[...] = jnp.zeros_like(red_ref)
        xv = x_ref[...]
        r = _rstd_rows(xv)
        xh = xv * r
        err = xh * fg_ref[...] - t_ref[...]
        l_ref[...] += 0.5 * jnp.sum(jnp.mean(err * err, axis=-1, keepdims=True))
        dyo = err * (1.0 / d)
        dxh = dyo * fg_ref[...]
        dx = r * (dxh - xh * jnp.mean(dxh * xh, axis=-1, keepdims=True))
        dx_ref[...] = dx
        dy_ref[...] = (dx * g_ref[...]).astype(BF16)
        red_ref[0:1, :] += jnp.sum(dyo * xh, axis=0, keepdims=True)
        red_ref[1:2, :] += jnp.sum(dx * y_ref[...].astype(F32), axis=0, keepdims=True)

    row = pl.BlockSpec((ts, d), lambda i: (i, 0))
    vec = pl.BlockSpec((1, d), lambda i: (0, 0))
    return pl.pallas_call(
        body, name=name, grid=(s // ts,),
        in_specs=[row, vec, row, vec, row],
        out_specs=[pl.BlockSpec((8, LANES), lambda i: (0, 0)), row, row, pl.BlockSpec((8, d), lambda i: (0, 0))],
        out_shape=[jax.ShapeDtypeStruct((8, LANES), F32), jax.ShapeDtypeStruct((s, d), F32),
                   jax.ShapeDtypeStruct((s, d), BF16), jax.ShapeDtypeStruct((8, d), F32)],
        compiler_params=_cparams(("arbitrary",)),
    )(x, fg, tgt, gprev, yprev)


def _norm_bwd(xin, dh, dxout, ng, sc, gprev, yprev, ts, name):
    s, d = xin.shape
    has_prev = gprev is not None
    fused = isinstance(dh, tuple)
    if fused:
        a, w, tk = dh
        ka, kw = a.shape[-1], w.shape[-1]
        k = ka * (a.shape[0] if a.ndim == 3 else 1)

    def body(*refs):
        if fused:
            x_ref, a_ref, w_ref, dxo_ref, ng_ref, sc_ref = refs[:6]
            rest = refs[6:]
        else:
            x_ref, dh_ref, dxo_ref, ng_ref, sc_ref = refs[:5]
            rest = refs[5:]
        if has_prev:
            g_ref, y_ref, dx_ref, dy_ref, red_ref = rest
        else:
            dx_ref, red_ref = rest

        @pl.when(pl.program_id(0) == 0)
        def _():
            red_ref[...] = jnp.zeros_like(red_ref)
        if fused:
            dhv = None
            for g0 in range(0, k, tk):
                av = a_ref[g0 // ka, :, g0 % ka:g0 % ka + tk] if a.ndim == 3 else a_ref[:, g0:g0 + tk]
                wv = w_ref[g0 // kw, :, g0 % kw:g0 % kw + tk] if w.ndim == 3 else w_ref[:, g0:g0 + tk]
                part = _dot_nt(av, wv)
                dhv = part if dhv is None else dhv + part
        else:
            dhv = dh_ref[...]
        xv = x_ref[...]
        r = _rstd_rows(xv)
        xh = xv * r
        dr = dhv * (1.0 + sc_ref[...])
        t = dr * ng_ref[...]
        dx = dxo_ref[...] + r * (t - xh * jnp.mean(t * xh, axis=-1, keepdims=True))
        dx_ref[...] = dx
        red_ref[0:1, :] += jnp.sum(dhv, axis=0, keepdims=True)
        red_ref[1:2, :] += jnp.sum(dhv * (xh * ng_ref[...]), axis=0, keepdims=True)
        red_ref[2:3, :] += jnp.sum(dr * xh, axis=0, keepdims=True)
        if has_prev:
            dy_ref[...] = (dx * g_ref[...]).astype(BF16)
            red_ref[3:4, :] += jnp.sum(dx * y_ref[...].astype(F32), axis=0, keepdims=True)

    row = pl.BlockSpec((ts, d), lambda i: (i, 0))
    vec = pl.BlockSpec((1, d), lambda i: (0, 0))
    red = pl.BlockSpec((8, d), lambda i: (0, 0))
    if fused:
        a_spec = (pl.BlockSpec((a.shape[0], ts, ka), lambda i: (0, i, 0)) if a.ndim == 3
                  else pl.BlockSpec((ts, ka), lambda i: (i, 0)))
        w_spec = pl.BlockSpec(w.shape, (lambda i: (0, 0, 0)) if w.ndim == 3 else (lambda i: (0, 0)))
        dh_specs, dh_args = [a_spec, w_spec], (a, w)
    else:
        dh_specs, dh_args = [row], (dh,)
    if has_prev:
        in_specs, args = [row] + dh_specs + [row, vec, vec, vec, row], (xin,) + dh_args + (dxout, ng, sc, gprev, yprev)
        out_specs = [row, row, red]
        out_shape = [jax.ShapeDtypeStruct((s, d), F32), jax.ShapeDtypeStruct((s, d), BF16),
                     jax.ShapeDtypeStruct((8, d), F32)]
    else:
        in_specs, args = [row] + dh_specs + [row, vec, vec], (xin,) + dh_args + (dxout, ng, sc)
        out_specs = [row, red]
        out_shape = [jax.ShapeDtypeStruct((s, d), F32), jax.ShapeDtypeStruct((8, d), F32)]
    return pl.pallas_call(
        body, name=name, grid=(s // ts,), in_specs=in_specs, out_specs=out_specs, out_shape=out_shape,
        compiler_params=_cparams(("arbitrary",)),
    )(*args)


def _ada_mod(c_all, ada_w, ada_b):
    nb = c_all.shape[0]
    da = ada_w.shape[2]

    def body(c_ref, w_ref, b_ref, o_ref, ca_ref):
        cv = c_ref[...]
        ca = cv * _sigmoid(cv)
        ca_ref[...] = ca
        o_ref[0] = lax.dot_general(ca, w_ref[0], (((1,), (0,)), ((), ())), precision=lax.Precision.HIGHEST,
                                   preferred_element_type=F32) + b_ref[0]

    return pl.pallas_call(
        body, name="ada_mod", grid=(2,),
        in_specs=[pl.BlockSpec((nb, D), lambda i: (0, 0)), pl.BlockSpec((1, D, da), lambda i: (i, 0, 0)),
                  pl.BlockSpec((1, 1, da), lambda i: (i, 0, 0))],
        out_specs=[pl.BlockSpec((1, nb, da), lambda i: (i, 0, 0)), pl.BlockSpec((nb, D), lambda i: (0, 0))],
        out_shape=[jax.ShapeDtypeStruct((2, nb, da), F32), jax.ShapeDtypeStruct((nb, D), F32)],
        compiler_params=_cparams(("arbitrary",)),
    )(c_all, ada_w, ada_b)


def _ada_w_grad(c_act_t, dmod):
    nb = c_act_t.shape[1]
    da = dmod.shape[2]
    tn = 512

    def body(c_ref, d_ref, o_ref):
        acc = c_ref[:, 0:1] * d_ref[0, 0:1, :]
        for b in range(1, nb):
            acc = acc + c_ref[:, b:b + 1] * d_ref[0, b:b + 1, :]
        o_ref[0] = acc

    return pl.pallas_call(
        body, name="ada_w_grad", grid=(2, da // tn),
        in_specs=[pl.BlockSpec((D, nb), lambda i, j: (0, 0)), pl.BlockSpec((1, nb, tn), lambda i, j: (i, 0, j))],
        out_specs=pl.BlockSpec((1, D, tn), lambda i, j: (i, 0, j)),
        out_shape=jax.ShapeDtypeStruct((2, D, da), F32),
        compiler_params=_cparams(("arbitrary", "arbitrary")),
    )(c_act_t, dmod)


def _conv_planes(cw, cb):
    cwp = jnp.swapaxes(cw.reshape(3, 2, DFF), 0, 1)
    return jnp.pad(cwp, ((0, 0), (0, 5), (0, 0))), cb.reshape(2, 1, DFF)


def _local_step(x, tgt, mod, wts, small, comm=None):
    wts = dict(wts)
    s = x.shape[0]
    ts, tq, tr, tp = TS, TQ, TR, TP
    tkw = min(TKW, s)
    tf = min(256, s)
    zb = lambda n: jnp.zeros((1, n), F32)
    m6 = mod.reshape(2, 6, 1, D)
    sh1, sc1, g1, sh2, sc2, g2 = ([m6[i, k] for i in range(2)] for k in range(6))
    n1g, n2g = small["norm1_g"], small["norm2_g"]
    row = lambda a, i: a[i:i + 1]

    qg2 = jnp.tile(small["fox_q_gain"], (1, 2))
    kg2 = jnp.tile(small["fox_k_gain"], (1, 2))
    bfp = jnp.pad(small["fox_b_f"], ((0, 0), (0, LANES - H)))
    proj, h1 = _norm_mod_matmul(x, row(n1g, 0), sc1[0], sh1[0], wts["fox_w_in"], zb(FOX_NP), F32, ts, 1408, "fox_in")
    qa, ka, va = _fox_post(proj, qg2, kg2, bfp, tp, "fox_post")
    att, ql, gathered = _attn_fwd(qa, ka, va, tq, "attn_fwd", shards=comm["shards"] if comm else ())
    if comm:
        wts.update(comm["make_wts"](gathered))
    x1, y0, gated = _gate_out(att, proj, wts["fox_w_out"], x, g1[0], ts, "fox_gate_out")

    def ffn_fwd(xin, i, tag):
        cw, cb = _conv_planes(small["ffn_conv_w"][i], small["ffn_conv_b"][i])
        a, h = _norm_mod_matmul(xin, row(n2g, i), sc2[i], sh2[i], wts["ffn_w_up"][i], zb(2 * DFF), BF16, ts, 1408,
                                "ffn_up" + tag, planes=2)
        xo, y, f, ap = _conv_down(a, cw, cb, wts["ffn_w_down"][i], xin, g2[i], min(256, s), "ffn_conv_down" + tag)
        return xo, (a, h, f, y, cw, ap)

    x2, ffn0 = ffn_fwd(x1, 0, "0")

    bst = jnp.pad(small["sgu_b_s"].T, ((0, 0), (0, LANES - SGG)))
    ws = small["sgu_w_s"]
    z, h3 = _norm_mod_matmul(x2, row(n1g, 1), sc1[1], sh1[1], wts["sgu_w_in"], small["sgu_b_in"], BF16, ts, 1024,
                             "sgu_in")
    x3, y1, yy = _sgu_fwd(z, small["sgu_v_gain"], small["sgu_v_bias"], ws, bst, wts["sgu_w_out"], x2, g1[1], tr,
                          "sgu_mix_out")
    x4, ffn1 = ffn_fwd(x3, 1, "1")

    lsum, dx4, dy, redf = _final_loss(x4, small["final_g"], tgt, g2[1], ffn1[3], ts, "final_loss")
    grads = {"final_g": redf[0]}
    dmod = [[None] * 6, [None] * 6]
    dmod[1][5] = redf[1]

    def ffn_bwd(dxo, dy2, xin, i, saved, gprev, yprev, tag):
        a, h, f, _, cw, ap = saved
        wd, wu = wts["ffn_w_down"][i], wts["ffn_w_up"][i]
        g_wd = _matmul(f, dy2, True, False, 1408, D, tkw, BF16, "ffn_dwdown" + tag)
        da, redc = _down_bwd_conv(dy2, wd, a, ap, cw, min(256, s), "ffn_down_bwd_conv" + tag)
        g_wu = _matmul(h, da, True, False, D, 1408, tkw, BF16, "ffn_dwup" + tag, out_parts=N_CHIP)
        outs = _norm_bwd(xin, (da, wu, 1408), dxo, row(n2g, i), sc2[i], gprev, yprev, tf, "ffn_dh_norm_bwd" + tag)
        return outs, g_wd, g_wu, redc

    (dx3, dy1, red), g_wd1, g_wu1, redc1 = ffn_bwd(dx4, dy, x3, 1, ffn1, g1[1], y1, "1")
    dmod[1][3], dmod[1][4], dn2g1, dmod[1][2] = red[0], red[1], red[2], red[3]

    g_swo = _matmul(yy, dy1, True, False, 1024, D, tkw, BF16, "sgu_dwout")
    dyy = _matmul_wt(dy1, wts["sgu_w_out"], 1024, D, BF16, ts, "sgu_dyy")
    wst = jnp.swapaxes(ws, 1, 2)
    dz, rb, rv, dws, dbst = _sgu_bwd(z, dyy, small["sgu_v_gain"], small["sgu_v_bias"], ws, wst, bst, tr, "sgu_mix_bwd")
    g_swi = _matmul(h3, dz, True, False, D, 1024, tkw, BF16, "sgu_dwin", out_parts=N_CHIP)
    dx2, dy2_0, red = _norm_bwd(x2, (dz, wts["sgu_w_in"], 1024), dx3, row(n1g, 1), sc1[1], g2[0], ffn0[3], tf,
                                "sgu_dh_norm_bwd")
    dmod[1][0], dmod[1][1], dn1g1, dmod[0][5] = red[0], red[1], red[2], red[3]

    (dx1, dy0, red), g_wd0, g_wu0, redc0 = ffn_bwd(dx2, dy2_0, x1, 0, ffn0, g1[0], y0, "0")
    dmod[0][3], dmod[0][4], dn2g0, dmod[0][2] = red[0], red[1], red[2], red[3]

    g_fwo = _matmul(gated, dy0, True, False, D, D, tkw, BF16, "fox_dwout")
    doa, dop = _attn_bwd_prep(dy0, wts["fox_w_out"], att, proj, ts, "attn_bwd_prep")
    css = comm["rs_prepare"]([g_fwo, g_swi, g_swo, g_wu0, g_wu1, g_wd0, g_wd1]) if comm else []
    dqa, dka, dva, rcvs = _attn_bwd(ql, ka, va, doa, tq, "attn_bwd", css=css)
    dproj, redx = _fox_post_bwd(proj, dqa, dka, dva, dop, qg2, kg2, bfp, tp, "fox_post_bwd")
    g_fwi = _matmul(h1, dproj, True, False, D, 1408, tkw, BF16, "fox_dwin")
    dx0, red = _norm_bwd(x, (dproj, wts["fox_w_in"], 1408), dx1, row(n1g, 0), sc1[0], None, None, tf,
                         "fox_dh_norm_bwd")
    dmod[0][0], dmod[0][1], dn1g0 = red[0], red[1], red[2]

    grads.update(
        fox_w_in=g_fwi, fox_w_out=g_fwo, sgu_w_in=g_swi, sgu_w_out=g_swo,
        ffn_w_up=[g_wu0, g_wu1], ffn_w_down=[g_wd0, g_wd1],
        fox_q_gain=redx[0, :DH] + redx[0, DH:], fox_k_gain=redx[1, :DH] + redx[1, DH:], fox_b_f=redx[2, :H],
        sgu_b_in=rb[0], sgu_v_gain=rv[0], sgu_v_bias=rv[1], sgu_w_s=dws, sgu_b_s=dbst[:, :SGG].T,
        ffn_conv_w=jnp.stack([jnp.swapaxes(r[:, 0:3], 0, 1).reshape(3, 2 * DFF) for r in (redc0, redc1)]),
        ffn_conv_b=jnp.stack([r[:, 3].reshape(2 * DFF) for r in (redc0, redc1)]),
        norm1_g=jnp.stack([dn1g0, dn1g1]), norm2_g=jnp.stack([dn2g0, dn2g1]),
    )
    dmod_arr = jnp.stack([jnp.concatenate(dmod[0]), jnp.concatenate(dmod[1])])
    return lsum[0, 0], dx0, grads, dmod_arr, (css, rcvs)


N_DEV = 8
N_CHIP = 4
HBM_SPEC = pl.BlockSpec(memory_space=pltpu.HBM)
VMEM_SPEC = pl.BlockSpec(memory_space=pltpu.VMEM)


def _mesh_pos():
    return lax.axis_index("x"), lax.axis_index("y"), lax.axis_index("c")


def _other_chips(x, y):
    return [(1 - x, y), (x, 1 - y), (1 - x, 1 - y)]


def _remote(src, dst, ssem, rsem, dev):
    return pltpu.make_async_remote_copy(src_ref=src, dst_ref=dst, send_sem=ssem, recv_sem=rsem,
                                        device_id=dev, device_id_type=MESH)


def _allgather8(xb, name):
    m_per, n = xb.shape

    def body(x_ref, out_ref, send_sems, recv_sems, local_sem):
        x, y, c = _mesh_pos()
        me, sibling = (x, y, c), (x, y, 1 - c)
        chips = _other_chips(x, y)

        def rows(px, py, pc):
            return out_ref.at[pl.ds((4 * px + 2 * py + pc) * m_per, m_per), :]

        def copy(k, block, to, src=None):
            return _remote(rows(*block) if src is None else src, rows(*block),
                           send_sems.at[k], recv_sems.at[k], to)

        mine = pltpu.make_async_copy(x_ref, rows(*me), local_sem)
        mine.start()
        first = [copy(0, me, sibling, src=x_ref)]
        first += [copy(1 + j, me, (*chip, c), src=x_ref) for j, chip in enumerate(chips)]
        for cp in first:
            cp.start()
        passed = [copy(4 + j, (*chip, c), sibling) for j, chip in enumerate(chips)]
        for j, chip in enumerate(chips):
            copy(1 + j, (*chip, c), me).wait_recv()
            passed[j].start()
        copy(0, sibling, me).wait_recv()
        for j, chip in enumerate(chips):
            copy(4 + j, (*chip, 1 - c), me).wait_recv()
        for cp in first + passed:
            cp.wait_send()
        mine.wait()

    return pl.pallas_call(
        body, name=name,
        out_shape=jax.ShapeDtypeStruct((N_DEV * m_per, n), xb.dtype),
        in_specs=[VMEM_SPEC], out_specs=VMEM_SPEC,
        scratch_shapes=[pltpu.SemaphoreType.DMA((7,)), pltpu.SemaphoreType.DMA((7,)), pltpu.SemaphoreType.DMA],
        compiler_params=pltpu.CompilerParams(vmem_limit_bytes=V7X_VMEM_LIMIT),
    )(xb)


def _gather_shards(shards, name):
    na = len(shards)

    def body(*refs):
        p_refs, o_refs = refs[:na], refs[na:2 * na]
        send_sems, recv_sems, pass_send, pass_recv = refs[2 * na:]
        x, y, c = _mesh_pos()
        me = 2 * x + y
        sibling = (x, y, 1 - c)
        chips = _other_chips(x, y)

        def half(a, ci, hf):
            rh = shards[a].shape[0] // 2
            return o_refs[a].at[ci, pl.ds(hf * rh, rh), :]

        sends = []
        for a in range(na):
            rh = shards[a].shape[0] // 2
            for k, chip in enumerate(chips):
                sends.append(_remote(p_refs[a].at[pl.ds(c * rh, rh), :], half(a, me, c),
                                     send_sems.at[3 * a + k], recv_sems.at[3 * a + k], (*chip, c)))
        for cp in sends:
            cp.start()
        passed = []
        for a in range(na):
            for k, chip in enumerate(chips):
                ci = 2 * chip[0] + chip[1]
                _remote(half(a, ci, c), half(a, ci, c), send_sems.at[3 * a + k], recv_sems.at[3 * a + k],
                        (*chip, c)).wait_recv()
                cp = _remote(half(a, ci, c), half(a, ci, c), pass_send.at[3 * a + k], pass_recv.at[3 * a + k], sibling)
                cp.start()
                passed.append(cp)
        for a in range(na):
            for k, chip in enumerate(chips):
                ci = 2 * chip[0] + chip[1]
                _remote(half(a, ci, 1 - c), half(a, ci, 1 - c), pass_send.at[3 * a + k], pass_recv.at[3 * a + k],
                        sibling).wait_recv()
        for cp in sends + passed:
            cp.wait_send()

    return pl.pallas_call(
        body, name=name,
        out_shape=[jax.ShapeDtypeStruct((N_CHIP,) + p.shape, p.dtype) for p in shards],
        in_specs=[HBM_SPEC] * na, out_specs=[HBM_SPEC] * na,
        scratch_shapes=[pltpu.SemaphoreType.DMA((3 * na,))] * 4,
    )(*shards)


def _rs_to_sibling(gs, name):
    na = len(gs)

    def body(*refs):
        g_refs, o_refs, ssems, rsems = refs[:na], refs[na:2 * na], refs[2 * na], refs[2 * na + 1]
        x, y, c = _mesh_pos()
        cps = []
        for a in range(na):
            rh = gs[a].shape[1] // 2
            cp = _remote(g_refs[a].at[:, pl.ds((1 - c) * rh, rh), :], o_refs[a], ssems.at[a], rsems.at[a],
                         (x, y, 1 - c))
            cp.start()
            cps.append(cp)
        for cp in cps:
            cp.wait()

    return pl.pallas_call(
        body, name=name,
        out_shape=[jax.ShapeDtypeStruct((g.shape[0], g.shape[1] // 2, g.shape[2]), g.dtype) for g in gs],
        in_specs=[HBM_SPEC] * na, out_specs=[HBM_SPEC] * na,
        scratch_shapes=[pltpu.SemaphoreType.DMA((na,)), pltpu.SemaphoreType.DMA((na,))],
    )(*gs)


def _rs_chip_sum(g, sib, c_arr, tr, name):
    nc, r, n = g.shape
    rh = r // 2
    g4 = g.reshape(nc, 2, rh, n)

    def body(c_ref, g_ref, s_ref, o_ref):
        o_ref[...] = (g_ref[0].astype(F32) + s_ref[...].astype(F32)).astype(BF16)

    return pl.pallas_call(
        body, name=name, out_shape=jax.ShapeDtypeStruct((nc, rh, n), BF16),
        grid_spec=pltpu.PrefetchScalarGridSpec(
            num_scalar_prefetch=1, grid=(nc, rh // tr),
            in_specs=[pl.BlockSpec((1, 1, tr, n), lambda j, i, cr: (j, cr[0], i, 0)),
                      pl.BlockSpec((1, tr, n), lambda j, i, cr: (j, i, 0))],
            out_specs=pl.BlockSpec((1, tr, n), lambda j, i, cr: (j, i, 0))),
        compiler_params=_cparams(("arbitrary", "arbitrary")),
    )(c_arr, g4, sib)


def _rs_across_chips(css, name):
    na = len(css)

    def body(*refs):
        cs_refs, o_refs, send_sems, recv_sems = refs[:na], refs[na:2 * na], refs[2 * na], refs[2 * na + 1]
        x, y, c = _mesh_pos()
        cps = []
        for a in range(na):
            for k, chip in enumerate(_other_chips(x, y)):
                ci = 2 * chip[0] + chip[1]
                cp = _remote(cs_refs[a].at[ci], o_refs[a].at[k], send_sems.at[3 * a + k], recv_sems.at[3 * a + k],
                             (*chip, c))
                cp.start()
                cps.append(cp)
        for cp in cps:
            cp.wait()

    return pl.pallas_call(
        body, name=name, out_shape=[jax.ShapeDtypeStruct((3,) + cs.shape[1:], cs.dtype) for cs in css],
        in_specs=[HBM_SPEC] * na, out_specs=[HBM_SPEC] * na,
        scratch_shapes=[pltpu.SemaphoreType.DMA((3 * na,)), pltpu.SemaphoreType.DMA((3 * na,))],
    )(*css)


def _rs_final_sum(cs, rcv, me_arr, tr, name):
    nc, rh, n = cs.shape

    def body(m_ref, c_ref, r_ref, o_ref):
        acc = c_ref[0].astype(F32)
        for k in range(3):
            acc = acc + r_ref[k].astype(F32)
        o_ref[...] = acc

    return pl.pallas_call(
        body, name=name, out_shape=jax.ShapeDtypeStruct((rh, n), F32),
        grid_spec=pltpu.PrefetchScalarGridSpec(
            num_scalar_prefetch=1, grid=(rh // tr,),
            in_specs=[pl.BlockSpec((1, tr, n), lambda i, mr: (mr[0], i, 0)),
                      pl.BlockSpec((3, tr, n), lambda i, mr: (0, i, 0))],
            out_specs=pl.BlockSpec((tr, n), lambda i, mr: (i, 0))),
        compiler_params=_cparams(("arbitrary",)),
    )(me_arr, cs, rcv)


def _rs_swap_halves(halves, name):
    na = len(halves)

    def body(*refs):
        h_refs, o_refs, ssems, rsems = refs[:na], refs[na:2 * na], refs[2 * na], refs[2 * na + 1]
        x, y, c = _mesh_pos()
        cps = []
        for a in range(na):
            cp = _remote(h_refs[a], o_refs[a], ssems.at[a], rsems.at[a], (x, y, 1 - c))
            cp.start()
            cps.append(cp)
        for cp in cps:
            cp.wait()

    return pl.pallas_call(
        body, name=name, out_shape=[jax.ShapeDtypeStruct(h.shape, h.dtype) for h in halves],
        in_specs=[HBM_SPEC] * na, out_specs=[HBM_SPEC] * na,
        scratch_shapes=[pltpu.SemaphoreType.DMA((na,)), pltpu.SemaphoreType.DMA((na,))],
    )(*halves)


def _join_columns(parts, n_out, name):
    p, k, c = parts.shape
    tr = 128

    def body(w_ref, o_ref):
        for j in range(p):
            o_ref[:, j * c:(j + 1) * c] = w_ref[j]
        o_ref[:, p * c:] = jnp.zeros((tr, n_out - p * c), parts.dtype)

    return pl.pallas_call(
        body, name=name, grid=(k // tr,),
        in_specs=[pl.BlockSpec((p, tr, c), lambda i: (0, i, 0))],
        out_specs=pl.BlockSpec((tr, n_out), lambda i: (i, 0)),
        out_shape=jax.ShapeDtypeStruct((k, n_out), parts.dtype),
        compiler_params=_cparams(("arbitrary",)),
    )(parts)


def _split_columns(g, p, c, name):
    k, n = g.shape
    tr = 128

    def body(g_ref, o_ref):
        for j in range(p):
            o_ref[j] = g_ref[:, j * c:(j + 1) * c]

    return pl.pallas_call(
        body, name=name, grid=(k // tr,),
        in_specs=[pl.BlockSpec((tr, n), lambda i: (i, 0))],
        out_specs=pl.BlockSpec((p, tr, c), lambda i: (0, i, 0)),
        out_shape=jax.ShapeDtypeStruct((p, k, c), g.dtype),
        compiler_params=_cparams(("arbitrary",)),
    )(g)


def _sum8(g, name):
    nd, r, n = g.shape

    def body(g_ref, o_ref):
        acc = g_ref[0]
        for k in range(1, nd):
            acc = acc + g_ref[k]
        o_ref[...] = acc

    return pl.pallas_call(
        body, name=name, grid=(r // 8,),
        in_specs=[pl.BlockSpec((nd, 8, n), lambda i: (0, i, 0))],
        out_specs=pl.BlockSpec((8, n), lambda i: (i, 0)),
        out_shape=jax.ShapeDtypeStruct((r, n), F32),
        compiler_params=_cparams(("arbitrary",)),
    )(g)


def _adamw(w, g, m, v, name):
    r, n = w.shape
    tr = 128 if r % 128 == 0 else 8
    bc1 = 1.0 - ADAM_B1 ** ADAM_STEP
    bc2 = 1.0 - ADAM_B2 ** ADAM_STEP

    def body(w_ref, g_ref, m_ref, v_ref, d_ref, mo_ref, vo_ref):
        gv = g_ref[...]
        mn = ADAM_B1 * m_ref[...] + (1.0 - ADAM_B1) * gv
        vn = ADAM_B2 * v_ref[...] + (1.0 - ADAM_B2) * (gv * gv)
        d_ref[...] = -ADAM_LR * ((mn / bc1) / (jnp.sqrt(vn / bc2) + ADAM_EPS) + ADAM_WD * w_ref[...])
        mo_ref[...] = mn
        vo_ref[...] = vn

    blk = pl.BlockSpec((tr, n), lambda i: (i, 0))
    shp = jax.ShapeDtypeStruct((r, n), F32)
    return pl.pallas_call(
        body, name=name, grid=(r // tr,), in_specs=[blk] * 4, out_specs=[blk] * 3, out_shape=[shp] * 3,
        compiler_params=_cparams(("arbitrary",)),
    )(w, g, m, v)


ROW = 1024
PACK_ROWS = 7168
BIG = ("fox_w_in", "fox_w_out", "sgu_w_in", "sgu_w_out", "ffn_w_up", "ffn_w_down")
SMALL_SHARDED = ("sgu_b_in", "sgu_v_gain", "sgu_v_bias", "ffn_conv_w")
SMALL_REPL = ("fox_b_f", "fox_q_gain", "fox_k_gain", "sgu_w_s", "sgu_b_s", "ffn_conv_b", "ada_b",
              "norm1_g", "norm2_g", "final_g")
WEIGHTS = ("fox_w_in", "fox_b_f", "fox_q_gain", "fox_k_gain", "fox_w_out", "sgu_w_in", "sgu_b_in", "sgu_v_gain",
           "sgu_v_bias", "sgu_w_s", "sgu_b_s", "sgu_w_out", "ffn_w_up", "ffn_conv_w", "ffn_conv_b", "ffn_w_down",
           "ada_w", "ada_b", "norm1_g", "norm2_g", "final_g")


def _rows_of(a, mult=1):
    flat = a.reshape(-1)
    rows = -(-flat.shape[0] // ROW)
    rows = -(-rows // mult) * mult
    return jnp.pad(flat, (0, rows * ROW - flat.shape[0])).reshape(rows, ROW)


def _pack(parts, mult, total=None):
    p = jnp.concatenate([_rows_of(a, mult) for a in parts], axis=0)
    if total is not None:
        p = jnp.pad(p, ((0, total - p.shape[0]), (0, 0)))
    return p


def _unpack(pack, shapes, mult):
    out, r0 = [], 0
    for shp in shapes:
        size = int(np.prod(shp))
        rows = -(-(-(-size // ROW)) // mult) * mult
        out.append(pack[r0:r0 + rows].reshape(-1)[:size].reshape(shp))
        r0 += rows
    return out


def _big_shards(t):
    return [t["fox_w_in"][0], t["fox_w_out"][0], t["sgu_w_in"][0], t["sgu_w_out"][0],
            t["ffn_w_up"][0], t["ffn_w_up"][1], t["ffn_w_down"][0], t["ffn_w_down"][1]]


def _row_tile(rows):
    return next(t for t in (512, 352, 256, 128, 64) if rows % t == 0)


def kernel(x, c, fox_w_in, fox_b_f, fox_q_gain, fox_k_gain, fox_w_out, sgu_w_in, sgu_b_in, sgu_v_gain, sgu_v_bias, sgu_w_s, sgu_b_s, sgu_w_out, ffn_w_up, ffn_conv_w, ffn_conv_b, ffn_w_down, ada_w, ada_b, norm1_g, norm2_g, final_g, loss_target, m_fox_w_in, m_fox_b_f, m_fox_q_gain, m_fox_k_gain, m_fox_w_out, m_sgu_w_in, m_sgu_b_in, m_sgu_v_gain, m_sgu_v_bias, m_sgu_w_s, m_sgu_b_s, m_sgu_w_out, m_ffn_w_up, m_ffn_conv_w, m_ffn_conv_b, m_ffn_w_down, m_ada_w, m_ada_b, m_norm1_g, m_norm2_g, m_final_g, v_fox_w_in, v_fox_b_f, v_fox_q_gain, v_fox_k_gain, v_fox_w_out, v_sgu_w_in, v_sgu_b_in, v_sgu_v_gain, v_sgu_v_bias, v_sgu_w_s, v_sgu_b_s, v_sgu_w_out, v_ffn_w_up, v_ffn_conv_w, v_ffn_conv_b, v_ffn_w_down, v_ada_w, v_ada_b, v_norm1_g, v_norm2_g, v_final_g):
    w = dict(fox_w_in=fox_w_in, fox_b_f=fox_b_f, fox_q_gain=fox_q_gain, fox_k_gain=fox_k_gain, fox_w_out=fox_w_out,
             sgu_w_in=sgu_w_in, sgu_b_in=sgu_b_in, sgu_v_gain=sgu_v_gain, sgu_v_bias=sgu_v_bias, sgu_w_s=sgu_w_s,
             sgu_b_s=sgu_b_s, sgu_w_out=sgu_w_out, ffn_w_up=ffn_w_up, ffn_conv_w=ffn_conv_w, ffn_conv_b=ffn_conv_b,
             ffn_w_down=ffn_w_down, ada_w=ada_w, ada_b=ada_b, norm1_g=norm1_g, norm2_g=norm2_g, final_g=final_g)
    mom = dict(fox_w_in=m_fox_w_in, fox_b_f=m_fox_b_f, fox_q_gain=m_fox_q_gain, fox_k_gain=m_fox_k_gain,
               fox_w_out=m_fox_w_out, sgu_w_in=m_sgu_w_in, sgu_b_in=m_sgu_b_in, sgu_v_gain=m_sgu_v_gain,
               sgu_v_bias=m_sgu_v_bias, sgu_w_s=m_sgu_w_s, sgu_b_s=m_sgu_b_s, sgu_w_out=m_sgu_w_out,
               ffn_w_up=m_ffn_w_up, ffn_conv_w=m_ffn_conv_w, ffn_conv_b=m_ffn_conv_b, ffn_w_down=m_ffn_w_down,
               ada_w=m_ada_w, ada_b=m_ada_b, norm1_g=m_norm1_g, norm2_g=m_norm2_g, final_g=m_final_g)
    var = dict(fox_w_in=v_fox_w_in, fox_b_f=v_fox_b_f, fox_q_gain=v_fox_q_gain, fox_k_gain=v_fox_k_gain,
               fox_w_out=v_fox_w_out, sgu_w_in=v_sgu_w_in, sgu_b_in=v_sgu_b_in, sgu_v_gain=v_sgu_v_gain,
               sgu_v_bias=v_sgu_v_bias, sgu_w_s=v_sgu_w_s, sgu_b_s=v_sgu_b_s, sgu_w_out=v_sgu_w_out,
               ffn_w_up=v_ffn_w_up, ffn_conv_w=v_ffn_conv_w, ffn_conv_b=v_ffn_conv_b, ffn_w_down=v_ffn_w_down,
               ada_w=v_ada_w, ada_b=v_ada_b, norm1_g=v_norm1_g, norm2_g=v_norm2_g, final_g=v_final_g)

    ax, ay, ac = _mesh_pos()
    chip = 2 * ax + ay
    dev = 2 * chip + ac

    small_shard_shapes = tuple(w[n].shape for n in SMALL_SHARDED)
    blk = _pack([c] + [w[n] for n in SMALL_SHARDED], 1, 16)
    gat = _allgather8(blk, "gather_small").reshape(N_DEV, 16, ROW)
    c_all = gat[:, 0, :]
    per_chip = [_unpack(gat[2 * j, 1:], small_shard_shapes, 1) for j in range(N_CHIP)]
    full_small = {n: jnp.concatenate([per_chip[j][i] for j in range(N_CHIP)], axis=-1)
                  for i, n in enumerate(SMALL_SHARDED)}

    mine = [a.astype(BF16) for a in _big_shards(w)]
    with_own = lambda gat, own: [lax.dynamic_update_slice(g_, m_[None], (chip, 0, 0)) for g_, m_ in zip(gat, own)]
    fwi, = with_own(_gather_shards(mine[:1], "gather_fox_w_in"), mine[:1])
    fwi_full = _join_columns(fwi, FOX_NP, "join_fox_w_in")
    wts = dict(fox_w_in=fwi_full)

    def make_wts(gathered):
        fwo, swi, swo, up0, up1, dn0, dn1 = with_own(gathered, mine[1:])
        return dict(fox_w_out=fwo.reshape(D, D), sgu_w_in=swi, sgu_w_out=swo.reshape(SGW, D),
                    ffn_w_up=[up0, up1], ffn_w_down=[dn0.reshape(DFF, D), dn1.reshape(DFF, D)])

    c_arr = jnp.reshape(ac, (1,)).astype(jnp.int32)
    me_arr = jnp.reshape(chip, (1,)).astype(jnp.int32)

    def chip_sums(glist, tag):
        sibs = _rs_to_sibling(glist, "rs_sibling" + tag)
        return [_rs_chip_sum(g_, s_, c_arr, _row_tile(s_.shape[1]), "rs_chip_sum%s%d" % (tag, a))
                for a, (g_, s_) in enumerate(zip(glist, sibs))]

    def rs_prepare(gl):
        g_fwo, g_swi, g_swo, g_wu0, g_wu1, g_wd0, g_wd1 = gl
        return chip_sums([g_fwo.reshape(N_CHIP, 256, D), g_swi, g_swo.reshape(N_CHIP, 512, D), g_wu0, g_wu1,
                          g_wd0.reshape(N_CHIP, 704, D), g_wd1.reshape(N_CHIP, 704, D)], "")

    comm = dict(shards=mine[1:], make_wts=make_wts, rs_prepare=rs_prepare)

    da = ada_w.shape[2]
    ada_b_cols = lax.dynamic_slice_in_dim(ada_b, chip * da, da, axis=1)[:, None, :]
    mod_cols, c_act = _ada_mod(c_all, ada_w, ada_b_cols)
    mod_all = _allgather8(mod_cols.reshape(-1, ROW), "gather_mod").reshape(N_DEV, 2, N_DEV, da)
    mod_mine = lax.dynamic_index_in_dim(mod_all[0::2], dev, axis=2, keepdims=False)
    mod = jnp.swapaxes(mod_mine, 0, 1).reshape(2, N_CHIP * da)

    small = dict(norm1_g=norm1_g, norm2_g=norm2_g, final_g=final_g[None], fox_q_gain=fox_q_gain,
                 fox_k_gain=fox_k_gain, fox_b_f=fox_b_f, sgu_b_in=full_small["sgu_b_in"],
                 sgu_v_gain=full_small["sgu_v_gain"], sgu_v_bias=full_small["sgu_v_bias"], sgu_w_s=sgu_w_s[0],
                 sgu_b_s=sgu_b_s[0], ffn_conv_w=full_small["ffn_conv_w"], ffn_conv_b=ffn_conv_b)
    loss_dev, dx, g, dmod, (css, rcvs) = _local_step(x[0], loss_target[0], mod, wts, small, comm)

    g["ada_b"] = dmod
    g["loss"] = loss_dev
    small_names = ("ada_b",) + SMALL_SHARDED + tuple(n for n in SMALL_REPL if n != "ada_b") + ("loss",)
    gs = _pack([g[n] for n in small_names], 1)
    rows_s = -(-gs.shape[0] // 8) * 8
    gs = jnp.pad(gs, ((0, rows_s - gs.shape[0]), (0, 0)))
    gs_all = _allgather8(gs, "gather_small_grads").reshape(N_DEV, rows_s, ROW)
    gsum = _sum8(gs_all, "sum_small_grads")
    full_shapes = {n: w[n].shape for n in SMALL_REPL}
    full_shapes.update({n: w[n].shape[:-1] + (w[n].shape[-1] * N_CHIP,) for n in SMALL_SHARDED})
    full_shapes["loss"] = ()
    gfull = dict(zip(small_names, _unpack(gsum, [full_shapes[n] for n in small_names], 1)))
    grads = {n: gfull[n] for n in SMALL_REPL}
    for n in SMALL_SHARDED:
        width = w[n].shape[-1]
        grads[n] = lax.dynamic_slice_in_dim(gfull[n], chip * width, width, axis=gfull[n].ndim - 1)
    dmod_all = gs_all[:, :12, :].reshape(N_DEV, 2, N_CHIP * da)
    dmod_cols = jnp.swapaxes(lax.dynamic_slice_in_dim(dmod_all, chip * da, da, axis=2), 0, 1)
    grads["ada_w"] = _ada_w_grad(c_act.T, dmod_cols)

    gfi = _split_columns(g["fox_w_in"], N_CHIP, FOX_N // N_CHIP, "split_fox_w_in")
    cs_fox = chip_sums([gfi], "_fox")
    css = cs_fox + list(css)
    rcvs = list(_rs_across_chips(cs_fox, "rs_chips_fox")) + list(rcvs)
    halves = [_rs_final_sum(cs_, r_, me_arr, _row_tile(cs_.shape[1]), "rs_final_sum%d" % a)
              for a, (cs_, r_) in enumerate(zip(css, rcvs))]
    others = _rs_swap_halves(halves, "rs_swap")
    red = [jnp.concatenate([jnp.where(ac == 0, h_, o_), jnp.where(ac == 0, o_, h_)]) for h_, o_ in zip(halves, others)]
    grads.update(fox_w_in=red[0], fox_w_out=red[1], sgu_w_in=red[2], sgu_w_out=red[3],
                 ffn_w_up=jnp.stack([red[4], red[5]]), ffn_w_down=jnp.stack([red[6], red[7]]))

    delta, new_m, new_v = {}, {}, {}
    for n in BIG + ("ada_w",):
        shp = w[n].shape
        two_d = lambda a: a.reshape(-1, shp[-1])
        d_, m_, v_ = _adamw(two_d(w[n]), two_d(grads[n]), two_d(mom[n]), two_d(var[n]), "adamw_" + n)
        delta[n], new_m[n], new_v[n] = d_.reshape(shp), m_.reshape(shp), v_.reshape(shp)
    rest = SMALL_SHARDED + SMALL_REPL
    packs = [_pack([t[n] for n in rest], 1) for t in (w, grads, mom, var)]
    rows_r = -(-packs[0].shape[0] // 8) * 8
    packs = [jnp.pad(p, ((0, rows_r - p.shape[0]), (0, 0))) for p in packs]
    outs = _adamw(*packs, "adamw_small")
    for t, o in zip((delta, new_m, new_v), outs):
        t.update(zip(rest, _unpack(o, [w[n].shape for n in rest], 1)))

    loss = gfull["loss"]
    return (loss, dx[None], *[grads[n].reshape(w[n].shape) for n in WEIGHTS], *[delta[n] for n in WEIGHTS],
            *[new_m[n] for n in WEIGHTS], *[new_v[n] for n in WEIGHTS])
```

```python
import functools
import math

import numpy as np
import jax
import jax.numpy as jnp
from jax import lax
from jax.experimental import pallas as pl
from jax.experimental.pallas import tpu as pltpu

F32 = jnp.float32
BF16 = jnp.bfloat16
MESH = pl.DeviceIdType.MESH

D = 1024
H = 16
DH = 64
NP = H // 2
LANES = 128
DFF = 2816
SGW = 2048
SGG = 8
SGC = 256
SGB = 128
CHUNK = 64
EPS = 1e-6
FOX_N = 4 * D + H
FOX_NP = 4224
GT = 256
NGT = DFF // GT
SCALE = DH ** -0.5
LOG2E = 1.4426950408889634

ADAM_LR = 0.001
ADAM_B1 = 0.9
ADAM_B2 = 0.999
ADAM_EPS = 1e-08
ADAM_WD = 0.01
ADAM_STEP = 10

V7X_VMEM_LIMIT = 56 * 1024 * 1024

L_F = 64
L_NF = 67
L_LSE = 70


def _cparams(sem=None):
    return pltpu.CompilerParams(dimension_semantics=sem, vmem_limit_bytes=V7X_VMEM_LIMIT)


def _split3(x):
    hi = x.astype(BF16)
    r = x - hi.astype(F32)
    mid = r.astype(BF16)
    lo = (r - mid.astype(F32)).astype(BF16)
    return hi, mid, lo


def _dot(a, b, dims=(((1,), (0,)), ((), ()))):
    return lax.dot_general(a, b, dims, preferred_element_type=F32)


def _dot_nt(a, b):
    return _dot(a, b, (((1,), (1,)), ((), ())))


def _dot_tn(a, b):
    return _dot(a, b, (((0,), (0,)), ((), ())))


def _exact_dot(m_bf16, x_f32):
    hi, mid, lo = _split3(x_f32)
    return _dot(m_bf16, hi) + _dot(m_bf16, mid) + _dot(m_bf16, lo)


def _exact_dot_r(x_f32, m_bf16):
    hi, mid, lo = _split3(x_f32)
    return _dot(hi, m_bf16) + _dot(mid, m_bf16) + _dot(lo, m_bf16)


def _head_block_ones():
    r = lax.broadcasted_iota(jnp.int32, (LANES, LANES), 0) // DH
    c = lax.broadcasted_iota(jnp.int32, (LANES, LANES), 1) // DH
    return (r == c).astype(BF16)


def _sigmoid(x):
    return 1.0 / (1.0 + jnp.exp(-x))


def _gelu(x):
    c = math.sqrt(2.0 / math.pi)
    return 0.5 * x * (1.0 + jnp.tanh(c * (x + 0.044715 * (x * x * x))))


def _gelu_and_grad(x):
    c = math.sqrt(2.0 / math.pi)
    x2 = x * x
    t = jnp.tanh(c * (x + 0.044715 * (x2 * x)))
    half = 0.5 * (1.0 + t)
    return x * half, half + 0.5 * x * (1.0 - t * t) * c * (1.0 + 3 * 0.044715 * x2)


def _rstd_rows(x):
    return lax.rsqrt(jnp.mean(x * x, axis=-1, keepdims=True) + EPS)


def _norm_mod_matmul(x, ng, sc, sh, w, bias, out_dtype, ts, tn, name, planes=1):
    s, d = x.shape
    ns = w.shape[-1]
    n = w.shape[0] * ns if w.ndim == 3 else ns
    nc = n // planes

    def body(x_ref, ng_ref, sc_ref, sh_ref, w_ref, b_ref, o_ref, h_ref):
        xv = x_ref[...]
        h = (xv * _rstd_rows(xv) * ng_ref[...] * (1.0 + sc_ref[...]) + sh_ref[...]).astype(BF16)
        h_ref[...] = h
        for e in range(planes):
            for c0 in range(0, nc, tn):
                g0 = e * nc + c0
                wv = w_ref[g0 // ns, :, g0 % ns:g0 % ns + tn] if w.ndim == 3 else w_ref[:, g0:g0 + tn]
                val = (_dot(h, wv) + b_ref[:, g0:g0 + tn]).astype(out_dtype)
                if planes == 1:
                    o_ref[:, c0:c0 + tn] = val
                else:
                    o_ref[e, :, c0:c0 + tn] = val

    vec = pl.BlockSpec((1, d), lambda i: (0, 0))
    w_spec = (pl.BlockSpec(w.shape, lambda i: (0, 0, 0)) if w.ndim == 3 else pl.BlockSpec((d, n), lambda i: (0, 0)))
    if planes == 1:
        o_spec, o_shape = pl.BlockSpec((ts, n), lambda i: (i, 0)), (s, n)
    else:
        o_spec, o_shape = pl.BlockSpec((planes, ts, nc), lambda i: (0, i, 0)), (planes, s, nc)
    return pl.pallas_call(
        body, name=name, grid=(s // ts,),
        in_specs=[pl.BlockSpec((ts, d), lambda i: (i, 0)), vec, vec, vec, w_spec,
                  pl.BlockSpec((1, n), lambda i: (0, 0))],
        out_specs=[o_spec, pl.BlockSpec((ts, d), lambda i: (i, 0))],
        out_shape=[jax.ShapeDtypeStruct(o_shape, out_dtype), jax.ShapeDtypeStruct((s, d), BF16)],
        compiler_params=_cparams(("arbitrary",)),
    )(x, ng, sc, sh, w, bias)


def _matmul(a, b, ta, tb, tm, tn, tk, out_dtype, name, out_parts=1):
    if a.ndim == 3:
        m, k = a.shape[1], a.shape[0] * a.shape[2]
        nkp = a.shape[2] // tk
    else:
        m, k = (a.shape[1], a.shape[0]) if ta else a.shape
    if b.ndim == 3:
        n = b.shape[1] if tb else b.shape[0] * b.shape[2]
        nbp = b.shape[2] // (tk if tb else tn)
    else:
        n = b.shape[0] if tb else b.shape[1]
    nk = k // tk
    nop = n // out_parts // tn
    dims = (((0,) if ta else (1,), (1,) if tb else (0,)), ((), ()))

    def body(a_ref, b_ref, o_ref, acc):
        kk = pl.program_id(2)

        @pl.when(kk == 0)
        def _():
            acc[...] = jnp.zeros_like(acc)
        acc[...] += _dot(a_ref[...], b_ref[...], dims)

        @pl.when(kk == nk - 1)
        def _():
            o_ref[...] = acc[...].astype(out_dtype)

    if a.ndim == 3:
        a_spec = pl.BlockSpec((None, tm, tk), lambda i, j, kk: (kk // nkp, i, kk % nkp))
    else:
        a_spec = (pl.BlockSpec((tk, tm), lambda i, j, kk: (kk, i)) if ta
                  else pl.BlockSpec((tm, tk), lambda i, j, kk: (i, kk)))
    if b.ndim == 3 and tb:
        b_spec = pl.BlockSpec((None, tn, tk), lambda i, j, kk: (kk // nbp, j, kk % nbp))
    elif b.ndim == 3:
        b_spec = pl.BlockSpec((None, tk, tn), lambda i, j, kk: (j // nbp, kk, j % nbp))
    else:
        b_spec = (pl.BlockSpec((tn, tk), lambda i, j, kk: (j, kk)) if tb
                  else pl.BlockSpec((tk, tn), lambda i, j, kk: (kk, j)))
    if out_parts > 1:
        o_spec = pl.BlockSpec((None, tm, tn), lambda i, j, kk: (j // nop, i, j % nop))
        o_shape = (out_parts, m, n // out_parts)
    else:
        o_spec, o_shape = pl.BlockSpec((tm, tn), lambda i, j, kk: (i, j)), (m, n)
    return pl.pallas_call(
        body, name=name, grid=(m // tm, n // tn, nk),
        in_specs=[a_spec, b_spec],
        out_specs=o_spec,
        out_shape=jax.ShapeDtypeStruct(o_shape, out_dtype),
        scratch_shapes=[pltpu.VMEM((tm, tn), F32)],
        compiler_params=_cparams(("arbitrary", "arbitrary", "arbitrary")),
    )(a, b)


def _matmul_wt(a, w, tn, tk, out_dtype, ts, name):
    s = a.shape[-2]
    ka, kw = a.shape[-1], w.shape[-1]
    k = ka * (a.shape[0] if a.ndim == 3 else 1)
    n = w.shape[-2]

    def body(a_ref, w_ref, o_ref):
        for n0 in range(0, n, tn):
            acc = None
            for g0 in range(0, k, tk):
                av = a_ref[g0 // ka, :, g0 % ka:g0 % ka + tk] if a.ndim == 3 else a_ref[:, g0:g0 + tk]
                wv = (w_ref[g0 // kw, n0:n0 + tn, g0 % kw:g0 % kw + tk] if w.ndim == 3
                      else w_ref[n0:n0 + tn, g0:g0 + tk])
                part = _dot_nt(av, wv)
                acc = part if acc is None else acc + part
            o_ref[:, n0:n0 + tn] = acc.astype(out_dtype)

    a_spec = (pl.BlockSpec((a.shape[0], ts, ka), lambda i: (0, i, 0)) if a.ndim == 3
              else pl.BlockSpec((ts, ka), lambda i: (i, 0)))
    w_spec = pl.BlockSpec(w.shape, (lambda i: (0, 0, 0)) if w.ndim == 3 else (lambda i: (0, 0)))
    return pl.pallas_call(
        body, name=name, grid=(s // ts,),
        in_specs=[a_spec, w_spec], out_specs=pl.BlockSpec((ts, n), lambda i: (i, 0)),
        out_shape=jax.ShapeDtypeStruct((s, n), out_dtype),
        compiler_params=_cparams(("arbitrary",)),
    )(a, w)


def _lane(shape):
    return lax.broadcasted_iota(jnp.int32, shape, 1)


def _pair_norm(x, gain2, bones):
    msq = _exact_dot_r(x * x, bones) * (1.0 / DH)
    r = lax.rsqrt(msq + EPS)
    xh = x * r
    return xh * gain2, xh, r


def _fox_post(proj, qg2, kg2, bf, ts, name):
    s = proj.shape[0]

    def body(p_ref, qg_ref, kg_ref, bf_ref, q_ref, k_ref, v_ref, carry):
        @pl.when(pl.program_id(0) == 0)
        def _():
            carry[...] = jnp.zeros_like(carry)
        lane = _lane((ts, LANES))
        bones = _head_block_ones()
        xf = p_ref[:, 4 * D:4 * D + LANES] + bf_ref[...]
        logf = jnp.minimum(xf, 0.0) - jnp.log(1.0 + jnp.exp(-jnp.abs(xf)))
        logf = jnp.where(lane < H, logf, 0.0)
        rr = lax.broadcasted_iota(jnp.int32, (ts, ts), 0)
        cc = lax.broadcasted_iota(jnp.int32, (ts, ts), 1)
        ltri = (cc <= rr).astype(BF16)
        fcum = _exact_dot(ltri, logf) + carry[0:1, :]
        carry[0:1, :] = fcum[ts - 1:ts, :]
        fhi, fmid, flo = _split3(fcum * LOG2E)
        fhi, fmid, flo = fhi.astype(F32), fmid.astype(F32), flo.astype(F32)
        one_q = ((lane >= L_NF) & (lane < L_NF + 3)).astype(F32)
        one_k = (((lane >= L_F) & (lane < L_F + 3)) | ((lane >= L_LSE) & (lane < L_LSE + 3))).astype(F32)
        one_v = ((lane >= L_F) & (lane < L_F + 3)).astype(F32)
        for p in range(NP):
            qn, _, _ = _pair_norm(p_ref[:, p * LANES:(p + 1) * LANES], qg_ref[...], bones)
            kn, _, _ = _pair_norm(p_ref[:, D + p * LANES:D + (p + 1) * LANES], kg_ref[...], bones)
            vv = p_ref[:, 2 * D + p * LANES:2 * D + (p + 1) * LANES]
            qn = qn * (SCALE * LOG2E)
            for e in range(2):
                h = 2 * p + e
                if e == 1:
                    qe, ke, ve = (pltpu.roll(t, DH, axis=1) for t in (qn, kn, vv))
                else:
                    qe, ke, ve = qn, kn, vv
                f0, f1, f2 = fhi[:, h:h + 1], fmid[:, h:h + 1], flo[:, h:h + 1]
                fq = jnp.where(lane == L_F, f0, jnp.where(lane == L_F + 1, f1, jnp.where(lane == L_F + 2, f2, one_q)))
                fk = jnp.where(lane == L_NF, -f0, jnp.where(lane == L_NF + 1, -f1, jnp.where(lane == L_NF + 2, -f2, one_k)))
                q_ref[h] = jnp.where(lane < DH, qe, fq).astype(BF16)
                k_ref[h] = jnp.where(lane < DH, ke, fk).astype(BF16)
                v_ref[h] = jnp.where(lane < DH, ve, one_v).astype(BF16)

    hs = pl.BlockSpec((H, ts, LANES), lambda i: (0, i, 0))
    vec = pl.BlockSpec((1, LANES), lambda i: (0, 0))
    shp = jax.ShapeDtypeStruct((H, s, LANES), BF16)
    return pl.pallas_call(
        body, name=name, grid=(s // ts,),
        in_specs=[pl.BlockSpec((ts, FOX_NP), lambda i: (i, 0)), vec, vec, vec],
        out_specs=[hs, hs, hs], out_shape=[shp, shp, shp],
        scratch_shapes=[pltpu.VMEM((8, LANES), F32)],
        compiler_params=_cparams(("arbitrary",)),
    )(proj, qg2, kg2, bf)


def _gather_copies(p_refs, o_refs, send_sems, recv_sems):
    x, y, c = _mesh_pos()
    me = 2 * x + y
    sends, arrivals = [], []
    for a, (p_ref, o_ref) in enumerate(zip(p_refs, o_refs)):
        rh = p_ref.shape[0] // 2
        for k, chip in enumerate(_other_chips(x, y)):
            ci = 2 * chip[0] + chip[1]
            for cc in range(2):
                sends.append(_remote(p_ref.at[pl.ds(c * rh, rh), :], o_ref.at[me, pl.ds(c * rh, rh), :],
                                     send_sems.at[6 * a + 2 * k + cc], recv_sems.at[6 * a + 2 * k + c], (*chip, cc)))
                arrivals.append(_remote(o_ref.at[ci, pl.ds(cc * rh, rh), :], o_ref.at[ci, pl.ds(cc * rh, rh), :],
                                        send_sems.at[6 * a + 2 * k + cc], recv_sems.at[6 * a + 2 * k + cc],
                                        (*chip, cc)))
    return sends, arrivals


def _attn_fwd(qa, ka, va, tq, name, shards=()):
    s = qa.shape[1]
    nq = s // tq
    na = len(shards)
    hps = HPS_FWD

    def body(*refs):
        q_ref, k_ref, v_ref = refs[:3]
        p_refs = refs[3:3 + na]
        o_ref, ql_ref = refs[3 + na:5 + na]
        g_refs = refs[5 + na:5 + 2 * na]
        i = pl.program_id(1)
        if na:
            send_sems, recv_sems = refs[5 + 2 * na:]

            @pl.when((pl.program_id(0) == 0) & (i == 0))
            def _():
                for cp in _gather_copies(p_refs, g_refs, send_sems, recv_sems)[0]:
                    cp.start()
        lane = _lane((tq, LANES))
        qs_ = [q_ref[e] for e in range(hps)]

        tk = min(TK_FWD, tq)
        nks = tq // tk

        def step(j, carry, diag=None):
            off = pl.multiple_of(j * tk, tk)
            scs = [_dot_nt(qs_[e], k_ref[e, pl.ds(off, tk), :]) for e in range(hps)]
            probs = []
            for e in range(hps):
                m, sc = carry[e][0], scs[e]
                if diag is not None:
                    rr = lax.broadcasted_iota(jnp.int32, (tq, tk), 0)
                    cc = lax.broadcasted_iota(jnp.int32, (tq, tk), 1) + diag * tk
                    sc = jnp.where(cc <= rr, sc, -jnp.inf)
                m_new = jnp.maximum(m, jnp.max(sc, axis=-1, keepdims=True))
                probs.append((m_new, jnp.exp2(sc - m_new).astype(BF16), jnp.exp2(m - m_new)))
            return tuple((m_new, carry[e][1] * alpha + _dot(pr, v_ref[e, pl.ds(off, tk), :]))
                         for e, (m_new, pr, alpha) in enumerate(probs))

        one = (jnp.full((tq, 1), -jnp.inf, F32), jnp.zeros((tq, LANES), F32))
        carry = lax.fori_loop(0, i * nks, step, (one,) * hps)
        for r in range(nks):
            carry = step(i * nks + r, carry, diag=r)
        outs = []
        for e in range(hps):
            m, acc = carry[e]
            l = acc[:, L_F:L_F + 1]
            outs.append(acc / l)
            lse = m + jnp.log2(l)
            h0, h1, h2 = _split3(-lse)
            ql = jnp.where(lane == L_LSE, h0.astype(F32),
                           jnp.where(lane == L_LSE + 1, h1.astype(F32),
                                     jnp.where(lane == L_LSE + 2, h2.astype(F32), qs_[e].astype(F32))))
            ql_ref[e] = ql.astype(BF16)
        for e in range(0, hps, 2):
            o_ref[:, e * DH:(e + 2) * DH] = jnp.where(lane < DH, outs[e], pltpu.roll(outs[e + 1], DH, axis=1))
        if na:
            @pl.when((pl.program_id(0) == H // hps - 1) & (i == nq - 1))
            def _():
                sends, arrivals = _gather_copies(p_refs, g_refs, send_sems, recv_sems)
                for cp in arrivals:
                    cp.wait_recv()
                for cp in sends:
                    cp.wait_send()

    res = pl.BlockSpec((hps, s, LANES), lambda p, i: (p, 0, 0))
    qs = pl.BlockSpec((hps, tq, LANES), lambda p, i: (p, i, 0))
    outs = pl.pallas_call(
        body, name=name, grid=(H // hps, nq),
        in_specs=[qs, res, res] + [HBM_SPEC] * na,
        out_specs=[pl.BlockSpec((tq, hps * DH), lambda p, i: (i, p)), qs] + [HBM_SPEC] * na,
        out_shape=[jax.ShapeDtypeStruct((s, D), F32), jax.ShapeDtypeStruct((H, s, LANES), BF16)]
        + [jax.ShapeDtypeStruct((N_CHIP,) + p.shape, p.dtype) for p in shards],
        scratch_shapes=[pltpu.SemaphoreType.DMA((6 * na,))] * 2 if na else [],
        compiler_params=_cparams(("arbitrary", "arbitrary")),
    )(qa, ka, va, *shards)
    return outs[0], outs[1], list(outs[2:])


def _chip_exchange_copies(cs_refs, o_refs, send_sems, recv_sems):
    x, y, c = _mesh_pos()
    cps = []
    for a, (cs_ref, o_ref) in enumerate(zip(cs_refs, o_refs)):
        for k, chip in enumerate(_other_chips(x, y)):
            ci = 2 * chip[0] + chip[1]
            cps.append(_remote(cs_ref.at[ci], o_ref.at[k], send_sems.at[3 * a + k], recv_sems.at[3 * a + k],
                               (*chip, c)))
    return cps


def _attn_bwd(ql, ka, va, doa, tq, name, css=()):
    s = ql.shape[1]
    nq = s // tq
    na = len(css)

    def body(*refs):
        q_ref, k_ref, v_ref, do_ref = refs[:4]
        cs_refs = refs[4:4 + na]
        dqo_ref, dk_ref, dv_ref = refs[4 + na:7 + na]
        r_refs = refs[7 + na:7 + 2 * na]
        dq_ref = refs[7 + 2 * na]
        j = pl.program_id(1)
        if na:
            send_sems, recv_sems = refs[8 + 2 * na:]

            @pl.when((pl.program_id(0) == 0) & (j == 0))
            def _():
                for cp in _chip_exchange_copies(cs_refs, r_refs, send_sems, recv_sems):
                    cp.start()

        @pl.when(j == 0)
        def _():
            dq_ref[...] = jnp.zeros_like(dq_ref)
        lane = _lane((tq, LANES))
        kbs = [k_ref[0], k_ref[1]]
        vbs = [v_ref[0], v_ref[1]]

        def step(i, carry, masked):
            ioff = pl.multiple_of(i * tq, tq)
            qbs = [q_ref[e, pl.ds(ioff, tq), :] for e in range(2)]
            dobs = [do_ref[e, pl.ds(ioff, tq), :] for e in range(2)]
            scs = [_dot_nt(qbs[e], kbs[e]) for e in range(2)]
            dps = [_dot_nt(dobs[e], vbs[e]) for e in range(2)]
            prs, dss = [], []
            for e in range(2):
                pr = jnp.exp2(scs[e])
                if masked:
                    rr = lax.broadcasted_iota(jnp.int32, (tq, tq), 0)
                    cc = lax.broadcasted_iota(jnp.int32, (tq, tq), 1)
                    pr = jnp.where(cc <= rr, pr, 0.0)
                dss.append((pr * dps[e]).astype(BF16))
                prs.append(pr.astype(BF16))
            new = []
            for e in range(2):
                dk, dv = carry[e]
                dv = dv + _dot_tn(prs[e], dobs[e])
                dk = dk + _dot_tn(dss[e], qbs[e])
                dq_ref[e, pl.ds(ioff, tq), :] += _dot(dss[e], kbs[e])
                new.append((dk, dv))
            return tuple(new)

        zero = jnp.zeros((tq, LANES), F32)
        carry = step(j, ((zero, zero), (zero, zero)), True)
        carry = lax.fori_loop(j + 1, nq, functools.partial(step, masked=False), carry)
        for e in range(2):
            dk, dv = carry[e]
            col = dk[:, L_NF:L_NF + 1]
            hi = col.astype(BF16).astype(F32)
            dk_ref[e] = jnp.where(lane == L_NF, hi, jnp.where(lane == L_NF + 1, col - hi, dk)).astype(BF16)
            dv_ref[e] = dv.astype(BF16)

        @pl.when(j == nq - 1)
        def _():
            lane_s = _lane((s, LANES))
            for e in range(2):
                dq = dq_ref[e]
                col = dq[:, L_F:L_F + 1]
                hi = col.astype(BF16).astype(F32)
                dqo_ref[e] = jnp.where(lane_s == L_F, hi, jnp.where(lane_s == L_F + 1, col - hi, dq)).astype(BF16)
        if na:
            @pl.when((pl.program_id(0) == NP - 1) & (j == nq - 1))
            def _():
                for cp in _chip_exchange_copies(cs_refs, r_refs, send_sems, recv_sems):
                    cp.wait()

    res = pl.BlockSpec((2, s, LANES), lambda p, j: (p, 0, 0))
    tile = pl.BlockSpec((2, tq, LANES), lambda p, j: (p, j, 0))
    shp = jax.ShapeDtypeStruct((H, s, LANES), BF16)
    outs = pl.pallas_call(
        body, name=name, grid=(NP, nq),
        in_specs=[res, tile, tile, res] + [HBM_SPEC] * na, out_specs=[res, tile, tile] + [HBM_SPEC] * na,
        out_shape=[shp, shp, shp] + [jax.ShapeDtypeStruct((3,) + cs.shape[1:], cs.dtype) for cs in css],
        scratch_shapes=[pltpu.VMEM((2, s, LANES), F32)] + ([pltpu.SemaphoreType.DMA((3 * na,))] * 2 if na else []),
        compiler_params=_cparams(("arbitrary", "arbitrary")),
    )(ql, ka, va, doa, *css)
    return outs[0], outs[1], outs[2], list(outs[3:])


def _gate_out(att, proj, w, xin, g, ts, name):
    s = att.shape[0]

    def body(a_ref, o_ref, w_ref, x_ref, g_ref, xo_ref, y_ref, gt_ref):
        gated = (a_ref[...] * _sigmoid(o_ref[...])).astype(BF16)
        gt_ref[...] = gated
        y = _dot(gated, w_ref[...])
        xo_ref[...] = x_ref[...] + g_ref[...] * y
        y_ref[...] = y.astype(BF16)

    row = pl.BlockSpec((ts, D), lambda i: (i, 0))
    return pl.pallas_call(
        body, name=name, grid=(s // ts,),
        in_specs=[row, pl.BlockSpec((ts, D), lambda i: (i, 3)), pl.BlockSpec((D, D), lambda i: (0, 0)), row,
                  pl.BlockSpec((1, D), lambda i: (0, 0))],
        out_specs=[row, row, row],
        out_shape=[jax.ShapeDtypeStruct((s, D), F32), jax.ShapeDtypeStruct((s, D), BF16),
                   jax.ShapeDtypeStruct((s, D), BF16)],
        compiler_params=_cparams(("arbitrary",)),
    )(att, proj, w, xin, g)


def _attn_bwd_prep(dy, w_out, att, proj, ts, name):
    s = att.shape[0]

    def body(dy_ref, w_ref, a_ref, o_ref, doa_ref, dop_ref):
        lane = _lane((ts, LANES))
        bones = _head_block_ones()
        dgv = _dot_nt(dy_ref[...], w_ref[...])
        for p in range(NP):
            sl = slice(p * LANES, (p + 1) * LANES)
            dg, a = dgv[:, sl], a_ref[:, sl]
            sig = _sigmoid(o_ref[:, sl])
            datt = dg * sig
            dop_ref[:, sl] = (dg * a * sig * (1.0 - sig)).astype(BF16)
            delta = _exact_dot_r(datt * a, bones)
            for e in range(2):
                de, dl = (datt, delta) if e == 0 else (pltpu.roll(datt, DH, axis=1), pltpu.roll(delta, DH, axis=1))
                h0, h1, h2 = _split3(-dl[:, 0:1])
                aug = jnp.where(lane == L_F, h0.astype(F32),
                                jnp.where(lane == L_F + 1, h1.astype(F32),
                                          jnp.where(lane == L_F + 2, h2.astype(F32), 0.0)))
                doa_ref[2 * p + e] = jnp.where(lane < DH, de, aug).astype(BF16)

    row = pl.BlockSpec((ts, D), lambda i: (i, 0))
    return pl.pallas_call(
        body, name=name, grid=(s // ts,),
        in_specs=[row, pl.BlockSpec((D, D), lambda i: (0, 0)), row, pl.BlockSpec((ts, D), lambda i: (i, 3))],
        out_specs=[pl.BlockSpec((H, ts, LANES), lambda i: (0, i, 0)), row],
        out_shape=[jax.ShapeDtypeStruct((H, s, LANES), BF16), jax.ShapeDtypeStruct((s, D), BF16)],
        compiler_params=_cparams(("arbitrary",)),
    )(dy, w_out, att, proj)


def _fox_post_bwd(proj, dqa, dka, dva, dop, qg2, kg2, bf, ts, name):
    s = proj.shape[0]
    nt = s // ts

    def body(p_ref, dq_ref, dk_ref, dv_ref, dop_ref, qg_ref, kg_ref, bf_ref, o_ref, red_ref, carry):
        @pl.when(pl.program_id(0) == 0)
        def _():
            carry[...] = jnp.zeros_like(carry)
            red_ref[...] = jnp.zeros_like(red_ref)
        lane = _lane((ts, LANES))
        bones = _head_block_ones()
        d_f = jnp.zeros((ts, LANES), F32)
        dqg = jnp.zeros((1, LANES), F32)
        dkg = jnp.zeros((1, LANES), F32)
        for p in range(NP):
            heads = [[ref[2 * p + e].astype(F32) for e in range(2)] for ref in (dq_ref, dk_ref, dv_ref)]
            pair = [jnp.where(lane < DH, a, pltpu.roll(b, DH, axis=1)) for a, b in heads]
            for e in range(2):
                dqe, dke = heads[0][e], heads[1][e]
                col = (dqe[:, L_F:L_F + 1] + dqe[:, L_F + 1:L_F + 2]
                       - dke[:, L_NF:L_NF + 1] - dke[:, L_NF + 1:L_NF + 2])
                d_f = jnp.where(lane == 2 * p + e, col, d_f)
            for idx, (g_ref, base) in enumerate(((qg_ref, 0), (kg_ref, D))):
                x = p_ref[:, base + p * LANES:base + (p + 1) * LANES]
                _, xh, r = _pair_norm(x, g_ref[...], bones)
                dn = pair[idx] * (SCALE if idx == 0 else 1.0 / LOG2E)
                t = dn * g_ref[...]
                mean_txh = _exact_dot_r(t * xh, bones) * (1.0 / DH)
                dx = r * (t - xh * mean_txh)
                o_ref[:, base + p * LANES:base + (p + 1) * LANES] = dx.astype(BF16)
                gsum = jnp.sum(dn * xh, axis=0, keepdims=True)
                if idx == 0:
                    dqg = dqg + gsum
                else:
                    dkg = dkg + gsum
            o_ref[:, 2 * D + p * LANES:2 * D + (p + 1) * LANES] = pair[2].astype(BF16)
        o_ref[:, 3 * D:4 * D] = dop_ref[...]
        rr = lax.broadcasted_iota(jnp.int32, (ts, ts), 0)
        cc = lax.broadcasted_iota(jnp.int32, (ts, ts), 1)
        utri = (cc >= rr).astype(BF16)
        dlogf = _exact_dot(utri, d_f) + carry[0:1, :]
        carry[0:1, :] = dlogf[0:1, :]
        xf = p_ref[:, 4 * D:4 * D + LANES] + bf_ref[...]
        dfl = jnp.where(lane < H, dlogf * _sigmoid(-xf), 0.0)
        o_ref[:, 4 * D:4 * D + LANES] = dfl.astype(BF16)
        red_ref[0:1, :] += dqg
        red_ref[1:2, :] += dkg
        red_ref[2:3, :] += jnp.sum(dfl, axis=0, keepdims=True)

    hs = pl.BlockSpec((H, ts, LANES), lambda i: (0, nt - 1 - i, 0))
    vec = pl.BlockSpec((1, LANES), lambda i: (0, 0))
    return pl.pallas_call(
        body, name=name, grid=(nt,),
        in_specs=[pl.BlockSpec((ts, FOX_NP), lambda i: (nt - 1 - i, 0)), hs, hs, hs,
                  pl.BlockSpec((ts, D), lambda i: (nt - 1 - i, 0)), vec, vec, vec],
        out_specs=[pl.BlockSpec((ts, FOX_NP), lambda i: (nt - 1 - i, 0)),
                   pl.BlockSpec((8, LANES), lambda i: (0, 0))],
        out_shape=[jax.ShapeDtypeStruct((s, FOX_NP), BF16), jax.ShapeDtypeStruct((8, LANES), F32)],
        scratch_shapes=[pltpu.VMEM((8, LANES), F32)],
        compiler_params=_cparams(("arbitrary",)),
    )(proj, dqa, dka, dva, dop, qg2, kg2, bf)


HALO = 16
TS = 512
TQ = 512
TR = 256
TP = 256
HPS_FWD = 4
TK_FWD = 512
TKW = 2048


def _shift_down(x, k):
    return pltpu.roll(x, k, axis=0)


def _shift_up(x, k):
    return pltpu.roll(x, x.shape[0] - k, axis=0)


def _conv_down(a, cw, cb, w, xin, gate, ts, name):
    s = a.shape[1]
    d = w.shape[1]
    hb = ts // HALO

    def body(prev_ref, a_ref, cw_ref, cb_ref, w_ref, x_ref, g_ref, o_ref, y_ref, f_ref, ap_ref):
        i = pl.program_id(0)
        acc = None
        for c in range(NGT):
            cols = slice(c * GT, (c + 1) * GT)
            both = lambda ref: jnp.concatenate([ref[0, :, cols].astype(F32), ref[1, :, cols].astype(F32)], axis=1)
            cwv, cbv = both(cw_ref), both(cb_ref)
            ext = jnp.concatenate([jnp.where(i > 0, both(prev_ref), 0.0), both(a_ref)], axis=0)
            ap = (_shift_down(ext, 2) * cwv[0:1, :] + _shift_down(ext, 1) * cwv[1:2, :]
                  + ext * cwv[2:3, :] + cbv)[HALO:, :]
            g, val = ap[:, :GT], ap[:, GT:]
            fch = (g * _sigmoid(g) * val).astype(BF16)
            f_ref[:, cols] = fch
            ap_ref[0, :, cols] = g.astype(BF16)
            ap_ref[1, :, cols] = val.astype(BF16)
            part = _dot(fch, w_ref[cols, :])
            acc = part if acc is None else acc + part
        y_ref[...] = acc.astype(BF16)
        o_ref[...] = x_ref[...] + g_ref[...] * acc

    row = pl.BlockSpec((ts, d), lambda i: (i, 0))
    planes = pl.BlockSpec((2, ts, DFF), lambda i: (0, i, 0))
    return pl.pallas_call(
        body, name=name, grid=(s // ts,),
        in_specs=[pl.BlockSpec((2, HALO, DFF), lambda i: (0, jnp.maximum(i * hb - 1, 0), 0)), planes,
                  pl.BlockSpec((2, 8, DFF), lambda i: (0, 0, 0)), pl.BlockSpec((2, 1, DFF), lambda i: (0, 0, 0)),
                  pl.BlockSpec((DFF, d), lambda i: (0, 0)), row, pl.BlockSpec((1, d), lambda i: (0, 0))],
        out_specs=[row, row, pl.BlockSpec((ts, DFF), lambda i: (i, 0)), planes],
        out_shape=[jax.ShapeDtypeStruct((s, d), F32), jax.ShapeDtypeStruct((s, d), BF16),
                   jax.ShapeDtypeStruct((s, DFF), BF16), jax.ShapeDtypeStruct((2, s, DFF), BF16)],
        compiler_params=_cparams(("arbitrary",)),
    )(a, a, cw, cb, w, xin, gate)


def _down_bwd_conv(dy, w, a, ap, cw, ts, name):
    s, d = dy.shape
    hb = ts // HALO
    nt = s // ts
    nhb = s // HALO

    def body(dy_ref, dyn_ref, w_ref, a_ref, ap_ref, apn_ref, cw_ref, da_ref, red_ref):
        i = pl.program_id(0)

        @pl.when(i == 0)
        def _():
            red_ref[...] = jnp.zeros_like(red_ref)
        dyn = jnp.where(i < nt - 1, dyn_ref[...], jnp.zeros_like(dyn_ref))
        dye = jnp.concatenate([dy_ref[...], dyn], axis=0)
        for c in range(NGT):
            cols = slice(c * GT, (c + 1) * GT)
            both = lambda ref: jnp.concatenate([ref[0, :, cols].astype(F32), ref[1, :, cols].astype(F32)], axis=1)
            cwv = both(cw_ref)
            dfe = _dot_nt(dye, w_ref[cols, :])
            apv = jnp.concatenate([both(ap_ref), both(apn_ref)], axis=0)
            g, val = apv[:, :GT], apv[:, GT:]
            sg = _sigmoid(g)
            dap = jnp.concatenate([dfe * val * (sg * (1.0 + g * (1.0 - sg))), dfe * (g * sg)], axis=1)
            shifted = [_shift_up(dap, 2)[:ts], _shift_up(dap, 1)[:ts], dap[:ts]]
            da = shifted[0] * cwv[0:1, :] + shifted[1] * cwv[1:2, :] + shifted[2] * cwv[2:3, :]
            av = both(a_ref)
            sums = [jnp.sum(av * t, axis=0, keepdims=True) for t in shifted]
            sums.append(jnp.sum(shifted[2], axis=0, keepdims=True))
            for e in range(2):
                half = slice(e * GT, (e + 1) * GT)
                da_ref[e, :, cols] = da[:, half].astype(BF16)
                for r, sm in enumerate(sums):
                    red_ref[e, r:r + 1, cols] += sm[:, half]

    planes = pl.BlockSpec((2, ts, DFF), lambda i: (0, i, 0))
    nxt = lambda i: jnp.minimum((i + 1) * hb, nhb - 1)
    return pl.pallas_call(
        body, name=name, grid=(nt,),
        in_specs=[pl.BlockSpec((ts, d), lambda i: (i, 0)), pl.BlockSpec((HALO, d), lambda i: (nxt(i), 0)),
                  pl.BlockSpec((DFF, d), lambda i: (0, 0)), planes, planes,
                  pl.BlockSpec((2, HALO, DFF), lambda i: (0, nxt(i), 0)),
                  pl.BlockSpec((2, 8, DFF), lambda i: (0, 0, 0))],
        out_specs=[planes, pl.BlockSpec((2, 8, DFF), lambda i: (0, 0, 0))],
        out_shape=[jax.ShapeDtypeStruct((2, s, DFF), BF16), jax.ShapeDtypeStruct((2, 8, DFF), F32)],
        compiler_params=_cparams(("arbitrary",)),
    )(dy, dy, w, a, ap, ap, cw)


def _chunk_mask(transposed=False):
    t = lax.broadcasted_iota(jnp.int32, (SGB, SGB), 0) // CHUNK
    u = lax.broadcasted_iota(jnp.int32, (SGB, SGB), 1) // CHUNK
    return (t <= u) if transposed else (u <= t)


def _sgu_ln(v, gain, bias):
    mu = jnp.mean(v, axis=-1, keepdims=True)
    vc = v - mu
    rstd = lax.rsqrt(jnp.mean(vc * vc, axis=-1, keepdims=True) + EPS)
    vhat = vc * rstd
    return vhat * gain + bias, vhat, rstd


def _sgu_fwd(z, vgain, vbias, ws, bst, w_out, xin, gate, tr, name):
    s = z.shape[0]

    def body(zu_ref, zv_ref, vg_ref, vb_ref, ws_ref, bs_ref, wo_ref, x_ref, gt_ref, xo_ref, yo_ref, y_ref):
        u = _gelu(zu_ref[...].astype(F32))
        vn, _, _ = _sgu_ln(_gelu(zv_ref[...].astype(F32)), vg_ref[...], vb_ref[...])
        vn = vn.astype(BF16)
        mask = _chunk_mask()
        for g in range(SGG):
            w = jnp.where(mask, ws_ref[g], 0.0).astype(BF16)
            for b in range(tr // SGB):
                rs, cs = slice(b * SGB, (b + 1) * SGB), slice(g * SGC, (g + 1) * SGC)
                mixed = _dot(w, vn[rs, cs]) + bs_ref[:, g:g + 1]
                y_ref[rs, cs] = (u[rs, cs] * mixed).astype(BF16)
        yo = _dot(y_ref[...], wo_ref[...])
        xo_ref[...] = x_ref[...] + gt_ref[...] * yo
        yo_ref[...] = yo.astype(BF16)

    vec = pl.BlockSpec((1, SGW), lambda i: (0, 0))
    row = pl.BlockSpec((tr, D), lambda i: (i, 0))
    return pl.pallas_call(
        body, name=name, grid=(s // tr,),
        in_specs=[pl.BlockSpec((tr, SGW), lambda i: (i, 0)), pl.BlockSpec((tr, SGW), lambda i: (i, 1)),
                  vec, vec, pl.BlockSpec((SGG, SGB, SGB), lambda i: (0, 0, 0)),
                  pl.BlockSpec((SGB, LANES), lambda i: (0, 0)), pl.BlockSpec((SGW, D), lambda i: (0, 0)), row,
                  pl.BlockSpec((1, D), lambda i: (0, 0))],
        out_specs=[row, row, pl.BlockSpec((tr, SGW), lambda i: (i, 0))],
        out_shape=[jax.ShapeDtypeStruct((s, D), F32), jax.ShapeDtypeStruct((s, D), BF16),
                   jax.ShapeDtypeStruct((s, SGW), BF16)],
        compiler_params=_cparams(("arbitrary",)),
    )(z, z, vgain, vbias, ws, bst, w_out, xin, gate)


def _sgu_bwd(z, dy, vgain, vbias, ws, wst, bst, tr, name):
    s = z.shape[0]

    def body(zu_ref, zv_ref, dy_ref, vg_ref, vb_ref, ws_ref, wst_ref, bs_ref,
             dz_ref, rb_ref, rv_ref, dws_ref, dbs_ref, dvn_s):
        @pl.when(pl.program_id(0) == 0)
        def _():
            rb_ref[...] = jnp.zeros_like(rb_ref)
            rv_ref[...] = jnp.zeros_like(rv_ref)
            dws_ref[...] = jnp.zeros_like(dws_ref)
            dbs_ref[...] = jnp.zeros_like(dbs_ref)
        zu = zu_ref[...].astype(F32)
        zv = zv_ref[...].astype(F32)
        u, gu = _gelu_and_grad(zu)
        v, gv = _gelu_and_grad(zv)
        vn, vhat, rstd = _sgu_ln(v, vg_ref[...], vb_ref[...])
        vnb = vn.astype(BF16)
        dyv = dy_ref[...].astype(F32)
        dmix = (dyv * u).astype(BF16)
        mask = _chunk_mask()
        mask_t = _chunk_mask(transposed=True)
        lane = _lane((SGB, LANES))
        dbs = jnp.zeros((SGB, LANES), F32)
        for g in range(SGG):
            w = jnp.where(mask, ws_ref[g], 0.0).astype(BF16)
            wt = jnp.where(mask_t, wst_ref[g], 0.0).astype(BF16)
            dw = jnp.zeros((SGB, SGB), F32)
            for b in range(tr // SGB):
                rs, cs = slice(b * SGB, (b + 1) * SGB), slice(g * SGC, (g + 1) * SGC)
                mixed = _dot(w, vnb[rs, cs]) + bs_ref[:, g:g + 1]
                dz_ref[rs, cs] = (dyv[rs, cs] * mixed * gu[rs, cs]).astype(BF16)
                dm = dmix[rs, cs]
                dw = dw + _dot_nt(dm, vnb[rs, cs])
                dbs = dbs + jnp.where(lane == g, jnp.sum(dm.astype(F32), axis=-1, keepdims=True), 0.0)
                dvn_s[rs, cs] = _dot(wt, dm)
            dws_ref[g] += jnp.where(mask, dw, 0.0)
        dbs_ref[...] += dbs
        dvn = dvn_s[...]
        rv_ref[0:1, :] += jnp.sum(dvn * vhat, axis=0, keepdims=True)
        rv_ref[1:2, :] += jnp.sum(dvn, axis=0, keepdims=True)
        dvh = dvn * vg_ref[...]
        dv = rstd * (dvh - jnp.mean(dvh, axis=-1, keepdims=True)
                     - vhat * jnp.mean(dvh * vhat, axis=-1, keepdims=True))
        dz_ref[:, SGW:] = (dv * gv).astype(BF16)
        dzf = dz_ref[...].astype(F32)
        rb_ref[0:1, :] += jnp.sum(dzf, axis=0, keepdims=True)

    vec = pl.BlockSpec((1, SGW), lambda i: (0, 0))
    wsp = pl.BlockSpec((SGG, SGB, SGB), lambda i: (0, 0, 0))
    return pl.pallas_call(
        body, name=name, grid=(s // tr,),
        in_specs=[pl.BlockSpec((tr, SGW), lambda i: (i, 0)), pl.BlockSpec((tr, SGW), lambda i: (i, 1)),
                  pl.BlockSpec((tr, SGW), lambda i: (i, 0)), vec, vec, wsp, wsp,
                  pl.BlockSpec((SGB, LANES), lambda i: (0, 0))],
        out_specs=[pl.BlockSpec((tr, 2 * SGW), lambda i: (i, 0)),
                   pl.BlockSpec((8, 2 * SGW), lambda i: (0, 0)),
                   pl.BlockSpec((8, SGW), lambda i: (0, 0)), wsp,
                   pl.BlockSpec((SGB, LANES), lambda i: (0, 0))],
        out_shape=[jax.ShapeDtypeStruct((s, 2 * SGW), BF16), jax.ShapeDtypeStruct((8, 2 * SGW), F32),
                   jax.ShapeDtypeStruct((8, SGW), F32), jax.ShapeDtypeStruct((SGG, SGB, SGB), F32),
                   jax.ShapeDtypeStruct((SGB, LANES), F32)],
        scratch_shapes=[pltpu.VMEM((tr, SGW), F32)],
        compiler_params=_cparams(("arbitrary",)),
    )(z, z, dy, vgain, vbias, ws, wst, bst)


def _final_loss(x, fg, tgt, gprev, yprev, ts, name):
    s, d = x.shape

    def body(x_ref, fg_ref, t_ref, g_ref, y_ref, l_ref, dx_ref, dy_ref, red_ref):
        @pl.when(pl.program_id(0) == 0)
        def _():
            l_ref[...] = jnp.zeros_like(l_ref)
            red_ref[...] = jnp.zeros_like(red_ref)
        xv = x_ref[...]
        r = _rstd_rows(xv)
        xh = xv * r
        err = xh * fg_ref[...] - t_ref[...]
        l_ref[...] += 0.5 * jnp.sum(jnp.mean(err * err, axis=-1, keepdims=True))
        dyo = err * (1.0 / d)
        dxh = dyo * fg_ref[...]
        dx = r * (dxh - xh * jnp.mean(dxh * xh, axis=-1, keepdims=True))
        dx_ref[...] = dx
        dy_ref[...] = (dx * g_ref[...]).astype(BF16)
        red_ref[0:1, :] += jnp.sum(dyo * xh, axis=0, keepdims=True)
        red_ref[1:2, :] += jnp.sum(dx * y_ref[...].astype(F32), axis=0, keepdims=True)

    row = pl.BlockSpec((ts, d), lambda i: (i, 0))
    vec = pl.BlockSpec((1, d), lambda i: (0, 0))
    return pl.pallas_call(
        body, name=name, grid=(s // ts,),
        in_specs=[row, vec, row, vec, row],
        out_specs=[pl.BlockSpec((8, LANES), lambda i: (0, 0)), row, row, pl.BlockSpec((8, d), lambda i: (0, 0))],
        out_shape=[jax.ShapeDtypeStruct((8, LANES), F32), jax.ShapeDtypeStruct((s, d), F32),
                   jax.ShapeDtypeStruct((s, d), BF16), jax.ShapeDtypeStruct((8, d), F32)],
        compiler_params=_cparams(("arbitrary",)),
    )(x, fg, tgt, gprev, yprev)


def _norm_bwd(xin, dh, dxout, ng, sc, gprev, yprev, ts, name):
    s, d = xin.shape
    has_prev = gprev is not None
    fused = isinstance(dh, tuple)
    if fused:
        a, w, tk = dh
        ka, kw = a.shape[-1], w.shape[-1]
        k = ka * (a.shape[0] if a.ndim == 3 else 1)

    def body(*refs):
        if fused:
            x_ref, a_ref, w_ref, dxo_ref, ng_ref, sc_ref = refs[:6]
            rest = refs[6:]
        else:
            x_ref, dh_ref, dxo_ref, ng_ref, sc_ref = refs[:5]
            rest = refs[5:]
        if has_prev:
            g_ref, y_ref, dx_ref, dy_ref, red_ref = rest
        else:
            dx_ref, red_ref = rest

        @pl.when(pl.program_id(0) == 0)
        def _():
            red_ref[...] = jnp.zeros_like(red_ref)
        if fused:
            dhv = None
            for g0 in range(0, k, tk):
                av = a_ref[g0 // ka, :, g0 % ka:g0 % ka + tk] if a.ndim == 3 else a_ref[:, g0:g0 + tk]
                wv = w_ref[g0 // kw, :, g0 % kw:g0 % kw + tk] if w.ndim == 3 else w_ref[:, g0:g0 + tk]
                part = _dot_nt(av, wv)
                dhv = part if dhv is None else dhv + part
        else:
            dhv = dh_ref[...]
        xv = x_ref[...]
        r = _rstd_rows(xv)
        xh = xv * r
        dr = dhv * (1.0 + sc_ref[...])
        t = dr * ng_ref[...]
        dx = dxo_ref[...] + r * (t - xh * jnp.mean(t * xh, axis=-1, keepdims=True))
        dx_ref[...] = dx
        red_ref[0:1, :] += jnp.sum(dhv, axis=0, keepdims=True)
        red_ref[1:2, :] += jnp.sum(dhv * (xh * ng_ref[...]), axis=0, keepdims=True)
        red_ref[2:3, :] += jnp.sum(dr * xh, axis=0, keepdims=True)
        if has_prev:
            dy_ref[...] = (dx * g_ref[...]).astype(BF16)
            red_ref[3:4, :] += jnp.sum(dx * y_ref[...].astype(F32), axis=0, keepdims=True)

    row = pl.BlockSpec((ts, d), lambda i: (i, 0))
    vec = pl.BlockSpec((1, d), lambda i: (0, 0))
    red = pl.BlockSpec((8, d), lambda i: (0, 0))
    if fused:
        a_spec = (pl.BlockSpec((a.shape[0], ts, ka), lambda i: (0, i, 0)) if a.ndim == 3
                  else pl.BlockSpec((ts, ka), lambda i: (i, 0)))
        w_spec = pl.BlockSpec(w.shape, (lambda i: (0, 0, 0)) if w.ndim == 3 else (lambda i: (0, 0)))
        dh_specs, dh_args = [a_spec, w_spec], (a, w)
    else:
        dh_specs, dh_args = [row], (dh,)
    if has_prev:
        in_specs, args = [row] + dh_specs + [row, vec, vec, vec, row], (xin,) + dh_args + (dxout, ng, sc, gprev, yprev)
        out_specs = [row, row, red]
        out_shape = [jax.ShapeDtypeStruct((s, d), F32), jax.ShapeDtypeStruct((s, d), BF16),
                     jax.ShapeDtypeStruct((8, d), F32)]
    else:
        in_specs, args = [row] + dh_specs + [row, vec, vec], (xin,) + dh_args + (dxout, ng, sc)
        out_specs = [row, red]
        out_shape = [jax.ShapeDtypeStruct((s, d), F32), jax.ShapeDtypeStruct((8, d), F32)]
    return pl.pallas_call(
        body, name=name, grid=(s // ts,), in_specs=in_specs, out_specs=out_specs, out_shape=out_shape,
        compiler_params=_cparams(("arbitrary",)),
    )(*args)


def _ada_mod(c_all, ada_w, ada_b):
    nb = c_all.shape[0]
    da = ada_w.shape[2]

    def body(c_ref, w_ref, b_ref, o_ref, ca_ref):
        cv = c_ref[...]
        ca = cv * _sigmoid(cv)
        ca_ref[...] = ca
        o_ref[0] = lax.dot_general(ca, w_ref[0], (((1,), (0,)), ((), ())), precision=lax.Precision.HIGHEST,
                                   preferred_element_type=F32) + b_ref[0]

    return pl.pallas_call(
        body, name="ada_mod", grid=(2,),
        in_specs=[pl.BlockSpec((nb, D), lambda i: (0, 0)), pl.BlockSpec((1, D, da), lambda i: (i, 0, 0)),
                  pl.BlockSpec((1, 1, da), lambda i: (i, 0, 0))],
        out_specs=[pl.BlockSpec((1, nb, da), lambda i: (i, 0, 0)), pl.BlockSpec((nb, D), lambda i: (0, 0))],
        out_shape=[jax.ShapeDtypeStruct((2, nb, da), F32), jax.ShapeDtypeStruct((nb, D), F32)],
        compiler_params=_cparams(("arbitrary",)),
    )(c_all, ada_w, ada_b)


def _ada_w_grad(c_act_t, dmod):
    nb = c_act_t.shape[1]
    da = dmod.shape[2]
    tn = 512

    def body(c_ref, d_ref, o_ref):
        acc = c_ref[:, 0:1] * d_ref[0, 0:1, :]
        for b in range(1, nb):
            acc = acc + c_ref[:, b:b + 1] * d_ref[0, b:b + 1, :]
        o_ref[0] = acc

    return pl.pallas_call(
        body, name="ada_w_grad", grid=(2, da // tn),
        in_specs=[pl.BlockSpec((D, nb), lambda i, j: (0, 0)), pl.BlockSpec((1, nb, tn), lambda i, j: (i, 0, j))],
        out_specs=pl.BlockSpec((1, D, tn), lambda i, j: (i, 0, j)),
        out_shape=jax.ShapeDtypeStruct((2, D, da), F32),
        compiler_params=_cparams(("arbitrary", "arbitrary")),
    )(c_act_t, dmod)


def _conv_planes(cw, cb):
    cwp = jnp.swapaxes(cw.reshape(3, 2, DFF), 0, 1)
    return jnp.pad(cwp, ((0, 0), (0, 5), (0, 0))), cb.reshape(2, 1, DFF)


def _local_step(x, tgt, mod, wts, small, comm=None):
    wts = dict(wts)
    s = x.shape[0]
    ts, tq, tr, tp = TS, TQ, TR, TP
    tkw = min(TKW, s)
    tf = min(256, s)
    zb = lambda n: jnp.zeros((1, n), F32)
    m6 = mod.reshape(2, 6, 1, D)
    sh1, sc1, g1, sh2, sc2, g2 = ([m6[i, k] for i in range(2)] for k in range(6))
    n1g, n2g = small["norm1_g"], small["norm2_g"]
    row = lambda a, i: a[i:i + 1]

    qg2 = jnp.tile(small["fox_q_gain"], (1, 2))
    kg2 = jnp.tile(small["fox_k_gain"], (1, 2))
    bfp = jnp.pad(small["fox_b_f"], ((0, 0), (0, LANES - H)))
    proj, h1 = _norm_mod_matmul(x, row(n1g, 0), sc1[0], sh1[0], wts["fox_w_in"], zb(FOX_NP), F32, ts, 1408, "fox_in")
    qa, ka, va = _fox_post(proj, qg2, kg2, bfp, tp, "fox_post")
    att, ql, gathered = _attn_fwd(qa, ka, va, tq, "attn_fwd", shards=comm["shards"] if comm else ())
    if comm:
        wts.update(comm["make_wts"](gathered))
    x1, y0, gated = _gate_out(att, proj, wts["fox_w_out"], x, g1[0], ts, "fox_gate_out")

    def ffn_fwd(xin, i, tag):
        cw, cb = _conv_planes(small["ffn_conv_w"][i], small["ffn_conv_b"][i])
        a, h = _norm_mod_matmul(xin, row(n2g, i), sc2[i], sh2[i], wts["ffn_w_up"][i], zb(2 * DFF), BF16, ts, 1408,
                                "ffn_up" + tag, planes=2)
        xo, y, f, ap = _conv_down(a, cw, cb, wts["ffn_w_down"][i], xin, g2[i], min(256, s), "ffn_conv_down" + tag)
        return xo, (a, h, f, y, cw, ap)

    x2, ffn0 = ffn_fwd(x1, 0, "0")

    bst = jnp.pad(small["sgu_b_s"].T, ((0, 0), (0, LANES - SGG)))
    ws = small["sgu_w_s"]
    z, h3 = _norm_mod_matmul(x2, row(n1g, 1), sc1[1], sh1[1], wts["sgu_w_in"], small["sgu_b_in"], BF16, ts, 1024,
                             "sgu_in")
    x3, y1, yy = _sgu_fwd(z, small["sgu_v_gain"], small["sgu_v_bias"], ws, bst, wts["sgu_w_out"], x2, g1[1], tr,
                          "sgu_mix_out")
    x4, ffn1 = ffn_fwd(x3, 1, "1")

    lsum, dx4, dy, redf = _final_loss(x4, small["final_g"], tgt, g2[1], ffn1[3], ts, "final_loss")
    grads = {"final_g": redf[0]}
    dmod = [[None] * 6, [None] * 6]
    dmod[1][5] = redf[1]

    def ffn_bwd(dxo, dy2, xin, i, saved, gprev, yprev, tag):
        a, h, f, _, cw, ap = saved
        wd, wu = wts["ffn_w_down"][i], wts["ffn_w_up"][i]
        g_wd = _matmul(f, dy2, True, False, 1408, D, tkw, BF16, "ffn_dwdown" + tag)
        da, redc = _down_bwd_conv(dy2, wd, a, ap, cw, min(256, s), "ffn_down_bwd_conv" + tag)
        g_wu = _matmul(h, da, True, False, D, 1408, tkw, BF16, "ffn_dwup" + tag, out_parts=N_CHIP)
        outs = _norm_bwd(xin, (da, wu, 1408), dxo, row(n2g, i), sc2[i], gprev, yprev, tf, "ffn_dh_norm_bwd" + tag)
        return outs, g_wd, g_wu, redc

    (dx3, dy1, red), g_wd1, g_wu1, redc1 = ffn_bwd(dx4, dy, x3, 1, ffn1, g1[1], y1, "1")
    dmod[1][3], dmod[1][4], dn2g1, dmod[1][2] = red[0], red[1], red[2], red[3]

    g_swo = _matmul(yy, dy1, True, False, 1024, D, tkw, BF16, "sgu_dwout")
    dyy = _matmul_wt(dy1, wts["sgu_w_out"], 1024, D, BF16, ts, "sgu_dyy")
    wst = jnp.swapaxes(ws, 1, 2)
    dz, rb, rv, dws, dbst = _sgu_bwd(z, dyy, small["sgu_v_gain"], small["sgu_v_bias"], ws, wst, bst, tr, "sgu_mix_bwd")
    g_swi = _matmul(h3, dz, True, False, D, 1024, tkw, BF16, "sgu_dwin", out_parts=N_CHIP)
    dx2, dy2_0, red = _norm_bwd(x2, (dz, wts["sgu_w_in"], 1024), dx3, row(n1g, 1), sc1[1], g2[0], ffn0[3], tf,
                                "sgu_dh_norm_bwd")
    dmod[1][0], dmod[1][1], dn1g1, dmod[0][5] = red[0], red[1], red[2], red[3]

    (dx1, dy0, red), g_wd0, g_wu0, redc0 = ffn_bwd(dx2, dy2_0, x1, 0, ffn0, g1[0], y0, "0")
    dmod[0][3], dmod[0][4], dn2g0, dmod[0][2] = red[0], red[1], red[2], red[3]

    g_fwo = _matmul(gated, dy0, True, False, D, D, tkw, BF16, "fox_dwout")
    doa, dop = _attn_bwd_prep(dy0, wts["fox_w_out"], att, proj, ts, "attn_bwd_prep")
    css = comm["rs_prepare"]([g_fwo, g_swi, g_swo, g_wu0, g_wu1, g_wd0, g_wd1]) if comm else []
    dqa, dka, dva, rcvs = _attn_bwd(ql, ka, va, doa, tq, "attn_bwd", css=css)
    dproj, redx = _fox_post_bwd(proj, dqa, dka, dva, dop, qg2, kg2, bfp, tp, "fox_post_bwd")
    g_fwi = _matmul(h1, dproj, True, False, D, 1408, tkw, BF16, "fox_dwin")
    dx0, red = _norm_bwd(x, (dproj, wts["fox_w_in"], 1408), dx1, row(n1g, 0), sc1[0], None, None, tf,
                         "fox_dh_norm_bwd")
    dmod[0][0], dmod[0][1], dn1g0 = red[0], red[1], red[2]

    grads.update(
        fox_w_in=g_fwi, fox_w_out=g_fwo, sgu_w_in=g_swi, sgu_w_out=g_swo,
        ffn_w_up=[g_wu0, g_wu1], ffn_w_down=[g_wd0, g_wd1],
        fox_q_gain=redx[0, :DH] + redx[0, DH:], fox_k_gain=redx[1, :DH] + redx[1, DH:], fox_b_f=redx[2, :H],
        sgu_b_in=rb[0], sgu_v_gain=rv[0], sgu_v_bias=rv[1], sgu_w_s=dws, sgu_b_s=dbst[:, :SGG].T,
        ffn_conv_w=jnp.stack([jnp.swapaxes(r[:, 0:3], 0, 1).reshape(3, 2 * DFF) for r in (redc0, redc1)]),
        ffn_conv_b=jnp.stack([r[:, 3].reshape(2 * DFF) for r in (redc0, redc1)]),
        norm1_g=jnp.stack([dn1g0, dn1g1]), norm2_g=jnp.stack([dn2g0, dn2g1]),
    )
    dmod_arr = jnp.stack([jnp.concatenate(dmod[0]), jnp.concatenate(dmod[1])])
    return lsum[0, 0], dx0, grads, dmod_arr, (css, rcvs)


N_DEV = 8
N_CHIP = 4
HBM_SPEC = pl.BlockSpec(memory_space=pltpu.HBM)
VMEM_SPEC = pl.BlockSpec(memory_space=pltpu.VMEM)


def _mesh_pos():
    return lax.axis_index("x"), lax.axis_index("y"), lax.axis_index("c")


def _other_chips(x, y):
    return [(1 - x, y), (x, 1 - y), (1 - x, 1 - y)]


def _remote(src, dst, ssem, rsem, dev):
    return pltpu.make_async_remote_copy(src_ref=src, dst_ref=dst, send_sem=ssem, recv_sem=rsem,
                                        device_id=dev, device_id_type=MESH)


def _allgather8(xb, name):
    m_per, n = xb.shape

    def body(x_ref, out_ref, send_sems, recv_sems, local_sem):
        x, y, c = _mesh_pos()
        me, sibling = (x, y, c), (x, y, 1 - c)
        chips = _other_chips(x, y)

        def rows(px, py, pc):
            return out_ref.at[pl.ds((4 * px + 2 * py + pc) * m_per, m_per), :]

        def copy(k, block, to, src=None):
            return _remote(rows(*block) if src is None else src, rows(*block),
                           send_sems.at[k], recv_sems.at[k], to)

        mine = pltpu.make_async_copy(x_ref, rows(*me), local_sem)
        mine.start()
        first = [copy(0, me, sibling, src=x_ref)]
        first += [copy(1 + j, me, (*chip, c), src=x_ref) for j, chip in enumerate(chips)]
        for cp in first:
            cp.start()
        passed = [copy(4 + j, (*chip, c), sibling) for j, chip in enumerate(chips)]
        for j, chip in enumerate(chips):
            copy(1 + j, (*chip, c), me).wait_recv()
            passed[j].start()
        copy(0, sibling, me).wait_recv()
        for j, chip in enumerate(chips):
            copy(4 + j, (*chip, 1 - c), me).wait_recv()
        for cp in first + passed:
            cp.wait_send()
        mine.wait()

    return pl.pallas_call(
        body, name=name,
        out_shape=jax.ShapeDtypeStruct((N_DEV * m_per, n), xb.dtype),
        in_specs=[VMEM_SPEC], out_specs=VMEM_SPEC,
        scratch_shapes=[pltpu.SemaphoreType.DMA((7,)), pltpu.SemaphoreType.DMA((7,)), pltpu.SemaphoreType.DMA],
        compiler_params=pltpu.CompilerParams(vmem_limit_bytes=V7X_VMEM_LIMIT),
    )(xb)


def _gather_shards(shards, name):
    na = len(shards)

    def body(*refs):
        p_refs, o_refs = refs[:na], refs[na:2 * na]
        send_sems, recv_sems, pass_send, pass_recv = refs[2 * na:]
        x, y, c = _mesh_pos()
        me = 2 * x + y
        sibling = (x, y, 1 - c)
        chips = _other_chips(x, y)

        def half(a, ci, hf):
            rh = shards[a].shape[0] // 2
            return o_refs[a].at[ci, pl.ds(hf * rh, rh), :]

        sends = []
        for a in range(na):
            rh = shards[a].shape[0] // 2
            for k, chip in enumerate(chips):
                sends.append(_remote(p_refs[a].at[pl.ds(c * rh, rh), :], half(a, me, c),
                                     send_sems.at[3 * a + k], recv_sems.at[3 * a + k], (*chip, c)))
        for cp in sends:
            cp.start()
        passed = []
        for a in range(na):
            for k, chip in enumerate(chips):
                ci = 2 * chip[0] + chip[1]
                _remote(half(a, ci, c), half(a, ci, c), send_sems.at[3 * a + k], recv_sems.at[3 * a + k],
                        (*chip, c)).wait_recv()
                cp = _remote(half(a, ci, c), half(a, ci, c), pass_send.at[3 * a + k], pass_recv.at[3 * a + k], sibling)
                cp.start()
                passed.append(cp)
        for a in range(na):
            for k, chip in enumerate(chips):
                ci = 2 * chip[0] + chip[1]
                _remote(half(a, ci, 1 - c), half(a, ci, 1 - c), pass_send.at[3 * a + k], pass_recv.at[3 * a + k],
                        sibling).wait_recv()
        for cp in sends + passed:
            cp.wait_send()

    return pl.pallas_call(
        body, name=name,
        out_shape=[jax.ShapeDtypeStruct((N_CHIP,) + p.shape, p.dtype) for p in shards],
        in_specs=[HBM_SPEC] * na, out_specs=[HBM_SPEC] * na,
        scratch_shapes=[pltpu.SemaphoreType.DMA((3 * na,))] * 4,
    )(*shards)


def _rs_to_sibling(gs, name):
    na = len(gs)

    def body(*refs):
        g_refs, o_refs, ssems, rsems = refs[:na], refs[na:2 * na], refs[2 * na], refs[2 * na + 1]
        x, y, c = _mesh_pos()
        cps = []
        for a in range(na):
            rh = gs[a].shape[1] // 2
            cp = _remote(g_refs[a].at[:, pl.ds((1 - c) * rh, rh), :], o_refs[a], ssems.at[a], rsems.at[a],
                         (x, y, 1 - c))
            cp.start()
            cps.append(cp)
        for cp in cps:
            cp.wait()

    return pl.pallas_call(
        body, name=name,
        out_shape=[jax.ShapeDtypeStruct((g.shape[0], g.shape[1] // 2, g.shape[2]), g.dtype) for g in gs],
        in_specs=[HBM_SPEC] * na, out_specs=[HBM_SPEC] * na,
        scratch_shapes=[pltpu.SemaphoreType.DMA((na,)), pltpu.SemaphoreType.DMA((na,))],
    )(*gs)


def _rs_chip_sum(g, sib, c_arr, tr, name):
    nc, r, n = g.shape
    rh = r // 2
    g4 = g.reshape(nc, 2, rh, n)

    def body(c_ref, g_ref, s_ref, o_ref):
        o_ref[...] = (g_ref[0].astype(F32) + s_ref[...].astype(F32)).astype(BF16)

    return pl.pallas_call(
        body, name=name, out_shape=jax.ShapeDtypeStruct((nc, rh, n), BF16),
        grid_spec=pltpu.PrefetchScalarGridSpec(
            num_scalar_prefetch=1, grid=(nc, rh // tr),
            in_specs=[pl.BlockSpec((1, 1, tr, n), lambda j, i, cr: (j, cr[0], i, 0)),
                      pl.BlockSpec((1, tr, n), lambda j, i, cr: (j, i, 0))],
            out_specs=pl.BlockSpec((1, tr, n), lambda j, i, cr: (j, i, 0))),
        compiler_params=_cparams(("arbitrary", "arbitrary")),
    )(c_arr, g4, sib)


def _rs_across_chips(css, name):
    na = len(css)

    def body(*refs):
        cs_refs, o_refs, send_sems, recv_sems = refs[:na], refs[na:2 * na], refs[2 * na], refs[2 * na + 1]
        x, y, c = _mesh_pos()
        cps = []
        for a in range(na):
            for k, chip in enumerate(_other_chips(x, y)):
                ci = 2 * chip[0] + chip[1]
                cp = _remote(cs_refs[a].at[ci], o_refs[a].at[k], send_sems.at[3 * a + k], recv_sems.at[3 * a + k],
                             (*chip, c))
                cp.start()
                cps.append(cp)
        for cp in cps:
            cp.wait()

    return pl.pallas_call(
        body, name=name, out_shape=[jax.ShapeDtypeStruct((3,) + cs.shape[1:], cs.dtype) for cs in css],
        in_specs=[HBM_SPEC] * na, out_specs=[HBM_SPEC] * na,
        scratch_shapes=[pltpu.SemaphoreType.DMA((3 * na,)), pltpu.SemaphoreType.DMA((3 * na,))],
    )(*css)


def _rs_final_sum(cs, rcv, me_arr, tr, name):
    nc, rh, n = cs.shape

    def body(m_ref, c_ref, r_ref, o_ref):
        acc = c_ref[0].astype(F32)
        for k in range(3):
            acc = acc + r_ref[k].astype(F32)
        o_ref[...] = acc

    return pl.pallas_call(
        body, name=name, out_shape=jax.ShapeDtypeStruct((rh, n), F32),
        grid_spec=pltpu.PrefetchScalarGridSpec(
            num_scalar_prefetch=1, grid=(rh // tr,),
            in_specs=[pl.BlockSpec((1, tr, n), lambda i, mr: (mr[0], i, 0)),
                      pl.BlockSpec((3, tr, n), lambda i, mr: (0, i, 0))],
            out_specs=pl.BlockSpec((tr, n), lambda i, mr: (i, 0))),
        compiler_params=_cparams(("arbitrary",)),
    )(me_arr, cs, rcv)


def _rs_swap_halves(halves, name):
    na = len(halves)

    def body(*refs):
        h_refs, o_refs, ssems, rsems = refs[:na], refs[na:2 * na], refs[2 * na], refs[2 * na + 1]
        x, y, c = _mesh_pos()
        cps = []
        for a in range(na):
            cp = _remote(h_refs[a], o_refs[a], ssems.at[a], rsems.at[a], (x, y, 1 - c))
            cp.start()
            cps.append(cp)
        for cp in cps:
            cp.wait()

    return pl.pallas_call(
        body, name=name, out_shape=[jax.ShapeDtypeStruct(h.shape, h.dtype) for h in halves],
        in_specs=[HBM_SPEC] * na, out_specs=[HBM_SPEC] * na,
        scratch_shapes=[pltpu.SemaphoreType.DMA((na,)), pltpu.SemaphoreType.DMA((na,))],
    )(*halves)


def _join_columns(parts, n_out, name):
    p, k, c = parts.shape
    tr = 128

    def body(w_ref, o_ref):
        for j in range(p):
            o_ref[:, j * c:(j + 1) * c] = w_ref[j]
        o_ref[:, p * c:] = jnp.zeros((tr, n_out - p * c), parts.dtype)

    return pl.pallas_call(
        body, name=name, grid=(k // tr,),
        in_specs=[pl.BlockSpec((p, tr, c), lambda i: (0, i, 0))],
        out_specs=pl.BlockSpec((tr, n_out), lambda i: (i, 0)),
        out_shape=jax.ShapeDtypeStruct((k, n_out), parts.dtype),
        compiler_params=_cparams(("arbitrary",)),
    )(parts)


def _split_columns(g, p, c, name):
    k, n = g.shape
    tr = 128

    def body(g_ref, o_ref):
        for j in range(p):
            o_ref[j] = g_ref[:, j * c:(j + 1) * c]

    return pl.pallas_call(
        body, name=name, grid=(k // tr,),
        in_specs=[pl.BlockSpec((tr, n), lambda i: (i, 0))],
        out_specs=pl.BlockSpec((p, tr, c), lambda i: (0, i, 0)),
        out_shape=jax.ShapeDtypeStruct((p, k, c), g.dtype),
        compiler_params=_cparams(("arbitrary",)),
    )(g)


def _sum8(g, name):
    nd, r, n = g.shape

    def body(g_ref, o_ref):
        acc = g_ref[0]
        for k in range(1, nd):
            acc = acc + g_ref[k]
        o_ref[...] = acc

    return pl.pallas_call(
        body, name=name, grid=(r // 8,),
        in_specs=[pl.BlockSpec((nd, 8, n), lambda i: (0, i, 0))],
        out_specs=pl.BlockSpec((8, n), lambda i: (i, 0)),
        out_shape=jax.ShapeDtypeStruct((r, n), F32),
        compiler_params=_cparams(("arbitrary",)),
    )(g)


def _adamw(w, g, m, v, name):
    lead = w.shape[0] if w.ndim == 3 else 1
    r, n = w.shape[-2:]
    tr = 128 if r % 128 == 0 else 8
    nr = r // tr
    bc1 = 1.0 - ADAM_B1 ** ADAM_STEP
    bc2 = 1.0 - ADAM_B2 ** ADAM_STEP

    def body(w_ref, g_ref, m_ref, v_ref, d_ref, mo_ref, vo_ref, go_ref):
        gv = g_ref[...]
        mn = ADAM_B1 * m_ref[...] + (1.0 - ADAM_B1) * gv
        vn = ADAM_B2 * v_ref[...] + (1.0 - ADAM_B2) * (gv * gv)
        d_ref[...] = -ADAM_LR * ((mn / bc1) / (jnp.sqrt(vn / bc2) + ADAM_EPS) + ADAM_WD * w_ref[...])
        mo_ref[...] = mn
        vo_ref[...] = vn
        go_ref[...] = gv

    blk = (pl.BlockSpec((None, tr, n), lambda l, i: (l, i, 0)) if w.ndim == 3
           else pl.BlockSpec((tr, n), lambda l, i: (i, 0)))
    gblk = pl.BlockSpec((tr, n), lambda l, i: (l * nr + i, 0))
    shp = jax.ShapeDtypeStruct(w.shape, F32)
    return pl.pallas_call(
        body, name=name, grid=(lead, nr), in_specs=[blk, gblk, blk, blk], out_specs=[blk] * 4, out_shape=[shp] * 4,
        compiler_params=_cparams(("arbitrary", "arbitrary")),
    )(w, g, m, v)


ROW = 1024
BIG = ("fox_w_in", "fox_w_out", "sgu_w_in", "sgu_w_out", "ffn_w_up", "ffn_w_down")
SMALL_SHARDED = ("sgu_b_in", "sgu_v_gain", "sgu_v_bias", "ffn_conv_w")
SMALL_REPL = ("fox_b_f", "fox_q_gain", "fox_k_gain", "sgu_w_s", "sgu_b_s", "ffn_conv_b", "ada_b",
              "norm1_g", "norm2_g", "final_g")
WEIGHTS = ("fox_w_in", "fox_b_f", "fox_q_gain", "fox_k_gain", "fox_w_out", "sgu_w_in", "sgu_b_in", "sgu_v_gain",
           "sgu_v_bias", "sgu_w_s", "sgu_b_s", "sgu_w_out", "ffn_w_up", "ffn_conv_w", "ffn_conv_b", "ffn_w_down",
           "ada_w", "ada_b", "norm1_g", "norm2_g", "final_g")


def _rows_of(a, mult=1):
    flat = a.reshape(-1)
    rows = -(-flat.shape[0] // ROW)
    rows = -(-rows // mult) * mult
    return jnp.pad(flat, (0, rows * ROW - flat.shape[0])).reshape(rows, ROW)


def _pack(parts, mult, total=None):
    p = jnp.concatenate([_rows_of(a, mult) for a in parts], axis=0)
    if total is not None:
        p = jnp.pad(p, ((0, total - p.shape[0]), (0, 0)))
    return p


def _unpack(pack, shapes, mult):
    out, r0 = [], 0
    for shp in shapes:
        size = int(np.prod(shp))
        rows = -(-(-(-size // ROW)) // mult) * mult
        out.append(pack[r0:r0 + rows].reshape(-1)[:size].reshape(shp))
        r0 += rows
    return out


def _big_shards(t):
    return [t["fox_w_in"][0], t["fox_w_out"][0], t["sgu_w_in"][0], t["sgu_w_out"][0],
            t["ffn_w_up"][0], t["ffn_w_up"][1], t["ffn_w_down"][0], t["ffn_w_down"][1]]


def _row_tile(rows):
    return next(t for t in (512, 352, 256, 128, 64) if rows % t == 0)


def kernel(x, c, fox_w_in, fox_b_f, fox_q_gain, fox_k_gain, fox_w_out, sgu_w_in, sgu_b_in, sgu_v_gain, sgu_v_bias, sgu_w_s, sgu_b_s, sgu_w_out, ffn_w_up, ffn_conv_w, ffn_conv_b, ffn_w_down, ada_w, ada_b, norm1_g, norm2_g, final_g, loss_target, m_fox_w_in, m_fox_b_f, m_fox_q_gain, m_fox_k_gain, m_fox_w_out, m_sgu_w_in, m_sgu_b_in, m_sgu_v_gain, m_sgu_v_bias, m_sgu_w_s, m_sgu_b_s, m_sgu_w_out, m_ffn_w_up, m_ffn_conv_w, m_ffn_conv_b, m_ffn_w_down, m_ada_w, m_ada_b, m_norm1_g, m_norm2_g, m_final_g, v_fox_w_in, v_fox_b_f, v_fox_q_gain, v_fox_k_gain, v_fox_w_out, v_sgu_w_in, v_sgu_b_in, v_sgu_v_gain, v_sgu_v_bias, v_sgu_w_s, v_sgu_b_s, v_sgu_w_out, v_ffn_w_up, v_ffn_conv_w, v_ffn_conv_b, v_ffn_w_down, v_ada_w, v_ada_b, v_norm1_g, v_norm2_g, v_final_g):
    w = dict(fox_w_in=fox_w_in, fox_b_f=fox_b_f, fox_q_gain=fox_q_gain, fox_k_gain=fox_k_gain, fox_w_out=fox_w_out,
             sgu_w_in=sgu_w_in, sgu_b_in=sgu_b_in, sgu_v_gain=sgu_v_gain, sgu_v_bias=sgu_v_bias, sgu_w_s=sgu_w_s,
             sgu_b_s=sgu_b_s, sgu_w_out=sgu_w_out, ffn_w_up=ffn_w_up, ffn_conv_w=ffn_conv_w, ffn_conv_b=ffn_conv_b,
             ffn_w_down=ffn_w_down, ada_w=ada_w, ada_b=ada_b, norm1_g=norm1_g, norm2_g=norm2_g, final_g=final_g)
    mom = dict(fox_w_in=m_fox_w_in, fox_b_f=m_fox_b_f, fox_q_gain=m_fox_q_gain, fox_k_gain=m_fox_k_gain,
               fox_w_out=m_fox_w_out, sgu_w_in=m_sgu_w_in, sgu_b_in=m_sgu_b_in, sgu_v_gain=m_sgu_v_gain,
               sgu_v_bias=m_sgu_v_bias, sgu_w_s=m_sgu_w_s, sgu_b_s=m_sgu_b_s, sgu_w_out=m_sgu_w_out,
               ffn_w_up=m_ffn_w_up, ffn_conv_w=m_ffn_conv_w, ffn_conv_b=m_ffn_conv_b, ffn_w_down=m_ffn_w_down,
               ada_w=m_ada_w, ada_b=m_ada_b, norm1_g=m_norm1_g, norm2_g=m_norm2_g, final_g=m_final_g)
    var = dict(fox_w_in=v_fox_w_in, fox_b_f=v_fox_b_f, fox_q_gain=v_fox_q_gain, fox_k_gain=v_fox_k_gain,
               fox_w_out=v_fox_w_out, sgu_w_in=v_sgu_w_in, sgu_b_in=v_sgu_b_in, sgu_v_gain=v_sgu_v_gain,
               sgu_v_bias=v_sgu_v_bias, sgu_w_s=v_sgu_w_s, sgu_b_s=v_sgu_b_s, sgu_w_out=v_sgu_w_out,
               ffn_w_up=v_ffn_w_up, ffn_conv_w=v_ffn_conv_w, ffn_conv_b=v_ffn_conv_b, ffn_w_down=v_ffn_w_down,
               ada_w=v_ada_w, ada_b=v_ada_b, norm1_g=v_norm1_g, norm2_g=v_norm2_g, final_g=v_final_g)

    ax, ay, ac = _mesh_pos()
    chip = 2 * ax + ay
    dev = 2 * chip + ac

    small_shard_shapes = tuple(w[n].shape for n in SMALL_SHARDED)
    blk = _pack([c] + [w[n] for n in SMALL_SHARDED], 1, 16)
    gat = _allgather8(blk, "gather_small").reshape(N_DEV, 16, ROW)
    c_all = gat[:, 0, :]
    per_chip = [_unpack(gat[2 * j, 1:], small_shard_shapes, 1) for j in range(N_CHIP)]
    full_small = {n: jnp.concatenate([per_chip[j][i] for j in range(N_CHIP)], axis=-1)
                  for i, n in enumerate(SMALL_SHARDED)}

    mine = [a.astype(BF16) for a in _big_shards(w)]
    with_own = lambda gat, own: [lax.dynamic_update_slice(g_, m_[None], (chip, 0, 0)) for g_, m_ in zip(gat, own)]
    fwi, = with_own(_gather_shards(mine[:1], "gather_fox_w_in"), mine[:1])
    fwi_full = _join_columns(fwi, FOX_NP, "join_fox_w_in")
    wts = dict(fox_w_in=fwi_full)

    def make_wts(gathered):
        fwo, swi, swo, up0, up1, dn0, dn1 = with_own(gathered, mine[1:])
        return dict(fox_w_out=fwo.reshape(D, D), sgu_w_in=swi, sgu_w_out=swo.reshape(SGW, D),
                    ffn_w_up=[up0, up1], ffn_w_down=[dn0.reshape(DFF, D), dn1.reshape(DFF, D)])

    c_arr = jnp.reshape(ac, (1,)).astype(jnp.int32)
    me_arr = jnp.reshape(chip, (1,)).astype(jnp.int32)

    def chip_sums(glist, tag):
        sibs = _rs_to_sibling(glist, "rs_sibling" + tag)
        return [_rs_chip_sum(g_, s_, c_arr, _row_tile(s_.shape[1]), "rs_chip_sum%s%d" % (tag, a))
                for a, (g_, s_) in enumerate(zip(glist, sibs))]

    def rs_prepare(gl):
        g_fwo, g_swi, g_swo, g_wu0, g_wu1, g_wd0, g_wd1 = gl
        return chip_sums([g_fwo.reshape(N_CHIP, 256, D), g_swi, g_swo.reshape(N_CHIP, 512, D), g_wu0, g_wu1,
                          g_wd0.reshape(N_CHIP, 704, D), g_wd1.reshape(N_CHIP, 704, D)], "")

    comm = dict(shards=mine[1:], make_wts=make_wts, rs_prepare=rs_prepare)

    da = ada_w.shape[2]
    ada_b_cols = lax.dynamic_slice_in_dim(ada_b, chip * da, da, axis=1)[:, None, :]
    mod_cols, c_act = _ada_mod(c_all, ada_w, ada_b_cols)
    mod_all = _allgather8(mod_cols.reshape(-1, ROW), "gather_mod").reshape(N_DEV, 2, N_DEV, da)
    mod_mine = lax.dynamic_index_in_dim(mod_all[0::2], dev, axis=2, keepdims=False)
    mod = jnp.swapaxes(mod_mine, 0, 1).reshape(2, N_CHIP * da)

    small = dict(norm1_g=norm1_g, norm2_g=norm2_g, final_g=final_g[None], fox_q_gain=fox_q_gain,
                 fox_k_gain=fox_k_gain, fox_b_f=fox_b_f, sgu_b_in=full_small["sgu_b_in"],
                 sgu_v_gain=full_small["sgu_v_gain"], sgu_v_bias=full_small["sgu_v_bias"], sgu_w_s=sgu_w_s[0],
                 sgu_b_s=sgu_b_s[0], ffn_conv_w=full_small["ffn_conv_w"], ffn_conv_b=ffn_conv_b)
    loss_dev, dx, g, dmod, (css, rcvs) = _local_step(x[0], loss_target[0], mod, wts, small, comm)

    g["ada_b"] = dmod
    g["loss"] = loss_dev
    small_names = ("ada_b",) + SMALL_SHARDED + tuple(n for n in SMALL_REPL if n != "ada_b") + ("loss",)
    gs = _pack([g[n] for n in small_names], 1)
    rows_s = -(-gs.shape[0] // 8) * 8
    gs = jnp.pad(gs, ((0, rows_s - gs.shape[0]), (0, 0)))
    gs_all = _allgather8(gs, "gather_small_grads").reshape(N_DEV, rows_s, ROW)
    gsum = _sum8(gs_all, "sum_small_grads")
    full_shapes = {n: w[n].shape for n in SMALL_REPL}
    full_shapes.update({n: w[n].shape[:-1] + (w[n].shape[-1] * N_CHIP,) for n in SMALL_SHARDED})
    full_shapes["loss"] = ()
    gfull = dict(zip(small_names, _unpack(gsum, [full_shapes[n] for n in small_names], 1)))
    grads = {n: gfull[n] for n in SMALL_REPL}
    for n in SMALL_SHARDED:
        width = w[n].shape[-1]
        grads[n] = lax.dynamic_slice_in_dim(gfull[n], chip * width, width, axis=gfull[n].ndim - 1)
    dmod_all = gs_all[:, :12, :].reshape(N_DEV, 2, N_CHIP * da)
    dmod_cols = jnp.swapaxes(lax.dynamic_slice_in_dim(dmod_all, chip * da, da, axis=2), 0, 1)
    grads["ada_w"] = _ada_w_grad(c_act.T, dmod_cols)

    gfi = _split_columns(g["fox_w_in"], N_CHIP, FOX_N // N_CHIP, "split_fox_w_in")
    cs_fox = chip_sums([gfi], "_fox")
    css = cs_fox + list(css)
    rcvs = list(_rs_across_chips(cs_fox, "rs_chips_fox")) + list(rcvs)
    halves = [_rs_final_sum(cs_, r_, me_arr, _row_tile(cs_.shape[1]), "rs_final_sum%d" % a)
              for a, (cs_, r_) in enumerate(zip(css, rcvs))]
    others = _rs_swap_halves(halves, "rs_swap")
    red = [jnp.concatenate([jnp.where(ac == 0, h_, o_), jnp.where(ac == 0, o_, h_)]) for h_, o_ in zip(halves, others)]
    grads.update(fox_w_in=red[0], fox_w_out=red[1], sgu_w_in=red[2], sgu_w_out=red[3],
                 ffn_w_up=jnp.concatenate([red[4], red[5]]), ffn_w_down=jnp.concatenate([red[6], red[7]]))

    delta, new_m, new_v = {}, {}, {}
    for n in BIG + ("ada_w",):
        g2 = grads[n].reshape(-1, w[n].shape[-1])
        delta[n], new_m[n], new_v[n], grads[n] = _adamw(w[n], g2, mom[n], var[n], "adamw_" + n)
    rest = SMALL_SHARDED + SMALL_REPL
    packs = [_pack([t[n] for n in rest], 1) for t in (w, grads, mom, var)]
    rows_r = -(-packs[0].shape[0] // 8) * 8
    packs = [jnp.pad(p, ((0, rows_r - p.shape[0]), (0, 0))) for p in packs]
    outs = _adamw(*packs, "adamw_small")[:3]
    for t, o in zip((delta, new_m, new_v), outs):
        t.update(zip(rest, _unpack(o, [w[n].shape for n in rest], 1)))

    loss = gfull["loss"]
    return (loss, dx[None], *[grads[n].reshape(w[n].shape) for n in WEIGHTS], *[delta[n] for n in WEIGHTS],
            *[new_m[n] for n in WEIGHTS], *[new_v[n] for n in WEIGHTS])
```

```python
import functools
import math

import numpy as np
import jax
import jax.numpy as jnp
from jax import lax
from jax.experimental import pallas as pl
from jax.experimental.pallas import tpu as pltpu

F32 = jnp.float32
BF16 = jnp.bfloat16
MESH = pl.DeviceIdType.MESH

D = 1024
H = 16
DH = 64
NP = H // 2
LANES = 128
DFF = 2816
SGW = 2048
SGG = 8
SGC = 256
SGB = 128
CHUNK = 64
EPS = 1e-6
FOX_N = 4 * D + H
FOX_NP = 4224
GT = 256
NGT = DFF // GT
SCALE = DH ** -0.5
LOG2E = 1.4426950408889634

ADAM_LR = 0.001
ADAM_B1 = 0.9
ADAM_B2 = 0.999
ADAM_EPS = 1e-08
ADAM_WD = 0.01
ADAM_STEP = 10

V7X_VMEM_LIMIT = 56 * 1024 * 1024

L_F = 64
L_NF = 67
L_LSE = 70


def _cparams(sem=None):
    return pltpu.CompilerParams(dimension_semantics=sem, vmem_limit_bytes=V7X_VMEM_LIMIT)


def _split3(x):
    hi = x.astype(BF16)
    r = x - hi.astype(F32)
    mid = r.astype(BF16)
    lo = (r - mid.astype(F32)).astype(BF16)
    return hi, mid, lo


def _dot(a, b, dims=(((1,), (0,)), ((), ()))):
    return lax.dot_general(a, b, dims, preferred_element_type=F32)


def _dot_nt(a, b):
    return _dot(a, b, (((1,), (1,)), ((), ())))


def _dot_tn(a, b):
    return _dot(a, b, (((0,), (0,)), ((), ())))


def _exact_dot(m_bf16, x_f32):
    hi, mid, lo = _split3(x_f32)
    return _dot(m_bf16, hi) + _dot(m_bf16, mid) + _dot(m_bf16, lo)


def _exact_dot_r(x_f32, m_bf16):
    hi, mid, lo = _split3(x_f32)
    return _dot(hi, m_bf16) + _dot(mid, m_bf16) + _dot(lo, m_bf16)


def _head_block_ones():
    r = lax.broadcasted_iota(jnp.int32, (LANES, LANES), 0) // DH
    c = lax.broadcasted_iota(jnp.int32, (LANES, LANES), 1) // DH
    return (r == c).astype(BF16)


def _sigmoid(x):
    return 1.0 / (1.0 + jnp.exp(-x))


def _gelu(x):
    c = math.sqrt(2.0 / math.pi)
    return 0.5 * x * (1.0 + jnp.tanh(c * (x + 0.044715 * (x * x * x))))


def _gelu_and_grad(x):
    c = math.sqrt(2.0 / math.pi)
    x2 = x * x
    t = jnp.tanh(c * (x + 0.044715 * (x2 * x)))
    half = 0.5 * (1.0 + t)
    return x * half, half + 0.5 * x * (1.0 - t * t) * c * (1.0 + 3 * 0.044715 * x2)


def _rstd_rows(x):
    return lax.rsqrt(jnp.mean(x * x, axis=-1, keepdims=True) + EPS)


def _norm_mod_matmul(x, ng, sc, sh, w, bias, out_dtype, ts, tn, name, planes=1):
    s, d = x.shape
    ns = w.shape[-1]
    n = w.shape[0] * ns if w.ndim == 3 else ns
    nc = n // planes

    def body(x_ref, ng_ref, sc_ref, sh_ref, w_ref, b_ref, o_ref, h_ref):
        xv = x_ref[...]
        h = (xv * _rstd_rows(xv) * ng_ref[...] * (1.0 + sc_ref[...]) + sh_ref[...]).astype(BF16)
        h_ref[...] = h
        for e in range(planes):
            for c0 in range(0, nc, tn):
                g0 = e * nc + c0
                wv = w_ref[g0 // ns, :, g0 % ns:g0 % ns + tn] if w.ndim == 3 else w_ref[:, g0:g0 + tn]
                val = (_dot(h, wv) + b_ref[:, g0:g0 + tn]).astype(out_dtype)
                if planes == 1:
                    o_ref[:, c0:c0 + tn] = val
                else:
                    o_ref[e, :, c0:c0 + tn] = val

    vec = pl.BlockSpec((1, d), lambda i: (0, 0))
    w_spec = (pl.BlockSpec(w.shape, lambda i: (0, 0, 0)) if w.ndim == 3 else pl.BlockSpec((d, n), lambda i: (0, 0)))
    if planes == 1:
        o_spec, o_shape = pl.BlockSpec((ts, n), lambda i: (i, 0)), (s, n)
    else:
        o_spec, o_shape = pl.BlockSpec((planes, ts, nc), lambda i: (0, i, 0)), (planes, s, nc)
    return pl.pallas_call(
        body, name=name, grid=(s // ts,),
        in_specs=[pl.BlockSpec((ts, d), lambda i: (i, 0)), vec, vec, vec, w_spec,
                  pl.BlockSpec((1, n), lambda i: (0, 0))],
        out_specs=[o_spec, pl.BlockSpec((ts, d), lambda i: (i, 0))],
        out_shape=[jax.ShapeDtypeStruct(o_shape, out_dtype), jax.ShapeDtypeStruct((s, d), BF16)],
        compiler_params=_cparams(("arbitrary",)),
    )(x, ng, sc, sh, w, bias)


def _matmul(a, b, ta, tb, tm, tn, tk, out_dtype, name, out_parts=1):
    if a.ndim == 3:
        m, k = a.shape[1], a.shape[0] * a.shape[2]
        nkp = a.shape[2] // tk
    else:
        m, k = (a.shape[1], a.shape[0]) if ta else a.shape
    if b.ndim == 3:
        n = b.shape[1] if tb else b.shape[0] * b.shape[2]
        nbp = b.shape[2] // (tk if tb else tn)
    else:
        n = b.shape[0] if tb else b.shape[1]
    nk = k // tk
    nop = n // out_parts // tn
    dims = (((0,) if ta else (1,), (1,) if tb else (0,)), ((), ()))

    def body(a_ref, b_ref, o_ref, acc):
        kk = pl.program_id(2)

        @pl.when(kk == 0)
        def _():
            acc[...] = jnp.zeros_like(acc)
        acc[...] += _dot(a_ref[...], b_ref[...], dims)

        @pl.when(kk == nk - 1)
        def _():
            o_ref[...] = acc[...].astype(out_dtype)

    if a.ndim == 3:
        a_spec = pl.BlockSpec((None, tm, tk), lambda i, j, kk: (kk // nkp, i, kk % nkp))
    else:
        a_spec = (pl.BlockSpec((tk, tm), lambda i, j, kk: (kk, i)) if ta
                  else pl.BlockSpec((tm, tk), lambda i, j, kk: (i, kk)))
    if b.ndim == 3 and tb:
        b_spec = pl.BlockSpec((None, tn, tk), lambda i, j, kk: (kk // nbp, j, kk % nbp))
    elif b.ndim == 3:
        b_spec = pl.BlockSpec((None, tk, tn), lambda i, j, kk: (j // nbp, kk, j % nbp))
    else:
        b_spec = (pl.BlockSpec((tn, tk), lambda i, j, kk: (j, kk)) if tb
                  else pl.BlockSpec((tk, tn), lambda i, j, kk: (kk, j)))
    if out_parts > 1:
        o_spec = pl.BlockSpec((None, tm, tn), lambda i, j, kk: (j // nop, i, j % nop))
        o_shape = (out_parts, m, n // out_parts)
    else:
        o_spec, o_shape = pl.BlockSpec((tm, tn), lambda i, j, kk: (i, j)), (m, n)
    return pl.pallas_call(
        body, name=name, grid=(m // tm, n // tn, nk),
        in_specs=[a_spec, b_spec],
        out_specs=o_spec,
        out_shape=jax.ShapeDtypeStruct(o_shape, out_dtype),
        scratch_shapes=[pltpu.VMEM((tm, tn), F32)],
        compiler_params=_cparams(("arbitrary", "arbitrary", "arbitrary")),
    )(a, b)


def _matmul_wt(a, w, tn, tk, out_dtype, ts, name):
    s = a.shape[-2]
    ka, kw = a.shape[-1], w.shape[-1]
    k = ka * (a.shape[0] if a.ndim == 3 else 1)
    n = w.shape[-2]

    def body(a_ref, w_ref, o_ref):
        for n0 in range(0, n, tn):
            acc = None
            for g0 in range(0, k, tk):
                av = a_ref[g0 // ka, :, g0 % ka:g0 % ka + tk] if a.ndim == 3 else a_ref[:, g0:g0 + tk]
                wv = (w_ref[g0 // kw, n0:n0 + tn, g0 % kw:g0 % kw + tk] if w.ndim == 3
                      else w_ref[n0:n0 + tn, g0:g0 + tk])
                part = _dot_nt(av, wv)
                acc = part if acc is None else acc + part
            o_ref[:, n0:n0 + tn] = acc.astype(out_dtype)

    a_spec = (pl.BlockSpec((a.shape[0], ts, ka), lambda i: (0, i, 0)) if a.ndim == 3
              else pl.BlockSpec((ts, ka), lambda i: (i, 0)))
    w_spec = pl.BlockSpec(w.shape, (lambda i: (0, 0, 0)) if w.ndim == 3 else (lambda i: (0, 0)))
    return pl.pallas_call(
        body, name=name, grid=(s // ts,),
        in_specs=[a_spec, w_spec], out_specs=pl.BlockSpec((ts, n), lambda i: (i, 0)),
        out_shape=jax.ShapeDtypeStruct((s, n), out_dtype),
        compiler_params=_cparams(("arbitrary",)),
    )(a, w)


def _lane(shape):
    return lax.broadcasted_iota(jnp.int32, shape, 1)


def _pair_norm(x, gain2, bones):
    msq = _exact_dot_r(x * x, bones) * (1.0 / DH)
    r = lax.rsqrt(msq + EPS)
    xh = x * r
    return xh * gain2, xh, r


def _fox_post(proj, qg2, kg2, bf, ts, name):
    s = proj.shape[0]

    def body(p_ref, qg_ref, kg_ref, bf_ref, q_ref, k_ref, v_ref, carry):
        @pl.when(pl.program_id(0) == 0)
        def _():
            carry[...] = jnp.zeros_like(carry)
        lane = _lane((ts, LANES))
        bones = _head_block_ones()
        xf = p_ref[:, 4 * D:4 * D + LANES] + bf_ref[...]
        logf = jnp.minimum(xf, 0.0) - jnp.log(1.0 + jnp.exp(-jnp.abs(xf)))
        logf = jnp.where(lane < H, logf, 0.0)
        rr = lax.broadcasted_iota(jnp.int32, (ts, ts), 0)
        cc = lax.broadcasted_iota(jnp.int32, (ts, ts), 1)
        ltri = (cc <= rr).astype(BF16)
        fcum = _exact_dot(ltri, logf) + carry[0:1, :]
        carry[0:1, :] = fcum[ts - 1:ts, :]
        fhi, fmid, flo = _split3(fcum * LOG2E)
        fhi, fmid, flo = fhi.astype(F32), fmid.astype(F32), flo.astype(F32)
        one_q = ((lane >= L_NF) & (lane < L_NF + 3)).astype(F32)
        one_k = (((lane >= L_F) & (lane < L_F + 3)) | ((lane >= L_LSE) & (lane < L_LSE + 3))).astype(F32)
        one_v = ((lane >= L_F) & (lane < L_F + 3)).astype(F32)
        for p in range(NP):
            qn, _, _ = _pair_norm(p_ref[:, p * LANES:(p + 1) * LANES], qg_ref[...], bones)
            kn, _, _ = _pair_norm(p_ref[:, D + p * LANES:D + (p + 1) * LANES], kg_ref[...], bones)
            vv = p_ref[:, 2 * D + p * LANES:2 * D + (p + 1) * LANES]
            qn = qn * (SCALE * LOG2E)
            for e in range(2):
                h = 2 * p + e
                if e == 1:
                    qe, ke, ve = (pltpu.roll(t, DH, axis=1) for t in (qn, kn, vv))
                else:
                    qe, ke, ve = qn, kn, vv
                f0, f1, f2 = fhi[:, h:h + 1], fmid[:, h:h + 1], flo[:, h:h + 1]
                fq = jnp.where(lane == L_F, f0, jnp.where(lane == L_F + 1, f1, jnp.where(lane == L_F + 2, f2, one_q)))
                fk = jnp.where(lane == L_NF, -f0, jnp.where(lane == L_NF + 1, -f1, jnp.where(lane == L_NF + 2, -f2, one_k)))
                q_ref[h] = jnp.where(lane < DH, qe, fq).astype(BF16)
                k_ref[h] = jnp.where(lane < DH, ke, fk).astype(BF16)
                v_ref[h] = jnp.where(lane < DH, ve, one_v).astype(BF16)

    hs = pl.BlockSpec((H, ts, LANES), lambda i: (0, i, 0))
    vec = pl.BlockSpec((1, LANES), lambda i: (0, 0))
    shp = jax.ShapeDtypeStruct((H, s, LANES), BF16)
    return pl.pallas_call(
        body, name=name, grid=(s // ts,),
        in_specs=[pl.BlockSpec((ts, FOX_NP), lambda i: (i, 0)), vec, vec, vec],
        out_specs=[hs, hs, hs], out_shape=[shp, shp, shp],
        scratch_shapes=[pltpu.VMEM((8, LANES), F32)],
        compiler_params=_cparams(("arbitrary",)),
    )(proj, qg2, kg2, bf)


def _gather_copies(p_refs, o_refs, send_sems, recv_sems):
    x, y, c = _mesh_pos()
    me = 2 * x + y
    sends, arrivals = [], []
    for a, (p_ref, o_ref) in enumerate(zip(p_refs, o_refs)):
        rh = p_ref.shape[0] // 2
        for k, chip in enumerate(_other_chips(x, y)):
            ci = 2 * chip[0] + chip[1]
            for cc in range(2):
                sends.append(_remote(p_ref.at[pl.ds(c * rh, rh), :], o_ref.at[me, pl.ds(c * rh, rh), :],
                                     send_sems.at[6 * a + 2 * k + cc], recv_sems.at[6 * a + 2 * k + c], (*chip, cc)))
                arrivals.append(_remote(o_ref.at[ci, pl.ds(cc * rh, rh), :], o_ref.at[ci, pl.ds(cc * rh, rh), :],
                                        send_sems.at[6 * a + 2 * k + cc], recv_sems.at[6 * a + 2 * k + cc],
                                        (*chip, cc)))
    return sends, arrivals


def _attn_fwd(qa, ka, va, tq, name, shards=()):
    s = qa.shape[1]
    nq = s // tq
    na = len(shards)
    hps = HPS_FWD

    def body(*refs):
        q_ref, k_ref, v_ref = refs[:3]
        p_refs = refs[3:3 + na]
        o_ref, ql_ref = refs[3 + na:5 + na]
        g_refs = refs[5 + na:5 + 2 * na]
        i = pl.program_id(1)
        if na:
            send_sems, recv_sems = refs[5 + 2 * na:]

            @pl.when((pl.program_id(0) == 0) & (i == 0))
            def _():
                for cp in _gather_copies(p_refs, g_refs, send_sems, recv_sems)[0]:
                    cp.start()
        lane = _lane((tq, LANES))
        qs_ = [q_ref[e] for e in range(hps)]

        tk = min(TK_FWD, tq)
        nks = tq // tk

        def step(j, carry, diag=None):
            off = pl.multiple_of(j * tk, tk)
            scs = [_dot_nt(qs_[e], k_ref[e, pl.ds(off, tk), :]) for e in range(hps)]
            probs = []
            for e in range(hps):
                m, sc = carry[e][0], scs[e]
                if diag is not None:
                    rr = lax.broadcasted_iota(jnp.int32, (tq, tk), 0)
                    cc = lax.broadcasted_iota(jnp.int32, (tq, tk), 1) + diag * tk
                    sc = jnp.where(cc <= rr, sc, -jnp.inf)
                m_new = jnp.maximum(m, jnp.max(sc, axis=-1, keepdims=True))
                probs.append((m_new, jnp.exp2(sc - m_new).astype(BF16), jnp.exp2(m - m_new)))
            return tuple((m_new, carry[e][1] * alpha + _dot(pr, v_ref[e, pl.ds(off, tk), :]))
                         for e, (m_new, pr, alpha) in enumerate(probs))

        one = (jnp.full((tq, 1), -jnp.inf, F32), jnp.zeros((tq, LANES), F32))
        carry = lax.fori_loop(0, i * nks, step, (one,) * hps)
        for r in range(nks):
            carry = step(i * nks + r, carry, diag=r)
        outs = []
        for e in range(hps):
            m, acc = carry[e]
            l = acc[:, L_F:L_F + 1]
            outs.append(acc / l)
            lse = m + jnp.log2(l)
            h0, h1, h2 = _split3(-lse)
            ql = jnp.where(lane == L_LSE, h0.astype(F32),
                           jnp.where(lane == L_LSE + 1, h1.astype(F32),
                                     jnp.where(lane == L_LSE + 2, h2.astype(F32), qs_[e].astype(F32))))
            ql_ref[e] = ql.astype(BF16)
        for e in range(0, hps, 2):
            o_ref[:, e * DH:(e + 2) * DH] = jnp.where(lane < DH, outs[e], pltpu.roll(outs[e + 1], DH, axis=1))
        if na:
            @pl.when((pl.program_id(0) == H // hps - 1) & (i == nq - 1))
            def _():
                sends, arrivals = _gather_copies(p_refs, g_refs, send_sems, recv_sems)
                for cp in arrivals:
                    cp.wait_recv()
                for cp in sends:
                    cp.wait_send()

    res = pl.BlockSpec((hps, s, LANES), lambda p, i: (p, 0, 0))
    qs = pl.BlockSpec((hps, tq, LANES), lambda p, i: (p, i, 0))
    outs = pl.pallas_call(
        body, name=name, grid=(H // hps, nq),
        in_specs=[qs, res, res] + [HBM_SPEC] * na,
        out_specs=[pl.BlockSpec((tq, hps * DH), lambda p, i: (i, p)), qs] + [HBM_SPEC] * na,
        out_shape=[jax.ShapeDtypeStruct((s, D), F32), jax.ShapeDtypeStruct((H, s, LANES), BF16)]
        + [jax.ShapeDtypeStruct((N_CHIP,) + p.shape, p.dtype) for p in shards],
        scratch_shapes=[pltpu.SemaphoreType.DMA((6 * na,))] * 2 if na else [],
        compiler_params=_cparams(("arbitrary", "arbitrary")),
    )(qa, ka, va, *shards)
    return outs[0], outs[1], list(outs[2:])


def _chip_exchange_copies(cs_refs, o_refs, send_sems, recv_sems):
    x, y, c = _mesh_pos()
    cps = []
    for a, (cs_ref, o_ref) in enumerate(zip(cs_refs, o_refs)):
        for k, chip in enumerate(_other_chips(x, y)):
            ci = 2 * chip[0] + chip[1]
            cps.append(_remote(cs_ref.at[ci], o_ref.at[k], send_sems.at[3 * a + k], recv_sems.at[3 * a + k],
                               (*chip, c)))
    return cps


def _attn_bwd(ql, ka, va, doa, tq, name, css=()):
    s = ql.shape[1]
    nq = s // tq
    na = len(css)

    def body(*refs):
        q_ref, k_ref, v_ref, do_ref = refs[:4]
        cs_refs = refs[4:4 + na]
        dqo_ref, dk_ref, dv_ref = refs[4 + na:7 + na]
        r_refs = refs[7 + na:7 + 2 * na]
        dq_ref = refs[7 + 2 * na]
        j = pl.program_id(1)
        if na:
            send_sems, recv_sems = refs[8 + 2 * na:]

            @pl.when((pl.program_id(0) == 0) & (j == 0))
            def _():
                for cp in _chip_exchange_copies(cs_refs, r_refs, send_sems, recv_sems):
                    cp.start()

        @pl.when(j == 0)
        def _():
            dq_ref[...] = jnp.zeros_like(dq_ref)
        lane = _lane((tq, LANES))
        kbs = [k_ref[0], k_ref[1]]
        vbs = [v_ref[0], v_ref[1]]

        def step(i, carry, masked):
            ioff = pl.multiple_of(i * tq, tq)
            qbs = [q_ref[e, pl.ds(ioff, tq), :] for e in range(2)]
            dobs = [do_ref[e, pl.ds(ioff, tq), :] for e in range(2)]
            scs = [_dot_nt(qbs[e], kbs[e]) for e in range(2)]
            dps = [_dot_nt(dobs[e], vbs[e]) for e in range(2)]
            prs, dss = [], []
            for e in range(2):
                pr = jnp.exp2(scs[e])
                if masked:
                    rr = lax.broadcasted_iota(jnp.int32, (tq, tq), 0)
                    cc = lax.broadcasted_iota(jnp.int32, (tq, tq), 1)
                    pr = jnp.where(cc <= rr, pr, 0.0)
                dss.append((pr * dps[e]).astype(BF16))
                prs.append(pr.astype(BF16))
            new = []
            for e in range(2):
                dk, dv = carry[e]
                dv = dv + _dot_tn(prs[e], dobs[e])
                dk = dk + _dot_tn(dss[e], qbs[e])
                dq_ref[e, pl.ds(ioff, tq), :] += _dot(dss[e], kbs[e])
                new.append((dk, dv))
            return tuple(new)

        zero = jnp.zeros((tq, LANES), F32)
        carry = step(j, ((zero, zero), (zero, zero)), True)
        carry = lax.fori_loop(j + 1, nq, functools.partial(step, masked=False), carry)
        for e in range(2):
            dk, dv = carry[e]
            col = dk[:, L_NF:L_NF + 1]
            hi = col.astype(BF16).astype(F32)
            dk_ref[e] = jnp.where(lane == L_NF, hi, jnp.where(lane == L_NF + 1, col - hi, dk)).astype(BF16)
            dv_ref[e] = dv.astype(BF16)

        @pl.when(j == nq - 1)
        def _():
            lane_s = _lane((s, LANES))
            for e in range(2):
                dq = dq_ref[e]
                col = dq[:, L_F:L_F + 1]
                hi = col.astype(BF16).astype(F32)
                dqo_ref[e] = jnp.where(lane_s == L_F, hi, jnp.where(lane_s == L_F + 1, col - hi, dq)).astype(BF16)
        if na:
            @pl.when((pl.program_id(0) == NP - 1) & (j == nq - 1))
            def _():
                for cp in _chip_exchange_copies(cs_refs, r_refs, send_sems, recv_sems):
                    cp.wait()

    res = pl.BlockSpec((2, s, LANES), lambda p, j: (p, 0, 0))
    tile = pl.BlockSpec((2, tq, LANES), lambda p, j: (p, j, 0))
    shp = jax.ShapeDtypeStruct((H, s, LANES), BF16)
    outs = pl.pallas_call(
        body, name=name, grid=(NP, nq),
        in_specs=[res, tile, tile, res] + [HBM_SPEC] * na, out_specs=[res, tile, tile] + [HBM_SPEC] * na,
        out_shape=[shp, shp, shp] + [jax.ShapeDtypeStruct((3,) + cs.shape[1:], cs.dtype) for cs in css],
        scratch_shapes=[pltpu.VMEM((2, s, LANES), F32)] + ([pltpu.SemaphoreType.DMA((3 * na,))] * 2 if na else []),
        compiler_params=_cparams(("arbitrary", "arbitrary")),
    )(ql, ka, va, doa, *css)
    return outs[0], outs[1], outs[2], list(outs[3:])


def _gate_out(att, proj, w, xin, g, ts, name):
    s = att.shape[0]

    def body(a_ref, o_ref, w_ref, x_ref, g_ref, xo_ref, y_ref, gt_ref):
        gated = (a_ref[...] * _sigmoid(o_ref[...])).astype(BF16)
        gt_ref[...] = gated
        y = _dot(gated, w_ref[...])
        xo_ref[...] = x_ref[...] + g_ref[...] * y
        y_ref[...] = y.astype(BF16)

    row = pl.BlockSpec((ts, D), lambda i: (i, 0))
    return pl.pallas_call(
        body, name=name, grid=(s // ts,),
        in_specs=[row, pl.BlockSpec((ts, D), lambda i: (i, 3)), pl.BlockSpec((D, D), lambda i: (0, 0)), row,
                  pl.BlockSpec((1, D), lambda i: (0, 0))],
        out_specs=[row, row, row],
        out_shape=[jax.ShapeDtypeStruct((s, D), F32), jax.ShapeDtypeStruct((s, D), BF16),
                   jax.ShapeDtypeStruct((s, D), BF16)],
        compiler_params=_cparams(("arbitrary",)),
    )(att, proj, w, xin, g)


def _attn_bwd_prep(dy, w_out, att, proj, ts, name):
    s = att.shape[0]

    def body(dy_ref, w_ref, a_ref, o_ref, doa_ref, dop_ref):
        lane = _lane((ts, LANES))
        bones = _head_block_ones()
        dgv = _dot_nt(dy_ref[...], w_ref[...])
        for p in range(NP):
            sl = slice(p * LANES, (p + 1) * LANES)
            dg, a = dgv[:, sl], a_ref[:, sl]
            sig = _sigmoid(o_ref[:, sl])
            datt = dg * sig
            dop_ref[:, sl] = (dg * a * sig * (1.0 - sig)).astype(BF16)
            delta = _exact_dot_r(datt * a, bones)
            for e in range(2):
                de, dl = (datt, delta) if e == 0 else (pltpu.roll(datt, DH, axis=1), pltpu.roll(delta, DH, axis=1))
                h0, h1, h2 = _split3(-dl[:, 0:1])
                aug = jnp.where(lane == L_F, h0.astype(F32),
                                jnp.where(lane == L_F + 1, h1.astype(F32),
                                          jnp.where(lane == L_F + 2, h2.astype(F32), 0.0)))
                doa_ref[2 * p + e] = jnp.where(lane < DH, de, aug).astype(BF16)

    row = pl.BlockSpec((ts, D), lambda i: (i, 0))
    return pl.pallas_call(
        body, name=name, grid=(s // ts,),
        in_specs=[row, pl.BlockSpec((D, D), lambda i: (0, 0)), row, pl.BlockSpec((ts, D), lambda i: (i, 3))],
        out_specs=[pl.BlockSpec((H, ts, LANES), lambda i: (0, i, 0)), row],
        out_shape=[jax.ShapeDtypeStruct((H, s, LANES), BF16), jax.ShapeDtypeStruct((s, D), BF16)],
        compiler_params=_cparams(("arbitrary",)),
    )(dy, w_out, att, proj)


def _fox_post_bwd(proj, dqa, dka, dva, dop, qg2, kg2, bf, ts, name):
    s = proj.shape[0]
    nt = s // ts

    def body(p_ref, dq_ref, dk_ref, dv_ref, dop_ref, qg_ref, kg_ref, bf_ref, o_ref, red_ref, carry):
        @pl.when(pl.program_id(0) == 0)
        def _():
            carry[...] = jnp.zeros_like(carry)
            red_ref[...] = jnp.zeros_like(red_ref)
        lane = _lane((ts, LANES))
        bones = _head_block_ones()
        d_f = jnp.zeros((ts, LANES), F32)
        dqg = jnp.zeros((1, LANES), F32)
        dkg = jnp.zeros((1, LANES), F32)
        for p in range(NP):
            heads = [[ref[2 * p + e].astype(F32) for e in range(2)] for ref in (dq_ref, dk_ref, dv_ref)]
            pair = [jnp.where(lane < DH, a, pltpu.roll(b, DH, axis=1)) for a, b in heads]
            for e in range(2):
                dqe, dke = heads[0][e], heads[1][e]
                col = (dqe[:, L_F:L_F + 1] + dqe[:, L_F + 1:L_F + 2]
                       - dke[:, L_NF:L_NF + 1] - dke[:, L_NF + 1:L_NF + 2])
                d_f = jnp.where(lane == 2 * p + e, col, d_f)
            for idx, (g_ref, base) in enumerate(((qg_ref, 0), (kg_ref, D))):
                x = p_ref[:, base + p * LANES:base + (p + 1) * LANES]
                _, xh, r = _pair_norm(x, g_ref[...], bones)
                dn = pair[idx] * (SCALE if idx == 0 else 1.0 / LOG2E)
                t = dn * g_ref[...]
                mean_txh = _exact_dot_r(t * xh, bones) * (1.0 / DH)
                dx = r * (t - xh * mean_txh)
                o_ref[:, base + p * LANES:base + (p + 1) * LANES] = dx.astype(BF16)
                gsum = jnp.sum(dn * xh, axis=0, keepdims=True)
                if idx == 0:
                    dqg = dqg + gsum
                else:
                    dkg = dkg + gsum
            o_ref[:, 2 * D + p * LANES:2 * D + (p + 1) * LANES] = pair[2].astype(BF16)
        o_ref[:, 3 * D:4 * D] = dop_ref[...]
        rr = lax.broadcasted_iota(jnp.int32, (ts, ts), 0)
        cc = lax.broadcasted_iota(jnp.int32, (ts, ts), 1)
        utri = (cc >= rr).astype(BF16)
        dlogf = _exact_dot(utri, d_f) + carry[0:1, :]
        carry[0:1, :] = dlogf[0:1, :]
        xf = p_ref[:, 4 * D:4 * D + LANES] + bf_ref[...]
        dfl = jnp.where(lane < H, dlogf * _sigmoid(-xf), 0.0)
        o_ref[:, 4 * D:4 * D + LANES] = dfl.astype(BF16)
        red_ref[0:1, :] += dqg
        red_ref[1:2, :] += dkg
        red_ref[2:3, :] += jnp.sum(dfl, axis=0, keepdims=True)

    hs = pl.BlockSpec((H, ts, LANES), lambda i: (0, nt - 1 - i, 0))
    vec = pl.BlockSpec((1, LANES), lambda i: (0, 0))
    return pl.pallas_call(
        body, name=name, grid=(nt,),
        in_specs=[pl.BlockSpec((ts, FOX_NP), lambda i: (nt - 1 - i, 0)), hs, hs, hs,
                  pl.BlockSpec((ts, D), lambda i: (nt - 1 - i, 0)), vec, vec, vec],
        out_specs=[pl.BlockSpec((ts, FOX_NP), lambda i: (nt - 1 - i, 0)),
                   pl.BlockSpec((8, LANES), lambda i: (0, 0))],
        out_shape=[jax.ShapeDtypeStruct((s, FOX_NP), BF16), jax.ShapeDtypeStruct((8, LANES), F32)],
        scratch_shapes=[pltpu.VMEM((8, LANES), F32)],
        compiler_params=_cparams(("arbitrary",)),
    )(proj, dqa, dka, dva, dop, qg2, kg2, bf)


HALO = 16
TS = 512
TQ = 512
TR = 256
TP = 256
HPS_FWD = 4
TK_FWD = 512
TKW = 2048


def _shift_down(x, k):
    return pltpu.roll(x, k, axis=0)


def _shift_up(x, k):
    return pltpu.roll(x, x.shape[0] - k, axis=0)


def _conv_down(a, cw, cb, w, xin, gate, ts, name):
    s = a.shape[1]
    d = w.shape[1]
    hb = ts // HALO

    def body(prev_ref, a_ref, cw_ref, cb_ref, w_ref, x_ref, g_ref, o_ref, y_ref, f_ref, ap_ref):
        i = pl.program_id(0)
        acc = None
        for c in range(NGT):
            cols = slice(c * GT, (c + 1) * GT)
            both = lambda ref: jnp.concatenate([ref[0, :, cols].astype(F32), ref[1, :, cols].astype(F32)], axis=1)
            cwv, cbv = both(cw_ref), both(cb_ref)
            ext = jnp.concatenate([jnp.where(i > 0, both(prev_ref), 0.0), both(a_ref)], axis=0)
            ap = (_shift_down(ext, 2) * cwv[0:1, :] + _shift_down(ext, 1) * cwv[1:2, :]
                  + ext * cwv[2:3, :] + cbv)[HALO:, :]
            g, val = ap[:, :GT], ap[:, GT:]
            fch = (g * _sigmoid(g) * val).astype(BF16)
            f_ref[:, cols] = fch
            ap_ref[0, :, cols] = g.astype(BF16)
            ap_ref[1, :, cols] = val.astype(BF16)
            part = _dot(fch, w_ref[cols, :])
            acc = part if acc is None else acc + part
        y_ref[...] = acc.astype(BF16)
        o_ref[...] = x_ref[...] + g_ref[...] * acc

    row = pl.BlockSpec((ts, d), lambda i: (i, 0))
    planes = pl.BlockSpec((2, ts, DFF), lambda i: (0, i, 0))
    return pl.pallas_call(
        body, name=name, grid=(s // ts,),
        in_specs=[pl.BlockSpec((2, HALO, DFF), lambda i: (0, jnp.maximum(i * hb - 1, 0), 0)), planes,
                  pl.BlockSpec((2, 8, DFF), lambda i: (0, 0, 0)), pl.BlockSpec((2, 1, DFF), lambda i: (0, 0, 0)),
                  pl.BlockSpec((DFF, d), lambda i: (0, 0)), row, pl.BlockSpec((1, d), lambda i: (0, 0))],
        out_specs=[row, row, pl.BlockSpec((ts, DFF), lambda i: (i, 0)), planes],
        out_shape=[jax.ShapeDtypeStruct((s, d), F32), jax.ShapeDtypeStruct((s, d), BF16),
                   jax.ShapeDtypeStruct((s, DFF), BF16), jax.ShapeDtypeStruct((2, s, DFF), BF16)],
        compiler_params=_cparams(("arbitrary",)),
    )(a, a, cw, cb, w, xin, gate)


def _down_bwd_conv(dy, w, a, ap, cw, ts, name):
    s, d = dy.shape
    hb = ts // HALO
    nt = s // ts
    nhb = s // HALO

    def body(dy_ref, dyn_ref, w_ref, a_ref, ap_ref, apn_ref, cw_ref, da_ref, red_ref):
        i = pl.program_id(0)

        @pl.when(i == 0)
        def _():
            red_ref[...] = jnp.zeros_like(red_ref)
        dyn = jnp.where(i < nt - 1, dyn_ref[...], jnp.zeros_like(dyn_ref))
        dye = jnp.concatenate([dy_ref[...], dyn], axis=0)
        for c in range(NGT):
            cols = slice(c * GT, (c + 1) * GT)
            both = lambda ref: jnp.concatenate([ref[0, :, cols].astype(F32), ref[1, :, cols].astype(F32)], axis=1)
            cwv = both(cw_ref)
            dfe = _dot_nt(dye, w_ref[cols, :])
            apv = jnp.concatenate([both(ap_ref), both(apn_ref)], axis=0)
            g, val = apv[:, :GT], apv[:, GT:]
            sg = _sigmoid(g)
            dap = jnp.concatenate([dfe * val * (sg * (1.0 + g * (1.0 - sg))), dfe * (g * sg)], axis=1)
            shifted = [_shift_up(dap, 2)[:ts], _shift_up(dap, 1)[:ts], dap[:ts]]
            da = shifted[0] * cwv[0:1, :] + shifted[1] * cwv[1:2, :] + shifted[2] * cwv[2:3, :]
            av = both(a_ref)
            sums = [jnp.sum(av * t, axis=0, keepdims=True) for t in shifted]
            sums.append(jnp.sum(shifted[2], axis=0, keepdims=True))
            for e in range(2):
                half = slice(e * GT, (e + 1) * GT)
                da_ref[e, :, cols] = da[:, half].astype(BF16)
                for r, sm in enumerate(sums):
                    red_ref[e, r:r + 1, cols] += sm[:, half]

    planes = pl.BlockSpec((2, ts, DFF), lambda i: (0, i, 0))
    nxt = lambda i: jnp.minimum((i + 1) * hb, nhb - 1)
    return pl.pallas_call(
        body, name=name, grid=(nt,),
        in_specs=[pl.BlockSpec((ts, d), lambda i: (i, 0)), pl.BlockSpec((HALO, d), lambda i: (nxt(i), 0)),
                  pl.BlockSpec((DFF, d), lambda i: (0, 0)), planes, planes,
                  pl.BlockSpec((2, HALO, DFF), lambda i: (0, nxt(i), 0)),
                  pl.BlockSpec((2, 8, DFF), lambda i: (0, 0, 0))],
        out_specs=[planes, pl.BlockSpec((2, 8, DFF), lambda i: (0, 0, 0))],
        out_shape=[jax.ShapeDtypeStruct((2, s, DFF), BF16), jax.ShapeDtypeStruct((2, 8, DFF), F32)],
        compiler_params=_cparams(("arbitrary",)),
    )(dy, dy, w, a, ap, ap, cw)


def _chunk_mask(transposed=False):
    t = lax.broadcasted_iota(jnp.int32, (SGB, SGB), 0) // CHUNK
    u = lax.broadcasted_iota(jnp.int32, (SGB, SGB), 1) // CHUNK
    return (t <= u) if transposed else (u <= t)


def _sgu_ln(v, gain, bias):
    mu = jnp.mean(v, axis=-1, keepdims=True)
    vc = v - mu
    rstd = lax.rsqrt(jnp.mean(vc * vc, axis=-1, keepdims=True) + EPS)
    vhat = vc * rstd
    return vhat * gain + bias, vhat, rstd


def _sgu_fwd(z, vgain, vbias, ws, bst, w_out, xin, gate, tr, name):
    s = z.shape[0]

    def body(zu_ref, zv_ref, vg_ref, vb_ref, ws_ref, bs_ref, wo_ref, x_ref, gt_ref, xo_ref, yo_ref, y_ref):
        u = _gelu(zu_ref[...].astype(F32))
        vn, _, _ = _sgu_ln(_gelu(zv_ref[...].astype(F32)), vg_ref[...], vb_ref[...])
        vn = vn.astype(BF16)
        mask = _chunk_mask()
        for g in range(SGG):
            w = jnp.where(mask, ws_ref[g], 0.0).astype(BF16)
            for b in range(tr // SGB):
                rs, cs = slice(b * SGB, (b + 1) * SGB), slice(g * SGC, (g + 1) * SGC)
                mixed = _dot(w, vn[rs, cs]) + bs_ref[:, g:g + 1]
                y_ref[rs, cs] = (u[rs, cs] * mixed).astype(BF16)
        yo = _dot(y_ref[...], wo_ref[...])
        xo_ref[...] = x_ref[...] + gt_ref[...] * yo
        yo_ref[...] = yo.astype(BF16)

    vec = pl.BlockSpec((1, SGW), lambda i: (0, 0))
    row = pl.BlockSpec((tr, D), lambda i: (i, 0))
    return pl.pallas_call(
        body, name=name, grid=(s // tr,),
        in_specs=[pl.BlockSpec((tr, SGW), lambda i: (i, 0)), pl.BlockSpec((tr, SGW), lambda i: (i, 1)),
                  vec, vec, pl.BlockSpec((SGG, SGB, SGB), lambda i: (0, 0, 0)),
                  pl.BlockSpec((SGB, LANES), lambda i: (0, 0)), pl.BlockSpec((SGW, D), lambda i: (0, 0)), row,
                  pl.BlockSpec((1, D), lambda i: (0, 0))],
        out_specs=[row, row, pl.BlockSpec((tr, SGW), lambda i: (i, 0))],
        out_shape=[jax.ShapeDtypeStruct((s, D), F32), jax.ShapeDtypeStruct((s, D), BF16),
                   jax.ShapeDtypeStruct((s, SGW), BF16)],
        compiler_params=_cparams(("arbitrary",)),
    )(z, z, vgain, vbias, ws, bst, w_out, xin, gate)


def _sgu_bwd(z, dy, vgain, vbias, ws, wst, bst, tr, name):
    s = z.shape[0]

    def body(zu_ref, zv_ref, dy_ref, vg_ref, vb_ref, ws_ref, wst_ref, bs_ref,
             dz_ref, rb_ref, rv_ref, dws_ref, dbs_ref, dvn_s):
        @pl.when(pl.program_id(0) == 0)
        def _():
            rb_ref[...] = jnp.zeros_like(rb_ref)
            rv_ref[...] = jnp.zeros_like(rv_ref)
            dws_ref[...] = jnp.zeros_like(dws_ref)
            dbs_ref[...] = jnp.zeros_like(dbs_ref)
        zu = zu_ref[...].astype(F32)
        zv = zv_ref[...].astype(F32)
        u, gu = _gelu_and_grad(zu)
        v, gv = _gelu_and_grad(zv)
        vn, vhat, rstd = _sgu_ln(v, vg_ref[...], vb_ref[...])
        vnb = vn.astype(BF16)
        dyv = dy_ref[...].astype(F32)
        dmix = (dyv * u).astype(BF16)
        mask = _chunk_mask()
        mask_t = _chunk_mask(transposed=True)
        lane = _lane((SGB, LANES))
        dbs = jnp.zeros((SGB, LANES), F32)
        for g in range(SGG):
            w = jnp.where(mask, ws_ref[g], 0.0).astype(BF16)
            wt = jnp.where(mask_t, wst_ref[g], 0.0).astype(BF16)
            dw = jnp.zeros((SGB, SGB), F32)
            for b in range(tr // SGB):
                rs, cs = slice(b * SGB, (b + 1) * SGB), slice(g * SGC, (g + 1) * SGC)
                mixed = _dot(w, vnb[rs, cs]) + bs_ref[:, g:g + 1]
                dz_ref[rs, cs] = (dyv[rs, cs] * mixed * gu[rs, cs]).astype(BF16)
                dm = dmix[rs, cs]
                dw = dw + _dot_nt(dm, vnb[rs, cs])
                dbs = dbs + jnp.where(lane == g, jnp.sum(dm.astype(F32), axis=-1, keepdims=True), 0.0)
                dvn_s[rs, cs] = _dot(wt, dm)
            dws_ref[g] += jnp.where(mask, dw, 0.0)
        dbs_ref[...] += dbs
        dvn = dvn_s[...]
        rv_ref[0:1, :] += jnp.sum(dvn * vhat, axis=0, keepdims=True)
        rv_ref[1:2, :] += jnp.sum(dvn, axis=0, keepdims=True)
        dvh = dvn * vg_ref[...]
        dv = rstd * (dvh - jnp.mean(dvh, axis=-1, keepdims=True)
                     - vhat * jnp.mean(dvh * vhat, axis=-1, keepdims=True))
        dz_ref[:, SGW:] = (dv * gv).astype(BF16)
        dzf = dz_ref[...].astype(F32)
        rb_ref[0:1, :] += jnp.sum(dzf, axis=0, keepdims=True)

    vec = pl.BlockSpec((1, SGW), lambda i: (0, 0))
    wsp = pl.BlockSpec((SGG, SGB, SGB), lambda i: (0, 0, 0))
    return pl.pallas_call(
        body, name=name, grid=(s // tr,),
        in_specs=[pl.BlockSpec((tr, SGW), lambda i: (i, 0)), pl.BlockSpec((tr, SGW), lambda i: (i, 1)),
                  pl.BlockSpec((tr, SGW), lambda i: (i, 0)), vec, vec, wsp, wsp,
                  pl.BlockSpec((SGB, LANES), lambda i: (0, 0))],
        out_specs=[pl.BlockSpec((tr, 2 * SGW), lambda i: (i, 0)),
                   pl.BlockSpec((8, 2 * SGW), lambda i: (0, 0)),
                   pl.BlockSpec((8, SGW), lambda i: (0, 0)), wsp,
                   pl.BlockSpec((SGB, LANES), lambda i: (0, 0))],
        out_shape=[jax.ShapeDtypeStruct((s, 2 * SGW), BF16), jax.ShapeDtypeStruct((8, 2 * SGW), F32),
                   jax.ShapeDtypeStruct((8, SGW), F32), jax.ShapeDtypeStruct((SGG, SGB, SGB), F32),
                   jax.ShapeDtypeStruct((SGB, LANES), F32)],
        scratch_shapes=[pltpu.VMEM((tr, SGW), F32)],
        compiler_params=_cparams(("arbitrary",)),
    )(z, z, dy, vgain, vbias, ws, wst, bst)


def _final_loss(x, fg, tgt, gprev, yprev, ts, name):
    s, d = x.shape

    def body(x_ref, fg_ref, t_ref, g_ref, y_ref, l_ref, dx_ref, dy_ref, red_ref):
        @pl.when(pl.program_id(0) == 0)
        def _():
            l_ref[...] = jnp.zeros_like(l_ref)
            red_ref[...] = jnp.zeros_like(red_ref)
        xv = x_ref[...]
        r = _rstd_rows(xv)
        xh = xv * r
        err = xh * fg_ref[...] - t_ref[...]
        l_ref[...] += 0.5 * jnp.sum(jnp.mean(err * err, axis=-1, keepdims=True))
        dyo = err * (1.0 / d)
        dxh = dyo * fg_ref[...]
        dx = r * (dxh - xh * jnp.mean(dxh * xh, axis=-1, keepdims=True))
        dx_ref[...] = dx
        dy_ref[...] = (dx * g_ref[...]).astype(BF16)
        red_ref[0:1, :] += jnp.sum(dyo * xh, axis=0, keepdims=True)
        red_ref[1:2, :] += jnp.sum(dx * y_ref[...].astype(F32), axis=0, keepdims=True)

    row = pl.BlockSpec((ts, d), lambda i: (i, 0))
    vec = pl.BlockSpec((1, d), lambda i: (0, 0))
    return pl.pallas_call(
        body, name=name, grid=(s // ts,),
        in_specs=[row, vec, row, vec, row],
        out_specs=[pl.BlockSpec((8, LANES), lambda i: (0, 0)), row, row, pl.BlockSpec((8, d), lambda i: (0, 0))],
        out_shape=[jax.ShapeDtypeStruct((8, LANES), F32), jax.ShapeDtypeStruct((s, d), F32),
                   jax.ShapeDtypeStruct((s, d), BF16), jax.ShapeDtypeStruct((8, d), F32)],
        compiler_params=_cparams(("arbitrary",)),
    )(x, fg, tgt, gprev, yprev)


def _norm_bwd(xin, dh, dxout, ng, sc, gprev, yprev, ts, name):
    s, d = xin.shape
    has_prev = gprev is not None
    fused = isinstance(dh, tuple)
    if fused:
        a, w, tk = dh
        ka, kw = a.shape[-1], w.shape[-1]
        k = ka * (a.shape[0] if a.ndim == 3 else 1)

    def body(*refs):
        if fused:
            x_ref, a_ref, w_ref, dxo_ref, ng_ref, sc_ref = refs[:6]
            rest = refs[6:]
        else:
            x_ref, dh_ref, dxo_ref, ng_ref, sc_ref = refs[:5]
            rest = refs[5:]
        if has_prev:
            g_ref, y_ref, dx_ref, dy_ref, red_ref = rest
        else:
            dx_ref, red_ref = rest

        @pl.when(pl.program_id(0) == 0)
        def _():
            red_ref[...] = jnp.zeros_like(red_ref)
        if fused:
            dhv = None
            for g0 in range(0, k, tk):
                av = a_ref[g0 // ka, :, g0 % ka:g0 % ka + tk] if a.ndim == 3 else a_ref[:, g0:g0 + tk]
                wv = w_ref[g0 // kw, :, g0 % kw:g0 % kw + tk] if w.ndim == 3 else w_ref[:, g0:g0 + tk]
                part = _dot_nt(av, wv)
                dhv = part if dhv is None else dhv + part
        else:
            dhv = dh_ref[...]
        xv = x_ref[...]
        r = _rstd_rows(xv)
        xh = xv * r
        dr = dhv * (1.0 + sc_ref[...])
        t = dr * ng_ref[...]
        dx = dxo_ref[...] + r * (t - xh * jnp.mean(t * xh, axis=-1, keepdims=True))
        dx_ref[...] = dx
        red_ref[0:1, :] += jnp.sum(dhv, axis=0, keepdims=True)
        red_ref[1:2, :] += jnp.sum(dhv * (xh * ng_ref[...]), axis=0, keepdims=True)
        red_ref[2:3, :] += jnp.sum(dr * xh, axis=0, keepdims=True)
        if has_prev:
            dy_ref[...] = (dx * g_ref[...]).astype(BF16)
            red_ref[3:4, :] += jnp.sum(dx * y_ref[...].astype(F32), axis=0, keepdims=True)

    row = pl.BlockSpec((ts, d), lambda i: (i, 0))
    vec = pl.BlockSpec((1, d), lambda i: (0, 0))
    red = pl.BlockSpec((8, d), lambda i: (0, 0))
    if fused:
        a_spec = (pl.BlockSpec((a.shape[0], ts, ka), lambda i: (0, i, 0)) if a.ndim == 3
                  else pl.BlockSpec((ts, ka), lambda i: (i, 0)))
        w_spec = pl.BlockSpec(w.shape, (lambda i: (0, 0, 0)) if w.ndim == 3 else (lambda i: (0, 0)))
        dh_specs, dh_args = [a_spec, w_spec], (a, w)
    else:
        dh_specs, dh_args = [row], (dh,)
    if has_prev:
        in_specs, args = [row] + dh_specs + [row, vec, vec, vec, row], (xin,) + dh_args + (dxout, ng, sc, gprev, yprev)
        out_specs = [row, row, red]
        out_shape = [jax.ShapeDtypeStruct((s, d), F32), jax.ShapeDtypeStruct((s, d), BF16),
                     jax.ShapeDtypeStruct((8, d), F32)]
    else:
        in_specs, args = [row] + dh_specs + [row, vec, vec], (xin,) + dh_args + (dxout, ng, sc)
        out_specs = [row, red]
        out_shape = [jax.ShapeDtypeStruct((s, d), F32), jax.ShapeDtypeStruct((8, d), F32)]
    return pl.pallas_call(
        body, name=name, grid=(s // ts,), in_specs=in_specs, out_specs=out_specs, out_shape=out_shape,
        compiler_params=_cparams(("arbitrary",)),
    )(*args)


def _ada_mod(c_all, ada_w, ada_b):
    nb = c_all.shape[0]
    da = ada_w.shape[2]

    def body(c_ref, w_ref, b_ref, o_ref, ca_ref):
        cv = c_ref[...]
        ca = cv * _sigmoid(cv)
        ca_ref[...] = ca
        o_ref[0] = lax.dot_general(ca, w_ref[0], (((1,), (0,)), ((), ())), precision=lax.Precision.HIGHEST,
                                   preferred_element_type=F32) + b_ref[0]

    return pl.pallas_call(
        body, name="ada_mod", grid=(2,),
        in_specs=[pl.BlockSpec((nb, D), lambda i: (0, 0)), pl.BlockSpec((1, D, da), lambda i: (i, 0, 0)),
                  pl.BlockSpec((1, 1, da), lambda i: (i, 0, 0))],
        out_specs=[pl.BlockSpec((1, nb, da), lambda i: (i, 0, 0)), pl.BlockSpec((nb, D), lambda i: (0, 0))],
        out_shape=[jax.ShapeDtypeStruct((2, nb, da), F32), jax.ShapeDtypeStruct((nb, D), F32)],
        compiler_params=_cparams(("arbitrary",)),
    )(c_all, ada_w, ada_b)


def _ada_w_grad(c_act_t, dmod):
    nb = c_act_t.shape[1]
    da = dmod.shape[2]
    tn = 512

    def body(c_ref, d_ref, o_ref):
        acc = c_ref[:, 0:1] * d_ref[0, 0:1, :]
        for b in range(1, nb):
            acc = acc + c_ref[:, b:b + 1] * d_ref[0, b:b + 1, :]
        o_ref[0] = acc

    return pl.pallas_call(
        body, name="ada_w_grad", grid=(2, da // tn),
        in_specs=[pl.BlockSpec((D, nb), lambda i, j: (0, 0)), pl.BlockSpec((1, nb, tn), lambda i, j: (i, 0, j))],
        out_specs=pl.BlockSpec((1, D, tn), lambda i, j: (i, 0, j)),
        out_shape=jax.ShapeDtypeStruct((2, D, da), F32),
        compiler_params=_cparams(("arbitrary", "arbitrary")),
    )(c_act_t, dmod)


def _conv_planes(cw, cb):
    cwp = jnp.swapaxes(cw.reshape(3, 2, DFF), 0, 1)
    return jnp.pad(cwp, ((0, 0), (0, 5), (0, 0))), cb.reshape(2, 1, DFF)


def _local_step(x, tgt, mod, wts, small, comm=None):
    wts = dict(wts)
    s = x.shape[0]
    ts, tq, tr, tp = TS, TQ, TR, TP
    tkw = min(TKW, s)
    tf = min(256, s)
    zb = lambda n: jnp.zeros((1, n), F32)
    m6 = mod.reshape(2, 6, 1, D)
    sh1, sc1, g1, sh2, sc2, g2 = ([m6[i, k] for i in range(2)] for k in range(6))
    n1g, n2g = small["norm1_g"], small["norm2_g"]
    row = lambda a, i: a[i:i + 1]

    qg2 = jnp.tile(small["fox_q_gain"], (1, 2))
    kg2 = jnp.tile(small["fox_k_gain"], (1, 2))
    bfp = jnp.pad(small["fox_b_f"], ((0, 0), (0, LANES - H)))
    proj, h1 = _norm_mod_matmul(x, row(n1g, 0), sc1[0], sh1[0], wts["fox_w_in"], zb(FOX_NP), F32, ts, 1408, "fox_in")
    qa, ka, va = _fox_post(proj, qg2, kg2, bfp, tp, "fox_post")
    att, ql, gathered = _attn_fwd(qa, ka, va, tq, "attn_fwd", shards=comm["shards"] if comm else ())
    if comm:
        wts.update(comm["make_wts"](gathered))
    x1, y0, gated = _gate_out(att, proj, wts["fox_w_out"], x, g1[0], ts, "fox_gate_out")

    def ffn_fwd(xin, i, tag):
        cw, cb = _conv_planes(small["ffn_conv_w"][i], small["ffn_conv_b"][i])
        a, h = _norm_mod_matmul(xin, row(n2g, i), sc2[i], sh2[i], wts["ffn_w_up"][i], zb(2 * DFF), BF16, ts, 1408,
                                "ffn_up" + tag, planes=2)
        xo, y, f, ap = _conv_down(a, cw, cb, wts["ffn_w_down"][i], xin, g2[i], min(256, s), "ffn_conv_down" + tag)
        return xo, (a, h, f, y, cw, ap)

    x2, ffn0 = ffn_fwd(x1, 0, "0")

    bst = jnp.pad(small["sgu_b_s"].T, ((0, 0), (0, LANES - SGG)))
    ws = small["sgu_w_s"]
    z, h3 = _norm_mod_matmul(x2, row(n1g, 1), sc1[1], sh1[1], wts["sgu_w_in"], small["sgu_b_in"], BF16, ts, 1024,
                             "sgu_in")
    x3, y1, yy = _sgu_fwd(z, small["sgu_v_gain"], small["sgu_v_bias"], ws, bst, wts["sgu_w_out"], x2, g1[1], tr,
                          "sgu_mix_out")
    x4, ffn1 = ffn_fwd(x3, 1, "1")

    lsum, dx4, dy, redf = _final_loss(x4, small["final_g"], tgt, g2[1], ffn1[3], ts, "final_loss")
    grads = {"final_g": redf[0]}
    dmod = [[None] * 6, [None] * 6]
    dmod[1][5] = redf[1]

    def ffn_bwd(dxo, dy2, xin, i, saved, gprev, yprev, tag):
        a, h, f, _, cw, ap = saved
        wd, wu = wts["ffn_w_down"][i], wts["ffn_w_up"][i]
        g_wd = _matmul(f, dy2, True, False, 1408, D, tkw, BF16, "ffn_dwdown" + tag)
        da, redc = _down_bwd_conv(dy2, wd, a, ap, cw, min(256, s), "ffn_down_bwd_conv" + tag)
        g_wu = _matmul(h, da, True, False, D, 1408, tkw, BF16, "ffn_dwup" + tag, out_parts=N_CHIP)
        outs = _norm_bwd(xin, (da, wu, 1408), dxo, row(n2g, i), sc2[i], gprev, yprev, tf, "ffn_dh_norm_bwd" + tag)
        return outs, g_wd, g_wu, redc

    (dx3, dy1, red), g_wd1, g_wu1, redc1 = ffn_bwd(dx4, dy, x3, 1, ffn1, g1[1], y1, "1")
    dmod[1][3], dmod[1][4], dn2g1, dmod[1][2] = red[0], red[1], red[2], red[3]

    g_swo = _matmul(yy, dy1, True, False, 1024, D, tkw, BF16, "sgu_dwout")
    dyy = _matmul_wt(dy1, wts["sgu_w_out"], 1024, D, BF16, ts, "sgu_dyy")
    wst = jnp.swapaxes(ws, 1, 2)
    dz, rb, rv, dws, dbst = _sgu_bwd(z, dyy, small["sgu_v_gain"], small["sgu_v_bias"], ws, wst, bst, tr, "sgu_mix_bwd")
    g_swi = _matmul(h3, dz, True, False, D, 1024, tkw, BF16, "sgu_dwin", out_parts=N_CHIP)
    dx2, dy2_0, red = _norm_bwd(x2, (dz, wts["sgu_w_in"], 1024), dx3, row(n1g, 1), sc1[1], g2[0], ffn0[3], tf,
                                "sgu_dh_norm_bwd")
    dmod[1][0], dmod[1][1], dn1g1, dmod[0][5] = red[0], red[1], red[2], red[3]

    (dx1, dy0, red), g_wd0, g_wu0, redc0 = ffn_bwd(dx2, dy2_0, x1, 0, ffn0, g1[0], y0, "0")
    dmod[0][3], dmod[0][4], dn2g0, dmod[0][2] = red[0], red[1], red[2], red[3]

    g_fwo = _matmul(gated, dy0, True, False, D, D, tkw, BF16, "fox_dwout")
    doa, dop = _attn_bwd_prep(dy0, wts["fox_w_out"], att, proj, ts, "attn_bwd_prep")
    css = comm["rs_prepare"]([g_fwo, g_swi, g_swo, g_wu0, g_wu1, g_wd0, g_wd1]) if comm else []
    dqa, dka, dva, rcvs = _attn_bwd(ql, ka, va, doa, tq, "attn_bwd", css=css)
    dproj, redx = _fox_post_bwd(proj, dqa, dka, dva, dop, qg2, kg2, bfp, tp, "fox_post_bwd")
    g_fwi = _matmul(h1, dproj, True, False, D, 1408, tkw, BF16, "fox_dwin")
    dx0, red = _norm_bwd(x, (dproj, wts["fox_w_in"], 1408), dx1, row(n1g, 0), sc1[0], None, None, tf,
                         "fox_dh_norm_bwd")
    dmod[0][0], dmod[0][1], dn1g0 = red[0], red[1], red[2]

    grads.update(
        fox_w_in=g_fwi, fox_w_out=g_fwo, sgu_w_in=g_swi, sgu_w_out=g_swo,
        ffn_w_up=[g_wu0, g_wu1], ffn_w_down=[g_wd0, g_wd1],
        fox_q_gain=redx[0, :DH] + redx[0, DH:], fox_k_gain=redx[1, :DH] + redx[1, DH:], fox_b_f=redx[2, :H],
        sgu_b_in=rb[0], sgu_v_gain=rv[0], sgu_v_bias=rv[1], sgu_w_s=dws, sgu_b_s=dbst[:, :SGG].T,
        ffn_conv_w=jnp.stack([jnp.swapaxes(r[:, 0:3], 0, 1).reshape(3, 2 * DFF) for r in (redc0, redc1)]),
        ffn_conv_b=jnp.stack([r[:, 3].reshape(2 * DFF) for r in (redc0, redc1)]),
        norm1_g=jnp.stack([dn1g0, dn1g1]), norm2_g=jnp.stack([dn2g0, dn2g1]),
    )
    dmod_arr = jnp.stack([jnp.concatenate(dmod[0]), jnp.concatenate(dmod[1])])
    return lsum[0, 0], dx0, grads, dmod_arr, (css, rcvs)


N_DEV = 8
N_CHIP = 4
HBM_SPEC = pl.BlockSpec(memory_space=pltpu.HBM)
VMEM_SPEC = pl.BlockSpec(memory_space=pltpu.VMEM)


def _mesh_pos():
    return lax.axis_index("x"), lax.axis_index("y"), lax.axis_index("c")


def _other_chips(x, y):
    return [(1 - x, y), (x, 1 - y), (1 - x, 1 - y)]


def _remote(src, dst, ssem, rsem, dev):
    return pltpu.make_async_remote_copy(src_ref=src, dst_ref=dst, send_sem=ssem, recv_sem=rsem,
                                        device_id=dev, device_id_type=MESH)


def _allgather8(xb, name):
    m_per, n = xb.shape

    def body(x_ref, out_ref, send_sems, recv_sems, local_sem):
        x, y, c = _mesh_pos()
        me, sibling = (x, y, c), (x, y, 1 - c)
        chips = _other_chips(x, y)

        def rows(px, py, pc):
            return out_ref.at[pl.ds((4 * px + 2 * py + pc) * m_per, m_per), :]

        def copy(k, block, to, src=None):
            return _remote(rows(*block) if src is None else src, rows(*block),
                           send_sems.at[k], recv_sems.at[k], to)

        mine = pltpu.make_async_copy(x_ref, rows(*me), local_sem)
        mine.start()
        first = [copy(0, me, sibling, src=x_ref)]
        first += [copy(1 + j, me, (*chip, c), src=x_ref) for j, chip in enumerate(chips)]
        for cp in first:
            cp.start()
        passed = [copy(4 + j, (*chip, c), sibling) for j, chip in enumerate(chips)]
        for j, chip in enumerate(chips):
            copy(1 + j, (*chip, c), me).wait_recv()
            passed[j].start()
        copy(0, sibling, me).wait_recv()
        for j, chip in enumerate(chips):
            copy(4 + j, (*chip, 1 - c), me).wait_recv()
        for cp in first + passed:
            cp.wait_send()
        mine.wait()

    return pl.pallas_call(
        body, name=name,
        out_shape=jax.ShapeDtypeStruct((N_DEV * m_per, n), xb.dtype),
        in_specs=[VMEM_SPEC], out_specs=VMEM_SPEC,
        scratch_shapes=[pltpu.SemaphoreType.DMA((7,)), pltpu.SemaphoreType.DMA((7,)), pltpu.SemaphoreType.DMA],
        compiler_params=pltpu.CompilerParams(vmem_limit_bytes=V7X_VMEM_LIMIT),
    )(xb)


def _gather_shards(shards, name):
    na = len(shards)

    def body(*refs):
        p_refs, o_refs = refs[:na], refs[na:2 * na]
        send_sems, recv_sems, pass_send, pass_recv = refs[2 * na:]
        x, y, c = _mesh_pos()
        me = 2 * x + y
        sibling = (x, y, 1 - c)
        chips = _other_chips(x, y)

        def half(a, ci, hf):
            rh = shards[a].shape[0] // 2
            return o_refs[a].at[ci, pl.ds(hf * rh, rh), :]

        sends = []
        for a in range(na):
            rh = shards[a].shape[0] // 2
            for k, chip in enumerate(chips):
                sends.append(_remote(p_refs[a].at[pl.ds(c * rh, rh), :], half(a, me, c),
                                     send_sems.at[3 * a + k], recv_sems.at[3 * a + k], (*chip, c)))
        for cp in sends:
            cp.start()
        passed = []
        for a in range(na):
            for k, chip in enumerate(chips):
                ci = 2 * chip[0] + chip[1]
                _remote(half(a, ci, c), half(a, ci, c), send_sems.at[3 * a + k], recv_sems.at[3 * a + k],
                        (*chip, c)).wait_recv()
                cp = _remote(half(a, ci, c), half(a, ci, c), pass_send.at[3 * a + k], pass_recv.at[3 * a + k], sibling)
                cp.start()
                passed.append(cp)
        for a in range(na):
            for k, chip in enumerate(chips):
                ci = 2 * chip[0] + chip[1]
                _remote(half(a, ci, 1 - c), half(a, ci, 1 - c), pass_send.at[3 * a + k], pass_recv.at[3 * a + k],
                        sibling).wait_recv()
        for cp in sends + passed:
            cp.wait_send()

    return pl.pallas_call(
        body, name=name,
        out_shape=[jax.ShapeDtypeStruct((N_CHIP,) + p.shape, p.dtype) for p in shards],
        in_specs=[HBM_SPEC] * na, out_specs=[HBM_SPEC] * na,
        scratch_shapes=[pltpu.SemaphoreType.DMA((3 * na,))] * 4,
    )(*shards)


def _rs_to_sibling(gs, name):
    na = len(gs)

    def body(*refs):
        g_refs, o_refs, ssems, rsems = refs[:na], refs[na:2 * na], refs[2 * na], refs[2 * na + 1]
        x, y, c = _mesh_pos()
        cps = []
        for a in range(na):
            rh = gs[a].shape[1] // 2
            cp = _remote(g_refs[a].at[:, pl.ds((1 - c) * rh, rh), :], o_refs[a], ssems.at[a], rsems.at[a],
                         (x, y, 1 - c))
            cp.start()
            cps.append(cp)
        for cp in cps:
            cp.wait()

    return pl.pallas_call(
        body, name=name,
        out_shape=[jax.ShapeDtypeStruct((g.shape[0], g.shape[1] // 2, g.shape[2]), g.dtype) for g in gs],
        in_specs=[HBM_SPEC] * na, out_specs=[HBM_SPEC] * na,
        scratch_shapes=[pltpu.SemaphoreType.DMA((na,)), pltpu.SemaphoreType.DMA((na,))],
    )(*gs)


def _rs_chip_sum(g, sib, c_arr, tr, name):
    nc, r, n = g.shape
    rh = r // 2
    g4 = g.reshape(nc, 2, rh, n)

    def body(c_ref, g_ref, s_ref, o_ref):
        o_ref[...] = (g_ref[0].astype(F32) + s_ref[...].astype(F32)).astype(BF16)

    return pl.pallas_call(
        body, name=name, out_shape=jax.ShapeDtypeStruct((nc, rh, n), BF16),
        grid_spec=pltpu.PrefetchScalarGridSpec(
            num_scalar_prefetch=1, grid=(nc, rh // tr),
            in_specs=[pl.BlockSpec((1, 1, tr, n), lambda j, i, cr: (j, cr[0], i, 0)),
                      pl.BlockSpec((1, tr, n), lambda j, i, cr: (j, i, 0))],
            out_specs=pl.BlockSpec((1, tr, n), lambda j, i, cr: (j, i, 0))),
        compiler_params=_cparams(("arbitrary", "arbitrary")),
    )(c_arr, g4, sib)


def _rs_across_chips(css, name):
    na = len(css)

    def body(*refs):
        cs_refs, o_refs, send_sems, recv_sems = refs[:na], refs[na:2 * na], refs[2 * na], refs[2 * na + 1]
        x, y, c = _mesh_pos()
        cps = []
        for a in range(na):
            for k, chip in enumerate(_other_chips(x, y)):
                ci = 2 * chip[0] + chip[1]
                cp = _remote(cs_refs[a].at[ci], o_refs[a].at[k], send_sems.at[3 * a + k], recv_sems.at[3 * a + k],
                             (*chip, c))
                cp.start()
                cps.append(cp)
        for cp in cps:
            cp.wait()

    return pl.pallas_call(
        body, name=name, out_shape=[jax.ShapeDtypeStruct((3,) + cs.shape[1:], cs.dtype) for cs in css],
        in_specs=[HBM_SPEC] * na, out_specs=[HBM_SPEC] * na,
        scratch_shapes=[pltpu.SemaphoreType.DMA((3 * na,)), pltpu.SemaphoreType.DMA((3 * na,))],
    )(*css)


def _rs_final_sum(cs, rcv, me_arr, tr, name):
    nc, rh, n = cs.shape

    def body(m_ref, c_ref, r_ref, o_ref):
        acc = c_ref[0].astype(F32)
        for k in range(3):
            acc = acc + r_ref[k].astype(F32)
        o_ref[...] = acc

    return pl.pallas_call(
        body, name=name, out_shape=jax.ShapeDtypeStruct((rh, n), F32),
        grid_spec=pltpu.PrefetchScalarGridSpec(
            num_scalar_prefetch=1, grid=(rh // tr,),
            in_specs=[pl.BlockSpec((1, tr, n), lambda i, mr: (mr[0], i, 0)),
                      pl.BlockSpec((3, tr, n), lambda i, mr: (0, i, 0))],
            out_specs=pl.BlockSpec((tr, n), lambda i, mr: (i, 0))),
        compiler_params=_cparams(("arbitrary",)),
    )(me_arr, cs, rcv)


def _rs_swap_halves(halves, name):
    na = len(halves)

    def body(*refs):
        h_refs, o_refs, ssems, rsems = refs[:na], refs[na:2 * na], refs[2 * na], refs[2 * na + 1]
        x, y, c = _mesh_pos()
        cps = []
        for a in range(na):
            cp = _remote(h_refs[a], o_refs[a], ssems.at[a], rsems.at[a], (x, y, 1 - c))
            cp.start()
            cps.append(cp)
        for cp in cps:
            cp.wait()

    return pl.pallas_call(
        body, name=name, out_shape=[jax.ShapeDtypeStruct(h.shape, h.dtype) for h in halves],
        in_specs=[HBM_SPEC] * na, out_specs=[HBM_SPEC] * na,
        scratch_shapes=[pltpu.SemaphoreType.DMA((na,)), pltpu.SemaphoreType.DMA((na,))],
    )(*halves)


def _join_columns(parts, n_out, name):
    p, k, c = parts.shape
    tr = 128

    def body(w_ref, o_ref):
        for j in range(p):
            o_ref[:, j * c:(j + 1) * c] = w_ref[j]
        o_ref[:, p * c:] = jnp.zeros((tr, n_out - p * c), parts.dtype)

    return pl.pallas_call(
        body, name=name, grid=(k // tr,),
        in_specs=[pl.BlockSpec((p, tr, c), lambda i: (0, i, 0))],
        out_specs=pl.BlockSpec((tr, n_out), lambda i: (i, 0)),
        out_shape=jax.ShapeDtypeStruct((k, n_out), parts.dtype),
        compiler_params=_cparams(("arbitrary",)),
    )(parts)


def _split_columns(g, p, c, name):
    k, n = g.shape
    tr = 128

    def body(g_ref, o_ref):
        for j in range(p):
            o_ref[j] = g_ref[:, j * c:(j + 1) * c]

    return pl.pallas_call(
        body, name=name, grid=(k // tr,),
        in_specs=[pl.BlockSpec((tr, n), lambda i: (i, 0))],
        out_specs=pl.BlockSpec((p, tr, c), lambda i: (0, i, 0)),
        out_shape=jax.ShapeDtypeStruct((p, k, c), g.dtype),
        compiler_params=_cparams(("arbitrary",)),
    )(g)


def _sum8(g, name):
    nd, r, n = g.shape

    def body(g_ref, o_ref):
        acc = g_ref[0]
        for k in range(1, nd):
            acc = acc + g_ref[k]
        o_ref[...] = acc

    return pl.pallas_call(
        body, name=name, grid=(r // 8,),
        in_specs=[pl.BlockSpec((nd, 8, n), lambda i: (0, i, 0))],
        out_specs=pl.BlockSpec((8, n), lambda i: (i, 0)),
        out_shape=jax.ShapeDtypeStruct((r, n), F32),
        compiler_params=_cparams(("arbitrary",)),
    )(g)


def _adamw(w, g, m, v, name):
    r, n = w.shape
    tr = next(t for t in (128, 64, 32, 16, 8) if r % t == 0)
    bc1 = 1.0 - ADAM_B1 ** ADAM_STEP
    bc2 = 1.0 - ADAM_B2 ** ADAM_STEP

    def body(w_ref, g_ref, m_ref, v_ref, d_ref, mo_ref, vo_ref):
        gv = g_ref[...]
        mn = ADAM_B1 * m_ref[...] + (1.0 - ADAM_B1) * gv
        vn = ADAM_B2 * v_ref[...] + (1.0 - ADAM_B2) * (gv * gv)
        d_ref[...] = -ADAM_LR * ((mn / bc1) / (jnp.sqrt(vn / bc2) + ADAM_EPS) + ADAM_WD * w_ref[...])
        mo_ref[...] = mn
        vo_ref[...] = vn

    blk = pl.BlockSpec((tr, n), lambda i: (i, 0))
    shp = jax.ShapeDtypeStruct((r, n), F32)
    return pl.pallas_call(
        body, name=name, grid=(r // tr,), in_specs=[blk] * 4, out_specs=[blk] * 3, out_shape=[shp] * 3,
        compiler_params=_cparams(("arbitrary",)),
    )(w, g, m, v)


ROW = 1024
BIG = ("fox_w_in", "fox_w_out", "sgu_w_in", "sgu_w_out", "ffn_w_up", "ffn_w_down")
SMALL_SHARDED = ("sgu_b_in", "sgu_v_gain", "sgu_v_bias", "ffn_conv_w")
SMALL_REPL = ("fox_b_f", "fox_q_gain", "fox_k_gain", "sgu_w_s", "sgu_b_s", "ffn_conv_b", "ada_b",
              "norm1_g", "norm2_g", "final_g")
WEIGHTS = ("fox_w_in", "fox_b_f", "fox_q_gain", "fox_k_gain", "fox_w_out", "sgu_w_in", "sgu_b_in", "sgu_v_gain",
           "sgu_v_bias", "sgu_w_s", "sgu_b_s", "sgu_w_out", "ffn_w_up", "ffn_conv_w", "ffn_conv_b", "ffn_w_down",
           "ada_w", "ada_b", "norm1_g", "norm2_g", "final_g")


def _rows_of(a, mult=1):
    flat = a.reshape(-1)
    rows = -(-flat.shape[0] // ROW)
    rows = -(-rows // mult) * mult
    return jnp.pad(flat, (0, rows * ROW - flat.shape[0])).reshape(rows, ROW)


def _pack(parts, mult, total=None):
    p = jnp.concatenate([_rows_of(a, mult) for a in parts], axis=0)
    if total is not None:
        p = jnp.pad(p, ((0, total - p.shape[0]), (0, 0)))
    return p


def _unpack(pack, shapes, mult):
    out, r0 = [], 0
    for shp in shapes:
        size = int(np.prod(shp))
        rows = -(-(-(-size // ROW)) // mult) * mult
        out.append(pack[r0:r0 + rows].reshape(-1)[:size].reshape(shp))
        r0 += rows
    return out


def _big_shards(t):
    return [t["fox_w_in"][0], t["fox_w_out"][0], t["sgu_w_in"][0], t["sgu_w_out"][0],
            t["ffn_w_up"][0], t["ffn_w_up"][1], t["ffn_w_down"][0], t["ffn_w_down"][1]]


def _row_tile(rows):
    return next(t for t in (512, 352, 256, 128, 64) if rows % t == 0)


def kernel(x, c, fox_w_in, fox_b_f, fox_q_gain, fox_k_gain, fox_w_out, sgu_w_in, sgu_b_in, sgu_v_gain, sgu_v_bias, sgu_w_s, sgu_b_s, sgu_w_out, ffn_w_up, ffn_conv_w, ffn_conv_b, ffn_w_down, ada_w, ada_b, norm1_g, norm2_g, final_g, loss_target, m_fox_w_in, m_fox_b_f, m_fox_q_gain, m_fox_k_gain, m_fox_w_out, m_sgu_w_in, m_sgu_b_in, m_sgu_v_gain, m_sgu_v_bias, m_sgu_w_s, m_sgu_b_s, m_sgu_w_out, m_ffn_w_up, m_ffn_conv_w, m_ffn_conv_b, m_ffn_w_down, m_ada_w, m_ada_b, m_norm1_g, m_norm2_g, m_final_g, v_fox_w_in, v_fox_b_f, v_fox_q_gain, v_fox_k_gain, v_fox_w_out, v_sgu_w_in, v_sgu_b_in, v_sgu_v_gain, v_sgu_v_bias, v_sgu_w_s, v_sgu_b_s, v_sgu_w_out, v_ffn_w_up, v_ffn_conv_w, v_ffn_conv_b, v_ffn_w_down, v_ada_w, v_ada_b, v_norm1_g, v_norm2_g, v_final_g):
    w = dict(fox_w_in=fox_w_in, fox_b_f=fox_b_f, fox_q_gain=fox_q_gain, fox_k_gain=fox_k_gain, fox_w_out=fox_w_out,
             sgu_w_in=sgu_w_in, sgu_b_in=sgu_b_in, sgu_v_gain=sgu_v_gain, sgu_v_bias=sgu_v_bias, sgu_w_s=sgu_w_s,
             sgu_b_s=sgu_b_s, sgu_w_out=sgu_w_out, ffn_w_up=ffn_w_up, ffn_conv_w=ffn_conv_w, ffn_conv_b=ffn_conv_b,
             ffn_w_down=ffn_w_down, ada_w=ada_w, ada_b=ada_b, norm1_g=norm1_g, norm2_g=norm2_g, final_g=final_g)
    mom = dict(fox_w_in=m_fox_w_in, fox_b_f=m_fox_b_f, fox_q_gain=m_fox_q_gain, fox_k_gain=m_fox_k_gain,
               fox_w_out=m_fox_w_out, sgu_w_in=m_sgu_w_in, sgu_b_in=m_sgu_b_in, sgu_v_gain=m_sgu_v_gain,
               sgu_v_bias=m_sgu_v_bias, sgu_w_s=m_sgu_w_s, sgu_b_s=m_sgu_b_s, sgu_w_out=m_sgu_w_out,
               ffn_w_up=m_ffn_w_up, ffn_conv_w=m_ffn_conv_w, ffn_conv_b=m_ffn_conv_b, ffn_w_down=m_ffn_w_down,
               ada_w=m_ada_w, ada_b=m_ada_b, norm1_g=m_norm1_g, norm2_g=m_norm2_g, final_g=m_final_g)
    var = dict(fox_w_in=v_fox_w_in, fox_b_f=v_fox_b_f, fox_q_gain=v_fox_q_gain, fox_k_gain=v_fox_k_gain,
               fox_w_out=v_fox_w_out, sgu_w_in=v_sgu_w_in, sgu_b_in=v_sgu_b_in, sgu_v_gain=v_sgu_v_gain,
               sgu_v_bias=v_sgu_v_bias, sgu_w_s=v_sgu_w_s, sgu_b_s=v_sgu_b_s, sgu_w_out=v_sgu_w_out,
               ffn_w_up=v_ffn_w_up, ffn_conv_w=v_ffn_conv_w, ffn_conv_b=v_ffn_conv_b, ffn_w_down=v_ffn_w_down,
               ada_w=v_ada_w, ada_b=v_ada_b, norm1_g=v_norm1_g, norm2_g=v_norm2_g, final_g=v_final_g)

    ax, ay, ac = _mesh_pos()
    chip = 2 * ax + ay
    dev = 2 * chip + ac

    small_shard_shapes = tuple(w[n].shape for n in SMALL_SHARDED)
    blk = _pack([c] + [w[n] for n in SMALL_SHARDED], 1, 16)
    gat = _allgather8(blk, "gather_small").reshape(N_DEV, 16, ROW)
    c_all = gat[:, 0, :]
    per_chip = [_unpack(gat[2 * j, 1:], small_shard_shapes, 1) for j in range(N_CHIP)]
    full_small = {n: jnp.concatenate([per_chip[j][i] for j in range(N_CHIP)], axis=-1)
                  for i, n in enumerate(SMALL_SHARDED)}

    mine = [a.astype(BF16) for a in _big_shards(w)]
    with_own = lambda gat, own: [lax.dynamic_update_slice(g_, m_[None], (chip, 0, 0)) for g_, m_ in zip(gat, own)]
    fwi, = with_own(_gather_shards(mine[:1], "gather_fox_w_in"), mine[:1])
    fwi_full = _join_columns(fwi, FOX_NP, "join_fox_w_in")
    wts = dict(fox_w_in=fwi_full)

    def make_wts(gathered):
        fwo, swi, swo, up0, up1, dn0, dn1 = with_own(gathered, mine[1:])
        return dict(fox_w_out=fwo.reshape(D, D), sgu_w_in=swi, sgu_w_out=swo.reshape(SGW, D),
                    ffn_w_up=[up0, up1], ffn_w_down=[dn0.reshape(DFF, D), dn1.reshape(DFF, D)])

    c_arr = jnp.reshape(ac, (1,)).astype(jnp.int32)
    me_arr = jnp.reshape(chip, (1,)).astype(jnp.int32)

    def chip_sums(glist, tag):
        sibs = _rs_to_sibling(glist, "rs_sibling" + tag)
        return [_rs_chip_sum(g_, s_, c_arr, _row_tile(s_.shape[1]), "rs_chip_sum%s%d" % (tag, a))
                for a, (g_, s_) in enumerate(zip(glist, sibs))]

    def rs_prepare(gl):
        g_fwo, g_swi, g_swo, g_wu0, g_wu1, g_wd0, g_wd1 = gl
        return chip_sums([g_fwo.reshape(N_CHIP, 256, D), g_swi, g_swo.reshape(N_CHIP, 512, D), g_wu0, g_wu1,
                          g_wd0.reshape(N_CHIP, 704, D), g_wd1.reshape(N_CHIP, 704, D)], "")

    comm = dict(shards=mine[1:], make_wts=make_wts, rs_prepare=rs_prepare)

    da = ada_w.shape[2]
    ada_b_cols = lax.dynamic_slice_in_dim(ada_b, chip * da, da, axis=1)[:, None, :]
    mod_cols, c_act = _ada_mod(c_all, ada_w, ada_b_cols)
    mod_all = _allgather8(mod_cols.reshape(-1, ROW), "gather_mod").reshape(N_DEV, 2, N_DEV, da)
    mod_mine = lax.dynamic_index_in_dim(mod_all[0::2], dev, axis=2, keepdims=False)
    mod = jnp.swapaxes(mod_mine, 0, 1).reshape(2, N_CHIP * da)

    small = dict(norm1_g=norm1_g, norm2_g=norm2_g, final_g=final_g[None], fox_q_gain=fox_q_gain,
                 fox_k_gain=fox_k_gain, fox_b_f=fox_b_f, sgu_b_in=full_small["sgu_b_in"],
                 sgu_v_gain=full_small["sgu_v_gain"], sgu_v_bias=full_small["sgu_v_bias"], sgu_w_s=sgu_w_s[0],
                 sgu_b_s=sgu_b_s[0], ffn_conv_w=full_small["ffn_conv_w"], ffn_conv_b=ffn_conv_b)
    loss_dev, dx, g, dmod, (css, rcvs) = _local_step(x[0], loss_target[0], mod, wts, small, comm)

    g["ada_b"] = dmod
    g["loss"] = loss_dev
    small_names = ("ada_b",) + SMALL_SHARDED + tuple(n for n in SMALL_REPL if n != "ada_b") + ("loss",)
    gs = _pack([g[n] for n in small_names], 1)
    rows_s = -(-gs.shape[0] // 8) * 8
    gs = jnp.pad(gs, ((0, rows_s - gs.shape[0]), (0, 0)))
    gs_all = _allgather8(gs, "gather_small_grads").reshape(N_DEV, rows_s, ROW)
    gsum = _sum8(gs_all, "sum_small_grads")
    full_shapes = {n: w[n].shape for n in SMALL_REPL}
    full_shapes.update({n: w[n].shape[:-1] + (w[n].shape[-1] * N_CHIP,) for n in SMALL_SHARDED})
    full_shapes["loss"] = ()
    gfull = dict(zip(small_names, _unpack(gsum, [full_shapes[n] for n in small_names], 1)))
    grads = {n: gfull[n] for n in SMALL_REPL}
    for n in SMALL_SHARDED:
        width = w[n].shape[-1]
        grads[n] = lax.dynamic_slice_in_dim(gfull[n], chip * width, width, axis=gfull[n].ndim - 1)
    dmod_all = gs_all[:, :12, :].reshape(N_DEV, 2, N_CHIP * da)
    dmod_cols = jnp.swapaxes(lax.dynamic_slice_in_dim(dmod_all, chip * da, da, axis=2), 0, 1)
    grads["ada_w"] = _ada_w_grad(c_act.T, dmod_cols)

    gfi = _split_columns(g["fox_w_in"], N_CHIP, FOX_N // N_CHIP, "split_fox_w_in")
    cs_fox = chip_sums([gfi], "_fox")
    css = cs_fox + list(css)
    rcvs = list(_rs_across_chips(cs_fox, "rs_chips_fox")) + list(rcvs)
    halves = [_rs_final_sum(cs_, r_, me_arr, _row_tile(cs_.shape[1]), "rs_final_sum%d" % a)
              for a, (cs_, r_) in enumerate(zip(css, rcvs))]
    others = _rs_swap_halves(halves, "rs_swap")
    red = [jnp.concatenate([jnp.where(ac == 0, h_, o_), jnp.where(ac == 0, o_, h_)]) for h_, o_ in zip(halves, others)]
    grads.update(fox_w_in=red[0], fox_w_out=red[1], sgu_w_in=red[2], sgu_w_out=red[3],
                 ffn_w_up=jnp.stack([red[4], red[5]]), ffn_w_down=jnp.stack([red[6], red[7]]))

    delta, new_m, new_v = {}, {}, {}
    for n in BIG + ("ada_w",):
        shp = w[n].shape
        two_d = lambda a: a.reshape(-1, shp[-1])
        d_, m_, v_ = _adamw(two_d(w[n]), two_d(grads[n]), two_d(mom[n]), two_d(var[n]), "adamw_" + n)
        delta[n], new_m[n], new_v[n] = d_.reshape(shp), m_.reshape(shp), v_.reshape(shp)
    rest = SMALL_SHARDED + SMALL_REPL
    packs = [_pack([t[n] for n in rest], 1) for t in (w, grads, mom, var)]
    rows_r = -(-packs[0].shape[0] // 8) * 8
    packs = [jnp.pad(p, ((0, rows_r - p.shape[0]), (0, 0))) for p in packs]
    outs = _adamw(*packs, "adamw_small")
    for t, o in zip((delta, new_m, new_v), outs):
        t.update(zip(rest, _unpack(o, [w[n].shape for n in rest], 1)))

    loss = gfull["loss"]
    return (loss, dx[None], *[grads[n].reshape(w[n].shape) for n in WEIGHTS], *[delta[n] for n in WEIGHTS],
            *[new_m[n] for n in WEIGHTS], *[new_v[n] for n in WEIGHTS])
```

```python
import functools
import math

import numpy as np
import jax
import jax.numpy as jnp
from jax import lax
from jax.experimental import pallas as pl
from jax.experimental.pallas import tpu as pltpu

F32 = jnp.float32
BF16 = jnp.bfloat16
MESH = pl.DeviceIdType.MESH

D = 1024
H = 16
DH = 64
NP = H // 2
LANES = 128
DFF = 2816
SGW = 2048
SGG = 8
SGC = 256
SGB = 128
CHUNK = 64
EPS = 1e-6
FOX_N = 4 * D + H
FOX_NP = 4224
GT = 256
NGT = DFF // GT
SCALE = DH ** -0.5
LOG2E = 1.4426950408889634

ADAM_LR = 0.001
ADAM_B1 = 0.9
ADAM_B2 = 0.999
ADAM_EPS = 1e-08
ADAM_WD = 0.01
ADAM_STEP = 10

V7X_VMEM_LIMIT = 56 * 1024 * 1024

L_F = 64
L_NF = 67
L_LSE = 70


def _cparams(sem=None):
    return pltpu.CompilerParams(dimension_semantics=sem, vmem_limit_bytes=V7X_VMEM_LIMIT)


def _split3(x):
    hi = x.astype(BF16)
    r = x - hi.astype(F32)
    mid = r.astype(BF16)
    lo = (r - mid.astype(F32)).astype(BF16)
    return hi, mid, lo


def _dot(a, b, dims=(((1,), (0,)), ((), ()))):
    return lax.dot_general(a, b, dims, preferred_element_type=F32)


def _dot_nt(a, b):
    return _dot(a, b, (((1,), (1,)), ((), ())))


def _dot_tn(a, b):
    return _dot(a, b, (((0,), (0,)), ((), ())))


def _exact_dot(m_bf16, x_f32):
    hi, mid, lo = _split3(x_f32)
    return _dot(m_bf16, hi) + _dot(m_bf16, mid) + _dot(m_bf16, lo)


def _exact_dot_r(x_f32, m_bf16):
    hi, mid, lo = _split3(x_f32)
    return _dot(hi, m_bf16) + _dot(mid, m_bf16) + _dot(lo, m_bf16)


def _head_block_ones():
    r = lax.broadcasted_iota(jnp.int32, (LANES, LANES), 0) // DH
    c = lax.broadcasted_iota(jnp.int32, (LANES, LANES), 1) // DH
    return (r == c).astype(BF16)


def _sigmoid(x):
    return 1.0 / (1.0 + jnp.exp(-x))


def _gelu(x):
    c = math.sqrt(2.0 / math.pi)
    return 0.5 * x * (1.0 + jnp.tanh(c * (x + 0.044715 * (x * x * x))))


def _gelu_and_grad(x):
    c = math.sqrt(2.0 / math.pi)
    x2 = x * x
    t = jnp.tanh(c * (x + 0.044715 * (x2 * x)))
    half = 0.5 * (1.0 + t)
    return x * half, half + 0.5 * x * (1.0 - t * t) * c * (1.0 + 3 * 0.044715 * x2)


def _rstd_rows(x):
    return lax.rsqrt(jnp.mean(x * x, axis=-1, keepdims=True) + EPS)


def _norm_mod_matmul(x, ng, sc, sh, w, bias, out_dtype, ts, tn, name, planes=1):
    s, d = x.shape
    ns = w.shape[-1]
    n = w.shape[0] * ns if w.ndim == 3 else ns
    nc = n // planes

    def body(x_ref, ng_ref, sc_ref, sh_ref, w_ref, b_ref, o_ref, h_ref):
        xv = x_ref[...]
        h = (xv * _rstd_rows(xv) * ng_ref[...] * (1.0 + sc_ref[...]) + sh_ref[...]).astype(BF16)
        h_ref[...] = h
        for e in range(planes):
            for c0 in range(0, nc, tn):
                g0 = e * nc + c0
                wv = w_ref[g0 // ns, :, g0 % ns:g0 % ns + tn] if w.ndim == 3 else w_ref[:, g0:g0 + tn]
                val = (_dot(h, wv) + b_ref[:, g0:g0 + tn]).astype(out_dtype)
                if planes == 1:
                    o_ref[:, c0:c0 + tn] = val
                else:
                    o_ref[e, :, c0:c0 + tn] = val

    vec = pl.BlockSpec((1, d), lambda i: (0, 0))
    w_spec = (pl.BlockSpec(w.shape, lambda i: (0, 0, 0)) if w.ndim == 3 else pl.BlockSpec((d, n), lambda i: (0, 0)))
    if planes == 1:
        o_spec, o_shape = pl.BlockSpec((ts, n), lambda i: (i, 0)), (s, n)
    else:
        o_spec, o_shape = pl.BlockSpec((planes, ts, nc), lambda i: (0, i, 0)), (planes, s, nc)
    return pl.pallas_call(
        body, name=name, grid=(s // ts,),
        in_specs=[pl.BlockSpec((ts, d), lambda i: (i, 0)), vec, vec, vec, w_spec,
                  pl.BlockSpec((1, n), lambda i: (0, 0))],
        out_specs=[o_spec, pl.BlockSpec((ts, d), lambda i: (i, 0))],
        out_shape=[jax.ShapeDtypeStruct(o_shape, out_dtype), jax.ShapeDtypeStruct((s, d), BF16)],
        compiler_params=_cparams(("arbitrary",)),
    )(x, ng, sc, sh, w, bias)


def _matmul(a, b, ta, tb, tm, tn, tk, out_dtype, name, out_parts=1):
    if a.ndim == 3:
        m, k = a.shape[1], a.shape[0] * a.shape[2]
        nkp = a.shape[2] // tk
    else:
        m, k = (a.shape[1], a.shape[0]) if ta else a.shape
    if b.ndim == 3:
        n = b.shape[1] if tb else b.shape[0] * b.shape[2]
        nbp = b.shape[2] // (tk if tb else tn)
    else:
        n = b.shape[0] if tb else b.shape[1]
    nk = k // tk
    nop = n // out_parts // tn
    dims = (((0,) if ta else (1,), (1,) if tb else (0,)), ((), ()))

    def body(a_ref, b_ref, o_ref, acc):
        kk = pl.program_id(2)

        @pl.when(kk == 0)
        def _():
            acc[...] = jnp.zeros_like(acc)
        acc[...] += _dot(a_ref[...], b_ref[...], dims)

        @pl.when(kk == nk - 1)
        def _():
            o_ref[...] = acc[...].astype(out_dtype)

    if a.ndim == 3:
        a_spec = pl.BlockSpec((None, tm, tk), lambda i, j, kk: (kk // nkp, i, kk % nkp))
    else:
        a_spec = (pl.BlockSpec((tk, tm), lambda i, j, kk: (kk, i)) if ta
                  else pl.BlockSpec((tm, tk), lambda i, j, kk: (i, kk)))
    if b.ndim == 3 and tb:
        b_spec = pl.BlockSpec((None, tn, tk), lambda i, j, kk: (kk // nbp, j, kk % nbp))
    elif b.ndim == 3:
        b_spec = pl.BlockSpec((None, tk, tn), lambda i, j, kk: (j // nbp, kk, j % nbp))
    else:
        b_spec = (pl.BlockSpec((tn, tk), lambda i, j, kk: (j, kk)) if tb
                  else pl.BlockSpec((tk, tn), lambda i, j, kk: (kk, j)))
    if out_parts > 1:
        o_spec = pl.BlockSpec((None, tm, tn), lambda i, j, kk: (j // nop, i, j % nop))
        o_shape = (out_parts, m, n // out_parts)
    else:
        o_spec, o_shape = pl.BlockSpec((tm, tn), lambda i, j, kk: (i, j)), (m, n)
    return pl.pallas_call(
        body, name=name, grid=(m // tm, n // tn, nk),
        in_specs=[a_spec, b_spec],
        out_specs=o_spec,
        out_shape=jax.ShapeDtypeStruct(o_shape, out_dtype),
        scratch_shapes=[pltpu.VMEM((tm, tn), F32)],
        compiler_params=_cparams(("arbitrary", "arbitrary", "arbitrary")),
    )(a, b)


def _matmul_wt(a, w, tn, tk, out_dtype, ts, name):
    s = a.shape[-2]
    ka, kw = a.shape[-1], w.shape[-1]
    k = ka * (a.shape[0] if a.ndim == 3 else 1)
    n = w.shape[-2]

    def body(a_ref, w_ref, o_ref):
        for n0 in range(0, n, tn):
            acc = None
            for g0 in range(0, k, tk):
                av = a_ref[g0 // ka, :, g0 % ka:g0 % ka + tk] if a.ndim == 3 else a_ref[:, g0:g0 + tk]
                wv = (w_ref[g0 // kw, n0:n0 + tn, g0 % kw:g0 % kw + tk] if w.ndim == 3
                      else w_ref[n0:n0 + tn, g0:g0 + tk])
                part = _dot_nt(av, wv)
                acc = part if acc is None else acc + part
            o_ref[:, n0:n0 + tn] = acc.astype(out_dtype)

    a_spec = (pl.BlockSpec((a.shape[0], ts, ka), lambda i: (0, i, 0)) if a.ndim == 3
              else pl.BlockSpec((ts, ka), lambda i: (i, 0)))
    w_spec = pl.BlockSpec(w.shape, (lambda i: (0, 0, 0)) if w.ndim == 3 else (lambda i: (0, 0)))
    return pl.pallas_call(
        body, name=name, grid=(s // ts,),
        in_specs=[a_spec, w_spec], out_specs=pl.BlockSpec((ts, n), lambda i: (i, 0)),
        out_shape=jax.ShapeDtypeStruct((s, n), out_dtype),
        compiler_params=_cparams(("arbitrary",)),
    )(a, w)


def _lane(shape):
    return lax.broadcasted_iota(jnp.int32, shape, 1)


def _pair_norm(x, gain2, bones):
    msq = _exact_dot_r(x * x, bones) * (1.0 / DH)
    r = lax.rsqrt(msq + EPS)
    xh = x * r
    return xh * gain2, xh, r


def _fox_post(proj, qg2, kg2, bf, ts, name):
    s = proj.shape[0]

    def body(p_ref, qg_ref, kg_ref, bf_ref, q_ref, k_ref, v_ref, carry):
        @pl.when(pl.program_id(0) == 0)
        def _():
            carry[...] = jnp.zeros_like(carry)
        lane = _lane((ts, LANES))
        bones = _head_block_ones()
        xf = p_ref[:, 4 * D:4 * D + LANES] + bf_ref[...]
        logf = jnp.minimum(xf, 0.0) - jnp.log(1.0 + jnp.exp(-jnp.abs(xf)))
        logf = jnp.where(lane < H, logf, 0.0)
        rr = lax.broadcasted_iota(jnp.int32, (ts, ts), 0)
        cc = lax.broadcasted_iota(jnp.int32, (ts, ts), 1)
        ltri = (cc <= rr).astype(BF16)
        fcum = _exact_dot(ltri, logf) + carry[0:1, :]
        carry[0:1, :] = fcum[ts - 1:ts, :]
        fhi, fmid, flo = _split3(fcum * LOG2E)
        fhi, fmid, flo = fhi.astype(F32), fmid.astype(F32), flo.astype(F32)
        one_q = ((lane >= L_NF) & (lane < L_NF + 3)).astype(F32)
        one_k = (((lane >= L_F) & (lane < L_F + 3)) | ((lane >= L_LSE) & (lane < L_LSE + 3))).astype(F32)
        one_v = ((lane >= L_F) & (lane < L_F + 3)).astype(F32)
        for p in range(NP):
            qn, _, _ = _pair_norm(p_ref[:, p * LANES:(p + 1) * LANES], qg_ref[...], bones)
            kn, _, _ = _pair_norm(p_ref[:, D + p * LANES:D + (p + 1) * LANES], kg_ref[...], bones)
            vv = p_ref[:, 2 * D + p * LANES:2 * D + (p + 1) * LANES]
            qn = qn * (SCALE * LOG2E)
            for e in range(2):
                h = 2 * p + e
                if e == 1:
                    qe, ke, ve = (pltpu.roll(t, DH, axis=1) for t in (qn, kn, vv))
                else:
                    qe, ke, ve = qn, kn, vv
                f0, f1, f2 = fhi[:, h:h + 1], fmid[:, h:h + 1], flo[:, h:h + 1]
                fq = jnp.where(lane == L_F, f0, jnp.where(lane == L_F + 1, f1, jnp.where(lane == L_F + 2, f2, one_q)))
                fk = jnp.where(lane == L_NF, -f0, jnp.where(lane == L_NF + 1, -f1, jnp.where(lane == L_NF + 2, -f2, one_k)))
                q_ref[h] = jnp.where(lane < DH, qe, fq).astype(BF16)
                k_ref[h] = jnp.where(lane < DH, ke, fk).astype(BF16)
                v_ref[h] = jnp.where(lane < DH, ve, one_v).astype(BF16)

    hs = pl.BlockSpec((H, ts, LANES), lambda i: (0, i, 0))
    vec = pl.BlockSpec((1, LANES), lambda i: (0, 0))
    shp = jax.ShapeDtypeStruct((H, s, LANES), BF16)
    return pl.pallas_call(
        body, name=name, grid=(s // ts,),
        in_specs=[pl.BlockSpec((ts, FOX_NP), lambda i: (i, 0)), vec, vec, vec],
        out_specs=[hs, hs, hs], out_shape=[shp, shp, shp],
        scratch_shapes=[pltpu.VMEM((8, LANES), F32)],
        compiler_params=_cparams(("arbitrary",)),
    )(proj, qg2, kg2, bf)


def _gather_copies(p_refs, o_refs, send_sems, recv_sems):
    x, y, c = _mesh_pos()
    me = 2 * x + y
    sends, arrivals = [], []
    for a, (p_ref, o_ref) in enumerate(zip(p_refs, o_refs)):
        rh = p_ref.shape[0] // 2
        for k, chip in enumerate(_other_chips(x, y)):
            ci = 2 * chip[0] + chip[1]
            for cc in range(2):
                sends.append(_remote(p_ref.at[pl.ds(c * rh, rh), :], o_ref.at[me, pl.ds(c * rh, rh), :],
                                     send_sems.at[6 * a + 2 * k + cc], recv_sems.at[6 * a + 2 * k + c], (*chip, cc)))
                arrivals.append(_remote(o_ref.at[ci, pl.ds(cc * rh, rh), :], o_ref.at[ci, pl.ds(cc * rh, rh), :],
                                        send_sems.at[6 * a + 2 * k + cc], recv_sems.at[6 * a + 2 * k + cc],
                                        (*chip, cc)))
    return sends, arrivals


def _attn_fwd(qa, ka, va, tq, name, shards=()):
    s = qa.shape[1]
    nq = s // tq
    na = len(shards)
    hps = HPS_FWD

    def body(*refs):
        q_ref, k_ref, v_ref = refs[:3]
        p_refs = refs[3:3 + na]
        o_ref, ql_ref = refs[3 + na:5 + na]
        g_refs = refs[5 + na:5 + 2 * na]
        i = pl.program_id(1)
        if na:
            send_sems, recv_sems = refs[5 + 2 * na:]

            @pl.when((pl.program_id(0) == 0) & (i == 0))
            def _():
                for cp in _gather_copies(p_refs, g_refs, send_sems, recv_sems)[0]:
                    cp.start()
        lane = _lane((tq, LANES))
        qs_ = [q_ref[e] for e in range(hps)]

        tk = min(TK_FWD, tq)
        nks = tq // tk

        def step(j, carry, diag=None):
            off = pl.multiple_of(j * tk, tk)
            scs = [_dot_nt(qs_[e], k_ref[e, pl.ds(off, tk), :]) for e in range(hps)]
            probs = []
            for e in range(hps):
                m, sc = carry[e][0], scs[e]
                if diag is not None:
                    rr = lax.broadcasted_iota(jnp.int32, (tq, tk), 0)
                    cc = lax.broadcasted_iota(jnp.int32, (tq, tk), 1) + diag * tk
                    sc = jnp.where(cc <= rr, sc, -jnp.inf)
                m_new = jnp.maximum(m, jnp.max(sc, axis=-1, keepdims=True))
                probs.append((m_new, jnp.exp2(sc - m_new).astype(BF16), jnp.exp2(m - m_new)))
            return tuple((m_new, carry[e][1] * alpha + _dot(pr, v_ref[e, pl.ds(off, tk), :]))
                         for e, (m_new, pr, alpha) in enumerate(probs))

        one = (jnp.full((tq, 1), -jnp.inf, F32), jnp.zeros((tq, LANES), F32))
        carry = lax.fori_loop(0, i * nks, step, (one,) * hps)
        for r in range(nks):
            carry = step(i * nks + r, carry, diag=r)
        outs = []
        for e in range(hps):
            m, acc = carry[e]
            l = acc[:, L_F:L_F + 1]
            outs.append(acc / l)
            lse = m + jnp.log2(l)
            h0, h1, h2 = _split3(-lse)
            ql = jnp.where(lane == L_LSE, h0.astype(F32),
                           jnp.where(lane == L_LSE + 1, h1.astype(F32),
                                     jnp.where(lane == L_LSE + 2, h2.astype(F32), qs_[e].astype(F32))))
            ql_ref[e] = ql.astype(BF16)
        for e in range(0, hps, 2):
            o_ref[:, e * DH:(e + 2) * DH] = jnp.where(lane < DH, outs[e], pltpu.roll(outs[e + 1], DH, axis=1))
        if na:
            @pl.when((pl.program_id(0) == H // hps - 1) & (i == nq - 1))
            def _():
                sends, arrivals = _gather_copies(p_refs, g_refs, send_sems, recv_sems)
                for cp in arrivals:
                    cp.wait_recv()
                for cp in sends:
                    cp.wait_send()

    res = pl.BlockSpec((hps, s, LANES), lambda p, i: (p, 0, 0))
    qs = pl.BlockSpec((hps, tq, LANES), lambda p, i: (p, i, 0))
    outs = pl.pallas_call(
        body, name=name, grid=(H // hps, nq),
        in_specs=[qs, res, res] + [HBM_SPEC] * na,
        out_specs=[pl.BlockSpec((tq, hps * DH), lambda p, i: (i, p)), qs] + [HBM_SPEC] * na,
        out_shape=[jax.ShapeDtypeStruct((s, D), F32), jax.ShapeDtypeStruct((H, s, LANES), BF16)]
        + [jax.ShapeDtypeStruct((N_CHIP,) + p.shape, p.dtype) for p in shards],
        scratch_shapes=[pltpu.SemaphoreType.DMA((6 * na,))] * 2 if na else [],
        compiler_params=_cparams(("arbitrary", "arbitrary")),
    )(qa, ka, va, *shards)
    return outs[0], outs[1], list(outs[2:])


def _chip_exchange_copies(cs_refs, o_refs, send_sems, recv_sems):
    x, y, c = _mesh_pos()
    cps = []
    for a, (cs_ref, o_ref) in enumerate(zip(cs_refs, o_refs)):
        for k, chip in enumerate(_other_chips(x, y)):
            ci = 2 * chip[0] + chip[1]
            cps.append(_remote(cs_ref.at[ci], o_ref.at[k], send_sems.at[3 * a + k], recv_sems.at[3 * a + k],
                               (*chip, c)))
    return cps


def _attn_bwd(ql, ka, va, doa, tq, name, css=()):
    s = ql.shape[1]
    nq = s // tq
    na = len(css)

    def body(*refs):
        q_ref, k_ref, v_ref, do_ref = refs[:4]
        cs_refs = refs[4:4 + na]
        dqo_ref, dk_ref, dv_ref = refs[4 + na:7 + na]
        r_refs = refs[7 + na:7 + 2 * na]
        dq_ref = refs[7 + 2 * na]
        j = pl.program_id(1)
        if na:
            send_sems, recv_sems = refs[8 + 2 * na:]

            @pl.when((pl.program_id(0) == 0) & (j == 0))
            def _():
                for cp in _chip_exchange_copies(cs_refs, r_refs, send_sems, recv_sems):
                    cp.start()

        @pl.when(j == 0)
        def _():
            dq_ref[...] = jnp.zeros_like(dq_ref)
        lane = _lane((tq, LANES))
        kbs = [k_ref[0], k_ref[1]]
        vbs = [v_ref[0], v_ref[1]]

        def step(i, carry, masked):
            ioff = pl.multiple_of(i * tq, tq)
            qbs = [q_ref[e, pl.ds(ioff, tq), :] for e in range(2)]
            dobs = [do_ref[e, pl.ds(ioff, tq), :] for e in range(2)]
            scs = [_dot_nt(qbs[e], kbs[e]) for e in range(2)]
            dps = [_dot_nt(dobs[e], vbs[e]) for e in range(2)]
            prs, dss = [], []
            for e in range(2):
                pr = jnp.exp2(scs[e])
                if masked:
                    rr = lax.broadcasted_iota(jnp.int32, (tq, tq), 0)
                    cc = lax.broadcasted_iota(jnp.int32, (tq, tq), 1)
                    pr = jnp.where(cc <= rr, pr, 0.0)
                dss.append((pr * dps[e]).astype(BF16))
                prs.append(pr.astype(BF16))
            new = []
            for e in range(2):
                dk, dv = carry[e]
                dv = dv + _dot_tn(prs[e], dobs[e])
                dk = dk + _dot_tn(dss[e], qbs[e])
                dq_ref[e, pl.ds(ioff, tq), :] += _dot(dss[e], kbs[e])
                new.append((dk, dv))
            return tuple(new)

        zero = jnp.zeros((tq, LANES), F32)
        carry = step(j, ((zero, zero), (zero, zero)), True)
        carry = lax.fori_loop(j + 1, nq, functools.partial(step, masked=False), carry)
        for e in range(2):
            dk, dv = carry[e]
            col = dk[:, L_NF:L_NF + 1]
            hi = col.astype(BF16).astype(F32)
            dk_ref[e] = jnp.where(lane == L_NF, hi, jnp.where(lane == L_NF + 1, col - hi, dk)).astype(BF16)
            dv_ref[e] = dv.astype(BF16)

        @pl.when(j == nq - 1)
        def _():
            lane_s = _lane((s, LANES))
            for e in range(2):
                dq = dq_ref[e]
                col = dq[:, L_F:L_F + 1]
                hi = col.astype(BF16).astype(F32)
                dqo_ref[e] = jnp.where(lane_s == L_F, hi, jnp.where(lane_s == L_F + 1, col - hi, dq)).astype(BF16)
        if na:
            @pl.when((pl.program_id(0) == NP - 1) & (j == nq - 1))
            def _():
                for cp in _chip_exchange_copies(cs_refs, r_refs, send_sems, recv_sems):
                    cp.wait()

    res = pl.BlockSpec((2, s, LANES), lambda p, j: (p, 0, 0))
    tile = pl.BlockSpec((2, tq, LANES), lambda p, j: (p, j, 0))
    shp = jax.ShapeDtypeStruct((H, s, LANES), BF16)
    outs = pl.pallas_call(
        body, name=name, grid=(NP, nq),
        in_specs=[res, tile, tile, res] + [HBM_SPEC] * na, out_specs=[res, tile, tile] + [HBM_SPEC] * na,
        out_shape=[shp, shp, shp] + [jax.ShapeDtypeStruct((3,) + cs.shape[1:], cs.dtype) for cs in css],
        scratch_shapes=[pltpu.VMEM((2, s, LANES), F32)] + ([pltpu.SemaphoreType.DMA((3 * na,))] * 2 if na else []),
        compiler_params=_cparams(("arbitrary", "arbitrary")),
    )(ql, ka, va, doa, *css)
    return outs[0], outs[1], outs[2], list(outs[3:])


def _gate_out(att, proj, w, xin, g, ts, name):
    s = att.shape[0]

    def body(a_ref, o_ref, w_ref, x_ref, g_ref, xo_ref, y_ref, gt_ref):
        gated = (a_ref[...] * _sigmoid(o_ref[...])).astype(BF16)
        gt_ref[...] = gated
        y = _dot(gated, w_ref[...])
        xo_ref[...] = x_ref[...] + g_ref[...] * y
        y_ref[...] = y.astype(BF16)

    row = pl.BlockSpec((ts, D), lambda i: (i, 0))
    return pl.pallas_call(
        body, name=name, grid=(s // ts,),
        in_specs=[row, pl.BlockSpec((ts, D), lambda i: (i, 3)), pl.BlockSpec((D, D), lambda i: (0, 0)), row,
                  pl.BlockSpec((1, D), lambda i: (0, 0))],
        out_specs=[row, row, row],
        out_shape=[jax.ShapeDtypeStruct((s, D), F32), jax.ShapeDtypeStruct((s, D), BF16),
                   jax.ShapeDtypeStruct((s, D), BF16)],
        compiler_params=_cparams(("arbitrary",)),
    )(att, proj, w, xin, g)


def _attn_bwd_prep(dy, w_out, att, proj, ts, name):
    s = att.shape[0]

    def body(dy_ref, w_ref, a_ref, o_ref, doa_ref, dop_ref):
        lane = _lane((ts, LANES))
        bones = _head_block_ones()
        dgv = _dot_nt(dy_ref[...], w_ref[...])
        for p in range(NP):
            sl = slice(p * LANES, (p + 1) * LANES)
            dg, a = dgv[:, sl], a_ref[:, sl]
            sig = _sigmoid(o_ref[:, sl])
            datt = dg * sig
            dop_ref[:, sl] = (dg * a * sig * (1.0 - sig)).astype(BF16)
            delta = _exact_dot_r(datt * a, bones)
            for e in range(2):
                de, dl = (datt, delta) if e == 0 else (pltpu.roll(datt, DH, axis=1), pltpu.roll(delta, DH, axis=1))
                h0, h1, h2 = _split3(-dl[:, 0:1])
                aug = jnp.where(lane == L_F, h0.astype(F32),
                                jnp.where(lane == L_F + 1, h1.astype(F32),
                                          jnp.where(lane == L_F + 2, h2.astype(F32), 0.0)))
                doa_ref[2 * p + e] = jnp.where(lane < DH, de, aug).astype(BF16)

    row = pl.BlockSpec((ts, D), lambda i: (i, 0))
    return pl.pallas_call(
        body, name=name, grid=(s // ts,),
        in_specs=[row, pl.BlockSpec((D, D), lambda i: (0, 0)), row, pl.BlockSpec((ts, D), lambda i: (i, 3))],
        out_specs=[pl.BlockSpec((H, ts, LANES), lambda i: (0, i, 0)), row],
        out_shape=[jax.ShapeDtypeStruct((H, s, LANES), BF16), jax.ShapeDtypeStruct((s, D), BF16)],
        compiler_params=_cparams(("arbitrary",)),
    )(dy, w_out, att, proj)


def _fox_post_bwd(proj, dqa, dka, dva, dop, qg2, kg2, bf, ts, name):
    s = proj.shape[0]
    nt = s // ts

    def body(p_ref, dq_ref, dk_ref, dv_ref, dop_ref, qg_ref, kg_ref, bf_ref, o_ref, red_ref, carry):
        @pl.when(pl.program_id(0) == 0)
        def _():
            carry[...] = jnp.zeros_like(carry)
            red_ref[...] = jnp.zeros_like(red_ref)
        lane = _lane((ts, LANES))
        bones = _head_block_ones()
        d_f = jnp.zeros((ts, LANES), F32)
        dqg = jnp.zeros((1, LANES), F32)
        dkg = jnp.zeros((1, LANES), F32)
        for p in range(NP):
            heads = [[ref[2 * p + e].astype(F32) for e in range(2)] for ref in (dq_ref, dk_ref, dv_ref)]
            pair = [jnp.where(lane < DH, a, pltpu.roll(b, DH, axis=1)) for a, b in heads]
            for e in range(2):
                dqe, dke = heads[0][e], heads[1][e]
                col = (dqe[:, L_F:L_F + 1] + dqe[:, L_F + 1:L_F + 2]
                       - dke[:, L_NF:L_NF + 1] - dke[:, L_NF + 1:L_NF + 2])
                d_f = jnp.where(lane == 2 * p + e, col, d_f)
            for idx, (g_ref, base) in enumerate(((qg_ref, 0), (kg_ref, D))):
                x = p_ref[:, base + p * LANES:base + (p + 1) * LANES]
                _, xh, r = _pair_norm(x, g_ref[...], bones)
                dn = pair[idx] * (SCALE if idx == 0 else 1.0 / LOG2E)
                t = dn * g_ref[...]
                mean_txh = _exact_dot_r(t * xh, bones) * (1.0 / DH)
                dx = r * (t - xh * mean_txh)
                o_ref[:, base + p * LANES:base + (p + 1) * LANES] = dx.astype(BF16)
                gsum = jnp.sum(dn * xh, axis=0, keepdims=True)
                if idx == 0:
                    dqg = dqg + gsum
                else:
                    dkg = dkg + gsum
            o_ref[:, 2 * D + p * LANES:2 * D + (p + 1) * LANES] = pair[2].astype(BF16)
        o_ref[:, 3 * D:4 * D] = dop_ref[...]
        rr = lax.broadcasted_iota(jnp.int32, (ts, ts), 0)
        cc = lax.broadcasted_iota(jnp.int32, (ts, ts), 1)
        utri = (cc >= rr).astype(BF16)
        dlogf = _exact_dot(utri, d_f) + carry[0:1, :]
        carry[0:1, :] = dlogf[0:1, :]
        xf = p_ref[:, 4 * D:4 * D + LANES] + bf_ref[...]
        dfl = jnp.where(lane < H, dlogf * _sigmoid(-xf), 0.0)
        o_ref[:, 4 * D:4 * D + LANES] = dfl.astype(BF16)
        red_ref[0:1, :] += dqg
        red_ref[1:2, :] += dkg
        red_ref[2:3, :] += jnp.sum(dfl, axis=0, keepdims=True)

    hs = pl.BlockSpec((H, ts, LANES), lambda i: (0, nt - 1 - i, 0))
    vec = pl.BlockSpec((1, LANES), lambda i: (0, 0))
    return pl.pallas_call(
        body, name=name, grid=(nt,),
        in_specs=[pl.BlockSpec((ts, FOX_NP), lambda i: (nt - 1 - i, 0)), hs, hs, hs,
                  pl.BlockSpec((ts, D), lambda i: (nt - 1 - i, 0)), vec, vec, vec],
        out_specs=[pl.BlockSpec((ts, FOX_NP), lambda i: (nt - 1 - i, 0)),
                   pl.BlockSpec((8, LANES), lambda i: (0, 0))],
        out_shape=[jax.ShapeDtypeStruct((s, FOX_NP), BF16), jax.ShapeDtypeStruct((8, LANES), F32)],
        scratch_shapes=[pltpu.VMEM((8, LANES), F32)],
        compiler_params=_cparams(("arbitrary",)),
    )(proj, dqa, dka, dva, dop, qg2, kg2, bf)


HALO = 16
TS = 512
TQ = 512
TR = 256
TP = 256
HPS_FWD = 4
TK_FWD = 512
TKW = 2048


def _shift_down(x, k):
    return pltpu.roll(x, k, axis=0)


def _shift_up(x, k):
    return pltpu.roll(x, x.shape[0] - k, axis=0)


def _conv_down(a, cw, cb, w, xin, gate, ts, name):
    s = a.shape[1]
    d = w.shape[1]
    hb = ts // HALO

    def body(prev_ref, a_ref, cw_ref, cb_ref, w_ref, x_ref, g_ref, o_ref, y_ref, f_ref, ap_ref):
        i = pl.program_id(0)
        acc = None
        for c in range(NGT):
            cols = slice(c * GT, (c + 1) * GT)
            both = lambda ref: jnp.concatenate([ref[0, :, cols].astype(F32), ref[1, :, cols].astype(F32)], axis=1)
            cwv, cbv = both(cw_ref), both(cb_ref)
            ext = jnp.concatenate([jnp.where(i > 0, both(prev_ref), 0.0), both(a_ref)], axis=0)
            ap = (_shift_down(ext, 2) * cwv[0:1, :] + _shift_down(ext, 1) * cwv[1:2, :]
                  + ext * cwv[2:3, :] + cbv)[HALO:, :]
            g, val = ap[:, :GT], ap[:, GT:]
            fch = (g * _sigmoid(g) * val).astype(BF16)
            f_ref[:, cols] = fch
            ap_ref[0, :, cols] = g.astype(BF16)
            ap_ref[1, :, cols] = val.astype(BF16)
            part = _dot(fch, w_ref[cols, :])
            acc = part if acc is None else acc + part
        y_ref[...] = acc.astype(BF16)
        o_ref[...] = x_ref[...] + g_ref[...] * acc

    row = pl.BlockSpec((ts, d), lambda i: (i, 0))
    planes = pl.BlockSpec((2, ts, DFF), lambda i: (0, i, 0))
    return pl.pallas_call(
        body, name=name, grid=(s // ts,),
        in_specs=[pl.BlockSpec((2, HALO, DFF), lambda i: (0, jnp.maximum(i * hb - 1, 0), 0)), planes,
                  pl.BlockSpec((2, 8, DFF), lambda i: (0, 0, 0)), pl.BlockSpec((2, 1, DFF), lambda i: (0, 0, 0)),
                  pl.BlockSpec((DFF, d), lambda i: (0, 0)), row, pl.BlockSpec((1, d), lambda i: (0, 0))],
        out_specs=[row, row, pl.BlockSpec((ts, DFF), lambda i: (i, 0)), planes],
        out_shape=[jax.ShapeDtypeStruct((s, d), F32), jax.ShapeDtypeStruct((s, d), BF16),
                   jax.ShapeDtypeStruct((s, DFF), BF16), jax.ShapeDtypeStruct((2, s, DFF), BF16)],
        compiler_params=_cparams(("arbitrary",)),
    )(a, a, cw, cb, w, xin, gate)


def _down_bwd_conv(dy, w, a, ap, cw, ts, name):
    s, d = dy.shape
    hb = ts // HALO
    nt = s // ts
    nhb = s // HALO

    def body(dy_ref, dyn_ref, w_ref, a_ref, ap_ref, apn_ref, cw_ref, da_ref, red_ref):
        i = pl.program_id(0)

        @pl.when(i == 0)
        def _():
            red_ref[...] = jnp.zeros_like(red_ref)
        dyn = jnp.where(i < nt - 1, dyn_ref[...], jnp.zeros_like(dyn_ref))
        dye = jnp.concatenate([dy_ref[...], dyn], axis=0)
        for c in range(NGT):
            cols = slice(c * GT, (c + 1) * GT)
            both = lambda ref: jnp.concatenate([ref[0, :, cols].astype(F32), ref[1, :, cols].astype(F32)], axis=1)
            cwv = both(cw_ref)
            dfe = _dot_nt(dye, w_ref[cols, :])
            apv = jnp.concatenate([both(ap_ref), both(apn_ref)], axis=0)
            g, val = apv[:, :GT], apv[:, GT:]
            sg = _sigmoid(g)
            dap = jnp.concatenate([dfe * val * (sg * (1.0 + g * (1.0 - sg))), dfe * (g * sg)], axis=1)
            shifted = [_shift_up(dap, 2)[:ts], _shift_up(dap, 1)[:ts], dap[:ts]]
            da = shifted[0] * cwv[0:1, :] + shifted[1] * cwv[1:2, :] + shifted[2] * cwv[2:3, :]
            av = both(a_ref)
            sums = [jnp.sum(av * t, axis=0, keepdims=True) for t in shifted]
            sums.append(jnp.sum(shifted[2], axis=0, keepdims=True))
            for e in range(2):
                half = slice(e * GT, (e + 1) * GT)
                da_ref[e, :, cols] = da[:, half].astype(BF16)
                for r, sm in enumerate(sums):
                    red_ref[e, r:r + 1, cols] += sm[:, half]

    planes = pl.BlockSpec((2, ts, DFF), lambda i: (0, i, 0))
    nxt = lambda i: jnp.minimum((i + 1) * hb, nhb - 1)
    return pl.pallas_call(
        body, name=name, grid=(nt,),
        in_specs=[pl.BlockSpec((ts, d), lambda i: (i, 0)), pl.BlockSpec((HALO, d), lambda i: (nxt(i), 0)),
                  pl.BlockSpec((DFF, d), lambda i: (0, 0)), planes, planes,
                  pl.BlockSpec((2, HALO, DFF), lambda i: (0, nxt(i), 0)),
                  pl.BlockSpec((2, 8, DFF), lambda i: (0, 0, 0))],
        out_specs=[planes, pl.BlockSpec((2, 8, DFF), lambda i: (0, 0, 0))],
        out_shape=[jax.ShapeDtypeStruct((2, s, DFF), BF16), jax.ShapeDtypeStruct((2, 8, DFF), F32)],
        compiler_params=_cparams(("arbitrary",)),
    )(dy, dy, w, a, ap, ap, cw)


def _chunk_mask(transposed=False):
    t = lax.broadcasted_iota(jnp.int32, (SGB, SGB), 0) // CHUNK
    u = lax.broadcasted_iota(jnp.int32, (SGB, SGB), 1) // CHUNK
    return (t <= u) if transposed else (u <= t)


def _sgu_ln(v, gain, bias):
    mu = jnp.mean(v, axis=-1, keepdims=True)
    vc = v - mu
    rstd = lax.rsqrt(jnp.mean(vc * vc, axis=-1, keepdims=True) + EPS)
    vhat = vc * rstd
    return vhat * gain + bias, vhat, rstd


def _sgu_fwd(z, vgain, vbias, ws, bst, w_out, xin, gate, tr, name):
    s = z.shape[0]

    def body(zu_ref, zv_ref, vg_ref, vb_ref, ws_ref, bs_ref, wo_ref, x_ref, gt_ref, xo_ref, yo_ref, y_ref):
        u = _gelu(zu_ref[...].astype(F32))
        vn, _, _ = _sgu_ln(_gelu(zv_ref[...].astype(F32)), vg_ref[...], vb_ref[...])
        vn = vn.astype(BF16)
        mask = _chunk_mask()
        for g in range(SGG):
            w = jnp.where(mask, ws_ref[g], 0.0).astype(BF16)
            for b in range(tr // SGB):
                rs, cs = slice(b * SGB, (b + 1) * SGB), slice(g * SGC, (g + 1) * SGC)
                mixed = _dot(w, vn[rs, cs]) + bs_ref[:, g:g + 1]
                y_ref[rs, cs] = (u[rs, cs] * mixed).astype(BF16)
        yo = _dot(y_ref[...], wo_ref[...])
        xo_ref[...] = x_ref[...] + gt_ref[...] * yo
        yo_ref[...] = yo.astype(BF16)

    vec = pl.BlockSpec((1, SGW), lambda i: (0, 0))
    row = pl.BlockSpec((tr, D), lambda i: (i, 0))
    return pl.pallas_call(
        body, name=name, grid=(s // tr,),
        in_specs=[pl.BlockSpec((tr, SGW), lambda i: (i, 0)), pl.BlockSpec((tr, SGW), lambda i: (i, 1)),
                  vec, vec, pl.BlockSpec((SGG, SGB, SGB), lambda i: (0, 0, 0)),
                  pl.BlockSpec((SGB, LANES), lambda i: (0, 0)), pl.BlockSpec((SGW, D), lambda i: (0, 0)), row,
                  pl.BlockSpec((1, D), lambda i: (0, 0))],
        out_specs=[row, row, pl.BlockSpec((tr, SGW), lambda i: (i, 0))],
        out_shape=[jax.ShapeDtypeStruct((s, D), F32), jax.ShapeDtypeStruct((s, D), BF16),
                   jax.ShapeDtypeStruct((s, SGW), BF16)],
        compiler_params=_cparams(("arbitrary",)),
    )(z, z, vgain, vbias, ws, bst, w_out, xin, gate)


def _sgu_bwd(z, dy, vgain, vbias, ws, wst, bst, tr, name):
    s = z.shape[0]

    def body(zu_ref, zv_ref, dy_ref, vg_ref, vb_ref, ws_ref, wst_ref, bs_ref,
             dz_ref, rb_ref, rv_ref, dws_ref, dbs_ref, dvn_s):
        @pl.when(pl.program_id(0) == 0)
        def _():
            rb_ref[...] = jnp.zeros_like(rb_ref)
            rv_ref[...] = jnp.zeros_like(rv_ref)
            dws_ref[...] = jnp.zeros_like(dws_ref)
            dbs_ref[...] = jnp.zeros_like(dbs_ref)
        zu = zu_ref[...].astype(F32)
        zv = zv_ref[...].astype(F32)
        u, gu = _gelu_and_grad(zu)
        v, gv = _gelu_and_grad(zv)
        vn, vhat, rstd = _sgu_ln(v, vg_ref[...], vb_ref[...])
        vnb = vn.astype(BF16)
        dyv = dy_ref[...].astype(F32)
        dmix = (dyv * u).astype(BF16)
        mask = _chunk_mask()
        mask_t = _chunk_mask(transposed=True)
        lane = _lane((SGB, LANES))
        dbs = jnp.zeros((SGB, LANES), F32)
        for g in range(SGG):
            w = jnp.where(mask, ws_ref[g], 0.0).astype(BF16)
            wt = jnp.where(mask_t, wst_ref[g], 0.0).astype(BF16)
            dw = jnp.zeros((SGB, SGB), F32)
            for b in range(tr // SGB):
                rs, cs = slice(b * SGB, (b + 1) * SGB), slice(g * SGC, (g + 1) * SGC)
                mixed = _dot(w, vnb[rs, cs]) + bs_ref[:, g:g + 1]
                dz_ref[rs, cs] = (dyv[rs, cs] * mixed * gu[rs, cs]).astype(BF16)
                dm = dmix[rs, cs]
                dw = dw + _dot_nt(dm, vnb[rs, cs])
                dbs = dbs + jnp.where(lane == g, jnp.sum(dm.astype(F32), axis=-1, keepdims=True), 0.0)
                dvn_s[rs, cs] = _dot(wt, dm)
            dws_ref[g] += jnp.where(mask, dw, 0.0)
        dbs_ref[...] += dbs
        dvn = dvn_s[...]
        rv_ref[0:1, :] += jnp.sum(dvn * vhat, axis=0, keepdims=True)
        rv_ref[1:2, :] += jnp.sum(dvn, axis=0, keepdims=True)
        dvh = dvn * vg_ref[...]
        dv = rstd * (dvh - jnp.mean(dvh, axis=-1, keepdims=True)
                     - vhat * jnp.mean(dvh * vhat, axis=-1, keepdims=True))
        dz_ref[:, SGW:] = (dv * gv).astype(BF16)
        dzf = dz_ref[...].astype(F32)
        rb_ref[0:1, :] += jnp.sum(dzf, axis=0, keepdims=True)

    vec = pl.BlockSpec((1, SGW), lambda i: (0, 0))
    wsp = pl.BlockSpec((SGG, SGB, SGB), lambda i: (0, 0, 0))
    return pl.pallas_call(
        body, name=name, grid=(s // tr,),
        in_specs=[pl.BlockSpec((tr, SGW), lambda i: (i, 0)), pl.BlockSpec((tr, SGW), lambda i: (i, 1)),
                  pl.BlockSpec((tr, SGW), lambda i: (i, 0)), vec, vec, wsp, wsp,
                  pl.BlockSpec((SGB, LANES), lambda i: (0, 0))],
        out_specs=[pl.BlockSpec((tr, 2 * SGW), lambda i: (i, 0)),
                   pl.BlockSpec((8, 2 * SGW), lambda i: (0, 0)),
                   pl.BlockSpec((8, SGW), lambda i: (0, 0)), wsp,
                   pl.BlockSpec((SGB, LANES), lambda i: (0, 0))],
        out_shape=[jax.ShapeDtypeStruct((s, 2 * SGW), BF16), jax.ShapeDtypeStruct((8, 2 * SGW), F32),
                   jax.ShapeDtypeStruct((8, SGW), F32), jax.ShapeDtypeStruct((SGG, SGB, SGB), F32),
                   jax.ShapeDtypeStruct((SGB, LANES), F32)],
        scratch_shapes=[pltpu.VMEM((tr, SGW), F32)],
        compiler_params=_cparams(("arbitrary",)),
    )(z, z, dy, vgain, vbias, ws, wst, bst)


def _final_loss(x, fg, tgt, gprev, yprev, ts, name):
    s, d = x.shape

    def body(x_ref, fg_ref, t_ref, g_ref, y_ref, l_ref, dx_ref, dy_ref, red_ref):
        @pl.when(pl.program_id(0) == 0)
        def _():
            l_ref[...] = jnp.zeros_like(l_ref)
            red_ref[...] = jnp.zeros_like(red_ref)
        xv = x_ref[...]
        r = _rstd_rows(xv)
        xh = xv * r
        err = xh * fg_ref[...] - t_ref[...]
        l_ref[...] += 0.5 * jnp.sum(jnp.mean(err * err, axis=-1, keepdims=True))
        dyo = err * (1.0 / d)
        dxh = dyo * fg_ref[...]
        dx = r * (dxh - xh * jnp.mean(dxh * xh, axis=-1, keepdims=True))
        dx_ref[...] = dx
        dy_ref[...] = (dx * g_ref[...]).astype(BF16)
        red_ref[0:1, :] += jnp.sum(dyo * xh, axis=0, keepdims=True)
        red_ref[1:2, :] += jnp.sum(dx * y_ref[...].astype(F32), axis=0, keepdims=True)

    row = pl.BlockSpec((ts, d), lambda i: (i, 0))
    vec = pl.BlockSpec((1, d), lambda i: (0, 0))
    return pl.pallas_call(
        body, name=name, grid=(s // ts,),
        in_specs=[row, vec, row, vec, row],
        out_specs=[pl.BlockSpec((8, LANES), lambda i: (0, 0)), row, row, pl.BlockSpec((8, d), lambda i: (0, 0))],
        out_shape=[jax.ShapeDtypeStruct((8, LANES), F32), jax.ShapeDtypeStruct((s, d), F32),
                   jax.ShapeDtypeStruct((s, d), BF16), jax.ShapeDtypeStruct((8, d), F32)],
        compiler_params=_cparams(("arbitrary",)),
    )(x, fg, tgt, gprev, yprev)


def _norm_bwd(xin, dh, dxout, ng, sc, gprev, yprev, ts, name, css=()):
    s, d = xin.shape
    has_prev = gprev is not None
    fused = isinstance(dh, tuple)
    na = len(css)
    if fused:
        a, w, tk = dh
        ka, kw = a.shape[-1], w.shape[-1]
        k = ka * (a.shape[0] if a.ndim == 3 else 1)

    def body(*refs):
        if fused:
            x_ref, a_ref, w_ref, dxo_ref, ng_ref, sc_ref = refs[:6]
            rest = refs[6:]
        else:
            x_ref, dh_ref, dxo_ref, ng_ref, sc_ref = refs[:5]
            rest = refs[5:]
        if has_prev:
            g_ref, y_ref = rest[:2]
            rest = rest[2:]
        cs_refs, rest = rest[:na], rest[na:]
        if has_prev:
            dx_ref, dy_ref, red_ref = rest[:3]
            rest = rest[3:]
        else:
            dx_ref, red_ref = rest[:2]
            rest = rest[2:]
        r_refs, sems = rest[:na], rest[na:]
        if na:
            @pl.when(pl.program_id(0) == 0)
            def _():
                for cp in _chip_exchange_copies(cs_refs, r_refs, *sems):
                    cp.start()

        @pl.when(pl.program_id(0) == 0)
        def _():
            red_ref[...] = jnp.zeros_like(red_ref)
        if fused:
            dhv = None
            for g0 in range(0, k, tk):
                av = a_ref[g0 // ka, :, g0 % ka:g0 % ka + tk] if a.ndim == 3 else a_ref[:, g0:g0 + tk]
                wv = w_ref[g0 // kw, :, g0 % kw:g0 % kw + tk] if w.ndim == 3 else w_ref[:, g0:g0 + tk]
                part = _dot_nt(av, wv)
                dhv = part if dhv is None else dhv + part
        else:
            dhv = dh_ref[...]
        xv = x_ref[...]
        r = _rstd_rows(xv)
        xh = xv * r
        dr = dhv * (1.0 + sc_ref[...])
        t = dr * ng_ref[...]
        dx = dxo_ref[...] + r * (t - xh * jnp.mean(t * xh, axis=-1, keepdims=True))
        dx_ref[...] = dx
        red_ref[0:1, :] += jnp.sum(dhv, axis=0, keepdims=True)
        red_ref[1:2, :] += jnp.sum(dhv * (xh * ng_ref[...]), axis=0, keepdims=True)
        red_ref[2:3, :] += jnp.sum(dr * xh, axis=0, keepdims=True)
        if has_prev:
            dy_ref[...] = (dx * g_ref[...]).astype(BF16)
            red_ref[3:4, :] += jnp.sum(dx * y_ref[...].astype(F32), axis=0, keepdims=True)
        if na:
            @pl.when(pl.program_id(0) == s // ts - 1)
            def _():
                for cp in _chip_exchange_copies(cs_refs, r_refs, *sems):
                    cp.wait()

    row = pl.BlockSpec((ts, d), lambda i: (i, 0))
    vec = pl.BlockSpec((1, d), lambda i: (0, 0))
    red = pl.BlockSpec((8, d), lambda i: (0, 0))
    if fused:
        a_spec = (pl.BlockSpec((a.shape[0], ts, ka), lambda i: (0, i, 0)) if a.ndim == 3
                  else pl.BlockSpec((ts, ka), lambda i: (i, 0)))
        w_spec = pl.BlockSpec(w.shape, (lambda i: (0, 0, 0)) if w.ndim == 3 else (lambda i: (0, 0)))
        dh_specs, dh_args = [a_spec, w_spec], (a, w)
    else:
        dh_specs, dh_args = [row], (dh,)
    if has_prev:
        in_specs, args = [row] + dh_specs + [row, vec, vec, vec, row], (xin,) + dh_args + (dxout, ng, sc, gprev, yprev)
        out_specs = [row, row, red]
        out_shape = [jax.ShapeDtypeStruct((s, d), F32), jax.ShapeDtypeStruct((s, d), BF16),
                     jax.ShapeDtypeStruct((8, d), F32)]
    else:
        in_specs, args = [row] + dh_specs + [row, vec, vec], (xin,) + dh_args + (dxout, ng, sc)
        out_specs = [row, red]
        out_shape = [jax.ShapeDtypeStruct((s, d), F32), jax.ShapeDtypeStruct((8, d), F32)]
    return pl.pallas_call(
        body, name=name, grid=(s // ts,), in_specs=in_specs + [HBM_SPEC] * na,
        out_specs=out_specs + [HBM_SPEC] * na,
        out_shape=out_shape + [jax.ShapeDtypeStruct((3,) + cs.shape[1:], cs.dtype) for cs in css],
        scratch_shapes=[pltpu.SemaphoreType.DMA((3 * na,))] * 2 if na else [],
        compiler_params=_cparams(("arbitrary",)),
    )(*args, *css)


def _ada_mod(c_all, ada_w, ada_b):
    nb = c_all.shape[0]
    da = ada_w.shape[2]

    def body(c_ref, w_ref, b_ref, o_ref, ca_ref):
        cv = c_ref[...]
        ca = cv * _sigmoid(cv)
        ca_ref[...] = ca
        o_ref[0] = lax.dot_general(ca, w_ref[0], (((1,), (0,)), ((), ())), precision=lax.Precision.HIGHEST,
                                   preferred_element_type=F32) + b_ref[0]

    return pl.pallas_call(
        body, name="ada_mod", grid=(2,),
        in_specs=[pl.BlockSpec((nb, D), lambda i: (0, 0)), pl.BlockSpec((1, D, da), lambda i: (i, 0, 0)),
                  pl.BlockSpec((1, 1, da), lambda i: (i, 0, 0))],
        out_specs=[pl.BlockSpec((1, nb, da), lambda i: (i, 0, 0)), pl.BlockSpec((nb, D), lambda i: (0, 0))],
        out_shape=[jax.ShapeDtypeStruct((2, nb, da), F32), jax.ShapeDtypeStruct((nb, D), F32)],
        compiler_params=_cparams(("arbitrary",)),
    )(c_all, ada_w, ada_b)


def _ada_w_grad(c_act_t, dmod):
    nb = c_act_t.shape[1]
    da = dmod.shape[2]
    tn = 512

    def body(c_ref, d_ref, o_ref):
        acc = c_ref[:, 0:1] * d_ref[0, 0:1, :]
        for b in range(1, nb):
            acc = acc + c_ref[:, b:b + 1] * d_ref[0, b:b + 1, :]
        o_ref[0] = acc

    return pl.pallas_call(
        body, name="ada_w_grad", grid=(2, da // tn),
        in_specs=[pl.BlockSpec((D, nb), lambda i, j: (0, 0)), pl.BlockSpec((1, nb, tn), lambda i, j: (i, 0, j))],
        out_specs=pl.BlockSpec((1, D, tn), lambda i, j: (i, 0, j)),
        out_shape=jax.ShapeDtypeStruct((2, D, da), F32),
        compiler_params=_cparams(("arbitrary", "arbitrary")),
    )(c_act_t, dmod)


def _conv_planes(cw, cb):
    cwp = jnp.swapaxes(cw.reshape(3, 2, DFF), 0, 1)
    return jnp.pad(cwp, ((0, 0), (0, 5), (0, 0))), cb.reshape(2, 1, DFF)


def _local_step(x, tgt, mod, wts, small, comm=None):
    wts = dict(wts)
    s = x.shape[0]
    ts, tq, tr, tp = TS, TQ, TR, TP
    tkw = min(TKW, s)
    tf = min(256, s)
    zb = lambda n: jnp.zeros((1, n), F32)
    m6 = mod.reshape(2, 6, 1, D)
    sh1, sc1, g1, sh2, sc2, g2 = ([m6[i, k] for i in range(2)] for k in range(6))
    n1g, n2g = small["norm1_g"], small["norm2_g"]
    row = lambda a, i: a[i:i + 1]

    qg2 = jnp.tile(small["fox_q_gain"], (1, 2))
    kg2 = jnp.tile(small["fox_k_gain"], (1, 2))
    bfp = jnp.pad(small["fox_b_f"], ((0, 0), (0, LANES - H)))
    proj, h1 = _norm_mod_matmul(x, row(n1g, 0), sc1[0], sh1[0], wts["fox_w_in"], zb(FOX_NP), F32, ts, 1408, "fox_in")
    qa, ka, va = _fox_post(proj, qg2, kg2, bfp, tp, "fox_post")
    att, ql, gathered = _attn_fwd(qa, ka, va, tq, "attn_fwd", shards=comm["shards"] if comm else ())
    if comm:
        wts.update(comm["make_wts"](gathered))
    x1, y0, gated = _gate_out(att, proj, wts["fox_w_out"], x, g1[0], ts, "fox_gate_out")

    def ffn_fwd(xin, i, tag):
        cw, cb = _conv_planes(small["ffn_conv_w"][i], small["ffn_conv_b"][i])
        a, h = _norm_mod_matmul(xin, row(n2g, i), sc2[i], sh2[i], wts["ffn_w_up"][i], zb(2 * DFF), BF16, ts, 1408,
                                "ffn_up" + tag, planes=2)
        xo, y, f, ap = _conv_down(a, cw, cb, wts["ffn_w_down"][i], xin, g2[i], min(256, s), "ffn_conv_down" + tag)
        return xo, (a, h, f, y, cw, ap)

    x2, ffn0 = ffn_fwd(x1, 0, "0")

    bst = jnp.pad(small["sgu_b_s"].T, ((0, 0), (0, LANES - SGG)))
    ws = small["sgu_w_s"]
    z, h3 = _norm_mod_matmul(x2, row(n1g, 1), sc1[1], sh1[1], wts["sgu_w_in"], small["sgu_b_in"], BF16, ts, 1024,
                             "sgu_in")
    x3, y1, yy = _sgu_fwd(z, small["sgu_v_gain"], small["sgu_v_bias"], ws, bst, wts["sgu_w_out"], x2, g1[1], tr,
                          "sgu_mix_out")
    x4, ffn1 = ffn_fwd(x3, 1, "1")

    lsum, dx4, dy, redf = _final_loss(x4, small["final_g"], tgt, g2[1], ffn1[3], ts, "final_loss")
    grads = {"final_g": redf[0]}
    dmod = [[None] * 6, [None] * 6]
    dmod[1][5] = redf[1]

    def ffn_bwd(dxo, dy2, xin, i, saved, gprev, yprev, tag):
        a, h, f, _, cw, ap = saved
        wd, wu = wts["ffn_w_down"][i], wts["ffn_w_up"][i]
        g_wd = _matmul(f, dy2, True, False, 1408, D, tkw, BF16, "ffn_dwdown" + tag)
        da, redc = _down_bwd_conv(dy2, wd, a, ap, cw, min(256, s), "ffn_down_bwd_conv" + tag)
        g_wu = _matmul(h, da, True, False, D, 1408, tkw, BF16, "ffn_dwup" + tag, out_parts=N_CHIP)
        outs = _norm_bwd(xin, (da, wu, 1408), dxo, row(n2g, i), sc2[i], gprev, yprev, tf, "ffn_dh_norm_bwd" + tag)
        return outs, g_wd, g_wu, redc

    (dx3, dy1, red), g_wd1, g_wu1, redc1 = ffn_bwd(dx4, dy, x3, 1, ffn1, g1[1], y1, "1")
    dmod[1][3], dmod[1][4], dn2g1, dmod[1][2] = red[0], red[1], red[2], red[3]

    g_swo = _matmul(yy, dy1, True, False, 1024, D, tkw, BF16, "sgu_dwout")
    dyy = _matmul_wt(dy1, wts["sgu_w_out"], 1024, D, BF16, ts, "sgu_dyy")
    wst = jnp.swapaxes(ws, 1, 2)
    dz, rb, rv, dws, dbst = _sgu_bwd(z, dyy, small["sgu_v_gain"], small["sgu_v_bias"], ws, wst, bst, tr, "sgu_mix_bwd")
    g_swi = _matmul(h3, dz, True, False, D, 1024, tkw, BF16, "sgu_dwin", out_parts=N_CHIP)
    dx2, dy2_0, red = _norm_bwd(x2, (dz, wts["sgu_w_in"], 1024), dx3, row(n1g, 1), sc1[1], g2[0], ffn0[3], tf,
                                "sgu_dh_norm_bwd")
    dmod[1][0], dmod[1][1], dn1g1, dmod[0][5] = red[0], red[1], red[2], red[3]

    (dx1, dy0, red), g_wd0, g_wu0, redc0 = ffn_bwd(dx2, dy2_0, x1, 0, ffn0, g1[0], y0, "0")
    dmod[0][3], dmod[0][4], dn2g0, dmod[0][2] = red[0], red[1], red[2], red[3]

    g_fwo = _matmul(gated, dy0, True, False, D, D, tkw, BF16, "fox_dwout")
    doa, dop = _attn_bwd_prep(dy0, wts["fox_w_out"], att, proj, ts, "attn_bwd_prep")
    css = comm["rs_prepare"]([g_fwo, g_swi, g_swo, g_wu0, g_wu1, g_wd0, g_wd1]) if comm else []
    dqa, dka, dva, rcvs = _attn_bwd(ql, ka, va, doa, tq, "attn_bwd", css=css)
    dproj, redx = _fox_post_bwd(proj, dqa, dka, dva, dop, qg2, kg2, bfp, tp, "fox_post_bwd")
    g_fwi = _matmul(h1, dproj, True, False, D, 1408, tkw, BF16, "fox_dwin")
    css_fox = comm["rs_prepare_fox"](g_fwi) if comm else []
    outs = _norm_bwd(x, (dproj, wts["fox_w_in"], 1408), dx1, row(n1g, 0), sc1[0], None, None, tf,
                     "fox_dh_norm_bwd", css=css_fox)
    dx0, red = outs[0], outs[1]
    css, rcvs = list(css_fox) + list(css), list(outs[2:]) + list(rcvs)
    dmod[0][0], dmod[0][1], dn1g0 = red[0], red[1], red[2]

    grads.update(
        fox_w_in=g_fwi, fox_w_out=g_fwo, sgu_w_in=g_swi, sgu_w_out=g_swo,
        ffn_w_up=[g_wu0, g_wu1], ffn_w_down=[g_wd0, g_wd1],
        fox_q_gain=redx[0, :DH] + redx[0, DH:], fox_k_gain=redx[1, :DH] + redx[1, DH:], fox_b_f=redx[2, :H],
        sgu_b_in=rb[0], sgu_v_gain=rv[0], sgu_v_bias=rv[1], sgu_w_s=dws, sgu_b_s=dbst[:, :SGG].T,
        ffn_conv_w=jnp.stack([jnp.swapaxes(r[:, 0:3], 0, 1).reshape(3, 2 * DFF) for r in (redc0, redc1)]),
        ffn_conv_b=jnp.stack([r[:, 3].reshape(2 * DFF) for r in (redc0, redc1)]),
        norm1_g=jnp.stack([dn1g0, dn1g1]), norm2_g=jnp.stack([dn2g0, dn2g1]),
    )
    dmod_arr = jnp.stack([jnp.concatenate(dmod[0]), jnp.concatenate(dmod[1])])
    return lsum[0, 0], dx0, grads, dmod_arr, (css, rcvs)


N_DEV = 8
N_CHIP = 4
HBM_SPEC = pl.BlockSpec(memory_space=pltpu.HBM)
VMEM_SPEC = pl.BlockSpec(memory_space=pltpu.VMEM)


def _mesh_pos():
    return lax.axis_index("x"), lax.axis_index("y"), lax.axis_index("c")


def _other_chips(x, y):
    return [(1 - x, y), (x, 1 - y), (1 - x, 1 - y)]


def _remote(src, dst, ssem, rsem, dev):
    return pltpu.make_async_remote_copy(src_ref=src, dst_ref=dst, send_sem=ssem, recv_sem=rsem,
                                        device_id=dev, device_id_type=MESH)


def _allgather8(xb, name):
    m_per, n = xb.shape

    def body(x_ref, out_ref, send_sems, recv_sems, local_sem):
        x, y, c = _mesh_pos()
        me, sibling = (x, y, c), (x, y, 1 - c)
        chips = _other_chips(x, y)

        def rows(px, py, pc):
            return out_ref.at[pl.ds((4 * px + 2 * py + pc) * m_per, m_per), :]

        def copy(k, block, to, src=None):
            return _remote(rows(*block) if src is None else src, rows(*block),
                           send_sems.at[k], recv_sems.at[k], to)

        mine = pltpu.make_async_copy(x_ref, rows(*me), local_sem)
        mine.start()
        first = [copy(0, me, sibling, src=x_ref)]
        first += [copy(1 + j, me, (*chip, c), src=x_ref) for j, chip in enumerate(chips)]
        for cp in first:
            cp.start()
        passed = [copy(4 + j, (*chip, c), sibling) for j, chip in enumerate(chips)]
        for j, chip in enumerate(chips):
            copy(1 + j, (*chip, c), me).wait_recv()
            passed[j].start()
        copy(0, sibling, me).wait_recv()
        for j, chip in enumerate(chips):
            copy(4 + j, (*chip, 1 - c), me).wait_recv()
        for cp in first + passed:
            cp.wait_send()
        mine.wait()

    return pl.pallas_call(
        body, name=name,
        out_shape=jax.ShapeDtypeStruct((N_DEV * m_per, n), xb.dtype),
        in_specs=[VMEM_SPEC], out_specs=VMEM_SPEC,
        scratch_shapes=[pltpu.SemaphoreType.DMA((7,)), pltpu.SemaphoreType.DMA((7,)), pltpu.SemaphoreType.DMA],
        compiler_params=pltpu.CompilerParams(vmem_limit_bytes=V7X_VMEM_LIMIT),
    )(xb)


def _gather_shards(shards, name):
    na = len(shards)

    def body(*refs):
        p_refs, o_refs = refs[:na], refs[na:2 * na]
        send_sems, recv_sems, pass_send, pass_recv = refs[2 * na:]
        x, y, c = _mesh_pos()
        me = 2 * x + y
        sibling = (x, y, 1 - c)
        chips = _other_chips(x, y)

        def half(a, ci, hf):
            rh = shards[a].shape[0] // 2
            return o_refs[a].at[ci, pl.ds(hf * rh, rh), :]

        sends = []
        for a in range(na):
            rh = shards[a].shape[0] // 2
            for k, chip in enumerate(chips):
                sends.append(_remote(p_refs[a].at[pl.ds(c * rh, rh), :], half(a, me, c),
                                     send_sems.at[3 * a + k], recv_sems.at[3 * a + k], (*chip, c)))
        for cp in sends:
            cp.start()
        passed = []
        for a in range(na):
            for k, chip in enumerate(chips):
                ci = 2 * chip[0] + chip[1]
                _remote(half(a, ci, c), half(a, ci, c), send_sems.at[3 * a + k], recv_sems.at[3 * a + k],
                        (*chip, c)).wait_recv()
                cp = _remote(half(a, ci, c), half(a, ci, c), pass_send.at[3 * a + k], pass_recv.at[3 * a + k], sibling)
                cp.start()
                passed.append(cp)
        for a in range(na):
            for k, chip in enumerate(chips):
                ci = 2 * chip[0] + chip[1]
                _remote(half(a, ci, 1 - c), half(a, ci, 1 - c), pass_send.at[3 * a + k], pass_recv.at[3 * a + k],
                        sibling).wait_recv()
        for cp in sends + passed:
            cp.wait_send()

    return pl.pallas_call(
        body, name=name,
        out_shape=[jax.ShapeDtypeStruct((N_CHIP,) + p.shape, p.dtype) for p in shards],
        in_specs=[HBM_SPEC] * na, out_specs=[HBM_SPEC] * na,
        scratch_shapes=[pltpu.SemaphoreType.DMA((3 * na,))] * 4,
    )(*shards)


def _rs_to_sibling(gs, name):
    na = len(gs)

    def body(*refs):
        g_refs, o_refs, ssems, rsems = refs[:na], refs[na:2 * na], refs[2 * na], refs[2 * na + 1]
        x, y, c = _mesh_pos()
        cps = []
        for a in range(na):
            rh = gs[a].shape[1] // 2
            cp = _remote(g_refs[a].at[:, pl.ds((1 - c) * rh, rh), :], o_refs[a], ssems.at[a], rsems.at[a],
                         (x, y, 1 - c))
            cp.start()
            cps.append(cp)
        for cp in cps:
            cp.wait()

    return pl.pallas_call(
        body, name=name,
        out_shape=[jax.ShapeDtypeStruct((g.shape[0], g.shape[1] // 2, g.shape[2]), g.dtype) for g in gs],
        in_specs=[HBM_SPEC] * na, out_specs=[HBM_SPEC] * na,
        scratch_shapes=[pltpu.SemaphoreType.DMA((na,)), pltpu.SemaphoreType.DMA((na,))],
    )(*gs)


def _rs_chip_sum(g, sib, c_arr, tr, name):
    nc, r, n = g.shape
    rh = r // 2
    g4 = g.reshape(nc, 2, rh, n)

    def body(c_ref, g_ref, s_ref, o_ref):
        o_ref[...] = (g_ref[0].astype(F32) + s_ref[...].astype(F32)).astype(BF16)

    return pl.pallas_call(
        body, name=name, out_shape=jax.ShapeDtypeStruct((nc, rh, n), BF16),
        grid_spec=pltpu.PrefetchScalarGridSpec(
            num_scalar_prefetch=1, grid=(nc, rh // tr),
            in_specs=[pl.BlockSpec((1, 1, tr, n), lambda j, i, cr: (j, cr[0], i, 0)),
                      pl.BlockSpec((1, tr, n), lambda j, i, cr: (j, i, 0))],
            out_specs=pl.BlockSpec((1, tr, n), lambda j, i, cr: (j, i, 0))),
        compiler_params=_cparams(("arbitrary", "arbitrary")),
    )(c_arr, g4, sib)


def _rs_final_sum(cs, rcv, me_arr, tr, name):
    nc, rh, n = cs.shape

    def body(m_ref, c_ref, r_ref, o_ref):
        acc = c_ref[0].astype(F32)
        for k in range(3):
            acc = acc + r_ref[k].astype(F32)
        o_ref[...] = acc

    return pl.pallas_call(
        body, name=name, out_shape=jax.ShapeDtypeStruct((rh, n), F32),
        grid_spec=pltpu.PrefetchScalarGridSpec(
            num_scalar_prefetch=1, grid=(rh // tr,),
            in_specs=[pl.BlockSpec((1, tr, n), lambda i, mr: (mr[0], i, 0)),
                      pl.BlockSpec((3, tr, n), lambda i, mr: (0, i, 0))],
            out_specs=pl.BlockSpec((tr, n), lambda i, mr: (i, 0))),
        compiler_params=_cparams(("arbitrary",)),
    )(me_arr, cs, rcv)


def _rs_swap_halves(halves, name):
    na = len(halves)

    def body(*refs):
        h_refs, o_refs, ssems, rsems = refs[:na], refs[na:2 * na], refs[2 * na], refs[2 * na + 1]
        x, y, c = _mesh_pos()
        cps = []
        for a in range(na):
            cp = _remote(h_refs[a], o_refs[a], ssems.at[a], rsems.at[a], (x, y, 1 - c))
            cp.start()
            cps.append(cp)
        for cp in cps:
            cp.wait()

    return pl.pallas_call(
        body, name=name, out_shape=[jax.ShapeDtypeStruct(h.shape, h.dtype) for h in halves],
        in_specs=[HBM_SPEC] * na, out_specs=[HBM_SPEC] * na,
        scratch_shapes=[pltpu.SemaphoreType.DMA((na,)), pltpu.SemaphoreType.DMA((na,))],
    )(*halves)


def _join_columns(parts, n_out, name):
    p, k, c = parts.shape
    tr = 128

    def body(w_ref, o_ref):
        for j in range(p):
            o_ref[:, j * c:(j + 1) * c] = w_ref[j]
        o_ref[:, p * c:] = jnp.zeros((tr, n_out - p * c), parts.dtype)

    return pl.pallas_call(
        body, name=name, grid=(k // tr,),
        in_specs=[pl.BlockSpec((p, tr, c), lambda i: (0, i, 0))],
        out_specs=pl.BlockSpec((tr, n_out), lambda i: (i, 0)),
        out_shape=jax.ShapeDtypeStruct((k, n_out), parts.dtype),
        compiler_params=_cparams(("arbitrary",)),
    )(parts)


def _split_columns(g, p, c, name):
    k, n = g.shape
    tr = 128

    def body(g_ref, o_ref):
        for j in range(p):
            o_ref[j] = g_ref[:, j * c:(j + 1) * c]

    return pl.pallas_call(
        body, name=name, grid=(k // tr,),
        in_specs=[pl.BlockSpec((tr, n), lambda i: (i, 0))],
        out_specs=pl.BlockSpec((p, tr, c), lambda i: (0, i, 0)),
        out_shape=jax.ShapeDtypeStruct((p, k, c), g.dtype),
        compiler_params=_cparams(("arbitrary",)),
    )(g)


def _sum8(g, name):
    nd, r, n = g.shape

    def body(g_ref, o_ref):
        acc = g_ref[0]
        for k in range(1, nd):
            acc = acc + g_ref[k]
        o_ref[...] = acc

    return pl.pallas_call(
        body, name=name, grid=(r // 8,),
        in_specs=[pl.BlockSpec((nd, 8, n), lambda i: (0, i, 0))],
        out_specs=pl.BlockSpec((8, n), lambda i: (i, 0)),
        out_shape=jax.ShapeDtypeStruct((r, n), F32),
        compiler_params=_cparams(("arbitrary",)),
    )(g)


def _adamw(w, g, m, v, name):
    r, n = w.shape
    tr = next(t for t in (128, 64, 32, 16, 8) if r % t == 0)
    bc1 = 1.0 - ADAM_B1 ** ADAM_STEP
    bc2 = 1.0 - ADAM_B2 ** ADAM_STEP

    def body(w_ref, g_ref, m_ref, v_ref, d_ref, mo_ref, vo_ref):
        gv = g_ref[...]
        mn = ADAM_B1 * m_ref[...] + (1.0 - ADAM_B1) * gv
        vn = ADAM_B2 * v_ref[...] + (1.0 - ADAM_B2) * (gv * gv)
        d_ref[...] = -ADAM_LR * ((mn / bc1) / (jnp.sqrt(vn / bc2) + ADAM_EPS) + ADAM_WD * w_ref[...])
        mo_ref[...] = mn
        vo_ref[...] = vn

    blk = pl.BlockSpec((tr, n), lambda i: (i, 0))
    shp = jax.ShapeDtypeStruct((r, n), F32)
    return pl.pallas_call(
        body, name=name, grid=(r // tr,), in_specs=[blk] * 4, out_specs=[blk] * 3, out_shape=[shp] * 3,
        compiler_params=_cparams(("arbitrary",)),
    )(w, g, m, v)


ROW = 1024
BIG = ("fox_w_in", "fox_w_out", "sgu_w_in", "sgu_w_out", "ffn_w_up", "ffn_w_down")
SMALL_SHARDED = ("sgu_b_in", "sgu_v_gain", "sgu_v_bias", "ffn_conv_w")
SMALL_REPL = ("fox_b_f", "fox_q_gain", "fox_k_gain", "sgu_w_s", "sgu_b_s", "ffn_conv_b", "ada_b",
              "norm1_g", "norm2_g", "final_g")
WEIGHTS = ("fox_w_in", "fox_b_f", "fox_q_gain", "fox_k_gain", "fox_w_out", "sgu_w_in", "sgu_b_in", "sgu_v_gain",
           "sgu_v_bias", "sgu_w_s", "sgu_b_s", "sgu_w_out", "ffn_w_up", "ffn_conv_w", "ffn_conv_b", "ffn_w_down",
           "ada_w", "ada_b", "norm1_g", "norm2_g", "final_g")


def _rows_of(a, mult=1):
    flat = a.reshape(-1)
    rows = -(-flat.shape[0] // ROW)
    rows = -(-rows // mult) * mult
    return jnp.pad(flat, (0, rows * ROW - flat.shape[0])).reshape(rows, ROW)


def _pack(parts, mult, total=None):
    p = jnp.concatenate([_rows_of(a, mult) for a in parts], axis=0)
    if total is not None:
        p = jnp.pad(p, ((0, total - p.shape[0]), (0, 0)))
    return p


def _unpack(pack, shapes, mult):
    out, r0 = [], 0
    for shp in shapes:
        size = int(np.prod(shp))
        rows = -(-(-(-size // ROW)) // mult) * mult
        out.append(pack[r0:r0 + rows].reshape(-1)[:size].reshape(shp))
        r0 += rows
    return out


def _big_shards(t):
    return [t["fox_w_in"][0], t["fox_w_out"][0], t["sgu_w_in"][0], t["sgu_w_out"][0],
            t["ffn_w_up"][0], t["ffn_w_up"][1], t["ffn_w_down"][0], t["ffn_w_down"][1]]


def _row_tile(rows):
    return next(t for t in (512, 352, 256, 128, 64) if rows % t == 0)


def kernel(x, c, fox_w_in, fox_b_f, fox_q_gain, fox_k_gain, fox_w_out, sgu_w_in, sgu_b_in, sgu_v_gain, sgu_v_bias, sgu_w_s, sgu_b_s, sgu_w_out, ffn_w_up, ffn_conv_w, ffn_conv_b, ffn_w_down, ada_w, ada_b, norm1_g, norm2_g, final_g, loss_target, m_fox_w_in, m_fox_b_f, m_fox_q_gain, m_fox_k_gain, m_fox_w_out, m_sgu_w_in, m_sgu_b_in, m_sgu_v_gain, m_sgu_v_bias, m_sgu_w_s, m_sgu_b_s, m_sgu_w_out, m_ffn_w_up, m_ffn_conv_w, m_ffn_conv_b, m_ffn_w_down, m_ada_w, m_ada_b, m_norm1_g, m_norm2_g, m_final_g, v_fox_w_in, v_fox_b_f, v_fox_q_gain, v_fox_k_gain, v_fox_w_out, v_sgu_w_in, v_sgu_b_in, v_sgu_v_gain, v_sgu_v_bias, v_sgu_w_s, v_sgu_b_s, v_sgu_w_out, v_ffn_w_up, v_ffn_conv_w, v_ffn_conv_b, v_ffn_w_down, v_ada_w, v_ada_b, v_norm1_g, v_norm2_g, v_final_g):
    w = dict(fox_w_in=fox_w_in, fox_b_f=fox_b_f, fox_q_gain=fox_q_gain, fox_k_gain=fox_k_gain, fox_w_out=fox_w_out,
             sgu_w_in=sgu_w_in, sgu_b_in=sgu_b_in, sgu_v_gain=sgu_v_gain, sgu_v_bias=sgu_v_bias, sgu_w_s=sgu_w_s,
             sgu_b_s=sgu_b_s, sgu_w_out=sgu_w_out, ffn_w_up=ffn_w_up, ffn_conv_w=ffn_conv_w, ffn_conv_b=ffn_conv_b,
             ffn_w_down=ffn_w_down, ada_w=ada_w, ada_b=ada_b, norm1_g=norm1_g, norm2_g=norm2_g, final_g=final_g)
    mom = dict(fox_w_in=m_fox_w_in, fox_b_f=m_fox_b_f, fox_q_gain=m_fox_q_gain, fox_k_gain=m_fox_k_gain,
               fox_w_out=m_fox_w_out, sgu_w_in=m_sgu_w_in, sgu_b_in=m_sgu_b_in, sgu_v_gain=m_sgu_v_gain,
               sgu_v_bias=m_sgu_v_bias, sgu_w_s=m_sgu_w_s, sgu_b_s=m_sgu_b_s, sgu_w_out=m_sgu_w_out,
               ffn_w_up=m_ffn_w_up, ffn_conv_w=m_ffn_conv_w, ffn_conv_b=m_ffn_conv_b, ffn_w_down=m_ffn_w_down,
               ada_w=m_ada_w, ada_b=m_ada_b, norm1_g=m_norm1_g, norm2_g=m_norm2_g, final_g=m_final_g)
    var = dict(fox_w_in=v_fox_w_in, fox_b_f=v_fox_b_f, fox_q_gain=v_fox_q_gain, fox_k_gain=v_fox_k_gain,
               fox_w_out=v_fox_w_out, sgu_w_in=v_sgu_w_in, sgu_b_in=v_sgu_b_in, sgu_v_gain=v_sgu_v_gain,
               sgu_v_bias=v_sgu_v_bias, sgu_w_s=v_sgu_w_s, sgu_b_s=v_sgu_b_s, sgu_w_out=v_sgu_w_out,
               ffn_w_up=v_ffn_w_up, ffn_conv_w=v_ffn_conv_w, ffn_conv_b=v_ffn_conv_b, ffn_w_down=v_ffn_w_down,
               ada_w=v_ada_w, ada_b=v_ada_b, norm1_g=v_norm1_g, norm2_g=v_norm2_g, final_g=v_final_g)

    ax, ay, ac = _mesh_pos()
    chip = 2 * ax + ay
    dev = 2 * chip + ac

    small_shard_shapes = tuple(w[n].shape for n in SMALL_SHARDED)
    blk = _pack([c] + [w[n] for n in SMALL_SHARDED], 1, 16)
    gat = _allgather8(blk, "gather_small").reshape(N_DEV, 16, ROW)
    c_all = gat[:, 0, :]
    per_chip = [_unpack(gat[2 * j, 1:], small_shard_shapes, 1) for j in range(N_CHIP)]
    full_small = {n: jnp.concatenate([per_chip[j][i] for j in range(N_CHIP)], axis=-1)
                  for i, n in enumerate(SMALL_SHARDED)}

    mine = [a.astype(BF16) for a in _big_shards(w)]
    with_own = lambda gat, own: [lax.dynamic_update_slice(g_, m_[None], (chip, 0, 0)) for g_, m_ in zip(gat, own)]
    fwi, = with_own(_gather_shards(mine[:1], "gather_fox_w_in"), mine[:1])
    fwi_full = _join_columns(fwi, FOX_NP, "join_fox_w_in")
    wts = dict(fox_w_in=fwi_full)

    def make_wts(gathered):
        fwo, swi, swo, up0, up1, dn0, dn1 = with_own(gathered, mine[1:])
        return dict(fox_w_out=fwo.reshape(D, D), sgu_w_in=swi, sgu_w_out=swo.reshape(SGW, D),
                    ffn_w_up=[up0, up1], ffn_w_down=[dn0.reshape(DFF, D), dn1.reshape(DFF, D)])

    c_arr = jnp.reshape(ac, (1,)).astype(jnp.int32)
    me_arr = jnp.reshape(chip, (1,)).astype(jnp.int32)

    def chip_sums(glist, tag):
        sibs = _rs_to_sibling(glist, "rs_sibling" + tag)
        return [_rs_chip_sum(g_, s_, c_arr, _row_tile(s_.shape[1]), "rs_chip_sum%s%d" % (tag, a))
                for a, (g_, s_) in enumerate(zip(glist, sibs))]

    def rs_prepare(gl):
        g_fwo, g_swi, g_swo, g_wu0, g_wu1, g_wd0, g_wd1 = gl
        return chip_sums([g_fwo.reshape(N_CHIP, 256, D), g_swi, g_swo.reshape(N_CHIP, 512, D), g_wu0, g_wu1,
                          g_wd0.reshape(N_CHIP, 704, D), g_wd1.reshape(N_CHIP, 704, D)], "")

    def rs_prepare_fox(g_fwi):
        return chip_sums([_split_columns(g_fwi, N_CHIP, FOX_N // N_CHIP, "split_fox_w_in")], "_fox")

    comm = dict(shards=mine[1:], make_wts=make_wts, rs_prepare=rs_prepare, rs_prepare_fox=rs_prepare_fox)

    da = ada_w.shape[2]
    ada_b_cols = lax.dynamic_slice_in_dim(ada_b, chip * da, da, axis=1)[:, None, :]
    mod_cols, c_act = _ada_mod(c_all, ada_w, ada_b_cols)
    mod_all = _allgather8(mod_cols.reshape(-1, ROW), "gather_mod").reshape(N_DEV, 2, N_DEV, da)
    mod_mine = lax.dynamic_index_in_dim(mod_all[0::2], dev, axis=2, keepdims=False)
    mod = jnp.swapaxes(mod_mine, 0, 1).reshape(2, N_CHIP * da)

    small = dict(norm1_g=norm1_g, norm2_g=norm2_g, final_g=final_g[None], fox_q_gain=fox_q_gain,
                 fox_k_gain=fox_k_gain, fox_b_f=fox_b_f, sgu_b_in=full_small["sgu_b_in"],
                 sgu_v_gain=full_small["sgu_v_gain"], sgu_v_bias=full_small["sgu_v_bias"], sgu_w_s=sgu_w_s[0],
                 sgu_b_s=sgu_b_s[0], ffn_conv_w=full_small["ffn_conv_w"], ffn_conv_b=ffn_conv_b)
    loss_dev, dx, g, dmod, (css, rcvs) = _local_step(x[0], loss_target[0], mod, wts, small, comm)

    g["ada_b"] = dmod
    g["loss"] = loss_dev
    small_names = ("ada_b",) + SMALL_SHARDED + tuple(n for n in SMALL_REPL if n != "ada_b") + ("loss",)
    gs = _pack([g[n] for n in small_names], 1)
    rows_s = -(-gs.shape[0] // 8) * 8
    gs = jnp.pad(gs, ((0, rows_s - gs.shape[0]), (0, 0)))
    gs_all = _allgather8(gs, "gather_small_grads").reshape(N_DEV, rows_s, ROW)
    gsum = _sum8(gs_all, "sum_small_grads")
    full_shapes = {n: w[n].shape for n in SMALL_REPL}
    full_shapes.update({n: w[n].shape[:-1] + (w[n].shape[-1] * N_CHIP,) for n in SMALL_SHARDED})
    full_shapes["loss"] = ()
    gfull = dict(zip(small_names, _unpack(gsum, [full_shapes[n] for n in small_names], 1)))
    grads = {n: gfull[n] for n in SMALL_REPL}
    for n in SMALL_SHARDED:
        width = w[n].shape[-1]
        grads[n] = lax.dynamic_slice_in_dim(gfull[n], chip * width, width, axis=gfull[n].ndim - 1)
    dmod_all = gs_all[:, :12, :].reshape(N_DEV, 2, N_CHIP * da)
    dmod_cols = jnp.swapaxes(lax.dynamic_slice_in_dim(dmod_all, chip * da, da, axis=2), 0, 1)
    grads["ada_w"] = _ada_w_grad(c_act.T, dmod_cols)

    halves =[_rs_final_sum(cs_, r_, me_arr, _row_tile(cs_.shape[1]), "rs_final_sum%d" % a)
              for a, (cs_, r_) in enumerate(zip(css, rcvs))]
    others = _rs_swap_halves(halves, "rs_swap")
    red = [jnp.concatenate([jnp.where(ac == 0, h_, o_), jnp.where(ac == 0, o_, h_)]) for h_, o_ in zip(halves, others)]
    grads.update(fox_w_in=red[0], fox_w_out=red[1], sgu_w_in=red[2], sgu_w_out=red[3],
                 ffn_w_up=jnp.stack([red[4], red[5]]), ffn_w_down=jnp.stack([red[6], red[7]]))

    delta, new_m, new_v = {}, {}, {}
    for n in BIG + ("ada_w",):
        shp = w[n].shape
        two_d = lambda a: a.reshape(-1, shp[-1])
        d_, m_, v_ = _adamw(two_d(w[n]), two_d(grads[n]), two_d(mom[n]), two_d(var[n]), "adamw_" + n)
        delta[n], new_m[n], new_v[n] = d_.reshape(shp), m_.reshape(shp), v_.reshape(shp)
    rest = SMALL_SHARDED + SMALL_REPL
    packs = [_pack([t[n] for n in rest], 1) for t in (w, grads, mom, var)]
    rows_r = -(-packs[0].shape[0] // 8) * 8
    packs = [jnp.pad(p, ((0, rows_r - p.shape[0]), (0, 0))) for p in packs]
    outs = _adamw(*packs, "adamw_small")
    for t, o in zip((delta, new_m, new_v), outs):
        t.update(zip(rest, _unpack(o, [w[n].shape for n in rest], 1)))

    loss = gfull["loss"]
    return (loss, dx[None], *[grads[n].reshape(w[n].shape) for n in WEIGHTS], *[delta[n] for n in WEIGHTS],
            *[new_m[n] for n in WEIGHTS], *[new_v[n] for n in WEIGHTS])
```

```python
import functools
import math

import numpy as np
import jax
import jax.numpy as jnp
from jax import lax
from jax.experimental import pallas as pl
from jax.experimental.pallas import tpu as pltpu

F32 = jnp.float32
BF16 = jnp.bfloat16
MESH = pl.DeviceIdType.MESH

D = 1024
H = 16
DH = 64
NP = H // 2
LANES = 128
DFF = 2816
SGW = 2048
SGG = 8
SGC = 256
SGB = 128
CHUNK = 64
EPS = 1e-6
FOX_N = 4 * D + H
FOX_NP = 4224
GT = 256
NGT = DFF // GT
SCALE = DH ** -0.5
LOG2E = 1.4426950408889634

ADAM_LR = 0.001
ADAM_B1 = 0.9
ADAM_B2 = 0.999
ADAM_EPS = 1e-08
ADAM_WD = 0.01
ADAM_STEP = 10

V7X_VMEM_LIMIT = 56 * 1024 * 1024

L_F = 64
L_NF = 67
L_LSE = 70


def _cparams(sem=None):
    return pltpu.CompilerParams(dimension_semantics=sem, vmem_limit_bytes=V7X_VMEM_LIMIT)


def _split3(x):
    hi = x.astype(BF16)
    r = x - hi.astype(F32)
    mid = r.astype(BF16)
    lo = (r - mid.astype(F32)).astype(BF16)
    return hi, mid, lo


def _dot(a, b, dims=(((1,), (0,)), ((), ()))):
    return lax.dot_general(a, b, dims, preferred_element_type=F32)


def _dot_nt(a, b):
    return _dot(a, b, (((1,), (1,)), ((), ())))


def _dot_tn(a, b):
    return _dot(a, b, (((0,), (0,)), ((), ())))


def _exact_dot(m_bf16, x_f32):
    hi, mid, lo = _split3(x_f32)
    return _dot(m_bf16, hi) + _dot(m_bf16, mid) + _dot(m_bf16, lo)


def _exact_dot_r(x_f32, m_bf16):
    hi, mid, lo = _split3(x_f32)
    return _dot(hi, m_bf16) + _dot(mid, m_bf16) + _dot(lo, m_bf16)


def _head_block_ones():
    r = lax.broadcasted_iota(jnp.int32, (LANES, LANES), 0) // DH
    c = lax.broadcasted_iota(jnp.int32, (LANES, LANES), 1) // DH
    return (r == c).astype(BF16)


def _sigmoid(x):
    return 1.0 / (1.0 + jnp.exp(-x))


def _gelu(x):
    c = math.sqrt(2.0 / math.pi)
    return 0.5 * x * (1.0 + jnp.tanh(c * (x + 0.044715 * (x * x * x))))


def _gelu_and_grad(x):
    c = math.sqrt(2.0 / math.pi)
    x2 = x * x
    t = jnp.tanh(c * (x + 0.044715 * (x2 * x)))
    half = 0.5 * (1.0 + t)
    return x * half, half + 0.5 * x * (1.0 - t * t) * c * (1.0 + 3 * 0.044715 * x2)


def _rstd_rows(x):
    return lax.rsqrt(jnp.mean(x * x, axis=-1, keepdims=True) + EPS)


def _norm_mod_matmul(x, ng, sc, sh, w, bias, out_dtype, ts, tn, name, planes=1):
    s, d = x.shape
    ns = w.shape[-1]
    n = w.shape[0] * ns if w.ndim == 3 else ns
    nc = n // planes

    def body(x_ref, ng_ref, sc_ref, sh_ref, w_ref, b_ref, o_ref, h_ref):
        xv = x_ref[...]
        h = (xv * _rstd_rows(xv) * ng_ref[...] * (1.0 + sc_ref[...]) + sh_ref[...]).astype(BF16)
        h_ref[...] = h
        for e in range(planes):
            for c0 in range(0, nc, tn):
                g0 = e * nc + c0
                wv = w_ref[g0 // ns, :, g0 % ns:g0 % ns + tn] if w.ndim == 3 else w_ref[:, g0:g0 + tn]
                val = (_dot(h, wv) + b_ref[:, g0:g0 + tn]).astype(out_dtype)
                if planes == 1:
                    o_ref[:, c0:c0 + tn] = val
                else:
                    o_ref[e, :, c0:c0 + tn] = val

    vec = pl.BlockSpec((1, d), lambda i: (0, 0))
    w_spec = (pl.BlockSpec(w.shape, lambda i: (0, 0, 0)) if w.ndim == 3 else pl.BlockSpec((d, n), lambda i: (0, 0)))
    if planes == 1:
        o_spec, o_shape = pl.BlockSpec((ts, n), lambda i: (i, 0)), (s, n)
    else:
        o_spec, o_shape = pl.BlockSpec((planes, ts, nc), lambda i: (0, i, 0)), (planes, s, nc)
    return pl.pallas_call(
        body, name=name, grid=(s // ts,),
        in_specs=[pl.BlockSpec((ts, d), lambda i: (i, 0)), vec, vec, vec, w_spec,
                  pl.BlockSpec((1, n), lambda i: (0, 0))],
        out_specs=[o_spec, pl.BlockSpec((ts, d), lambda i: (i, 0))],
        out_shape=[jax.ShapeDtypeStruct(o_shape, out_dtype), jax.ShapeDtypeStruct((s, d), BF16)],
        compiler_params=_cparams(("arbitrary",)),
    )(x, ng, sc, sh, w, bias)


def _matmul(a, b, ta, tb, tm, tn, tk, out_dtype, name, out_parts=1):
    if a.ndim == 3:
        m, k = a.shape[1], a.shape[0] * a.shape[2]
        nkp = a.shape[2] // tk
    else:
        m, k = (a.shape[1], a.shape[0]) if ta else a.shape
    if b.ndim == 3:
        n = b.shape[1] if tb else b.shape[0] * b.shape[2]
        nbp = b.shape[2] // (tk if tb else tn)
    else:
        n = b.shape[0] if tb else b.shape[1]
    nk = k // tk
    nop = n // out_parts // tn
    dims = (((0,) if ta else (1,), (1,) if tb else (0,)), ((), ()))

    def body(a_ref, b_ref, o_ref, acc):
        kk = pl.program_id(2)

        @pl.when(kk == 0)
        def _():
            acc[...] = jnp.zeros_like(acc)
        acc[...] += _dot(a_ref[...], b_ref[...], dims)

        @pl.when(kk == nk - 1)
        def _():
            o_ref[...] = acc[...].astype(out_dtype)

    if a.ndim == 3:
        a_spec = pl.BlockSpec((None, tm, tk), lambda i, j, kk: (kk // nkp, i, kk % nkp))
    else:
        a_spec = (pl.BlockSpec((tk, tm), lambda i, j, kk: (kk, i)) if ta
                  else pl.BlockSpec((tm, tk), lambda i, j, kk: (i, kk)))
    if b.ndim == 3 and tb:
        b_spec = pl.BlockSpec((None, tn, tk), lambda i, j, kk: (kk // nbp, j, kk % nbp))
    elif b.ndim == 3:
        b_spec = pl.BlockSpec((None, tk, tn), lambda i, j, kk: (j // nbp, kk, j % nbp))
    else:
        b_spec = (pl.BlockSpec((tn, tk), lambda i, j, kk: (j, kk)) if tb
                  else pl.BlockSpec((tk, tn), lambda i, j, kk: (kk, j)))
    if out_parts > 1:
        o_spec = pl.BlockSpec((None, tm, tn), lambda i, j, kk: (j // nop, i, j % nop))
        o_shape = (out_parts, m, n // out_parts)
    else:
        o_spec, o_shape = pl.BlockSpec((tm, tn), lambda i, j, kk: (i, j)), (m, n)
    return pl.pallas_call(
        body, name=name, grid=(m // tm, n // tn, nk),
        in_specs=[a_spec, b_spec],
        out_specs=o_spec,
        out_shape=jax.ShapeDtypeStruct(o_shape, out_dtype),
        scratch_shapes=[pltpu.VMEM((tm, tn), F32)],
        compiler_params=_cparams(("arbitrary", "arbitrary", "arbitrary")),
    )(a, b)


def _matmul_wt(a, w, tn, tk, out_dtype, ts, name):
    s = a.shape[-2]
    ka, kw = a.shape[-1], w.shape[-1]
    k = ka * (a.shape[0] if a.ndim == 3 else 1)
    n = w.shape[-2]

    def body(a_ref, w_ref, o_ref):
        for n0 in range(0, n, tn):
            acc = None
            for g0 in range(0, k, tk):
                av = a_ref[g0 // ka, :, g0 % ka:g0 % ka + tk] if a.ndim == 3 else a_ref[:, g0:g0 + tk]
                wv = (w_ref[g0 // kw, n0:n0 + tn, g0 % kw:g0 % kw + tk] if w.ndim == 3
                      else w_ref[n0:n0 + tn, g0:g0 + tk])
                part = _dot_nt(av, wv)
                acc = part if acc is None else acc + part
            o_ref[:, n0:n0 + tn] = acc.astype(out_dtype)

    a_spec = (pl.BlockSpec((a.shape[0], ts, ka), lambda i: (0, i, 0)) if a.ndim == 3
              else pl.BlockSpec((ts, ka), lambda i: (i, 0)))
    w_spec = pl.BlockSpec(w.shape, (lambda i: (0, 0, 0)) if w.ndim == 3 else (lambda i: (0, 0)))
    return pl.pallas_call(
        body, name=name, grid=(s // ts,),
        in_specs=[a_spec, w_spec], out_specs=pl.BlockSpec((ts, n), lambda i: (i, 0)),
        out_shape=jax.ShapeDtypeStruct((s, n), out_dtype),
        compiler_params=_cparams(("arbitrary",)),
    )(a, w)


def _lane(shape):
    return lax.broadcasted_iota(jnp.int32, shape, 1)


def _pair_norm(x, gain2, bones):
    msq = _exact_dot_r(x * x, bones) * (1.0 / DH)
    r = lax.rsqrt(msq + EPS)
    xh = x * r
    return xh * gain2, xh, r


def _fox_post(proj, qg2, kg2, bf, ts, name):
    s = proj.shape[0]

    def body(p_ref, qg_ref, kg_ref, bf_ref, q_ref, k_ref, v_ref, carry):
        @pl.when(pl.program_id(0) == 0)
        def _():
            carry[...] = jnp.zeros_like(carry)
        lane = _lane((ts, LANES))
        bones = _head_block_ones()
        xf = p_ref[:, 4 * D:4 * D + LANES] + bf_ref[...]
        logf = jnp.minimum(xf, 0.0) - jnp.log(1.0 + jnp.exp(-jnp.abs(xf)))
        logf = jnp.where(lane < H, logf, 0.0)
        rr = lax.broadcasted_iota(jnp.int32, (ts, ts), 0)
        cc = lax.broadcasted_iota(jnp.int32, (ts, ts), 1)
        ltri = (cc <= rr).astype(BF16)
        fcum = _exact_dot(ltri, logf) + carry[0:1, :]
        carry[0:1, :] = fcum[ts - 1:ts, :]
        fhi, fmid, flo = _split3(fcum * LOG2E)
        fhi, fmid, flo = fhi.astype(F32), fmid.astype(F32), flo.astype(F32)
        one_q = ((lane >= L_NF) & (lane < L_NF + 3)).astype(F32)
        one_k = (((lane >= L_F) & (lane < L_F + 3)) | ((lane >= L_LSE) & (lane < L_LSE + 3))).astype(F32)
        one_v = ((lane >= L_F) & (lane < L_F + 3)).astype(F32)
        for p in range(NP):
            qn, _, _ = _pair_norm(p_ref[:, p * LANES:(p + 1) * LANES], qg_ref[...], bones)
            kn, _, _ = _pair_norm(p_ref[:, D + p * LANES:D + (p + 1) * LANES], kg_ref[...], bones)
            vv = p_ref[:, 2 * D + p * LANES:2 * D + (p + 1) * LANES]
            qn = qn * (SCALE * LOG2E)
            for e in range(2):
                h = 2 * p + e
                if e == 1:
                    qe, ke, ve = (pltpu.roll(t, DH, axis=1) for t in (qn, kn, vv))
                else:
                    qe, ke, ve = qn, kn, vv
                f0, f1, f2 = fhi[:, h:h + 1], fmid[:, h:h + 1], flo[:, h:h + 1]
                fq = jnp.where(lane == L_F, f0, jnp.where(lane == L_F + 1, f1, jnp.where(lane == L_F + 2, f2, one_q)))
                fk = jnp.where(lane == L_NF, -f0, jnp.where(lane == L_NF + 1, -f1, jnp.where(lane == L_NF + 2, -f2, one_k)))
                q_ref[h] = jnp.where(lane < DH, qe, fq).astype(BF16)
                k_ref[h] = jnp.where(lane < DH, ke, fk).astype(BF16)
                v_ref[h] = jnp.where(lane < DH, ve, one_v).astype(BF16)

    hs = pl.BlockSpec((H, ts, LANES), lambda i: (0, i, 0))
    vec = pl.BlockSpec((1, LANES), lambda i: (0, 0))
    shp = jax.ShapeDtypeStruct((H, s, LANES), BF16)
    return pl.pallas_call(
        body, name=name, grid=(s // ts,),
        in_specs=[pl.BlockSpec((ts, FOX_NP), lambda i: (i, 0)), vec, vec, vec],
        out_specs=[hs, hs, hs], out_shape=[shp, shp, shp],
        scratch_shapes=[pltpu.VMEM((8, LANES), F32)],
        compiler_params=_cparams(("arbitrary",)),
    )(proj, qg2, kg2, bf)


def _gather_copies(p_refs, o_refs, send_sems, recv_sems):
    x, y, c = _mesh_pos()
    me = 2 * x + y
    sends, arrivals = [], []
    for a, (p_ref, o_ref) in enumerate(zip(p_refs, o_refs)):
        rh = p_ref.shape[0] // 2
        for k, chip in enumerate(_other_chips(x, y)):
            ci = 2 * chip[0] + chip[1]
            for cc in range(2):
                sends.append(_remote(p_ref.at[pl.ds(c * rh, rh), :], o_ref.at[me, pl.ds(c * rh, rh), :],
                                     send_sems.at[6 * a + 2 * k + cc], recv_sems.at[6 * a + 2 * k + c], (*chip, cc)))
                arrivals.append(_remote(o_ref.at[ci, pl.ds(cc * rh, rh), :], o_ref.at[ci, pl.ds(cc * rh, rh), :],
                                        send_sems.at[6 * a + 2 * k + cc], recv_sems.at[6 * a + 2 * k + cc],
                                        (*chip, cc)))
    return sends, arrivals


def _attn_fwd(qa, ka, va, tq, name, shards=()):
    s = qa.shape[1]
    nq = s // tq
    na = len(shards)
    hps = HPS_FWD

    def body(*refs):
        q_ref, k_ref, v_ref = refs[:3]
        p_refs = refs[3:3 + na]
        o_ref, ql_ref = refs[3 + na:5 + na]
        g_refs = refs[5 + na:5 + 2 * na]
        i = pl.program_id(1)
        if na:
            send_sems, recv_sems = refs[5 + 2 * na:]

            @pl.when((pl.program_id(0) == 0) & (i == 0))
            def _():
                for cp in _gather_copies(p_refs, g_refs, send_sems, recv_sems)[0]:
                    cp.start()
        lane = _lane((tq, LANES))
        qs_ = [q_ref[e] for e in range(hps)]

        tk = min(TK_FWD, tq)
        nks = tq // tk

        def step(j, carry, diag=None):
            off = pl.multiple_of(j * tk, tk)
            scs = [_dot_nt(qs_[e], k_ref[e, pl.ds(off, tk), :]) for e in range(hps)]
            probs = []
            for e in range(hps):
                m, sc = carry[e][0], scs[e]
                if diag is not None:
                    rr = lax.broadcasted_iota(jnp.int32, (tq, tk), 0)
                    cc = lax.broadcasted_iota(jnp.int32, (tq, tk), 1) + diag * tk
                    sc = jnp.where(cc <= rr, sc, -jnp.inf)
                m_new = jnp.maximum(m, jnp.max(sc, axis=-1, keepdims=True))
                probs.append((m_new, jnp.exp2(sc - m_new).astype(BF16), jnp.exp2(m - m_new)))
            return tuple((m_new, carry[e][1] * alpha + _dot(pr, v_ref[e, pl.ds(off, tk), :]))
                         for e, (m_new, pr, alpha) in enumerate(probs))

        one = (jnp.full((tq, 1), -jnp.inf, F32), jnp.zeros((tq, LANES), F32))
        carry = lax.fori_loop(0, i * nks, step, (one,) * hps)
        for r in range(nks):
            carry = step(i * nks + r, carry, diag=r)
        outs = []
        for e in range(hps):
            m, acc = carry[e]
            l = acc[:, L_F:L_F + 1]
            outs.append(acc / l)
            lse = m + jnp.log2(l)
            h0, h1, h2 = _split3(-lse)
            ql = jnp.where(lane == L_LSE, h0.astype(F32),
                           jnp.where(lane == L_LSE + 1, h1.astype(F32),
                                     jnp.where(lane == L_LSE + 2, h2.astype(F32), qs_[e].astype(F32))))
            ql_ref[e] = ql.astype(BF16)
        for e in range(0, hps, 2):
            o_ref[:, e * DH:(e + 2) * DH] = jnp.where(lane < DH, outs[e], pltpu.roll(outs[e + 1], DH, axis=1))
        if na:
            @pl.when((pl.program_id(0) == H // hps - 1) & (i == nq - 1))
            def _():
                sends, arrivals = _gather_copies(p_refs, g_refs, send_sems, recv_sems)
                for cp in arrivals:
                    cp.wait_recv()
                for cp in sends:
                    cp.wait_send()

    res = pl.BlockSpec((hps, s, LANES), lambda p, i: (p, 0, 0))
    qs = pl.BlockSpec((hps, tq, LANES), lambda p, i: (p, i, 0))
    outs = pl.pallas_call(
        body, name=name, grid=(H // hps, nq),
        in_specs=[qs, res, res] + [HBM_SPEC] * na,
        out_specs=[pl.BlockSpec((tq, hps * DH), lambda p, i: (i, p)), qs] + [HBM_SPEC] * na,
        out_shape=[jax.ShapeDtypeStruct((s, D), F32), jax.ShapeDtypeStruct((H, s, LANES), BF16)]
        + [jax.ShapeDtypeStruct((N_CHIP,) + p.shape, p.dtype) for p in shards],
        scratch_shapes=[pltpu.SemaphoreType.DMA((6 * na,))] * 2 if na else [],
        compiler_params=_cparams(("arbitrary", "arbitrary")),
    )(qa, ka, va, *shards)
    return outs[0], outs[1], list(outs[2:])


def _chip_exchange_copies(cs_refs, o_refs, send_sems, recv_sems):
    x, y, c = _mesh_pos()
    cps = []
    for a, (cs_ref, o_ref) in enumerate(zip(cs_refs, o_refs)):
        for k, chip in enumerate(_other_chips(x, y)):
            ci = 2 * chip[0] + chip[1]
            cps.append(_remote(cs_ref.at[ci], o_ref.at[k], send_sems.at[3 * a + k], recv_sems.at[3 * a + k],
                               (*chip, c)))
    return cps


def _attn_bwd(ql, ka, va, doa, tq, name, css=()):
    s = ql.shape[1]
    nq = s // tq
    na = len(css)

    def body(*refs):
        q_ref, k_ref, v_ref, do_ref = refs[:4]
        cs_refs = refs[4:4 + na]
        dqo_ref, dk_ref, dv_ref = refs[4 + na:7 + na]
        r_refs = refs[7 + na:7 + 2 * na]
        dq_ref = refs[7 + 2 * na]
        j = pl.program_id(1)
        if na:
            send_sems, recv_sems = refs[8 + 2 * na:]

            @pl.when((pl.program_id(0) == 0) & (j == 0))
            def _():
                for cp in _chip_exchange_copies(cs_refs, r_refs, send_sems, recv_sems):
                    cp.start()

        @pl.when(j == 0)
        def _():
            dq_ref[...] = jnp.zeros_like(dq_ref)
        lane = _lane((tq, LANES))
        kbs = [k_ref[0], k_ref[1]]
        vbs = [v_ref[0], v_ref[1]]

        def step(i, carry, masked):
            ioff = pl.multiple_of(i * tq, tq)
            qbs = [q_ref[e, pl.ds(ioff, tq), :] for e in range(2)]
            dobs = [do_ref[e, pl.ds(ioff, tq), :] for e in range(2)]
            scs = [_dot_nt(qbs[e], kbs[e]) for e in range(2)]
            dps = [_dot_nt(dobs[e], vbs[e]) for e in range(2)]
            prs, dss = [], []
            for e in range(2):
                pr = jnp.exp2(scs[e])
                if masked:
                    rr = lax.broadcasted_iota(jnp.int32, (tq, tq), 0)
                    cc = lax.broadcasted_iota(jnp.int32, (tq, tq), 1)
                    pr = jnp.where(cc <= rr, pr, 0.0)
                dss.append((pr * dps[e]).astype(BF16))
                prs.append(pr.astype(BF16))
            new = []
            for e in range(2):
                dk, dv = carry[e]
                dv = dv + _dot_tn(prs[e], dobs[e])
                dk = dk + _dot_tn(dss[e], qbs[e])
                dq_ref[e, pl.ds(ioff, tq), :] += _dot(dss[e], kbs[e])
                new.append((dk, dv))
            return tuple(new)

        zero = jnp.zeros((tq, LANES), F32)
        carry = step(j, ((zero, zero), (zero, zero)), True)
        carry = lax.fori_loop(j + 1, nq, functools.partial(step, masked=False), carry)
        for e in range(2):
            dk, dv = carry[e]
            col = dk[:, L_NF:L_NF + 1]
            hi = col.astype(BF16).astype(F32)
            dk_ref[e] = jnp.where(lane == L_NF, hi, jnp.where(lane == L_NF + 1, col - hi, dk)).astype(BF16)
            dv_ref[e] = dv.astype(BF16)

        @pl.when(j == nq - 1)
        def _():
            lane_s = _lane((s, LANES))
            for e in range(2):
                dq = dq_ref[e]
                col = dq[:, L_F:L_F + 1]
                hi = col.astype(BF16).astype(F32)
                dqo_ref[e] = jnp.where(lane_s == L_F, hi, jnp.where(lane_s == L_F + 1, col - hi, dq)).astype(BF16)
        if na:
            @pl.when((pl.program_id(0) == NP - 1) & (j == nq - 1))
            def _():
                for cp in _chip_exchange_copies(cs_refs, r_refs, send_sems, recv_sems):
                    cp.wait()

    res = pl.BlockSpec((2, s, LANES), lambda p, j: (p, 0, 0))
    tile = pl.BlockSpec((2, tq, LANES), lambda p, j: (p, j, 0))
    shp = jax.ShapeDtypeStruct((H, s, LANES), BF16)
    outs = pl.pallas_call(
        body, name=name, grid=(NP, nq),
        in_specs=[res, tile, tile, res] + [HBM_SPEC] * na, out_specs=[res, tile, tile] + [HBM_SPEC] * na,
        out_shape=[shp, shp, shp] + [jax.ShapeDtypeStruct((3,) + cs.shape[1:], cs.dtype) for cs in css],
        scratch_shapes=[pltpu.VMEM((2, s, LANES), F32)] + ([pltpu.SemaphoreType.DMA((3 * na,))] * 2 if na else []),
        compiler_params=_cparams(("arbitrary", "arbitrary")),
    )(ql, ka, va, doa, *css)
    return outs[0], outs[1], outs[2], list(outs[3:])


def _gate_out(att, proj, w, xin, g, ts, name):
    s = att.shape[0]

    def body(a_ref, o_ref, w_ref, x_ref, g_ref, xo_ref, y_ref, gt_ref):
        gated = (a_ref[...] * _sigmoid(o_ref[...])).astype(BF16)
        gt_ref[...] = gated
        y = _dot(gated, w_ref[...])
        xo_ref[...] = x_ref[...] + g_ref[...] * y
        y_ref[...] = y.astype(BF16)

    row = pl.BlockSpec((ts, D), lambda i: (i, 0))
    return pl.pallas_call(
        body, name=name, grid=(s // ts,),
        in_specs=[row, pl.BlockSpec((ts, D), lambda i: (i, 3)), pl.BlockSpec((D, D), lambda i: (0, 0)), row,
                  pl.BlockSpec((1, D), lambda i: (0, 0))],
        out_specs=[row, row, row],
        out_shape=[jax.ShapeDtypeStruct((s, D), F32), jax.ShapeDtypeStruct((s, D), BF16),
                   jax.ShapeDtypeStruct((s, D), BF16)],
        compiler_params=_cparams(("arbitrary",)),
    )(att, proj, w, xin, g)


def _sibling_copies(g_refs, o_refs, ssems, rsems):
    x, y, c = _mesh_pos()
    cps = []
    for a, (g_ref, o_ref) in enumerate(zip(g_refs, o_refs)):
        rh = g_ref.shape[1] // 2
        cps.append(_remote(g_ref.at[:, pl.ds((1 - c) * rh, rh), :], o_ref, ssems.at[a], rsems.at[a], (x, y, 1 - c)))
    return cps


def _attn_bwd_prep(dy, w_out, att, proj, ts, name, glist=()):
    s = att.shape[0]
    na = len(glist)

    def body(*refs):
        dy_ref, w_ref, a_ref, o_ref = refs[:4]
        g_refs = refs[4:4 + na]
        doa_ref, dop_ref = refs[4 + na:6 + na]
        sib_refs, sems = refs[6 + na:6 + 2 * na], refs[6 + 2 * na:]
        if na:
            @pl.when(pl.program_id(0) == 0)
            def _():
                for cp in _sibling_copies(g_refs, sib_refs, *sems):
                    cp.start()
        lane = _lane((ts, LANES))
        bones = _head_block_ones()
        dgv = _dot_nt(dy_ref[...], w_ref[...])
        for p in range(NP):
            sl = slice(p * LANES, (p + 1) * LANES)
            dg, a = dgv[:, sl], a_ref[:, sl]
            sig = _sigmoid(o_ref[:, sl])
            datt = dg * sig
            dop_ref[:, sl] = (dg * a * sig * (1.0 - sig)).astype(BF16)
            delta = _exact_dot_r(datt * a, bones)
            for e in range(2):
                de, dl = (datt, delta) if e == 0 else (pltpu.roll(datt, DH, axis=1), pltpu.roll(delta, DH, axis=1))
                h0, h1, h2 = _split3(-dl[:, 0:1])
                aug = jnp.where(lane == L_F, h0.astype(F32),
                                jnp.where(lane == L_F + 1, h1.astype(F32),
                                          jnp.where(lane == L_F + 2, h2.astype(F32), 0.0)))
                doa_ref[2 * p + e] = jnp.where(lane < DH, de, aug).astype(BF16)
        if na:
            @pl.when(pl.program_id(0) == s // ts - 1)
            def _():
                for cp in _sibling_copies(g_refs, sib_refs, *sems):
                    cp.wait()

    row = pl.BlockSpec((ts, D), lambda i: (i, 0))
    outs = pl.pallas_call(
        body, name=name, grid=(s // ts,),
        in_specs=[row, pl.BlockSpec((D, D), lambda i: (0, 0)), row, pl.BlockSpec((ts, D), lambda i: (i, 3))]
        + [HBM_SPEC] * na,
        out_specs=[pl.BlockSpec((H, ts, LANES), lambda i: (0, i, 0)), row] + [HBM_SPEC] * na,
        out_shape=[jax.ShapeDtypeStruct((H, s, LANES), BF16), jax.ShapeDtypeStruct((s, D), BF16)]
        + [jax.ShapeDtypeStruct((g_.shape[0], g_.shape[1] // 2, g_.shape[2]), g_.dtype) for g_ in glist],
        scratch_shapes=[pltpu.SemaphoreType.DMA((na,))] * 2 if na else [],
        compiler_params=_cparams(("arbitrary",)),
    )(dy, w_out, att, proj, *glist)
    return outs[0], outs[1], list(outs[2:])


def _fox_post_bwd(proj, dqa, dka, dva, dop, qg2, kg2, bf, ts, name):
    s = proj.shape[0]
    nt = s // ts

    def body(p_ref, dq_ref, dk_ref, dv_ref, dop_ref, qg_ref, kg_ref, bf_ref, o_ref, red_ref, carry):
        @pl.when(pl.program_id(0) == 0)
        def _():
            carry[...] = jnp.zeros_like(carry)
            red_ref[...] = jnp.zeros_like(red_ref)
        lane = _lane((ts, LANES))
        bones = _head_block_ones()
        d_f = jnp.zeros((ts, LANES), F32)
        dqg = jnp.zeros((1, LANES), F32)
        dkg = jnp.zeros((1, LANES), F32)
        for p in range(NP):
            heads = [[ref[2 * p + e].astype(F32) for e in range(2)] for ref in (dq_ref, dk_ref, dv_ref)]
            pair = [jnp.where(lane < DH, a, pltpu.roll(b, DH, axis=1)) for a, b in heads]
            for e in range(2):
                dqe, dke = heads[0][e], heads[1][e]
                col = (dqe[:, L_F:L_F + 1] + dqe[:, L_F + 1:L_F + 2]
                       - dke[:, L_NF:L_NF + 1] - dke[:, L_NF + 1:L_NF + 2])
                d_f = jnp.where(lane == 2 * p + e, col, d_f)
            for idx, (g_ref, base) in enumerate(((qg_ref, 0), (kg_ref, D))):
                x = p_ref[:, base + p * LANES:base + (p + 1) * LANES]
                _, xh, r = _pair_norm(x, g_ref[...], bones)
                dn = pair[idx] * (SCALE if idx == 0 else 1.0 / LOG2E)
                t = dn * g_ref[...]
                mean_txh = _exact_dot_r(t * xh, bones) * (1.0 / DH)
                dx = r * (t - xh * mean_txh)
                o_ref[:, base + p * LANES:base + (p + 1) * LANES] = dx.astype(BF16)
                gsum = jnp.sum(dn * xh, axis=0, keepdims=True)
                if idx == 0:
                    dqg = dqg + gsum
                else:
                    dkg = dkg + gsum
            o_ref[:, 2 * D + p * LANES:2 * D + (p + 1) * LANES] = pair[2].astype(BF16)
        o_ref[:, 3 * D:4 * D] = dop_ref[...]
        rr = lax.broadcasted_iota(jnp.int32, (ts, ts), 0)
        cc = lax.broadcasted_iota(jnp.int32, (ts, ts), 1)
        utri = (cc >= rr).astype(BF16)
        dlogf = _exact_dot(utri, d_f) + carry[0:1, :]
        carry[0:1, :] = dlogf[0:1, :]
        xf = p_ref[:, 4 * D:4 * D + LANES] + bf_ref[...]
        dfl = jnp.where(lane < H, dlogf * _sigmoid(-xf), 0.0)
        o_ref[:, 4 * D:4 * D + LANES] = dfl.astype(BF16)
        red_ref[0:1, :] += dqg
        red_ref[1:2, :] += dkg
        red_ref[2:3, :] += jnp.sum(dfl, axis=0, keepdims=True)

    hs = pl.BlockSpec((H, ts, LANES), lambda i: (0, nt - 1 - i, 0))
    vec = pl.BlockSpec((1, LANES), lambda i: (0, 0))
    return pl.pallas_call(
        body, name=name, grid=(nt,),
        in_specs=[pl.BlockSpec((ts, FOX_NP), lambda i: (nt - 1 - i, 0)), hs, hs, hs,
                  pl.BlockSpec((ts, D), lambda i: (nt - 1 - i, 0)), vec, vec, vec],
        out_specs=[pl.BlockSpec((ts, FOX_NP), lambda i: (nt - 1 - i, 0)),
                   pl.BlockSpec((8, LANES), lambda i: (0, 0))],
        out_shape=[jax.ShapeDtypeStruct((s, FOX_NP), BF16), jax.ShapeDtypeStruct((8, LANES), F32)],
        scratch_shapes=[pltpu.VMEM((8, LANES), F32)],
        compiler_params=_cparams(("arbitrary",)),
    )(proj, dqa, dka, dva, dop, qg2, kg2, bf)


HALO = 16
TS = 512
TQ = 512
TR = 256
TP = 256
HPS_FWD = 4
TK_FWD = 512
TKW = 2048


def _shift_down(x, k):
    return pltpu.roll(x, k, axis=0)


def _shift_up(x, k):
    return pltpu.roll(x, x.shape[0] - k, axis=0)


def _conv_down(a, cw, cb, w, xin, gate, ts, name):
    s = a.shape[1]
    d = w.shape[1]
    hb = ts // HALO

    def body(prev_ref, a_ref, cw_ref, cb_ref, w_ref, x_ref, g_ref, o_ref, y_ref, f_ref, ap_ref):
        i = pl.program_id(0)
        acc = None
        for c in range(NGT):
            cols = slice(c * GT, (c + 1) * GT)
            both = lambda ref: jnp.concatenate([ref[0, :, cols].astype(F32), ref[1, :, cols].astype(F32)], axis=1)
            cwv, cbv = both(cw_ref), both(cb_ref)
            ext = jnp.concatenate([jnp.where(i > 0, both(prev_ref), 0.0), both(a_ref)], axis=0)
            ap = (_shift_down(ext, 2) * cwv[0:1, :] + _shift_down(ext, 1) * cwv[1:2, :]
                  + ext * cwv[2:3, :] + cbv)[HALO:, :]
            g, val = ap[:, :GT], ap[:, GT:]
            fch = (g * _sigmoid(g) * val).astype(BF16)
            f_ref[:, cols] = fch
            ap_ref[0, :, cols] = g.astype(BF16)
            ap_ref[1, :, cols] = val.astype(BF16)
            part = _dot(fch, w_ref[cols, :])
            acc = part if acc is None else acc + part
        y_ref[...] = acc.astype(BF16)
        o_ref[...] = x_ref[...] + g_ref[...] * acc

    row = pl.BlockSpec((ts, d), lambda i: (i, 0))
    planes = pl.BlockSpec((2, ts, DFF), lambda i: (0, i, 0))
    return pl.pallas_call(
        body, name=name, grid=(s // ts,),
        in_specs=[pl.BlockSpec((2, HALO, DFF), lambda i: (0, jnp.maximum(i * hb - 1, 0), 0)), planes,
                  pl.BlockSpec((2, 8, DFF), lambda i: (0, 0, 0)), pl.BlockSpec((2, 1, DFF), lambda i: (0, 0, 0)),
                  pl.BlockSpec((DFF, d), lambda i: (0, 0)), row, pl.BlockSpec((1, d), lambda i: (0, 0))],
        out_specs=[row, row, pl.BlockSpec((ts, DFF), lambda i: (i, 0)), planes],
        out_shape=[jax.ShapeDtypeStruct((s, d), F32), jax.ShapeDtypeStruct((s, d), BF16),
                   jax.ShapeDtypeStruct((s, DFF), BF16), jax.ShapeDtypeStruct((2, s, DFF), BF16)],
        compiler_params=_cparams(("arbitrary",)),
    )(a, a, cw, cb, w, xin, gate)


def _down_bwd_conv(dy, w, a, ap, cw, ts, name):
    s, d = dy.shape
    hb = ts // HALO
    nt = s // ts
    nhb = s // HALO

    def body(dy_ref, dyn_ref, w_ref, a_ref, ap_ref, apn_ref, cw_ref, da_ref, red_ref):
        i = pl.program_id(0)

        @pl.when(i == 0)
        def _():
            red_ref[...] = jnp.zeros_like(red_ref)
        dyn = jnp.where(i < nt - 1, dyn_ref[...], jnp.zeros_like(dyn_ref))
        dye = jnp.concatenate([dy_ref[...], dyn], axis=0)
        for c in range(NGT):
            cols = slice(c * GT, (c + 1) * GT)
            both = lambda ref: jnp.concatenate([ref[0, :, cols].astype(F32), ref[1, :, cols].astype(F32)], axis=1)
            cwv = both(cw_ref)
            dfe = _dot_nt(dye, w_ref[cols, :])
            apv = jnp.concatenate([both(ap_ref), both(apn_ref)], axis=0)
            g, val = apv[:, :GT], apv[:, GT:]
            sg = _sigmoid(g)
            dap = jnp.concatenate([dfe * val * (sg * (1.0 + g * (1.0 - sg))), dfe * (g * sg)], axis=1)
            shifted = [_shift_up(dap, 2)[:ts], _shift_up(dap, 1)[:ts], dap[:ts]]
            da = shifted[0] * cwv[0:1, :] + shifted[1] * cwv[1:2, :] + shifted[2] * cwv[2:3, :]
            av = both(a_ref)
            sums = [jnp.sum(av * t, axis=0, keepdims=True) for t in shifted]
            sums.append(jnp.sum(shifted[2], axis=0, keepdims=True))
            for e in range(2):
                half = slice(e * GT, (e + 1) * GT)
                da_ref[e, :, cols] = da[:, half].astype(BF16)
                for r, sm in enumerate(sums):
                    red_ref[e, r:r + 1, cols] += sm[:, half]

    planes = pl.BlockSpec((2, ts, DFF), lambda i: (0, i, 0))
    nxt = lambda i: jnp.minimum((i + 1) * hb, nhb - 1)
    return pl.pallas_call(
        body, name=name, grid=(nt,),
        in_specs=[pl.BlockSpec((ts, d), lambda i: (i, 0)), pl.BlockSpec((HALO, d), lambda i: (nxt(i), 0)),
                  pl.BlockSpec((DFF, d), lambda i: (0, 0)), planes, planes,
                  pl.BlockSpec((2, HALO, DFF), lambda i: (0, nxt(i), 0)),
                  pl.BlockSpec((2, 8, DFF), lambda i: (0, 0, 0))],
        out_specs=[planes, pl.BlockSpec((2, 8, DFF), lambda i: (0, 0, 0))],
        out_shape=[jax.ShapeDtypeStruct((2, s, DFF), BF16), jax.ShapeDtypeStruct((2, 8, DFF), F32)],
        compiler_params=_cparams(("arbitrary",)),
    )(dy, dy, w, a, ap, ap, cw)


def _chunk_mask(transposed=False):
    t = lax.broadcasted_iota(jnp.int32, (SGB, SGB), 0) // CHUNK
    u = lax.broadcasted_iota(jnp.int32, (SGB, SGB), 1) // CHUNK
    return (t <= u) if transposed else (u <= t)


def _sgu_ln(v, gain, bias):
    mu = jnp.mean(v, axis=-1, keepdims=True)
    vc = v - mu
    rstd = lax.rsqrt(jnp.mean(vc * vc, axis=-1, keepdims=True) + EPS)
    vhat = vc * rstd
    return vhat * gain + bias, vhat, rstd


def _sgu_fwd(z, vgain, vbias, ws, bst, w_out, xin, gate, tr, name):
    s = z.shape[0]

    def body(zu_ref, zv_ref, vg_ref, vb_ref, ws_ref, bs_ref, wo_ref, x_ref, gt_ref, xo_ref, yo_ref, y_ref):
        u = _gelu(zu_ref[...].astype(F32))
        vn, _, _ = _sgu_ln(_gelu(zv_ref[...].astype(F32)), vg_ref[...], vb_ref[...])
        vn = vn.astype(BF16)
        mask = _chunk_mask()
        for g in range(SGG):
            w = jnp.where(mask, ws_ref[g], 0.0).astype(BF16)
            for b in range(tr // SGB):
                rs, cs = slice(b * SGB, (b + 1) * SGB), slice(g * SGC, (g + 1) * SGC)
                mixed = _dot(w, vn[rs, cs]) + bs_ref[:, g:g + 1]
                y_ref[rs, cs] = (u[rs, cs] * mixed).astype(BF16)
        yo = _dot(y_ref[...], wo_ref[...])
        xo_ref[...] = x_ref[...] + gt_ref[...] * yo
        yo_ref[...] = yo.astype(BF16)

    vec = pl.BlockSpec((1, SGW), lambda i: (0, 0))
    row = pl.BlockSpec((tr, D), lambda i: (i, 0))
    return pl.pallas_call(
        body, name=name, grid=(s // tr,),
        in_specs=[pl.BlockSpec((tr, SGW), lambda i: (i, 0)), pl.BlockSpec((tr, SGW), lambda i: (i, 1)),
                  vec, vec, pl.BlockSpec((SGG, SGB, SGB), lambda i: (0, 0, 0)),
                  pl.BlockSpec((SGB, LANES), lambda i: (0, 0)), pl.BlockSpec((SGW, D), lambda i: (0, 0)), row,
                  pl.BlockSpec((1, D), lambda i: (0, 0))],
        out_specs=[row, row, pl.BlockSpec((tr, SGW), lambda i: (i, 0))],
        out_shape=[jax.ShapeDtypeStruct((s, D), F32), jax.ShapeDtypeStruct((s, D), BF16),
                   jax.ShapeDtypeStruct((s, SGW), BF16)],
        compiler_params=_cparams(("arbitrary",)),
    )(z, z, vgain, vbias, ws, bst, w_out, xin, gate)


def _sgu_bwd(z, dy, vgain, vbias, ws, wst, bst, tr, name):
    s = z.shape[0]

    def body(zu_ref, zv_ref, dy_ref, vg_ref, vb_ref, ws_ref, wst_ref, bs_ref,
             dz_ref, rb_ref, rv_ref, dws_ref, dbs_ref, dvn_s):
        @pl.when(pl.program_id(0) == 0)
        def _():
            rb_ref[...] = jnp.zeros_like(rb_ref)
            rv_ref[...] = jnp.zeros_like(rv_ref)
            dws_ref[...] = jnp.zeros_like(dws_ref)
            dbs_ref[...] = jnp.zeros_like(dbs_ref)
        zu = zu_ref[...].astype(F32)
        zv = zv_ref[...].astype(F32)
        u, gu = _gelu_and_grad(zu)
        v, gv = _gelu_and_grad(zv)
        vn, vhat, rstd = _sgu_ln(v, vg_ref[...], vb_ref[...])
        vnb = vn.astype(BF16)
        dyv = dy_ref[...].astype(F32)
        dmix = (dyv * u).astype(BF16)
        mask = _chunk_mask()
        mask_t = _chunk_mask(transposed=True)
        lane = _lane((SGB, LANES))
        dbs = jnp.zeros((SGB, LANES), F32)
        for g in range(SGG):
            w = jnp.where(mask, ws_ref[g], 0.0).astype(BF16)
            wt = jnp.where(mask_t, wst_ref[g], 0.0).astype(BF16)
            dw = jnp.zeros((SGB, SGB), F32)
            for b in range(tr // SGB):
                rs, cs = slice(b * SGB, (b + 1) * SGB), slice(g * SGC, (g + 1) * SGC)
                mixed = _dot(w, vnb[rs, cs]) + bs_ref[:, g:g + 1]
                dz_ref[rs, cs] = (dyv[rs, cs] * mixed * gu[rs, cs]).astype(BF16)
                dm = dmix[rs, cs]
                dw = dw + _dot_nt(dm, vnb[rs, cs])
                dbs = dbs + jnp.where(lane == g, jnp.sum(dm.astype(F32), axis=-1, keepdims=True), 0.0)
                dvn_s[rs, cs] = _dot(wt, dm)
            dws_ref[g] += jnp.where(mask, dw, 0.0)
        dbs_ref[...] += dbs
        dvn = dvn_s[...]
        rv_ref[0:1, :] += jnp.sum(dvn * vhat, axis=0, keepdims=True)
        rv_ref[1:2, :] += jnp.sum(dvn, axis=0, keepdims=True)
        dvh = dvn * vg_ref[...]
        dv = rstd * (dvh - jnp.mean(dvh, axis=-1, keepdims=True)
                     - vhat * jnp.mean(dvh * vhat, axis=-1, keepdims=True))
        dz_ref[:, SGW:] = (dv * gv).astype(BF16)
        dzf = dz_ref[...].astype(F32)
        rb_ref[0:1, :] += jnp.sum(dzf, axis=0, keepdims=True)

    vec = pl.BlockSpec((1, SGW), lambda i: (0, 0))
    wsp = pl.BlockSpec((SGG, SGB, SGB), lambda i: (0, 0, 0))
    return pl.pallas_call(
        body, name=name, grid=(s // tr,),
        in_specs=[pl.BlockSpec((tr, SGW), lambda i: (i, 0)), pl.BlockSpec((tr, SGW), lambda i: (i, 1)),
                  pl.BlockSpec((tr, SGW), lambda i: (i, 0)), vec, vec, wsp, wsp,
                  pl.BlockSpec((SGB, LANES), lambda i: (0, 0))],
        out_specs=[pl.BlockSpec((tr, 2 * SGW), lambda i: (i, 0)),
                   pl.BlockSpec((8, 2 * SGW), lambda i: (0, 0)),
                   pl.BlockSpec((8, SGW), lambda i: (0, 0)), wsp,
                   pl.BlockSpec((SGB, LANES), lambda i: (0, 0))],
        out_shape=[jax.ShapeDtypeStruct((s, 2 * SGW), BF16), jax.ShapeDtypeStruct((8, 2 * SGW), F32),
                   jax.ShapeDtypeStruct((8, SGW), F32), jax.ShapeDtypeStruct((SGG, SGB, SGB), F32),
                   jax.ShapeDtypeStruct((SGB, LANES), F32)],
        scratch_shapes=[pltpu.VMEM((tr, SGW), F32)],
        compiler_params=_cparams(("arbitrary",)),
    )(z, z, dy, vgain, vbias, ws, wst, bst)


def _final_loss(x, fg, tgt, gprev, yprev, ts, name):
    s, d = x.shape

    def body(x_ref, fg_ref, t_ref, g_ref, y_ref, l_ref, dx_ref, dy_ref, red_ref):
        @pl.when(pl.program_id(0) == 0)
        def _():
            l_ref[...] = jnp.zeros_like(l_ref)
            red_ref[...] = jnp.zeros_like(red_ref)
        xv = x_ref[...]
        r = _rstd_rows(xv)
        xh = xv * r
        err = xh * fg_ref[...] - t_ref[...]
        l_ref[...] += 0.5 * jnp.sum(jnp.mean(err * err, axis=-1, keepdims=True))
        dyo = err * (1.0 / d)
        dxh = dyo * fg_ref[...]
        dx = r * (dxh - xh * jnp.mean(dxh * xh, axis=-1, keepdims=True))
        dx_ref[...] = dx
        dy_ref[...] = (dx * g_ref[...]).astype(BF16)
        red_ref[0:1, :] += jnp.sum(dyo * xh, axis=0, keepdims=True)
        red_ref[1:2, :] += jnp.sum(dx * y_ref[...].astype(F32), axis=0, keepdims=True)

    row = pl.BlockSpec((ts, d), lambda i: (i, 0))
    vec = pl.BlockSpec((1, d), lambda i: (0, 0))
    return pl.pallas_call(
        body, name=name, grid=(s // ts,),
        in_specs=[row, vec, row, vec, row],
        out_specs=[pl.BlockSpec((8, LANES), lambda i: (0, 0)), row, row, pl.BlockSpec((8, d), lambda i: (0, 0))],
        out_shape=[jax.ShapeDtypeStruct((8, LANES), F32), jax.ShapeDtypeStruct((s, d), F32),
                   jax.ShapeDtypeStruct((s, d), BF16), jax.ShapeDtypeStruct((8, d), F32)],
        compiler_params=_cparams(("arbitrary",)),
    )(x, fg, tgt, gprev, yprev)


def _norm_bwd(xin, dh, dxout, ng, sc, gprev, yprev, ts, name, css=()):
    s, d = xin.shape
    has_prev = gprev is not None
    fused = isinstance(dh, tuple)
    na = len(css)
    if fused:
        a, w, tk = dh
        ka, kw = a.shape[-1], w.shape[-1]
        k = ka * (a.shape[0] if a.ndim == 3 else 1)

    def body(*refs):
        if fused:
            x_ref, a_ref, w_ref, dxo_ref, ng_ref, sc_ref = refs[:6]
            rest = refs[6:]
        else:
            x_ref, dh_ref, dxo_ref, ng_ref, sc_ref = refs[:5]
            rest = refs[5:]
        if has_prev:
            g_ref, y_ref = rest[:2]
            rest = rest[2:]
        cs_refs, rest = rest[:na], rest[na:]
        if has_prev:
            dx_ref, dy_ref, red_ref = rest[:3]
            rest = rest[3:]
        else:
            dx_ref, red_ref = rest[:2]
            rest = rest[2:]
        r_refs, sems = rest[:na], rest[na:]
        if na:
            @pl.when(pl.program_id(0) == 0)
            def _():
                for cp in _chip_exchange_copies(cs_refs, r_refs, *sems):
                    cp.start()

        @pl.when(pl.program_id(0) == 0)
        def _():
            red_ref[...] = jnp.zeros_like(red_ref)
        if fused:
            dhv = None
            for g0 in range(0, k, tk):
                av = a_ref[g0 // ka, :, g0 % ka:g0 % ka + tk] if a.ndim == 3 else a_ref[:, g0:g0 + tk]
                wv = w_ref[g0 // kw, :, g0 % kw:g0 % kw + tk] if w.ndim == 3 else w_ref[:, g0:g0 + tk]
                part = _dot_nt(av, wv)
                dhv = part if dhv is None else dhv + part
        else:
            dhv = dh_ref[...]
        xv = x_ref[...]
        r = _rstd_rows(xv)
        xh = xv * r
        dr = dhv * (1.0 + sc_ref[...])
        t = dr * ng_ref[...]
        dx = dxo_ref[...] + r * (t - xh * jnp.mean(t * xh, axis=-1, keepdims=True))
        dx_ref[...] = dx
        red_ref[0:1, :] += jnp.sum(dhv, axis=0, keepdims=True)
        red_ref[1:2, :] += jnp.sum(dhv * (xh * ng_ref[...]), axis=0, keepdims=True)
        red_ref[2:3, :] += jnp.sum(dr * xh, axis=0, keepdims=True)
        if has_prev:
            dy_ref[...] = (dx * g_ref[...]).astype(BF16)
            red_ref[3:4, :] += jnp.sum(dx * y_ref[...].astype(F32), axis=0, keepdims=True)
        if na:
            @pl.when(pl.program_id(0) == s // ts - 1)
            def _():
                for cp in _chip_exchange_copies(cs_refs, r_refs, *sems):
                    cp.wait()

    row = pl.BlockSpec((ts, d), lambda i: (i, 0))
    vec = pl.BlockSpec((1, d), lambda i: (0, 0))
    red = pl.BlockSpec((8, d), lambda i: (0, 0))
    if fused:
        a_spec = (pl.BlockSpec((a.shape[0], ts, ka), lambda i: (0, i, 0)) if a.ndim == 3
                  else pl.BlockSpec((ts, ka), lambda i: (i, 0)))
        w_spec = pl.BlockSpec(w.shape, (lambda i: (0, 0, 0)) if w.ndim == 3 else (lambda i: (0, 0)))
        dh_specs, dh_args = [a_spec, w_spec], (a, w)
    else:
        dh_specs, dh_args = [row], (dh,)
    if has_prev:
        in_specs, args = [row] + dh_specs + [row, vec, vec, vec, row], (xin,) + dh_args + (dxout, ng, sc, gprev, yprev)
        out_specs = [row, row, red]
        out_shape = [jax.ShapeDtypeStruct((s, d), F32), jax.ShapeDtypeStruct((s, d), BF16),
                     jax.ShapeDtypeStruct((8, d), F32)]
    else:
        in_specs, args = [row] + dh_specs + [row, vec, vec], (xin,) + dh_args + (dxout, ng, sc)
        out_specs = [row, red]
        out_shape = [jax.ShapeDtypeStruct((s, d), F32), jax.ShapeDtypeStruct((8, d), F32)]
    return pl.pallas_call(
        body, name=name, grid=(s // ts,), in_specs=in_specs + [HBM_SPEC] * na,
        out_specs=out_specs + [HBM_SPEC] * na,
        out_shape=out_shape + [jax.ShapeDtypeStruct((3,) + cs.shape[1:], cs.dtype) for cs in css],
        scratch_shapes=[pltpu.SemaphoreType.DMA((3 * na,))] * 2 if na else [],
        compiler_params=_cparams(("arbitrary",)),
    )(*args, *css)


def _ada_mod(c_all, ada_w, ada_b):
    nb = c_all.shape[0]
    da = ada_w.shape[2]

    def body(c_ref, w_ref, b_ref, o_ref, ca_ref):
        cv = c_ref[...]
        ca = cv * _sigmoid(cv)
        ca_ref[...] = ca
        o_ref[0] = lax.dot_general(ca, w_ref[0], (((1,), (0,)), ((), ())), precision=lax.Precision.HIGHEST,
                                   preferred_element_type=F32) + b_ref[0]

    return pl.pallas_call(
        body, name="ada_mod", grid=(2,),
        in_specs=[pl.BlockSpec((nb, D), lambda i: (0, 0)), pl.BlockSpec((1, D, da), lambda i: (i, 0, 0)),
                  pl.BlockSpec((1, 1, da), lambda i: (i, 0, 0))],
        out_specs=[pl.BlockSpec((1, nb, da), lambda i: (i, 0, 0)), pl.BlockSpec((nb, D), lambda i: (0, 0))],
        out_shape=[jax.ShapeDtypeStruct((2, nb, da), F32), jax.ShapeDtypeStruct((nb, D), F32)],
        compiler_params=_cparams(("arbitrary",)),
    )(c_all, ada_w, ada_b)


def _ada_w_grad(c_act_t, dmod):
    nb = c_act_t.shape[1]
    da = dmod.shape[2]
    tn = 512

    def body(c_ref, d_ref, o_ref):
        acc = c_ref[:, 0:1] * d_ref[0, 0:1, :]
        for b in range(1, nb):
            acc = acc + c_ref[:, b:b + 1] * d_ref[0, b:b + 1, :]
        o_ref[0] = acc

    return pl.pallas_call(
        body, name="ada_w_grad", grid=(2, da // tn),
        in_specs=[pl.BlockSpec((D, nb), lambda i, j: (0, 0)), pl.BlockSpec((1, nb, tn), lambda i, j: (i, 0, j))],
        out_specs=pl.BlockSpec((1, D, tn), lambda i, j: (i, 0, j)),
        out_shape=jax.ShapeDtypeStruct((2, D, da), F32),
        compiler_params=_cparams(("arbitrary", "arbitrary")),
    )(c_act_t, dmod)


def _conv_planes(cw, cb):
    cwp = jnp.swapaxes(cw.reshape(3, 2, DFF), 0, 1)
    return jnp.pad(cwp, ((0, 0), (0, 5), (0, 0))), cb.reshape(2, 1, DFF)


def _local_step(x, tgt, mod, wts, small, comm=None):
    wts = dict(wts)
    s = x.shape[0]
    ts, tq, tr, tp = TS, TQ, TR, TP
    tkw = min(TKW, s)
    tf = min(256, s)
    zb = lambda n: jnp.zeros((1, n), F32)
    m6 = mod.reshape(2, 6, 1, D)
    sh1, sc1, g1, sh2, sc2, g2 = ([m6[i, k] for i in range(2)] for k in range(6))
    n1g, n2g = small["norm1_g"], small["norm2_g"]
    row = lambda a, i: a[i:i + 1]

    qg2 = jnp.tile(small["fox_q_gain"], (1, 2))
    kg2 = jnp.tile(small["fox_k_gain"], (1, 2))
    bfp = jnp.pad(small["fox_b_f"], ((0, 0), (0, LANES - H)))
    proj, h1 = _norm_mod_matmul(x, row(n1g, 0), sc1[0], sh1[0], wts["fox_w_in"], zb(FOX_NP), F32, ts, 1408, "fox_in")
    qa, ka, va = _fox_post(proj, qg2, kg2, bfp, tp, "fox_post")
    att, ql, gathered = _attn_fwd(qa, ka, va, tq, "attn_fwd", shards=comm["shards"] if comm else ())
    if comm:
        wts.update(comm["make_wts"](gathered))
    x1, y0, gated = _gate_out(att, proj, wts["fox_w_out"], x, g1[0], ts, "fox_gate_out")

    def ffn_fwd(xin, i, tag):
        cw, cb = _conv_planes(small["ffn_conv_w"][i], small["ffn_conv_b"][i])
        a, h = _norm_mod_matmul(xin, row(n2g, i), sc2[i], sh2[i], wts["ffn_w_up"][i], zb(2 * DFF), BF16, ts, 1408,
                                "ffn_up" + tag, planes=2)
        xo, y, f, ap = _conv_down(a, cw, cb, wts["ffn_w_down"][i], xin, g2[i], min(256, s), "ffn_conv_down" + tag)
        return xo, (a, h, f, y, cw, ap)

    x2, ffn0 = ffn_fwd(x1, 0, "0")

    bst = jnp.pad(small["sgu_b_s"].T, ((0, 0), (0, LANES - SGG)))
    ws = small["sgu_w_s"]
    z, h3 = _norm_mod_matmul(x2, row(n1g, 1), sc1[1], sh1[1], wts["sgu_w_in"], small["sgu_b_in"], BF16, ts, 1024,
                             "sgu_in")
    x3, y1, yy = _sgu_fwd(z, small["sgu_v_gain"], small["sgu_v_bias"], ws, bst, wts["sgu_w_out"], x2, g1[1], tr,
                          "sgu_mix_out")
    x4, ffn1 = ffn_fwd(x3, 1, "1")

    lsum, dx4, dy, redf = _final_loss(x4, small["final_g"], tgt, g2[1], ffn1[3], ts, "final_loss")
    grads = {"final_g": redf[0]}
    dmod = [[None] * 6, [None] * 6]
    dmod[1][5] = redf[1]

    def ffn_bwd(dxo, dy2, xin, i, saved, gprev, yprev, tag):
        a, h, f, _, cw, ap = saved
        wd, wu = wts["ffn_w_down"][i], wts["ffn_w_up"][i]
        g_wd = _matmul(f, dy2, True, False, 1408, D, tkw, BF16, "ffn_dwdown" + tag)
        da, redc = _down_bwd_conv(dy2, wd, a, ap, cw, min(256, s), "ffn_down_bwd_conv" + tag)
        g_wu = _matmul(h, da, True, False, D, 1408, tkw, BF16, "ffn_dwup" + tag, out_parts=N_CHIP)
        outs = _norm_bwd(xin, (da, wu, 1408), dxo, row(n2g, i), sc2[i], gprev, yprev, tf, "ffn_dh_norm_bwd" + tag)
        return outs, g_wd, g_wu, redc

    (dx3, dy1, red), g_wd1, g_wu1, redc1 = ffn_bwd(dx4, dy, x3, 1, ffn1, g1[1], y1, "1")
    dmod[1][3], dmod[1][4], dn2g1, dmod[1][2] = red[0], red[1], red[2], red[3]

    g_swo = _matmul(yy, dy1, True, False, 1024, D, tkw, BF16, "sgu_dwout")
    dyy = _matmul_wt(dy1, wts["sgu_w_out"], 1024, D, BF16, ts, "sgu_dyy")
    wst = jnp.swapaxes(ws, 1, 2)
    dz, rb, rv, dws, dbst = _sgu_bwd(z, dyy, small["sgu_v_gain"], small["sgu_v_bias"], ws, wst, bst, tr, "sgu_mix_bwd")
    g_swi = _matmul(h3, dz, True, False, D, 1024, tkw, BF16, "sgu_dwin", out_parts=N_CHIP)
    dx2, dy2_0, red = _norm_bwd(x2, (dz, wts["sgu_w_in"], 1024), dx3, row(n1g, 1), sc1[1], g2[0], ffn0[3], tf,
                                "sgu_dh_norm_bwd")
    dmod[1][0], dmod[1][1], dn1g1, dmod[0][5] = red[0], red[1], red[2], red[3]

    (dx1, dy0, red), g_wd0, g_wu0, redc0 = ffn_bwd(dx2, dy2_0, x1, 0, ffn0, g1[0], y0, "0")
    dmod[0][3], dmod[0][4], dn2g0, dmod[0][2] = red[0], red[1], red[2], red[3]

    g_fwo = _matmul(gated, dy0, True, False, D, D, tkw, BF16, "fox_dwout")
    glist = comm["rs_lists"]([g_fwo, g_swi, g_swo, g_wu0, g_wu1, g_wd0, g_wd1]) if comm else []
    doa, dop, sibs = _attn_bwd_prep(dy0, wts["fox_w_out"], att, proj, ts, "attn_bwd_prep", glist=glist)
    css = comm["rs_chip_sums"](glist, sibs, "") if comm else []
    dqa, dka, dva, rcvs = _attn_bwd(ql, ka, va, doa, tq, "attn_bwd", css=css)
    dproj, redx = _fox_post_bwd(proj, dqa, dka, dva, dop, qg2, kg2, bfp, tp, "fox_post_bwd")
    g_fwi = _matmul(h1, dproj, True, False, D, 1408, tkw, BF16, "fox_dwin")
    css_fox = comm["rs_prepare_fox"](g_fwi) if comm else []
    outs = _norm_bwd(x, (dproj, wts["fox_w_in"], 1408), dx1, row(n1g, 0), sc1[0], None, None, tf,
                     "fox_dh_norm_bwd", css=css_fox)
    dx0, red = outs[0], outs[1]
    css, rcvs = list(css_fox) + list(css), list(outs[2:]) + list(rcvs)
    dmod[0][0], dmod[0][1], dn1g0 = red[0], red[1], red[2]

    grads.update(
        fox_w_in=g_fwi, fox_w_out=g_fwo, sgu_w_in=g_swi, sgu_w_out=g_swo,
        ffn_w_up=[g_wu0, g_wu1], ffn_w_down=[g_wd0, g_wd1],
        fox_q_gain=redx[0, :DH] + redx[0, DH:], fox_k_gain=redx[1, :DH] + redx[1, DH:], fox_b_f=redx[2, :H],
        sgu_b_in=rb[0], sgu_v_gain=rv[0], sgu_v_bias=rv[1], sgu_w_s=dws, sgu_b_s=dbst[:, :SGG].T,
        ffn_conv_w=jnp.stack([jnp.swapaxes(r[:, 0:3], 0, 1).reshape(3, 2 * DFF) for r in (redc0, redc1)]),
        ffn_conv_b=jnp.stack([r[:, 3].reshape(2 * DFF) for r in (redc0, redc1)]),
        norm1_g=jnp.stack([dn1g0, dn1g1]), norm2_g=jnp.stack([dn2g0, dn2g1]),
    )
    dmod_arr = jnp.stack([jnp.concatenate(dmod[0]), jnp.concatenate(dmod[1])])
    return lsum[0, 0], dx0, grads, dmod_arr, (css, rcvs)


N_DEV = 8
N_CHIP = 4
HBM_SPEC = pl.BlockSpec(memory_space=pltpu.HBM)
VMEM_SPEC = pl.BlockSpec(memory_space=pltpu.VMEM)


def _mesh_pos():
    return lax.axis_index("x"), lax.axis_index("y"), lax.axis_index("c")


def _other_chips(x, y):
    return [(1 - x, y), (x, 1 - y), (1 - x, 1 - y)]


def _remote(src, dst, ssem, rsem, dev):
    return pltpu.make_async_remote_copy(src_ref=src, dst_ref=dst, send_sem=ssem, recv_sem=rsem,
                                        device_id=dev, device_id_type=MESH)


def _allgather8(xb, name):
    m_per, n = xb.shape

    def body(x_ref, out_ref, send_sems, recv_sems, local_sem):
        x, y, c = _mesh_pos()
        me, sibling = (x, y, c), (x, y, 1 - c)
        chips = _other_chips(x, y)

        def rows(px, py, pc):
            return out_ref.at[pl.ds((4 * px + 2 * py + pc) * m_per, m_per), :]

        def copy(k, block, to, src=None):
            return _remote(rows(*block) if src is None else src, rows(*block),
                           send_sems.at[k], recv_sems.at[k], to)

        mine = pltpu.make_async_copy(x_ref, rows(*me), local_sem)
        mine.start()
        first = [copy(0, me, sibling, src=x_ref)]
        first += [copy(1 + j, me, (*chip, c), src=x_ref) for j, chip in enumerate(chips)]
        for cp in first:
            cp.start()
        passed = [copy(4 + j, (*chip, c), sibling) for j, chip in enumerate(chips)]
        for j, chip in enumerate(chips):
            copy(1 + j, (*chip, c), me).wait_recv()
            passed[j].start()
        copy(0, sibling, me).wait_recv()
        for j, chip in enumerate(chips):
            copy(4 + j, (*chip, 1 - c), me).wait_recv()
        for cp in first + passed:
            cp.wait_send()
        mine.wait()

    return pl.pallas_call(
        body, name=name,
        out_shape=jax.ShapeDtypeStruct((N_DEV * m_per, n), xb.dtype),
        in_specs=[VMEM_SPEC], out_specs=VMEM_SPEC,
        scratch_shapes=[pltpu.SemaphoreType.DMA((7,)), pltpu.SemaphoreType.DMA((7,)), pltpu.SemaphoreType.DMA],
        compiler_params=pltpu.CompilerParams(vmem_limit_bytes=V7X_VMEM_LIMIT),
    )(xb)


def _gather_shards(shards, name):
    na = len(shards)

    def body(*refs):
        p_refs, o_refs = refs[:na], refs[na:2 * na]
        send_sems, recv_sems, pass_send, pass_recv = refs[2 * na:]
        x, y, c = _mesh_pos()
        me = 2 * x + y
        sibling = (x, y, 1 - c)
        chips = _other_chips(x, y)

        def half(a, ci, hf):
            rh = shards[a].shape[0] // 2
            return o_refs[a].at[ci, pl.ds(hf * rh, rh), :]

        sends = []
        for a in range(na):
            rh = shards[a].shape[0] // 2
            for k, chip in enumerate(chips):
                sends.append(_remote(p_refs[a].at[pl.ds(c * rh, rh), :], half(a, me, c),
                                     send_sems.at[3 * a + k], recv_sems.at[3 * a + k], (*chip, c)))
        for cp in sends:
            cp.start()
        passed = []
        for a in range(na):
            for k, chip in enumerate(chips):
                ci = 2 * chip[0] + chip[1]
                _remote(half(a, ci, c), half(a, ci, c), send_sems.at[3 * a + k], recv_sems.at[3 * a + k],
                        (*chip, c)).wait_recv()
                cp = _remote(half(a, ci, c), half(a, ci, c), pass_send.at[3 * a + k], pass_recv.at[3 * a + k], sibling)
                cp.start()
                passed.append(cp)
        for a in range(na):
            for k, chip in enumerate(chips):
                ci = 2 * chip[0] + chip[1]
                _remote(half(a, ci, 1 - c), half(a, ci, 1 - c), pass_send.at[3 * a + k], pass_recv.at[3 * a + k],
                        sibling).wait_recv()
        for cp in sends + passed:
            cp.wait_send()

    return pl.pallas_call(
        body, name=name,
        out_shape=[jax.ShapeDtypeStruct((N_CHIP,) + p.shape, p.dtype) for p in shards],
        in_specs=[HBM_SPEC] * na, out_specs=[HBM_SPEC] * na,
        scratch_shapes=[pltpu.SemaphoreType.DMA((3 * na,))] * 4,
    )(*shards)


def _rs_to_sibling(gs, name):
    na = len(gs)

    def body(*refs):
        cps = _sibling_copies(refs[:na], refs[na:2 * na], refs[2 * na], refs[2 * na + 1])
        for cp in cps:
            cp.start()
        for cp in cps:
            cp.wait()

    return pl.pallas_call(
        body, name=name,
        out_shape=[jax.ShapeDtypeStruct((g.shape[0], g.shape[1] // 2, g.shape[2]), g.dtype) for g in gs],
        in_specs=[HBM_SPEC] * na, out_specs=[HBM_SPEC] * na,
        scratch_shapes=[pltpu.SemaphoreType.DMA((na,)), pltpu.SemaphoreType.DMA((na,))],
    )(*gs)


def _rs_chip_sum(g, sib, c_arr, tr, name):
    nc, r, n = g.shape
    rh = r // 2
    g4 = g.reshape(nc, 2, rh, n)

    def body(c_ref, g_ref, s_ref, o_ref):
        o_ref[...] = (g_ref[0].astype(F32) + s_ref[...].astype(F32)).astype(BF16)

    return pl.pallas_call(
        body, name=name, out_shape=jax.ShapeDtypeStruct((nc, rh, n), BF16),
        grid_spec=pltpu.PrefetchScalarGridSpec(
            num_scalar_prefetch=1, grid=(nc, rh // tr),
            in_specs=[pl.BlockSpec((1, 1, tr, n), lambda j, i, cr: (j, cr[0], i, 0)),
                      pl.BlockSpec((1, tr, n), lambda j, i, cr: (j, i, 0))],
            out_specs=pl.BlockSpec((1, tr, n), lambda j, i, cr: (j, i, 0))),
        compiler_params=_cparams(("arbitrary", "arbitrary")),
    )(c_arr, g4, sib)


def _rs_final_sum(cs, rcv, me_arr, tr, name):
    nc, rh, n = cs.shape

    def body(m_ref, c_ref, r_ref, o_ref):
        acc = c_ref[0].astype(F32)
        for k in range(3):
            acc = acc + r_ref[k].astype(F32)
        o_ref[...] = acc

    return pl.pallas_call(
        body, name=name, out_shape=jax.ShapeDtypeStruct((rh, n), F32),
        grid_spec=pltpu.PrefetchScalarGridSpec(
            num_scalar_prefetch=1, grid=(rh // tr,),
            in_specs=[pl.BlockSpec((1, tr, n), lambda i, mr: (mr[0], i, 0)),
                      pl.BlockSpec((3, tr, n), lambda i, mr: (0, i, 0))],
            out_specs=pl.BlockSpec((tr, n), lambda i, mr: (i, 0))),
        compiler_params=_cparams(("arbitrary",)),
    )(me_arr, cs, rcv)


def _rs_swap_halves(halves, name):
    na = len(halves)

    def body(*refs):
        h_refs, o_refs, ssems, rsems = refs[:na], refs[na:2 * na], refs[2 * na], refs[2 * na + 1]
        x, y, c = _mesh_pos()
        cps = []
        for a in range(na):
            cp = _remote(h_refs[a], o_refs[a], ssems.at[a], rsems.at[a], (x, y, 1 - c))
            cp.start()
            cps.append(cp)
        for cp in cps:
            cp.wait()

    return pl.pallas_call(
        body, name=name, out_shape=[jax.ShapeDtypeStruct(h.shape, h.dtype) for h in halves],
        in_specs=[HBM_SPEC] * na, out_specs=[HBM_SPEC] * na,
        scratch_shapes=[pltpu.SemaphoreType.DMA((na,)), pltpu.SemaphoreType.DMA((na,))],
    )(*halves)


def _join_columns(parts, n_out, name):
    p, k, c = parts.shape
    tr = 128

    def body(w_ref, o_ref):
        for j in range(p):
            o_ref[:, j * c:(j + 1) * c] = w_ref[j]
        o_ref[:, p * c:] = jnp.zeros((tr, n_out - p * c), parts.dtype)

    return pl.pallas_call(
        body, name=name, grid=(k // tr,),
        in_specs=[pl.BlockSpec((p, tr, c), lambda i: (0, i, 0))],
        out_specs=pl.BlockSpec((tr, n_out), lambda i: (i, 0)),
        out_shape=jax.ShapeDtypeStruct((k, n_out), parts.dtype),
        compiler_params=_cparams(("arbitrary",)),
    )(parts)


def _split_columns(g, p, c, name):
    k, n = g.shape
    tr = 128

    def body(g_ref, o_ref):
        for j in range(p):
            o_ref[j] = g_ref[:, j * c:(j + 1) * c]

    return pl.pallas_call(
        body, name=name, grid=(k // tr,),
        in_specs=[pl.BlockSpec((tr, n), lambda i: (i, 0))],
        out_specs=pl.BlockSpec((p, tr, c), lambda i: (0, i, 0)),
        out_shape=jax.ShapeDtypeStruct((p, k, c), g.dtype),
        compiler_params=_cparams(("arbitrary",)),
    )(g)


def _sum8(g, name):
    nd, r, n = g.shape

    def body(g_ref, o_ref):
        acc = g_ref[0]
        for k in range(1, nd):
            acc = acc + g_ref[k]
        o_ref[...] = acc

    return pl.pallas_call(
        body, name=name, grid=(r // 8,),
        in_specs=[pl.BlockSpec((nd, 8, n), lambda i: (0, i, 0))],
        out_specs=pl.BlockSpec((8, n), lambda i: (i, 0)),
        out_shape=jax.ShapeDtypeStruct((r, n), F32),
        compiler_params=_cparams(("arbitrary",)),
    )(g)


def _adamw(w, g, m, v, name):
    r, n = w.shape
    tr = next(t for t in (128, 64, 32, 16, 8) if r % t == 0)
    bc1 = 1.0 - ADAM_B1 ** ADAM_STEP
    bc2 = 1.0 - ADAM_B2 ** ADAM_STEP

    def body(w_ref, g_ref, m_ref, v_ref, d_ref, mo_ref, vo_ref):
        gv = g_ref[...]
        mn = ADAM_B1 * m_ref[...] + (1.0 - ADAM_B1) * gv
        vn = ADAM_B2 * v_ref[...] + (1.0 - ADAM_B2) * (gv * gv)
        d_ref[...] = -ADAM_LR * ((mn / bc1) / (jnp.sqrt(vn / bc2) + ADAM_EPS) + ADAM_WD * w_ref[...])
        mo_ref[...] = mn
        vo_ref[...] = vn

    blk = pl.BlockSpec((tr, n), lambda i: (i, 0))
    shp = jax.ShapeDtypeStruct((r, n), F32)
    return pl.pallas_call(
        body, name=name, grid=(r // tr,), in_specs=[blk] * 4, out_specs=[blk] * 3, out_shape=[shp] * 3,
        compiler_params=_cparams(("arbitrary",)),
    )(w, g, m, v)


ROW = 1024
BIG = ("fox_w_in", "fox_w_out", "sgu_w_in", "sgu_w_out", "ffn_w_up", "ffn_w_down")
SMALL_SHARDED = ("sgu_b_in", "sgu_v_gain", "sgu_v_bias", "ffn_conv_w")
SMALL_REPL = ("fox_b_f", "fox_q_gain", "fox_k_gain", "sgu_w_s", "sgu_b_s", "ffn_conv_b", "ada_b",
              "norm1_g", "norm2_g", "final_g")
WEIGHTS = ("fox_w_in", "fox_b_f", "fox_q_gain", "fox_k_gain", "fox_w_out", "sgu_w_in", "sgu_b_in", "sgu_v_gain",
           "sgu_v_bias", "sgu_w_s", "sgu_b_s", "sgu_w_out", "ffn_w_up", "ffn_conv_w", "ffn_conv_b", "ffn_w_down",
           "ada_w", "ada_b", "norm1_g", "norm2_g", "final_g")


def _rows_of(a, mult=1):
    flat = a.reshape(-1)
    rows = -(-flat.shape[0] // ROW)
    rows = -(-rows // mult) * mult
    return jnp.pad(flat, (0, rows * ROW - flat.shape[0])).reshape(rows, ROW)


def _pack(parts, mult, total=None):
    p = jnp.concatenate([_rows_of(a, mult) for a in parts], axis=0)
    if total is not None:
        p = jnp.pad(p, ((0, total - p.shape[0]), (0, 0)))
    return p


def _unpack(pack, shapes, mult):
    out, r0 = [], 0
    for shp in shapes:
        size = int(np.prod(shp))
        rows = -(-(-(-size // ROW)) // mult) * mult
        out.append(pack[r0:r0 + rows].reshape(-1)[:size].reshape(shp))
        r0 += rows
    return out


def _big_shards(t):
    return [t["fox_w_in"][0], t["fox_w_out"][0], t["sgu_w_in"][0], t["sgu_w_out"][0],
            t["ffn_w_up"][0], t["ffn_w_up"][1], t["ffn_w_down"][0], t["ffn_w_down"][1]]


def _row_tile(rows):
    return next(t for t in (512, 352, 256, 128, 64) if rows % t == 0)


def kernel(x, c, fox_w_in, fox_b_f, fox_q_gain, fox_k_gain, fox_w_out, sgu_w_in, sgu_b_in, sgu_v_gain, sgu_v_bias, sgu_w_s, sgu_b_s, sgu_w_out, ffn_w_up, ffn_conv_w, ffn_conv_b, ffn_w_down, ada_w, ada_b, norm1_g, norm2_g, final_g, loss_target, m_fox_w_in, m_fox_b_f, m_fox_q_gain, m_fox_k_gain, m_fox_w_out, m_sgu_w_in, m_sgu_b_in, m_sgu_v_gain, m_sgu_v_bias, m_sgu_w_s, m_sgu_b_s, m_sgu_w_out, m_ffn_w_up, m_ffn_conv_w, m_ffn_conv_b, m_ffn_w_down, m_ada_w, m_ada_b, m_norm1_g, m_norm2_g, m_final_g, v_fox_w_in, v_fox_b_f, v_fox_q_gain, v_fox_k_gain, v_fox_w_out, v_sgu_w_in, v_sgu_b_in, v_sgu_v_gain, v_sgu_v_bias, v_sgu_w_s, v_sgu_b_s, v_sgu_w_out, v_ffn_w_up, v_ffn_conv_w, v_ffn_conv_b, v_ffn_w_down, v_ada_w, v_ada_b, v_norm1_g, v_norm2_g, v_final_g):
    w = dict(fox_w_in=fox_w_in, fox_b_f=fox_b_f, fox_q_gain=fox_q_gain, fox_k_gain=fox_k_gain, fox_w_out=fox_w_out,
             sgu_w_in=sgu_w_in, sgu_b_in=sgu_b_in, sgu_v_gain=sgu_v_gain, sgu_v_bias=sgu_v_bias, sgu_w_s=sgu_w_s,
             sgu_b_s=sgu_b_s, sgu_w_out=sgu_w_out, ffn_w_up=ffn_w_up, ffn_conv_w=ffn_conv_w, ffn_conv_b=ffn_conv_b,
             ffn_w_down=ffn_w_down, ada_w=ada_w, ada_b=ada_b, norm1_g=norm1_g, norm2_g=norm2_g, final_g=final_g)
    mom = dict(fox_w_in=m_fox_w_in, fox_b_f=m_fox_b_f, fox_q_gain=m_fox_q_gain, fox_k_gain=m_fox_k_gain,
               fox_w_out=m_fox_w_out, sgu_w_in=m_sgu_w_in, sgu_b_in=m_sgu_b_in, sgu_v_gain=m_sgu_v_gain,
               sgu_v_bias=m_sgu_v_bias, sgu_w_s=m_sgu_w_s, sgu_b_s=m_sgu_b_s, sgu_w_out=m_sgu_w_out,
               ffn_w_up=m_ffn_w_up, ffn_conv_w=m_ffn_conv_w, ffn_conv_b=m_ffn_conv_b, ffn_w_down=m_ffn_w_down,
               ada_w=m_ada_w, ada_b=m_ada_b, norm1_g=m_norm1_g, norm2_g=m_norm2_g, final_g=m_final_g)
    var = dict(fox_w_in=v_fox_w_in, fox_b_f=v_fox_b_f, fox_q_gain=v_fox_q_gain, fox_k_gain=v_fox_k_gain,
               fox_w_out=v_fox_w_out, sgu_w_in=v_sgu_w_in, sgu_b_in=v_sgu_b_in, sgu_v_gain=v_sgu_v_gain,
               sgu_v_bias=v_sgu_v_bias, sgu_w_s=v_sgu_w_s, sgu_b_s=v_sgu_b_s, sgu_w_out=v_sgu_w_out,
               ffn_w_up=v_ffn_w_up, ffn_conv_w=v_ffn_conv_w, ffn_conv_b=v_ffn_conv_b, ffn_w_down=v_ffn_w_down,
               ada_w=v_ada_w, ada_b=v_ada_b, norm1_g=v_norm1_g, norm2_g=v_norm2_g, final_g=v_final_g)

    ax, ay, ac = _mesh_pos()
    chip = 2 * ax + ay
    dev = 2 * chip + ac

    small_shard_shapes = tuple(w[n].shape for n in SMALL_SHARDED)
    blk = _pack([c] + [w[n] for n in SMALL_SHARDED], 1, 16)
    gat = _allgather8(blk, "gather_small").reshape(N_DEV, 16, ROW)
    c_all = gat[:, 0, :]
    per_chip = [_unpack(gat[2 * j, 1:], small_shard_shapes, 1) for j in range(N_CHIP)]
    full_small = {n: jnp.concatenate([per_chip[j][i] for j in range(N_CHIP)], axis=-1)
                  for i, n in enumerate(SMALL_SHARDED)}

    mine = [a.astype(BF16) for a in _big_shards(w)]
    with_own = lambda gat, own: [lax.dynamic_update_slice(g_, m_[None], (chip, 0, 0)) for g_, m_ in zip(gat, own)]
    fwi, = with_own(_gather_shards(mine[:1], "gather_fox_w_in"), mine[:1])
    fwi_full = _join_columns(fwi, FOX_NP, "join_fox_w_in")
    wts = dict(fox_w_in=fwi_full)

    def make_wts(gathered):
        fwo, swi, swo, up0, up1, dn0, dn1 = with_own(gathered, mine[1:])
        return dict(fox_w_out=fwo.reshape(D, D), sgu_w_in=swi, sgu_w_out=swo.reshape(SGW, D),
                    ffn_w_up=[up0, up1], ffn_w_down=[dn0.reshape(DFF, D), dn1.reshape(DFF, D)])

    c_arr = jnp.reshape(ac, (1,)).astype(jnp.int32)
    me_arr = jnp.reshape(chip, (1,)).astype(jnp.int32)

    def rs_chip_sums(glist, sibs, tag):
        return [_rs_chip_sum(g_, s_, c_arr, _row_tile(s_.shape[1]), "rs_chip_sum%s%d" % (tag, a))
                for a, (g_, s_) in enumerate(zip(glist, sibs))]

    def rs_lists(gl):
        g_fwo, g_swi, g_swo, g_wu0, g_wu1, g_wd0, g_wd1 = gl
        return [g_fwo.reshape(N_CHIP, 256, D), g_swi, g_swo.reshape(N_CHIP, 512, D), g_wu0, g_wu1,
                g_wd0.reshape(N_CHIP, 704, D), g_wd1.reshape(N_CHIP, 704, D)]

    def rs_prepare_fox(g_fwi):
        glist = [_split_columns(g_fwi, N_CHIP, FOX_N // N_CHIP, "split_fox_w_in")]
        return rs_chip_sums(glist, _rs_to_sibling(glist, "rs_sibling_fox"), "_fox")

    comm = dict(shards=mine[1:], make_wts=make_wts, rs_lists=rs_lists, rs_chip_sums=rs_chip_sums,
                rs_prepare_fox=rs_prepare_fox)

    da = ada_w.shape[2]
    ada_b_cols = lax.dynamic_slice_in_dim(ada_b, chip * da, da, axis=1)[:, None, :]
    mod_cols, c_act = _ada_mod(c_all, ada_w, ada_b_cols)
    mod_all = _allgather8(mod_cols.reshape(-1, ROW), "gather_mod").reshape(N_DEV, 2, N_DEV, da)
    mod_mine = lax.dynamic_index_in_dim(mod_all[0::2], dev, axis=2, keepdims=False)
    mod = jnp.swapaxes(mod_mine, 0, 1).reshape(2, N_CHIP * da)

    small = dict(norm1_g=norm1_g, norm2_g=norm2_g, final_g=final_g[None], fox_q_gain=fox_q_gain,
                 fox_k_gain=fox_k_gain, fox_b_f=fox_b_f, sgu_b_in=full_small["sgu_b_in"],
                 sgu_v_gain=full_small["sgu_v_gain"], sgu_v_bias=full_small["sgu_v_bias"], sgu_w_s=sgu_w_s[0],
                 sgu_b_s=sgu_b_s[0], ffn_conv_w=full_small["ffn_conv_w"], ffn_conv_b=ffn_conv_b)
    loss_dev, dx, g, dmod, (css, rcvs) = _local_step(x[0], loss_target[0], mod, wts, small, comm)

    g["ada_b"] = dmod
    g["loss"] = loss_dev
    small_names = ("ada_b",) + SMALL_SHARDED + tuple(n for n in SMALL_REPL if n != "ada_b") + ("loss",)
    gs = _pack([g[n] for n in small_names], 1)
    rows_s = -(-gs.shape[0] // 8) * 8
    gs = jnp.pad(gs, ((0, rows_s - gs.shape[0]), (0, 0)))
    gs_all = _allgather8(gs, "gather_small_grads").reshape(N_DEV, rows_s, ROW)
    gsum = _sum8(gs_all, "sum_small_grads")
    full_shapes = {n: w[n].shape for n in SMALL_REPL}
    full_shapes.update({n: w[n].shape[:-1] + (w[n].shape[-1] * N_CHIP,) for n in SMALL_SHARDED})
    full_shapes["loss"] = ()
    gfull = dict(zip(small_names, _unpack(gsum, [full_shapes[n] for n in small_names], 1)))
    grads = {n: gfull[n] for n in SMALL_REPL}
    for n in SMALL_SHARDED:
        width = w[n].shape[-1]
        grads[n] = lax.dynamic_slice_in_dim(gfull[n], chip * width, width, axis=gfull[n].ndim - 1)
    dmod_all = gs_all[:, :12, :].reshape(N_DEV, 2, N_CHIP * da)
    dmod_cols = jnp.swapaxes(lax.dynamic_slice_in_dim(dmod_all, chip * da, da, axis=2), 0, 1)
    grads["ada_w"] = _ada_w_grad(c_act.T, dmod_cols)

    halves =[_rs_final_sum(cs_, r_, me_arr, _row_tile(cs_.shape[1]), "rs_final_sum%d" % a)
              for a, (cs_, r_) in enumerate(zip(css, rcvs))]
    others = _rs_swap_halves(halves, "rs_swap")
    red = [jnp.concatenate([jnp.where(ac == 0, h_, o_), jnp.where(ac == 0, o_, h_)]) for h_, o_ in zip(halves, others)]
    grads.update(fox_w_in=red[0], fox_w_out=red[1], sgu_w_in=red[2], sgu_w_out=red[3],
                 ffn_w_up=jnp.stack([red[4], red[5]]), ffn_w_down=jnp.stack([red[6], red[7]]))

    delta, new_m, new_v = {}, {}, {}
    for n in BIG + ("ada_w",):
        shp = w[n].shape
        two_d = lambda a: a.reshape(-1, shp[-1])
        d_, m_, v_ = _adamw(two_d(w[n]), two_d(grads[n]), two_d(mom[n]), two_d(var[n]), "adamw_" + n)
        delta[n], new_m[n], new_v[n] = d_.reshape(shp), m_.reshape(shp), v_.reshape(shp)
    rest = SMALL_SHARDED + SMALL_REPL
    packs = [_pack([t[n] for n in rest], 1) for t in (w, grads, mom, var)]
    rows_r = -(-packs[0].shape[0] // 8) * 8
    packs = [jnp.pad(p, ((0, rows_r - p.shape[0]), (0, 0))) for p in packs]
    outs = _adamw(*packs, "adamw_small")
    for t, o in zip((delta, new_m, new_v), outs):
        t.update(zip(rest, _unpack(o, [w[n].shape for n in rest], 1)))

    loss = gfull["loss"]
    return (loss, dx[None], *[grads[n].reshape(w[n].shape) for n in WEIGHTS], *[delta[n] for n in WEIGHTS],
            *[new_m[n] for n in WEIGHTS], *[new_v[n] for n in WEIGHTS])
```

```python
import functools
import math

import numpy as np
import jax
import jax.numpy as jnp
from jax import lax
from jax.experimental import pallas as pl
from jax.experimental.pallas import tpu as pltpu

F32 = jnp.float32
BF16 = jnp.bfloat16
MESH = pl.DeviceIdType.MESH

D = 1024
H = 16
DH = 64
NP = H // 2
LANES = 128
DFF = 2816
SGW = 2048
SGG = 8
SGC = 256
SGB = 128
CHUNK = 64
EPS = 1e-6
FOX_N = 4 * D + H
FOX_NP = 4224
GT = 256
NGT = DFF // GT
SCALE = DH ** -0.5
LOG2E = 1.4426950408889634

ADAM_LR = 0.001
ADAM_B1 = 0.9
ADAM_B2 = 0.999
ADAM_EPS = 1e-08
ADAM_WD = 0.01
ADAM_STEP = 10

V7X_VMEM_LIMIT = 56 * 1024 * 1024

L_F = 64
L_NF = 67
L_LSE = 70


def _cparams(sem=None):
    return pltpu.CompilerParams(dimension_semantics=sem, vmem_limit_bytes=V7X_VMEM_LIMIT)


def _split3(x):
    hi = x.astype(BF16)
    r = x - hi.astype(F32)
    mid = r.astype(BF16)
    lo = (r - mid.astype(F32)).astype(BF16)
    return hi, mid, lo


def _dot(a, b, dims=(((1,), (0,)), ((), ()))):
    return lax.dot_general(a, b, dims, preferred_element_type=F32)


def _dot_nt(a, b):
    return _dot(a, b, (((1,), (1,)), ((), ())))


def _dot_tn(a, b):
    return _dot(a, b, (((0,), (0,)), ((), ())))


def _exact_dot(m_bf16, x_f32):
    hi, mid, lo = _split3(x_f32)
    return _dot(m_bf16, hi) + _dot(m_bf16, mid) + _dot(m_bf16, lo)


def _exact_dot_r(x_f32, m_bf16):
    hi, mid, lo = _split3(x_f32)
    return _dot(hi, m_bf16) + _dot(mid, m_bf16) + _dot(lo, m_bf16)


def _head_block_ones():
    r = lax.broadcasted_iota(jnp.int32, (LANES, LANES), 0) // DH
    c = lax.broadcasted_iota(jnp.int32, (LANES, LANES), 1) // DH
    return (r == c).astype(BF16)


def _sigmoid(x):
    return 1.0 / (1.0 + jnp.exp(-x))


def _gelu(x):
    c = math.sqrt(2.0 / math.pi)
    return 0.5 * x * (1.0 + jnp.tanh(c * (x + 0.044715 * (x * x * x))))


def _gelu_and_grad(x):
    c = math.sqrt(2.0 / math.pi)
    x2 = x * x
    t = jnp.tanh(c * (x + 0.044715 * (x2 * x)))
    half = 0.5 * (1.0 + t)
    return x * half, half + 0.5 * x * (1.0 - t * t) * c * (1.0 + 3 * 0.044715 * x2)


def _rstd_rows(x):
    return lax.rsqrt(jnp.mean(x * x, axis=-1, keepdims=True) + EPS)


def _norm_mod_matmul(x, ng, sc, sh, w, bias, out_dtype, ts, tn, name, planes=1):
    s, d = x.shape
    ns = w.shape[-1]
    n = w.shape[0] * ns if w.ndim == 3 else ns
    nc = n // planes

    def body(x_ref, ng_ref, sc_ref, sh_ref, w_ref, b_ref, o_ref, h_ref):
        xv = x_ref[...]
        h = (xv * _rstd_rows(xv) * ng_ref[...] * (1.0 + sc_ref[...]) + sh_ref[...]).astype(BF16)
        h_ref[...] = h
        for e in range(planes):
            for c0 in range(0, nc, tn):
                g0 = e * nc + c0
                wv = w_ref[g0 // ns, :, g0 % ns:g0 % ns + tn] if w.ndim == 3 else w_ref[:, g0:g0 + tn]
                val = (_dot(h, wv) + b_ref[:, g0:g0 + tn]).astype(out_dtype)
                if planes == 1:
                    o_ref[:, c0:c0 + tn] = val
                else:
                    o_ref[e, :, c0:c0 + tn] = val

    vec = pl.BlockSpec((1, d), lambda i: (0, 0))
    w_spec = (pl.BlockSpec(w.shape, lambda i: (0, 0, 0)) if w.ndim == 3 else pl.BlockSpec((d, n), lambda i: (0, 0)))
    if planes == 1:
        o_spec, o_shape = pl.BlockSpec((ts, n), lambda i: (i, 0)), (s, n)
    else:
        o_spec, o_shape = pl.BlockSpec((planes, ts, nc), lambda i: (0, i, 0)), (planes, s, nc)
    return pl.pallas_call(
        body, name=name, grid=(s // ts,),
        in_specs=[pl.BlockSpec((ts, d), lambda i: (i, 0)), vec, vec, vec, w_spec,
                  pl.BlockSpec((1, n), lambda i: (0, 0))],
        out_specs=[o_spec, pl.BlockSpec((ts, d), lambda i: (i, 0))],
        out_shape=[jax.ShapeDtypeStruct(o_shape, out_dtype), jax.ShapeDtypeStruct((s, d), BF16)],
        compiler_params=_cparams(("arbitrary",)),
    )(x, ng, sc, sh, w, bias)


def _matmul(a, b, ta, tb, tm, tn, tk, out_dtype, name, out_parts=1):
    if a.ndim == 3:
        m, k = a.shape[1], a.shape[0] * a.shape[2]
        nkp = a.shape[2] // tk
    else:
        m, k = (a.shape[1], a.shape[0]) if ta else a.shape
    if b.ndim == 3:
        n = b.shape[1] if tb else b.shape[0] * b.shape[2]
        nbp = b.shape[2] // (tk if tb else tn)
    else:
        n = b.shape[0] if tb else b.shape[1]
    nk = k // tk
    nop = n // out_parts // tn
    dims = (((0,) if ta else (1,), (1,) if tb else (0,)), ((), ()))

    def body(a_ref, b_ref, o_ref, acc):
        kk = pl.program_id(2)

        @pl.when(kk == 0)
        def _():
            acc[...] = jnp.zeros_like(acc)
        acc[...] += _dot(a_ref[...], b_ref[...], dims)

        @pl.when(kk == nk - 1)
        def _():
            o_ref[...] = acc[...].astype(out_dtype)

    if a.ndim == 3:
        a_spec = pl.BlockSpec((None, tm, tk), lambda i, j, kk: (kk // nkp, i, kk % nkp))
    else:
        a_spec = (pl.BlockSpec((tk, tm), lambda i, j, kk: (kk, i)) if ta
                  else pl.BlockSpec((tm, tk), lambda i, j, kk: (i, kk)))
    if b.ndim == 3 and tb:
        b_spec = pl.BlockSpec((None, tn, tk), lambda i, j, kk: (kk // nbp, j, kk % nbp))
    elif b.ndim == 3:
        b_spec = pl.BlockSpec((None, tk, tn), lambda i, j, kk: (j // nbp, kk, j % nbp))
    else:
        b_spec = (pl.BlockSpec((tn, tk), lambda i, j, kk: (j, kk)) if tb
                  else pl.BlockSpec((tk, tn), lambda i, j, kk: (kk, j)))
    if out_parts > 1:
        o_spec = pl.BlockSpec((None, tm, tn), lambda i, j, kk: (j // nop, i, j % nop))
        o_shape = (out_parts, m, n // out_parts)
    else:
        o_spec, o_shape = pl.BlockSpec((tm, tn), lambda i, j, kk: (i, j)), (m, n)
    return pl.pallas_call(
        body, name=name, grid=(m // tm, n // tn, nk),
        in_specs=[a_spec, b_spec],
        out_specs=o_spec,
        out_shape=jax.ShapeDtypeStruct(o_shape, out_dtype),
        scratch_shapes=[pltpu.VMEM((tm, tn), F32)],
        compiler_params=_cparams(("arbitrary", "arbitrary", "arbitrary")),
    )(a, b)


def _matmul_wt(a, w, tn, tk, out_dtype, ts, name):
    s = a.shape[-2]
    ka, kw = a.shape[-1], w.shape[-1]
    k = ka * (a.shape[0] if a.ndim == 3 else 1)
    n = w.shape[-2]

    def body(a_ref, w_ref, o_ref):
        for n0 in range(0, n, tn):
            acc = None
            for g0 in range(0, k, tk):
                av = a_ref[g0 // ka, :, g0 % ka:g0 % ka + tk] if a.ndim == 3 else a_ref[:, g0:g0 + tk]
                wv = (w_ref[g0 // kw, n0:n0 + tn, g0 % kw:g0 % kw + tk] if w.ndim == 3
                      else w_ref[n0:n0 + tn, g0:g0 + tk])
                part = _dot_nt(av, wv)
                acc = part if acc is None else acc + part
            o_ref[:, n0:n0 + tn] = acc.astype(out_dtype)

    a_spec = (pl.BlockSpec((a.shape[0], ts, ka), lambda i: (0, i, 0)) if a.ndim == 3
              else pl.BlockSpec((ts, ka), lambda i: (i, 0)))
    w_spec = pl.BlockSpec(w.shape, (lambda i: (0, 0, 0)) if w.ndim == 3 else (lambda i: (0, 0)))
    return pl.pallas_call(
        body, name=name, grid=(s // ts,),
        in_specs=[a_spec, w_spec], out_specs=pl.BlockSpec((ts, n), lambda i: (i, 0)),
        out_shape=jax.ShapeDtypeStruct((s, n), out_dtype),
        compiler_params=_cparams(("arbitrary",)),
    )(a, w)


def _lane(shape):
    return lax.broadcasted_iota(jnp.int32, shape, 1)


def _pair_norm(x, gain2, bones):
    msq = _exact_dot_r(x * x, bones) * (1.0 / DH)
    r = lax.rsqrt(msq + EPS)
    xh = x * r
    return xh * gain2, xh, r


def _fox_post(proj, qg2, kg2, bf, ts, name):
    s = proj.shape[0]

    def body(p_ref, qg_ref, kg_ref, bf_ref, q_ref, k_ref, v_ref, carry):
        @pl.when(pl.program_id(0) == 0)
        def _():
            carry[...] = jnp.zeros_like(carry)
        lane = _lane((ts, LANES))
        bones = _head_block_ones()
        xf = p_ref[:, 4 * D:4 * D + LANES] + bf_ref[...]
        logf = jnp.minimum(xf, 0.0) - jnp.log(1.0 + jnp.exp(-jnp.abs(xf)))
        logf = jnp.where(lane < H, logf, 0.0)
        rr = lax.broadcasted_iota(jnp.int32, (ts, ts), 0)
        cc = lax.broadcasted_iota(jnp.int32, (ts, ts), 1)
        ltri = (cc <= rr).astype(BF16)
        fcum = _exact_dot(ltri, logf) + carry[0:1, :]
        carry[0:1, :] = fcum[ts - 1:ts, :]
        fhi, fmid, flo = _split3(fcum * LOG2E)
        fhi, fmid, flo = fhi.astype(F32), fmid.astype(F32), flo.astype(F32)
        one_q = ((lane >= L_NF) & (lane < L_NF + 3)).astype(F32)
        one_k = (((lane >= L_F) & (lane < L_F + 3)) | ((lane >= L_LSE) & (lane < L_LSE + 3))).astype(F32)
        one_v = ((lane >= L_F) & (lane < L_F + 3)).astype(F32)
        for p in range(NP):
            qn, _, _ = _pair_norm(p_ref[:, p * LANES:(p + 1) * LANES], qg_ref[...], bones)
            kn, _, _ = _pair_norm(p_ref[:, D + p * LANES:D + (p + 1) * LANES], kg_ref[...], bones)
            vv = p_ref[:, 2 * D + p * LANES:2 * D + (p + 1) * LANES]
            qn = qn * (SCALE * LOG2E)
            for e in range(2):
                h = 2 * p + e
                if e == 1:
                    qe, ke, ve = (pltpu.roll(t, DH, axis=1) for t in (qn, kn, vv))
                else:
                    qe, ke, ve = qn, kn, vv
                f0, f1, f2 = fhi[:, h:h + 1], fmid[:, h:h + 1], flo[:, h:h + 1]
                fq = jnp.where(lane == L_F, f0, jnp.where(lane == L_F + 1, f1, jnp.where(lane == L_F + 2, f2, one_q)))
                fk = jnp.where(lane == L_NF, -f0, jnp.where(lane == L_NF + 1, -f1, jnp.where(lane == L_NF + 2, -f2, one_k)))
                q_ref[h] = jnp.where(lane < DH, qe, fq).astype(BF16)
                k_ref[h] = jnp.where(lane < DH, ke, fk).astype(BF16)
                v_ref[h] = jnp.where(lane < DH, ve, one_v).astype(BF16)

    hs = pl.BlockSpec((H, ts, LANES), lambda i: (0, i, 0))
    vec = pl.BlockSpec((1, LANES), lambda i: (0, 0))
    shp = jax.ShapeDtypeStruct((H, s, LANES), BF16)
    return pl.pallas_call(
        body, name=name, grid=(s // ts,),
        in_specs=[pl.BlockSpec((ts, FOX_NP), lambda i: (i, 0)), vec, vec, vec],
        out_specs=[hs, hs, hs], out_shape=[shp, shp, shp],
        scratch_shapes=[pltpu.VMEM((8, LANES), F32)],
        compiler_params=_cparams(("arbitrary",)),
    )(proj, qg2, kg2, bf)


def _gather_copies(p_refs, o_refs, send_sems, recv_sems):
    x, y, c = _mesh_pos()
    me = 2 * x + y
    sends, arrivals = [], []
    for a, (p_ref, o_ref) in enumerate(zip(p_refs, o_refs)):
        rh = p_ref.shape[0] // 2
        for k, chip in enumerate(_other_chips(x, y)):
            ci = 2 * chip[0] + chip[1]
            for cc in range(2):
                sends.append(_remote(p_ref.at[pl.ds(c * rh, rh), :], o_ref.at[me, pl.ds(c * rh, rh), :],
                                     send_sems.at[6 * a + 2 * k + cc], recv_sems.at[6 * a + 2 * k + c], (*chip, cc)))
                arrivals.append(_remote(o_ref.at[ci, pl.ds(cc * rh, rh), :], o_ref.at[ci, pl.ds(cc * rh, rh), :],
                                        send_sems.at[6 * a + 2 * k + cc], recv_sems.at[6 * a + 2 * k + cc],
                                        (*chip, cc)))
    return sends, arrivals


def _attn_fwd(qa, ka, va, tq, name, shards=()):
    s = qa.shape[1]
    nq = s // tq
    na = len(shards)
    hps = HPS_FWD

    def body(*refs):
        q_ref, k_ref, v_ref = refs[:3]
        p_refs = refs[3:3 + na]
        o_ref, ql_ref = refs[3 + na:5 + na]
        g_refs = refs[5 + na:5 + 2 * na]
        i = pl.program_id(1)
        if na:
            send_sems, recv_sems = refs[5 + 2 * na:]

            @pl.when((pl.program_id(0) == 0) & (i == 0))
            def _():
                for cp in _gather_copies(p_refs, g_refs, send_sems, recv_sems)[0]:
                    cp.start()
        lane = _lane((tq, LANES))
        qs_ = [q_ref[e] for e in range(hps)]

        tk = min(TK_FWD, tq)
        nks = tq // tk

        def step(j, carry, diag=None):
            off = pl.multiple_of(j * tk, tk)
            scs = [_dot_nt(qs_[e], k_ref[e, pl.ds(off, tk), :]) for e in range(hps)]
            probs = []
            for e in range(hps):
                m, sc = carry[e][0], scs[e]
                if diag is not None:
                    rr = lax.broadcasted_iota(jnp.int32, (tq, tk), 0)
                    cc = lax.broadcasted_iota(jnp.int32, (tq, tk), 1) + diag * tk
                    sc = jnp.where(cc <= rr, sc, -jnp.inf)
                m_new = jnp.maximum(m, jnp.max(sc, axis=-1, keepdims=True))
                probs.append((m_new, jnp.exp2(sc - m_new).astype(BF16), jnp.exp2(m - m_new)))
            return tuple((m_new, carry[e][1] * alpha + _dot(pr, v_ref[e, pl.ds(off, tk), :]))
                         for e, (m_new, pr, alpha) in enumerate(probs))

        one = (jnp.full((tq, 1), -jnp.inf, F32), jnp.zeros((tq, LANES), F32))
        carry = lax.fori_loop(0, i * nks, step, (one,) * hps)
        for r in range(nks):
            carry = step(i * nks + r, carry, diag=r)
        outs = []
        for e in range(hps):
            m, acc = carry[e]
            l = acc[:, L_F:L_F + 1]
            outs.append(acc / l)
            lse = m + jnp.log2(l)
            h0, h1, h2 = _split3(-lse)
            ql = jnp.where(lane == L_LSE, h0.astype(F32),
                           jnp.where(lane == L_LSE + 1, h1.astype(F32),
                                     jnp.where(lane == L_LSE + 2, h2.astype(F32), qs_[e].astype(F32))))
            ql_ref[e] = ql.astype(BF16)
        for e in range(0, hps, 2):
            o_ref[:, e * DH:(e + 2) * DH] = jnp.where(lane < DH, outs[e], pltpu.roll(outs[e + 1], DH, axis=1))
        if na:
            @pl.when((pl.program_id(0) == H // hps - 1) & (i == nq - 1))
            def _():
                sends, arrivals = _gather_copies(p_refs, g_refs, send_sems, recv_sems)
                for cp in arrivals:
                    cp.wait_recv()
                for cp in sends:
                    cp.wait_send()

    res = pl.BlockSpec((hps, s, LANES), lambda p, i: (p, 0, 0))
    qs = pl.BlockSpec((hps, tq, LANES), lambda p, i: (p, i, 0))
    outs = pl.pallas_call(
        body, name=name, grid=(H // hps, nq),
        in_specs=[qs, res, res] + [HBM_SPEC] * na,
        out_specs=[pl.BlockSpec((tq, hps * DH), lambda p, i: (i, p)), qs] + [HBM_SPEC] * na,
        out_shape=[jax.ShapeDtypeStruct((s, D), F32), jax.ShapeDtypeStruct((H, s, LANES), BF16)]
        + [jax.ShapeDtypeStruct((N_CHIP,) + p.shape, p.dtype) for p in shards],
        scratch_shapes=[pltpu.SemaphoreType.DMA((6 * na,))] * 2 if na else [],
        compiler_params=_cparams(("arbitrary", "arbitrary")),
    )(qa, ka, va, *shards)
    return outs[0], outs[1], list(outs[2:])


def _chip_exchange_copies(cs_refs, o_refs, send_sems, recv_sems):
    x, y, c = _mesh_pos()
    cps = []
    for a, (cs_ref, o_ref) in enumerate(zip(cs_refs, o_refs)):
        for k, chip in enumerate(_other_chips(x, y)):
            ci = 2 * chip[0] + chip[1]
            cps.append(_remote(cs_ref.at[ci], o_ref.at[k], send_sems.at[3 * a + k], recv_sems.at[3 * a + k],
                               (*chip, c)))
    return cps


def _attn_bwd(ql, ka, va, doa, tq, name, css=()):
    s = ql.shape[1]
    nq = s // tq
    na = len(css)

    def body(*refs):
        q_ref, k_ref, v_ref, do_ref = refs[:4]
        cs_refs = refs[4:4 + na]
        dqo_ref, dk_ref, dv_ref = refs[4 + na:7 + na]
        r_refs = refs[7 + na:7 + 2 * na]
        dq_ref = refs[7 + 2 * na]
        j = pl.program_id(1)
        if na:
            send_sems, recv_sems = refs[8 + 2 * na:]

            @pl.when((pl.program_id(0) == 0) & (j == 0))
            def _():
                for cp in _chip_exchange_copies(cs_refs, r_refs, send_sems, recv_sems):
                    cp.start()

        @pl.when(j == 0)
        def _():
            dq_ref[...] = jnp.zeros_like(dq_ref)
        lane = _lane((tq, LANES))
        kbs = [k_ref[0], k_ref[1]]
        vbs = [v_ref[0], v_ref[1]]

        def step(i, carry, masked):
            ioff = pl.multiple_of(i * tq, tq)
            qbs = [q_ref[e, pl.ds(ioff, tq), :] for e in range(2)]
            dobs = [do_ref[e, pl.ds(ioff, tq), :] for e in range(2)]
            scs = [_dot_nt(qbs[e], kbs[e]) for e in range(2)]
            dps = [_dot_nt(dobs[e], vbs[e]) for e in range(2)]
            prs, dss = [], []
            for e in range(2):
                pr = jnp.exp2(scs[e])
                if masked:
                    rr = lax.broadcasted_iota(jnp.int32, (tq, tq), 0)
                    cc = lax.broadcasted_iota(jnp.int32, (tq, tq), 1)
                    pr = jnp.where(cc <= rr, pr, 0.0)
                dss.append((pr * dps[e]).astype(BF16))
                prs.append(pr.astype(BF16))
            new = []
            for e in range(2):
                dk, dv = carry[e]
                dv = dv + _dot_tn(prs[e], dobs[e])
                dk = dk + _dot_tn(dss[e], qbs[e])
                dq_ref[e, pl.ds(ioff, tq), :] += _dot(dss[e], kbs[e])
                new.append((dk, dv))
            return tuple(new)

        zero = jnp.zeros((tq, LANES), F32)
        carry = step(j, ((zero, zero), (zero, zero)), True)
        carry = lax.fori_loop(j + 1, nq, functools.partial(step, masked=False), carry)
        for e in range(2):
            dk, dv = carry[e]
            col = dk[:, L_NF:L_NF + 1]
            hi = col.astype(BF16).astype(F32)
            dk_ref[e] = jnp.where(lane == L_NF, hi, jnp.where(lane == L_NF + 1, col - hi, dk)).astype(BF16)
            dv_ref[e] = dv.astype(BF16)

        @pl.when(j == nq - 1)
        def _():
            lane_s = _lane((s, LANES))
            for e in range(2):
                dq = dq_ref[e]
                col = dq[:, L_F:L_F + 1]
                hi = col.astype(BF16).astype(F32)
                dqo_ref[e] = jnp.where(lane_s == L_F, hi, jnp.where(lane_s == L_F + 1, col - hi, dq)).astype(BF16)
        if na:
            @pl.when((pl.program_id(0) == NP - 1) & (j == nq - 1))
            def _():
                for cp in _chip_exchange_copies(cs_refs, r_refs, send_sems, recv_sems):
                    cp.wait()

    res = pl.BlockSpec((2, s, LANES), lambda p, j: (p, 0, 0))
    tile = pl.BlockSpec((2, tq, LANES), lambda p, j: (p, j, 0))
    shp = jax.ShapeDtypeStruct((H, s, LANES), BF16)
    outs = pl.pallas_call(
        body, name=name, grid=(NP, nq),
        in_specs=[res, tile, tile, res] + [HBM_SPEC] * na, out_specs=[res, tile, tile] + [HBM_SPEC] * na,
        out_shape=[shp, shp, shp] + [jax.ShapeDtypeStruct((3,) + cs.shape[1:], cs.dtype) for cs in css],
        scratch_shapes=[pltpu.VMEM((2, s, LANES), F32)] + ([pltpu.SemaphoreType.DMA((3 * na,))] * 2 if na else []),
        compiler_params=_cparams(("arbitrary", "arbitrary")),
    )(ql, ka, va, doa, *css)
    return outs[0], outs[1], outs[2], list(outs[3:])


def _gate_out(att, proj, w, xin, g, ts, name):
    s = att.shape[0]

    def body(a_ref, o_ref, w_ref, x_ref, g_ref, xo_ref, y_ref, gt_ref):
        gated = (a_ref[...] * _sigmoid(o_ref[...])).astype(BF16)
        gt_ref[...] = gated
        y = _dot(gated, w_ref[...])
        xo_ref[...] = x_ref[...] + g_ref[...] * y
        y_ref[...] = y.astype(BF16)

    row = pl.BlockSpec((ts, D), lambda i: (i, 0))
    return pl.pallas_call(
        body, name=name, grid=(s // ts,),
        in_specs=[row, pl.BlockSpec((ts, D), lambda i: (i, 3)), pl.BlockSpec((D, D), lambda i: (0, 0)), row,
                  pl.BlockSpec((1, D), lambda i: (0, 0))],
        out_specs=[row, row, row],
        out_shape=[jax.ShapeDtypeStruct((s, D), F32), jax.ShapeDtypeStruct((s, D), BF16),
                   jax.ShapeDtypeStruct((s, D), BF16)],
        compiler_params=_cparams(("arbitrary",)),
    )(att, proj, w, xin, g)


def _sibling_copies(g_refs, o_refs, ssems, rsems):
    x, y, c = _mesh_pos()
    cps = []
    for a, (g_ref, o_ref) in enumerate(zip(g_refs, o_refs)):
        rh = g_ref.shape[1] // 2
        cps.append(_remote(g_ref.at[:, pl.ds((1 - c) * rh, rh), :], o_ref, ssems.at[a], rsems.at[a], (x, y, 1 - c)))
    return cps


def _attn_bwd_prep(dy, w_out, att, proj, ts, name, glist=()):
    s = att.shape[0]
    na = len(glist)

    def body(*refs):
        dy_ref, w_ref, a_ref, o_ref = refs[:4]
        g_refs = refs[4:4 + na]
        doa_ref, dop_ref = refs[4 + na:6 + na]
        sib_refs, sems = refs[6 + na:6 + 2 * na], refs[6 + 2 * na:]
        if na:
            @pl.when(pl.program_id(0) == 0)
            def _():
                for cp in _sibling_copies(g_refs, sib_refs, *sems):
                    cp.start()
        lane = _lane((ts, LANES))
        bones = _head_block_ones()
        dgv = _dot_nt(dy_ref[...], w_ref[...])
        for p in range(NP):
            sl = slice(p * LANES, (p + 1) * LANES)
            dg, a = dgv[:, sl], a_ref[:, sl]
            sig = _sigmoid(o_ref[:, sl])
            datt = dg * sig
            dop_ref[:, sl] = (dg * a * sig * (1.0 - sig)).astype(BF16)
            delta = _exact_dot_r(datt * a, bones)
            for e in range(2):
                de, dl = (datt, delta) if e == 0 else (pltpu.roll(datt, DH, axis=1), pltpu.roll(delta, DH, axis=1))
                h0, h1, h2 = _split3(-dl[:, 0:1])
                aug = jnp.where(lane == L_F, h0.astype(F32),
                                jnp.where(lane == L_F + 1, h1.astype(F32),
                                          jnp.where(lane == L_F + 2, h2.astype(F32), 0.0)))
                doa_ref[2 * p + e] = jnp.where(lane < DH, de, aug).astype(BF16)
        if na:
            @pl.when(pl.program_id(0) == s // ts - 1)
            def _():
                for cp in _sibling_copies(g_refs, sib_refs, *sems):
                    cp.wait()

    row = pl.BlockSpec((ts, D), lambda i: (i, 0))
    outs = pl.pallas_call(
        body, name=name, grid=(s // ts,),
        in_specs=[row, pl.BlockSpec((D, D), lambda i: (0, 0)), row, pl.BlockSpec((ts, D), lambda i: (i, 3))]
        + [HBM_SPEC] * na,
        out_specs=[pl.BlockSpec((H, ts, LANES), lambda i: (0, i, 0)), row] + [HBM_SPEC] * na,
        out_shape=[jax.ShapeDtypeStruct((H, s, LANES), BF16), jax.ShapeDtypeStruct((s, D), BF16)]
        + [jax.ShapeDtypeStruct((g_.shape[0], g_.shape[1] // 2, g_.shape[2]), g_.dtype) for g_ in glist],
        scratch_shapes=[pltpu.SemaphoreType.DMA((na,))] * 2 if na else [],
        compiler_params=_cparams(("arbitrary",)),
    )(dy, w_out, att, proj, *glist)
    return outs[0], outs[1], list(outs[2:])


def _fox_post_bwd(proj, dqa, dka, dva, dop, qg2, kg2, bf, ts, name):
    s = proj.shape[0]
    nt = s // ts

    def body(p_ref, dq_ref, dk_ref, dv_ref, dop_ref, qg_ref, kg_ref, bf_ref, o_ref, red_ref, carry):
        @pl.when(pl.program_id(0) == 0)
        def _():
            carry[...] = jnp.zeros_like(carry)
            red_ref[...] = jnp.zeros_like(red_ref)
        lane = _lane((ts, LANES))
        bones = _head_block_ones()
        d_f = jnp.zeros((ts, LANES), F32)
        dqg = jnp.zeros((1, LANES), F32)
        dkg = jnp.zeros((1, LANES), F32)
        for p in range(NP):
            heads = [[ref[2 * p + e].astype(F32) for e in range(2)] for ref in (dq_ref, dk_ref, dv_ref)]
            pair = [jnp.where(lane < DH, a, pltpu.roll(b, DH, axis=1)) for a, b in heads]
            for e in range(2):
                dqe, dke = heads[0][e], heads[1][e]
                col = (dqe[:, L_F:L_F + 1] + dqe[:, L_F + 1:L_F + 2]
                       - dke[:, L_NF:L_NF + 1] - dke[:, L_NF + 1:L_NF + 2])
                d_f = jnp.where(lane == 2 * p + e, col, d_f)
            for idx, (g_ref, base) in enumerate(((qg_ref, 0), (kg_ref, D))):
                x = p_ref[:, base + p * LANES:base + (p + 1) * LANES]
                _, xh, r = _pair_norm(x, g_ref[...], bones)
                dn = pair[idx] * (SCALE if idx == 0 else 1.0 / LOG2E)
                t = dn * g_ref[...]
                mean_txh = _exact_dot_r(t * xh, bones) * (1.0 / DH)
                dx = r * (t - xh * mean_txh)
                o_ref[:, base + p * LANES:base + (p + 1) * LANES] = dx.astype(BF16)
                gsum = jnp.sum(dn * xh, axis=0, keepdims=True)
                if idx == 0:
                    dqg = dqg + gsum
                else:
                    dkg = dkg + gsum
            o_ref[:, 2 * D + p * LANES:2 * D + (p + 1) * LANES] = pair[2].astype(BF16)
        o_ref[:, 3 * D:4 * D] = dop_ref[...]
        rr = lax.broadcasted_iota(jnp.int32, (ts, ts), 0)
        cc = lax.broadcasted_iota(jnp.int32, (ts, ts), 1)
        utri = (cc >= rr).astype(BF16)
        dlogf = _exact_dot(utri, d_f) + carry[0:1, :]
        carry[0:1, :] = dlogf[0:1, :]
        xf = p_ref[:, 4 * D:4 * D + LANES] + bf_ref[...]
        dfl = jnp.where(lane < H, dlogf * _sigmoid(-xf), 0.0)
        o_ref[:, 4 * D:4 * D + LANES] = dfl.astype(BF16)
        red_ref[0:1, :] += dqg
        red_ref[1:2, :] += dkg
        red_ref[2:3, :] += jnp.sum(dfl, axis=0, keepdims=True)

    hs = pl.BlockSpec((H, ts, LANES), lambda i: (0, nt - 1 - i, 0))
    vec = pl.BlockSpec((1, LANES), lambda i: (0, 0))
    return pl.pallas_call(
        body, name=name, grid=(nt,),
        in_specs=[pl.BlockSpec((ts, FOX_NP), lambda i: (nt - 1 - i, 0)), hs, hs, hs,
                  pl.BlockSpec((ts, D), lambda i: (nt - 1 - i, 0)), vec, vec, vec],
        out_specs=[pl.BlockSpec((ts, FOX_NP), lambda i: (nt - 1 - i, 0)),
                   pl.BlockSpec((8, LANES), lambda i: (0, 0))],
        out_shape=[jax.ShapeDtypeStruct((s, FOX_NP), BF16), jax.ShapeDtypeStruct((8, LANES), F32)],
        scratch_shapes=[pltpu.VMEM((8, LANES), F32)],
        compiler_params=_cparams(("arbitrary",)),
    )(proj, dqa, dka, dva, dop, qg2, kg2, bf)


HALO = 16
TS = 512
TQ = 512
TR = 256
TP = 256
HPS_FWD = 4
TK_FWD = 512
TKW = 2048


def _shift_down(x, k):
    return pltpu.roll(x, k, axis=0)


def _shift_up(x, k):
    return pltpu.roll(x, x.shape[0] - k, axis=0)


def _conv_down(a, cw, cb, w, xin, gate, ts, name):
    s = a.shape[1]
    d = w.shape[1]
    hb = ts // HALO

    def body(prev_ref, a_ref, cw_ref, cb_ref, w_ref, x_ref, g_ref, o_ref, y_ref, f_ref, ap_ref):
        i = pl.program_id(0)
        acc = None
        for c in range(NGT):
            cols = slice(c * GT, (c + 1) * GT)
            both = lambda ref: jnp.concatenate([ref[0, :, cols].astype(F32), ref[1, :, cols].astype(F32)], axis=1)
            cwv, cbv = both(cw_ref), both(cb_ref)
            ext = jnp.concatenate([jnp.where(i > 0, both(prev_ref), 0.0), both(a_ref)], axis=0)
            ap = (_shift_down(ext, 2) * cwv[0:1, :] + _shift_down(ext, 1) * cwv[1:2, :]
                  + ext * cwv[2:3, :] + cbv)[HALO:, :]
            g, val = ap[:, :GT], ap[:, GT:]
            fch = (g * _sigmoid(g) * val).astype(BF16)
            f_ref[:, cols] = fch
            ap_ref[0, :, cols] = g.astype(BF16)
            ap_ref[1, :, cols] = val.astype(BF16)
            part = _dot(fch, w_ref[cols, :])
            acc = part if acc is None else acc + part
        y_ref[...] = acc.astype(BF16)
        o_ref[...] = x_ref[...] + g_ref[...] * acc

    row = pl.BlockSpec((ts, d), lambda i: (i, 0))
    planes = pl.BlockSpec((2, ts, DFF), lambda i: (0, i, 0))
    return pl.pallas_call(
        body, name=name, grid=(s // ts,),
        in_specs=[pl.BlockSpec((2, HALO, DFF), lambda i: (0, jnp.maximum(i * hb - 1, 0), 0)), planes,
                  pl.BlockSpec((2, 8, DFF), lambda i: (0, 0, 0)), pl.BlockSpec((2, 1, DFF), lambda i: (0, 0, 0)),
                  pl.BlockSpec((DFF, d), lambda i: (0, 0)), row, pl.BlockSpec((1, d), lambda i: (0, 0))],
        out_specs=[row, row, pl.BlockSpec((ts, DFF), lambda i: (i, 0)), planes],
        out_shape=[jax.ShapeDtypeStruct((s, d), F32), jax.ShapeDtypeStruct((s, d), BF16),
                   jax.ShapeDtypeStruct((s, DFF), BF16), jax.ShapeDtypeStruct((2, s, DFF), BF16)],
        compiler_params=_cparams(("arbitrary",)),
    )(a, a, cw, cb, w, xin, gate)


def _down_bwd_conv(dy, w, a, ap, cw, ts, name):
    s, d = dy.shape
    hb = ts // HALO
    nt = s // ts
    nhb = s // HALO

    def body(dy_ref, dyn_ref, w_ref, a_ref, ap_ref, apn_ref, cw_ref, da_ref, red_ref):
        i = pl.program_id(0)

        @pl.when(i == 0)
        def _():
            red_ref[...] = jnp.zeros_like(red_ref)
        dyn = jnp.where(i < nt - 1, dyn_ref[...], jnp.zeros_like(dyn_ref))
        dye = jnp.concatenate([dy_ref[...], dyn], axis=0)
        for c in range(NGT):
            cols = slice(c * GT, (c + 1) * GT)
            both = lambda ref: jnp.concatenate([ref[0, :, cols].astype(F32), ref[1, :, cols].astype(F32)], axis=1)
            cwv = both(cw_ref)
            dfe = _dot_nt(dye, w_ref[cols, :])
            apv = jnp.concatenate([both(ap_ref), both(apn_ref)], axis=0)
            g, val = apv[:, :GT], apv[:, GT:]
            sg = _sigmoid(g)
            dap = jnp.concatenate([dfe * val * (sg * (1.0 + g * (1.0 - sg))), dfe * (g * sg)], axis=1)
            shifted = [_shift_up(dap, 2)[:ts], _shift_up(dap, 1)[:ts], dap[:ts]]
            da = shifted[0] * cwv[0:1, :] + shifted[1] * cwv[1:2, :] + shifted[2] * cwv[2:3, :]
            av = both(a_ref)
            sums = [jnp.sum(av * t, axis=0, keepdims=True) for t in shifted]
            sums.append(jnp.sum(shifted[2], axis=0, keepdims=True))
            for e in range(2):
                half = slice(e * GT, (e + 1) * GT)
                da_ref[e, :, cols] = da[:, half].astype(BF16)
                for r, sm in enumerate(sums):
                    red_ref[e, r:r + 1, cols] += sm[:, half]

    planes = pl.BlockSpec((2, ts, DFF), lambda i: (0, i, 0))
    nxt = lambda i: jnp.minimum((i + 1) * hb, nhb - 1)
    return pl.pallas_call(
        body, name=name, grid=(nt,),
        in_specs=[pl.BlockSpec((ts, d), lambda i: (i, 0)), pl.BlockSpec((HALO, d), lambda i: (nxt(i), 0)),
                  pl.BlockSpec((DFF, d), lambda i: (0, 0)), planes, planes,
                  pl.BlockSpec((2, HALO, DFF), lambda i: (0, nxt(i), 0)),
                  pl.BlockSpec((2, 8, DFF), lambda i: (0, 0, 0))],
        out_specs=[planes, pl.BlockSpec((2, 8, DFF), lambda i: (0, 0, 0))],
        out_shape=[jax.ShapeDtypeStruct((2, s, DFF), BF16), jax.ShapeDtypeStruct((2, 8, DFF), F32)],
        compiler_params=_cparams(("arbitrary",)),
    )(dy, dy, w, a, ap, ap, cw)


def _chunk_mask(transposed=False):
    t = lax.broadcasted_iota(jnp.int32, (SGB, SGB), 0) // CHUNK
    u = lax.broadcasted_iota(jnp.int32, (SGB, SGB), 1) // CHUNK
    return (t <= u) if transposed else (u <= t)


def _sgu_ln(v, gain, bias):
    mu = jnp.mean(v, axis=-1, keepdims=True)
    vc = v - mu
    rstd = lax.rsqrt(jnp.mean(vc * vc, axis=-1, keepdims=True) + EPS)
    vhat = vc * rstd
    return vhat * gain + bias, vhat, rstd


def _sgu_fwd(z, vgain, vbias, ws, bst, w_out, xin, gate, tr, name):
    s = z.shape[0]

    def body(zu_ref, zv_ref, vg_ref, vb_ref, ws_ref, bs_ref, wo_ref, x_ref, gt_ref, xo_ref, yo_ref, y_ref):
        u = _gelu(zu_ref[...].astype(F32))
        vn, _, _ = _sgu_ln(_gelu(zv_ref[...].astype(F32)), vg_ref[...], vb_ref[...])
        vn = vn.astype(BF16)
        mask = _chunk_mask()
        for g in range(SGG):
            w = jnp.where(mask, ws_ref[g], 0.0).astype(BF16)
            for b in range(tr // SGB):
                rs, cs = slice(b * SGB, (b + 1) * SGB), slice(g * SGC, (g + 1) * SGC)
                mixed = _dot(w, vn[rs, cs]) + bs_ref[:, g:g + 1]
                y_ref[rs, cs] = (u[rs, cs] * mixed).astype(BF16)
        yo = _dot(y_ref[...], wo_ref[...])
        xo_ref[...] = x_ref[...] + gt_ref[...] * yo
        yo_ref[...] = yo.astype(BF16)

    vec = pl.BlockSpec((1, SGW), lambda i: (0, 0))
    row = pl.BlockSpec((tr, D), lambda i: (i, 0))
    return pl.pallas_call(
        body, name=name, grid=(s // tr,),
        in_specs=[pl.BlockSpec((tr, SGW), lambda i: (i, 0)), pl.BlockSpec((tr, SGW), lambda i: (i, 1)),
                  vec, vec, pl.BlockSpec((SGG, SGB, SGB), lambda i: (0, 0, 0)),
                  pl.BlockSpec((SGB, LANES), lambda i: (0, 0)), pl.BlockSpec((SGW, D), lambda i: (0, 0)), row,
                  pl.BlockSpec((1, D), lambda i: (0, 0))],
        out_specs=[row, row, pl.BlockSpec((tr, SGW), lambda i: (i, 0))],
        out_shape=[jax.ShapeDtypeStruct((s, D), F32), jax.ShapeDtypeStruct((s, D), BF16),
                   jax.ShapeDtypeStruct((s, SGW), BF16)],
        compiler_params=_cparams(("arbitrary",)),
    )(z, z, vgain, vbias, ws, bst, w_out, xin, gate)


def _sgu_bwd(z, dy, vgain, vbias, ws, wst, bst, tr, name):
    s = z.shape[0]

    def body(zu_ref, zv_ref, dy_ref, vg_ref, vb_ref, ws_ref, wst_ref, bs_ref,
             dz_ref, rb_ref, rv_ref, dws_ref, dbs_ref, dvn_s):
        @pl.when(pl.program_id(0) == 0)
        def _():
            rb_ref[...] = jnp.zeros_like(rb_ref)
            rv_ref[...] = jnp.zeros_like(rv_ref)
            dws_ref[...] = jnp.zeros_like(dws_ref)
            dbs_ref[...] = jnp.zeros_like(dbs_ref)
        zu = zu_ref[...].astype(F32)
        zv = zv_ref[...].astype(F32)
        u, gu = _gelu_and_grad(zu)
        v, gv = _gelu_and_grad(zv)
        vn, vhat, rstd = _sgu_ln(v, vg_ref[...], vb_ref[...])
        vnb = vn.astype(BF16)
        dyv = dy_ref[...].astype(F32)
        dmix = (dyv * u).astype(BF16)
        mask = _chunk_mask()
        mask_t = _chunk_mask(transposed=True)
        lane = _lane((SGB, LANES))
        dbs = jnp.zeros((SGB, LANES), F32)
        for g in range(SGG):
            w = jnp.where(mask, ws_ref[g], 0.0).astype(BF16)
            wt = jnp.where(mask_t, wst_ref[g], 0.0).astype(BF16)
            dw = jnp.zeros((SGB, SGB), F32)
            for b in range(tr // SGB):
                rs, cs = slice(b * SGB, (b + 1) * SGB), slice(g * SGC, (g + 1) * SGC)
                mixed = _dot(w, vnb[rs, cs]) + bs_ref[:, g:g + 1]
                dz_ref[rs, cs] = (dyv[rs, cs] * mixed * gu[rs, cs]).astype(BF16)
                dm = dmix[rs, cs]
                dw = dw + _dot_nt(dm, vnb[rs, cs])
                dbs = dbs + jnp.where(lane == g, jnp.sum(dm.astype(F32), axis=-1, keepdims=True), 0.0)
                dvn_s[rs, cs] = _dot(wt, dm)
            dws_ref[g] += jnp.where(mask, dw, 0.0)
        dbs_ref[...] += dbs
        dvn = dvn_s[...]
        rv_ref[0:1, :] += jnp.sum(dvn * vhat, axis=0, keepdims=True)
        rv_ref[1:2, :] += jnp.sum(dvn, axis=0, keepdims=True)
        dvh = dvn * vg_ref[...]
        dv = rstd * (dvh - jnp.mean(dvh, axis=-1, keepdims=True)
                     - vhat * jnp.mean(dvh * vhat, axis=-1, keepdims=True))
        dz_ref[:, SGW:] = (dv * gv).astype(BF16)
        dzf = dz_ref[...].astype(F32)
        rb_ref[0:1, :] += jnp.sum(dzf, axis=0, keepdims=True)

    vec = pl.BlockSpec((1, SGW), lambda i: (0, 0))
    wsp = pl.BlockSpec((SGG, SGB, SGB), lambda i: (0, 0, 0))
    return pl.pallas_call(
        body, name=name, grid=(s // tr,),
        in_specs=[pl.BlockSpec((tr, SGW), lambda i: (i, 0)), pl.BlockSpec((tr, SGW), lambda i: (i, 1)),
                  pl.BlockSpec((tr, SGW), lambda i: (i, 0)), vec, vec, wsp, wsp,
                  pl.BlockSpec((SGB, LANES), lambda i: (0, 0))],
        out_specs=[pl.BlockSpec((tr, 2 * SGW), lambda i: (i, 0)),
                   pl.BlockSpec((8, 2 * SGW), lambda i: (0, 0)),
                   pl.BlockSpec((8, SGW), lambda i: (0, 0)), wsp,
                   pl.BlockSpec((SGB, LANES), lambda i: (0, 0))],
        out_shape=[jax.ShapeDtypeStruct((s, 2 * SGW), BF16), jax.ShapeDtypeStruct((8, 2 * SGW), F32),
                   jax.ShapeDtypeStruct((8, SGW), F32), jax.ShapeDtypeStruct((SGG, SGB, SGB), F32),
                   jax.ShapeDtypeStruct((SGB, LANES), F32)],
        scratch_shapes=[pltpu.VMEM((tr, SGW), F32)],
        compiler_params=_cparams(("arbitrary",)),
    )(z, z, dy, vgain, vbias, ws, wst, bst)


def _final_loss(x, fg, tgt, gprev, yprev, ts, name):
    s, d = x.shape

    def body(x_ref, fg_ref, t_ref, g_ref, y_ref, l_ref, dx_ref, dy_ref, red_ref):
        @pl.when(pl.program_id(0) == 0)
        def _():
            l_ref[...] = jnp.zeros_like(l_ref)
            red_ref[...] = jnp.zeros_like(red_ref)
        xv = x_ref[...]
        r = _rstd_rows(xv)
        xh = xv * r
        err = xh * fg_ref[...] - t_ref[...]
        l_ref[...] += 0.5 * jnp.sum(jnp.mean(err * err, axis=-1, keepdims=True))
        dyo = err * (1.0 / d)
        dxh = dyo * fg_ref[...]
        dx = r * (dxh - xh * jnp.mean(dxh * xh, axis=-1, keepdims=True))
        dx_ref[...] = dx
        dy_ref[...] = (dx * g_ref[...]).astype(BF16)
        red_ref[0:1, :] += jnp.sum(dyo * xh, axis=0, keepdims=True)
        red_ref[1:2, :] += jnp.sum(dx * y_ref[...].astype(F32), axis=0, keepdims=True)

    row = pl.BlockSpec((ts, d), lambda i: (i, 0))
    vec = pl.BlockSpec((1, d), lambda i: (0, 0))
    return pl.pallas_call(
        body, name=name, grid=(s // ts,),
        in_specs=[row, vec, row, vec, row],
        out_specs=[pl.BlockSpec((8, LANES), lambda i: (0, 0)), row, row, pl.BlockSpec((8, d), lambda i: (0, 0))],
        out_shape=[jax.ShapeDtypeStruct((8, LANES), F32), jax.ShapeDtypeStruct((s, d), F32),
                   jax.ShapeDtypeStruct((s, d), BF16), jax.ShapeDtypeStruct((8, d), F32)],
        compiler_params=_cparams(("arbitrary",)),
    )(x, fg, tgt, gprev, yprev)


def _norm_bwd(xin, dh, dxout, ng, sc, gprev, yprev, ts, name, css=()):
    s, d = xin.shape
    has_prev = gprev is not None
    fused = isinstance(dh, tuple)
    na = len(css)
    if fused:
        a, w, tk = dh
        ka, kw = a.shape[-1], w.shape[-1]
        k = ka * (a.shape[0] if a.ndim == 3 else 1)

    def body(*refs):
        if fused:
            x_ref, a_ref, w_ref, dxo_ref, ng_ref, sc_ref = refs[:6]
            rest = refs[6:]
        else:
            x_ref, dh_ref, dxo_ref, ng_ref, sc_ref = refs[:5]
            rest = refs[5:]
        if has_prev:
            g_ref, y_ref = rest[:2]
            rest = rest[2:]
        cs_refs, rest = rest[:na], rest[na:]
        if has_prev:
            dx_ref, dy_ref, red_ref = rest[:3]
            rest = rest[3:]
        else:
            dx_ref, red_ref = rest[:2]
            rest = rest[2:]
        r_refs, sems = rest[:na], rest[na:]
        if na:
            @pl.when(pl.program_id(0) == 0)
            def _():
                for cp in _chip_exchange_copies(cs_refs, r_refs, *sems):
                    cp.start()

        @pl.when(pl.program_id(0) == 0)
        def _():
            red_ref[...] = jnp.zeros_like(red_ref)
        if fused:
            dhv = None
            for g0 in range(0, k, tk):
                av = a_ref[g0 // ka, :, g0 % ka:g0 % ka + tk] if a.ndim == 3 else a_ref[:, g0:g0 + tk]
                wv = w_ref[g0 // kw, :, g0 % kw:g0 % kw + tk] if w.ndim == 3 else w_ref[:, g0:g0 + tk]
                part = _dot_nt(av, wv)
                dhv = part if dhv is None else dhv + part
        else:
            dhv = dh_ref[...]
        xv = x_ref[...]
        r = _rstd_rows(xv)
        xh = xv * r
        dr = dhv * (1.0 + sc_ref[...])
        t = dr * ng_ref[...]
        dx = dxo_ref[...] + r * (t - xh * jnp.mean(t * xh, axis=-1, keepdims=True))
        dx_ref[...] = dx
        red_ref[0:1, :] += jnp.sum(dhv, axis=0, keepdims=True)
        red_ref[1:2, :] += jnp.sum(dhv * (xh * ng_ref[...]), axis=0, keepdims=True)
        red_ref[2:3, :] += jnp.sum(dr * xh, axis=0, keepdims=True)
        if has_prev:
            dy_ref[...] = (dx * g_ref[...]).astype(BF16)
            red_ref[3:4, :] += jnp.sum(dx * y_ref[...].astype(F32), axis=0, keepdims=True)
        if na:
            @pl.when(pl.program_id(0) == s // ts - 1)
            def _():
                for cp in _chip_exchange_copies(cs_refs, r_refs, *sems):
                    cp.wait()

    row = pl.BlockSpec((ts, d), lambda i: (i, 0))
    vec = pl.BlockSpec((1, d), lambda i: (0, 0))
    red = pl.BlockSpec((8, d), lambda i: (0, 0))
    if fused:
        a_spec = (pl.BlockSpec((a.shape[0], ts, ka), lambda i: (0, i, 0)) if a.ndim == 3
                  else pl.BlockSpec((ts, ka), lambda i: (i, 0)))
        w_spec = pl.BlockSpec(w.shape, (lambda i: (0, 0, 0)) if w.ndim == 3 else (lambda i: (0, 0)))
        dh_specs, dh_args = [a_spec, w_spec], (a, w)
    else:
        dh_specs, dh_args = [row], (dh,)
    if has_prev:
        in_specs, args = [row] + dh_specs + [row, vec, vec, vec, row], (xin,) + dh_args + (dxout, ng, sc, gprev, yprev)
        out_specs = [row, row, red]
        out_shape = [jax.ShapeDtypeStruct((s, d), F32), jax.ShapeDtypeStruct((s, d), BF16),
                     jax.ShapeDtypeStruct((8, d), F32)]
    else:
        in_specs, args = [row] + dh_specs + [row, vec, vec], (xin,) + dh_args + (dxout, ng, sc)
        out_specs = [row, red]
        out_shape = [jax.ShapeDtypeStruct((s, d), F32), jax.ShapeDtypeStruct((8, d), F32)]
    return pl.pallas_call(
        body, name=name, grid=(s // ts,), in_specs=in_specs + [HBM_SPEC] * na,
        out_specs=out_specs + [HBM_SPEC] * na,
        out_shape=out_shape + [jax.ShapeDtypeStruct((3,) + cs.shape[1:], cs.dtype) for cs in css],
        scratch_shapes=[pltpu.SemaphoreType.DMA((3 * na,))] * 2 if na else [],
        compiler_params=_cparams(("arbitrary",)),
    )(*args, *css)


def _ada_mod(c_all, ada_w, ada_b):
    nb = c_all.shape[0]
    da = ada_w.shape[2]

    def body(c_ref, w_ref, b_ref, o_ref, ca_ref):
        cv = c_ref[...]
        ca = cv * _sigmoid(cv)
        ca_ref[...] = ca
        o_ref[0] = lax.dot_general(ca, w_ref[0], (((1,), (0,)), ((), ())), precision=lax.Precision.HIGHEST,
                                   preferred_element_type=F32) + b_ref[0]

    return pl.pallas_call(
        body, name="ada_mod", grid=(2,),
        in_specs=[pl.BlockSpec((nb, D), lambda i: (0, 0)), pl.BlockSpec((1, D, da), lambda i: (i, 0, 0)),
                  pl.BlockSpec((1, 1, da), lambda i: (i, 0, 0))],
        out_specs=[pl.BlockSpec((1, nb, da), lambda i: (i, 0, 0)), pl.BlockSpec((nb, D), lambda i: (0, 0))],
        out_shape=[jax.ShapeDtypeStruct((2, nb, da), F32), jax.ShapeDtypeStruct((nb, D), F32)],
        compiler_params=_cparams(("arbitrary",)),
    )(c_all, ada_w, ada_b)


def _ada_w_grad(c_act_t, dmod):
    nb = c_act_t.shape[1]
    da = dmod.shape[2]
    tn = 512

    def body(c_ref, d_ref, o_ref):
        acc = c_ref[:, 0:1] * d_ref[0, 0:1, :]
        for b in range(1, nb):
            acc = acc + c_ref[:, b:b + 1] * d_ref[0, b:b + 1, :]
        o_ref[0] = acc

    return pl.pallas_call(
        body, name="ada_w_grad", grid=(2, da // tn),
        in_specs=[pl.BlockSpec((D, nb), lambda i, j: (0, 0)), pl.BlockSpec((1, nb, tn), lambda i, j: (i, 0, j))],
        out_specs=pl.BlockSpec((1, D, tn), lambda i, j: (i, 0, j)),
        out_shape=jax.ShapeDtypeStruct((2, D, da), F32),
        compiler_params=_cparams(("arbitrary", "arbitrary")),
    )(c_act_t, dmod)


def _conv_planes(cw, cb):
    cwp = jnp.swapaxes(cw.reshape(3, 2, DFF), 0, 1)
    return jnp.pad(cwp, ((0, 0), (0, 5), (0, 0))), cb.reshape(2, 1, DFF)


def _local_step(x, tgt, mod, wts, small, comm=None):
    wts = dict(wts)
    s = x.shape[0]
    ts, tq, tr, tp = TS, TQ, TR, TP
    tkw = min(TKW, s)
    tf = min(256, s)
    zb = lambda n: jnp.zeros((1, n), F32)
    m6 = mod.reshape(2, 6, 1, D)
    sh1, sc1, g1, sh2, sc2, g2 = ([m6[i, k] for i in range(2)] for k in range(6))
    n1g, n2g = small["norm1_g"], small["norm2_g"]
    row = lambda a, i: a[i:i + 1]

    qg2 = jnp.tile(small["fox_q_gain"], (1, 2))
    kg2 = jnp.tile(small["fox_k_gain"], (1, 2))
    bfp = jnp.pad(small["fox_b_f"], ((0, 0), (0, LANES - H)))
    proj, h1 = _norm_mod_matmul(x, row(n1g, 0), sc1[0], sh1[0], wts["fox_w_in"], zb(FOX_NP), F32, ts, 1408, "fox_in")
    qa, ka, va = _fox_post(proj, qg2, kg2, bfp, tp, "fox_post")
    att, ql, gathered = _attn_fwd(qa, ka, va, tq, "attn_fwd", shards=comm["shards"] if comm else ())
    if comm:
        wts.update(comm["make_wts"](gathered))
    x1, y0, gated = _gate_out(att, proj, wts["fox_w_out"], x, g1[0], ts, "fox_gate_out")

    def ffn_fwd(xin, i, tag):
        cw, cb = _conv_planes(small["ffn_conv_w"][i], small["ffn_conv_b"][i])
        a, h = _norm_mod_matmul(xin, row(n2g, i), sc2[i], sh2[i], wts["ffn_w_up"][i], zb(2 * DFF), BF16, ts, 1408,
                                "ffn_up" + tag, planes=2)
        xo, y, f, ap = _conv_down(a, cw, cb, wts["ffn_w_down"][i], xin, g2[i], min(256, s), "ffn_conv_down" + tag)
        return xo, (a, h, f, y, cw, ap)

    x2, ffn0 = ffn_fwd(x1, 0, "0")

    bst = jnp.pad(small["sgu_b_s"].T, ((0, 0), (0, LANES - SGG)))
    ws = small["sgu_w_s"]
    z, h3 = _norm_mod_matmul(x2, row(n1g, 1), sc1[1], sh1[1], wts["sgu_w_in"], small["sgu_b_in"], BF16, ts, 1024,
                             "sgu_in")
    x3, y1, yy = _sgu_fwd(z, small["sgu_v_gain"], small["sgu_v_bias"], ws, bst, wts["sgu_w_out"], x2, g1[1], tr,
                          "sgu_mix_out")
    x4, ffn1 = ffn_fwd(x3, 1, "1")

    lsum, dx4, dy, redf = _final_loss(x4, small["final_g"], tgt, g2[1], ffn1[3], ts, "final_loss")
    grads = {"final_g": redf[0]}
    dmod = [[None] * 6, [None] * 6]
    dmod[1][5] = redf[1]

    def ffn_bwd(dxo, dy2, xin, i, saved, gprev, yprev, tag):
        a, h, f, _, cw, ap = saved
        wd, wu = wts["ffn_w_down"][i], wts["ffn_w_up"][i]
        g_wd = _matmul(f, dy2, True, False, 1408, D, tkw, BF16, "ffn_dwdown" + tag)
        da, redc = _down_bwd_conv(dy2, wd, a, ap, cw, min(256, s), "ffn_down_bwd_conv" + tag)
        g_wu = _matmul(h, da, True, False, D, 1408, tkw, BF16, "ffn_dwup" + tag, out_parts=N_CHIP)
        outs = _norm_bwd(xin, (da, wu, 1408), dxo, row(n2g, i), sc2[i], gprev, yprev, tf, "ffn_dh_norm_bwd" + tag)
        return outs, g_wd, g_wu, redc

    (dx3, dy1, red), g_wd1, g_wu1, redc1 = ffn_bwd(dx4, dy, x3, 1, ffn1, g1[1], y1, "1")
    dmod[1][3], dmod[1][4], dn2g1, dmod[1][2] = red[0], red[1], red[2], red[3]

    g_swo = _matmul(yy, dy1, True, False, 1024, D, tkw, BF16, "sgu_dwout")
    dyy = _matmul_wt(dy1, wts["sgu_w_out"], 1024, D, BF16, ts, "sgu_dyy")
    wst = jnp.swapaxes(ws, 1, 2)
    dz, rb, rv, dws, dbst = _sgu_bwd(z, dyy, small["sgu_v_gain"], small["sgu_v_bias"], ws, wst, bst, tr, "sgu_mix_bwd")
    g_swi = _matmul(h3, dz, True, False, D, 1024, tkw, BF16, "sgu_dwin", out_parts=N_CHIP)
    dx2, dy2_0, red = _norm_bwd(x2, (dz, wts["sgu_w_in"], 1024), dx3, row(n1g, 1), sc1[1], g2[0], ffn0[3], ts,
                                "sgu_dh_norm_bwd")
    dmod[1][0], dmod[1][1], dn1g1, dmod[0][5] = red[0], red[1], red[2], red[3]

    (dx1, dy0, red), g_wd0, g_wu0, redc0 = ffn_bwd(dx2, dy2_0, x1, 0, ffn0, g1[0], y0, "0")
    dmod[0][3], dmod[0][4], dn2g0, dmod[0][2] = red[0], red[1], red[2], red[3]

    g_fwo = _matmul(gated, dy0, True, False, D, D, tkw, BF16, "fox_dwout")
    glist = comm["rs_lists"]([g_fwo, g_swi, g_swo, g_wu0, g_wu1, g_wd0, g_wd1]) if comm else []
    doa, dop, sibs = _attn_bwd_prep(dy0, wts["fox_w_out"], att, proj, ts, "attn_bwd_prep", glist=glist)
    css = comm["rs_chip_sums"](glist, sibs, "") if comm else []
    dqa, dka, dva, rcvs = _attn_bwd(ql, ka, va, doa, tq, "attn_bwd", css=css)
    dproj, redx = _fox_post_bwd(proj, dqa, dka, dva, dop, qg2, kg2, bfp, tp, "fox_post_bwd")
    g_fwi = _matmul(h1, dproj, True, False, D, 1408, tkw, BF16, "fox_dwin")
    css_fox = comm["rs_prepare_fox"](g_fwi) if comm else []
    outs = _norm_bwd(x, (dproj, wts["fox_w_in"], 1408), dx1, row(n1g, 0), sc1[0], None, None, ts,
                     "fox_dh_norm_bwd", css=css_fox)
    dx0, red = outs[0], outs[1]
    css, rcvs = list(css_fox) + list(css), list(outs[2:]) + list(rcvs)
    dmod[0][0], dmod[0][1], dn1g0 = red[0], red[1], red[2]

    grads.update(
        fox_w_in=g_fwi, fox_w_out=g_fwo, sgu_w_in=g_swi, sgu_w_out=g_swo,
        ffn_w_up=[g_wu0, g_wu1], ffn_w_down=[g_wd0, g_wd1],
        fox_q_gain=redx[0, :DH] + redx[0, DH:], fox_k_gain=redx[1, :DH] + redx[1, DH:], fox_b_f=redx[2, :H],
        sgu_b_in=rb[0], sgu_v_gain=rv[0], sgu_v_bias=rv[1], sgu_w_s=dws, sgu_b_s=dbst[:, :SGG].T,
        ffn_conv_w=jnp.stack([jnp.swapaxes(r[:, 0:3], 0, 1).reshape(3, 2 * DFF) for r in (redc0, redc1)]),
        ffn_conv_b=jnp.stack([r[:, 3].reshape(2 * DFF) for r in (redc0, redc1)]),
        norm1_g=jnp.stack([dn1g0, dn1g1]), norm2_g=jnp.stack([dn2g0, dn2g1]),
    )
    dmod_arr = jnp.stack([jnp.concatenate(dmod[0]), jnp.concatenate(dmod[1])])
    return lsum[0, 0], dx0, grads, dmod_arr, (css, rcvs)


N_DEV = 8
N_CHIP = 4
HBM_SPEC = pl.BlockSpec(memory_space=pltpu.HBM)
VMEM_SPEC = pl.BlockSpec(memory_space=pltpu.VMEM)


def _mesh_pos():
    return lax.axis_index("x"), lax.axis_index("y"), lax.axis_index("c")


def _other_chips(x, y):
    return [(1 - x, y), (x, 1 - y), (1 - x, 1 - y)]


def _remote(src, dst, ssem, rsem, dev):
    return pltpu.make_async_remote_copy(src_ref=src, dst_ref=dst, send_sem=ssem, recv_sem=rsem,
                                        device_id=dev, device_id_type=MESH)


def _allgather8(xb, name):
    m_per, n = xb.shape

    def body(x_ref, out_ref, send_sems, recv_sems, local_sem):
        x, y, c = _mesh_pos()
        me, sibling = (x, y, c), (x, y, 1 - c)
        chips = _other_chips(x, y)

        def rows(px, py, pc):
            return out_ref.at[pl.ds((4 * px + 2 * py + pc) * m_per, m_per), :]

        def copy(k, block, to, src=None):
            return _remote(rows(*block) if src is None else src, rows(*block),
                           send_sems.at[k], recv_sems.at[k], to)

        mine = pltpu.make_async_copy(x_ref, rows(*me), local_sem)
        mine.start()
        first = [copy(0, me, sibling, src=x_ref)]
        first += [copy(1 + j, me, (*chip, c), src=x_ref) for j, chip in enumerate(chips)]
        for cp in first:
            cp.start()
        passed = [copy(4 + j, (*chip, c), sibling) for j, chip in enumerate(chips)]
        for j, chip in enumerate(chips):
            copy(1 + j, (*chip, c), me).wait_recv()
            passed[j].start()
        copy(0, sibling, me).wait_recv()
        for j, chip in enumerate(chips):
            copy(4 + j, (*chip, 1 - c), me).wait_recv()
        for cp in first + passed:
            cp.wait_send()
        mine.wait()

    return pl.pallas_call(
        body, name=name,
        out_shape=jax.ShapeDtypeStruct((N_DEV * m_per, n), xb.dtype),
        in_specs=[VMEM_SPEC], out_specs=VMEM_SPEC,
        scratch_shapes=[pltpu.SemaphoreType.DMA((7,)), pltpu.SemaphoreType.DMA((7,)), pltpu.SemaphoreType.DMA],
        compiler_params=pltpu.CompilerParams(vmem_limit_bytes=V7X_VMEM_LIMIT),
    )(xb)


def _gather_shards(shards, name):
    na = len(shards)

    def body(*refs):
        p_refs, o_refs = refs[:na], refs[na:2 * na]
        send_sems, recv_sems, pass_send, pass_recv = refs[2 * na:]
        x, y, c = _mesh_pos()
        me = 2 * x + y
        sibling = (x, y, 1 - c)
        chips = _other_chips(x, y)

        def half(a, ci, hf):
            rh = shards[a].shape[0] // 2
            return o_refs[a].at[ci, pl.ds(hf * rh, rh), :]

        sends = []
        for a in range(na):
            rh = shards[a].shape[0] // 2
            for k, chip in enumerate(chips):
                sends.append(_remote(p_refs[a].at[pl.ds(c * rh, rh), :], half(a, me, c),
                                     send_sems.at[3 * a + k], recv_sems.at[3 * a + k], (*chip, c)))
        for cp in sends:
            cp.start()
        passed = []
        for a in range(na):
            for k, chip in enumerate(chips):
                ci = 2 * chip[0] + chip[1]
                _remote(half(a, ci, c), half(a, ci, c), send_sems.at[3 * a + k], recv_sems.at[3 * a + k],
                        (*chip, c)).wait_recv()
                cp = _remote(half(a, ci, c), half(a, ci, c), pass_send.at[3 * a + k], pass_recv.at[3 * a + k], sibling)
                cp.start()
                passed.append(cp)
        for a in range(na):
            for k, chip in enumerate(chips):
                ci = 2 * chip[0] + chip[1]
                _remote(half(a, ci, 1 - c), half(a, ci, 1 - c), pass_send.at[3 * a + k], pass_recv.at[3 * a + k],
                        sibling).wait_recv()
        for cp in sends + passed:
            cp.wait_send()

    return pl.pallas_call(
        body, name=name,
        out_shape=[jax.ShapeDtypeStruct((N_CHIP,) + p.shape, p.dtype) for p in shards],
        in_specs=[HBM_SPEC] * na, out_specs=[HBM_SPEC] * na,
        scratch_shapes=[pltpu.SemaphoreType.DMA((3 * na,))] * 4,
    )(*shards)


def _rs_to_sibling(gs, name):
    na = len(gs)

    def body(*refs):
        cps = _sibling_copies(refs[:na], refs[na:2 * na], refs[2 * na], refs[2 * na + 1])
        for cp in cps:
            cp.start()
        for cp in cps:
            cp.wait()

    return pl.pallas_call(
        body, name=name,
        out_shape=[jax.ShapeDtypeStruct((g.shape[0], g.shape[1] // 2, g.shape[2]), g.dtype) for g in gs],
        in_specs=[HBM_SPEC] * na, out_specs=[HBM_SPEC] * na,
        scratch_shapes=[pltpu.SemaphoreType.DMA((na,)), pltpu.SemaphoreType.DMA((na,))],
    )(*gs)


def _rs_chip_sum(g, sib, c_arr, tr, name):
    nc, r, n = g.shape
    rh = r // 2
    g4 = g.reshape(nc, 2, rh, n)

    def body(c_ref, g_ref, s_ref, o_ref):
        o_ref[...] = (g_ref[0].astype(F32) + s_ref[...].astype(F32)).astype(BF16)

    return pl.pallas_call(
        body, name=name, out_shape=jax.ShapeDtypeStruct((nc, rh, n), BF16),
        grid_spec=pltpu.PrefetchScalarGridSpec(
            num_scalar_prefetch=1, grid=(nc, rh // tr),
            in_specs=[pl.BlockSpec((1, 1, tr, n), lambda j, i, cr: (j, cr[0], i, 0)),
                      pl.BlockSpec((1, tr, n), lambda j, i, cr: (j, i, 0))],
            out_specs=pl.BlockSpec((1, tr, n), lambda j, i, cr: (j, i, 0))),
        compiler_params=_cparams(("arbitrary", "arbitrary")),
    )(c_arr, g4, sib)


def _rs_final_sum(cs, rcv, me_arr, tr, name):
    nc, rh, n = cs.shape

    def body(m_ref, c_ref, r_ref, o_ref):
        acc = c_ref[0].astype(F32)
        for k in range(3):
            acc = acc + r_ref[k].astype(F32)
        o_ref[...] = acc

    return pl.pallas_call(
        body, name=name, out_shape=jax.ShapeDtypeStruct((rh, n), F32),
        grid_spec=pltpu.PrefetchScalarGridSpec(
            num_scalar_prefetch=1, grid=(rh // tr,),
            in_specs=[pl.BlockSpec((1, tr, n), lambda i, mr: (mr[0], i, 0)),
                      pl.BlockSpec((3, tr, n), lambda i, mr: (0, i, 0))],
            out_specs=pl.BlockSpec((tr, n), lambda i, mr: (i, 0))),
        compiler_params=_cparams(("arbitrary",)),
    )(me_arr, cs, rcv)


def _rs_swap_halves(halves, name):
    na = len(halves)

    def body(*refs):
        h_refs, o_refs, ssems, rsems = refs[:na], refs[na:2 * na], refs[2 * na], refs[2 * na + 1]
        x, y, c = _mesh_pos()
        cps = []
        for a in range(na):
            cp = _remote(h_refs[a], o_refs[a], ssems.at[a], rsems.at[a], (x, y, 1 - c))
            cp.start()
            cps.append(cp)
        for cp in cps:
            cp.wait()

    return pl.pallas_call(
        body, name=name, out_shape=[jax.ShapeDtypeStruct(h.shape, h.dtype) for h in halves],
        in_specs=[HBM_SPEC] * na, out_specs=[HBM_SPEC] * na,
        scratch_shapes=[pltpu.SemaphoreType.DMA((na,)), pltpu.SemaphoreType.DMA((na,))],
    )(*halves)


def _join_columns(parts, n_out, name):
    p, k, c = parts.shape
    tr = 128

    def body(w_ref, o_ref):
        for j in range(p):
            o_ref[:, j * c:(j + 1) * c] = w_ref[j]
        o_ref[:, p * c:] = jnp.zeros((tr, n_out - p * c), parts.dtype)

    return pl.pallas_call(
        body, name=name, grid=(k // tr,),
        in_specs=[pl.BlockSpec((p, tr, c), lambda i: (0, i, 0))],
        out_specs=pl.BlockSpec((tr, n_out), lambda i: (i, 0)),
        out_shape=jax.ShapeDtypeStruct((k, n_out), parts.dtype),
        compiler_params=_cparams(("arbitrary",)),
    )(parts)


def _split_columns(g, p, c, name):
    k, n = g.shape
    tr = 128

    def body(g_ref, o_ref):
        for j in range(p):
            o_ref[j] = g_ref[:, j * c:(j + 1) * c]

    return pl.pallas_call(
        body, name=name, grid=(k // tr,),
        in_specs=[pl.BlockSpec((tr, n), lambda i: (i, 0))],
        out_specs=pl.BlockSpec((p, tr, c), lambda i: (0, i, 0)),
        out_shape=jax.ShapeDtypeStruct((p, k, c), g.dtype),
        compiler_params=_cparams(("arbitrary",)),
    )(g)


def _sum8(g, name):
    nd, r, n = g.shape

    def body(g_ref, o_ref):
        acc = g_ref[0]
        for k in range(1, nd):
            acc = acc + g_ref[k]
        o_ref[...] = acc

    return pl.pallas_call(
        body, name=name, grid=(r // 8,),
        in_specs=[pl.BlockSpec((nd, 8, n), lambda i: (0, i, 0))],
        out_specs=pl.BlockSpec((8, n), lambda i: (i, 0)),
        out_shape=jax.ShapeDtypeStruct((r, n), F32),
        compiler_params=_cparams(("arbitrary",)),
    )(g)


def _adamw(w, g, m, v, name):
    r, n = w.shape
    tr = next(t for t in (128, 64, 32, 16, 8) if r % t == 0)
    bc1 = 1.0 - ADAM_B1 ** ADAM_STEP
    bc2 = 1.0 - ADAM_B2 ** ADAM_STEP

    def body(w_ref, g_ref, m_ref, v_ref, d_ref, mo_ref, vo_ref):
        gv = g_ref[...]
        mn = ADAM_B1 * m_ref[...] + (1.0 - ADAM_B1) * gv
        vn = ADAM_B2 * v_ref[...] + (1.0 - ADAM_B2) * (gv * gv)
        d_ref[...] = -ADAM_LR * ((mn / bc1) / (jnp.sqrt(vn / bc2) + ADAM_EPS) + ADAM_WD * w_ref[...])
        mo_ref[...] = mn
        vo_ref[...] = vn

    blk = pl.BlockSpec((tr, n), lambda i: (i, 0))
    shp = jax.ShapeDtypeStruct((r, n), F32)
    return pl.pallas_call(
        body, name=name, grid=(r // tr,), in_specs=[blk] * 4, out_specs=[blk] * 3, out_shape=[shp] * 3,
        compiler_params=_cparams(("arbitrary",)),
    )(w, g, m, v)


ROW = 1024
BIG = ("fox_w_in", "fox_w_out", "sgu_w_in", "sgu_w_out", "ffn_w_up", "ffn_w_down")
SMALL_SHARDED = ("sgu_b_in", "sgu_v_gain", "sgu_v_bias", "ffn_conv_w")
SMALL_REPL = ("fox_b_f", "fox_q_gain", "fox_k_gain", "sgu_w_s", "sgu_b_s", "ffn_conv_b", "ada_b",
              "norm1_g", "norm2_g", "final_g")
WEIGHTS = ("fox_w_in", "fox_b_f", "fox_q_gain", "fox_k_gain", "fox_w_out", "sgu_w_in", "sgu_b_in", "sgu_v_gain",
           "sgu_v_bias", "sgu_w_s", "sgu_b_s", "sgu_w_out", "ffn_w_up", "ffn_conv_w", "ffn_conv_b", "ffn_w_down",
           "ada_w", "ada_b", "norm1_g", "norm2_g", "final_g")


def _rows_of(a, mult=1):
    flat = a.reshape(-1)
    rows = -(-flat.shape[0] // ROW)
    rows = -(-rows // mult) * mult
    return jnp.pad(flat, (0, rows * ROW - flat.shape[0])).reshape(rows, ROW)


def _pack(parts, mult, total=None):
    p = jnp.concatenate([_rows_of(a, mult) for a in parts], axis=0)
    if total is not None:
        p = jnp.pad(p, ((0, total - p.shape[0]), (0, 0)))
    return p


def _unpack(pack, shapes, mult):
    out, r0 = [], 0
    for shp in shapes:
        size = int(np.prod(shp))
        rows = -(-(-(-size // ROW)) // mult) * mult
        out.append(pack[r0:r0 + rows].reshape(-1)[:size].reshape(shp))
        r0 += rows
    return out


def _big_shards(t):
    return [t["fox_w_in"][0], t["fox_w_out"][0], t["sgu_w_in"][0], t["sgu_w_out"][0],
            t["ffn_w_up"][0], t["ffn_w_up"][1], t["ffn_w_down"][0], t["ffn_w_down"][1]]


def _row_tile(rows):
    return next(t for t in (512, 352, 256, 128, 64) if rows % t == 0)


def kernel(x, c, fox_w_in, fox_b_f, fox_q_gain, fox_k_gain, fox_w_out, sgu_w_in, sgu_b_in, sgu_v_gain, sgu_v_bias, sgu_w_s, sgu_b_s, sgu_w_out, ffn_w_up, ffn_conv_w, ffn_conv_b, ffn_w_down, ada_w, ada_b, norm1_g, norm2_g, final_g, loss_target, m_fox_w_in, m_fox_b_f, m_fox_q_gain, m_fox_k_gain, m_fox_w_out, m_sgu_w_in, m_sgu_b_in, m_sgu_v_gain, m_sgu_v_bias, m_sgu_w_s, m_sgu_b_s, m_sgu_w_out, m_ffn_w_up, m_ffn_conv_w, m_ffn_conv_b, m_ffn_w_down, m_ada_w, m_ada_b, m_norm1_g, m_norm2_g, m_final_g, v_fox_w_in, v_fox_b_f, v_fox_q_gain, v_fox_k_gain, v_fox_w_out, v_sgu_w_in, v_sgu_b_in, v_sgu_v_gain, v_sgu_v_bias, v_sgu_w_s, v_sgu_b_s, v_sgu_w_out, v_ffn_w_up, v_ffn_conv_w, v_ffn_conv_b, v_ffn_w_down, v_ada_w, v_ada_b, v_norm1_g, v_norm2_g, v_final_g):
    w = dict(fox_w_in=fox_w_in, fox_b_f=fox_b_f, fox_q_gain=fox_q_gain, fox_k_gain=fox_k_gain, fox_w_out=fox_w_out,
             sgu_w_in=sgu_w_in, sgu_b_in=sgu_b_in, sgu_v_gain=sgu_v_gain, sgu_v_bias=sgu_v_bias, sgu_w_s=sgu_w_s,
             sgu_b_s=sgu_b_s, sgu_w_out=sgu_w_out, ffn_w_up=ffn_w_up, ffn_conv_w=ffn_conv_w, ffn_conv_b=ffn_conv_b,
             ffn_w_down=ffn_w_down, ada_w=ada_w, ada_b=ada_b, norm1_g=norm1_g, norm2_g=norm2_g, final_g=final_g)
    mom = dict(fox_w_in=m_fox_w_in, fox_b_f=m_fox_b_f, fox_q_gain=m_fox_q_gain, fox_k_gain=m_fox_k_gain,
               fox_w_out=m_fox_w_out, sgu_w_in=m_sgu_w_in, sgu_b_in=m_sgu_b_in, sgu_v_gain=m_sgu_v_gain,
               sgu_v_bias=m_sgu_v_bias, sgu_w_s=m_sgu_w_s, sgu_b_s=m_sgu_b_s, sgu_w_out=m_sgu_w_out,
               ffn_w_up=m_ffn_w_up, ffn_conv_w=m_ffn_conv_w, ffn_conv_b=m_ffn_conv_b, ffn_w_down=m_ffn_w_down,
               ada_w=m_ada_w, ada_b=m_ada_b, norm1_g=m_norm1_g, norm2_g=m_norm2_g, final_g=m_final_g)
    var = dict(fox_w_in=v_fox_w_in, fox_b_f=v_fox_b_f, fox_q_gain=v_fox_q_gain, fox_k_gain=v_fox_k_gain,
               fox_w_out=v_fox_w_out, sgu_w_in=v_sgu_w_in, sgu_b_in=v_sgu_b_in, sgu_v_gain=v_sgu_v_gain,
               sgu_v_bias=v_sgu_v_bias, sgu_w_s=v_sgu_w_s, sgu_b_s=v_sgu_b_s, sgu_w_out=v_sgu_w_out,
               ffn_w_up=v_ffn_w_up, ffn_conv_w=v_ffn_conv_w, ffn_conv_b=v_ffn_conv_b, ffn_w_down=v_ffn_w_down,
               ada_w=v_ada_w, ada_b=v_ada_b, norm1_g=v_norm1_g, norm2_g=v_norm2_g, final_g=v_final_g)

    ax, ay, ac = _mesh_pos()
    chip = 2 * ax + ay
    dev = 2 * chip + ac

    small_shard_shapes = tuple(w[n].shape for n in SMALL_SHARDED)
    blk = _pack([c] + [w[n] for n in SMALL_SHARDED], 1, 16)
    gat = _allgather8(blk, "gather_small").reshape(N_DEV, 16, ROW)
    c_all = gat[:, 0, :]
    per_chip = [_unpack(gat[2 * j, 1:], small_shard_shapes, 1) for j in range(N_CHIP)]
    full_small = {n: jnp.concatenate([per_chip[j][i] for j in range(N_CHIP)], axis=-1)
                  for i, n in enumerate(SMALL_SHARDED)}

    mine = [a.astype(BF16) for a in _big_shards(w)]
    with_own = lambda gat, own: [lax.dynamic_update_slice(g_, m_[None], (chip, 0, 0)) for g_, m_ in zip(gat, own)]
    fwi, = with_own(_gather_shards(mine[:1], "gather_fox_w_in"), mine[:1])
    fwi_full = _join_columns(fwi, FOX_NP, "join_fox_w_in")
    wts = dict(fox_w_in=fwi_full)

    def make_wts(gathered):
        fwo, swi, swo, up0, up1, dn0, dn1 = with_own(gathered, mine[1:])
        return dict(fox_w_out=fwo.reshape(D, D), sgu_w_in=swi, sgu_w_out=swo.reshape(SGW, D),
                    ffn_w_up=[up0, up1], ffn_w_down=[dn0.reshape(DFF, D), dn1.reshape(DFF, D)])

    c_arr = jnp.reshape(ac, (1,)).astype(jnp.int32)
    me_arr = jnp.reshape(chip, (1,)).astype(jnp.int32)

    def rs_chip_sums(glist, sibs, tag):
        return [_rs_chip_sum(g_, s_, c_arr, _row_tile(s_.shape[1]), "rs_chip_sum%s%d" % (tag, a))
                for a, (g_, s_) in enumerate(zip(glist, sibs))]

    def rs_lists(gl):
        g_fwo, g_swi, g_swo, g_wu0, g_wu1, g_wd0, g_wd1 = gl
        return [g_fwo.reshape(N_CHIP, 256, D), g_swi, g_swo.reshape(N_CHIP, 512, D), g_wu0, g_wu1,
                g_wd0.reshape(N_CHIP, 704, D), g_wd1.reshape(N_CHIP, 704, D)]

    def rs_prepare_fox(g_fwi):
        glist = [_split_columns(g_fwi, N_CHIP, FOX_N // N_CHIP, "split_fox_w_in")]
        return rs_chip_sums(glist, _rs_to_sibling(glist, "rs_sibling_fox"), "_fox")

    comm = dict(shards=mine[1:], make_wts=make_wts, rs_lists=rs_lists, rs_chip_sums=rs_chip_sums,
                rs_prepare_fox=rs_prepare_fox)

    da = ada_w.shape[2]
    ada_b_cols = lax.dynamic_slice_in_dim(ada_b, chip * da, da, axis=1)[:, None, :]
    mod_cols, c_act = _ada_mod(c_all, ada_w, ada_b_cols)
    mod_all = _allgather8(mod_cols.reshape(-1, ROW), "gather_mod").reshape(N_DEV, 2, N_DEV, da)
    mod_mine = lax.dynamic_index_in_dim(mod_all[0::2], dev, axis=2, keepdims=False)
    mod = jnp.swapaxes(mod_mine, 0, 1).reshape(2, N_CHIP * da)

    small = dict(norm1_g=norm1_g, norm2_g=norm2_g, final_g=final_g[None], fox_q_gain=fox_q_gain,
                 fox_k_gain=fox_k_gain, fox_b_f=fox_b_f, sgu_b_in=full_small["sgu_b_in"],
                 sgu_v_gain=full_small["sgu_v_gain"], sgu_v_bias=full_small["sgu_v_bias"], sgu_w_s=sgu_w_s[0],
                 sgu_b_s=sgu_b_s[0], ffn_conv_w=full_small["ffn_conv_w"], ffn_conv_b=ffn_conv_b)
    loss_dev, dx, g, dmod, (css, rcvs) = _local_step(x[0], loss_target[0], mod, wts, small, comm)

    g["ada_b"] = dmod
    g["loss"] = loss_dev
    small_names = ("ada_b",) + SMALL_SHARDED + tuple(n for n in SMALL_REPL if n != "ada_b") + ("loss",)
    gs = _pack([g[n] for n in small_names], 1)
    rows_s = -(-gs.shape[0] // 8) * 8
    gs = jnp.pad(gs, ((0, rows_s - gs.shape[0]), (0, 0)))
    gs_all = _allgather8(gs, "gather_small_grads").reshape(N_DEV, rows_s, ROW)
    gsum = _sum8(gs_all, "sum_small_grads")
    full_shapes = {n: w[n].shape for n in SMALL_REPL}
    full_shapes.update({n: w[n].shape[:-1] + (w[n].shape[-1] * N_CHIP,) for n in SMALL_SHARDED})
    full_shapes["loss"] = ()
    gfull = dict(zip(small_names, _unpack(gsum, [full_shapes[n] for n in small_names], 1)))
    grads = {n: gfull[n] for n in SMALL_REPL}
    for n in SMALL_SHARDED:
        width = w[n].shape[-1]
        grads[n] = lax.dynamic_slice_in_dim(gfull[n], chip * width, width, axis=gfull[n].ndim - 1)
    dmod_all = gs_all[:, :12, :].reshape(N_DEV, 2, N_CHIP * da)
    dmod_cols = jnp.swapaxes(lax.dynamic_slice_in_dim(dmod_all, chip * da, da, axis=2), 0, 1)
    grads["ada_w"] = _ada_w_grad(c_act.T, dmod_cols)

    halves =[_rs_final_sum(cs_, r_, me_arr, _row_tile(cs_.shape[1]), "rs_final_sum%d" % a)
              for a, (cs_, r_) in enumerate(zip(css, rcvs))]
    others = _rs_swap_halves(halves, "rs_swap")
    red = [jnp.concatenate([jnp.where(ac == 0, h_, o_), jnp.where(ac == 0, o_, h_)]) for h_, o_ in zip(halves, others)]
    grads.update(fox_w_in=red[0], fox_w_out=red[1], sgu_w_in=red[2], sgu_w_out=red[3],
                 ffn_w_up=jnp.stack([red[4], red[5]]), ffn_w_down=jnp.stack([red[6], red[7]]))

    delta, new_m, new_v = {}, {}, {}
    for n in BIG + ("ada_w",):
        shp = w[n].shape
        two_d = lambda a: a.reshape(-1, shp[-1])
        d_, m_, v_ = _adamw(two_d(w[n]), two_d(grads[n]), two_d(mom[n]), two_d(var[n]), "adamw_" + n)
        delta[n], new_m[n], new_v[n] = d_.reshape(shp), m_.reshape(shp), v_.reshape(shp)
    rest = SMALL_SHARDED + SMALL_REPL
    packs = [_pack([t[n] for n in rest], 1) for t in (w, grads, mom, var)]
    rows_r = -(-packs[0].shape[0] // 8) * 8
    packs = [jnp.pad(p, ((0, rows_r - p.shape[0]), (0, 0))) for p in packs]
    outs = _adamw(*packs, "adamw_small")
    for t, o in zip((delta, new_m, new_v), outs):
        t.update(zip(rest, _unpack(o, [w[n].shape for n in rest], 1)))

    loss = gfull["loss"]
    return (loss, dx[None], *[grads[n].reshape(w[n].shape) for n in WEIGHTS], *[delta[n] for n in WEIGHTS],
            *[new_m[n] for n in WEIGHTS], *[new_v[n] for n in WEIGHTS])
```

```python
import functools
import math

import numpy as np
import jax
import jax.numpy as jnp
from jax import lax
from jax.experimental import pallas as pl
from jax.experimental.pallas import tpu as pltpu

F32 = jnp.float32
BF16 = jnp.bfloat16
MESH = pl.DeviceIdType.MESH

D = 1024
H = 16
DH = 64
NP = H // 2
LANES = 128
DFF = 2816
SGW = 2048
SGG = 8
SGC = 256
SGB = 128
CHUNK = 64
EPS = 1e-6
FOX_N = 4 * D + H
FOX_NP = 4224
GT = 256
NGT = DFF // GT
SCALE = DH ** -0.5
LOG2E = 1.4426950408889634

ADAM_LR = 0.001
ADAM_B1 = 0.9
ADAM_B2 = 0.999
ADAM_EPS = 1e-08
ADAM_WD = 0.01
ADAM_STEP = 10

V7X_VMEM_LIMIT = 56 * 1024 * 1024

L_F = 64
L_NF = 67
L_LSE = 70


def _cparams(sem=None):
    return pltpu.CompilerParams(dimension_semantics=sem, vmem_limit_bytes=V7X_VMEM_LIMIT)


def _split3(x):
    hi = x.astype(BF16)
    r = x - hi.astype(F32)
    mid = r.astype(BF16)
    lo = (r - mid.astype(F32)).astype(BF16)
    return hi, mid, lo


def _dot(a, b, dims=(((1,), (0,)), ((), ()))):
    return lax.dot_general(a, b, dims, preferred_element_type=F32)


def _dot_nt(a, b):
    return _dot(a, b, (((1,), (1,)), ((), ())))


def _dot_tn(a, b):
    return _dot(a, b, (((0,), (0,)), ((), ())))


def _exact_dot(m_bf16, x_f32):
    hi, mid, lo = _split3(x_f32)
    return _dot(m_bf16, hi) + _dot(m_bf16, mid) + _dot(m_bf16, lo)


def _exact_dot_r(x_f32, m_bf16):
    hi, mid, lo = _split3(x_f32)
    return _dot(hi, m_bf16) + _dot(mid, m_bf16) + _dot(lo, m_bf16)


def _head_block_ones():
    r = lax.broadcasted_iota(jnp.int32, (LANES, LANES), 0) // DH
    c = lax.broadcasted_iota(jnp.int32, (LANES, LANES), 1) // DH
    return (r == c).astype(BF16)


def _sigmoid(x):
    return 1.0 / (1.0 + jnp.exp(-x))


def _gelu(x):
    c = math.sqrt(2.0 / math.pi)
    return 0.5 * x * (1.0 + jnp.tanh(c * (x + 0.044715 * (x * x * x))))


def _gelu_and_grad(x):
    c = math.sqrt(2.0 / math.pi)
    x2 = x * x
    t = jnp.tanh(c * (x + 0.044715 * (x2 * x)))
    half = 0.5 * (1.0 + t)
    return x * half, half + 0.5 * x * (1.0 - t * t) * c * (1.0 + 3 * 0.044715 * x2)


def _rstd_rows(x):
    return lax.rsqrt(jnp.mean(x * x, axis=-1, keepdims=True) + EPS)


def _norm_mod_matmul(x, ng, sc, sh, w, bias, out_dtype, ts, tn, name, planes=1):
    s, d = x.shape
    ns = w.shape[-1]
    n = w.shape[0] * ns if w.ndim == 3 else ns
    nc = n // planes

    def body(x_ref, ng_ref, sc_ref, sh_ref, w_ref, b_ref, o_ref, h_ref):
        xv = x_ref[...]
        h = (xv * _rstd_rows(xv) * ng_ref[...] * (1.0 + sc_ref[...]) + sh_ref[...]).astype(BF16)
        h_ref[...] = h
        for e in range(planes):
            for c0 in range(0, nc, tn):
                g0 = e * nc + c0
                wv = w_ref[g0 // ns, :, g0 % ns:g0 % ns + tn] if w.ndim == 3 else w_ref[:, g0:g0 + tn]
                val = (_dot(h, wv) + b_ref[:, g0:g0 + tn]).astype(out_dtype)
                if planes == 1:
                    o_ref[:, c0:c0 + tn] = val
                else:
                    o_ref[e, :, c0:c0 + tn] = val

    vec = pl.BlockSpec((1, d), lambda i: (0, 0))
    w_spec = (pl.BlockSpec(w.shape, lambda i: (0, 0, 0)) if w.ndim == 3 else pl.BlockSpec((d, n), lambda i: (0, 0)))
    if planes == 1:
        o_spec, o_shape = pl.BlockSpec((ts, n), lambda i: (i, 0)), (s, n)
    else:
        o_spec, o_shape = pl.BlockSpec((planes, ts, nc), lambda i: (0, i, 0)), (planes, s, nc)
    return pl.pallas_call(
        body, name=name, grid=(s // ts,),
        in_specs=[pl.BlockSpec((ts, d), lambda i: (i, 0)), vec, vec, vec, w_spec,
                  pl.BlockSpec((1, n), lambda i: (0, 0))],
        out_specs=[o_spec, pl.BlockSpec((ts, d), lambda i: (i, 0))],
        out_shape=[jax.ShapeDtypeStruct(o_shape, out_dtype), jax.ShapeDtypeStruct((s, d), BF16)],
        compiler_params=_cparams(("arbitrary",)),
    )(x, ng, sc, sh, w, bias)


def _matmul(a, b, ta, tb, tm, tn, tk, out_dtype, name, out_parts=1):
    if a.ndim == 3:
        m, k = a.shape[1], a.shape[0] * a.shape[2]
        nkp = a.shape[2] // tk
    else:
        m, k = (a.shape[1], a.shape[0]) if ta else a.shape
    if b.ndim == 3:
        n = b.shape[1] if tb else b.shape[0] * b.shape[2]
        nbp = b.shape[2] // (tk if tb else tn)
    else:
        n = b.shape[0] if tb else b.shape[1]
    nk = k // tk
    nop = n // out_parts // tn
    dims = (((0,) if ta else (1,), (1,) if tb else (0,)), ((), ()))

    def body(a_ref, b_ref, o_ref, acc):
        kk = pl.program_id(2)

        @pl.when(kk == 0)
        def _():
            acc[...] = jnp.zeros_like(acc)
        acc[...] += _dot(a_ref[...], b_ref[...], dims)

        @pl.when(kk == nk - 1)
        def _():
            o_ref[...] = acc[...].astype(out_dtype)

    if a.ndim == 3:
        a_spec = pl.BlockSpec((None, tm, tk), lambda i, j, kk: (kk // nkp, i, kk % nkp))
    else:
        a_spec = (pl.BlockSpec((tk, tm), lambda i, j, kk: (kk, i)) if ta
                  else pl.BlockSpec((tm, tk), lambda i, j, kk: (i, kk)))
    if b.ndim == 3 and tb:
        b_spec = pl.BlockSpec((None, tn, tk), lambda i, j, kk: (kk // nbp, j, kk % nbp))
    elif b.ndim == 3:
        b_spec = pl.BlockSpec((None, tk, tn), lambda i, j, kk: (j // nbp, kk, j % nbp))
    else:
        b_spec = (pl.BlockSpec((tn, tk), lambda i, j, kk: (j, kk)) if tb
                  else pl.BlockSpec((tk, tn), lambda i, j, kk: (kk, j)))
    if out_parts > 1:
        o_spec = pl.BlockSpec((None, tm, tn), lambda i, j, kk: (j // nop, i, j % nop))
        o_shape = (out_parts, m, n // out_parts)
    else:
        o_spec, o_shape = pl.BlockSpec((tm, tn), lambda i, j, kk: (i, j)), (m, n)
    return pl.pallas_call(
        body, name=name, grid=(m // tm, n // tn, nk),
        in_specs=[a_spec, b_spec],
        out_specs=o_spec,
        out_shape=jax.ShapeDtypeStruct(o_shape, out_dtype),
        scratch_shapes=[pltpu.VMEM((tm, tn), F32)],
        compiler_params=_cparams(("arbitrary", "arbitrary", "arbitrary")),
    )(a, b)


def _matmul_wt(a, w, tn, tk, out_dtype, ts, name):
    s = a.shape[-2]
    ka, kw = a.shape[-1], w.shape[-1]
    k = ka * (a.shape[0] if a.ndim == 3 else 1)
    n = w.shape[-2]

    def body(a_ref, w_ref, o_ref):
        for n0 in range(0, n, tn):
            acc = None
            for g0 in range(0, k, tk):
                av = a_ref[g0 // ka, :, g0 % ka:g0 % ka + tk] if a.ndim == 3 else a_ref[:, g0:g0 + tk]
                wv = (w_ref[g0 // kw, n0:n0 + tn, g0 % kw:g0 % kw + tk] if w.ndim == 3
                      else w_ref[n0:n0 + tn, g0:g0 + tk])
                part = _dot_nt(av, wv)
                acc = part if acc is None else acc + part
            o_ref[:, n0:n0 + tn] = acc.astype(out_dtype)

    a_spec = (pl.BlockSpec((a.shape[0], ts, ka), lambda i: (0, i, 0)) if a.ndim == 3
              else pl.BlockSpec((ts, ka), lambda i: (i, 0)))
    w_spec = pl.BlockSpec(w.shape, (lambda i: (0, 0, 0)) if w.ndim == 3 else (lambda i: (0, 0)))
    return pl.pallas_call(
        body, name=name, grid=(s // ts,),
        in_specs=[a_spec, w_spec], out_specs=pl.BlockSpec((ts, n), lambda i: (i, 0)),
        out_shape=jax.ShapeDtypeStruct((s, n), out_dtype),
        compiler_params=_cparams(("arbitrary",)),
    )(a, w)


def _lane(shape):
    return lax.broadcasted_iota(jnp.int32, shape, 1)


def _pair_norm(x, gain2, bones):
    msq = _exact_dot_r(x * x, bones) * (1.0 / DH)
    r = lax.rsqrt(msq + EPS)
    xh = x * r
    return xh * gain2, xh, r


def _fox_post(proj, qg2, kg2, bf, ts, name):
    s = proj.shape[0]

    def body(p_ref, qg_ref, kg_ref, bf_ref, q_ref, k_ref, v_ref, carry):
        @pl.when(pl.program_id(0) == 0)
        def _():
            carry[...] = jnp.zeros_like(carry)
        lane = _lane((ts, LANES))
        bones = _head_block_ones()
        xf = p_ref[:, 4 * D:4 * D + LANES] + bf_ref[...]
        logf = jnp.minimum(xf, 0.0) - jnp.log(1.0 + jnp.exp(-jnp.abs(xf)))
        logf = jnp.where(lane < H, logf, 0.0)
        rr = lax.broadcasted_iota(jnp.int32, (ts, ts), 0)
        cc = lax.broadcasted_iota(jnp.int32, (ts, ts), 1)
        ltri = (cc <= rr).astype(BF16)
        fcum = _exact_dot(ltri, logf) + carry[0:1, :]
        carry[0:1, :] = fcum[ts - 1:ts, :]
        fhi, fmid, flo = _split3(fcum * LOG2E)
        fhi, fmid, flo = fhi.astype(F32), fmid.astype(F32), flo.astype(F32)
        one_q = ((lane >= L_NF) & (lane < L_NF + 3)).astype(F32)
        one_k = (((lane >= L_F) & (lane < L_F + 3)) | ((lane >= L_LSE) & (lane < L_LSE + 3))).astype(F32)
        one_v = ((lane >= L_F) & (lane < L_F + 3)).astype(F32)
        for p in range(NP):
            qn, _, _ = _pair_norm(p_ref[:, p * LANES:(p + 1) * LANES], qg_ref[...], bones)
            kn, _, _ = _pair_norm(p_ref[:, D + p * LANES:D + (p + 1) * LANES], kg_ref[...], bones)
            vv = p_ref[:, 2 * D + p * LANES:2 * D + (p + 1) * LANES]
            qn = qn * (SCALE * LOG2E)
            for e in range(2):
                h = 2 * p + e
                if e == 1:
                    qe, ke, ve = (pltpu.roll(t, DH, axis=1) for t in (qn, kn, vv))
                else:
                    qe, ke, ve = qn, kn, vv
                f0, f1, f2 = fhi[:, h:h + 1], fmid[:, h:h + 1], flo[:, h:h + 1]
                fq = jnp.where(lane == L_F, f0, jnp.where(lane == L_F + 1, f1, jnp.where(lane == L_F + 2, f2, one_q)))
                fk = jnp.where(lane == L_NF, -f0, jnp.where(lane == L_NF + 1, -f1, jnp.where(lane == L_NF + 2, -f2, one_k)))
                q_ref[h] = jnp.where(lane < DH, qe, fq).astype(BF16)
                k_ref[h] = jnp.where(lane < DH, ke, fk).astype(BF16)
                v_ref[h] = jnp.where(lane < DH, ve, one_v).astype(BF16)

    hs = pl.BlockSpec((H, ts, LANES), lambda i: (0, i, 0))
    vec = pl.BlockSpec((1, LANES), lambda i: (0, 0))
    shp = jax.ShapeDtypeStruct((H, s, LANES), BF16)
    return pl.pallas_call(
        body, name=name, grid=(s // ts,),
        in_specs=[pl.BlockSpec((ts, FOX_NP), lambda i: (i, 0)), vec, vec, vec],
        out_specs=[hs, hs, hs], out_shape=[shp, shp, shp],
        scratch_shapes=[pltpu.VMEM((8, LANES), F32)],
        compiler_params=_cparams(("arbitrary",)),
    )(proj, qg2, kg2, bf)


def _gather_copies(p_refs, o_refs, send_sems, recv_sems):
    x, y, c = _mesh_pos()
    me = 2 * x + y
    sends, arrivals = [], []
    for a, (p_ref, o_ref) in enumerate(zip(p_refs, o_refs)):
        rh = p_ref.shape[0] // 2
        for k, chip in enumerate(_other_chips(x, y)):
            ci = 2 * chip[0] + chip[1]
            for cc in range(2):
                sends.append(_remote(p_ref.at[pl.ds(c * rh, rh), :], o_ref.at[me, pl.ds(c * rh, rh), :],
                                     send_sems.at[6 * a + 2 * k + cc], recv_sems.at[6 * a + 2 * k + c], (*chip, cc)))
                arrivals.append(_remote(o_ref.at[ci, pl.ds(cc * rh, rh), :], o_ref.at[ci, pl.ds(cc * rh, rh), :],
                                        send_sems.at[6 * a + 2 * k + cc], recv_sems.at[6 * a + 2 * k + cc],
                                        (*chip, cc)))
    return sends, arrivals


def _attn_fwd(qa, ka, va, tq, name, shards=()):
    s = qa.shape[1]
    nq = s // tq
    na = len(shards)
    hps = HPS_FWD

    def body(*refs):
        q_ref, k_ref, v_ref = refs[:3]
        p_refs = refs[3:3 + na]
        o_ref, ql_ref = refs[3 + na:5 + na]
        g_refs = refs[5 + na:5 + 2 * na]
        i = pl.program_id(1)
        if na:
            send_sems, recv_sems = refs[5 + 2 * na:]

            @pl.when((pl.program_id(0) == 0) & (i == 0))
            def _():
                for cp in _gather_copies(p_refs, g_refs, send_sems, recv_sems)[0]:
                    cp.start()
        lane = _lane((tq, LANES))
        qs_ = [q_ref[e] for e in range(hps)]

        tk = min(TK_FWD, tq)
        nks = tq // tk

        def step(j, carry, diag=None):
            off = pl.multiple_of(j * tk, tk)
            scs = [_dot_nt(qs_[e], k_ref[e, pl.ds(off, tk), :]) for e in range(hps)]
            probs = []
            for e in range(hps):
                m, sc = carry[e][0], scs[e]
                if diag is not None:
                    rr = lax.broadcasted_iota(jnp.int32, (tq, tk), 0)
                    cc = lax.broadcasted_iota(jnp.int32, (tq, tk), 1) + diag * tk
                    sc = jnp.where(cc <= rr, sc, -jnp.inf)
                m_new = jnp.maximum(m, jnp.max(sc, axis=-1, keepdims=True))
                probs.append((m_new, jnp.exp2(sc - m_new).astype(BF16), jnp.exp2(m - m_new)))
            return tuple((m_new, carry[e][1] * alpha + _dot(pr, v_ref[e, pl.ds(off, tk), :]))
                         for e, (m_new, pr, alpha) in enumerate(probs))

        one = (jnp.full((tq, 1), -jnp.inf, F32), jnp.zeros((tq, LANES), F32))
        carry = lax.fori_loop(0, i * nks, step, (one,) * hps)
        for r in range(nks):
            carry = step(i * nks + r, carry, diag=r)
        outs = []
        for e in range(hps):
            m, acc = carry[e]
            l = acc[:, L_F:L_F + 1]
            outs.append(acc / l)
            lse = m + jnp.log2(l)
            h0, h1, h2 = _split3(-lse)
            ql = jnp.where(lane == L_LSE, h0.astype(F32),
                           jnp.where(lane == L_LSE + 1, h1.astype(F32),
                                     jnp.where(lane == L_LSE + 2, h2.astype(F32), qs_[e].astype(F32))))
            ql_ref[e] = ql.astype(BF16)
        for e in range(0, hps, 2):
            o_ref[:, e * DH:(e + 2) * DH] = jnp.where(lane < DH, outs[e], pltpu.roll(outs[e + 1], DH, axis=1))
        if na:
            @pl.when((pl.program_id(0) == H // hps - 1) & (i == nq - 1))
            def _():
                sends, arrivals = _gather_copies(p_refs, g_refs, send_sems, recv_sems)
                for cp in arrivals:
                    cp.wait_recv()
                for cp in sends:
                    cp.wait_send()

    res = pl.BlockSpec((hps, s, LANES), lambda p, i: (p, 0, 0))
    qs = pl.BlockSpec((hps, tq, LANES), lambda p, i: (p, i, 0))
    outs = pl.pallas_call(
        body, name=name, grid=(H // hps, nq),
        in_specs=[qs, res, res] + [HBM_SPEC] * na,
        out_specs=[pl.BlockSpec((tq, hps * DH), lambda p, i: (i, p)), qs] + [HBM_SPEC] * na,
        out_shape=[jax.ShapeDtypeStruct((s, D), F32), jax.ShapeDtypeStruct((H, s, LANES), BF16)]
        + [jax.ShapeDtypeStruct((N_CHIP,) + p.shape, p.dtype) for p in shards],
        scratch_shapes=[pltpu.SemaphoreType.DMA((6 * na,))] * 2 if na else [],
        compiler_params=_cparams(("arbitrary", "arbitrary")),
    )(qa, ka, va, *shards)
    return outs[0], outs[1], list(outs[2:])


def _chip_exchange_copies(cs_refs, o_refs, send_sems, recv_sems):
    x, y, c = _mesh_pos()
    cps = []
    for a, (cs_ref, o_ref) in enumerate(zip(cs_refs, o_refs)):
        for k, chip in enumerate(_other_chips(x, y)):
            ci = 2 * chip[0] + chip[1]
            cps.append(_remote(cs_ref.at[ci], o_ref.at[k], send_sems.at[3 * a + k], recv_sems.at[3 * a + k],
                               (*chip, c)))
    return cps


def _attn_bwd(ql, ka, va, doa, tq, name, css=()):
    s = ql.shape[1]
    nq = s // tq
    na = len(css)

    def body(*refs):
        q_ref, k_ref, v_ref, do_ref = refs[:4]
        cs_refs = refs[4:4 + na]
        dqo_ref, dk_ref, dv_ref = refs[4 + na:7 + na]
        r_refs = refs[7 + na:7 + 2 * na]
        dq_ref = refs[7 + 2 * na]
        j = pl.program_id(1)
        if na:
            send_sems, recv_sems = refs[8 + 2 * na:]

            @pl.when((pl.program_id(0) == 0) & (j == 0))
            def _():
                for cp in _chip_exchange_copies(cs_refs, r_refs, send_sems, recv_sems):
                    cp.start()

        @pl.when(j == 0)
        def _():
            dq_ref[...] = jnp.zeros_like(dq_ref)
        lane = _lane((tq, LANES))
        kbs = [k_ref[0], k_ref[1]]
        vbs = [v_ref[0], v_ref[1]]

        def step(i, carry, masked):
            ioff = pl.multiple_of(i * tq, tq)
            qbs = [q_ref[e, pl.ds(ioff, tq), :] for e in range(2)]
            dobs = [do_ref[e, pl.ds(ioff, tq), :] for e in range(2)]
            scs = [_dot_nt(qbs[e], kbs[e]) for e in range(2)]
            dps = [_dot_nt(dobs[e], vbs[e]) for e in range(2)]
            prs, dss = [], []
            for e in range(2):
                pr = jnp.exp2(scs[e])
                if masked:
                    rr = lax.broadcasted_iota(jnp.int32, (tq, tq), 0)
                    cc = lax.broadcasted_iota(jnp.int32, (tq, tq), 1)
                    pr = jnp.where(cc <= rr, pr, 0.0)
                dss.append((pr * dps[e]).astype(BF16))
                prs.append(pr.astype(BF16))
            new = []
            for e in range(2):
                dk, dv = carry[e]
                dv = dv + _dot_tn(prs[e], dobs[e])
                dk = dk + _dot_tn(dss[e], qbs[e])
                dq_ref[e, pl.ds(ioff, tq), :] += _dot(dss[e], kbs[e])
                new.append((dk, dv))
            return tuple(new)

        zero = jnp.zeros((tq, LANES), F32)
        carry = step(j, ((zero, zero), (zero, zero)), True)
        carry = lax.fori_loop(j + 1, nq, functools.partial(step, masked=False), carry)
        for e in range(2):
            dk, dv = carry[e]
            col = dk[:, L_NF:L_NF + 1]
            hi = col.astype(BF16).astype(F32)
            dk_ref[e] = jnp.where(lane == L_NF, hi, jnp.where(lane == L_NF + 1, col - hi, dk)).astype(BF16)
            dv_ref[e] = dv.astype(BF16)

        @pl.when(j == nq - 1)
        def _():
            lane_s = _lane((s, LANES))
            for e in range(2):
                dq = dq_ref[e]
                col = dq[:, L_F:L_F + 1]
                hi = col.astype(BF16).astype(F32)
                dqo_ref[e] = jnp.where(lane_s == L_F, hi, jnp.where(lane_s == L_F + 1, col - hi, dq)).astype(BF16)
        if na:
            @pl.when((pl.program_id(0) == NP - 1) & (j == nq - 1))
            def _():
                for cp in _chip_exchange_copies(cs_refs, r_refs, send_sems, recv_sems):
                    cp.wait()

    res = pl.BlockSpec((2, s, LANES), lambda p, j: (p, 0, 0))
    tile = pl.BlockSpec((2, tq, LANES), lambda p, j: (p, j, 0))
    shp = jax.ShapeDtypeStruct((H, s, LANES), BF16)
    outs = pl.pallas_call(
        body, name=name, grid=(NP, nq),
        in_specs=[res, tile, tile, res] + [HBM_SPEC] * na, out_specs=[res, tile, tile] + [HBM_SPEC] * na,
        out_shape=[shp, shp, shp] + [jax.ShapeDtypeStruct((3,) + cs.shape[1:], cs.dtype) for cs in css],
        scratch_shapes=[pltpu.VMEM((2, s, LANES), F32)] + ([pltpu.SemaphoreType.DMA((3 * na,))] * 2 if na else []),
        compiler_params=_cparams(("arbitrary", "arbitrary")),
    )(ql, ka, va, doa, *css)
    return outs[0], outs[1], outs[2], list(outs[3:])


def _gate_out(att, proj, w, xin, g, ts, name):
    s = att.shape[0]

    def body(a_ref, o_ref, w_ref, x_ref, g_ref, xo_ref, y_ref, gt_ref):
        gated = (a_ref[...] * _sigmoid(o_ref[...])).astype(BF16)
        gt_ref[...] = gated
        y = _dot(gated, w_ref[...])
        xo_ref[...] = x_ref[...] + g_ref[...] * y
        y_ref[...] = y.astype(BF16)

    row = pl.BlockSpec((ts, D), lambda i: (i, 0))
    return pl.pallas_call(
        body, name=name, grid=(s // ts,),
        in_specs=[row, pl.BlockSpec((ts, D), lambda i: (i, 3)), pl.BlockSpec((D, D), lambda i: (0, 0)), row,
                  pl.BlockSpec((1, D), lambda i: (0, 0))],
        out_specs=[row, row, row],
        out_shape=[jax.ShapeDtypeStruct((s, D), F32), jax.ShapeDtypeStruct((s, D), BF16),
                   jax.ShapeDtypeStruct((s, D), BF16)],
        compiler_params=_cparams(("arbitrary",)),
    )(att, proj, w, xin, g)


def _sibling_copies(g_refs, o_refs, ssems, rsems):
    x, y, c = _mesh_pos()
    cps = []
    for a, (g_ref, o_ref) in enumerate(zip(g_refs, o_refs)):
        rh = g_ref.shape[1] // 2
        cps.append(_remote(g_ref.at[:, pl.ds((1 - c) * rh, rh), :], o_ref, ssems.at[a], rsems.at[a], (x, y, 1 - c)))
    return cps


def _attn_bwd_prep(dy, w_out, att, proj, ts, name, glist=()):
    s = att.shape[0]
    na = len(glist)

    def body(*refs):
        dy_ref, w_ref, a_ref, o_ref = refs[:4]
        g_refs = refs[4:4 + na]
        doa_ref, dop_ref = refs[4 + na:6 + na]
        sib_refs, sems = refs[6 + na:6 + 2 * na], refs[6 + 2 * na:]
        if na:
            @pl.when(pl.program_id(0) == 0)
            def _():
                for cp in _sibling_copies(g_refs, sib_refs, *sems):
                    cp.start()
        lane = _lane((ts, LANES))
        bones = _head_block_ones()
        dgv = _dot_nt(dy_ref[...], w_ref[...])
        for p in range(NP):
            sl = slice(p * LANES, (p + 1) * LANES)
            dg, a = dgv[:, sl], a_ref[:, sl]
            sig = _sigmoid(o_ref[:, sl])
            datt = dg * sig
            dop_ref[:, sl] = (dg * a * sig * (1.0 - sig)).astype(BF16)
            delta = _exact_dot_r(datt * a, bones)
            for e in range(2):
                de, dl = (datt, delta) if e == 0 else (pltpu.roll(datt, DH, axis=1), pltpu.roll(delta, DH, axis=1))
                h0, h1, h2 = _split3(-dl[:, 0:1])
                aug = jnp.where(lane == L_F, h0.astype(F32),
                                jnp.where(lane == L_F + 1, h1.astype(F32),
                                          jnp.where(lane == L_F + 2, h2.astype(F32), 0.0)))
                doa_ref[2 * p + e] = jnp.where(lane < DH, de, aug).astype(BF16)
        if na:
            @pl.when(pl.program_id(0) == s // ts - 1)
            def _():
                for cp in _sibling_copies(g_refs, sib_refs, *sems):
                    cp.wait()

    row = pl.BlockSpec((ts, D), lambda i: (i, 0))
    outs = pl.pallas_call(
        body, name=name, grid=(s // ts,),
        in_specs=[row, pl.BlockSpec((D, D), lambda i: (0, 0)), row, pl.BlockSpec((ts, D), lambda i: (i, 3))]
        + [HBM_SPEC] * na,
        out_specs=[pl.BlockSpec((H, ts, LANES), lambda i: (0, i, 0)), row] + [HBM_SPEC] * na,
        out_shape=[jax.ShapeDtypeStruct((H, s, LANES), BF16), jax.ShapeDtypeStruct((s, D), BF16)]
        + [jax.ShapeDtypeStruct((g_.shape[0], g_.shape[1] // 2, g_.shape[2]), g_.dtype) for g_ in glist],
        scratch_shapes=[pltpu.SemaphoreType.DMA((na,))] * 2 if na else [],
        compiler_params=_cparams(("arbitrary",)),
    )(dy, w_out, att, proj, *glist)
    return outs[0], outs[1], list(outs[2:])


def _fox_post_bwd(proj, dqa, dka, dva, dop, qg2, kg2, bf, ts, name):
    s = proj.shape[0]
    nt = s // ts

    def body(p_ref, dq_ref, dk_ref, dv_ref, dop_ref, qg_ref, kg_ref, bf_ref, o_ref, red_ref, carry):
        @pl.when(pl.program_id(0) == 0)
        def _():
            carry[...] = jnp.zeros_like(carry)
            red_ref[...] = jnp.zeros_like(red_ref)
        lane = _lane((ts, LANES))
        bones = _head_block_ones()
        d_f = jnp.zeros((ts, LANES), F32)
        dqg = jnp.zeros((1, LANES), F32)
        dkg = jnp.zeros((1, LANES), F32)
        for p in range(NP):
            heads = [[ref[2 * p + e].astype(F32) for e in range(2)] for ref in (dq_ref, dk_ref, dv_ref)]
            pair = [jnp.where(lane < DH, a, pltpu.roll(b, DH, axis=1)) for a, b in heads]
            for e in range(2):
                dqe, dke = heads[0][e], heads[1][e]
                col = (dqe[:, L_F:L_F + 1] + dqe[:, L_F + 1:L_F + 2]
                       - dke[:, L_NF:L_NF + 1] - dke[:, L_NF + 1:L_NF + 2])
                d_f = jnp.where(lane == 2 * p + e, col, d_f)
            for idx, (g_ref, base) in enumerate(((qg_ref, 0), (kg_ref, D))):
                x = p_ref[:, base + p * LANES:base + (p + 1) * LANES]
                _, xh, r = _pair_norm(x, g_ref[...], bones)
                dn = pair[idx] * (SCALE if idx == 0 else 1.0 / LOG2E)
                t = dn * g_ref[...]
                mean_txh = _exact_dot_r(t * xh, bones) * (1.0 / DH)
                dx = r * (t - xh * mean_txh)
                o_ref[:, base + p * LANES:base + (p + 1) * LANES] = dx.astype(BF16)
                gsum = jnp.sum(dn * xh, axis=0, keepdims=True)
                if idx == 0:
                    dqg = dqg + gsum
                else:
                    dkg = dkg + gsum
            o_ref[:, 2 * D + p * LANES:2 * D + (p + 1) * LANES] = pair[2].astype(BF16)
        o_ref[:, 3 * D:4 * D] = dop_ref[...]
        rr = lax.broadcasted_iota(jnp.int32, (ts, ts), 0)
        cc = lax.broadcasted_iota(jnp.int32, (ts, ts), 1)
        utri = (cc >= rr).astype(BF16)
        dlogf = _exact_dot(utri, d_f) + carry[0:1, :]
        carry[0:1, :] = dlogf[0:1, :]
        xf = p_ref[:, 4 * D:4 * D + LANES] + bf_ref[...]
        dfl = jnp.where(lane < H, dlogf * _sigmoid(-xf), 0.0)
        o_ref[:, 4 * D:4 * D + LANES] = dfl.astype(BF16)
        red_ref[0:1, :] += dqg
        red_ref[1:2, :] += dkg
        red_ref[2:3, :] += jnp.sum(dfl, axis=0, keepdims=True)

    hs = pl.BlockSpec((H, ts, LANES), lambda i: (0, nt - 1 - i, 0))
    vec = pl.BlockSpec((1, LANES), lambda i: (0, 0))
    return pl.pallas_call(
        body, name=name, grid=(nt,),
        in_specs=[pl.BlockSpec((ts, FOX_NP), lambda i: (nt - 1 - i, 0)), hs, hs, hs,
                  pl.BlockSpec((ts, D), lambda i: (nt - 1 - i, 0)), vec, vec, vec],
        out_specs=[pl.BlockSpec((ts, FOX_NP), lambda i: (nt - 1 - i, 0)),
                   pl.BlockSpec((8, LANES), lambda i: (0, 0))],
        out_shape=[jax.ShapeDtypeStruct((s, FOX_NP), BF16), jax.ShapeDtypeStruct((8, LANES), F32)],
        scratch_shapes=[pltpu.VMEM((8, LANES), F32)],
        compiler_params=_cparams(("arbitrary",)),
    )(proj, dqa, dka, dva, dop, qg2, kg2, bf)


HALO = 16
TS = 512
TQ = 512
TR = 256
TP = 256
HPS_FWD = 4
TK_FWD = 512
TKW = 2048


def _shift_down(x, k):
    return pltpu.roll(x, k, axis=0)


def _shift_up(x, k):
    return pltpu.roll(x, x.shape[0] - k, axis=0)


def _conv_down(a, cw, cb, w, xin, gate, ts, name):
    s = a.shape[1]
    d = w.shape[1]
    hb = ts // HALO

    def body(prev_ref, a_ref, cw_ref, cb_ref, w_ref, x_ref, g_ref, o_ref, y_ref, f_ref, ap_ref):
        i = pl.program_id(0)
        acc = None
        for c in range(NGT):
            cols = slice(c * GT, (c + 1) * GT)
            both = lambda ref: jnp.concatenate([ref[0, :, cols].astype(F32), ref[1, :, cols].astype(F32)], axis=1)
            cwv, cbv = both(cw_ref), both(cb_ref)
            ext = jnp.concatenate([jnp.where(i > 0, both(prev_ref), 0.0), both(a_ref)], axis=0)
            ap = (_shift_down(ext, 2) * cwv[0:1, :] + _shift_down(ext, 1) * cwv[1:2, :]
                  + ext * cwv[2:3, :] + cbv)[HALO:, :]
            g, val = ap[:, :GT], ap[:, GT:]
            fch = (g * _sigmoid(g) * val).astype(BF16)
            f_ref[:, cols] = fch
            ap_ref[0, :, cols] = g.astype(BF16)
            ap_ref[1, :, cols] = val.astype(BF16)
            part = _dot(fch, w_ref[cols, :])
            acc = part if acc is None else acc + part
        y_ref[...] = acc.astype(BF16)
        o_ref[...] = x_ref[...] + g_ref[...] * acc

    row = pl.BlockSpec((ts, d), lambda i: (i, 0))
    planes = pl.BlockSpec((2, ts, DFF), lambda i: (0, i, 0))
    return pl.pallas_call(
        body, name=name, grid=(s // ts,),
        in_specs=[pl.BlockSpec((2, HALO, DFF), lambda i: (0, jnp.maximum(i * hb - 1, 0), 0)), planes,
                  pl.BlockSpec((2, 8, DFF), lambda i: (0, 0, 0)), pl.BlockSpec((2, 1, DFF), lambda i: (0, 0, 0)),
                  pl.BlockSpec((DFF, d), lambda i: (0, 0)), row, pl.BlockSpec((1, d), lambda i: (0, 0))],
        out_specs=[row, row, pl.BlockSpec((ts, DFF), lambda i: (i, 0)), planes],
        out_shape=[jax.ShapeDtypeStruct((s, d), F32), jax.ShapeDtypeStruct((s, d), BF16),
                   jax.ShapeDtypeStruct((s, DFF), BF16), jax.ShapeDtypeStruct((2, s, DFF), BF16)],
        compiler_params=_cparams(("arbitrary",)),
    )(a, a, cw, cb, w, xin, gate)


def _down_bwd_conv(dy, w, a, ap, cw, ts, name):
    s, d = dy.shape
    hb = ts // HALO
    nt = s // ts
    nhb = s // HALO

    def body(dy_ref, dyn_ref, w_ref, a_ref, ap_ref, apn_ref, cw_ref, da_ref, red_ref):
        i = pl.program_id(0)

        @pl.when(i == 0)
        def _():
            red_ref[...] = jnp.zeros_like(red_ref)
        dyn = jnp.where(i < nt - 1, dyn_ref[...], jnp.zeros_like(dyn_ref))
        dye = jnp.concatenate([dy_ref[...], dyn], axis=0)
        for c in range(NGT):
            cols = slice(c * GT, (c + 1) * GT)
            both = lambda ref: jnp.concatenate([ref[0, :, cols].astype(F32), ref[1, :, cols].astype(F32)], axis=1)
            cwv = both(cw_ref)
            dfe = _dot_nt(dye, w_ref[cols, :])
            apv = jnp.concatenate([both(ap_ref), both(apn_ref)], axis=0)
            g, val = apv[:, :GT], apv[:, GT:]
            sg = _sigmoid(g)
            dap = jnp.concatenate([dfe * val * (sg * (1.0 + g * (1.0 - sg))), dfe * (g * sg)], axis=1)
            shifted = [_shift_up(dap, 2)[:ts], _shift_up(dap, 1)[:ts], dap[:ts]]
            da = shifted[0] * cwv[0:1, :] + shifted[1] * cwv[1:2, :] + shifted[2] * cwv[2:3, :]
            av = both(a_ref)
            sums = [jnp.sum(av * t, axis=0, keepdims=True) for t in shifted]
            sums.append(jnp.sum(shifted[2], axis=0, keepdims=True))
            for e in range(2):
                half = slice(e * GT, (e + 1) * GT)
                da_ref[e, :, cols] = da[:, half].astype(BF16)
                for r, sm in enumerate(sums):
                    red_ref[e, r:r + 1, cols] += sm[:, half]

    planes = pl.BlockSpec((2, ts, DFF), lambda i: (0, i, 0))
    nxt = lambda i: jnp.minimum((i + 1) * hb, nhb - 1)
    return pl.pallas_call(
        body, name=name, grid=(nt,),
        in_specs=[pl.BlockSpec((ts, d), lambda i: (i, 0)), pl.BlockSpec((HALO, d), lambda i: (nxt(i), 0)),
                  pl.BlockSpec((DFF, d), lambda i: (0, 0)), planes, planes,
                  pl.BlockSpec((2, HALO, DFF), lambda i: (0, nxt(i), 0)),
                  pl.BlockSpec((2, 8, DFF), lambda i: (0, 0, 0))],
        out_specs=[planes, pl.BlockSpec((2, 8, DFF), lambda i: (0, 0, 0))],
        out_shape=[jax.ShapeDtypeStruct((2, s, DFF), BF16), jax.ShapeDtypeStruct((2, 8, DFF), F32)],
        compiler_params=_cparams(("arbitrary",)),
    )(dy, dy, w, a, ap, ap, cw)


def _chunk_mask(transposed=False):
    t = lax.broadcasted_iota(jnp.int32, (SGB, SGB), 0) // CHUNK
    u = lax.broadcasted_iota(jnp.int32, (SGB, SGB), 1) // CHUNK
    return (t <= u) if transposed else (u <= t)


def _sgu_ln(v, gain, bias):
    mu = jnp.mean(v, axis=-1, keepdims=True)
    vc = v - mu
    rstd = lax.rsqrt(jnp.mean(vc * vc, axis=-1, keepdims=True) + EPS)
    vhat = vc * rstd
    return vhat * gain + bias, vhat, rstd


def _sgu_fwd(z, vgain, vbias, ws, bst, w_out, xin, gate, tr, name):
    s = z.shape[0]

    def body(zu_ref, zv_ref, vg_ref, vb_ref, ws_ref, bs_ref, wo_ref, x_ref, gt_ref, xo_ref, yo_ref, y_ref):
        u = _gelu(zu_ref[...].astype(F32))
        vn, _, _ = _sgu_ln(_gelu(zv_ref[...].astype(F32)), vg_ref[...], vb_ref[...])
        vn = vn.astype(BF16)
        mask = _chunk_mask()
        for g in range(SGG):
            w = jnp.where(mask, ws_ref[g], 0.0).astype(BF16)
            for b in range(tr // SGB):
                rs, cs = slice(b * SGB, (b + 1) * SGB), slice(g * SGC, (g + 1) * SGC)
                mixed = _dot(w, vn[rs, cs]) + bs_ref[:, g:g + 1]
                y_ref[rs, cs] = (u[rs, cs] * mixed).astype(BF16)
        yo = _dot(y_ref[...], wo_ref[...])
        xo_ref[...] = x_ref[...] + gt_ref[...] * yo
        yo_ref[...] = yo.astype(BF16)

    vec = pl.BlockSpec((1, SGW), lambda i: (0, 0))
    row = pl.BlockSpec((tr, D), lambda i: (i, 0))
    return pl.pallas_call(
        body, name=name, grid=(s // tr,),
        in_specs=[pl.BlockSpec((tr, SGW), lambda i: (i, 0)), pl.BlockSpec((tr, SGW), lambda i: (i, 1)),
                  vec, vec, pl.BlockSpec((SGG, SGB, SGB), lambda i: (0, 0, 0)),
                  pl.BlockSpec((SGB, LANES), lambda i: (0, 0)), pl.BlockSpec((SGW, D), lambda i: (0, 0)), row,
                  pl.BlockSpec((1, D), lambda i: (0, 0))],
        out_specs=[row, row, pl.BlockSpec((tr, SGW), lambda i: (i, 0))],
        out_shape=[jax.ShapeDtypeStruct((s, D), F32), jax.ShapeDtypeStruct((s, D), BF16),
                   jax.ShapeDtypeStruct((s, SGW), BF16)],
        compiler_params=_cparams(("arbitrary",)),
    )(z, z, vgain, vbias, ws, bst, w_out, xin, gate)


def _sgu_bwd(z, dy, vgain, vbias, ws, wst, bst, tr, name):
    s = z.shape[0]

    def body(zu_ref, zv_ref, dy_ref, vg_ref, vb_ref, ws_ref, wst_ref, bs_ref,
             dz_ref, rb_ref, rv_ref, dws_ref, dbs_ref, dvn_s):
        @pl.when(pl.program_id(0) == 0)
        def _():
            rb_ref[...] = jnp.zeros_like(rb_ref)
            rv_ref[...] = jnp.zeros_like(rv_ref)
            dws_ref[...] = jnp.zeros_like(dws_ref)
            dbs_ref[...] = jnp.zeros_like(dbs_ref)
        zu = zu_ref[...].astype(F32)
        zv = zv_ref[...].astype(F32)
        u, gu = _gelu_and_grad(zu)
        v, gv = _gelu_and_grad(zv)
        vn, vhat, rstd = _sgu_ln(v, vg_ref[...], vb_ref[...])
        vnb = vn.astype(BF16)
        dyv = dy_ref[...].astype(F32)
        dmix = (dyv * u).astype(BF16)
        mask = _chunk_mask()
        mask_t = _chunk_mask(transposed=True)
        lane = _lane((SGB, LANES))
        dbs = jnp.zeros((SGB, LANES), F32)
        for g in range(SGG):
            w = jnp.where(mask, ws_ref[g], 0.0).astype(BF16)
            wt = jnp.where(mask_t, wst_ref[g], 0.0).astype(BF16)
            dw = jnp.zeros((SGB, SGB), F32)
            for b in range(tr // SGB):
                rs, cs = slice(b * SGB, (b + 1) * SGB), slice(g * SGC, (g + 1) * SGC)
                mixed = _dot(w, vnb[rs, cs]) + bs_ref[:, g:g + 1]
                dz_ref[rs, cs] = (dyv[rs, cs] * mixed * gu[rs, cs]).astype(BF16)
                dm = dmix[rs, cs]
                dw = dw + _dot_nt(dm, vnb[rs, cs])
                dbs = dbs + jnp.where(lane == g, jnp.sum(dm.astype(F32), axis=-1, keepdims=True), 0.0)
                dvn_s[rs, cs] = _dot(wt, dm)
            dws_ref[g] += jnp.where(mask, dw, 0.0)
        dbs_ref[...] += dbs
        dvn = dvn_s[...]
        rv_ref[0:1, :] += jnp.sum(dvn * vhat, axis=0, keepdims=True)
        rv_ref[1:2, :] += jnp.sum(dvn, axis=0, keepdims=True)
        dvh = dvn * vg_ref[...]
        dv = rstd * (dvh - jnp.mean(dvh, axis=-1, keepdims=True)
                     - vhat * jnp.mean(dvh * vhat, axis=-1, keepdims=True))
        dz_ref[:, SGW:] = (dv * gv).astype(BF16)
        dzf = dz_ref[...].astype(F32)
        rb_ref[0:1, :] += jnp.sum(dzf, axis=0, keepdims=True)

    vec = pl.BlockSpec((1, SGW), lambda i: (0, 0))
    wsp = pl.BlockSpec((SGG, SGB, SGB), lambda i: (0, 0, 0))
    return pl.pallas_call(
        body, name=name, grid=(s // tr,),
        in_specs=[pl.BlockSpec((tr, SGW), lambda i: (i, 0)), pl.BlockSpec((tr, SGW), lambda i: (i, 1)),
                  pl.BlockSpec((tr, SGW), lambda i: (i, 0)), vec, vec, wsp, wsp,
                  pl.BlockSpec((SGB, LANES), lambda i: (0, 0))],
        out_specs=[pl.BlockSpec((tr, 2 * SGW), lambda i: (i, 0)),
                   pl.BlockSpec((8, 2 * SGW), lambda i: (0, 0)),
                   pl.BlockSpec((8, SGW), lambda i: (0, 0)), wsp,
                   pl.BlockSpec((SGB, LANES), lambda i: (0, 0))],
        out_shape=[jax.ShapeDtypeStruct((s, 2 * SGW), BF16), jax.ShapeDtypeStruct((8, 2 * SGW), F32),
                   jax.ShapeDtypeStruct((8, SGW), F32), jax.ShapeDtypeStruct((SGG, SGB, SGB), F32),
                   jax.ShapeDtypeStruct((SGB, LANES), F32)],
        scratch_shapes=[pltpu.VMEM((tr, SGW), F32)],
        compiler_params=_cparams(("arbitrary",)),
    )(z, z, dy, vgain, vbias, ws, wst, bst)


def _final_loss(x, fg, tgt, gprev, yprev, ts, name):
    s, d = x.shape

    def body(x_ref, fg_ref, t_ref, g_ref, y_ref, l_ref, dx_ref, dy_ref, red_ref):
        @pl.when(pl.program_id(0) == 0)
        def _():
            l_ref[...] = jnp.zeros_like(l_ref)
            red_ref[...] = jnp.zeros_like(red_ref)
        xv = x_ref[...]
        r = _rstd_rows(xv)
        xh = xv * r
        err = xh * fg_ref[...] - t_ref[...]
        l_ref[...] += 0.5 * jnp.sum(jnp.mean(err * err, axis=-1, keepdims=True))
        dyo = err * (1.0 / d)
        dxh = dyo * fg_ref[...]
        dx = r * (dxh - xh * jnp.mean(dxh * xh, axis=-1, keepdims=True))
        dx_ref[...] = dx
        dy_ref[...] = (dx * g_ref[...]).astype(BF16)
        red_ref[0:1, :] += jnp.sum(dyo * xh, axis=0, keepdims=True)
        red_ref[1:2, :] += jnp.sum(dx * y_ref[...].astype(F32), axis=0, keepdims=True)

    row = pl.BlockSpec((ts, d), lambda i: (i, 0))
    vec = pl.BlockSpec((1, d), lambda i: (0, 0))
    return pl.pallas_call(
        body, name=name, grid=(s // ts,),
        in_specs=[row, vec, row, vec, row],
        out_specs=[pl.BlockSpec((8, LANES), lambda i: (0, 0)), row, row, pl.BlockSpec((8, d), lambda i: (0, 0))],
        out_shape=[jax.ShapeDtypeStruct((8, LANES), F32), jax.ShapeDtypeStruct((s, d), F32),
                   jax.ShapeDtypeStruct((s, d), BF16), jax.ShapeDtypeStruct((8, d), F32)],
        compiler_params=_cparams(("arbitrary",)),
    )(x, fg, tgt, gprev, yprev)


def _norm_bwd(xin, dh, dxout, ng, sc, gprev, yprev, ts, name, css=()):
    s, d = xin.shape
    has_prev = gprev is not None
    fused = isinstance(dh, tuple)
    na = len(css)
    if fused:
        a, w, tk = dh
        ka, kw = a.shape[-1], w.shape[-1]
        k = ka * (a.shape[0] if a.ndim == 3 else 1)

    def body(*refs):
        if fused:
            x_ref, a_ref, w_ref, dxo_ref, ng_ref, sc_ref = refs[:6]
            rest = refs[6:]
        else:
            x_ref, dh_ref, dxo_ref, ng_ref, sc_ref = refs[:5]
            rest = refs[5:]
        if has_prev:
            g_ref, y_ref = rest[:2]
            rest = rest[2:]
        cs_refs, rest = rest[:na], rest[na:]
        if has_prev:
            dx_ref, dy_ref, red_ref = rest[:3]
            rest = rest[3:]
        else:
            dx_ref, red_ref = rest[:2]
            rest = rest[2:]
        r_refs, sems = rest[:na], rest[na:]
        if na:
            @pl.when(pl.program_id(0) == 0)
            def _():
                for cp in _chip_exchange_copies(cs_refs, r_refs, *sems):
                    cp.start()

        @pl.when(pl.program_id(0) == 0)
        def _():
            red_ref[...] = jnp.zeros_like(red_ref)
        if fused:
            dhv = None
            for g0 in range(0, k, tk):
                av = a_ref[g0 // ka, :, g0 % ka:g0 % ka + tk] if a.ndim == 3 else a_ref[:, g0:g0 + tk]
                wv = w_ref[g0 // kw, :, g0 % kw:g0 % kw + tk] if w.ndim == 3 else w_ref[:, g0:g0 + tk]
                part = _dot_nt(av, wv)
                dhv = part if dhv is None else dhv + part
        else:
            dhv = dh_ref[...]
        xv = x_ref[...]
        r = _rstd_rows(xv)
        xh = xv * r
        dr = dhv * (1.0 + sc_ref[...])
        t = dr * ng_ref[...]
        dx = dxo_ref[...] + r * (t - xh * jnp.mean(t * xh, axis=-1, keepdims=True))
        dx_ref[...] = dx
        red_ref[0:1, :] += jnp.sum(dhv, axis=0, keepdims=True)
        red_ref[1:2, :] += jnp.sum(dhv * (xh * ng_ref[...]), axis=0, keepdims=True)
        red_ref[2:3, :] += jnp.sum(dr * xh, axis=0, keepdims=True)
        if has_prev:
            dy_ref[...] = (dx * g_ref[...]).astype(BF16)
            red_ref[3:4, :] += jnp.sum(dx * y_ref[...].astype(F32), axis=0, keepdims=True)
        if na:
            @pl.when(pl.program_id(0) == s // ts - 1)
            def _():
                for cp in _chip_exchange_copies(cs_refs, r_refs, *sems):
                    cp.wait()

    row = pl.BlockSpec((ts, d), lambda i: (i, 0))
    vec = pl.BlockSpec((1, d), lambda i: (0, 0))
    red = pl.BlockSpec((8, d), lambda i: (0, 0))
    if fused:
        a_spec = (pl.BlockSpec((a.shape[0], ts, ka), lambda i: (0, i, 0)) if a.ndim == 3
                  else pl.BlockSpec((ts, ka), lambda i: (i, 0)))
        w_spec = pl.BlockSpec(w.shape, (lambda i: (0, 0, 0)) if w.ndim == 3 else (lambda i: (0, 0)))
        dh_specs, dh_args = [a_spec, w_spec], (a, w)
    else:
        dh_specs, dh_args = [row], (dh,)
    if has_prev:
        in_specs, args = [row] + dh_specs + [row, vec, vec, vec, row], (xin,) + dh_args + (dxout, ng, sc, gprev, yprev)
        out_specs = [row, row, red]
        out_shape = [jax.ShapeDtypeStruct((s, d), F32), jax.ShapeDtypeStruct((s, d), BF16),
                     jax.ShapeDtypeStruct((8, d), F32)]
    else:
        in_specs, args = [row] + dh_specs + [row, vec, vec], (xin,) + dh_args + (dxout, ng, sc)
        out_specs = [row, red]
        out_shape = [jax.ShapeDtypeStruct((s, d), F32), jax.ShapeDtypeStruct((8, d), F32)]
    return pl.pallas_call(
        body, name=name, grid=(s // ts,), in_specs=in_specs + [HBM_SPEC] * na,
        out_specs=out_specs + [HBM_SPEC] * na,
        out_shape=out_shape + [jax.ShapeDtypeStruct((3,) + cs.shape[1:], cs.dtype) for cs in css],
        scratch_shapes=[pltpu.SemaphoreType.DMA((3 * na,))] * 2 if na else [],
        compiler_params=_cparams(("arbitrary",)),
    )(*args, *css)


def _ada_mod(c_all, ada_w, ada_b):
    nb = c_all.shape[0]
    da = ada_w.shape[2]

    def body(c_ref, w_ref, b_ref, o_ref, ca_ref):
        cv = c_ref[...]
        ca = cv * _sigmoid(cv)
        ca_ref[...] = ca
        o_ref[0] = lax.dot_general(ca, w_ref[0], (((1,), (0,)), ((), ())), precision=lax.Precision.HIGHEST,
                                   preferred_element_type=F32) + b_ref[0]

    return pl.pallas_call(
        body, name="ada_mod", grid=(2,),
        in_specs=[pl.BlockSpec((nb, D), lambda i: (0, 0)), pl.BlockSpec((1, D, da), lambda i: (i, 0, 0)),
                  pl.BlockSpec((1, 1, da), lambda i: (i, 0, 0))],
        out_specs=[pl.BlockSpec((1, nb, da), lambda i: (i, 0, 0)), pl.BlockSpec((nb, D), lambda i: (0, 0))],
        out_shape=[jax.ShapeDtypeStruct((2, nb, da), F32), jax.ShapeDtypeStruct((nb, D), F32)],
        compiler_params=_cparams(("arbitrary",)),
    )(c_all, ada_w, ada_b)


def _ada_w_grad(c_act_t, dmod):
    nb = c_act_t.shape[1]
    da = dmod.shape[2]
    tn = 512

    def body(c_ref, d_ref, o_ref):
        acc = c_ref[:, 0:1] * d_ref[0, 0:1, :]
        for b in range(1, nb):
            acc = acc + c_ref[:, b:b + 1] * d_ref[0, b:b + 1, :]
        o_ref[0] = acc

    return pl.pallas_call(
        body, name="ada_w_grad", grid=(2, da // tn),
        in_specs=[pl.BlockSpec((D, nb), lambda i, j: (0, 0)), pl.BlockSpec((1, nb, tn), lambda i, j: (i, 0, j))],
        out_specs=pl.BlockSpec((1, D, tn), lambda i, j: (i, 0, j)),
        out_shape=jax.ShapeDtypeStruct((2, D, da), F32),
        compiler_params=_cparams(("arbitrary", "arbitrary")),
    )(c_act_t, dmod)


def _conv_planes(cw, cb):
    cwp = jnp.swapaxes(cw.reshape(3, 2, DFF), 0, 1)
    return jnp.pad(cwp, ((0, 0), (0, 5), (0, 0))), cb.reshape(2, 1, DFF)


def _local_step(x, tgt, mod, wts, small, comm=None):
    wts = dict(wts)
    s = x.shape[0]
    ts, tq, tr, tp = TS, TQ, TR, TP
    tkw = min(TKW, s)
    tf = min(256, s)
    zb = lambda n: jnp.zeros((1, n), F32)
    m6 = mod.reshape(2, 6, 1, D)
    sh1, sc1, g1, sh2, sc2, g2 = ([m6[i, k] for i in range(2)] for k in range(6))
    n1g, n2g = small["norm1_g"], small["norm2_g"]
    row = lambda a, i: a[i:i + 1]

    qg2 = jnp.tile(small["fox_q_gain"], (1, 2))
    kg2 = jnp.tile(small["fox_k_gain"], (1, 2))
    bfp = jnp.pad(small["fox_b_f"], ((0, 0), (0, LANES - H)))
    proj, h1 = _norm_mod_matmul(x, row(n1g, 0), sc1[0], sh1[0], wts["fox_w_in"], zb(FOX_NP), F32, ts, 1408, "fox_in")
    qa, ka, va = _fox_post(proj, qg2, kg2, bfp, tp, "fox_post")
    att, ql, gathered = _attn_fwd(qa, ka, va, tq, "attn_fwd", shards=comm["shards"] if comm else ())
    if comm:
        wts.update(comm["make_wts"](gathered))
    x1, y0, gated = _gate_out(att, proj, wts["fox_w_out"], x, g1[0], ts, "fox_gate_out")

    def ffn_fwd(xin, i, tag):
        cw, cb = _conv_planes(small["ffn_conv_w"][i], small["ffn_conv_b"][i])
        a, h = _norm_mod_matmul(xin, row(n2g, i), sc2[i], sh2[i], wts["ffn_w_up"][i], zb(2 * DFF), BF16, ts, 1408,
                                "ffn_up" + tag, planes=2)
        xo, y, f, ap = _conv_down(a, cw, cb, wts["ffn_w_down"][i], xin, g2[i], min(256, s), "ffn_conv_down" + tag)
        return xo, (a, h, f, y, cw, ap)

    x2, ffn0 = ffn_fwd(x1, 0, "0")

    bst = jnp.pad(small["sgu_b_s"].T, ((0, 0), (0, LANES - SGG)))
    ws = small["sgu_w_s"]
    z, h3 = _norm_mod_matmul(x2, row(n1g, 1), sc1[1], sh1[1], wts["sgu_w_in"], small["sgu_b_in"], BF16, ts, 1024,
                             "sgu_in")
    x3, y1, yy = _sgu_fwd(z, small["sgu_v_gain"], small["sgu_v_bias"], ws, bst, wts["sgu_w_out"], x2, g1[1], tr,
                          "sgu_mix_out")
    x4, ffn1 = ffn_fwd(x3, 1, "1")

    lsum, dx4, dy, redf = _final_loss(x4, small["final_g"], tgt, g2[1], ffn1[3], ts, "final_loss")
    grads = {"final_g": redf[0]}
    dmod = [[None] * 6, [None] * 6]
    dmod[1][5] = redf[1]

    def ffn_bwd(dxo, dy2, xin, i, saved, gprev, yprev, tag):
        a, h, f, _, cw, ap = saved
        wd, wu = wts["ffn_w_down"][i], wts["ffn_w_up"][i]
        g_wd = _matmul(f, dy2, True, False, 1408, D, tkw, BF16, "ffn_dwdown" + tag)
        da, redc = _down_bwd_conv(dy2, wd, a, ap, cw, min(512, s), "ffn_down_bwd_conv" + tag)
        g_wu = _matmul(h, da, True, False, D, 1408, tkw, BF16, "ffn_dwup" + tag, out_parts=N_CHIP)
        outs = _norm_bwd(xin, (da, wu, 1408), dxo, row(n2g, i), sc2[i], gprev, yprev, tf, "ffn_dh_norm_bwd" + tag)
        return outs, g_wd, g_wu, redc

    (dx3, dy1, red), g_wd1, g_wu1, redc1 = ffn_bwd(dx4, dy, x3, 1, ffn1, g1[1], y1, "1")
    dmod[1][3], dmod[1][4], dn2g1, dmod[1][2] = red[0], red[1], red[2], red[3]

    g_swo = _matmul(yy, dy1, True, False, 1024, D, tkw, BF16, "sgu_dwout")
    dyy = _matmul_wt(dy1, wts["sgu_w_out"], 1024, D, BF16, ts, "sgu_dyy")
    wst = jnp.swapaxes(ws, 1, 2)
    dz, rb, rv, dws, dbst = _sgu_bwd(z, dyy, small["sgu_v_gain"], small["sgu_v_bias"], ws, wst, bst, tr, "sgu_mix_bwd")
    g_swi = _matmul(h3, dz, True, False, D, 1024, tkw, BF16, "sgu_dwin", out_parts=N_CHIP)
    dx2, dy2_0, red = _norm_bwd(x2, (dz, wts["sgu_w_in"], 1024), dx3, row(n1g, 1), sc1[1], g2[0], ffn0[3], ts,
                                "sgu_dh_norm_bwd")
    dmod[1][0], dmod[1][1], dn1g1, dmod[0][5] = red[0], red[1], red[2], red[3]

    (dx1, dy0, red), g_wd0, g_wu0, redc0 = ffn_bwd(dx2, dy2_0, x1, 0, ffn0, g1[0], y0, "0")
    dmod[0][3], dmod[0][4], dn2g0, dmod[0][2] = red[0], red[1], red[2], red[3]

    g_fwo = _matmul(gated, dy0, True, False, D, D, tkw, BF16, "fox_dwout")
    glist = comm["rs_lists"]([g_fwo, g_swi, g_swo, g_wu0, g_wu1, g_wd0, g_wd1]) if comm else []
    doa, dop, sibs = _attn_bwd_prep(dy0, wts["fox_w_out"], att, proj, ts, "attn_bwd_prep", glist=glist)
    css = comm["rs_chip_sums"](glist, sibs, "") if comm else []
    dqa, dka, dva, rcvs = _attn_bwd(ql, ka, va, doa, tq, "attn_bwd", css=css)
    dproj, redx = _fox_post_bwd(proj, dqa, dka, dva, dop, qg2, kg2, bfp, tp, "fox_post_bwd")
    g_fwi = _matmul(h1, dproj, True, False, D, 1408, tkw, BF16, "fox_dwin")
    css_fox = comm["rs_prepare_fox"](g_fwi) if comm else []
    outs = _norm_bwd(x, (dproj, wts["fox_w_in"], 1408), dx1, row(n1g, 0), sc1[0], None, None, ts,
                     "fox_dh_norm_bwd", css=css_fox)
    dx0, red = outs[0], outs[1]
    css, rcvs = list(css_fox) + list(css), list(outs[2:]) + list(rcvs)
    dmod[0][0], dmod[0][1], dn1g0 = red[0], red[1], red[2]

    grads.update(
        fox_w_in=g_fwi, fox_w_out=g_fwo, sgu_w_in=g_swi, sgu_w_out=g_swo,
        ffn_w_up=[g_wu0, g_wu1], ffn_w_down=[g_wd0, g_wd1],
        fox_q_gain=redx[0, :DH] + redx[0, DH:], fox_k_gain=redx[1, :DH] + redx[1, DH:], fox_b_f=redx[2, :H],
        sgu_b_in=rb[0], sgu_v_gain=rv[0], sgu_v_bias=rv[1], sgu_w_s=dws, sgu_b_s=dbst[:, :SGG].T,
        ffn_conv_w=jnp.stack([jnp.swapaxes(r[:, 0:3], 0, 1).reshape(3, 2 * DFF) for r in (redc0, redc1)]),
        ffn_conv_b=jnp.stack([r[:, 3].reshape(2 * DFF) for r in (redc0, redc1)]),
        norm1_g=jnp.stack([dn1g0, dn1g1]), norm2_g=jnp.stack([dn2g0, dn2g1]),
    )
    dmod_arr = jnp.stack([jnp.concatenate(dmod[0]), jnp.concatenate(dmod[1])])
    return lsum[0, 0], dx0, grads, dmod_arr, (css, rcvs)


N_DEV = 8
N_CHIP = 4
HBM_SPEC = pl.BlockSpec(memory_space=pltpu.HBM)
VMEM_SPEC = pl.BlockSpec(memory_space=pltpu.VMEM)


def _mesh_pos():
    return lax.axis_index("x"), lax.axis_index("y"), lax.axis_index("c")


def _other_chips(x, y):
    return [(1 - x, y), (x, 1 - y), (1 - x, 1 - y)]


def _remote(src, dst, ssem, rsem, dev):
    return pltpu.make_async_remote_copy(src_ref=src, dst_ref=dst, send_sem=ssem, recv_sem=rsem,
                                        device_id=dev, device_id_type=MESH)


def _allgather8(xb, name):
    m_per, n = xb.shape

    def body(x_ref, out_ref, send_sems, recv_sems, local_sem):
        x, y, c = _mesh_pos()
        me, sibling = (x, y, c), (x, y, 1 - c)
        chips = _other_chips(x, y)

        def rows(px, py, pc):
            return out_ref.at[pl.ds((4 * px + 2 * py + pc) * m_per, m_per), :]

        def copy(k, block, to, src=None):
            return _remote(rows(*block) if src is None else src, rows(*block),
                           send_sems.at[k], recv_sems.at[k], to)

        mine = pltpu.make_async_copy(x_ref, rows(*me), local_sem)
        mine.start()
        first = [copy(0, me, sibling, src=x_ref)]
        first += [copy(1 + j, me, (*chip, c), src=x_ref) for j, chip in enumerate(chips)]
        for cp in first:
            cp.start()
        passed = [copy(4 + j, (*chip, c), sibling) for j, chip in enumerate(chips)]
        for j, chip in enumerate(chips):
            copy(1 + j, (*chip, c), me).wait_recv()
            passed[j].start()
        copy(0, sibling, me).wait_recv()
        for j, chip in enumerate(chips):
            copy(4 + j, (*chip, 1 - c), me).wait_recv()
        for cp in first + passed:
            cp.wait_send()
        mine.wait()

    return pl.pallas_call(
        body, name=name,
        out_shape=jax.ShapeDtypeStruct((N_DEV * m_per, n), xb.dtype),
        in_specs=[VMEM_SPEC], out_specs=VMEM_SPEC,
        scratch_shapes=[pltpu.SemaphoreType.DMA((7,)), pltpu.SemaphoreType.DMA((7,)), pltpu.SemaphoreType.DMA],
        compiler_params=pltpu.CompilerParams(vmem_limit_bytes=V7X_VMEM_LIMIT),
    )(xb)


def _gather_shards(shards, name):
    na = len(shards)

    def body(*refs):
        p_refs, o_refs = refs[:na], refs[na:2 * na]
        send_sems, recv_sems, pass_send, pass_recv = refs[2 * na:]
        x, y, c = _mesh_pos()
        me = 2 * x + y
        sibling = (x, y, 1 - c)
        chips = _other_chips(x, y)

        def half(a, ci, hf):
            rh = shards[a].shape[0] // 2
            return o_refs[a].at[ci, pl.ds(hf * rh, rh), :]

        sends = []
        for a in range(na):
            rh = shards[a].shape[0] // 2
            for k, chip in enumerate(chips):
                sends.append(_remote(p_refs[a].at[pl.ds(c * rh, rh), :], half(a, me, c),
                                     send_sems.at[3 * a + k], recv_sems.at[3 * a + k], (*chip, c)))
        for cp in sends:
            cp.start()
        passed = []
        for a in range(na):
            for k, chip in enumerate(chips):
                ci = 2 * chip[0] + chip[1]
                _remote(half(a, ci, c), half(a, ci, c), send_sems.at[3 * a + k], recv_sems.at[3 * a + k],
                        (*chip, c)).wait_recv()
                cp = _remote(half(a, ci, c), half(a, ci, c), pass_send.at[3 * a + k], pass_recv.at[3 * a + k], sibling)
                cp.start()
                passed.append(cp)
        for a in range(na):
            for k, chip in enumerate(chips):
                ci = 2 * chip[0] + chip[1]
                _remote(half(a, ci, 1 - c), half(a, ci, 1 - c), pass_send.at[3 * a + k], pass_recv.at[3 * a + k],
                        sibling).wait_recv()
        for cp in sends + passed:
            cp.wait_send()

    return pl.pallas_call(
        body, name=name,
        out_shape=[jax.ShapeDtypeStruct((N_CHIP,) + p.shape, p.dtype) for p in shards],
        in_specs=[HBM_SPEC] * na, out_specs=[HBM_SPEC] * na,
        scratch_shapes=[pltpu.SemaphoreType.DMA((3 * na,))] * 4,
    )(*shards)


def _rs_to_sibling(gs, name):
    na = len(gs)

    def body(*refs):
        cps = _sibling_copies(refs[:na], refs[na:2 * na], refs[2 * na], refs[2 * na + 1])
        for cp in cps:
            cp.start()
        for cp in cps:
            cp.wait()

    return pl.pallas_call(
        body, name=name,
        out_shape=[jax.ShapeDtypeStruct((g.shape[0], g.shape[1] // 2, g.shape[2]), g.dtype) for g in gs],
        in_specs=[HBM_SPEC] * na, out_specs=[HBM_SPEC] * na,
        scratch_shapes=[pltpu.SemaphoreType.DMA((na,)), pltpu.SemaphoreType.DMA((na,))],
    )(*gs)


def _rs_chip_sum(g, sib, c_arr, tr, name):
    nc, r, n = g.shape
    rh = r // 2
    g4 = g.reshape(nc, 2, rh, n)

    def body(c_ref, g_ref, s_ref, o_ref):
        o_ref[...] = (g_ref[0].astype(F32) + s_ref[...].astype(F32)).astype(BF16)

    return pl.pallas_call(
        body, name=name, out_shape=jax.ShapeDtypeStruct((nc, rh, n), BF16),
        grid_spec=pltpu.PrefetchScalarGridSpec(
            num_scalar_prefetch=1, grid=(nc, rh // tr),
            in_specs=[pl.BlockSpec((1, 1, tr, n), lambda j, i, cr: (j, cr[0], i, 0)),
                      pl.BlockSpec((1, tr, n), lambda j, i, cr: (j, i, 0))],
            out_specs=pl.BlockSpec((1, tr, n), lambda j, i, cr: (j, i, 0))),
        compiler_params=_cparams(("arbitrary", "arbitrary")),
    )(c_arr, g4, sib)


def _rs_final_sum(cs, rcv, me_arr, tr, name):
    nc, rh, n = cs.shape

    def body(m_ref, c_ref, r_ref, o_ref):
        acc = c_ref[0].astype(F32)
        for k in range(3):
            acc = acc + r_ref[k].astype(F32)
        o_ref[...] = acc

    return pl.pallas_call(
        body, name=name, out_shape=jax.ShapeDtypeStruct((rh, n), F32),
        grid_spec=pltpu.PrefetchScalarGridSpec(
            num_scalar_prefetch=1, grid=(rh // tr,),
            in_specs=[pl.BlockSpec((1, tr, n), lambda i, mr: (mr[0], i, 0)),
                      pl.BlockSpec((3, tr, n), lambda i, mr: (0, i, 0))],
            out_specs=pl.BlockSpec((tr, n), lambda i, mr: (i, 0))),
        compiler_params=_cparams(("arbitrary",)),
    )(me_arr, cs, rcv)


def _rs_swap_halves(halves, name):
    na = len(halves)

    def body(*refs):
        h_refs, o_refs, ssems, rsems = refs[:na], refs[na:2 * na], refs[2 * na], refs[2 * na + 1]
        x, y, c = _mesh_pos()
        cps = []
        for a in range(na):
            cp = _remote(h_refs[a], o_refs[a], ssems.at[a], rsems.at[a], (x, y, 1 - c))
            cp.start()
            cps.append(cp)
        for cp in cps:
            cp.wait()

    return pl.pallas_call(
        body, name=name, out_shape=[jax.ShapeDtypeStruct(h.shape, h.dtype) for h in halves],
        in_specs=[HBM_SPEC] * na, out_specs=[HBM_SPEC] * na,
        scratch_shapes=[pltpu.SemaphoreType.DMA((na,)), pltpu.SemaphoreType.DMA((na,))],
    )(*halves)


def _join_columns(parts, n_out, name):
    p, k, c = parts.shape
    tr = 128

    def body(w_ref, o_ref):
        for j in range(p):
            o_ref[:, j * c:(j + 1) * c] = w_ref[j]
        o_ref[:, p * c:] = jnp.zeros((tr, n_out - p * c), parts.dtype)

    return pl.pallas_call(
        body, name=name, grid=(k // tr,),
        in_specs=[pl.BlockSpec((p, tr, c), lambda i: (0, i, 0))],
        out_specs=pl.BlockSpec((tr, n_out), lambda i: (i, 0)),
        out_shape=jax.ShapeDtypeStruct((k, n_out), parts.dtype),
        compiler_params=_cparams(("arbitrary",)),
    )(parts)


def _split_columns(g, p, c, name):
    k, n = g.shape
    tr = 128

    def body(g_ref, o_ref):
        for j in range(p):
            o_ref[j] = g_ref[:, j * c:(j + 1) * c]

    return pl.pallas_call(
        body, name=name, grid=(k // tr,),
        in_specs=[pl.BlockSpec((tr, n), lambda i: (i, 0))],
        out_specs=pl.BlockSpec((p, tr, c), lambda i: (0, i, 0)),
        out_shape=jax.ShapeDtypeStruct((p, k, c), g.dtype),
        compiler_params=_cparams(("arbitrary",)),
    )(g)


def _sum8(g, name):
    nd, r, n = g.shape

    def body(g_ref, o_ref):
        acc = g_ref[0]
        for k in range(1, nd):
            acc = acc + g_ref[k]
        o_ref[...] = acc

    return pl.pallas_call(
        body, name=name, grid=(r // 8,),
        in_specs=[pl.BlockSpec((nd, 8, n), lambda i: (0, i, 0))],
        out_specs=pl.BlockSpec((8, n), lambda i: (i, 0)),
        out_shape=jax.ShapeDtypeStruct((r, n), F32),
        compiler_params=_cparams(("arbitrary",)),
    )(g)


def _adamw(w, g, m, v, name):
    r, n = w.shape
    tr = next(t for t in (128, 64, 32, 16, 8) if r % t == 0)
    bc1 = 1.0 - ADAM_B1 ** ADAM_STEP
    bc2 = 1.0 - ADAM_B2 ** ADAM_STEP

    def body(w_ref, g_ref, m_ref, v_ref, d_ref, mo_ref, vo_ref):
        gv = g_ref[...]
        mn = ADAM_B1 * m_ref[...] + (1.0 - ADAM_B1) * gv
        vn = ADAM_B2 * v_ref[...] + (1.0 - ADAM_B2) * (gv * gv)
        d_ref[...] = -ADAM_LR * ((mn / bc1) / (jnp.sqrt(vn / bc2) + ADAM_EPS) + ADAM_WD * w_ref[...])
        mo_ref[...] = mn
        vo_ref[...] = vn

    blk = pl.BlockSpec((tr, n), lambda i: (i, 0))
    shp = jax.ShapeDtypeStruct((r, n), F32)
    return pl.pallas_call(
        body, name=name, grid=(r // tr,), in_specs=[blk] * 4, out_specs=[blk] * 3, out_shape=[shp] * 3,
        compiler_params=_cparams(("arbitrary",)),
    )(w, g, m, v)


ROW = 1024
BIG = ("fox_w_in", "fox_w_out", "sgu_w_in", "sgu_w_out", "ffn_w_up", "ffn_w_down")
SMALL_SHARDED = ("sgu_b_in", "sgu_v_gain", "sgu_v_bias", "ffn_conv_w")
SMALL_REPL = ("fox_b_f", "fox_q_gain", "fox_k_gain", "sgu_w_s", "sgu_b_s", "ffn_conv_b", "ada_b",
              "norm1_g", "norm2_g", "final_g")
WEIGHTS = ("fox_w_in", "fox_b_f", "fox_q_gain", "fox_k_gain", "fox_w_out", "sgu_w_in", "sgu_b_in", "sgu_v_gain",
           "sgu_v_bias", "sgu_w_s", "sgu_b_s", "sgu_w_out", "ffn_w_up", "ffn_conv_w", "ffn_conv_b", "ffn_w_down",
           "ada_w", "ada_b", "norm1_g", "norm2_g", "final_g")


def _rows_of(a, mult=1):
    flat = a.reshape(-1)
    rows = -(-flat.shape[0] // ROW)
    rows = -(-rows // mult) * mult
    return jnp.pad(flat, (0, rows * ROW - flat.shape[0])).reshape(rows, ROW)


def _pack(parts, mult, total=None):
    p = jnp.concatenate([_rows_of(a, mult) for a in parts], axis=0)
    if total is not None:
        p = jnp.pad(p, ((0, total - p.shape[0]), (0, 0)))
    return p


def _unpack(pack, shapes, mult):
    out, r0 = [], 0
    for shp in shapes:
        size = int(np.prod(shp))
        rows = -(-(-(-size // ROW)) // mult) * mult
        out.append(pack[r0:r0 + rows].reshape(-1)[:size].reshape(shp))
        r0 += rows
    return out


def _big_shards(t):
    return [t["fox_w_in"][0], t["fox_w_out"][0], t["sgu_w_in"][0], t["sgu_w_out"][0],
            t["ffn_w_up"][0], t["ffn_w_up"][1], t["ffn_w_down"][0], t["ffn_w_down"][1]]


def _row_tile(rows):
    return next(t for t in (512, 352, 256, 128, 64) if rows % t == 0)


def kernel(x, c, fox_w_in, fox_b_f, fox_q_gain, fox_k_gain, fox_w_out, sgu_w_in, sgu_b_in, sgu_v_gain, sgu_v_bias, sgu_w_s, sgu_b_s, sgu_w_out, ffn_w_up, ffn_conv_w, ffn_conv_b, ffn_w_down, ada_w, ada_b, norm1_g, norm2_g, final_g, loss_target, m_fox_w_in, m_fox_b_f, m_fox_q_gain, m_fox_k_gain, m_fox_w_out, m_sgu_w_in, m_sgu_b_in, m_sgu_v_gain, m_sgu_v_bias, m_sgu_w_s, m_sgu_b_s, m_sgu_w_out, m_ffn_w_up, m_ffn_conv_w, m_ffn_conv_b, m_ffn_w_down, m_ada_w, m_ada_b, m_norm1_g, m_norm2_g, m_final_g, v_fox_w_in, v_fox_b_f, v_fox_q_gain, v_fox_k_gain, v_fox_w_out, v_sgu_w_in, v_sgu_b_in, v_sgu_v_gain, v_sgu_v_bias, v_sgu_w_s, v_sgu_b_s, v_sgu_w_out, v_ffn_w_up, v_ffn_conv_w, v_ffn_conv_b, v_ffn_w_down, v_ada_w, v_ada_b, v_norm1_g, v_norm2_g, v_final_g):
    w = dict(fox_w_in=fox_w_in, fox_b_f=fox_b_f, fox_q_gain=fox_q_gain, fox_k_gain=fox_k_gain, fox_w_out=fox_w_out,
             sgu_w_in=sgu_w_in, sgu_b_in=sgu_b_in, sgu_v_gain=sgu_v_gain, sgu_v_bias=sgu_v_bias, sgu_w_s=sgu_w_s,
             sgu_b_s=sgu_b_s, sgu_w_out=sgu_w_out, ffn_w_up=ffn_w_up, ffn_conv_w=ffn_conv_w, ffn_conv_b=ffn_conv_b,
             ffn_w_down=ffn_w_down, ada_w=ada_w, ada_b=ada_b, norm1_g=norm1_g, norm2_g=norm2_g, final_g=final_g)
    mom = dict(fox_w_in=m_fox_w_in, fox_b_f=m_fox_b_f, fox_q_gain=m_fox_q_gain, fox_k_gain=m_fox_k_gain,
               fox_w_out=m_fox_w_out, sgu_w_in=m_sgu_w_in, sgu_b_in=m_sgu_b_in, sgu_v_gain=m_sgu_v_gain,
               sgu_v_bias=m_sgu_v_bias, sgu_w_s=m_sgu_w_s, sgu_b_s=m_sgu_b_s, sgu_w_out=m_sgu_w_out,
               ffn_w_up=m_ffn_w_up, ffn_conv_w=m_ffn_conv_w, ffn_conv_b=m_ffn_conv_b, ffn_w_down=m_ffn_w_down,
               ada_w=m_ada_w, ada_b=m_ada_b, norm1_g=m_norm1_g, norm2_g=m_norm2_g, final_g=m_final_g)
    var = dict(fox_w_in=v_fox_w_in, fox_b_f=v_fox_b_f, fox_q_gain=v_fox_q_gain, fox_k_gain=v_fox_k_gain,
               fox_w_out=v_fox_w_out, sgu_w_in=v_sgu_w_in, sgu_b_in=v_sgu_b_in, sgu_v_gain=v_sgu_v_gain,
               sgu_v_bias=v_sgu_v_bias, sgu_w_s=v_sgu_w_s, sgu_b_s=v_sgu_b_s, sgu_w_out=v_sgu_w_out,
               ffn_w_up=v_ffn_w_up, ffn_conv_w=v_ffn_conv_w, ffn_conv_b=v_ffn_conv_b, ffn_w_down=v_ffn_w_down,
               ada_w=v_ada_w, ada_b=v_ada_b, norm1_g=v_norm1_g, norm2_g=v_norm2_g, final_g=v_final_g)

    ax, ay, ac = _mesh_pos()
    chip = 2 * ax + ay
    dev = 2 * chip + ac

    small_shard_shapes = tuple(w[n].shape for n in SMALL_SHARDED)
    blk = _pack([c] + [w[n] for n in SMALL_SHARDED], 1, 16)
    gat = _allgather8(blk, "gather_small").reshape(N_DEV, 16, ROW)
    c_all = gat[:, 0, :]
    per_chip = [_unpack(gat[2 * j, 1:], small_shard_shapes, 1) for j in range(N_CHIP)]
    full_small = {n: jnp.concatenate([per_chip[j][i] for j in range(N_CHIP)], axis=-1)
                  for i, n in enumerate(SMALL_SHARDED)}

    mine = [a.astype(BF16) for a in _big_shards(w)]
    with_own = lambda gat, own: [lax.dynamic_update_slice(g_, m_[None], (chip, 0, 0)) for g_, m_ in zip(gat, own)]
    fwi, = with_own(_gather_shards(mine[:1], "gather_fox_w_in"), mine[:1])
    fwi_full = _join_columns(fwi, FOX_NP, "join_fox_w_in")
    wts = dict(fox_w_in=fwi_full)

    def make_wts(gathered):
        fwo, swi, swo, up0, up1, dn0, dn1 = with_own(gathered, mine[1:])
        return dict(fox_w_out=fwo.reshape(D, D), sgu_w_in=swi, sgu_w_out=swo.reshape(SGW, D),
                    ffn_w_up=[up0, up1], ffn_w_down=[dn0.reshape(DFF, D), dn1.reshape(DFF, D)])

    c_arr = jnp.reshape(ac, (1,)).astype(jnp.int32)
    me_arr = jnp.reshape(chip, (1,)).astype(jnp.int32)

    def rs_chip_sums(glist, sibs, tag):
        return [_rs_chip_sum(g_, s_, c_arr, _row_tile(s_.shape[1]), "rs_chip_sum%s%d" % (tag, a))
                for a, (g_, s_) in enumerate(zip(glist, sibs))]

    def rs_lists(gl):
        g_fwo, g_swi, g_swo, g_wu0, g_wu1, g_wd0, g_wd1 = gl
        return [g_fwo.reshape(N_CHIP, 256, D), g_swi, g_swo.reshape(N_CHIP, 512, D), g_wu0, g_wu1,
                g_wd0.reshape(N_CHIP, 704, D), g_wd1.reshape(N_CHIP, 704, D)]

    def rs_prepare_fox(g_fwi):
        glist = [_split_columns(g_fwi, N_CHIP, FOX_N // N_CHIP, "split_fox_w_in")]
        return rs_chip_sums(glist, _rs_to_sibling(glist, "rs_sibling_fox"), "_fox")

    comm = dict(shards=mine[1:], make_wts=make_wts, rs_lists=rs_lists, rs_chip_sums=rs_chip_sums,
                rs_prepare_fox=rs_prepare_fox)

    da = ada_w.shape[2]
    ada_b_cols = lax.dynamic_slice_in_dim(ada_b, chip * da, da, axis=1)[:, None, :]
    mod_cols, c_act = _ada_mod(c_all, ada_w, ada_b_cols)
    mod_all = _allgather8(mod_cols.reshape(-1, ROW), "gather_mod").reshape(N_DEV, 2, N_DEV, da)
    mod_mine = lax.dynamic_index_in_dim(mod_all[0::2], dev, axis=2, keepdims=False)
    mod = jnp.swapaxes(mod_mine, 0, 1).reshape(2, N_CHIP * da)

    small = dict(norm1_g=norm1_g, norm2_g=norm2_g, final_g=final_g[None], fox_q_gain=fox_q_gain,
                 fox_k_gain=fox_k_gain, fox_b_f=fox_b_f, sgu_b_in=full_small["sgu_b_in"],
                 sgu_v_gain=full_small["sgu_v_gain"], sgu_v_bias=full_small["sgu_v_bias"], sgu_w_s=sgu_w_s[0],
                 sgu_b_s=sgu_b_s[0], ffn_conv_w=full_small["ffn_conv_w"], ffn_conv_b=ffn_conv_b)
    loss_dev, dx, g, dmod, (css, rcvs) = _local_step(x[0], loss_target[0], mod, wts, small, comm)

    g["ada_b"] = dmod
    g["loss"] = loss_dev
    small_names = ("ada_b",) + SMALL_SHARDED + tuple(n for n in SMALL_REPL if n != "ada_b") + ("loss",)
    gs = _pack([g[n] for n in small_names], 1)
    rows_s = -(-gs.shape[0] // 8) * 8
    gs = jnp.pad(gs, ((0, rows_s - gs.shape[0]), (0, 0)))
    gs_all = _allgather8(gs, "gather_small_grads").reshape(N_DEV, rows_s, ROW)
    gsum = _sum8(gs_all, "sum_small_grads")
    full_shapes = {n: w[n].shape for n in SMALL_REPL}
    full_shapes.update({n: w[n].shape[:-1] + (w[n].shape[-1] * N_CHIP,) for n in SMALL_SHARDED})
    full_shapes["loss"] = ()
    gfull = dict(zip(small_names, _unpack(gsum, [full_shapes[n] for n in small_names], 1)))
    grads = {n: gfull[n] for n in SMALL_REPL}
    for n in SMALL_SHARDED:
        width = w[n].shape[-1]
        grads[n] = lax.dynamic_slice_in_dim(gfull[n], chip * width, width, axis=gfull[n].ndim - 1)
    dmod_all = gs_all[:, :12, :].reshape(N_DEV, 2, N_CHIP * da)
    dmod_cols = jnp.swapaxes(lax.dynamic_slice_in_dim(dmod_all, chip * da, da, axis=2), 0, 1)
    grads["ada_w"] = _ada_w_grad(c_act.T, dmod_cols)

    halves =[_rs_final_sum(cs_, r_, me_arr, _row_tile(cs_.shape[1]), "rs_final_sum%d" % a)
              for a, (cs_, r_) in enumerate(zip(css, rcvs))]
    others = _rs_swap_halves(halves, "rs_swap")
    red = [jnp.concatenate([jnp.where(ac == 0, h_, o_), jnp.where(ac == 0, o_, h_)]) for h_, o_ in zip(halves, others)]
    grads.update(fox_w_in=red[0], fox_w_out=red[1], sgu_w_in=red[2], sgu_w_out=red[3],
                 ffn_w_up=jnp.stack([red[4], red[5]]), ffn_w_down=jnp.stack([red[6], red[7]]))

    delta, new_m, new_v = {}, {}, {}
    for n in BIG + ("ada_w",):
        shp = w[n].shape
        two_d = lambda a: a.reshape(-1, shp[-1])
        d_, m_, v_ = _adamw(two_d(w[n]), two_d(grads[n]), two_d(mom[n]), two_d(var[n]), "adamw_" + n)
        delta[n], new_m[n], new_v[n] = d_.reshape(shp), m_.reshape(shp), v_.reshape(shp)
    rest = SMALL_SHARDED + SMALL_REPL
    packs = [_pack([t[n] for n in rest], 1) for t in (w, grads, mom, var)]
    rows_r = -(-packs[0].shape[0] // 8) * 8
    packs = [jnp.pad(p, ((0, rows_r - p.shape[0]), (0, 0))) for p in packs]
    outs = _adamw(*packs, "adamw_small")
    for t, o in zip((delta, new_m, new_v), outs):
        t.update(zip(rest, _unpack(o, [w[n].shape for n in rest], 1)))

    loss = gfull["loss"]
    return (loss, dx[None], *[grads[n].reshape(w[n].shape) for n in WEIGHTS], *[delta[n] for n in WEIGHTS],
            *[new_m[n] for n in WEIGHTS], *[new_v[n] for n in WEIGHTS])
```
